```python
import jax, jax.numpy as jnp
from jax import lax
import numpy as np

D_MODEL = 2048
BATCH = 1
SEQ = 8192
DEPTH = 1

CHUNK = 64
N_MEM = 256
EPS = 1e-6

DSA_HEADS = 8
DSA_HEAD_DIM = 128
DSA_LATENT = 256
IDX_HEADS = 8
IDX_DIM = 64
TOPK_MAX = 256
Q_BLOCK = 128

ML_HEADS = 4
ML_QK_DIM = 128
ML_V_DIM = 256
CONV_W = 4

D_MIX = DSA_HEADS * DSA_HEAD_DIM + ML_HEADS * ML_V_DIM

IN_SIZES = (
    DSA_HEADS * DSA_HEAD_DIM,
    DSA_LATENT,
    IDX_HEADS * IDX_DIM,
    IDX_DIM,
    IDX_HEADS,
    ML_HEADS * ML_QK_DIM,
    ML_HEADS * ML_QK_DIM,
    ML_HEADS * ML_V_DIM,
    ML_HEADS,
    ML_HEADS,
    ML_HEADS * ML_V_DIM,
)
D_IN = sum(IN_SIZES)

X_HEADS = 4
X_HEAD_DIM = D_MODEL // X_HEADS

N_GROUPS = 4
EXP_PER_GROUP = 8
N_EXPERTS = N_GROUPS * EXP_PER_GROUP
TOPK_IN_GROUP = 2
D_EXPERT = 512
MOE_BLOCK = 128

kernel_name = "hybrid_dsa_mlstm_hiermoe_layer"


def rmsnorm(x, g):
    xf = x.astype(jnp.float32)
    y = xf * lax.rsqrt(jnp.mean(xf * xf, axis=-1, keepdims=True) + EPS)
    return (y * g.astype(jnp.float32)).astype(x.dtype)


def dsa_mixer(q, c_kv, q_idx, k_idx, w_idx, w_uk, w_uv, kv_norm_g, k_idx_norm_g):
    B, T = q.shape[:2]
    topk = min(TOPK_MAX, T // 4)
    c_kv = rmsnorm(c_kv, kv_norm_g)
    k_idx = rmsnorm(k_idx, k_idx_norm_g).astype(jnp.float32)
    q_abs = jnp.einsum('bthd,chd->bthc', q, w_uk)
    key_chunk = jnp.arange(T) // CHUNK
    nb = T // Q_BLOCK
    bidx = jnp.arange(B)[:, None, None]

    def to_blocks(a):
        return jnp.moveaxis(a.reshape((B, nb, Q_BLOCK) + a.shape[2:]), 1, 0)

    def block_fn(args):
        qi, wi, qa, start = args
        q_chunk = (start + jnp.arange(Q_BLOCK)) // CHUNK
        admissible = key_chunk[None, :] <= q_chunk[:, None]
        dots = jnp.einsum('bqhd,bsd->bqhs', qi.astype(jnp.float32), k_idx) * IDX_DIM ** -0.5
        score = jnp.einsum('bqh,bqhs->bqs', wi.astype(jnp.float32) * IDX_HEADS ** -0.5,
                           jax.nn.relu(dots))
        score = jnp.where(admissible[None], score, -jnp.inf)
        _, sel = lax.top_k(score, topk)
        valid = key_chunk[sel] <= q_chunk[None, :, None]
        c_sel = c_kv[bidx, sel]
        logits = jnp.einsum('bqhc,bqkc->bqhk', qa.astype(jnp.float32),
                            c_sel.astype(jnp.float32)) * DSA_HEAD_DIM ** -0.5
        logits = jnp.where(valid[:, :, None, :], logits, -jnp.inf)
        p = jax.nn.softmax(logits, axis=-1)
        return jnp.einsum('bqhk,bqkc->bqhc', p.astype(c_sel.dtype), c_sel)

    o_lat = lax.map(block_fn, (to_blocks(q_idx), to_blocks(w_idx), to_blocks(q_abs),
                               jnp.arange(nb) * Q_BLOCK))
    o_lat = jnp.moveaxis(o_lat, 0, 1).reshape(B, T, DSA_HEADS, DSA_LATENT)
    o = jnp.einsum('bthc,chd->bthd', o_lat, w_uv)
    return o.reshape(B, T, DSA_HEADS * DSA_HEAD_DIM)


def causal_conv(x, w, b):
    T = x.shape[1]
    xp = jnp.pad(x, ((0, 0), (CONV_W - 1, 0), (0, 0)))
    y = b
    for j in range(CONV_W):
        y = y + xp[:, j:j + T] * w[j]
    return y


def mlstm_mixer(q, k, v, i_pre, f_pre, o_pre, head_norm_g):
    B, T, H, DK = q.shape
    DV = v.shape[-1]
    nc = T // CHUNK
    f32 = jnp.float32

    def chunks(a):
        a = a.reshape((B, nc, CHUNK) + a.shape[2:])
        return jnp.moveaxis(jnp.moveaxis(a, 3, 2), 1, 0)

    qc = chunks(q.astype(f32) * DK ** -0.5)
    kc = chunks(k.astype(f32))
    vc = chunks(v.astype(f32))
    li = chunks(i_pre.astype(f32))
    lf = chunks(jax.nn.log_sigmoid(f_pre.astype(f32)))
    causal = jnp.tril(jnp.ones((CHUNK, CHUNK), bool))

    def step(carry, inp):
        C, n, m = carry
        qh, kh, vh, li_c, lf_c = inp
        b = jnp.cumsum(lf_c, axis=-1)
        g = b[..., -1]
        dmat = b[..., :, None] - b[..., None, :] + li_c[..., None, :]
        dmat = jnp.where(causal, dmat, -jnp.inf)
        inter = b + m[..., None]
        m_t = jnp.maximum(inter, jnp.max(dmat, axis=-1))
        w_intra = jnp.exp(dmat - m_t[..., None])
        a_inter = jnp.exp(inter - m_t)
        qk = jnp.einsum('bhtd,bhsd->bhts', qh, kh) * w_intra
        num = (a_inter[..., None] * jnp.einsum('bhtd,bhdv->bhtv', qh, C)
               + jnp.einsum('bhts,bhsv->bhtv', qk, vh))
        den = a_inter * jnp.einsum('bhtd,bhd->bht', qh, n) + qk.sum(-1)
        h = num / jnp.maximum(jnp.abs(den), jnp.exp(-m_t))[..., None]
        decay_s = g[..., None] - b + li_c
        m_new = jnp.maximum(g + m, decay_s.max(-1))
        ws = jnp.exp(decay_s - m_new[..., None])
        a_state = jnp.exp(g + m - m_new)
        wk = ws[..., None] * kh
        C_new = a_state[..., None, None] * C + jnp.einsum('bhsd,bhsv->bhdv', wk, vh)
        n_new = a_state[..., None] * n + wk.sum(-2)
        return (C_new, n_new, m_new), h

    init = (jnp.zeros((B, H, DK, DV), f32), jnp.zeros((B, H, DK), f32), jnp.zeros((B, H), f32))
    _, hs = lax.scan(step, init, (qc, kc, vc, li, lf))
    hs = jnp.moveaxis(jnp.moveaxis(hs, 0, 1), 2, 3).reshape(B, T, H, DV).astype(v.dtype)
    hs = rmsnorm(hs, head_norm_g.reshape(H, DV)).reshape(B, T, H * DV)
    return hs * jax.nn.sigmoid(o_pre)


def memory_cross_attn(h, mem_n, w_cq, w_ckv, w_co):
    B, T, D = h.shape
    M = mem_n.shape[1]
    q = (h @ w_cq).reshape(B, T, X_HEADS, X_HEAD_DIM)
    kv = (mem_n @ w_ckv).reshape(B, M, 2, X_HEADS, X_HEAD_DIM)
    k, v = kv[:, :, 0], kv[:, :, 1]
    logits = jnp.einsum('bthd,bmhd->bhtm', q.astype(jnp.float32),
                        k.astype(jnp.float32)) * X_HEAD_DIM ** -0.5
    p = jax.nn.softmax(logits, axis=-1).astype(v.dtype)
    o = jnp.einsum('bhtm,bmhd->bthd', p, v).reshape(B, T, D)
    return o @ w_co


def hier_moe(h, w_group, b_group, w_router, b_router, w_gate, w_up, w_down):
    B, T, D = h.shape
    N = B * T
    xt = h.reshape(N, D)
    g_logits = (xt @ w_group).astype(jnp.float32) + b_group
    g_prob = jax.nn.softmax(g_logits, axis=-1)
    g_sel = jnp.argmax(g_logits, axis=-1)
    p_g = jnp.take_along_axis(g_prob, g_sel[:, None], axis=-1)
    e_logits = ((xt @ w_router).astype(jnp.float32) + b_router).reshape(N, N_GROUPS, EXP_PER_GROUP)
    e_logits = jnp.take_along_axis(e_logits, g_sel[:, None, None], axis=1)[:, 0]
    top_p, top_local = lax.top_k(jax.nn.softmax(e_logits, axis=-1), TOPK_IN_GROUP)
    gates = p_g * top_p / top_p.sum(-1, keepdims=True)
    expert_id = g_sel[:, None] * EXP_PER_GROUP + top_local

    A = N * TOPK_IN_GROUP
    flat_e = expert_id.reshape(A).astype(jnp.int32)
    flat_tok = jnp.repeat(jnp.arange(N, dtype=jnp.int32), TOPK_IN_GROUP)
    flat_w = gates.reshape(A)
    order = jnp.argsort(flat_e)
    se, stok, sw = flat_e[order], flat_tok[order], flat_w[order]
    counts = jnp.zeros((N_EXPERTS,), jnp.int32).at[flat_e].add(1)
    starts = jnp.cumsum(counts) - counts
    padded = (counts + MOE_BLOCK - 1) // MOE_BLOCK * MOE_BLOCK
    pad_ends = jnp.cumsum(padded)
    pad_starts = pad_ends - padded
    row = pad_starts[se] + (jnp.arange(A, dtype=jnp.int32) - starts[se])
    nblk = -(-A // MOE_BLOCK) + N_EXPERTS
    n_rows = nblk * MOE_BLOCK
    row_tok = jnp.full((n_rows,), N, jnp.int32).at[row].set(stok)
    row_w = jnp.zeros((n_rows,), jnp.float32).at[row].set(sw)
    blk_expert = jnp.minimum(
        jnp.searchsorted(pad_ends, jnp.arange(nblk, dtype=jnp.int32) * MOE_BLOCK, side='right'),
        N_EXPERTS - 1)
    x_pad = jnp.concatenate([xt, jnp.zeros((1, D), xt.dtype)], axis=0)

    def expert_block(args):
        tok, e = args
        xb = x_pad[tok]
        a = jax.nn.silu(xb @ w_gate[e]) * (xb @ w_up[e])
        return a @ w_down[e]

    y_rows = lax.map(expert_block, (row_tok.reshape(nblk, MOE_BLOCK), blk_expert))
    y_rows = y_rows.reshape(n_rows, D) * row_w[:, None].astype(y_rows.dtype)
    y = jax.ops.segment_sum(y_rows, row_tok, num_segments=N + 1)[:N]
    return y.reshape(B, T, D)


def setup_inputs(seed: int = 0) -> dict:
    key = jax.random.key(seed)
    ks = jax.random.split(key, 32)
    f32 = jnp.float32
    L = DEPTH

    def nrm(k, shape, scale):
        return jax.random.normal(k, shape, f32) * scale

    def gain(k, shape):
        return 1.0 + 0.02 * jax.random.normal(k, shape, f32)

    return {
        "x": nrm(ks[0], (BATCH, SEQ, D_MODEL), 1.0),
        "mem": nrm(ks[1], (BATCH, N_MEM, D_MODEL), 1.0),
        "norm_mix_g": gain(ks[2], (L, D_MODEL)),
        "w_in": nrm(ks[3], (L, D_MODEL, D_IN), D_MODEL ** -0.5),
        "kv_norm_g": gain(ks[4], (L, DSA_LATENT)),
        "k_idx_norm_g": gain(ks[5], (L, IDX_DIM)),
        "w_uk": nrm(ks[6], (L, DSA_LATENT, DSA_HEADS, DSA_HEAD_DIM), DSA_LATENT ** -0.5),
        "w_uv": nrm(ks[7], (L, DSA_LATENT, DSA_HEADS, DSA_HEAD_DIM), DSA_LATENT ** -0.5),
        "conv_w": nrm(ks[8], (L, CONV_W, 2 * ML_HEADS * ML_QK_DIM), CONV_W ** -0.5),
        "conv_b": nrm(ks[9], (L, 2 * ML_HEADS * ML_QK_DIM), 0.01),
        "gate_b": jnp.concatenate([nrm(ks[10], (L, ML_HEADS), 0.1),
                                   3.0 + nrm(ks[11], (L, ML_HEADS), 0.1)], axis=-1),
        "ml_norm_g": gain(ks[12], (L, ML_HEADS * ML_V_DIM)),
        "w_out": nrm(ks[13], (L, D_MIX, D_MODEL), D_MIX ** -0.5),
        "norm_x_g": gain(ks[14], (L, D_MODEL)),
        "mem_norm_g": gain(ks[15], (L, D_MODEL)),
        "w_cq": nrm(ks[16], (L, D_MODEL, D_MODEL), D_MODEL ** -0.5),
        "w_ckv": nrm(ks[17], (L, D_MODEL, 2 * D_MODEL), D_MODEL ** -0.5),
        "w_co": nrm(ks[18], (L, D_MODEL, D_MODEL), D_MODEL ** -0.5),
        "norm_ffn_g": gain(ks[19], (L, D_MODEL)),
        "w_group": nrm(ks[20], (L, D_MODEL, N_GROUPS), D_MODEL ** -0.5),
        "b_group": nrm(ks[21], (L, N_GROUPS), 0.01),
        "w_router": nrm(ks[22], (L, D_MODEL, N_EXPERTS), D_MODEL ** -0.5),
        "b_router": nrm(ks[23], (L, N_EXPERTS), 0.01),
        "w_gate": nrm(ks[24], (L, N_EXPERTS, D_MODEL, D_EXPERT), D_MODEL ** -0.5),
        "w_up": nrm(ks[25], (L, N_EXPERTS, D_MODEL, D_EXPERT), D_MODEL ** -0.5),
        "w_down": nrm(ks[26], (L, N_EXPERTS, D_EXPERT, D_MODEL), D_EXPERT ** -0.5),
        "final_norm_g": gain(ks[27], (D_MODEL,)),
    }


def reference(x, mem, norm_mix_g, w_in, kv_norm_g, k_idx_norm_g, w_uk, w_uv, conv_w, conv_b,
              gate_b, ml_norm_g, w_out, norm_x_g, mem_norm_g, w_cq, w_ckv, w_co, norm_ffn_g,
              w_group, b_group, w_router, b_router, w_gate, w_up, w_down, final_norm_g):
    B, T, _ = x.shape
    split_points = np.cumsum(IN_SIZES)[:-1].tolist()
    for layer in range(DEPTH):
        h = rmsnorm(x, norm_mix_g[layer])
        proj = h @ w_in[layer]
        dq, ckv, qi, ki, wi, mq, mk, mv, mi, mf, mo = jnp.split(proj, split_points, axis=-1)
        dsa_out = dsa_mixer(dq.reshape(B, T, DSA_HEADS, DSA_HEAD_DIM), ckv,
                            qi.reshape(B, T, IDX_HEADS, IDX_DIM), ki, wi,
                            w_uk[layer], w_uv[layer], kv_norm_g[layer], k_idx_norm_g[layer])
        qk = jax.nn.silu(causal_conv(jnp.concatenate([mq, mk], axis=-1), conv_w[layer], conv_b[layer]))
        mq, mk = jnp.split(qk, 2, axis=-1)
        gb = gate_b[layer]
        ml_out = mlstm_mixer(mq.reshape(B, T, ML_HEADS, ML_QK_DIM),
                             mk.reshape(B, T, ML_HEADS, ML_QK_DIM),
                             mv.reshape(B, T, ML_HEADS, ML_V_DIM),
                             mi + gb[:ML_HEADS], mf + gb[ML_HEADS:], mo, ml_norm_g[layer])
        x = x + jnp.concatenate([dsa_out, ml_out], axis=-1) @ w_out[layer]
        mem_n = rmsnorm(mem, mem_norm_g[layer])
        x = x + memory_cross_attn(rmsnorm(x, norm_x_g[layer]), mem_n,
                                  w_cq[layer], w_ckv[layer], w_co[layer])
        x = x + hier_moe(rmsnorm(x, norm_ffn_g[layer]), w_group[layer], b_group[layer],
                         w_router[layer], b_router[layer], w_gate[layer], w_up[layer], w_down[layer])
    return rmsnorm(x, final_norm_g)
```

```python
import functools

import jax
import jax.numpy as jnp
import numpy as np
from jax import lax
from jax.experimental import pallas as pl
from jax.experimental.pallas import tpu as pltpu

F32 = jnp.float32
BF16 = jnp.bfloat16
I32 = jnp.int32

EPS = 1e-6
CHUNK = 64
D_MODEL = 2048

DSA_HEADS = 8
DSA_HEAD_DIM = 128
DSA_LATENT = 256
IDX_HEADS = 8
IDX_DIM = 64
TOPK_MAX = 256

ML_HEADS = 4
ML_QK_DIM = 128
ML_V_DIM = 256
CONV_W = 4

X_HEADS = 4
X_HEAD_DIM = D_MODEL // X_HEADS

N_GROUPS = 4
EXP_PER_GROUP = 8
N_EXPERTS = N_GROUPS * EXP_PER_GROUP
TOPK_IN_GROUP = 2
D_EXPERT = 512

_O_DQ = 0
_O_CKV = _O_DQ + DSA_HEADS * DSA_HEAD_DIM
_O_QI = _O_CKV + DSA_LATENT
_O_KI = _O_QI + IDX_HEADS * IDX_DIM
_O_WI = _O_KI + IDX_DIM
_O_MQ = _O_WI + IDX_HEADS
_O_MK = _O_MQ + ML_HEADS * ML_QK_DIM
_O_MV = _O_MK + ML_HEADS * ML_QK_DIM
_O_MI = _O_MV + ML_HEADS * ML_V_DIM
_O_MF = _O_MI + ML_HEADS
_O_MO = _O_MF + ML_HEADS
_O_END = _O_MO + ML_HEADS * ML_V_DIM

_G_DQ = (0, 1024)
_G_CKV = (1024, 1280)
_G_QI = (1280, 1792)
_G_SMALL = (1792, 1920)
_G_MQK = (1920, 2944)
_G_MV = (2944, 3968)
_G_MO = (3968, 4992)
_W_COLS = 4992
_S_WI = IDX_DIM
_S_MI = _S_WI + IDX_HEADS
_S_MF = _S_MI + ML_HEADS
_SMALL = 128

_VMEM_LIMIT = 56 * 1024 * 1024
_INT_MIN = -(2 ** 31)
_CHUNK_SHIFT = CHUNK.bit_length() - 1
_NEG = -1e30


def _rms(v, g):
    return v * lax.rsqrt(jnp.mean(v * v, axis=-1, keepdims=True) + EPS) * g


def _dot(a, b):
    return jnp.dot(a, b, preferred_element_type=F32)


def _dot_nt(a, b):
    return lax.dot_general(a, b, (((1,), (1,)), ((), ())), preferred_element_type=F32)


def _resident(shape):
    nd = len(shape)
    return pl.BlockSpec(shape, lambda *_: (0,) * nd, pipeline_mode=pl.Buffered(1))


def _params(n_axes=1):
    return pltpu.CompilerParams(dimension_semantics=("arbitrary",) * n_axes,
                                vmem_limit_bytes=_VMEM_LIMIT)


def _inproj_body(x_ref, g_ref, w_ref, wuk_ref, kvg_ref, kig_ref,
                 qabs_ref, ckv_ref, qi_ref, kidx_ref, small_ref, mqk_ref, mv_ref, mo_ref):
    h = _rms(x_ref[...], g_ref[...]).astype(BF16)

    def proj(grp):
        return _dot(h, w_ref[:, grp[0]:grp[1]])

    dq = proj(_G_DQ)
    for hd in range(DSA_HEADS):
        qh = dq[:, hd * DSA_HEAD_DIM:(hd + 1) * DSA_HEAD_DIM].astype(BF16)
        qa = _dot(qh, wuk_ref[hd]) * (DSA_HEAD_DIM ** -0.5)
        qabs_ref[:, hd * DSA_LATENT:(hd + 1) * DSA_LATENT] = qa.astype(BF16)
    ckv_ref[...] = _rms(proj(_G_CKV), kvg_ref[...]).astype(BF16)
    qi_ref[...] = (proj(_G_QI) * (IDX_DIM ** -0.5)).astype(BF16)
    small = proj(_G_SMALL)
    small_ref[...] = small
    kidx_ref[...] = _rms(small[:, :IDX_DIM], kig_ref[...]).astype(BF16)
    mqk_ref[...] = proj(_G_MQK)
    mv_ref[...] = proj(_G_MV).astype(BF16)
    mo_ref[...] = proj(_G_MO)


def _inproj(x, g, w, wuk, kvg, kig, tm):
    T = x.shape[0]
    row = lambda n: pl.BlockSpec((tm, n), lambda i: (i, 0))
    outs = [(8 * DSA_LATENT, BF16), (DSA_LATENT, BF16), (IDX_HEADS * IDX_DIM, BF16), (IDX_DIM, BF16),
            (_SMALL, F32), (2 * ML_HEADS * ML_QK_DIM, F32), (ML_HEADS * ML_V_DIM, BF16),
            (ML_HEADS * ML_V_DIM, F32)]
    return pl.pallas_call(
        _inproj_body,
        grid=(T // tm,),
        in_specs=[row(D_MODEL), _resident(g.shape), _resident(w.shape), _resident(wuk.shape),
                  _resident(kvg.shape), _resident(kig.shape)],
        out_specs=[row(n) for n, _ in outs],
        out_shape=[jax.ShapeDtypeStruct((T, n), dt) for n, dt in outs],
        compiler_params=_params(),
        name="inproj",
    )(x, g, w, wuk, kvg, kig)


def _dsa_body(qi_ref, small_ref, qabs_ref, kidx_ref, ckv_ref, wuv_ref, out_ref,
              key_ref, m_ref, l_ref, acc_ref, *, tq, tk, topk, nbits_idx):
    i = pl.program_id(0)
    n_kb = ((i + 1) * tq + tk - 1) // tk
    w_idx = small_ref[:, _S_WI:_S_WI + IDX_HEADS] * (IDX_HEADS ** -0.5)
    q_chunk = (i * tq + lax.broadcasted_iota(I32, (tq, 1), 0)) >> _CHUNK_SHIFT

    def score_block(j, carry):
        kx = kidx_ref[pl.ds(pl.multiple_of(j * tk, tk), tk), :]
        s = jnp.zeros((tq, tk), F32)
        for hd in range(IDX_HEADS):
            d = _dot_nt(qi_ref[:, hd * IDX_DIM:(hd + 1) * IDX_DIM], kx)
            s = s + w_idx[:, hd:hd + 1] * jnp.maximum(d, 0.0)
        k_chunk = (j * tk + lax.broadcasted_iota(I32, (1, tk), 1)) >> _CHUNK_SHIFT
        bits = lax.bitcast_convert_type(s, I32)
        key = bits ^ ((bits >> 31) & 0x7FFFFFFF)
        key_ref[j] = jnp.where(k_chunk <= q_chunk, key, _INT_MIN)
        return carry

    lax.fori_loop(0, n_kb, score_block, 0)

    def count(pred):
        def body(j, acc):
            kb = key_ref[j]
            idx = j * tk + lax.broadcasted_iota(I32, (1, tk), 1)
            hit =pred(kb, idx).astype(I32)
            for c in range(tk // 128):
                acc = acc + hit[:, c * 128:(c + 1) * 128]
            return acc
        acc = lax.fori_loop(0, n_kb, body, jnp.zeros((tq, 128), I32))
        return jnp.sum(acc, axis=-1, keepdims=True)

    def thr_bit(b, t):
        cand = t + lax.shift_left(jnp.int32(1), 31 - b)
        return jnp.where(count(lambda kb, idx: kb >= cand) >= topk, cand, t)

    t = lax.fori_loop(0, 32, thr_bit, jnp.full((tq, 1), _INT_MIN, I32))
    t = jnp.maximum(t, _INT_MIN + 1)
    n_ge = count(lambda kb, idx: kb >= t)
    n_gt = count(lambda kb, idx: kb > t)
    all_idx = jnp.int32(2 ** nbits_idx - 1)
    n_tie_take = jnp.where(n_ge > topk, topk - n_gt, all_idx)

    def tie_cutoff():
        def idx_bit(b, c):
            cand = c + lax.shift_left(jnp.int32(1), nbits_idx - 1 - b)
            f = count(lambda kb, idx: (kb == t) & (idx < cand))
            return jnp.where(f <= n_tie_take, cand, c)
        return lax.fori_loop(0, nbits_idx, idx_bit, jnp.zeros((tq, 1), I32))

    cut = lax.cond(jnp.max(n_ge) > topk, tie_cutoff, lambda: jnp.full((tq, 1), all_idx, I32))

    m_ref[...] = jnp.full(m_ref.shape, _NEG, F32)
    l_ref[...] = jnp.zeros(l_ref.shape, F32)
    acc_ref[...] = jnp.zeros(acc_ref.shape, F32)

    def attn_block(j, carry):
        kb = key_ref[j]
        idx = j * tk + lax.broadcasted_iota(I32, (1, tk), 1)
        sel = (kb > t) | ((kb == t) & (idx < cut))
        c_blk = ckv_ref[pl.ds(pl.multiple_of(j * tk, tk), tk), :]
        for hd in range(DSA_HEADS):
            lg = _dot_nt(qabs_ref[:, hd * DSA_LATENT:(hd + 1) * DSA_LATENT], c_blk)
            lg = jnp.where(sel, lg, _NEG)
            m_old = m_ref[hd]
            m_new = jnp.maximum(m_old, jnp.max(lg, axis=-1, keepdims=True))
            p = jnp.exp(lg - m_new)
            alpha = jnp.exp(m_old - m_new)
            l_ref[hd] = alpha * l_ref[hd] + jnp.sum(p, axis=-1, keepdims=True)
            acc_ref[hd] = alpha * acc_ref[hd] + _dot(p.astype(BF16), c_blk)
            m_ref[hd] = m_new
        return carry

    lax.fori_loop(0, n_kb, attn_block, 0)

    for hd in range(DSA_HEADS):
        o_lat = (acc_ref[hd] / l_ref[hd]).astype(BF16)
        out_ref[:, hd * DSA_HEAD_DIM:(hd + 1) * DSA_HEAD_DIM] = _dot(o_lat, wuv_ref[hd]).astype(BF16)


def _dsa(qi, small, qabs, kidx, ckv, wuv, tq, tk):
    T = qi.shape[0]
    topk = min(TOPK_MAX, T // 4)
    row = lambda n: pl.BlockSpec((tq, n), lambda i: (i, 0))
    body = functools.partial(_dsa_body, tq=tq, tk=tk, topk=topk, nbits_idx=int(T).bit_length())
    return pl.pallas_call(
        body,
        grid=(T // tq,),
        in_specs=[row(qi.shape[1]), row(_SMALL), row(qabs.shape[1]),
                  _resident(kidx.shape), _resident(ckv.shape), _resident(wuv.shape)],
        out_specs=row(DSA_HEADS * DSA_HEAD_DIM),
        out_shape=jax.ShapeDtypeStruct((T, DSA_HEADS * DSA_HEAD_DIM), BF16),
        scratch_shapes=[pltpu.VMEM((T // tk, tq, tk), I32),
                        pltpu.VMEM((DSA_HEADS, tq, 1), F32),
                        pltpu.VMEM((DSA_HEADS, tq, 1), F32),
                        pltpu.VMEM((DSA_HEADS, tq, DSA_LATENT), F32)],
        compiler_params=_params(),
        name="dsa",
    )(qi, small, qabs, kidx, ckv, wuv)


def _log_sigmoid(v):
    return jnp.minimum(v, 0.0) - jnp.log1p(jnp.exp(-jnp.abs(v)))


def _chunk_cumsum(v, axis):
    pos = lax.broadcasted_iota(I32, v.shape, axis) & (CHUNK - 1)
    d = 1
    while d < CHUNK:
        v = v + jnp.where(pos >= d, pltpu.roll(v, d, axis=axis), 0.0)
        d *= 2
    return v


def _mlstm_body(mqk_ref, mv_ref, small_ref, gt_ref, mo_ref, cw_ref, cb_ref, gbc_ref, gbr_ref, ng_ref,
                out_ref, xe_ref, c_ref, n_ref, m_ref, hs_ref, *, rows):
    @pl.when(pl.program_id(0) == 0)
    def _():
        xe_ref[0:8, :] = jnp.zeros((8, xe_ref.shape[1]), F32)
        c_ref[...] = jnp.zeros(c_ref.shape, F32)
        n_ref[...] = jnp.zeros(n_ref.shape, F32)
        m_ref[...] = jnp.zeros(m_ref.shape, F32)

    x = mqk_ref[...]
    xe_ref[8:8 + rows, :] = x
    y = cb_ref[...]
    for j in range(CONV_W - 1):
        y = y + xe_ref[5 + j:5 + j + rows, :] * cw_ref[j:j + 1, :]
    y = y + x * cw_ref[CONV_W - 1:CONV_W, :]
    xe_ref[0:8, :] = x[rows - 8:rows, :]
    qk = y * jax.nn.sigmoid(y)
    nqk = ML_HEADS * ML_QK_DIM
    q_all = (qk[:, :nqk] * (ML_QK_DIM ** -0.5)).astype(BF16)
    k_all = qk[:, nqk:]

    g_col = small_ref[...] + gbc_ref[...]
    g_row = gt_ref[...] + gbr_ref[...]
    b_col = _chunk_cumsum(_log_sigmoid(g_col), 0)
    b_row = _chunk_cumsum(_log_sigmoid(g_row), 1)

    tri = lax.broadcasted_iota(I32, (CHUNK, CHUNK), 1) <= lax.broadcasted_iota(I32, (CHUNK, CHUNK), 0)

    for c in range(rows // CHUNK):
        lo, hi = c * CHUNK, (c + 1) * CHUNK
        for hd in range(ML_HEADS):
            bc = b_col[lo:hi, _S_MF + hd:_S_MF + hd + 1]
            lic = g_col[lo:hi, _S_MI + hd:_S_MI + hd + 1]
            br = b_row[ML_HEADS + hd:ML_HEADS + hd + 1, lo:hi]
            lir = g_row[hd:hd + 1, lo:hi]
            g_tot = bc[CHUNK - 1:CHUNK, :]
            m_prev = m_ref[hd][:, 0:1]

            dmat = jnp.where(tri, bc - br + lir, -jnp.inf)
            inter = bc + m_prev
            m_t = jnp.maximum(inter, jnp.max(dmat, axis=-1, keepdims=True))
            w_intra = jnp.exp(dmat - m_t)
            a_inter = jnp.exp(inter - m_t)

            qh = q_all[lo:hi, hd * ML_QK_DIM:(hd + 1) * ML_QK_DIM]
            kh = k_all[lo:hi, hd * ML_QK_DIM:(hd + 1) * ML_QK_DIM]
            vh = mv_ref[lo:hi, hd * ML_V_DIM:(hd + 1) * ML_V_DIM]
            s_qk = _dot_nt(qh, kh.astype(BF16)) * w_intra
            c_prev = c_ref[hd]
            n_prev = n_ref[hd]
            num = a_inter * _dot(qh, c_prev.astype(BF16)) + _dot(s_qk.astype(BF16), vh)
            den = (a_inter * jnp.sum(qh.astype(F32) * n_prev, axis=-1, keepdims=True)
                   + jnp.sum(s_qk, axis=-1, keepdims=True))
            hs_ref[lo:hi, hd * ML_V_DIM:(hd + 1) * ML_V_DIM] = (
                num / jnp.maximum(jnp.abs(den), jnp.exp(-m_t)))

            m_new = jnp.maximum(g_tot + m_prev, jnp.max(g_tot - br + lir, axis=-1, keepdims=True))
            a_state = jnp.exp(g_tot + m_prev - m_new)
            wk = jnp.exp(g_tot - bc + lic - m_new) * kh
            c_ref[hd] = a_state * c_prev + _dot(wk.T.astype(BF16), vh)
            n_ref[hd] = a_state * n_prev + jnp.sum(wk, axis=0, keepdims=True)
            m_ref[hd] = jnp.broadcast_to(m_new, m_ref.shape[1:])

    for hd in range(ML_HEADS):
        sl = slice(hd * ML_V_DIM, (hd + 1) * ML_V_DIM)
        out_ref[:, sl] = (_rms(hs_ref[:, sl], ng_ref[:, sl]) * jax.nn.sigmoid(mo_ref[:, sl])).astype(BF16)


def _mlstm(mqk, mv, small, gt, mo, cw, cb, gbc, gbr, ng, rows):
    T = mqk.shape[0]
    row = lambda n: pl.BlockSpec((rows, n), lambda i: (i, 0))
    nv = ML_HEADS * ML_V_DIM
    return pl.pallas_call(
        functools.partial(_mlstm_body, rows=rows),
        grid=(T // rows,),
        in_specs=[row(mqk.shape[1]), row(nv), row(_SMALL), pl.BlockSpec((8, rows), lambda i: (0, i)), row(nv),
                  _resident(cw.shape), _resident(cb.shape), _resident(gbc.shape), _resident(gbr.shape),
                  _resident(ng.shape)],
        out_specs=row(nv),
        out_shape=jax.ShapeDtypeStruct((T, nv), BF16),
        scratch_shapes=[pltpu.VMEM((rows + 8, mqk.shape[1]), F32),
                        pltpu.VMEM((ML_HEADS, ML_QK_DIM, ML_V_DIM), F32),
                        pltpu.VMEM((ML_HEADS, 1, ML_QK_DIM), F32),
                        pltpu.VMEM((ML_HEADS, 1, 128), F32),
                        pltpu.VMEM((rows, nv), F32)],
        compiler_params=_params(),
        name="mlstm",
    )(mqk, mv, small, gt, mo, cw, cb, gbc, gbr, ng)


def _memkv_body(mem_ref, g_ref, w_ref, out_ref):
    mn = _rms(mem_ref[...], g_ref[...]).astype(BF16)
    out_ref[...] = _dot(mn, w_ref[...].astype(BF16)).astype(BF16)


def _memkv(mem, g, w, tn):
    M, D = mem.shape
    N = w.shape[1]
    return pl.pallas_call(
        _memkv_body,
        grid=(N // tn,),
        in_specs=[_resident(mem.shape), _resident(g.shape), pl.BlockSpec((D, tn), lambda j: (0, j))],
        out_specs=pl.BlockSpec((M, tn), lambda j: (0, j)),
        out_shape=jax.ShapeDtypeStruct((M, N), BF16),
        compiler_params=_params(),
        name="memkv",
    )(mem, g, w)


def _mixout_body(x_ref, dsa_ref, ml_ref, kv_ref, wo_ref, wq_ref, wc_ref, wr_ref, gx_ref, gf_ref,
                 x2_ref, hf_ref, rl_ref, o_ref):
    nd = dsa_ref.shape[1]
    x1 = x_ref[...] + _dot(dsa_ref[...], wo_ref[0:nd, :]) + _dot(ml_ref[...], wo_ref[nd:, :])
    q = _dot(_rms(x1, gx_ref[...]).astype(BF16), wq_ref[...]).astype(BF16)
    for hd in range(X_HEADS):
        sl = slice(hd * X_HEAD_DIM, (hd + 1) * X_HEAD_DIM)
        lg = _dot_nt(q[:, sl], kv_ref[:, sl]) * (X_HEAD_DIM ** -0.5)
        e = jnp.exp(lg - jnp.max(lg, axis=-1, keepdims=True))
        p = e / jnp.sum(e, axis=-1, keepdims=True)
        v = kv_ref[:, D_MODEL + hd * X_HEAD_DIM:D_MODEL + (hd + 1) * X_HEAD_DIM]
        o_ref[:, sl] = _dot(p.astype(BF16), v).astype(BF16)
    x2 = x1 + _dot(o_ref[...], wc_ref[...])
    x2_ref[...] = x2
    hf = _rms(x2, gf_ref[...])
    hf_ref[...] = hf
    rl_ref[...] = _dot(hf.astype(BF16), wr_ref[...])


def _mixout(x, dsa, ml, kv, wo, wq, wc, wr, gx, gf, tm):
    T = x.shape[0]
    row = lambda n: pl.BlockSpec((tm, n), lambda i: (i, 0))
    return pl.pallas_call(
        _mixout_body,
        grid=(T // tm,),
        in_specs=[row(D_MODEL), row(dsa.shape[1]), row(ml.shape[1]), _resident(kv.shape), _resident(wo.shape),
                  _resident(wq.shape), _resident(wc.shape), _resident(wr.shape), _resident(gx.shape),
                  _resident(gf.shape)],
        out_specs=[row(D_MODEL), row(D_MODEL), row(wr.shape[1])],
        out_shape=[jax.ShapeDtypeStruct((T, D_MODEL), F32), jax.ShapeDtypeStruct((T, D_MODEL), F32),
                   jax.ShapeDtypeStruct((T, wr.shape[1]), F32)],
        scratch_shapes=[pltpu.VMEM((tm, D_MODEL), BF16)],
        compiler_params=_params(),
        name="mixout",
    )(x, dsa, ml, kv, wo, wq, wc, wr, gx, gf)


def _moe_body(tok_ref, dst_ref, be_ref, cnt_ref, nv_ref, hf_hbm, roww_ref, wg_ref, wu_ref, wd_ref, y_hbm,
              xbuf, ybuf, wgb, wub, wdb, gsem, ssem, *, bm):
    i = pl.program_id(0)
    n_valid = nv_ref[0]
    slot = i % 2

    def gather(blk, s):
        def issue(r, carry):
            tok = tok_ref[blk * bm + r]
            pltpu.make_async_copy(hf_hbm.at[pl.ds(tok, 1)], xbuf.at[s, pl.ds(r, 1)], gsem.at[s]).start()
            return carry
        lax.fori_loop(0, bm, issue, 0)

    def wait_gather(s):
        pltpu.make_async_copy(hf_hbm.at[pl.ds(0, bm)], xbuf.at[s], gsem.at[s]).wait()

    def scatter(blk, s):
        def issue(r, carry):
            dst = dst_ref[blk * bm + r]
            pltpu.make_async_copy(ybuf.at[s, pl.ds(r, 1)], y_hbm.at[pl.ds(dst, 1)], ssem.at[s]).start()
            return carry
        lax.fori_loop(0, cnt_ref[blk], issue, 0)

    def wait_scatter(blk, s):
        def wait_row(r, carry):
            pltpu.make_async_copy(ybuf.at[s, pl.ds(0, 1)], y_hbm.at[pl.ds(0, 1)], ssem.at[s]).wait()
            return carry
        lax.fori_loop(0, cnt_ref[blk], wait_row, 0)

    @pl.when(i < n_valid)
    def _():
        @pl.when(i == 0)
        def _():
            gather(0, 0)

        @pl.when(i + 1 < n_valid)
        def _():
            gather(i + 1, 1 - slot)

        @pl.when((i == 0) | (be_ref[i] != be_ref[jnp.maximum(i - 1, 0)]))
        def _():
            wgb[...] = wg_ref[0].astype(BF16)
            wub[...] = wu_ref[0].astype(BF16)
            wdb[...] = wd_ref[0].astype(BF16)

        wait_gather(slot)
        xb = xbuf[slot].astype(BF16)
        gate = _dot(xb, wgb[...])
        a = gate * jax.nn.sigmoid(gate) * _dot(xb, wub[...])
        y = _dot(a.astype(BF16), wdb[...]) * roww_ref[...]

        @pl.when(i >= 2)
        def _():
            wait_scatter(i - 2, slot)

        ybuf[slot] = y
        scatter(i, slot)

        @pl.when(i == n_valid - 1)
        def _():
            @pl.when(i >= 1)
            def _():
                wait_scatter(i - 1, 1 - slot)
            wait_scatter(i, slot)


def _moe(tok, dst, blk_e, blk_cnt, n_valid, hf, roww, wg, wu, wd, n_out_rows, bm):
    n_blk = blk_e.shape[0]
    D = hf.shape[1]
    wspec = lambda shape: pl.BlockSpec((1,) + shape, lambda i, tok, dst, be, cnt, nv: (be[i], 0, 0))
    grid_spec = pltpu.PrefetchScalarGridSpec(
        num_scalar_prefetch=5,
        grid=(n_blk,),
        in_specs=[pl.BlockSpec(memory_space=pl.ANY),
                  pl.BlockSpec((bm, 1), lambda i, *_: (i, 0)),
                  wspec((D, D_EXPERT)), wspec((D, D_EXPERT)), wspec((D_EXPERT, D))],
        out_specs=pl.BlockSpec(memory_space=pl.ANY),
        scratch_shapes=[pltpu.VMEM((2, bm, D), F32), pltpu.VMEM((2, bm, D), F32),
                        pltpu.VMEM((D, D_EXPERT), BF16), pltpu.VMEM((D, D_EXPERT), BF16),
                        pltpu.VMEM((D_EXPERT, D), BF16),
                        pltpu.SemaphoreType.DMA((2,)), pltpu.SemaphoreType.DMA((2,))],
    )
    return pl.pallas_call(
        functools.partial(_moe_body, bm=bm),
        grid_spec=grid_spec,
        out_shape=jax.ShapeDtypeStruct((n_out_rows, D), F32),
        compiler_params=_params(),
        name="moe",
    )(tok, dst, blk_e, blk_cnt, n_valid, hf, roww, wg, wu, wd)


def _route(rl, b_group, b_router, bm):
    N = rl.shape[0]
    g_logits = rl[:, :N_GROUPS] + b_group
    g_prob = jax.nn.softmax(g_logits, axis=-1)
    g_sel = jnp.argmax(g_logits, axis=-1)
    p_g = jnp.take_along_axis(g_prob, g_sel[:, None], axis=-1)
    e_logits = (rl[:, N_GROUPS:N_GROUPS + N_EXPERTS] + b_router).reshape(N, N_GROUPS, EXP_PER_GROUP)
    e_logits = jnp.take_along_axis(e_logits, g_sel[:, None, None], axis=1)[:, 0]
    top_p, top_local = lax.top_k(jax.nn.softmax(e_logits, axis=-1), TOPK_IN_GROUP)
    gates = p_g * top_p / top_p.sum(-1, keepdims=True)
    expert_id = (g_sel[:, None] * EXP_PER_GROUP + top_local).astype(I32)

    A = N * TOPK_IN_GROUP
    flat_e = expert_id.reshape(A)
    flat_w = gates.reshape(A)
    onehot = (flat_e[:, None] == jnp.arange(N_EXPERTS, dtype=I32)[None, :]).astype(I32)
    rank = jnp.sum((jnp.cumsum(onehot, axis=0) - onehot) * onehot, axis=1)
    counts = jnp.sum(onehot, axis=0)
    padded = (counts + bm - 1) // bm * bm
    pad_ends = jnp.cumsum(padded)
    row = (pad_ends - padded)[flat_e] + rank
    n_blk = -(-A // bm) + N_EXPERTS
    n_rows = n_blk * bm
    a_ids = jnp.arange(A, dtype=I32)
    row_tok = jnp.zeros((n_rows,), I32).at[row].set(a_ids // TOPK_IN_GROUP)
    row_dst = jnp.zeros((n_rows,), I32).at[row].set(a_ids)
    row_w = jnp.zeros((n_rows,), F32).at[row].set(flat_w)
    blk_lo = jnp.arange(n_blk, dtype=I32) * bm
    blk_e = jnp.minimum(jnp.searchsorted(pad_ends, blk_lo, side='right'), N_EXPERTS - 1).astype(I32)
    blk_cnt = jnp.clip(counts[blk_e] - (blk_lo - (pad_ends - padded)[blk_e]), 0, bm).astype(I32)
    n_valid = (pad_ends[-1] // bm).astype(I32).reshape(1)
    return row_tok, row_dst, row_w.reshape(n_rows, 1), blk_e, blk_cnt, n_valid


def _final_body(x_ref, y_ref, g_ref, out_ref):
    d = x_ref.shape[1]
    out_ref[...] = _rms(x_ref[...] + y_ref[:, :d] + y_ref[:, d:], g_ref[...])


def _final(x2, y_pairs, g, tm):
    T, D = x2.shape
    return pl.pallas_call(
        _final_body,
        grid=(T // tm,),
        in_specs=[pl.BlockSpec((tm, D), lambda i: (i, 0)), pl.BlockSpec((tm, 2 * D), lambda i: (i, 0)),
                  _resident(g.shape)],
        out_specs=pl.BlockSpec((tm, D), lambda i: (i, 0)),
        out_shape=jax.ShapeDtypeStruct((T, D), F32),
        compiler_params=_params(),
        name="final",
    )(x2, y_pairs, g)


def _tile_sizes(T):
    pick = lambda want: want if T % want == 0 else CHUNK
    return dict(inproj=pick(256), dsa_q=pick(256), dsa_k=pick(256), mlstm=pick(256), mixout=pick(256),
                final=pick(512), moe=128)


def _layer(x, mem, norm_mix_g, w_in, kv_norm_g, k_idx_norm_g, w_uk, w_uv, conv_w, conv_b, gate_b, ml_norm_g,
           w_out, norm_x_g, mem_norm_g, w_cq, w_ckv, w_co, norm_ffn_g, w_group, b_group, w_router, b_router,
           w_gate, w_up, w_down, out_g):
    T = x.shape[0]
    ts = _tile_sizes(T)
    r2 = lambda v: v.reshape(1, -1)

    w_r = jnp.concatenate([
        w_in[:, _O_DQ:_O_MQ], w_in[:, _O_MI:_O_MO],
        jnp.zeros((D_MODEL, _SMALL - IDX_DIM - IDX_HEADS - 2 * ML_HEADS), w_in.dtype),
        w_in[:, _O_MQ:_O_MI], w_in[:, _O_MO:_O_END]], axis=1).astype(BF16)
    wuk_t = jnp.transpose(w_uk, (1, 2, 0)).astype(BF16)
    wuv_t = jnp.transpose(w_uv, (1, 0, 2)).astype(BF16)

    qabs, ckv, qi, kidx, small, mqk, mv, mo = _inproj(
        x, r2(norm_mix_g), w_r, wuk_t, r2(kv_norm_g), r2(k_idx_norm_g), ts["inproj"])

    dsa_out = _dsa(qi, small, qabs, kidx, ckv, wuv_t, ts["dsa_q"], ts["dsa_k"])

    gate_rows = jnp.transpose(small[:, _S_MI:_S_MI + 2 * ML_HEADS])
    gb_col = jnp.zeros((1, _SMALL), F32).at[0, _S_MI:_S_MI + 2 * ML_HEADS].set(gate_b)
    ml_out = _mlstm(mqk, mv, small, gate_rows, mo, conv_w, r2(conv_b), gb_col, gate_b.reshape(-1, 1),
                    r2(ml_norm_g), ts["mlstm"])

    kv = _memkv(mem, r2(mem_norm_g), w_ckv, 512)
    w_rt = jnp.concatenate([w_group, w_router,
                            jnp.zeros((D_MODEL, 128 - N_GROUPS - N_EXPERTS), w_group.dtype)], axis=1)
    x2, hf, rl = _mixout(x, dsa_out, ml_out, kv, w_out.astype(BF16), w_cq.astype(BF16), w_co.astype(BF16),
                         w_rt.astype(BF16), r2(norm_x_g), r2(norm_ffn_g), ts["mixout"])

    bm = ts["moe"]
    row_tok, row_dst, row_w, blk_e, blk_cnt, n_valid = _route(rl, b_group, b_router, bm)
    y_rows = _moe(row_tok, row_dst, blk_e, blk_cnt, n_valid, hf, row_w, w_gate, w_up, w_down,
                  T * TOPK_IN_GROUP, bm)
    y_pairs = y_rows.reshape(T, TOPK_IN_GROUP * D_MODEL)
    return _final(x2, y_pairs, r2(out_g), ts["final"])


def kernel(x, mem, norm_mix_g, w_in, kv_norm_g, k_idx_norm_g, w_uk, w_uv, conv_w, conv_b, gate_b, ml_norm_g,
           w_out, norm_x_g, mem_norm_g, w_cq, w_ckv, w_co, norm_ffn_g, w_group, b_group, w_router, b_router,
           w_gate, w_up, w_down, final_norm_g):
    B, T, D = x.shape
    assert B == 1 and D == D_MODEL and norm_mix_g.shape[0] == 1 and T % CHUNK == 0
    out = _layer(x[0], mem[0], norm_mix_g[0], w_in[0], kv_norm_g[0], k_idx_norm_g[0], w_uk[0], w_uv[0],
                 conv_w[0], conv_b[0], gate_b[0], ml_norm_g[0], w_out[0], norm_x_g[0], mem_norm_g[0],
                 w_cq[0], w_ckv[0], w_co[0], norm_ffn_g[0], w_group[0], b_group[0], w_router[0], b_router[0],
                 w_gate[0], w_up[0], w_down[0], final_norm_g)
    return out[None]
```

```python
import functools

import jax
import jax.numpy as jnp
import numpy as np
from jax import lax
from jax.experimental import pallas as pl
from jax.experimental.pallas import tpu as pltpu

F32 = jnp.float32
BF16 = jnp.bfloat16
I32 = jnp.int32
I16 = jnp.int16

EPS = 1e-6
CHUNK = 64
D_MODEL = 2048

DSA_HEADS = 8
DSA_HEAD_DIM = 128
DSA_LATENT = 256
IDX_HEADS = 8
IDX_DIM = 64
TOPK_MAX = 256

ML_HEADS = 4
ML_QK_DIM = 128
ML_V_DIM = 256
CONV_W = 4

X_HEADS = 4
X_HEAD_DIM = D_MODEL // X_HEADS

N_GROUPS = 4
EXP_PER_GROUP = 8
N_EXPERTS = N_GROUPS * EXP_PER_GROUP
TOPK_IN_GROUP = 2
D_EXPERT = 512

_O_DQ = 0
_O_CKV = _O_DQ + DSA_HEADS * DSA_HEAD_DIM
_O_QI = _O_CKV + DSA_LATENT
_O_KI = _O_QI + IDX_HEADS * IDX_DIM
_O_WI = _O_KI + IDX_DIM
_O_MQ = _O_WI + IDX_HEADS
_O_MK = _O_MQ + ML_HEADS * ML_QK_DIM
_O_MV = _O_MK + ML_HEADS * ML_QK_DIM
_O_MI = _O_MV + ML_HEADS * ML_V_DIM
_O_MF = _O_MI + ML_HEADS
_O_MO = _O_MF + ML_HEADS
_O_END = _O_MO + ML_HEADS * ML_V_DIM

_G_DQ = (0, 1024)
_G_CKV = (1024, 1280)
_G_QI = (1280, 1792)
_G_SMALL = (1792, 1920)
_G_MQK = (1920, 2944)
_G_MV = (2944, 3968)
_G_MO = (3968, 4992)
_W_COLS = 4992
_S_WI = IDX_DIM
_S_MI = _S_WI + IDX_HEADS
_S_MF = _S_MI + ML_HEADS
_SMALL = 128

_VMEM_LIMIT = 56 * 1024 * 1024
_INT_MIN = -(2 ** 31)
_I16_MIN = -(2 ** 15)
_CHUNK_SHIFT = CHUNK.bit_length() - 1
_LOG2E = 1.4426950408889634
_NEG = -1e30


def _rms(v, g):
    return v * lax.rsqrt(jnp.mean(v * v, axis=-1, keepdims=True) + EPS) * g


def _dot(a, b):
    return jnp.dot(a, b, preferred_element_type=F32)


def _dot_nt(a, b):
    return lax.dot_general(a, b, (((1,), (1,)), ((), ())), preferred_element_type=F32)


def _resident(shape):
    nd = len(shape)
    return pl.BlockSpec(shape, lambda *_: (0,) * nd, pipeline_mode=pl.Buffered(1))


def _params(n_axes=1):
    return pltpu.CompilerParams(dimension_semantics=("arbitrary",) * n_axes,
                                vmem_limit_bytes=_VMEM_LIMIT)


def _inproj_body(x_ref, g_ref, w_ref, wuk_ref, kvg_ref, kig_ref,
                 qabs_ref, ckv_ref, qi_ref, kidx_ref, small_ref, mqk_ref, mv_ref, mo_ref):
    h = _rms(x_ref[...], g_ref[...]).astype(BF16)

    def proj(grp):
        return _dot(h, w_ref[:, grp[0]:grp[1]])

    dq = proj(_G_DQ)
    for hd in range(DSA_HEADS):
        qh = dq[:, hd * DSA_HEAD_DIM:(hd + 1) * DSA_HEAD_DIM].astype(BF16)
        qa = _dot(qh, wuk_ref[hd]) * (DSA_HEAD_DIM ** -0.5 * _LOG2E)
        qabs_ref[:, hd * DSA_LATENT:(hd + 1) * DSA_LATENT] = qa.astype(BF16)
    ckv_ref[...] = _rms(proj(_G_CKV), kvg_ref[...]).astype(BF16)
    qi_ref[...] = (proj(_G_QI) * (IDX_DIM ** -0.5)).astype(BF16)
    small = proj(_G_SMALL)
    small_ref[...] = small
    kidx_ref[...] = _rms(small[:, :IDX_DIM], kig_ref[...]).astype(BF16)
    mqk_ref[...] = proj(_G_MQK)
    mv_ref[...] = proj(_G_MV).astype(BF16)
    mo_ref[...] = proj(_G_MO)


def _inproj(x, g, w, wuk, kvg, kig, tm):
    T = x.shape[0]
    row = lambda n: pl.BlockSpec((tm, n), lambda i: (i, 0))
    outs = [(8 * DSA_LATENT, BF16), (DSA_LATENT, BF16), (IDX_HEADS * IDX_DIM, BF16), (IDX_DIM, BF16),
            (_SMALL, F32), (2 * ML_HEADS * ML_QK_DIM, F32), (ML_HEADS * ML_V_DIM, BF16),
            (ML_HEADS * ML_V_DIM, F32)]
    return pl.pallas_call(
        _inproj_body,
        grid=(T // tm,),
        in_specs=[row(D_MODEL), _resident(g.shape), _resident(w.shape), _resident(wuk.shape),
                  _resident(kvg.shape), _resident(kig.shape)],
        out_specs=[row(n) for n, _ in outs],
        out_shape=[jax.ShapeDtypeStruct((T, n), dt) for n, dt in outs],
        compiler_params=_params(),
        name="inproj",
    )(x, g, w, wuk, kvg, kig)


def _sublane_fold(v, op):
    acc = v[0:8, :]
    for r in range(1, v.shape[0] // 8):
        acc = op(acc, v[r * 8:(r + 1) * 8, :])
    return acc


def _dsa_body(qi_ref, wrow_ref, qabs_ref, kidx_ref, ckv_ref, ckvt_ref, wuv_ref, out_ref,
              key_ref, hi_ref, m_ref, l_ref, acc_ref, *, tq, tk, topk, nbits_idx):
    i = pl.program_id(0)
    n_kb = ((i + 1) * tq + tk - 1) // tk
    w_rows = wrow_ref[0:IDX_HEADS, :] * (IDX_HEADS ** -0.5)
    q_chunk = (i * tq + lax.broadcasted_iota(I32, (1, tq), 1)) >> _CHUNK_SHIFT

    def key_pos(j):
        return j * tk + lax.broadcasted_iota(I32, (tk, 1), 0)

    def score_block(j, carry):
        kx = kidx_ref[pl.ds(pl.multiple_of(j * tk, tk), tk), :]
        s = jnp.zeros((tk, tq), F32)
        for hd in range(IDX_HEADS):
            d = _dot_nt(kx, qi_ref[:, hd * IDX_DIM:(hd + 1) * IDX_DIM])
            s = s + w_rows[hd:hd + 1, :] * jnp.maximum(d, 0.0)
        bits = lax.bitcast_convert_type(s, I32)
        key = bits ^ ((bits >> 31) & 0x7FFFFFFF)
        key_ref[j] = jnp.where((key_pos(j) >> _CHUNK_SHIFT) <= q_chunk, key, _INT_MIN)
        return carry

    lax.fori_loop(0, n_kb, score_block, 0)

    def count(pred):
        def body(j, acc):
            hit = pred(key_ref[j], key_pos(j)).astype(I32)
            return acc + _sublane_fold(hit, jnp.add)
        acc = lax.fori_loop(0, n_kb, body, jnp.zeros((8, tq), I32))
        return jnp.sum(acc, axis=0, keepdims=True)

    def count16(ref, cand):
        c16 = cand.astype(I16)
        def body(j, acc):
            hit = jnp.where(ref[j] >= c16, jnp.int16(1), jnp.int16(0))
            part = hit[0:16, :]
            for r in range(1, tk // 16):
                part = part + hit[r * 16:(r + 1) * 16, :]
            return acc + part
        acc = lax.fori_loop(0, n_kb, body, jnp.zeros((16, tq), I16))
        return jnp.sum(acc.astype(I32), axis=0, keepdims=True)

    def kth_largest16(ref, kth):
        def bit(b, t):
            cand = t + lax.shift_left(jnp.int32(1), 15 - b)
            return jnp.where(count16(ref, cand) >= kth, cand, t)
        return lax.fori_loop(0, 16, bit, jnp.full((1, tq), _I16_MIN, I32))

    def split_block(j, carry):
        kb = key_ref[j]
        hi_ref[j] = (kb >> 16).astype(I16)
        return carry

    lax.fori_loop(0, n_kb, split_block, 0)
    t_hi = kth_largest16(hi_ref, topk)
    n_above = count16(hi_ref, t_hi + 1)

    def low_block(j, carry):
        kb = key_ref[j]
        lo = ((kb & 0xFFFF) + _I16_MIN).astype(I16)
        hi_ref[j] = jnp.where((kb >> 16) == t_hi, lo, jnp.int16(_I16_MIN))
        return carry

    lax.fori_loop(0, n_kb, low_block, 0)
    t_lo = kth_largest16(hi_ref, topk - n_above)
    t = lax.shift_left(t_hi, 16) + (t_lo - _I16_MIN)
    t = jnp.maximum(t, _INT_MIN + 1)
    n_ge = count(lambda kb, pos: kb >= t)
    n_gt = count(lambda kb, pos: kb > t)
    all_pos = jnp.int32(2 ** nbits_idx - 1)
    n_tie_take = jnp.where(n_ge > topk, topk - n_gt, all_pos)

    def tie_cutoff():
        def pos_bit(b, c):
            cand = c + lax.shift_left(jnp.int32(1), nbits_idx - 1 - b)
            f = count(lambda kb, pos: (kb == t) & (pos < cand))
            return jnp.where(f <= n_tie_take, cand, c)
        return lax.fori_loop(0, nbits_idx, pos_bit, jnp.zeros((1, tq), I32))

    cut = lax.cond(jnp.max(n_ge) > topk, tie_cutoff, lambda: jnp.full((1, tq), all_pos, I32))

    def bias_block(j, carry):
        kb = key_ref[j]
        sel = (kb > t) | ((kb == t) & (key_pos(j) < cut))
        key_ref[j] = lax.bitcast_convert_type(jnp.where(sel, 0.0, _NEG).astype(F32), I32)
        return carry

    lax.fori_loop(0, n_kb, bias_block, 0)

    m_ref[...] = jnp.full(m_ref.shape, _NEG, F32)
    l_ref[...] = jnp.zeros(l_ref.shape, F32)
    acc_ref[...] = jnp.zeros(acc_ref.shape, F32)

    def attn_block(j, carry):
        c_blk = ckv_ref[pl.ds(pl.multiple_of(j * tk, tk), tk), :]
        c_blk_t = ckvt_ref[j]
        bias = lax.bitcast_convert_type(key_ref[j], F32)
        for hd in range(DSA_HEADS):
            lg = _dot_nt(c_blk, qabs_ref[:, hd * DSA_LATENT:(hd + 1) * DSA_LATENT]) + bias
            m_old = m_ref[hd:hd + 1, :]
            m_new = jnp.maximum(m_old, jnp.max(_sublane_fold(lg, jnp.maximum), axis=0, keepdims=True))
            p = jnp.exp2(lg - m_new)
            alpha = jnp.exp2(m_old - m_new)
            l_ref[hd:hd + 1, :] = alpha * l_ref[hd:hd + 1, :] + jnp.sum(_sublane_fold(p, jnp.add), axis=0,
                                                                         keepdims=True)
            acc_ref[hd] = alpha * acc_ref[hd] + _dot(c_blk_t, p.astype(BF16))
            m_ref[hd:hd + 1, :] = m_new
        return carry

    lax.fori_loop(0, n_kb, attn_block, 0)

    for hd in range(DSA_HEADS):
        o_lat = (acc_ref[hd] / l_ref[hd:hd + 1, :]).T.astype(BF16)
        out_ref[:, hd * DSA_HEAD_DIM:(hd + 1) * DSA_HEAD_DIM] = _dot(o_lat, wuv_ref[hd]).astype(BF16)


def _dsa(qi, wrows, qabs, kidx, ckv, wuv, tq, tk):
    T = qi.shape[0]
    topk = min(TOPK_MAX, T // 4)
    n_kb = T // tk
    ckvt = jnp.transpose(ckv.reshape(n_kb, tk, DSA_LATENT), (0, 2, 1))
    row = lambda n: pl.BlockSpec((tq, n), lambda i: (i, 0))
    body = functools.partial(_dsa_body, tq=tq, tk=tk, topk=topk, nbits_idx=int(T).bit_length())
    return pl.pallas_call(
        body,
        grid=(T // tq,),
        in_specs=[row(qi.shape[1]), pl.BlockSpec((wrows.shape[0], tq), lambda i: (0, i)), row(qabs.shape[1]),
                  _resident(kidx.shape), _resident(ckv.shape), _resident(ckvt.shape), _resident(wuv.shape)],
        out_specs=row(DSA_HEADS * DSA_HEAD_DIM),
        out_shape=jax.ShapeDtypeStruct((T, DSA_HEADS * DSA_HEAD_DIM), BF16),
        scratch_shapes=[pltpu.VMEM((n_kb, tk, tq), I32), pltpu.VMEM((n_kb, tk, tq), I16),
                        pltpu.VMEM((DSA_HEADS, tq), F32),
                        pltpu.VMEM((DSA_HEADS, tq), F32), pltpu.VMEM((DSA_HEADS, DSA_LATENT, tq), F32)],
        compiler_params=_params(),
        name="dsa",
    )(qi, wrows, qabs, kidx, ckv, ckvt, wuv)


def _log_sigmoid(v):
    return jnp.minimum(v, 0.0) - jnp.log1p(jnp.exp(-jnp.abs(v)))


def _chunk_cumsum(v, axis):
    pos = lax.broadcasted_iota(I32, v.shape, axis) & (CHUNK - 1)
    d = 1
    while d < CHUNK:
        v = v + jnp.where(pos >= d, pltpu.roll(v, d, axis=axis), 0.0)
        d *= 2
    return v


def _mlstm_body(mqk_ref, mv_ref, small_ref, gt_ref, mo_ref, cw_ref, cb_ref, gbc_ref, gbr_ref, ng_ref,
                out_ref, xe_ref, c_ref, n_ref, m_ref, hs_ref, *, rows):
    @pl.when(pl.program_id(0) == 0)
    def _():
        xe_ref[0:8, :] = jnp.zeros((8, xe_ref.shape[1]), F32)
        c_ref[...] = jnp.zeros(c_ref.shape, F32)
        n_ref[...] = jnp.zeros(n_ref.shape, F32)
        m_ref[...] = jnp.zeros(m_ref.shape, F32)

    x = mqk_ref[...]
    xe_ref[8:8 + rows, :] = x
    y = cb_ref[...]
    for j in range(CONV_W - 1):
        y = y + xe_ref[5 + j:5 + j + rows, :] * cw_ref[j:j + 1, :]
    y = y + x * cw_ref[CONV_W - 1:CONV_W, :]
    xe_ref[0:8, :] = x[rows - 8:rows, :]
    qk = y * jax.nn.sigmoid(y)
    nqk = ML_HEADS * ML_QK_DIM
    q_all = (qk[:, :nqk] * (ML_QK_DIM ** -0.5)).astype(BF16)
    k_all = qk[:, nqk:]

    g_col = small_ref[...] + gbc_ref[...]
    g_row = gt_ref[...] + gbr_ref[...]
    b_col = _chunk_cumsum(_log_sigmoid(g_col), 0)
    b_row = _chunk_cumsum(_log_sigmoid(g_row), 1)

    tri = lax.broadcasted_iota(I32, (CHUNK, CHUNK), 1) <= lax.broadcasted_iota(I32, (CHUNK, CHUNK), 0)

    for c in range(rows // CHUNK):
        lo, hi = c * CHUNK, (c + 1) * CHUNK
        for hd in range(ML_HEADS):
            bc = b_col[lo:hi, _S_MF + hd:_S_MF + hd + 1]
            lic = g_col[lo:hi, _S_MI + hd:_S_MI + hd + 1]
            br = b_row[ML_HEADS + hd:ML_HEADS + hd + 1, lo:hi]
            lir = g_row[hd:hd + 1, lo:hi]
            g_tot = bc[CHUNK - 1:CHUNK, :]
            m_prev = m_ref[hd][:, 0:1]

            dmat = jnp.where(tri, bc - br + lir, -jnp.inf)
            inter = bc + m_prev
            m_t = jnp.maximum(inter, jnp.max(dmat, axis=-1, keepdims=True))
            w_intra = jnp.exp(dmat - m_t)
            a_inter = jnp.exp(inter - m_t)

            qh = q_all[lo:hi, hd * ML_QK_DIM:(hd + 1) * ML_QK_DIM]
            kh = k_all[lo:hi, hd * ML_QK_DIM:(hd + 1) * ML_QK_DIM]
            vh = mv_ref[lo:hi, hd * ML_V_DIM:(hd + 1) * ML_V_DIM]
            s_qk = _dot_nt(qh, kh.astype(BF16)) * w_intra
            c_prev = c_ref[hd]
            n_prev = n_ref[hd]
            num = a_inter * _dot(qh, c_prev.astype(BF16)) + _dot(s_qk.astype(BF16), vh)
            den = (a_inter * jnp.sum(qh.astype(F32) * n_prev, axis=-1, keepdims=True)
                   + jnp.sum(s_qk, axis=-1, keepdims=True))
            hs_ref[lo:hi, hd * ML_V_DIM:(hd + 1) * ML_V_DIM] = (
                num / jnp.maximum(jnp.abs(den), jnp.exp(-m_t)))

            m_new = jnp.maximum(g_tot + m_prev, jnp.max(g_tot - br + lir, axis=-1, keepdims=True))
            a_state = jnp.exp(g_tot + m_prev - m_new)
            wk = jnp.exp(g_tot - bc + lic - m_new) * kh
            c_ref[hd] = a_state * c_prev + _dot(wk.T.astype(BF16), vh)
            n_ref[hd] = a_state * n_prev + jnp.sum(wk, axis=0, keepdims=True)
            m_ref[hd] = jnp.broadcast_to(m_new, m_ref.shape[1:])

    for hd in range(ML_HEADS):
        sl = slice(hd * ML_V_DIM, (hd + 1) * ML_V_DIM)
        out_ref[:, sl] = (_rms(hs_ref[:, sl], ng_ref[:, sl]) * jax.nn.sigmoid(mo_ref[:, sl])).astype(BF16)


def _mlstm(mqk, mv, small, gt, mo, cw, cb, gbc, gbr, ng, rows):
    T = mqk.shape[0]
    row = lambda n: pl.BlockSpec((rows, n), lambda i: (i, 0))
    nv = ML_HEADS * ML_V_DIM
    return pl.pallas_call(
        functools.partial(_mlstm_body, rows=rows),
        grid=(T // rows,),
        in_specs=[row(mqk.shape[1]), row(nv), row(_SMALL), pl.BlockSpec((8, rows), lambda i: (1, i)), row(nv),
                  _resident(cw.shape), _resident(cb.shape), _resident(gbc.shape), _resident(gbr.shape),
                  _resident(ng.shape)],
        out_specs=row(nv),
        out_shape=jax.ShapeDtypeStruct((T, nv), BF16),
        scratch_shapes=[pltpu.VMEM((rows + 8, mqk.shape[1]), F32),
                        pltpu.VMEM((ML_HEADS, ML_QK_DIM, ML_V_DIM), F32),
                        pltpu.VMEM((ML_HEADS, 1, ML_QK_DIM), F32),
                        pltpu.VMEM((ML_HEADS, 1, 128), F32),
                        pltpu.VMEM((rows, nv), F32)],
        compiler_params=_params(),
        name="mlstm",
    )(mqk, mv, small, gt, mo, cw, cb, gbc, gbr, ng)


def _memkv_body(mem_ref, g_ref, w_ref, out_ref):
    mn = _rms(mem_ref[...], g_ref[...]).astype(BF16)
    out_ref[...] = _dot(mn, w_ref[...].astype(BF16)).astype(BF16)


def _memkv(mem, g, w, tn):
    M, D = mem.shape
    N = w.shape[1]
    return pl.pallas_call(
        _memkv_body,
        grid=(N // tn,),
        in_specs=[_resident(mem.shape), _resident(g.shape), pl.BlockSpec((D, tn), lambda j: (0, j))],
        out_specs=pl.BlockSpec((M, tn), lambda j: (0, j)),
        out_shape=jax.ShapeDtypeStruct((M, N), BF16),
        compiler_params=_params(),
        name="memkv",
    )(mem, g, w)


def _mixout_body(x_ref, dsa_ref, ml_ref, kv_ref, wo_ref, wq_ref, wc_ref, wr_ref, gx_ref, gf_ref,
                 x2_ref, hf_ref, rl_ref, o_ref):
    nd = dsa_ref.shape[1]
    x1 = x_ref[...] + _dot(dsa_ref[...], wo_ref[0:nd, :]) + _dot(ml_ref[...], wo_ref[nd:, :])
    q = _dot(_rms(x1, gx_ref[...]).astype(BF16), wq_ref[...]).astype(BF16)
    for hd in range(X_HEADS):
        sl = slice(hd * X_HEAD_DIM, (hd + 1) * X_HEAD_DIM)
        lg = _dot_nt(q[:, sl], kv_ref[:, sl]) * (X_HEAD_DIM ** -0.5)
        e = jnp.exp(lg - jnp.max(lg, axis=-1, keepdims=True))
        p = e / jnp.sum(e, axis=-1, keepdims=True)
        v = kv_ref[:, D_MODEL + hd * X_HEAD_DIM:D_MODEL + (hd + 1) * X_HEAD_DIM]
        o_ref[:, sl] = _dot(p.astype(BF16), v).astype(BF16)
    x2 = x1 + _dot(o_ref[...], wc_ref[...])
    x2_ref[...] = x2
    hf = _rms(x2, gf_ref[...])
    hf_ref[...] = hf
    rl_ref[...] = _dot(hf.astype(BF16), wr_ref[...])


def _mixout(x, dsa, ml, kv, wo, wq, wc, wr, gx, gf, tm):
    T = x.shape[0]
    row = lambda n: pl.BlockSpec((tm, n), lambda i: (i, 0))
    return pl.pallas_call(
        _mixout_body,
        grid=(T // tm,),
        in_specs=[row(D_MODEL), row(dsa.shape[1]), row(ml.shape[1]), _resident(kv.shape), _resident(wo.shape),
                  _resident(wq.shape), _resident(wc.shape), _resident(wr.shape), _resident(gx.shape),
                  _resident(gf.shape)],
        out_specs=[row(D_MODEL), row(D_MODEL), row(wr.shape[1])],
        out_shape=[jax.ShapeDtypeStruct((T, D_MODEL), F32), jax.ShapeDtypeStruct((T, D_MODEL), F32),
                   jax.ShapeDtypeStruct((T, wr.shape[1]), F32)],
        scratch_shapes=[pltpu.VMEM((tm, D_MODEL), BF16)],
        compiler_params=_params(),
        name="mixout",
    )(x, dsa, ml, kv, wo, wq, wc, wr, gx, gf)


def _moe_body(tok_ref, dst_ref, be_ref, cnt_ref, nv_ref, hf_hbm, roww_ref, wg_ref, wu_ref, wd_ref, y_hbm,
              xbuf, ybuf, wgb, wub, wdb, gsem, ssem, *, bm):
    i = pl.program_id(0)
    n_valid = nv_ref[0]
    slot = i % 2

    def gather(blk, s):
        def issue(r, carry):
            tok = tok_ref[blk * bm + r]
            pltpu.make_async_copy(hf_hbm.at[pl.ds(tok, 1)], xbuf.at[s, pl.ds(r, 1)], gsem.at[s]).start()
            return carry
        lax.fori_loop(0, bm, issue, 0)

    def wait_gather(s):
        pltpu.make_async_copy(hf_hbm.at[pl.ds(0, bm)], xbuf.at[s], gsem.at[s]).wait()

    def scatter(blk, s):
        def issue(r, carry):
            dst = dst_ref[blk * bm + r]
            pltpu.make_async_copy(ybuf.at[s, pl.ds(r, 1)], y_hbm.at[pl.ds(dst, 1)], ssem.at[s]).start()
            return carry
        lax.fori_loop(0, cnt_ref[blk], issue, 0)

    def wait_scatter(blk, s):
        def wait_row(r, carry):
            pltpu.make_async_copy(ybuf.at[s, pl.ds(0, 1)], y_hbm.at[pl.ds(0, 1)], ssem.at[s]).wait()
            return carry
        lax.fori_loop(0, cnt_ref[blk], wait_row, 0)

    @pl.when(i < n_valid)
    def _():
        @pl.when(i == 0)
        def _():
            gather(0, 0)

        @pl.when(i + 1 < n_valid)
        def _():
            gather(i + 1, 1 - slot)

        @pl.when((i == 0) | (be_ref[i] != be_ref[jnp.maximum(i - 1, 0)]))
        def _():
            wgb[...] = wg_ref[0].astype(BF16)
            wub[...] = wu_ref[0].astype(BF16)
            wdb[...] = wd_ref[0].astype(BF16)

        wait_gather(slot)
        xb = xbuf[slot].astype(BF16)
        gate = _dot(xb, wgb[...])
        a = gate * jax.nn.sigmoid(gate) * _dot(xb, wub[...])
        y = _dot(a.astype(BF16), wdb[...]) * roww_ref[...]

        @pl.when(i >= 2)
        def _():
            wait_scatter(i - 2, slot)

        ybuf[slot] = y
        scatter(i, slot)

        @pl.when(i == n_valid - 1)
        def _():
            @pl.when(i >= 1)
            def _():
                wait_scatter(i - 1, 1 - slot)
            wait_scatter(i, slot)


def _moe(tok, dst, blk_e, blk_cnt, n_valid, hf, roww, wg, wu, wd, n_out_rows, bm):
    n_blk = blk_e.shape[0]
    D = hf.shape[1]
    wspec = lambda shape: pl.BlockSpec((1,) + shape, lambda i, tok, dst, be, cnt, nv: (be[i], 0, 0))
    grid_spec = pltpu.PrefetchScalarGridSpec(
        num_scalar_prefetch=5,
        grid=(n_blk,),
        in_specs=[pl.BlockSpec(memory_space=pl.ANY),
                  pl.BlockSpec((bm, 1), lambda i, *_: (i, 0)),
                  wspec((D, D_EXPERT)), wspec((D, D_EXPERT)), wspec((D_EXPERT, D))],
        out_specs=pl.BlockSpec(memory_space=pl.ANY),
        scratch_shapes=[pltpu.VMEM((2, bm, D), F32), pltpu.VMEM((2, bm, D), F32),
                        pltpu.VMEM((D, D_EXPERT), BF16), pltpu.VMEM((D, D_EXPERT), BF16),
                        pltpu.VMEM((D_EXPERT, D), BF16),
                        pltpu.SemaphoreType.DMA((2,)), pltpu.SemaphoreType.DMA((2,))],
    )
    return pl.pallas_call(
        functools.partial(_moe_body, bm=bm),
        grid_spec=grid_spec,
        out_shape=jax.ShapeDtypeStruct((n_out_rows, D), F32),
        compiler_params=_params(),
        name="moe",
    )(tok, dst, blk_e, blk_cnt, n_valid, hf, roww, wg, wu, wd)


def _route(rl, b_group, b_router, bm):
    N = rl.shape[0]
    g_logits = rl[:, :N_GROUPS] + b_group
    g_prob = jax.nn.softmax(g_logits, axis=-1)
    g_sel = jnp.argmax(g_logits, axis=-1)
    p_g = jnp.take_along_axis(g_prob, g_sel[:, None], axis=-1)
    e_logits = (rl[:, N_GROUPS:N_GROUPS + N_EXPERTS] + b_router).reshape(N, N_GROUPS, EXP_PER_GROUP)
    e_logits = jnp.take_along_axis(e_logits, g_sel[:, None, None], axis=1)[:, 0]
    top_p, top_local = lax.top_k(jax.nn.softmax(e_logits, axis=-1), TOPK_IN_GROUP)
    gates = p_g * top_p / top_p.sum(-1, keepdims=True)
    expert_id = (g_sel[:, None] * EXP_PER_GROUP + top_local).astype(I32)

    A = N * TOPK_IN_GROUP
    flat_e = expert_id.reshape(A)
    flat_w = gates.reshape(A)
    onehot = (flat_e[:, None] == jnp.arange(N_EXPERTS, dtype=I32)[None, :]).astype(I32)
    rank = jnp.sum((jnp.cumsum(onehot, axis=0) - onehot) * onehot, axis=1)
    counts = jnp.sum(onehot, axis=0)
    padded = (counts + bm - 1) // bm * bm
    pad_ends = jnp.cumsum(padded)
    row = (pad_ends - padded)[flat_e] + rank
    n_blk = -(-A // bm) + N_EXPERTS
    n_rows = n_blk * bm
    a_ids = jnp.arange(A, dtype=I32)
    row_tok = jnp.zeros((n_rows,), I32).at[row].set(a_ids // TOPK_IN_GROUP)
    row_dst = jnp.zeros((n_rows,), I32).at[row].set(a_ids)
    row_w = jnp.zeros((n_rows,), F32).at[row].set(flat_w)
    blk_lo = jnp.arange(n_blk, dtype=I32) * bm
    blk_e = jnp.minimum(jnp.searchsorted(pad_ends, blk_lo, side='right'), N_EXPERTS - 1).astype(I32)
    blk_cnt = jnp.clip(counts[blk_e] - (blk_lo - (pad_ends - padded)[blk_e]), 0, bm).astype(I32)
    n_valid = (pad_ends[-1] // bm).astype(I32).reshape(1)
    return row_tok, row_dst, row_w.reshape(n_rows, 1), blk_e, blk_cnt, n_valid


def _final_body(x_ref, y_ref, g_ref, out_ref):
    d = x_ref.shape[1]
    out_ref[...] = _rms(x_ref[...] + y_ref[:, :d] + y_ref[:, d:], g_ref[...])


def _final(x2, y_pairs, g, tm):
    T, D = x2.shape
    return pl.pallas_call(
        _final_body,
        grid=(T // tm,),
        in_specs=[pl.BlockSpec((tm, D), lambda i: (i, 0)), pl.BlockSpec((tm, 2 * D), lambda i: (i, 0)),
                  _resident(g.shape)],
        out_specs=pl.BlockSpec((tm, D), lambda i: (i, 0)),
        out_shape=jax.ShapeDtypeStruct((T, D), F32),
        compiler_params=_params(),
        name="final",
    )(x2, y_pairs, g)


def _tile_sizes(T):
    pick = lambda want: want if T % want == 0 else CHUNK
    return dict(inproj=pick(256), dsa_q=pick(256), dsa_k=pick(512), mlstm=pick(256), mixout=pick(256),
                final=pick(512), moe=128)


def _layer(x, mem, norm_mix_g, w_in, kv_norm_g, k_idx_norm_g, w_uk, w_uv, conv_w, conv_b, gate_b, ml_norm_g,
           w_out, norm_x_g, mem_norm_g, w_cq, w_ckv, w_co, norm_ffn_g, w_group, b_group, w_router, b_router,
           w_gate, w_up, w_down, out_g):
    T = x.shape[0]
    ts = _tile_sizes(T)
    r2 = lambda v: v.reshape(1, -1)

    w_r = jnp.concatenate([
        w_in[:, _O_DQ:_O_MQ], w_in[:, _O_MI:_O_MO],
        jnp.zeros((D_MODEL, _SMALL - IDX_DIM - IDX_HEADS - 2 * ML_HEADS), w_in.dtype),
        w_in[:, _O_MQ:_O_MI], w_in[:, _O_MO:_O_END]], axis=1).astype(BF16)
    wuk_t = jnp.transpose(w_uk, (1, 2, 0)).astype(BF16)
    wuv_t = jnp.transpose(w_uv, (1, 0, 2)).astype(BF16)

    qabs, ckv, qi, kidx, small, mqk, mv, mo = _inproj(
        x, r2(norm_mix_g), w_r, wuk_t, r2(kv_norm_g), r2(k_idx_norm_g), ts["inproj"])

    gate_rows = jnp.transpose(small[:, _S_WI:_S_MF + ML_HEADS])
    dsa_out = _dsa(qi, gate_rows, qabs, kidx, ckv, wuv_t, ts["dsa_q"], ts["dsa_k"])

    gb_col = jnp.zeros((1, _SMALL), F32).at[0, _S_MI:_S_MI + 2 * ML_HEADS].set(gate_b)
    ml_out = _mlstm(mqk, mv, small, gate_rows, mo, conv_w, r2(conv_b), gb_col, gate_b.reshape(-1, 1),
                    r2(ml_norm_g), ts["mlstm"])

    kv = _memkv(mem, r2(mem_norm_g), w_ckv, 512)
    w_rt = jnp.concatenate([w_group, w_router,
                            jnp.zeros((D_MODEL, 128 - N_GROUPS - N_EXPERTS), w_group.dtype)], axis=1)
    x2, hf, rl = _mixout(x, dsa_out, ml_out, kv, w_out.astype(BF16), w_cq.astype(BF16), w_co.astype(BF16),
                         w_rt.astype(BF16), r2(norm_x_g), r2(norm_ffn_g), ts["mixout"])

    bm = ts["moe"]
    row_tok, row_dst, row_w, blk_e, blk_cnt, n_valid = _route(rl, b_group, b_router, bm)
    y_rows = _moe(row_tok, row_dst, blk_e, blk_cnt, n_valid, hf, row_w, w_gate, w_up, w_down,
                  T * TOPK_IN_GROUP, bm)
    y_pairs = y_rows.reshape(T, TOPK_IN_GROUP * D_MODEL)
    return _final(x2, y_pairs, r2(out_g), ts["final"])


def kernel(x, mem, norm_mix_g, w_in, kv_norm_g, k_idx_norm_g, w_uk, w_uv, conv_w, conv_b, gate_b, ml_norm_g,
           w_out, norm_x_g, mem_norm_g, w_cq, w_ckv, w_co, norm_ffn_g, w_group, b_group, w_router, b_router,
           w_gate, w_up, w_down, final_norm_g):
    B, T, D = x.shape
    assert B == 1 and D == D_MODEL and norm_mix_g.shape[0] == 1 and T % CHUNK == 0
    out = _layer(x[0], mem[0], norm_mix_g[0], w_in[0], kv_norm_g[0], k_idx_norm_g[0], w_uk[0], w_uv[0],
                 conv_w[0], conv_b[0], gate_b[0], ml_norm_g[0], w_out[0], norm_x_g[0], mem_norm_g[0],
                 w_cq[0], w_ckv[0], w_co[0], norm_ffn_g[0], w_group[0], b_group[0], w_router[0], b_router[0],
                 w_gate[0], w_up[0], w_down[0], final_norm_g)
    return out[None]
```

```python
import functools

import jax
import jax.numpy as jnp
import numpy as np
from jax import lax
from jax.experimental import pallas as pl
from jax.experimental.pallas import tpu as pltpu

F32 = jnp.float32
BF16 = jnp.bfloat16
I32 = jnp.int32
I16 = jnp.int16

EPS = 1e-6
CHUNK = 64
D_MODEL = 2048

DSA_HEADS = 8
DSA_HEAD_DIM = 128
DSA_LATENT = 256
IDX_HEADS = 8
IDX_DIM = 64
TOPK_MAX = 256

ML_HEADS = 4
ML_QK_DIM = 128
ML_V_DIM = 256
CONV_W = 4

X_HEADS = 4
X_HEAD_DIM = D_MODEL // X_HEADS

N_GROUPS = 4
EXP_PER_GROUP = 8
N_EXPERTS = N_GROUPS * EXP_PER_GROUP
TOPK_IN_GROUP = 2
D_EXPERT = 512

_O_DQ = 0
_O_CKV = _O_DQ + DSA_HEADS * DSA_HEAD_DIM
_O_QI = _O_CKV + DSA_LATENT
_O_KI = _O_QI + IDX_HEADS * IDX_DIM
_O_WI = _O_KI + IDX_DIM
_O_MQ = _O_WI + IDX_HEADS
_O_MK = _O_MQ + ML_HEADS * ML_QK_DIM
_O_MV = _O_MK + ML_HEADS * ML_QK_DIM
_O_MI = _O_MV + ML_HEADS * ML_V_DIM
_O_MF = _O_MI + ML_HEADS
_O_MO = _O_MF + ML_HEADS
_O_END = _O_MO + ML_HEADS * ML_V_DIM

_G_DQ = (0, 1024)
_G_CKV = (1024, 1280)
_G_QI = (1280, 1792)
_G_SMALL = (1792, 1920)
_G_MQK = (1920, 2944)
_G_MV = (2944, 3968)
_G_MO = (3968, 4992)
_W_COLS = 4992
_S_WI = IDX_DIM
_S_MI = _S_WI + IDX_HEADS
_S_MF = _S_MI + ML_HEADS
_SMALL = 128

_VMEM_LIMIT = 56 * 1024 * 1024
_INT_MIN = -(2 ** 31)
_I16_MIN = -(2 ** 15)
_CHUNK_SHIFT = CHUNK.bit_length() - 1
_LOG2E = 1.4426950408889634
_SUBLANES = 8
_RANK_SEG = 512
_NEG = -1e30


def _rms(v, g):
    return v * lax.rsqrt(jnp.mean(v * v, axis=-1, keepdims=True) + EPS) * g


def _dot(a, b):
    return jnp.dot(a, b, preferred_element_type=F32)


def _dot_nt(a, b):
    return lax.dot_general(a, b, (((1,), (1,)), ((), ())), preferred_element_type=F32)


def _resident(shape):
    nd = len(shape)
    return pl.BlockSpec(shape, lambda *_: (0,) * nd, pipeline_mode=pl.Buffered(1))


def _params(n_axes=1):
    return pltpu.CompilerParams(dimension_semantics=("arbitrary",) * n_axes,
                                vmem_limit_bytes=_VMEM_LIMIT)


def _inproj_body(x_ref, g_ref, w_ref, wuk_ref, kvg_ref, kig_ref,
                 qabs_ref, ckv_ref, qi_ref, kidx_ref, small_ref, mqk_ref, mv_ref, mo_ref):
    h = _rms(x_ref[...], g_ref[...]).astype(BF16)

    def proj(grp):
        return _dot(h, w_ref[:, grp[0]:grp[1]])

    dq = proj(_G_DQ)
    for hd in range(DSA_HEADS):
        qh = dq[:, hd * DSA_HEAD_DIM:(hd + 1) * DSA_HEAD_DIM].astype(BF16)
        qa = _dot(qh, wuk_ref[hd]) * (DSA_HEAD_DIM ** -0.5 * _LOG2E)
        qabs_ref[:, hd * DSA_LATENT:(hd + 1) * DSA_LATENT] = qa.astype(BF16)
    ckv_ref[...] = _rms(proj(_G_CKV), kvg_ref[...]).astype(BF16)
    qi_ref[...] = (proj(_G_QI) * (IDX_DIM ** -0.5)).astype(BF16)
    small = proj(_G_SMALL)
    small_ref[...] = small
    kidx_ref[...] = _rms(small[:, :IDX_DIM], kig_ref[...]).astype(BF16)
    mqk_ref[...] = proj(_G_MQK)
    mv_ref[...] = proj(_G_MV).astype(BF16)
    mo_ref[...] = proj(_G_MO)


def _inproj(x, g, w, wuk, kvg, kig, tm):
    T = x.shape[0]
    row = lambda n: pl.BlockSpec((tm, n), lambda i: (i, 0))
    outs = [(8 * DSA_LATENT, BF16), (DSA_LATENT, BF16), (IDX_HEADS * IDX_DIM, BF16), (IDX_DIM, BF16),
            (_SMALL, F32), (2 * ML_HEADS * ML_QK_DIM, F32), (ML_HEADS * ML_V_DIM, BF16),
            (ML_HEADS * ML_V_DIM, F32)]
    return pl.pallas_call(
        _inproj_body,
        grid=(T // tm,),
        in_specs=[row(D_MODEL), _resident(g.shape), _resident(w.shape), _resident(wuk.shape),
                  _resident(kvg.shape), _resident(kig.shape)],
        out_specs=[row(n) for n, _ in outs],
        out_shape=[jax.ShapeDtypeStruct((T, n), dt) for n, dt in outs],
        compiler_params=_params(),
        name="inproj",
    )(x, g, w, wuk, kvg, kig)


def _sublane_fold(v, op):
    acc = v[0:8, :]
    for r in range(1, v.shape[0] // 8):
        acc = op(acc, v[r * 8:(r + 1) * 8, :])
    return acc


def _dsa_body(qi_ref, wrow_ref, qabs_ref, kidx_ref, ckv_ref, ckvt_ref, wuv_ref, out_ref,
              key_ref, hi_ref, m_ref, l_ref, acc_ref, *, tq, tk, topk, nbits_idx):
    i = pl.program_id(0)
    n_kb = ((i + 1) * tq + tk - 1) // tk
    w_rows = wrow_ref[0:IDX_HEADS, :] * (IDX_HEADS ** -0.5)
    q_chunk = (i * tq + lax.broadcasted_iota(I32, (1, tq), 1)) >> _CHUNK_SHIFT

    def key_pos(j):
        return j * tk + lax.broadcasted_iota(I32, (tk, 1), 0)

    def score_block(j, carry):
        kx = kidx_ref[pl.ds(pl.multiple_of(j * tk, tk), tk), :]
        s = jnp.zeros((tk, tq), F32)
        for hd in range(IDX_HEADS):
            d = _dot_nt(kx, qi_ref[:, hd * IDX_DIM:(hd + 1) * IDX_DIM])
            s = s + w_rows[hd:hd + 1, :] * jnp.maximum(d, 0.0)
        bits = lax.bitcast_convert_type(s, I32)
        key = bits ^ ((bits >> 31) & 0x7FFFFFFF)
        key_ref[j] = jnp.where((key_pos(j) >> _CHUNK_SHIFT) <= q_chunk, key, _INT_MIN)
        return carry

    lax.fori_loop(0, n_kb, score_block, 0)

    def count(pred):
        def body(j, acc):
            hit = pred(key_ref[j], key_pos(j)).astype(I32)
            return acc + _sublane_fold(hit, jnp.add)
        acc = lax.fori_loop(0, n_kb, body, jnp.zeros((8, tq), I32))
        return jnp.sum(acc, axis=0, keepdims=True)

    def count16(ref, cand):
        c16 = cand.astype(I16)
        def body(j, acc):
            hit = jnp.where(ref[j] >= c16, jnp.int16(1), jnp.int16(0))
            part = hit[0:16, :]
            for r in range(1, tk // 16):
                part = part + hit[r * 16:(r + 1) * 16, :]
            return acc + part
        acc = lax.fori_loop(0, n_kb, body, jnp.zeros((16, tq), I16))
        return jnp.sum(acc.astype(I32), axis=0, keepdims=True)

    def kth_largest16(ref, kth):
        def bit(b, t):
            cand = t + lax.shift_left(jnp.int32(1), 15 - b)
            return jnp.where(count16(ref, cand) >= kth, cand, t)
        return lax.fori_loop(0, 16, bit, jnp.full((1, tq), _I16_MIN, I32))

    def split_block(j, carry):
        kb = key_ref[j]
        hi_ref[j] = (kb >> 16).astype(I16)
        return carry

    lax.fori_loop(0, n_kb, split_block, 0)
    t_hi = kth_largest16(hi_ref, topk)
    n_above = count16(hi_ref, t_hi + 1)

    def low_block(j, carry):
        kb = key_ref[j]
        lo = ((kb & 0xFFFF) + _I16_MIN).astype(I16)
        hi_ref[j] = jnp.where((kb >> 16) == t_hi, lo, jnp.int16(_I16_MIN))
        return carry

    lax.fori_loop(0, n_kb, low_block, 0)
    t_lo = kth_largest16(hi_ref, topk - n_above)
    t = lax.shift_left(t_hi, 16) + (t_lo - _I16_MIN)
    t = jnp.maximum(t, _INT_MIN + 1)
    n_ge = count(lambda kb, pos: kb >= t)
    n_gt = count(lambda kb, pos: kb > t)
    all_pos = jnp.int32(2 ** nbits_idx - 1)
    n_tie_take = jnp.where(n_ge > topk, topk - n_gt, all_pos)

    def tie_cutoff():
        def pos_bit(b, c):
            cand = c + lax.shift_left(jnp.int32(1), nbits_idx - 1 - b)
            f = count(lambda kb, pos: (kb == t) & (pos < cand))
            return jnp.where(f <= n_tie_take, cand, c)
        return lax.fori_loop(0, nbits_idx, pos_bit, jnp.zeros((1, tq), I32))

    cut = lax.cond(jnp.max(n_ge) > topk, tie_cutoff, lambda: jnp.full((1, tq), all_pos, I32))

    def bias_block(j, carry):
        kb = key_ref[j]
        sel = (kb > t) | ((kb == t) & (key_pos(j) < cut))
        key_ref[j] = lax.bitcast_convert_type(jnp.where(sel, 0.0, _NEG).astype(F32), I32)
        return carry

    lax.fori_loop(0, n_kb, bias_block, 0)

    m_ref[...] = jnp.full(m_ref.shape, _NEG, F32)
    l_ref[...] = jnp.zeros(l_ref.shape, F32)
    acc_ref[...] = jnp.zeros(acc_ref.shape, F32)

    def attn_block(j, carry):
        c_blk = ckv_ref[pl.ds(pl.multiple_of(j * tk, tk), tk), :]
        c_blk_t = ckvt_ref[j]
        bias = lax.bitcast_convert_type(key_ref[j], F32)
        for hd in range(DSA_HEADS):
            lg = _dot_nt(c_blk, qabs_ref[:, hd * DSA_LATENT:(hd + 1) * DSA_LATENT]) + bias
            m_old = m_ref[hd:hd + 1, :]
            m_new = jnp.maximum(m_old, jnp.max(_sublane_fold(lg, jnp.maximum), axis=0, keepdims=True))
            p = jnp.exp2(lg - m_new)
            alpha = jnp.exp2(m_old - m_new)
            l_ref[hd:hd + 1, :] = alpha * l_ref[hd:hd + 1, :] + jnp.sum(_sublane_fold(p, jnp.add), axis=0,
                                                                         keepdims=True)
            acc_ref[hd] = alpha * acc_ref[hd] + _dot(c_blk_t, p.astype(BF16))
            m_ref[hd:hd + 1, :] = m_new
        return carry

    lax.fori_loop(0, n_kb, attn_block, 0)

    for hd in range(DSA_HEADS):
        o_lat = (acc_ref[hd] / l_ref[hd:hd + 1, :]).T.astype(BF16)
        out_ref[:, hd * DSA_HEAD_DIM:(hd + 1) * DSA_HEAD_DIM] = _dot(o_lat, wuv_ref[hd]).astype(BF16)


def _dsa(qi, wrows, qabs, kidx, ckv, wuv, tq, tk):
    T = qi.shape[0]
    topk = min(TOPK_MAX, T // 4)
    n_kb = T // tk
    ckvt = jnp.transpose(ckv.reshape(n_kb, tk, DSA_LATENT), (0, 2, 1))
    row = lambda n: pl.BlockSpec((tq, n), lambda i: (i, 0))
    body = functools.partial(_dsa_body, tq=tq, tk=tk, topk=topk, nbits_idx=int(T).bit_length())
    return pl.pallas_call(
        body,
        grid=(T // tq,),
        in_specs=[row(qi.shape[1]), pl.BlockSpec((wrows.shape[0], tq), lambda i: (0, i)), row(qabs.shape[1]),
                  _resident(kidx.shape), _resident(ckv.shape), _resident(ckvt.shape), _resident(wuv.shape)],
        out_specs=row(DSA_HEADS * DSA_HEAD_DIM),
        out_shape=jax.ShapeDtypeStruct((T, DSA_HEADS * DSA_HEAD_DIM), BF16),
        scratch_shapes=[pltpu.VMEM((n_kb, tk, tq), I32), pltpu.VMEM((n_kb, tk, tq), I16),
                        pltpu.VMEM((DSA_HEADS, tq), F32),
                        pltpu.VMEM((DSA_HEADS, tq), F32), pltpu.VMEM((DSA_HEADS, DSA_LATENT, tq), F32)],
        compiler_params=_params(),
        name="dsa",
    )(qi, wrows, qabs, kidx, ckv, ckvt, wuv)


def _log_sigmoid(v):
    return jnp.minimum(v, 0.0) - jnp.log1p(jnp.exp(-jnp.abs(v)))


def _chunk_cumsum(v, axis):
    pos = lax.broadcasted_iota(I32, v.shape, axis) & (CHUNK - 1)
    d = 1
    while d < CHUNK:
        v = v + jnp.where(pos >= d, pltpu.roll(v, d, axis=axis), 0.0)
        d *= 2
    return v


def _mlstm_body(mqk_ref, mv_ref, small_ref, gt_ref, mo_ref, cw_ref, cb_ref, gbc_ref, gbr_ref, ng_ref,
                out_ref, xe_ref, c_ref, n_ref, m_ref, hs_ref, *, rows):
    @pl.when(pl.program_id(0) == 0)
    def _():
        xe_ref[0:8, :] = jnp.zeros((8, xe_ref.shape[1]), F32)
        c_ref[...] = jnp.zeros(c_ref.shape, F32)
        n_ref[...] = jnp.zeros(n_ref.shape, F32)
        m_ref[...] = jnp.zeros(m_ref.shape, F32)

    x = mqk_ref[...]
    xe_ref[8:8 + rows, :] = x
    y = cb_ref[...]
    for j in range(CONV_W - 1):
        y = y + xe_ref[5 + j:5 + j + rows, :] * cw_ref[j:j + 1, :]
    y = y + x * cw_ref[CONV_W - 1:CONV_W, :]
    xe_ref[0:8, :] = x[rows - 8:rows, :]
    qk = y * jax.nn.sigmoid(y)
    nqk = ML_HEADS * ML_QK_DIM
    q_all = (qk[:, :nqk] * (ML_QK_DIM ** -0.5)).astype(BF16)
    k_all = qk[:, nqk:]

    g_col = small_ref[...] + gbc_ref[...]
    g_row = gt_ref[...] + gbr_ref[...]
    b_col = _chunk_cumsum(_log_sigmoid(g_col), 0)
    b_row = _chunk_cumsum(_log_sigmoid(g_row), 1)

    tri = lax.broadcasted_iota(I32, (CHUNK, CHUNK), 1) <= lax.broadcasted_iota(I32, (CHUNK, CHUNK), 0)

    for c in range(rows // CHUNK):
        lo, hi = c * CHUNK, (c + 1) * CHUNK
        for hd in range(ML_HEADS):
            bc = b_col[lo:hi, _S_MF + hd:_S_MF + hd + 1]
            lic = g_col[lo:hi, _S_MI + hd:_S_MI + hd + 1]
            br = b_row[ML_HEADS + hd:ML_HEADS + hd + 1, lo:hi]
            lir = g_row[hd:hd + 1, lo:hi]
            g_tot = bc[CHUNK - 1:CHUNK, :]
            m_prev = m_ref[hd][:, 0:1]

            dmat = jnp.where(tri, bc - br + lir, -jnp.inf)
            inter = bc + m_prev
            m_t = jnp.maximum(inter, jnp.max(dmat, axis=-1, keepdims=True))
            w_intra = jnp.exp(dmat - m_t)
            a_inter = jnp.exp(inter - m_t)

            qh = q_all[lo:hi, hd * ML_QK_DIM:(hd + 1) * ML_QK_DIM]
            kh = k_all[lo:hi, hd * ML_QK_DIM:(hd + 1) * ML_QK_DIM]
            vh = mv_ref[lo:hi, hd * ML_V_DIM:(hd + 1) * ML_V_DIM]
            s_qk = _dot_nt(qh, kh.astype(BF16)) * w_intra
            c_prev = c_ref[hd]
            n_prev = n_ref[hd]
            num = a_inter * _dot(qh, c_prev.astype(BF16)) + _dot(s_qk.astype(BF16), vh)
            den = (a_inter * jnp.sum(qh.astype(F32) * n_prev, axis=-1, keepdims=True)
                   + jnp.sum(s_qk, axis=-1, keepdims=True))
            hs_ref[lo:hi, hd * ML_V_DIM:(hd + 1) * ML_V_DIM] = (
                num / jnp.maximum(jnp.abs(den), jnp.exp(-m_t)))

            m_new = jnp.maximum(g_tot + m_prev, jnp.max(g_tot - br + lir, axis=-1, keepdims=True))
            a_state = jnp.exp(g_tot + m_prev - m_new)
            wk = jnp.exp(g_tot - bc + lic - m_new) * kh
            c_ref[hd] = a_state * c_prev + _dot(wk.T.astype(BF16), vh)
            n_ref[hd] = a_state * n_prev + jnp.sum(wk, axis=0, keepdims=True)
            m_ref[hd] = jnp.broadcast_to(m_new, m_ref.shape[1:])

    for hd in range(ML_HEADS):
        sl = slice(hd * ML_V_DIM, (hd + 1) * ML_V_DIM)
        out_ref[:, sl] = (_rms(hs_ref[:, sl], ng_ref[:, sl]) * jax.nn.sigmoid(mo_ref[:, sl])).astype(BF16)


def _mlstm(mqk, mv, small, gt, mo, cw, cb, gbc, gbr, ng, rows):
    T = mqk.shape[0]
    row = lambda n: pl.BlockSpec((rows, n), lambda i: (i, 0))
    nv = ML_HEADS * ML_V_DIM
    return pl.pallas_call(
        functools.partial(_mlstm_body, rows=rows),
        grid=(T // rows,),
        in_specs=[row(mqk.shape[1]), row(nv), row(_SMALL), pl.BlockSpec((8, rows), lambda i: (1, i)), row(nv),
                  _resident(cw.shape), _resident(cb.shape), _resident(gbc.shape), _resident(gbr.shape),
                  _resident(ng.shape)],
        out_specs=row(nv),
        out_shape=jax.ShapeDtypeStruct((T, nv), BF16),
        scratch_shapes=[pltpu.VMEM((rows + 8, mqk.shape[1]), F32),
                        pltpu.VMEM((ML_HEADS, ML_QK_DIM, ML_V_DIM), F32),
                        pltpu.VMEM((ML_HEADS, 1, ML_QK_DIM), F32),
                        pltpu.VMEM((ML_HEADS, 1, 128), F32),
                        pltpu.VMEM((rows, nv), F32)],
        compiler_params=_params(),
        name="mlstm",
    )(mqk, mv, small, gt, mo, cw, cb, gbc, gbr, ng)


def _memkv_body(mem_ref, g_ref, w_ref, out_ref):
    mn = _rms(mem_ref[...], g_ref[...]).astype(BF16)
    out_ref[...] = _dot(mn, w_ref[...].astype(BF16)).astype(BF16)


def _memkv(mem, g, w, tn):
    M, D = mem.shape
    N = w.shape[1]
    return pl.pallas_call(
        _memkv_body,
        grid=(N // tn,),
        in_specs=[_resident(mem.shape), _resident(g.shape), pl.BlockSpec((D, tn), lambda j: (0, j))],
        out_specs=pl.BlockSpec((M, tn), lambda j: (0, j)),
        out_shape=jax.ShapeDtypeStruct((M, N), BF16),
        compiler_params=_params(),
        name="memkv",
    )(mem, g, w)


def _mixout_body(x_ref, dsa_ref, ml_ref, kv_ref, wo_ref, wq_ref, wc_ref, wr_ref, gx_ref, gf_ref,
                 x2_ref, hf_ref, rl_ref, o_ref):
    nd = dsa_ref.shape[1]
    x1 = x_ref[...] + _dot(dsa_ref[...], wo_ref[0:nd, :]) + _dot(ml_ref[...], wo_ref[nd:, :])
    q = _dot(_rms(x1, gx_ref[...]).astype(BF16), wq_ref[...]).astype(BF16)
    for hd in range(X_HEADS):
        sl = slice(hd * X_HEAD_DIM, (hd + 1) * X_HEAD_DIM)
        lg = _dot_nt(q[:, sl], kv_ref[:, sl]) * (X_HEAD_DIM ** -0.5)
        e = jnp.exp(lg - jnp.max(lg, axis=-1, keepdims=True))
        p = e / jnp.sum(e, axis=-1, keepdims=True)
        v = kv_ref[:, D_MODEL + hd * X_HEAD_DIM:D_MODEL + (hd + 1) * X_HEAD_DIM]
        o_ref[:, sl] = _dot(p.astype(BF16), v).astype(BF16)
    x2 = x1 + _dot(o_ref[...], wc_ref[...])
    x2_ref[...] = x2
    hf = _rms(x2, gf_ref[...])
    hf_ref[...] = hf
    rl_ref[...] = _dot(hf.astype(BF16), wr_ref[...])


def _mixout(x, dsa, ml, kv, wo, wq, wc, wr, gx, gf, tm):
    T = x.shape[0]
    row = lambda n: pl.BlockSpec((tm, n), lambda i: (i, 0))
    return pl.pallas_call(
        _mixout_body,
        grid=(T // tm,),
        in_specs=[row(D_MODEL), row(dsa.shape[1]), row(ml.shape[1]), _resident(kv.shape), _resident(wo.shape),
                  _resident(wq.shape), _resident(wc.shape), _resident(wr.shape), _resident(gx.shape),
                  _resident(gf.shape)],
        out_specs=[row(D_MODEL), row(D_MODEL), row(wr.shape[1])],
        out_shape=[jax.ShapeDtypeStruct((T, D_MODEL), F32), jax.ShapeDtypeStruct((T, D_MODEL), F32),
                   jax.ShapeDtypeStruct((T, wr.shape[1]), F32)],
        scratch_shapes=[pltpu.VMEM((tm, D_MODEL), BF16)],
        compiler_params=_params(),
        name="mixout",
    )(x, dsa, ml, kv, wo, wq, wc, wr, gx, gf)


def _moe_body(tok_ref, be_ref, nv_ref, hf_hbm, wg_ref, wu_ref, wd_ref, y_ref,
              xbuf, wgb, wub, wdb, gsem, *, bm):
    i = pl.program_id(0)
    n_valid = nv_ref[0]
    slot = i % 2

    def gather(blk, s):
        def issue(g, carry):
            for u in range(_SUBLANES):
                tok = tok_ref[blk * bm + g * _SUBLANES + u]
                pltpu.make_async_copy(hf_hbm.at[pl.ds(tok, 1)], xbuf.at[s, g, pl.ds(u, 1)], gsem.at[s]).start()
            return carry
        lax.fori_loop(0, bm // _SUBLANES, issue, 0)

    def wait_gather(s):
        pltpu.make_async_copy(xbuf.at[s], xbuf.at[s], gsem.at[s]).wait()

    @pl.when(i < n_valid)
    def _():
        @pl.when(i == 0)
        def _():
            gather(0, 0)

        @pl.when(i + 1 < n_valid)
        def _():
            gather(i + 1, 1 - slot)

        @pl.when((i == 0) | (be_ref[i] != be_ref[jnp.maximum(i - 1, 0)]))
        def _():
            wgb[...] = wg_ref[0].astype(BF16)
            wub[...] = wu_ref[0].astype(BF16)
            wdb[...] = wd_ref[0].astype(BF16)

        wait_gather(slot)
        xb = xbuf[slot].reshape(bm, xbuf.shape[-1]).astype(BF16)
        gate = _dot(xb, wgb[...])
        a = gate * jax.nn.sigmoid(gate) * _dot(xb, wub[...])
        y_ref[...] = _dot(a.astype(BF16), wdb[...])

    @pl.when(i >= n_valid)
    def _():
        y_ref[...] = jnp.zeros(y_ref.shape, F32)


def _moe(tok, blk_e, n_valid, hf, wg, wu, wd, bm):
    n_blk = blk_e.shape[0]
    D = hf.shape[1]
    wspec = lambda shape: pl.BlockSpec((1,) + shape, lambda i, tok, be, nv: (be[i], 0, 0))
    grid_spec = pltpu.PrefetchScalarGridSpec(
        num_scalar_prefetch=3,
        grid=(n_blk,),
        in_specs=[pl.BlockSpec(memory_space=pl.ANY),
                  wspec((D, D_EXPERT)), wspec((D, D_EXPERT)), wspec((D_EXPERT, D))],
        out_specs=pl.BlockSpec((bm, D), lambda i, *_: (i, 0)),
        scratch_shapes=[pltpu.VMEM((2, bm // _SUBLANES, _SUBLANES, D), F32),
                        pltpu.VMEM((D, D_EXPERT), BF16), pltpu.VMEM((D, D_EXPERT), BF16),
                        pltpu.VMEM((D_EXPERT, D), BF16),
                        pltpu.SemaphoreType.DMA((2,))],
    )
    return pl.pallas_call(
        functools.partial(_moe_body, bm=bm),
        grid_spec=grid_spec,
        out_shape=jax.ShapeDtypeStruct((n_blk * bm, D), F32),
        compiler_params=_params(),
        name="moe",
    )(tok, blk_e, n_valid, hf, wg, wu, wd)


def _route(rl, b_group, b_router, bm):
    N = rl.shape[0]
    g_logits = rl[:, :N_GROUPS] + b_group
    g_prob = jax.nn.softmax(g_logits, axis=-1)
    g_sel = jnp.argmax(g_logits, axis=-1)
    p_g = jnp.take_along_axis(g_prob, g_sel[:, None], axis=-1)
    e_logits = (rl[:, N_GROUPS:N_GROUPS + N_EXPERTS] + b_router).reshape(N, N_GROUPS, EXP_PER_GROUP)
    e_logits = jnp.take_along_axis(e_logits, g_sel[:, None, None], axis=1)[:, 0]
    top_p, top_local = lax.top_k(jax.nn.softmax(e_logits, axis=-1), TOPK_IN_GROUP)
    gates = p_g * top_p / top_p.sum(-1, keepdims=True)
    expert_id = (g_sel[:, None] * EXP_PER_GROUP + top_local).astype(I32)

    A = N * TOPK_IN_GROUP
    flat_e = expert_id.reshape(A)
    onehot = (flat_e[:, None] == jnp.arange(N_EXPERTS, dtype=I32)[None, :])
    seg = _RANK_SEG if A % _RANK_SEG == 0 else A
    oh = onehot.astype(BF16).reshape(A // seg, seg, N_EXPERTS)
    before = (jnp.arange(seg)[None, :] < jnp.arange(seg)[:, None]).astype(BF16)
    within = jnp.einsum('ij,bjk->bik', before, oh, preferred_element_type=F32)
    seg_tot = jnp.sum(oh.astype(F32), axis=1)
    seg_base = jnp.cumsum(seg_tot, axis=0) - seg_tot
    rank = jnp.sum((within + seg_base[:, None, :]) * oh.astype(F32), axis=-1).reshape(A).astype(I32)
    counts = jnp.sum(seg_tot, axis=0).astype(I32)
    padded = (counts + bm - 1) // bm * bm
    pad_ends = jnp.cumsum(padded)
    pad_starts = pad_ends - padded
    row = jnp.sum(jnp.where(onehot, pad_starts[None, :], 0), axis=1) + rank
    n_blk = -(-A // bm) + N_EXPERTS
    row_tok = jnp.zeros((n_blk * bm,), I32).at[row].set(jnp.arange(A, dtype=I32) // TOPK_IN_GROUP)
    blk_lo = jnp.arange(n_blk, dtype=I32) * bm
    blk_e = jnp.minimum(jnp.searchsorted(pad_ends, blk_lo, side='right'), N_EXPERTS - 1).astype(I32)
    n_valid = (pad_ends[-1] // bm).astype(I32).reshape(1)
    return row_tok, row, gates, blk_e, n_valid


def _final_body(row_ref, x_ref, gate_ref, y_hbm, g_ref, out_ref, ybuf, sem, *, tm):
    i = pl.program_id(0)
    slot = i % 2

    def gather(tile, s):
        def issue(g, carry):
            for u in range(_SUBLANES):
                for k in range(TOPK_IN_GROUP):
                    src = row_ref[(tile * tm + g * _SUBLANES + u) * TOPK_IN_GROUP + k]
                    pltpu.make_async_copy(y_hbm.at[pl.ds(src, 1)], ybuf.at[s, k, g, pl.ds(u, 1)],
                                          sem.at[s]).start()
            return carry
        lax.fori_loop(0, tm // _SUBLANES, issue, 0)

    @pl.when(i == 0)
    def _():
        gather(0, 0)

    @pl.when(i + 1 < pl.num_programs(0))
    def _():
        gather(i + 1, 1 - slot)

    pltpu.make_async_copy(ybuf.at[slot], ybuf.at[slot], sem.at[slot]).wait()
    acc = x_ref[...]
    for k in range(TOPK_IN_GROUP):
        acc = acc + gate_ref[:, k:k + 1] * ybuf[slot, k].reshape(tm, ybuf.shape[-1])
    out_ref[...] = _rms(acc, g_ref[...])


def _final(row, x2, gates, y_rows, g, tm):
    T, D = x2.shape
    grid_spec = pltpu.PrefetchScalarGridSpec(
        num_scalar_prefetch=1,
        grid=(T // tm,),
        in_specs=[pl.BlockSpec((tm, D), lambda i, *_: (i, 0)),
                  pl.BlockSpec((tm, TOPK_IN_GROUP), lambda i, *_: (i, 0)),
                  pl.BlockSpec(memory_space=pl.ANY),
                  pl.BlockSpec(g.shape, lambda i, *_: (0, 0))],
        out_specs=pl.BlockSpec((tm, D), lambda i, *_: (i, 0)),
        scratch_shapes=[pltpu.VMEM((2, TOPK_IN_GROUP, tm // _SUBLANES, _SUBLANES, D), F32),
                        pltpu.SemaphoreType.DMA((2,))],
    )
    return pl.pallas_call(
        functools.partial(_final_body, tm=tm),
        grid_spec=grid_spec,
        out_shape=jax.ShapeDtypeStruct((T, D), F32),
        compiler_params=_params(),
        name="final",
    )(row, x2, gates, y_rows, g)


def _tile_sizes(T):
    pick = lambda want: want if T % want == 0 else CHUNK
    return dict(inproj=pick(256), dsa_q=pick(256), dsa_k=pick(512), mlstm=pick(256), mixout=pick(256),
                final=pick(256), moe=128)


def _layer(x, mem, norm_mix_g, w_in, kv_norm_g, k_idx_norm_g, w_uk, w_uv, conv_w, conv_b, gate_b, ml_norm_g,
           w_out, norm_x_g, mem_norm_g, w_cq, w_ckv, w_co, norm_ffn_g, w_group, b_group, w_router, b_router,
           w_gate, w_up, w_down, out_g):
    T = x.shape[0]
    ts = _tile_sizes(T)
    r2 = lambda v: v.reshape(1, -1)

    w_r = jnp.concatenate([
        w_in[:, _O_DQ:_O_MQ], w_in[:, _O_MI:_O_MO],
        jnp.zeros((D_MODEL, _SMALL - IDX_DIM - IDX_HEADS - 2 * ML_HEADS), w_in.dtype),
        w_in[:, _O_MQ:_O_MI], w_in[:, _O_MO:_O_END]], axis=1).astype(BF16)
    wuk_t = jnp.transpose(w_uk, (1, 2, 0)).astype(BF16)
    wuv_t = jnp.transpose(w_uv, (1, 0, 2)).astype(BF16)

    qabs, ckv, qi, kidx, small, mqk, mv, mo = _inproj(
        x, r2(norm_mix_g), w_r, wuk_t, r2(kv_norm_g), r2(k_idx_norm_g), ts["inproj"])

    gate_rows = jnp.transpose(small[:, _S_WI:_S_MF + ML_HEADS])
    dsa_out = _dsa(qi, gate_rows, qabs, kidx, ckv, wuv_t, ts["dsa_q"], ts["dsa_k"])

    gb_col = jnp.zeros((1, _SMALL), F32).at[0, _S_MI:_S_MI + 2 * ML_HEADS].set(gate_b)
    ml_out = _mlstm(mqk, mv, small, gate_rows, mo, conv_w, r2(conv_b), gb_col, gate_b.reshape(-1, 1),
                    r2(ml_norm_g), ts["mlstm"])

    kv = _memkv(mem, r2(mem_norm_g), w_ckv, 512)
    w_rt = jnp.concatenate([w_group, w_router,
                            jnp.zeros((D_MODEL, 128 - N_GROUPS - N_EXPERTS), w_group.dtype)], axis=1)
    x2, hf, rl = _mixout(x, dsa_out, ml_out, kv, w_out.astype(BF16), w_cq.astype(BF16), w_co.astype(BF16),
                         w_rt.astype(BF16), r2(norm_x_g), r2(norm_ffn_g), ts["mixout"])

    bm = ts["moe"]
    row_tok, row, gates, blk_e, n_valid = _route(rl, b_group, b_router, bm)
    y_rows = _moe(row_tok, blk_e, n_valid, hf, w_gate, w_up, w_down, bm)
    return _final(row, x2, gates, y_rows, r2(out_g), ts["final"])


def kernel(x, mem, norm_mix_g, w_in, kv_norm_g, k_idx_norm_g, w_uk, w_uv, conv_w, conv_b, gate_b, ml_norm_g,
           w_out, norm_x_g, mem_norm_g, w_cq, w_ckv, w_co, norm_ffn_g, w_group, b_group, w_router, b_router,
           w_gate, w_up, w_down, final_norm_g):
    B, T, D = x.shape
    assert B == 1 and D == D_MODEL and norm_mix_g.shape[0] == 1 and T % CHUNK == 0
    out = _layer(x[0], mem[0], norm_mix_g[0], w_in[0], kv_norm_g[0], k_idx_norm_g[0], w_uk[0], w_uv[0],
                 conv_w[0], conv_b[0], gate_b[0], ml_norm_g[0], w_out[0], norm_x_g[0], mem_norm_g[0],
                 w_cq[0], w_ckv[0], w_co[0], norm_ffn_g[0], w_group[0], b_group[0], w_router[0], b_router[0],
                 w_gate[0], w_up[0], w_down[0], final_norm_g)
    return out[None]
```

```python
import functools

import jax
import jax.numpy as jnp
import numpy as np
from jax import lax
from jax.experimental import pallas as pl
from jax.experimental.pallas import tpu as pltpu

F32 = jnp.float32
BF16 = jnp.bfloat16
I32 = jnp.int32
I16 = jnp.int16

EPS = 1e-6
CHUNK = 64
D_MODEL = 2048

DSA_HEADS = 8
DSA_HEAD_DIM = 128
DSA_LATENT = 256
IDX_HEADS = 8
IDX_DIM = 64
TOPK_MAX = 256

ML_HEADS = 4
ML_QK_DIM = 128
ML_V_DIM = 256
CONV_W = 4

X_HEADS = 4
X_HEAD_DIM = D_MODEL // X_HEADS

N_GROUPS = 4
EXP_PER_GROUP = 8
N_EXPERTS = N_GROUPS * EXP_PER_GROUP
TOPK_IN_GROUP = 2
D_EXPERT = 512

_O_DQ = 0
_O_CKV = _O_DQ + DSA_HEADS * DSA_HEAD_DIM
_O_QI = _O_CKV + DSA_LATENT
_O_KI = _O_QI + IDX_HEADS * IDX_DIM
_O_WI = _O_KI + IDX_DIM
_O_MQ = _O_WI + IDX_HEADS
_O_MK = _O_MQ + ML_HEADS * ML_QK_DIM
_O_MV = _O_MK + ML_HEADS * ML_QK_DIM
_O_MI = _O_MV + ML_HEADS * ML_V_DIM
_O_MF = _O_MI + ML_HEADS
_O_MO = _O_MF + ML_HEADS
_O_END = _O_MO + ML_HEADS * ML_V_DIM

_G_DQ = (0, 1024)
_G_CKV = (1024, 1280)
_G_QI = (1280, 1792)
_G_SMALL = (1792, 1920)
_G_MQK = (1920, 2944)
_G_MV = (2944, 3968)
_G_MO = (3968, 4992)
_W_COLS = 4992
_S_WI = IDX_DIM
_S_MI = _S_WI + IDX_HEADS
_S_MF = _S_MI + ML_HEADS
_SMALL = 128

_VMEM_LIMIT = 56 * 1024 * 1024
_INT_MIN = -(2 ** 31)
_I16_MIN = -(2 ** 15)
_CHUNK_SHIFT = CHUNK.bit_length() - 1
_LOG2E = 1.4426950408889634
_SUBLANES = 8
_RANK_SEG = 512
_NEG = -1e30


def _rms(v, g):
    return v * lax.rsqrt(jnp.mean(v * v, axis=-1, keepdims=True) + EPS) * g


def _dot(a, b):
    return jnp.dot(a, b, preferred_element_type=F32)


def _dot_nt(a, b):
    return lax.dot_general(a, b, (((1,), (1,)), ((), ())), preferred_element_type=F32)


def _resident(shape):
    nd = len(shape)
    return pl.BlockSpec(shape, lambda *_: (0,) * nd, pipeline_mode=pl.Buffered(1))


def _params(n_axes=1):
    return pltpu.CompilerParams(dimension_semantics=("arbitrary",) * n_axes,
                                vmem_limit_bytes=_VMEM_LIMIT)


def _inproj_body(x_ref, g_ref, w_ref, wuk_ref, kvg_ref, kig_ref,
                 qabs_ref, ckv_ref, qi_ref, kidx_ref, small_ref, mqk_ref, mv_ref, mo_ref):
    h = _rms(x_ref[...], g_ref[...]).astype(BF16)

    def proj(grp):
        return _dot(h, w_ref[:, grp[0]:grp[1]])

    dq = proj(_G_DQ)
    for hd in range(DSA_HEADS):
        qh = dq[:, hd * DSA_HEAD_DIM:(hd + 1) * DSA_HEAD_DIM].astype(BF16)
        qa = _dot(qh, wuk_ref[hd]) * (DSA_HEAD_DIM ** -0.5 * _LOG2E)
        qabs_ref[:, hd * DSA_LATENT:(hd + 1) * DSA_LATENT] = qa.astype(BF16)
    ckv_ref[...] = _rms(proj(_G_CKV), kvg_ref[...]).astype(BF16)
    qi_ref[...] = (proj(_G_QI) * (IDX_DIM ** -0.5)).astype(BF16)
    small = proj(_G_SMALL)
    small_ref[...] = small
    kidx_ref[...] = _rms(small[:, :IDX_DIM], kig_ref[...]).astype(BF16)
    mqk_ref[...] = proj(_G_MQK)
    mv_ref[...] = proj(_G_MV).astype(BF16)
    mo_ref[...] = proj(_G_MO)


def _inproj(x, g, w, wuk, kvg, kig, tm):
    T = x.shape[0]
    row = lambda n: pl.BlockSpec((tm, n), lambda i: (i, 0))
    outs = [(8 * DSA_LATENT, BF16), (DSA_LATENT, BF16), (IDX_HEADS * IDX_DIM, BF16), (IDX_DIM, BF16),
            (_SMALL, F32), (2 * ML_HEADS * ML_QK_DIM, F32), (ML_HEADS * ML_V_DIM, BF16),
            (ML_HEADS * ML_V_DIM, F32)]
    return pl.pallas_call(
        _inproj_body,
        grid=(T // tm,),
        in_specs=[row(D_MODEL), _resident(g.shape), _resident(w.shape), _resident(wuk.shape),
                  _resident(kvg.shape), _resident(kig.shape)],
        out_specs=[row(n) for n, _ in outs],
        out_shape=[jax.ShapeDtypeStruct((T, n), dt) for n, dt in outs],
        compiler_params=_params(),
        name="inproj",
    )(x, g, w, wuk, kvg, kig)


def _sublane_fold(v, op):
    acc = v[0:8, :]
    for r in range(1, v.shape[0] // 8):
        acc = op(acc, v[r * 8:(r + 1) * 8, :])
    return acc


def _dsa_body(qi_ref, wrow_ref, qabs_ref, kidx_ref, ckv_ref, ckvt_ref, wuv_ref, out_ref,
              key_ref, hi_ref, m_ref, l_ref, acc_ref, *, tq, tk, topk, nbits_idx):
    i = pl.program_id(0)
    n_kb = ((i + 1) * tq + tk - 1) // tk
    w_rows = wrow_ref[0:IDX_HEADS, :] * (IDX_HEADS ** -0.5)
    q_chunk = (i * tq + lax.broadcasted_iota(I32, (1, tq), 1)) >> _CHUNK_SHIFT

    def key_pos(j):
        return j * tk + lax.broadcasted_iota(I32, (tk, 1), 0)

    def score_block(j, carry):
        kx = kidx_ref[pl.ds(pl.multiple_of(j * tk, tk), tk), :]
        s = jnp.zeros((tk, tq), F32)
        for hd in range(IDX_HEADS):
            d = _dot_nt(kx, qi_ref[:, hd * IDX_DIM:(hd + 1) * IDX_DIM])
            s = s + w_rows[hd:hd + 1, :] * jnp.maximum(d, 0.0)
        bits = lax.bitcast_convert_type(s, I32)
        key = bits ^ ((bits >> 31) & 0x7FFFFFFF)
        key_ref[j] = jnp.where((key_pos(j) >> _CHUNK_SHIFT) <= q_chunk, key, _INT_MIN)
        return carry

    lax.fori_loop(0, n_kb, score_block, 0)

    def count(pred):
        def body(j, acc):
            hit = pred(key_ref[j], key_pos(j)).astype(I32)
            return acc + _sublane_fold(hit, jnp.add)
        acc = lax.fori_loop(0, n_kb, body, jnp.zeros((8, tq), I32))
        return jnp.sum(acc, axis=0, keepdims=True)

    def count16(ref, cand):
        c16 = cand.astype(I16)
        def body(j, acc):
            hit = jnp.where(ref[j] >= c16, jnp.int16(1), jnp.int16(0))
            part = hit[0:16, :]
            for r in range(1, tk // 16):
                part = part + hit[r * 16:(r + 1) * 16, :]
            return acc + part
        acc = lax.fori_loop(0, n_kb, body, jnp.zeros((16, tq), I16))
        return jnp.sum(acc.astype(I32), axis=0, keepdims=True)

    def kth_largest16(ref, kth):
        def bit(b, t):
            cand = t + lax.shift_left(jnp.int32(1), 15 - b)
            return jnp.where(count16(ref, cand) >= kth, cand, t)
        return lax.fori_loop(0, 16, bit, jnp.full((1, tq), _I16_MIN, I32))

    def split_block(j, carry):
        kb = key_ref[j]
        hi_ref[j] = (kb >> 16).astype(I16)
        return carry

    lax.fori_loop(0, n_kb, split_block, 0)
    t_hi = kth_largest16(hi_ref, topk)
    n_above = count16(hi_ref, t_hi + 1)

    def low_block(j, carry):
        kb = key_ref[j]
        lo = ((kb & 0xFFFF) + _I16_MIN).astype(I16)
        hi_ref[j] = jnp.where((kb >> 16) == t_hi, lo, jnp.int16(_I16_MIN))
        return carry

    lax.fori_loop(0, n_kb, low_block, 0)
    t_lo = kth_largest16(hi_ref, topk - n_above)
    t = lax.shift_left(t_hi, 16) + (t_lo - _I16_MIN)
    t = jnp.maximum(t, _INT_MIN + 1)
    n_ge = count(lambda kb, pos: kb >= t)
    n_gt = count(lambda kb, pos: kb > t)
    all_pos = jnp.int32(2 ** nbits_idx - 1)
    n_tie_take = jnp.where(n_ge > topk, topk - n_gt, all_pos)

    def tie_cutoff():
        def pos_bit(b, c):
            cand = c + lax.shift_left(jnp.int32(1), nbits_idx - 1 - b)
            f = count(lambda kb, pos: (kb == t) & (pos < cand))
            return jnp.where(f <= n_tie_take, cand, c)
        return lax.fori_loop(0, nbits_idx, pos_bit, jnp.zeros((1, tq), I32))

    cut = lax.cond(jnp.max(n_ge) > topk, tie_cutoff, lambda: jnp.full((1, tq), all_pos, I32))

    def bias_block(j, carry):
        kb = key_ref[j]
        sel = (kb > t) | ((kb == t) & (key_pos(j) < cut))
        key_ref[j] = lax.bitcast_convert_type(jnp.where(sel, 0.0, _NEG).astype(F32), I32)
        return carry

    lax.fori_loop(0, n_kb, bias_block, 0)

    m_ref[...] = jnp.full(m_ref.shape, _NEG, F32)
    l_ref[...] = jnp.zeros(l_ref.shape, F32)
    acc_ref[...] = jnp.zeros(acc_ref.shape, F32)

    def attn_block(j, carry):
        c_blk = ckv_ref[pl.ds(pl.multiple_of(j * tk, tk), tk), :]
        c_blk_t = ckvt_ref[j]
        bias = lax.bitcast_convert_type(key_ref[j], F32)
        for hd in range(DSA_HEADS):
            lg = _dot_nt(c_blk, qabs_ref[:, hd * DSA_LATENT:(hd + 1) * DSA_LATENT]) + bias
            m_old = m_ref[hd:hd + 1, :]
            m_new = jnp.maximum(m_old, jnp.max(_sublane_fold(lg, jnp.maximum), axis=0, keepdims=True))
            p = jnp.exp2(lg - m_new)
            alpha = jnp.exp2(m_old - m_new)
            l_ref[hd:hd + 1, :] = alpha * l_ref[hd:hd + 1, :] + jnp.sum(_sublane_fold(p, jnp.add), axis=0,
                                                                         keepdims=True)
            acc_ref[hd] = alpha * acc_ref[hd] + _dot(c_blk_t, p.astype(BF16))
            m_ref[hd:hd + 1, :] = m_new
        return carry

    lax.fori_loop(0, n_kb, attn_block, 0)

    for hd in range(DSA_HEADS):
        o_lat = (acc_ref[hd] / l_ref[hd:hd + 1, :]).T.astype(BF16)
        out_ref[:, hd * DSA_HEAD_DIM:(hd + 1) * DSA_HEAD_DIM] = _dot(o_lat, wuv_ref[hd]).astype(BF16)


def _dsa(qi, wrows, qabs, kidx, ckv, wuv, tq, tk):
    T = qi.shape[0]
    topk = min(TOPK_MAX, T // 4)
    n_kb = T // tk
    ckvt = jnp.transpose(ckv.reshape(n_kb, tk, DSA_LATENT), (0, 2, 1))
    row = lambda n: pl.BlockSpec((tq, n), lambda i: (i, 0))
    body = functools.partial(_dsa_body, tq=tq, tk=tk, topk=topk, nbits_idx=int(T).bit_length())
    return pl.pallas_call(
        body,
        grid=(T // tq,),
        in_specs=[row(qi.shape[1]), pl.BlockSpec((wrows.shape[0], tq), lambda i: (0, i)), row(qabs.shape[1]),
                  _resident(kidx.shape), _resident(ckv.shape), _resident(ckvt.shape), _resident(wuv.shape)],
        out_specs=row(DSA_HEADS * DSA_HEAD_DIM),
        out_shape=jax.ShapeDtypeStruct((T, DSA_HEADS * DSA_HEAD_DIM), BF16),
        scratch_shapes=[pltpu.VMEM((n_kb, tk, tq), I32), pltpu.VMEM((n_kb, tk, tq), I16),
                        pltpu.VMEM((DSA_HEADS, tq), F32),
                        pltpu.VMEM((DSA_HEADS, tq), F32), pltpu.VMEM((DSA_HEADS, DSA_LATENT, tq), F32)],
        compiler_params=_params(),
        name="dsa",
    )(qi, wrows, qabs, kidx, ckv, ckvt, wuv)


def _log_sigmoid(v):
    return jnp.minimum(v, 0.0) - jnp.log1p(jnp.exp(-jnp.abs(v)))


def _chunk_cumsum(v, axis):
    pos = lax.broadcasted_iota(I32, v.shape, axis) & (CHUNK - 1)
    d = 1
    while d < CHUNK:
        v = v + jnp.where(pos >= d, pltpu.roll(v, d, axis=axis), 0.0)
        d *= 2
    return v


def _mlstm_body(mqk_ref, mv_ref, small_ref, gt_ref, mo_ref, cw_ref, cb_ref, gbc_ref, gbr_ref, ng_ref,
                out_ref, xe_ref, c_ref, n_ref, m_ref, hs_ref, *, rows):
    @pl.when(pl.program_id(0) == 0)
    def _():
        xe_ref[0:8, :] = jnp.zeros((8, xe_ref.shape[1]), F32)
        c_ref[...] = jnp.zeros(c_ref.shape, F32)
        n_ref[...] = jnp.zeros(n_ref.shape, F32)
        m_ref[...] = jnp.zeros(m_ref.shape, F32)

    x = mqk_ref[...]
    xe_ref[8:8 + rows, :] = x
    y = cb_ref[...]
    for j in range(CONV_W - 1):
        y = y + xe_ref[5 + j:5 + j + rows, :] * cw_ref[j:j + 1, :]
    y = y + x * cw_ref[CONV_W - 1:CONV_W, :]
    xe_ref[0:8, :] = x[rows - 8:rows, :]
    qk = y * jax.nn.sigmoid(y)
    nqk = ML_HEADS * ML_QK_DIM
    q_all = (qk[:, :nqk] * (ML_QK_DIM ** -0.5)).astype(BF16)
    k_all = qk[:, nqk:]

    g_col = small_ref[...] + gbc_ref[...]
    g_row = gt_ref[...] + gbr_ref[...]
    b_col = _chunk_cumsum(_log_sigmoid(g_col), 0)
    b_row = _chunk_cumsum(_log_sigmoid(g_row), 1)

    tri = lax.broadcasted_iota(I32, (CHUNK, CHUNK), 1) <= lax.broadcasted_iota(I32, (CHUNK, CHUNK), 0)

    for c in range(rows // CHUNK):
        lo, hi = c * CHUNK, (c + 1) * CHUNK
        for hd in range(ML_HEADS):
            bc = b_col[lo:hi, _S_MF + hd:_S_MF + hd + 1]
            lic = g_col[lo:hi, _S_MI + hd:_S_MI + hd + 1]
            br = b_row[ML_HEADS + hd:ML_HEADS + hd + 1, lo:hi]
            lir = g_row[hd:hd + 1, lo:hi]
            g_tot = bc[CHUNK - 1:CHUNK, :]
            m_prev = m_ref[hd][:, 0:1]

            dmat = jnp.where(tri, bc - br + lir, -jnp.inf)
            inter = bc + m_prev
            m_t = jnp.maximum(inter, jnp.max(dmat, axis=-1, keepdims=True))
            w_intra = jnp.exp(dmat - m_t)
            a_inter = jnp.exp(inter - m_t)

            qh = q_all[lo:hi, hd * ML_QK_DIM:(hd + 1) * ML_QK_DIM]
            kh = k_all[lo:hi, hd * ML_QK_DIM:(hd + 1) * ML_QK_DIM]
            vh = mv_ref[lo:hi, hd * ML_V_DIM:(hd + 1) * ML_V_DIM]
            s_qk = _dot_nt(qh, kh.astype(BF16)) * w_intra
            c_prev = c_ref[hd]
            n_prev = n_ref[hd]
            num = a_inter * _dot(qh, c_prev.astype(BF16)) + _dot(s_qk.astype(BF16), vh)
            den = (a_inter * jnp.sum(qh.astype(F32) * n_prev, axis=-1, keepdims=True)
                   + jnp.sum(s_qk, axis=-1, keepdims=True))
            hs_ref[lo:hi, hd * ML_V_DIM:(hd + 1) * ML_V_DIM] = (
                num / jnp.maximum(jnp.abs(den), jnp.exp(-m_t)))

            m_new = jnp.maximum(g_tot + m_prev, jnp.max(g_tot - br + lir, axis=-1, keepdims=True))
            a_state = jnp.exp(g_tot + m_prev - m_new)
            wk = jnp.exp(g_tot - bc + lic - m_new) * kh
            c_ref[hd] = a_state * c_prev + _dot(wk.T.astype(BF16), vh)
            n_ref[hd] = a_state * n_prev + jnp.sum(wk, axis=0, keepdims=True)
            m_ref[hd] = jnp.broadcast_to(m_new, m_ref.shape[1:])

    for hd in range(ML_HEADS):
        sl = slice(hd * ML_V_DIM, (hd + 1) * ML_V_DIM)
        out_ref[:, sl] = (_rms(hs_ref[:, sl], ng_ref[:, sl]) * jax.nn.sigmoid(mo_ref[:, sl])).astype(BF16)


def _mlstm(mqk, mv, small, gt, mo, cw, cb, gbc, gbr, ng, rows):
    T = mqk.shape[0]
    row = lambda n: pl.BlockSpec((rows, n), lambda i: (i, 0))
    nv = ML_HEADS * ML_V_DIM
    return pl.pallas_call(
        functools.partial(_mlstm_body, rows=rows),
        grid=(T // rows,),
        in_specs=[row(mqk.shape[1]), row(nv), row(_SMALL), pl.BlockSpec((8, rows), lambda i: (1, i)), row(nv),
                  _resident(cw.shape), _resident(cb.shape), _resident(gbc.shape), _resident(gbr.shape),
                  _resident(ng.shape)],
        out_specs=row(nv),
        out_shape=jax.ShapeDtypeStruct((T, nv), BF16),
        scratch_shapes=[pltpu.VMEM((rows + 8, mqk.shape[1]), F32),
                        pltpu.VMEM((ML_HEADS, ML_QK_DIM, ML_V_DIM), F32),
                        pltpu.VMEM((ML_HEADS, 1, ML_QK_DIM), F32),
                        pltpu.VMEM((ML_HEADS, 1, 128), F32),
                        pltpu.VMEM((rows, nv), F32)],
        compiler_params=_params(),
        name="mlstm",
    )(mqk, mv, small, gt, mo, cw, cb, gbc, gbr, ng)


def _memkv_body(mem_ref, g_ref, w_ref, out_ref):
    mn = _rms(mem_ref[...], g_ref[...]).astype(BF16)
    out_ref[...] = _dot(mn, w_ref[...].astype(BF16)).astype(BF16)


def _memkv(mem, g, w, tn):
    M, D = mem.shape
    N = w.shape[1]
    return pl.pallas_call(
        _memkv_body,
        grid=(N // tn,),
        in_specs=[_resident(mem.shape), _resident(g.shape), pl.BlockSpec((D, tn), lambda j: (0, j))],
        out_specs=pl.BlockSpec((M, tn), lambda j: (0, j)),
        out_shape=jax.ShapeDtypeStruct((M, N), BF16),
        compiler_params=_params(),
        name="memkv",
    )(mem, g, w)


def _mixout_body(x_ref, dsa_ref, ml_ref, kv_ref, wo_ref, wq_ref, wc_ref, wr_ref, gx_ref, gf_ref,
                 x2_ref, hf_ref, rl_ref, o_ref):
    nd = dsa_ref.shape[1]
    x1 = x_ref[...] + _dot(dsa_ref[...], wo_ref[0:nd, :]) + _dot(ml_ref[...], wo_ref[nd:, :])
    q = _dot(_rms(x1, gx_ref[...]).astype(BF16), wq_ref[...]).astype(BF16)
    for hd in range(X_HEADS):
        sl = slice(hd * X_HEAD_DIM, (hd + 1) * X_HEAD_DIM)
        lg = _dot_nt(q[:, sl], kv_ref[:, sl]) * (X_HEAD_DIM ** -0.5)
        e = jnp.exp(lg - jnp.max(lg, axis=-1, keepdims=True))
        p = e / jnp.sum(e, axis=-1, keepdims=True)
        v = kv_ref[:, D_MODEL + hd * X_HEAD_DIM:D_MODEL + (hd + 1) * X_HEAD_DIM]
        o_ref[:, sl] = _dot(p.astype(BF16), v).astype(BF16)
    x2 = x1 + _dot(o_ref[...], wc_ref[...])
    x2_ref[...] = x2
    hf = _rms(x2, gf_ref[...])
    hf_ref[...] = hf
    rl_ref[...] = _dot(hf.astype(BF16), wr_ref[...])


def _mixout(x, dsa, ml, kv, wo, wq, wc, wr, gx, gf, tm):
    T = x.shape[0]
    row = lambda n: pl.BlockSpec((tm, n), lambda i: (i, 0))
    return pl.pallas_call(
        _mixout_body,
        grid=(T // tm,),
        in_specs=[row(D_MODEL), row(dsa.shape[1]), row(ml.shape[1]), _resident(kv.shape), _resident(wo.shape),
                  _resident(wq.shape), _resident(wc.shape), _resident(wr.shape), _resident(gx.shape),
                  _resident(gf.shape)],
        out_specs=[row(D_MODEL), row(D_MODEL), row(wr.shape[1])],
        out_shape=[jax.ShapeDtypeStruct((T, D_MODEL), F32), jax.ShapeDtypeStruct((T, D_MODEL), F32),
                   jax.ShapeDtypeStruct((T, wr.shape[1]), F32)],
        scratch_shapes=[pltpu.VMEM((tm, D_MODEL), BF16)],
        compiler_params=_params(),
        name="mixout",
    )(x, dsa, ml, kv, wo, wq, wc, wr, gx, gf)


def _moe_body(tok_ref, eb_ref, hf_hbm, wg_ref, wu_ref, wd_ref, y_hbm,
              xbuf, ybuf, wgb, wub, wdb, gsem, ysem, *, bm, n_blk):
    e = pl.program_id(0)
    n_valid = eb_ref[N_EXPERTS]
    b_lo = eb_ref[e]
    b_hi = eb_ref[e + 1]

    def gather(blk, s):
        def issue(g, carry):
            for u in range(_SUBLANES):
                tok = tok_ref[blk * bm + g * _SUBLANES + u]
                pltpu.make_async_copy(hf_hbm.at[pl.ds(tok, 1)], xbuf.at[s, g, pl.ds(u, 1)], gsem.at[s]).start()
            return carry
        lax.fori_loop(0, bm // _SUBLANES, issue, 0)

    def wait_gather(s):
        pltpu.make_async_copy(xbuf.at[s], xbuf.at[s], gsem.at[s]).wait()

    def y_copy(blk, s):
        return pltpu.make_async_copy(ybuf.at[s], y_hbm.at[pl.ds(pl.multiple_of(blk * bm, bm), bm)], ysem.at[s])

    @pl.when(e == 0)
    def _():
        gather(0, 0)

    @pl.when(b_hi > b_lo)
    def _():
        wgb[...] = wg_ref[0].astype(BF16)
        wub[...] = wu_ref[0].astype(BF16)
        wdb[...] = wd_ref[0].astype(BF16)

        def block(b, carry):
            s = b % 2

            @pl.when(b + 1 < n_valid)
            def _():
                gather(b + 1, 1 - s)

            wait_gather(s)
            xb = xbuf[s].reshape(bm, xbuf.shape[-1]).astype(BF16)
            gate = _dot(xb, wgb[...])
            a = gate * jax.nn.sigmoid(gate) * _dot(xb, wub[...])
            y = _dot(a.astype(BF16), wdb[...])

            @pl.when(b >= 2)
            def _():
                y_copy(b - 2, s).wait()

            ybuf[s] = y
            y_copy(b, s).start()
            return carry

        lax.fori_loop(b_lo, b_hi, block, 0)

    @pl.when(e == pl.num_programs(0) - 1)
    def _():
        @pl.when(n_valid >= 2)
        def _():
            y_copy(n_valid - 2, n_valid % 2).wait()
        y_copy(n_valid - 1, (n_valid - 1) % 2).wait()
        ybuf[0] = jnp.zeros(ybuf.shape[1:], F32)

        def zero_block(b, carry):
            cp = y_copy(b, 0)
            cp.start()
            cp.wait()
            return carry

        lax.fori_loop(n_valid, n_blk, zero_block, 0)


def _moe(tok, e_blk, hf, wg, wu, wd, n_blk, bm):
    D = hf.shape[1]
    wspec = lambda shape: pl.BlockSpec((1,) + shape, lambda e, *_: (e, 0, 0))
    grid_spec = pltpu.PrefetchScalarGridSpec(
        num_scalar_prefetch=2,
        grid=(N_EXPERTS,),
        in_specs=[pl.BlockSpec(memory_space=pl.ANY),
                  wspec((D, D_EXPERT)), wspec((D, D_EXPERT)), wspec((D_EXPERT, D))],
        out_specs=pl.BlockSpec(memory_space=pl.ANY),
        scratch_shapes=[pltpu.VMEM((2, bm // _SUBLANES, _SUBLANES, D), F32), pltpu.VMEM((2, bm, D), F32),
                        pltpu.VMEM((D, D_EXPERT), BF16), pltpu.VMEM((D, D_EXPERT), BF16),
                        pltpu.VMEM((D_EXPERT, D), BF16),
                        pltpu.SemaphoreType.DMA((2,)), pltpu.SemaphoreType.DMA((2,))],
    )
    return pl.pallas_call(
        functools.partial(_moe_body, bm=bm, n_blk=n_blk),
        grid_spec=grid_spec,
        out_shape=jax.ShapeDtypeStruct((n_blk * bm, D), F32),
        compiler_params=_params(),
        name="moe",
    )(tok, e_blk, hf, wg, wu, wd)


def _route(rl, b_group, b_router, bm):
    N = rl.shape[0]
    g_logits = rl[:, :N_GROUPS] + b_group
    g_prob = jax.nn.softmax(g_logits, axis=-1)
    g_sel = jnp.argmax(g_logits, axis=-1)
    p_g = jnp.take_along_axis(g_prob, g_sel[:, None], axis=-1)
    e_logits = (rl[:, N_GROUPS:N_GROUPS + N_EXPERTS] + b_router).reshape(N, N_GROUPS, EXP_PER_GROUP)
    e_logits = jnp.take_along_axis(e_logits, g_sel[:, None, None], axis=1)[:, 0]
    top_p, top_local = lax.top_k(jax.nn.softmax(e_logits, axis=-1), TOPK_IN_GROUP)
    gates = p_g * top_p / top_p.sum(-1, keepdims=True)
    expert_id = (g_sel[:, None] * EXP_PER_GROUP + top_local).astype(I32)

    A = N * TOPK_IN_GROUP
    flat_e = expert_id.reshape(A)
    onehot = (flat_e[:, None] == jnp.arange(N_EXPERTS, dtype=I32)[None, :])
    seg = _RANK_SEG if A % _RANK_SEG == 0 else A
    oh = onehot.astype(BF16).reshape(A // seg, seg, N_EXPERTS)
    before = (jnp.arange(seg)[None, :] < jnp.arange(seg)[:, None]).astype(BF16)
    within = jnp.einsum('ij,bjk->bik', before, oh, preferred_element_type=F32)
    seg_tot = jnp.sum(oh.astype(F32), axis=1)
    seg_base = jnp.cumsum(seg_tot, axis=0) - seg_tot
    rank = jnp.sum((within + seg_base[:, None, :]) * oh.astype(F32), axis=-1).reshape(A).astype(I32)
    counts = jnp.sum(seg_tot, axis=0).astype(I32)
    padded = (counts + bm - 1) // bm * bm
    pad_ends = jnp.cumsum(padded)
    pad_starts = pad_ends - padded
    row = jnp.sum(jnp.where(onehot, pad_starts[None, :], 0), axis=1) + rank
    n_blk = -(-A // bm) + N_EXPERTS
    row_tok = jnp.zeros((n_blk * bm,), I32).at[row].set(jnp.arange(A, dtype=I32) // TOPK_IN_GROUP)
    e_blk = jnp.concatenate([pad_starts, pad_ends[-1:]]) // bm
    return row_tok, row, gates, e_blk.astype(I32), n_blk


def _final_body(row_ref, x_ref, gate_ref, y_hbm, g_ref, out_ref, ybuf, sem, *, tm):
    i = pl.program_id(0)
    slot = i % 2

    def gather(tile, s):
        def issue(g, carry):
            for u in range(_SUBLANES):
                for k in range(TOPK_IN_GROUP):
                    src = row_ref[(tile * tm + g * _SUBLANES + u) * TOPK_IN_GROUP + k]
                    pltpu.make_async_copy(y_hbm.at[pl.ds(src, 1)], ybuf.at[s, k, g, pl.ds(u, 1)],
                                          sem.at[s]).start()
            return carry
        lax.fori_loop(0, tm // _SUBLANES, issue, 0)

    @pl.when(i == 0)
    def _():
        gather(0, 0)

    @pl.when(i + 1 < pl.num_programs(0))
    def _():
        gather(i + 1, 1 - slot)

    pltpu.make_async_copy(ybuf.at[slot], ybuf.at[slot], sem.at[slot]).wait()
    acc = x_ref[...]
    for k in range(TOPK_IN_GROUP):
        acc = acc + gate_ref[:, k:k + 1] * ybuf[slot, k].reshape(tm, ybuf.shape[-1])
    out_ref[...] = _rms(acc, g_ref[...])


def _final(row, x2, gates, y_rows, g, tm):
    T, D = x2.shape
    grid_spec = pltpu.PrefetchScalarGridSpec(
        num_scalar_prefetch=1,
        grid=(T // tm,),
        in_specs=[pl.BlockSpec((tm, D), lambda i, *_: (i, 0)),
                  pl.BlockSpec((tm, TOPK_IN_GROUP), lambda i, *_: (i, 0)),
                  pl.BlockSpec(memory_space=pl.ANY),
                  pl.BlockSpec(g.shape, lambda i, *_: (0, 0))],
        out_specs=pl.BlockSpec((tm, D), lambda i, *_: (i, 0)),
        scratch_shapes=[pltpu.VMEM((2, TOPK_IN_GROUP, tm // _SUBLANES, _SUBLANES, D), F32),
                        pltpu.SemaphoreType.DMA((2,))],
    )
    return pl.pallas_call(
        functools.partial(_final_body, tm=tm),
        grid_spec=grid_spec,
        out_shape=jax.ShapeDtypeStruct((T, D), F32),
        compiler_params=_params(),
        name="final",
    )(row, x2, gates, y_rows, g)


def _tile_sizes(T):
    pick = lambda want: want if T % want == 0 else CHUNK
    return dict(inproj=pick(256), dsa_q=pick(256), dsa_k=pick(512), mlstm=pick(256), mixout=pick(256),
                final=pick(256), moe=128)


def _layer(x, mem, norm_mix_g, w_in, kv_norm_g, k_idx_norm_g, w_uk, w_uv, conv_w, conv_b, gate_b, ml_norm_g,
           w_out, norm_x_g, mem_norm_g, w_cq, w_ckv, w_co, norm_ffn_g, w_group, b_group, w_router, b_router,
           w_gate, w_up, w_down, out_g):
    T = x.shape[0]
    ts = _tile_sizes(T)
    r2 = lambda v: v.reshape(1, -1)

    w_r = jnp.concatenate([
        w_in[:, _O_DQ:_O_MQ], w_in[:, _O_MI:_O_MO],
        jnp.zeros((D_MODEL, _SMALL - IDX_DIM - IDX_HEADS - 2 * ML_HEADS), w_in.dtype),
        w_in[:, _O_MQ:_O_MI], w_in[:, _O_MO:_O_END]], axis=1).astype(BF16)
    wuk_t = jnp.transpose(w_uk, (1, 2, 0)).astype(BF16)
    wuv_t = jnp.transpose(w_uv, (1, 0, 2)).astype(BF16)

    qabs, ckv, qi, kidx, small, mqk, mv, mo = _inproj(
        x, r2(norm_mix_g), w_r, wuk_t, r2(kv_norm_g), r2(k_idx_norm_g), ts["inproj"])

    gate_rows = jnp.transpose(small[:, _S_WI:_S_MF + ML_HEADS])
    dsa_out = _dsa(qi, gate_rows, qabs, kidx, ckv, wuv_t, ts["dsa_q"], ts["dsa_k"])

    gb_col = jnp.zeros((1, _SMALL), F32).at[0, _S_MI:_S_MI + 2 * ML_HEADS].set(gate_b)
    ml_out = _mlstm(mqk, mv, small, gate_rows, mo, conv_w, r2(conv_b), gb_col, gate_b.reshape(-1, 1),
                    r2(ml_norm_g), ts["mlstm"])

    kv = _memkv(mem, r2(mem_norm_g), w_ckv, 512)
    w_rt = jnp.concatenate([w_group, w_router,
                            jnp.zeros((D_MODEL, 128 - N_GROUPS - N_EXPERTS), w_group.dtype)], axis=1)
    x2, hf, rl = _mixout(x, dsa_out, ml_out, kv, w_out.astype(BF16), w_cq.astype(BF16), w_co.astype(BF16),
                         w_rt.astype(BF16), r2(norm_x_g), r2(norm_ffn_g), ts["mixout"])

    bm = ts["moe"]
    row_tok, row, gates, e_blk, n_blk = _route(rl, b_group, b_router, bm)
    y_rows = _moe(row_tok, e_blk, hf, w_gate, w_up, w_down, n_blk, bm)
    return _final(row, x2, gates, y_rows, r2(out_g), ts["final"])


def kernel(x, mem, norm_mix_g, w_in, kv_norm_g, k_idx_norm_g, w_uk, w_uv, conv_w, conv_b, gate_b, ml_norm_g,
           w_out, norm_x_g, mem_norm_g, w_cq, w_ckv, w_co, norm_ffn_g, w_group, b_group, w_router, b_router,
           w_gate, w_up, w_down, final_norm_g):
    B, T, D = x.shape
    assert B == 1 and D == D_MODEL and norm_mix_g.shape[0] == 1 and T % CHUNK == 0
    out = _layer(x[0], mem[0], norm_mix_g[0], w_in[0], kv_norm_g[0], k_idx_norm_g[0], w_uk[0], w_uv[0],
                 conv_w[0], conv_b[0], gate_b[0], ml_norm_g[0], w_out[0], norm_x_g[0], mem_norm_g[0],
                 w_cq[0], w_ckv[0], w_co[0], norm_ffn_g[0], w_group[0], b_group[0], w_router[0], b_router[0],
                 w_gate[0], w_up[0], w_down[0], final_norm_g)
    return out[None]
```

```python
import functools

import jax
import jax.numpy as jnp
import numpy as np
from jax import lax
from jax.experimental import pallas as pl
from jax.experimental.pallas import tpu as pltpu

F32 = jnp.float32
BF16 = jnp.bfloat16
I32 = jnp.int32
I16 = jnp.int16

EPS = 1e-6
CHUNK = 64
D_MODEL = 2048

DSA_HEADS = 8
DSA_HEAD_DIM = 128
DSA_LATENT = 256
IDX_HEADS = 8
IDX_DIM = 64
TOPK_MAX = 256

ML_HEADS = 4
ML_QK_DIM = 128
ML_V_DIM = 256
CONV_W = 4

X_HEADS = 4
X_HEAD_DIM = D_MODEL // X_HEADS

N_GROUPS = 4
EXP_PER_GROUP = 8
N_EXPERTS = N_GROUPS * EXP_PER_GROUP
TOPK_IN_GROUP = 2
D_EXPERT = 512

_O_DQ = 0
_O_CKV = _O_DQ + DSA_HEADS * DSA_HEAD_DIM
_O_QI = _O_CKV + DSA_LATENT
_O_KI = _O_QI + IDX_HEADS * IDX_DIM
_O_WI = _O_KI + IDX_DIM
_O_MQ = _O_WI + IDX_HEADS
_O_MK = _O_MQ + ML_HEADS * ML_QK_DIM
_O_MV = _O_MK + ML_HEADS * ML_QK_DIM
_O_MI = _O_MV + ML_HEADS * ML_V_DIM
_O_MF = _O_MI + ML_HEADS
_O_MO = _O_MF + ML_HEADS
_O_END = _O_MO + ML_HEADS * ML_V_DIM

_G_DQ = (0, 1024)
_G_CKV = (1024, 1280)
_G_QI = (1280, 1792)
_G_SMALL = (1792, 1920)
_G_MQK = (1920, 2944)
_G_MV = (2944, 3968)
_G_MO = (3968, 4992)
_W_COLS = 4992
_S_WI = IDX_DIM
_S_MI = _S_WI + IDX_HEADS
_S_MF = _S_MI + ML_HEADS
_SMALL = 128

_VMEM_LIMIT = 56 * 1024 * 1024
_INT_MIN = -(2 ** 31)
_I16_MIN = -(2 ** 15)
_CHUNK_SHIFT = CHUNK.bit_length() - 1
_LOG2E = 1.4426950408889634
_SUBLANES = 8
_GATHER_SLOTS = 3
_RANK_SEG = 512
_NEG = -1e30


def _rms(v, g):
    return v * lax.rsqrt(jnp.mean(v * v, axis=-1, keepdims=True) + EPS) * g


def _dot(a, b):
    return jnp.dot(a, b, preferred_element_type=F32)


def _dot_nt(a, b):
    return lax.dot_general(a, b, (((1,), (1,)), ((), ())), preferred_element_type=F32)


def _resident(shape):
    nd = len(shape)
    return pl.BlockSpec(shape, lambda *_: (0,) * nd, pipeline_mode=pl.Buffered(1))


def _params(n_axes=1):
    return pltpu.CompilerParams(dimension_semantics=("arbitrary",) * n_axes,
                                vmem_limit_bytes=_VMEM_LIMIT)


def _inproj_body(x_ref, g_ref, w_ref, wuk_ref, kvg_ref, kig_ref,
                 qabs_ref, ckv_ref, qi_ref, kidx_ref, small_ref, mqk_ref, mv_ref, mo_ref):
    h = _rms(x_ref[...], g_ref[...]).astype(BF16)

    def proj(grp):
        return _dot(h, w_ref[:, grp[0]:grp[1]])

    dq = proj(_G_DQ)
    for hd in range(DSA_HEADS):
        qh = dq[:, hd * DSA_HEAD_DIM:(hd + 1) * DSA_HEAD_DIM].astype(BF16)
        qa = _dot(qh, wuk_ref[hd]) * (DSA_HEAD_DIM ** -0.5 * _LOG2E)
        qabs_ref[:, hd * DSA_LATENT:(hd + 1) * DSA_LATENT] = qa.astype(BF16)
    ckv_ref[...] = _rms(proj(_G_CKV), kvg_ref[...]).astype(BF16)
    qi_ref[...] = (proj(_G_QI) * (IDX_DIM ** -0.5)).astype(BF16)
    small = proj(_G_SMALL)
    small_ref[...] = small
    kidx_ref[...] = _rms(small[:, :IDX_DIM], kig_ref[...]).astype(BF16)
    mqk_ref[...] = proj(_G_MQK)
    mv_ref[...] = proj(_G_MV).astype(BF16)
    mo_ref[...] = proj(_G_MO)


def _inproj(x, g, w, wuk, kvg, kig, tm):
    T = x.shape[0]
    row = lambda n: pl.BlockSpec((tm, n), lambda i: (i, 0))
    outs = [(8 * DSA_LATENT, BF16), (DSA_LATENT, BF16), (IDX_HEADS * IDX_DIM, BF16), (IDX_DIM, BF16),
            (_SMALL, F32), (2 * ML_HEADS * ML_QK_DIM, F32), (ML_HEADS * ML_V_DIM, BF16),
            (ML_HEADS * ML_V_DIM, F32)]
    return pl.pallas_call(
        _inproj_body,
        grid=(T // tm,),
        in_specs=[row(D_MODEL), _resident(g.shape), _resident(w.shape), _resident(wuk.shape),
                  _resident(kvg.shape), _resident(kig.shape)],
        out_specs=[row(n) for n, _ in outs],
        out_shape=[jax.ShapeDtypeStruct((T, n), dt) for n, dt in outs],
        compiler_params=_params(),
        name="inproj",
    )(x, g, w, wuk, kvg, kig)


def _sublane_fold(v, op):
    acc = v[0:8, :]
    for r in range(1, v.shape[0] // 8):
        acc = op(acc, v[r * 8:(r + 1) * 8, :])
    return acc


def _dsa_body(qi_ref, wrow_ref, qabs_ref, kidx_ref, ckv_ref, ckvt_ref, wuv_ref, out_ref,
              key_ref, hi_ref, m_ref, l_ref, acc_ref, *, tq, tk, topk, nbits_idx):
    i = pl.program_id(0)
    n_kb = ((i + 1) * tq + tk - 1) // tk
    w_rows = wrow_ref[0:IDX_HEADS, :] * (IDX_HEADS ** -0.5)
    q_chunk = (i * tq + lax.broadcasted_iota(I32, (1, tq), 1)) >> _CHUNK_SHIFT

    def key_pos(j):
        return j * tk + lax.broadcasted_iota(I32, (tk, 1), 0)

    def score_block(j, carry):
        kx = kidx_ref[pl.ds(pl.multiple_of(j * tk, tk), tk), :]
        s = jnp.zeros((tk, tq), F32)
        for hd in range(IDX_HEADS):
            d = _dot_nt(kx, qi_ref[:, hd * IDX_DIM:(hd + 1) * IDX_DIM])
            s = s + w_rows[hd:hd + 1, :] * jnp.maximum(d, 0.0)
        bits = lax.bitcast_convert_type(s, I32)
        key = bits ^ ((bits >> 31) & 0x7FFFFFFF)
        key_ref[j] = jnp.where((key_pos(j) >> _CHUNK_SHIFT) <= q_chunk, key, _INT_MIN)
        return carry

    lax.fori_loop(0, n_kb, score_block, 0)

    def count(pred):
        def body(j, acc):
            hit = pred(key_ref[j], key_pos(j)).astype(I32)
            return acc + _sublane_fold(hit, jnp.add)
        acc = lax.fori_loop(0, n_kb, body, jnp.zeros((8, tq), I32))
        return jnp.sum(acc, axis=0, keepdims=True)

    def count16(ref, cand):
        c16 = cand.astype(I16)
        def body(j, acc):
            hit = jnp.where(ref[j] >= c16, jnp.int16(1), jnp.int16(0))
            part = hit[0:16, :]
            for r in range(1, tk // 16):
                part = part + hit[r * 16:(r + 1) * 16, :]
            return acc + part
        acc = lax.fori_loop(0, n_kb, body, jnp.zeros((16, tq), I16))
        return jnp.sum(acc.astype(I32), axis=0, keepdims=True)

    def kth_largest16(ref, kth):
        def bit(b, t):
            cand = t + lax.shift_left(jnp.int32(1), 15 - b)
            return jnp.where(count16(ref, cand) >= kth, cand, t)
        return lax.fori_loop(0, 16, bit, jnp.full((1, tq), _I16_MIN, I32))

    def split_block(j, carry):
        kb = key_ref[j]
        hi_ref[j] = (kb >> 16).astype(I16)
        return carry

    lax.fori_loop(0, n_kb, split_block, 0)
    t_hi = kth_largest16(hi_ref, topk)
    n_above = count16(hi_ref, t_hi + 1)

    def low_block(j, carry):
        kb = key_ref[j]
        lo = ((kb & 0xFFFF) + _I16_MIN).astype(I16)
        hi_ref[j] = jnp.where((kb >> 16) == t_hi, lo, jnp.int16(_I16_MIN))
        return carry

    lax.fori_loop(0, n_kb, low_block, 0)
    t_lo = kth_largest16(hi_ref, topk - n_above)
    t = lax.shift_left(t_hi, 16) + (t_lo - _I16_MIN)
    t = jnp.maximum(t, _INT_MIN + 1)
    n_ge = count(lambda kb, pos: kb >= t)
    n_gt = count(lambda kb, pos: kb > t)
    all_pos = jnp.int32(2 ** nbits_idx - 1)
    n_tie_take = jnp.where(n_ge > topk, topk - n_gt, all_pos)

    def tie_cutoff():
        def pos_bit(b, c):
            cand = c + lax.shift_left(jnp.int32(1), nbits_idx - 1 - b)
            f = count(lambda kb, pos: (kb == t) & (pos < cand))
            return jnp.where(f <= n_tie_take, cand, c)
        return lax.fori_loop(0, nbits_idx, pos_bit, jnp.zeros((1, tq), I32))

    cut = lax.cond(jnp.max(n_ge) > topk, tie_cutoff, lambda: jnp.full((1, tq), all_pos, I32))

    def bias_block(j, carry):
        kb = key_ref[j]
        sel = (kb > t) | ((kb == t) & (key_pos(j) < cut))
        key_ref[j] = lax.bitcast_convert_type(jnp.where(sel, 0.0, _NEG).astype(F32), I32)
        return carry

    lax.fori_loop(0, n_kb, bias_block, 0)

    m_ref[...] = jnp.full(m_ref.shape, _NEG, F32)
    l_ref[...] = jnp.zeros(l_ref.shape, F32)
    acc_ref[...] = jnp.zeros(acc_ref.shape, F32)

    def attn_block(j, carry):
        c_blk = ckv_ref[pl.ds(pl.multiple_of(j * tk, tk), tk), :]
        c_blk_t = ckvt_ref[j]
        bias = lax.bitcast_convert_type(key_ref[j], F32)
        for hd in range(DSA_HEADS):
            lg = _dot_nt(c_blk, qabs_ref[:, hd * DSA_LATENT:(hd + 1) * DSA_LATENT]) + bias
            m_old = m_ref[hd:hd + 1, :]
            m_new = jnp.maximum(m_old, jnp.max(_sublane_fold(lg, jnp.maximum), axis=0, keepdims=True))
            p = jnp.exp2(lg - m_new)
            alpha = jnp.exp2(m_old - m_new)
            l_ref[hd:hd + 1, :] = alpha * l_ref[hd:hd + 1, :] + jnp.sum(_sublane_fold(p, jnp.add), axis=0,
                                                                         keepdims=True)
            acc_ref[hd] = alpha * acc_ref[hd] + _dot(c_blk_t, p.astype(BF16))
            m_ref[hd:hd + 1, :] = m_new
        return carry

    lax.fori_loop(0, n_kb, attn_block, 0)

    for hd in range(DSA_HEADS):
        o_lat = (acc_ref[hd] / l_ref[hd:hd + 1, :]).T.astype(BF16)
        out_ref[:, hd * DSA_HEAD_DIM:(hd + 1) * DSA_HEAD_DIM] = _dot(o_lat, wuv_ref[hd]).astype(BF16)


def _dsa(qi, wrows, qabs, kidx, ckv, wuv, tq, tk):
    T = qi.shape[0]
    topk = min(TOPK_MAX, T // 4)
    n_kb = T // tk
    ckvt = jnp.transpose(ckv.reshape(n_kb, tk, DSA_LATENT), (0, 2, 1))
    row = lambda n: pl.BlockSpec((tq, n), lambda i: (i, 0))
    body = functools.partial(_dsa_body, tq=tq, tk=tk, topk=topk, nbits_idx=int(T).bit_length())
    return pl.pallas_call(
        body,
        grid=(T // tq,),
        in_specs=[row(qi.shape[1]), pl.BlockSpec((wrows.shape[0], tq), lambda i: (0, i)), row(qabs.shape[1]),
                  _resident(kidx.shape), _resident(ckv.shape), _resident(ckvt.shape), _resident(wuv.shape)],
        out_specs=row(DSA_HEADS * DSA_HEAD_DIM),
        out_shape=jax.ShapeDtypeStruct((T, DSA_HEADS * DSA_HEAD_DIM), BF16),
        scratch_shapes=[pltpu.VMEM((n_kb, tk, tq), I32), pltpu.VMEM((n_kb, tk, tq), I16),
                        pltpu.VMEM((DSA_HEADS, tq), F32),
                        pltpu.VMEM((DSA_HEADS, tq), F32), pltpu.VMEM((DSA_HEADS, DSA_LATENT, tq), F32)],
        compiler_params=_params(),
        name="dsa",
    )(qi, wrows, qabs, kidx, ckv, ckvt, wuv)


def _log_sigmoid(v):
    return jnp.minimum(v, 0.0) - jnp.log1p(jnp.exp(-jnp.abs(v)))


def _chunk_cumsum(v, axis):
    pos = lax.broadcasted_iota(I32, v.shape, axis) & (CHUNK - 1)
    d = 1
    while d < CHUNK:
        v = v + jnp.where(pos >= d, pltpu.roll(v, d, axis=axis), 0.0)
        d *= 2
    return v


def _mlstm_body(mqk_ref, mv_ref, small_ref, gt_ref, mo_ref, cw_ref, cb_ref, gbc_ref, gbr_ref, ng_ref,
                out_ref, xe_ref, c_ref, n_ref, m_ref, hs_ref, *, rows):
    @pl.when(pl.program_id(0) == 0)
    def _():
        xe_ref[0:8, :] = jnp.zeros((8, xe_ref.shape[1]), F32)
        c_ref[...] = jnp.zeros(c_ref.shape, F32)
        n_ref[...] = jnp.zeros(n_ref.shape, F32)
        m_ref[...] = jnp.zeros(m_ref.shape, F32)

    x = mqk_ref[...]
    xe_ref[8:8 + rows, :] = x
    y = cb_ref[...]
    for j in range(CONV_W - 1):
        y = y + xe_ref[5 + j:5 + j + rows, :] * cw_ref[j:j + 1, :]
    y = y + x * cw_ref[CONV_W - 1:CONV_W, :]
    xe_ref[0:8, :] = x[rows - 8:rows, :]
    qk = y * jax.nn.sigmoid(y)
    nqk = ML_HEADS * ML_QK_DIM
    q_all = (qk[:, :nqk] * (ML_QK_DIM ** -0.5)).astype(BF16)
    k_all = qk[:, nqk:]

    g_col = small_ref[...] + gbc_ref[...]
    g_row = gt_ref[...] + gbr_ref[...]
    b_col = _chunk_cumsum(_log_sigmoid(g_col), 0)
    b_row = _chunk_cumsum(_log_sigmoid(g_row), 1)

    tri = lax.broadcasted_iota(I32, (CHUNK, CHUNK), 1) <= lax.broadcasted_iota(I32, (CHUNK, CHUNK), 0)

    for c in range(rows // CHUNK):
        lo, hi = c * CHUNK, (c + 1) * CHUNK
        for hd in range(ML_HEADS):
            bc = b_col[lo:hi, _S_MF + hd:_S_MF + hd + 1]
            lic = g_col[lo:hi, _S_MI + hd:_S_MI + hd + 1]
            br = b_row[ML_HEADS + hd:ML_HEADS + hd + 1, lo:hi]
            lir = g_row[hd:hd + 1, lo:hi]
            g_tot = bc[CHUNK - 1:CHUNK, :]
            m_prev = m_ref[hd][:, 0:1]

            dmat = jnp.where(tri, bc - br + lir, -jnp.inf)
            inter = bc + m_prev
            m_t = jnp.maximum(inter, jnp.max(dmat, axis=-1, keepdims=True))
            w_intra = jnp.exp(dmat - m_t)
            a_inter = jnp.exp(inter - m_t)

            qh = q_all[lo:hi, hd * ML_QK_DIM:(hd + 1) * ML_QK_DIM]
            kh = k_all[lo:hi, hd * ML_QK_DIM:(hd + 1) * ML_QK_DIM]
            vh = mv_ref[lo:hi, hd * ML_V_DIM:(hd + 1) * ML_V_DIM]
            s_qk = _dot_nt(qh, kh.astype(BF16)) * w_intra
            c_prev = c_ref[hd]
            n_prev = n_ref[hd]
            num = a_inter * _dot(qh, c_prev.astype(BF16)) + _dot(s_qk.astype(BF16), vh)
            den = (a_inter * jnp.sum(qh.astype(F32) * n_prev, axis=-1, keepdims=True)
                   + jnp.sum(s_qk, axis=-1, keepdims=True))
            hs_ref[lo:hi, hd * ML_V_DIM:(hd + 1) * ML_V_DIM] = (
                num / jnp.maximum(jnp.abs(den), jnp.exp(-m_t)))

            m_new = jnp.maximum(g_tot + m_prev, jnp.max(g_tot - br + lir, axis=-1, keepdims=True))
            a_state = jnp.exp(g_tot + m_prev - m_new)
            wk = jnp.exp(g_tot - bc + lic - m_new) * kh
            c_ref[hd] = a_state * c_prev + _dot(wk.T.astype(BF16), vh)
            n_ref[hd] = a_state * n_prev + jnp.sum(wk, axis=0, keepdims=True)
            m_ref[hd] = jnp.broadcast_to(m_new, m_ref.shape[1:])

    for hd in range(ML_HEADS):
        sl = slice(hd * ML_V_DIM, (hd + 1) * ML_V_DIM)
        out_ref[:, sl] = (_rms(hs_ref[:, sl], ng_ref[:, sl]) * jax.nn.sigmoid(mo_ref[:, sl])).astype(BF16)


def _mlstm(mqk, mv, small, gt, mo, cw, cb, gbc, gbr, ng, rows):
    T = mqk.shape[0]
    row = lambda n: pl.BlockSpec((rows, n), lambda i: (i, 0))
    nv = ML_HEADS * ML_V_DIM
    return pl.pallas_call(
        functools.partial(_mlstm_body, rows=rows),
        grid=(T // rows,),
        in_specs=[row(mqk.shape[1]), row(nv), row(_SMALL), pl.BlockSpec((8, rows), lambda i: (1, i)), row(nv),
                  _resident(cw.shape), _resident(cb.shape), _resident(gbc.shape), _resident(gbr.shape),
                  _resident(ng.shape)],
        out_specs=row(nv),
        out_shape=jax.ShapeDtypeStruct((T, nv), BF16),
        scratch_shapes=[pltpu.VMEM((rows + 8, mqk.shape[1]), F32),
                        pltpu.VMEM((ML_HEADS, ML_QK_DIM, ML_V_DIM), F32),
                        pltpu.VMEM((ML_HEADS, 1, ML_QK_DIM), F32),
                        pltpu.VMEM((ML_HEADS, 1, 128), F32),
                        pltpu.VMEM((rows, nv), F32)],
        compiler_params=_params(),
        name="mlstm",
    )(mqk, mv, small, gt, mo, cw, cb, gbc, gbr, ng)


def _memkv_body(mem_ref, g_ref, w_ref, out_ref):
    mn = _rms(mem_ref[...], g_ref[...]).astype(BF16)
    out_ref[...] = _dot(mn, w_ref[...].astype(BF16)).astype(BF16)


def _memkv(mem, g, w, tn):
    M, D = mem.shape
    N = w.shape[1]
    return pl.pallas_call(
        _memkv_body,
        grid=(N // tn,),
        in_specs=[_resident(mem.shape), _resident(g.shape), pl.BlockSpec((D, tn), lambda j: (0, j))],
        out_specs=pl.BlockSpec((M, tn), lambda j: (0, j)),
        out_shape=jax.ShapeDtypeStruct((M, N), BF16),
        compiler_params=_params(),
        name="memkv",
    )(mem, g, w)


def _mixout_body(x_ref, dsa_ref, ml_ref, kv_ref, wo_ref, wq_ref, wc_ref, wr_ref, gx_ref, gf_ref,
                 x2_ref, hf_ref, rl_ref, o_ref):
    nd = dsa_ref.shape[1]
    x1 = x_ref[...] + _dot(dsa_ref[...], wo_ref[0:nd, :]) + _dot(ml_ref[...], wo_ref[nd:, :])
    q = _dot(_rms(x1, gx_ref[...]).astype(BF16), wq_ref[...]).astype(BF16)
    for hd in range(X_HEADS):
        sl = slice(hd * X_HEAD_DIM, (hd + 1) * X_HEAD_DIM)
        lg = _dot_nt(q[:, sl], kv_ref[:, sl]) * (X_HEAD_DIM ** -0.5)
        e = jnp.exp(lg - jnp.max(lg, axis=-1, keepdims=True))
        p = e / jnp.sum(e, axis=-1, keepdims=True)
        v = kv_ref[:, D_MODEL + hd * X_HEAD_DIM:D_MODEL + (hd + 1) * X_HEAD_DIM]
        o_ref[:, sl] = _dot(p.astype(BF16), v).astype(BF16)
    x2 = x1 + _dot(o_ref[...], wc_ref[...])
    x2_ref[...] = x2
    hf = _rms(x2, gf_ref[...])
    hf_ref[...] = hf
    rl_ref[...] = _dot(hf.astype(BF16), wr_ref[...])


def _mixout(x, dsa, ml, kv, wo, wq, wc, wr, gx, gf, tm):
    T = x.shape[0]
    row = lambda n: pl.BlockSpec((tm, n), lambda i: (i, 0))
    return pl.pallas_call(
        _mixout_body,
        grid=(T // tm,),
        in_specs=[row(D_MODEL), row(dsa.shape[1]), row(ml.shape[1]), _resident(kv.shape), _resident(wo.shape),
                  _resident(wq.shape), _resident(wc.shape), _resident(wr.shape), _resident(gx.shape),
                  _resident(gf.shape)],
        out_specs=[row(D_MODEL), row(D_MODEL), row(wr.shape[1])],
        out_shape=[jax.ShapeDtypeStruct((T, D_MODEL), F32), jax.ShapeDtypeStruct((T, D_MODEL), F32),
                   jax.ShapeDtypeStruct((T, wr.shape[1]), F32)],
        scratch_shapes=[pltpu.VMEM((tm, D_MODEL), BF16)],
        compiler_params=_params(),
        name="mixout",
    )(x, dsa, ml, kv, wo, wq, wc, wr, gx, gf)


def _moe_body(tok_ref, eb_ref, hf_hbm, wg_ref, wu_ref, wd_ref, y_hbm,
              xbuf, ybuf, wgb, wub, wdb, gsem, ysem, *, bm, n_blk):
    e = pl.program_id(0)
    n_valid = eb_ref[N_EXPERTS]
    b_lo = eb_ref[e]
    b_hi = eb_ref[e + 1]

    def gather(blk, s):
        def issue(g, carry):
            for u in range(_SUBLANES):
                tok = tok_ref[blk * bm + g * _SUBLANES + u]
                pltpu.make_async_copy(hf_hbm.at[pl.ds(tok, 1)], xbuf.at[s, g, pl.ds(u, 1)],
                                      gsem.at[s]).start(priority=u % 2)
            return carry
        lax.fori_loop(0, bm // _SUBLANES, issue, 0)

    def wait_gather(s):
        pltpu.make_async_copy(xbuf.at[s], xbuf.at[s], gsem.at[s]).wait()

    def y_copy(blk, s):
        return pltpu.make_async_copy(ybuf.at[s], y_hbm.at[pl.ds(pl.multiple_of(blk * bm, bm), bm)], ysem.at[s])

    @pl.when(e == 0)
    def _():
        gather(0, 0)

        @pl.when(n_valid > 1)
        def _():
            gather(1, 1)

    @pl.when(b_hi > b_lo)
    def _():
        wgb[...] = wg_ref[0].astype(BF16)
        wub[...] = wu_ref[0].astype(BF16)
        wdb[...] = wd_ref[0].astype(BF16)

        def block(b, carry):
            s = b % _GATHER_SLOTS

            @pl.when(b + 2 < n_valid)
            def _():
                gather(b + 2, (b + 2) % _GATHER_SLOTS)

            wait_gather(s)
            xb = xbuf[s].reshape(bm, xbuf.shape[-1]).astype(BF16)
            gate = _dot(xb, wgb[...])
            a = gate * jax.nn.sigmoid(gate) * _dot(xb, wub[...])
            y = _dot(a.astype(BF16), wdb[...])

            @pl.when(b >= 2)
            def _():
                y_copy(b - 2, b % 2).wait()

            ybuf[b % 2] = y
            y_copy(b, b % 2).start()
            return carry

        lax.fori_loop(b_lo, b_hi, block, 0)

    @pl.when(e == pl.num_programs(0) - 1)
    def _():
        @pl.when(n_valid >= 2)
        def _():
            y_copy(n_valid - 2, n_valid % 2).wait()
        y_copy(n_valid - 1, (n_valid - 1) % 2).wait()
        ybuf[0] = jnp.zeros(ybuf.shape[1:], F32)

        def zero_block(b, carry):
            cp = y_copy(b, 0)
            cp.start()
            cp.wait()
            return carry

        lax.fori_loop(n_valid, n_blk, zero_block, 0)


def _moe(tok, e_blk, hf, wg, wu, wd, n_blk, bm):
    D = hf.shape[1]
    wspec = lambda shape: pl.BlockSpec((1,) + shape, lambda e, *_: (e, 0, 0))
    grid_spec = pltpu.PrefetchScalarGridSpec(
        num_scalar_prefetch=2,
        grid=(N_EXPERTS,),
        in_specs=[pl.BlockSpec(memory_space=pl.ANY),
                  wspec((D, D_EXPERT)), wspec((D, D_EXPERT)), wspec((D_EXPERT, D))],
        out_specs=pl.BlockSpec(memory_space=pl.ANY),
        scratch_shapes=[pltpu.VMEM((_GATHER_SLOTS, bm // _SUBLANES, _SUBLANES, D), F32),
                        pltpu.VMEM((2, bm, D), F32),
                        pltpu.VMEM((D, D_EXPERT), BF16), pltpu.VMEM((D, D_EXPERT), BF16),
                        pltpu.VMEM((D_EXPERT, D), BF16),
                        pltpu.SemaphoreType.DMA((_GATHER_SLOTS,)), pltpu.SemaphoreType.DMA((2,))],
    )
    return pl.pallas_call(
        functools.partial(_moe_body, bm=bm, n_blk=n_blk),
        grid_spec=grid_spec,
        out_shape=jax.ShapeDtypeStruct((n_blk * bm, D), F32),
        compiler_params=_params(),
        name="moe",
    )(tok, e_blk, hf, wg, wu, wd)


def _route(rl, b_group, b_router, bm):
    N = rl.shape[0]
    g_logits = rl[:, :N_GROUPS] + b_group
    g_prob = jax.nn.softmax(g_logits, axis=-1)
    g_sel = jnp.argmax(g_logits, axis=-1)
    p_g = jnp.take_along_axis(g_prob, g_sel[:, None], axis=-1)
    e_logits = (rl[:, N_GROUPS:N_GROUPS + N_EXPERTS] + b_router).reshape(N, N_GROUPS, EXP_PER_GROUP)
    e_logits = jnp.take_along_axis(e_logits, g_sel[:, None, None], axis=1)[:, 0]
    top_p, top_local = lax.top_k(jax.nn.softmax(e_logits, axis=-1), TOPK_IN_GROUP)
    gates = p_g * top_p / top_p.sum(-1, keepdims=True)
    expert_id = (g_sel[:, None] * EXP_PER_GROUP + top_local).astype(I32)

    A = N * TOPK_IN_GROUP
    flat_e = expert_id.reshape(A)
    onehot = (flat_e[:, None] == jnp.arange(N_EXPERTS, dtype=I32)[None, :])
    seg = _RANK_SEG if A % _RANK_SEG == 0 else A
    oh = onehot.astype(BF16).reshape(A // seg, seg, N_EXPERTS)
    before = (jnp.arange(seg)[None, :] < jnp.arange(seg)[:, None]).astype(BF16)
    within = jnp.einsum('ij,bjk->bik', before, oh, preferred_element_type=F32)
    seg_tot = jnp.sum(oh.astype(F32), axis=1)
    seg_base = jnp.cumsum(seg_tot, axis=0) - seg_tot
    rank = jnp.sum((within + seg_base[:, None, :]) * oh.astype(F32), axis=-1).reshape(A).astype(I32)
    counts = jnp.sum(seg_tot, axis=0).astype(I32)
    padded = (counts + bm - 1) // bm * bm
    pad_ends = jnp.cumsum(padded)
    pad_starts = pad_ends - padded
    row = jnp.sum(jnp.where(onehot, pad_starts[None, :], 0), axis=1) + rank
    n_blk = -(-A // bm) + N_EXPERTS
    row_tok = jnp.zeros((n_blk * bm,), I32).at[row].set(jnp.arange(A, dtype=I32) // TOPK_IN_GROUP)
    e_blk = jnp.concatenate([pad_starts, pad_ends[-1:]]) // bm
    return row_tok, row, gates, e_blk.astype(I32), n_blk


def _final_body(row_ref, x_ref, gate_ref, y_hbm, g_ref, out_ref, ybuf, sem, *, tm):
    i = pl.program_id(0)
    slot = i % 2

    def gather(tile, s):
        def issue(g, carry):
            for u in range(_SUBLANES):
                for k in range(TOPK_IN_GROUP):
                    src = row_ref[(tile * tm + g * _SUBLANES + u) * TOPK_IN_GROUP + k]
                    pltpu.make_async_copy(y_hbm.at[pl.ds(src, 1)], ybuf.at[s, k, g, pl.ds(u, 1)],
                                          sem.at[s]).start()
            return carry
        lax.fori_loop(0, tm // _SUBLANES, issue, 0)

    @pl.when(i == 0)
    def _():
        gather(0, 0)

    @pl.when(i + 1 < pl.num_programs(0))
    def _():
        gather(i + 1, 1 - slot)

    pltpu.make_async_copy(ybuf.at[slot], ybuf.at[slot], sem.at[slot]).wait()
    acc = x_ref[...]
    for k in range(TOPK_IN_GROUP):
        acc = acc + gate_ref[:, k:k + 1] * ybuf[slot, k].reshape(tm, ybuf.shape[-1])
    out_ref[...] = _rms(acc, g_ref[...])


def _final(row, x2, gates, y_rows, g, tm):
    T, D = x2.shape
    grid_spec = pltpu.PrefetchScalarGridSpec(
        num_scalar_prefetch=1,
        grid=(T // tm,),
        in_specs=[pl.BlockSpec((tm, D), lambda i, *_: (i, 0)),
                  pl.BlockSpec((tm, TOPK_IN_GROUP), lambda i, *_: (i, 0)),
                  pl.BlockSpec(memory_space=pl.ANY),
                  pl.BlockSpec(g.shape, lambda i, *_: (0, 0))],
        out_specs=pl.BlockSpec((tm, D), lambda i, *_: (i, 0)),
        scratch_shapes=[pltpu.VMEM((2, TOPK_IN_GROUP, tm // _SUBLANES, _SUBLANES, D), F32),
                        pltpu.SemaphoreType.DMA((2,))],
    )
    return pl.pallas_call(
        functools.partial(_final_body, tm=tm),
        grid_spec=grid_spec,
        out_shape=jax.ShapeDtypeStruct((T, D), F32),
        compiler_params=_params(),
        name="final",
    )(row, x2, gates, y_rows, g)


def _tile_sizes(T):
    pick = lambda want: want if T % want == 0 else CHUNK
    return dict(inproj=pick(256), dsa_q=pick(256), dsa_k=pick(512), mlstm=pick(256), mixout=pick(256),
                final=pick(256), moe=128)


def _layer(x, mem, norm_mix_g, w_in, kv_norm_g, k_idx_norm_g, w_uk, w_uv, conv_w, conv_b, gate_b, ml_norm_g,
           w_out, norm_x_g, mem_norm_g, w_cq, w_ckv, w_co, norm_ffn_g, w_group, b_group, w_router, b_router,
           w_gate, w_up, w_down, out_g):
    T = x.shape[0]
    ts = _tile_sizes(T)
    r2 = lambda v: v.reshape(1, -1)

    w_b = w_in.astype(BF16)
    w_r = jnp.concatenate([
        w_b[:, _O_DQ:_O_MQ], w_b[:, _O_MI:_O_MO],
        jnp.zeros((D_MODEL, _SMALL - IDX_DIM - IDX_HEADS - 2 * ML_HEADS), BF16),
        w_b[:, _O_MQ:_O_MI], w_b[:, _O_MO:_O_END]], axis=1)
    wuk_t = jnp.transpose(w_uk, (1, 2, 0)).astype(BF16)
    wuv_t = jnp.transpose(w_uv, (1, 0, 2)).astype(BF16)

    qabs, ckv, qi, kidx, small, mqk, mv, mo = _inproj(
        x, r2(norm_mix_g), w_r, wuk_t, r2(kv_norm_g), r2(k_idx_norm_g), ts["inproj"])

    gate_rows = jnp.transpose(small[:, _S_WI:_S_MF + ML_HEADS])
    dsa_out = _dsa(qi, gate_rows, qabs, kidx, ckv, wuv_t, ts["dsa_q"], ts["dsa_k"])

    gb_col = jnp.zeros((1, _SMALL), F32).at[0, _S_MI:_S_MI + 2 * ML_HEADS].set(gate_b)
    ml_out = _mlstm(mqk, mv, small, gate_rows, mo, conv_w, r2(conv_b), gb_col, gate_b.reshape(-1, 1),
                    r2(ml_norm_g), ts["mlstm"])

    kv = _memkv(mem, r2(mem_norm_g), w_ckv, 512)
    w_rt = jnp.concatenate([w_group, w_router,
                            jnp.zeros((D_MODEL, 128 - N_GROUPS - N_EXPERTS), w_group.dtype)], axis=1)
    x2, hf, rl = _mixout(x, dsa_out, ml_out, kv, w_out.astype(BF16), w_cq.astype(BF16), w_co.astype(BF16),
                         w_rt.astype(BF16), r2(norm_x_g), r2(norm_ffn_g), ts["mixout"])

    bm = ts["moe"]
    row_tok, row, gates, e_blk, n_blk = _route(rl, b_group, b_router, bm)
    y_rows = _moe(row_tok, e_blk, hf, w_gate, w_up, w_down, n_blk, bm)
    return _final(row, x2, gates, y_rows, r2(out_g), ts["final"])


def kernel(x, mem, norm_mix_g, w_in, kv_norm_g, k_idx_norm_g, w_uk, w_uv, conv_w, conv_b, gate_b, ml_norm_g,
           w_out, norm_x_g, mem_norm_g, w_cq, w_ckv, w_co, norm_ffn_g, w_group, b_group, w_router, b_router,
           w_gate, w_up, w_down, final_norm_g):
    B, T, D = x.shape
    assert B == 1 and D == D_MODEL and norm_mix_g.shape[0] == 1 and T % CHUNK == 0
    out = _layer(x[0], mem[0], norm_mix_g[0], w_in[0], kv_norm_g[0], k_idx_norm_g[0], w_uk[0], w_uv[0],
                 conv_w[0], conv_b[0], gate_b[0], ml_norm_g[0], w_out[0], norm_x_g[0], mem_norm_g[0],
                 w_cq[0], w_ckv[0], w_co[0], norm_ffn_g[0], w_group[0], b_group[0], w_router[0], b_router[0],
                 w_gate[0], w_up[0], w_down[0], final_norm_g)
    return out[None]
```

```python
import functools

import jax
import jax.numpy as jnp
import numpy as np
from jax import lax
from jax.experimental import pallas as pl
from jax.experimental.pallas import tpu as pltpu

F32 = jnp.float32
BF16 = jnp.bfloat16
I32 = jnp.int32
I16 = jnp.int16

EPS = 1e-6
CHUNK = 64
D_MODEL = 2048

DSA_HEADS = 8
DSA_HEAD_DIM = 128
DSA_LATENT = 256
IDX_HEADS = 8
IDX_DIM = 64
TOPK_MAX = 256

ML_HEADS = 4
ML_QK_DIM = 128
ML_V_DIM = 256
CONV_W = 4

X_HEADS = 4
X_HEAD_DIM = D_MODEL // X_HEADS

N_GROUPS = 4
EXP_PER_GROUP = 8
N_EXPERTS = N_GROUPS * EXP_PER_GROUP
TOPK_IN_GROUP = 2
D_EXPERT = 512

_O_DQ = 0
_O_CKV = _O_DQ + DSA_HEADS * DSA_HEAD_DIM
_O_QI = _O_CKV + DSA_LATENT
_O_KI = _O_QI + IDX_HEADS * IDX_DIM
_O_WI = _O_KI + IDX_DIM
_O_MQ = _O_WI + IDX_HEADS
_O_MK = _O_MQ + ML_HEADS * ML_QK_DIM
_O_MV = _O_MK + ML_HEADS * ML_QK_DIM
_O_MI = _O_MV + ML_HEADS * ML_V_DIM
_O_MF = _O_MI + ML_HEADS
_O_MO = _O_MF + ML_HEADS
_O_END = _O_MO + ML_HEADS * ML_V_DIM

_G_DQ = (0, 1024)
_G_CKV = (1024, 1280)
_G_QI = (1280, 1792)
_G_SMALL = (1792, 1920)
_G_MQK = (1920, 2944)
_G_MV = (2944, 3968)
_G_MO = (3968, 4992)
_W_COLS = 4992
_S_WI = IDX_DIM
_S_MI = _S_WI + IDX_HEADS
_S_MF = _S_MI + ML_HEADS
_SMALL = 128

_VMEM_LIMIT = 56 * 1024 * 1024
_INT_MIN = -(2 ** 31)
_I16_MIN = -(2 ** 15)
_CHUNK_SHIFT = CHUNK.bit_length() - 1
_LOG2E = 1.4426950408889634
_SUBLANES = 8
_GATHER_SLOTS = 3
_RANK_SEG = 512
_NEG = -1e30


def _rms(v, g):
    return v * lax.rsqrt(jnp.mean(v * v, axis=-1, keepdims=True) + EPS) * g


def _dot(a, b):
    return jnp.dot(a, b, preferred_element_type=F32)


def _dot_nt(a, b):
    return lax.dot_general(a, b, (((1,), (1,)), ((), ())), preferred_element_type=F32)


def _resident(shape):
    nd = len(shape)
    return pl.BlockSpec(shape, lambda *_: (0,) * nd, pipeline_mode=pl.Buffered(1))


def _params(n_axes=1):
    return pltpu.CompilerParams(dimension_semantics=("arbitrary",) * n_axes,
                                vmem_limit_bytes=_VMEM_LIMIT)


def _inproj_body(x_ref, g_ref, w_ref, wuk_ref, kvg_ref, kig_ref,
                 qabs_ref, ckv_ref, qi_ref, kidx_ref, small_ref, mqk_ref, mv_ref, mo_ref):
    h = _rms(x_ref[...], g_ref[...]).astype(BF16)

    def proj(grp):
        return _dot(h, w_ref[:, grp[0]:grp[1]])

    dq = proj(_G_DQ)
    for hd in range(DSA_HEADS):
        qh = dq[:, hd * DSA_HEAD_DIM:(hd + 1) * DSA_HEAD_DIM].astype(BF16)
        qa = _dot(qh, wuk_ref[hd]) * (DSA_HEAD_DIM ** -0.5 * _LOG2E)
        qabs_ref[:, hd * DSA_LATENT:(hd + 1) * DSA_LATENT] = qa.astype(BF16)
    ckv_ref[...] = _rms(proj(_G_CKV), kvg_ref[...]).astype(BF16)
    qi_ref[...] = (proj(_G_QI) * (IDX_DIM ** -0.5)).astype(BF16)
    small = proj(_G_SMALL)
    small_ref[...] = small
    kidx_ref[...] = _rms(small[:, :IDX_DIM], kig_ref[...]).astype(BF16)
    mqk_ref[...] = proj(_G_MQK)
    mv_ref[...] = proj(_G_MV).astype(BF16)
    mo_ref[...] = proj(_G_MO)


def _inproj(x, g, w, wuk, kvg, kig, tm):
    T = x.shape[0]
    row = lambda n: pl.BlockSpec((tm, n), lambda i: (i, 0))
    outs = [(8 * DSA_LATENT, BF16), (DSA_LATENT, BF16), (IDX_HEADS * IDX_DIM, BF16), (IDX_DIM, BF16),
            (_SMALL, F32), (2 * ML_HEADS * ML_QK_DIM, F32), (ML_HEADS * ML_V_DIM, BF16),
            (ML_HEADS * ML_V_DIM, F32)]
    return pl.pallas_call(
        _inproj_body,
        grid=(T // tm,),
        in_specs=[row(D_MODEL), _resident(g.shape), _resident(w.shape), _resident(wuk.shape),
                  _resident(kvg.shape), _resident(kig.shape)],
        out_specs=[row(n) for n, _ in outs],
        out_shape=[jax.ShapeDtypeStruct((T, n), dt) for n, dt in outs],
        compiler_params=_params(),
        name="inproj",
    )(x, g, w, wuk, kvg, kig)


def _sublane_fold(v, op):
    acc = v[0:8, :]
    for r in range(1, v.shape[0] // 8):
        acc = op(acc, v[r * 8:(r + 1) * 8, :])
    return acc


def _dsa_body(qi_ref, wrow_ref, qabs_ref, kidx_ref, ckv_ref, ckvt_ref, wuv_ref, out_ref,
              key_ref, hi_ref, m_ref, l_ref, acc_ref, *, tq, tk, topk, nbits_idx):
    i = pl.program_id(0)
    n_kb = ((i + 1) * tq + tk - 1) // tk
    w_rows = wrow_ref[0:IDX_HEADS, :] * (IDX_HEADS ** -0.5)
    q_chunk = (i * tq + lax.broadcasted_iota(I32, (1, tq), 1)) >> _CHUNK_SHIFT

    def key_pos(j):
        return j * tk + lax.broadcasted_iota(I32, (tk, 1), 0)

    def score_block(j, carry):
        kx = kidx_ref[pl.ds(pl.multiple_of(j * tk, tk), tk), :]
        s = jnp.zeros((tk, tq), F32)
        for hd in range(IDX_HEADS):
            d = _dot_nt(kx, qi_ref[:, hd * IDX_DIM:(hd + 1) * IDX_DIM])
            s = s + w_rows[hd:hd + 1, :] * jnp.maximum(d, 0.0)
        bits = lax.bitcast_convert_type(s, I32)
        key = bits ^ ((bits >> 31) & 0x7FFFFFFF)
        key_ref[j] = jnp.where((key_pos(j) >> _CHUNK_SHIFT) <= q_chunk, key, _INT_MIN)
        return carry

    lax.fori_loop(0, n_kb, score_block, 0)

    def count(pred):
        def body(j, acc):
            hit = pred(key_ref[j], key_pos(j)).astype(I32)
            return acc + _sublane_fold(hit, jnp.add)
        acc = lax.fori_loop(0, n_kb, body, jnp.zeros((8, tq), I32))
        return jnp.sum(acc, axis=0, keepdims=True)

    def count16(ref, cand):
        c16 = cand.astype(I16)
        def body(j, acc):
            hit = jnp.where(ref[j] >= c16, jnp.int16(1), jnp.int16(0))
            part = hit[0:16, :]
            for r in range(1, tk // 16):
                part = part + hit[r * 16:(r + 1) * 16, :]
            return acc + part
        acc = lax.fori_loop(0, n_kb, body, jnp.zeros((16, tq), I16))
        return jnp.sum(acc.astype(I32), axis=0, keepdims=True)

    def kth_largest16(ref, kth):
        def bit(b, t):
            cand = t + lax.shift_left(jnp.int32(1), 15 - b)
            return jnp.where(count16(ref, cand) >= kth, cand, t)
        return lax.fori_loop(0, 16, bit, jnp.full((1, tq), _I16_MIN, I32))

    def split_block(j, carry):
        kb = key_ref[j]
        hi_ref[j] = (kb >> 16).astype(I16)
        return carry

    lax.fori_loop(0, n_kb, split_block, 0)
    t_hi = kth_largest16(hi_ref, topk)
    n_above = count16(hi_ref, t_hi + 1)

    def low_block(j, carry):
        kb = key_ref[j]
        lo = ((kb & 0xFFFF) + _I16_MIN).astype(I16)
        hi_ref[j] = jnp.where((kb >> 16) == t_hi, lo, jnp.int16(_I16_MIN))
        return carry

    lax.fori_loop(0, n_kb, low_block, 0)
    t_lo = kth_largest16(hi_ref, topk - n_above)
    t = lax.shift_left(t_hi, 16) + (t_lo - _I16_MIN)
    t = jnp.maximum(t, _INT_MIN + 1)
    n_ge = count(lambda kb, pos: kb >= t)
    n_gt = count(lambda kb, pos: kb > t)
    all_pos = jnp.int32(2 ** nbits_idx - 1)
    n_tie_take = jnp.where(n_ge > topk, topk - n_gt, all_pos)

    def tie_cutoff():
        def pos_bit(b, c):
            cand = c + lax.shift_left(jnp.int32(1), nbits_idx - 1 - b)
            f = count(lambda kb, pos: (kb == t) & (pos < cand))
            return jnp.where(f <= n_tie_take, cand, c)
        return lax.fori_loop(0, nbits_idx, pos_bit, jnp.zeros((1, tq), I32))

    cut = lax.cond(jnp.max(n_ge) > topk, tie_cutoff, lambda: jnp.full((1, tq), all_pos, I32))

    def bias_block(j, carry):
        kb = key_ref[j]
        sel = (kb > t) | ((kb == t) & (key_pos(j) < cut))
        key_ref[j] = lax.bitcast_convert_type(jnp.where(sel, 0.0, _NEG).astype(F32), I32)
        return carry

    lax.fori_loop(0, n_kb, bias_block, 0)

    m_ref[...] = jnp.full(m_ref.shape, _NEG, F32)
    l_ref[...] = jnp.zeros(l_ref.shape, F32)
    acc_ref[...] = jnp.zeros(acc_ref.shape, F32)

    def attn_block(j, carry):
        c_blk = ckv_ref[pl.ds(pl.multiple_of(j * tk, tk), tk), :]
        c_blk_t = ckvt_ref[j]
        bias = lax.bitcast_convert_type(key_ref[j], F32)
        for hd in range(DSA_HEADS):
            lg = _dot_nt(c_blk, qabs_ref[:, hd * DSA_LATENT:(hd + 1) * DSA_LATENT]) + bias
            m_old = m_ref[hd:hd + 1, :]
            m_new = jnp.maximum(m_old, jnp.max(_sublane_fold(lg, jnp.maximum), axis=0, keepdims=True))
            p = jnp.exp2(lg - m_new)
            alpha = jnp.exp2(m_old - m_new)
            l_ref[hd:hd + 1, :] = alpha * l_ref[hd:hd + 1, :] + jnp.sum(_sublane_fold(p, jnp.add), axis=0,
                                                                         keepdims=True)
            acc_ref[hd] = alpha * acc_ref[hd] + _dot(c_blk_t, p.astype(BF16))
            m_ref[hd:hd + 1, :] = m_new
        return carry

    lax.fori_loop(0, n_kb, attn_block, 0)

    for hd in range(DSA_HEADS):
        o_lat = (acc_ref[hd] / l_ref[hd:hd + 1, :]).T.astype(BF16)
        out_ref[:, hd * DSA_HEAD_DIM:(hd + 1) * DSA_HEAD_DIM] = _dot(o_lat, wuv_ref[hd]).astype(BF16)


def _dsa(qi, wrows, qabs, kidx, ckv, wuv, tq, tk):
    T = qi.shape[0]
    topk = min(TOPK_MAX, T // 4)
    n_kb = T // tk
    ckvt = jnp.transpose(ckv.reshape(n_kb, tk, DSA_LATENT), (0, 2, 1))
    row = lambda n: pl.BlockSpec((tq, n), lambda i: (i, 0))
    body = functools.partial(_dsa_body, tq=tq, tk=tk, topk=topk, nbits_idx=int(T).bit_length())
    return pl.pallas_call(
        body,
        grid=(T // tq,),
        in_specs=[row(qi.shape[1]), pl.BlockSpec((wrows.shape[0], tq), lambda i: (0, i)), row(qabs.shape[1]),
                  _resident(kidx.shape), _resident(ckv.shape), _resident(ckvt.shape), _resident(wuv.shape)],
        out_specs=row(DSA_HEADS * DSA_HEAD_DIM),
        out_shape=jax.ShapeDtypeStruct((T, DSA_HEADS * DSA_HEAD_DIM), BF16),
        scratch_shapes=[pltpu.VMEM((n_kb, tk, tq), I32), pltpu.VMEM((n_kb, tk, tq), I16),
                        pltpu.VMEM((DSA_HEADS, tq), F32),
                        pltpu.VMEM((DSA_HEADS, tq), F32), pltpu.VMEM((DSA_HEADS, DSA_LATENT, tq), F32)],
        compiler_params=_params(),
        name="dsa",
    )(qi, wrows, qabs, kidx, ckv, ckvt, wuv)


def _log_sigmoid(v):
    return jnp.minimum(v, 0.0) - jnp.log1p(jnp.exp(-jnp.abs(v)))


def _chunk_cumsum(v, axis):
    pos = lax.broadcasted_iota(I32, v.shape, axis) & (CHUNK - 1)
    d = 1
    while d < CHUNK:
        v = v + jnp.where(pos >= d, pltpu.roll(v, d, axis=axis), 0.0)
        d *= 2
    return v


def _mlstm_body(mqk_ref, mv_ref, small_ref, gt_ref, mo_ref, cw_ref, cb_ref, gbc_ref, gbr_ref, ng_ref,
                out_ref, xe_ref, c_ref, n_ref, m_ref, hs_ref, *, rows):
    @pl.when(pl.program_id(0) == 0)
    def _():
        xe_ref[0:8, :] = jnp.zeros((8, xe_ref.shape[1]), F32)
        c_ref[...] = jnp.zeros(c_ref.shape, F32)
        n_ref[...] = jnp.zeros(n_ref.shape, F32)
        m_ref[...] = jnp.zeros(m_ref.shape, F32)

    x = mqk_ref[...]
    xe_ref[8:8 + rows, :] = x
    y = cb_ref[...]
    for j in range(CONV_W - 1):
        y = y + xe_ref[5 + j:5 + j + rows, :] * cw_ref[j:j + 1, :]
    y = y + x * cw_ref[CONV_W - 1:CONV_W, :]
    xe_ref[0:8, :] = x[rows - 8:rows, :]
    qk = y * jax.nn.sigmoid(y)
    nqk = ML_HEADS * ML_QK_DIM
    q_all = (qk[:, :nqk] * (ML_QK_DIM ** -0.5)).astype(BF16)
    k_all = qk[:, nqk:]

    g_col = small_ref[...] + gbc_ref[...]
    g_row = gt_ref[...] + gbr_ref[...]
    b_col = _chunk_cumsum(_log_sigmoid(g_col), 0)
    b_row = _chunk_cumsum(_log_sigmoid(g_row), 1)

    tri = lax.broadcasted_iota(I32, (CHUNK, CHUNK), 1) <= lax.broadcasted_iota(I32, (CHUNK, CHUNK), 0)

    for c in range(rows // CHUNK):
        lo, hi = c * CHUNK, (c + 1) * CHUNK
        for hd in range(ML_HEADS):
            bc = b_col[lo:hi, _S_MF + hd:_S_MF + hd + 1]
            lic = g_col[lo:hi, _S_MI + hd:_S_MI + hd + 1]
            br = b_row[ML_HEADS + hd:ML_HEADS + hd + 1, lo:hi]
            lir = g_row[hd:hd + 1, lo:hi]
            g_tot = bc[CHUNK - 1:CHUNK, :]
            m_prev = m_ref[hd][:, 0:1]

            dmat = jnp.where(tri, bc - br + lir, -jnp.inf)
            inter = bc + m_prev
            m_t = jnp.maximum(inter, jnp.max(dmat, axis=-1, keepdims=True))
            w_intra = jnp.exp(dmat - m_t)
            a_inter = jnp.exp(inter - m_t)

            qh = q_all[lo:hi, hd * ML_QK_DIM:(hd + 1) * ML_QK_DIM]
            kh = k_all[lo:hi, hd * ML_QK_DIM:(hd + 1) * ML_QK_DIM]
            vh = mv_ref[lo:hi, hd * ML_V_DIM:(hd + 1) * ML_V_DIM]
            s_qk = _dot_nt(qh, kh.astype(BF16)) * w_intra
            c_prev = c_ref[hd]
            n_prev = n_ref[hd]
            num = a_inter * _dot(qh, c_prev.astype(BF16)) + _dot(s_qk.astype(BF16), vh)
            den = (a_inter * jnp.sum(qh.astype(F32) * n_prev, axis=-1, keepdims=True)
                   + jnp.sum(s_qk, axis=-1, keepdims=True))
            hs_ref[lo:hi, hd * ML_V_DIM:(hd + 1) * ML_V_DIM] = (
                num / jnp.maximum(jnp.abs(den), jnp.exp(-m_t)))

            m_new = jnp.maximum(g_tot + m_prev, jnp.max(g_tot - br + lir, axis=-1, keepdims=True))
            a_state = jnp.exp(g_tot + m_prev - m_new)
            wk = jnp.exp(g_tot - bc + lic - m_new) * kh
            c_ref[hd] = a_state * c_prev + _dot(wk.T.astype(BF16), vh)
            n_ref[hd] = a_state * n_prev + jnp.sum(wk, axis=0, keepdims=True)
            m_ref[hd] = jnp.broadcast_to(m_new, m_ref.shape[1:])

    for hd in range(ML_HEADS):
        sl = slice(hd * ML_V_DIM, (hd + 1) * ML_V_DIM)
        out_ref[:, sl] = (_rms(hs_ref[:, sl], ng_ref[:, sl]) * jax.nn.sigmoid(mo_ref[:, sl])).astype(BF16)


def _mlstm(mqk, mv, small, gt, mo, cw, cb, gbc, gbr, ng, rows):
    T = mqk.shape[0]
    row = lambda n: pl.BlockSpec((rows, n), lambda i: (i, 0))
    nv = ML_HEADS * ML_V_DIM
    return pl.pallas_call(
        functools.partial(_mlstm_body, rows=rows),
        grid=(T // rows,),
        in_specs=[row(mqk.shape[1]), row(nv), row(_SMALL), pl.BlockSpec((8, rows), lambda i: (1, i)), row(nv),
                  _resident(cw.shape), _resident(cb.shape), _resident(gbc.shape), _resident(gbr.shape),
                  _resident(ng.shape)],
        out_specs=row(nv),
        out_shape=jax.ShapeDtypeStruct((T, nv), BF16),
        scratch_shapes=[pltpu.VMEM((rows + 8, mqk.shape[1]), F32),
                        pltpu.VMEM((ML_HEADS, ML_QK_DIM, ML_V_DIM), F32),
                        pltpu.VMEM((ML_HEADS, 1, ML_QK_DIM), F32),
                        pltpu.VMEM((ML_HEADS, 1, 128), F32),
                        pltpu.VMEM((rows, nv), F32)],
        compiler_params=_params(),
        name="mlstm",
    )(mqk, mv, small, gt, mo, cw, cb, gbc, gbr, ng)


def _memkv_body(mem_ref, g_ref, w_ref, out_ref):
    mn = _rms(mem_ref[...], g_ref[...]).astype(BF16)
    out_ref[...] = _dot(mn, w_ref[...].astype(BF16)).astype(BF16)


def _memkv(mem, g, w, tn):
    M, D = mem.shape
    N = w.shape[1]
    return pl.pallas_call(
        _memkv_body,
        grid=(N // tn,),
        in_specs=[_resident(mem.shape), _resident(g.shape), pl.BlockSpec((D, tn), lambda j: (0, j))],
        out_specs=pl.BlockSpec((M, tn), lambda j: (0, j)),
        out_shape=jax.ShapeDtypeStruct((M, N), BF16),
        compiler_params=_params(),
        name="memkv",
    )(mem, g, w)


def _mixout_body(x_ref, dsa_ref, ml_ref, kv_ref, wo_ref, wq_ref, wc_ref, wr_ref, gx_ref, gf_ref,
                 x2_ref, hf_ref, rl_ref, o_ref):
    nd = dsa_ref.shape[1]
    x1 = x_ref[...] + _dot(dsa_ref[...], wo_ref[0:nd, :]) + _dot(ml_ref[...], wo_ref[nd:, :])
    q = _dot(_rms(x1, gx_ref[...]).astype(BF16), wq_ref[...]).astype(BF16)
    for hd in range(X_HEADS):
        sl = slice(hd * X_HEAD_DIM, (hd + 1) * X_HEAD_DIM)
        lg = _dot_nt(q[:, sl], kv_ref[:, sl]) * (X_HEAD_DIM ** -0.5)
        e = jnp.exp(lg - jnp.max(lg, axis=-1, keepdims=True))
        p = e / jnp.sum(e, axis=-1, keepdims=True)
        v = kv_ref[:, D_MODEL + hd * X_HEAD_DIM:D_MODEL + (hd + 1) * X_HEAD_DIM]
        o_ref[:, sl] = _dot(p.astype(BF16), v).astype(BF16)
    x2 = x1 + _dot(o_ref[...], wc_ref[...])
    x2_ref[...] = x2
    hf = _rms(x2, gf_ref[...])
    hf_ref[...] = hf
    rl_ref[...] = _dot(hf.astype(BF16), wr_ref[...])


def _mixout(x, dsa, ml, kv, wo, wq, wc, wr, gx, gf, tm):
    T = x.shape[0]
    row = lambda n: pl.BlockSpec((tm, n), lambda i: (i, 0))
    return pl.pallas_call(
        _mixout_body,
        grid=(T // tm,),
        in_specs=[row(D_MODEL), row(dsa.shape[1]), row(ml.shape[1]), _resident(kv.shape), _resident(wo.shape),
                  _resident(wq.shape), _resident(wc.shape), _resident(wr.shape), _resident(gx.shape),
                  _resident(gf.shape)],
        out_specs=[row(D_MODEL), row(D_MODEL), row(wr.shape[1])],
        out_shape=[jax.ShapeDtypeStruct((T, D_MODEL), F32), jax.ShapeDtypeStruct((T, D_MODEL), F32),
                   jax.ShapeDtypeStruct((T, wr.shape[1]), F32)],
        scratch_shapes=[pltpu.VMEM((tm, D_MODEL), BF16)],
        compiler_params=_params(),
        name="mixout",
    )(x, dsa, ml, kv, wo, wq, wc, wr, gx, gf)


def _moe_body(tok_ref, eb_ref, hf_hbm, wg_ref, wu_ref, wd_ref, y_hbm,
              xbuf, ybuf, wgb, wub, wdb, gsem, ysem, *, bm, n_blk):
    e = pl.program_id(0)
    n_valid = eb_ref[N_EXPERTS]
    b_lo = eb_ref[e]
    b_hi = eb_ref[e + 1]

    def gather(blk, s):
        def issue(g, carry):
            for u in range(_SUBLANES):
                tok = tok_ref[blk * bm + g * _SUBLANES + u]
                pltpu.make_async_copy(hf_hbm.at[pl.ds(tok, 1)], xbuf.at[s, g, pl.ds(u, 1)],
                                      gsem.at[s]).start(priority=1)
            return carry
        lax.fori_loop(0, bm // _SUBLANES, issue, 0)

    def wait_gather(s):
        pltpu.make_async_copy(xbuf.at[s], xbuf.at[s], gsem.at[s]).wait()

    def y_copy(blk, s):
        return pltpu.make_async_copy(ybuf.at[s], y_hbm.at[pl.ds(pl.multiple_of(blk * bm, bm), bm)], ysem.at[s])

    @pl.when(e == 0)
    def _():
        gather(0, 0)

        @pl.when(n_valid > 1)
        def _():
            gather(1, 1)

    @pl.when(b_hi > b_lo)
    def _():
        wgb[...] = wg_ref[0].astype(BF16)
        wub[...] = wu_ref[0].astype(BF16)
        wdb[...] = wd_ref[0].astype(BF16)

        def block(b, carry):
            s = b % _GATHER_SLOTS

            @pl.when(b + 2 < n_valid)
            def _():
                gather(b + 2, (b + 2) % _GATHER_SLOTS)

            wait_gather(s)
            xb = xbuf[s].reshape(bm, xbuf.shape[-1]).astype(BF16)
            gate = _dot(xb, wgb[...])
            a = gate * jax.nn.sigmoid(gate) * _dot(xb, wub[...])
            y = _dot(a.astype(BF16), wdb[...])

            @pl.when(b >= 2)
            def _():
                y_copy(b - 2, b % 2).wait()

            ybuf[b % 2] = y
            y_copy(b, b % 2).start()
            return carry

        lax.fori_loop(b_lo, b_hi, block, 0)

    @pl.when(e == pl.num_programs(0) - 1)
    def _():
        @pl.when(n_valid >= 2)
        def _():
            y_copy(n_valid - 2, n_valid % 2).wait()
        y_copy(n_valid - 1, (n_valid - 1) % 2).wait()
        ybuf[0] = jnp.zeros(ybuf.shape[1:], F32)

        def zero_block(b, carry):
            cp = y_copy(b, 0)
            cp.start()
            cp.wait()
            return carry

        lax.fori_loop(n_valid, n_blk, zero_block, 0)


def _moe(tok, e_blk, hf, wg, wu, wd, n_blk, bm):
    D = hf.shape[1]
    wspec = lambda shape: pl.BlockSpec((1,) + shape, lambda e, *_: (e, 0, 0))
    grid_spec = pltpu.PrefetchScalarGridSpec(
        num_scalar_prefetch=2,
        grid=(N_EXPERTS,),
        in_specs=[pl.BlockSpec(memory_space=pl.ANY),
                  wspec((D, D_EXPERT)), wspec((D, D_EXPERT)), wspec((D_EXPERT, D))],
        out_specs=pl.BlockSpec(memory_space=pl.ANY),
        scratch_shapes=[pltpu.VMEM((_GATHER_SLOTS, bm // _SUBLANES, _SUBLANES, D), F32),
                        pltpu.VMEM((2, bm, D), F32),
                        pltpu.VMEM((D, D_EXPERT), BF16), pltpu.VMEM((D, D_EXPERT), BF16),
                        pltpu.VMEM((D_EXPERT, D), BF16),
                        pltpu.SemaphoreType.DMA((_GATHER_SLOTS,)), pltpu.SemaphoreType.DMA((2,))],
    )
    return pl.pallas_call(
        functools.partial(_moe_body, bm=bm, n_blk=n_blk),
        grid_spec=grid_spec,
        out_shape=jax.ShapeDtypeStruct((n_blk * bm, D), F32),
        compiler_params=_params(),
        name="moe",
    )(tok, e_blk, hf, wg, wu, wd)


def _route(rl, b_group, b_router, bm):
    N = rl.shape[0]
    g_logits = rl[:, :N_GROUPS] + b_group
    g_prob = jax.nn.softmax(g_logits, axis=-1)
    g_sel = jnp.argmax(g_logits, axis=-1)
    p_g = jnp.take_along_axis(g_prob, g_sel[:, None], axis=-1)
    e_logits = (rl[:, N_GROUPS:N_GROUPS + N_EXPERTS] + b_router).reshape(N, N_GROUPS, EXP_PER_GROUP)
    e_logits = jnp.take_along_axis(e_logits, g_sel[:, None, None], axis=1)[:, 0]
    top_p, top_local = lax.top_k(jax.nn.softmax(e_logits, axis=-1), TOPK_IN_GROUP)
    gates = p_g * top_p / top_p.sum(-1, keepdims=True)
    expert_id = (g_sel[:, None] * EXP_PER_GROUP + top_local).astype(I32)

    A = N * TOPK_IN_GROUP
    flat_e = expert_id.reshape(A)
    onehot = (flat_e[:, None] == jnp.arange(N_EXPERTS, dtype=I32)[None, :])
    seg = _RANK_SEG if A % _RANK_SEG == 0 else A
    oh = onehot.astype(BF16).reshape(A // seg, seg, N_EXPERTS)
    before = (jnp.arange(seg)[None, :] < jnp.arange(seg)[:, None]).astype(BF16)
    within = jnp.einsum('ij,bjk->bik', before, oh, preferred_element_type=F32)
    seg_tot = jnp.sum(oh.astype(F32), axis=1)
    seg_base = jnp.cumsum(seg_tot, axis=0) - seg_tot
    rank = jnp.sum((within + seg_base[:, None, :]) * oh.astype(F32), axis=-1).reshape(A).astype(I32)
    counts = jnp.sum(seg_tot, axis=0).astype(I32)
    padded = (counts + bm - 1) // bm * bm
    pad_ends = jnp.cumsum(padded)
    pad_starts = pad_ends - padded
    row = jnp.sum(jnp.where(onehot, pad_starts[None, :], 0), axis=1) + rank
    n_blk = -(-A // bm) + N_EXPERTS
    row_tok = jnp.zeros((n_blk * bm,), I32).at[row].set(jnp.arange(A, dtype=I32) // TOPK_IN_GROUP)
    e_blk = jnp.concatenate([pad_starts, pad_ends[-1:]]) // bm
    return row_tok, row, gates, e_blk.astype(I32), n_blk


def _final_body(row_ref, x_ref, gate_ref, y_hbm, g_ref, out_ref, ybuf, sem, *, tm):
    i = pl.program_id(0)
    slot = i % 2

    def gather(tile, s):
        def issue(g, carry):
            for u in range(_SUBLANES):
                for k in range(TOPK_IN_GROUP):
                    src = row_ref[(tile * tm + g * _SUBLANES + u) * TOPK_IN_GROUP + k]
                    pltpu.make_async_copy(y_hbm.at[pl.ds(src, 1)], ybuf.at[s, k, g, pl.ds(u, 1)],
                                          sem.at[s]).start()
            return carry
        lax.fori_loop(0, tm // _SUBLANES, issue, 0)

    @pl.when(i == 0)
    def _():
        gather(0, 0)

    @pl.when(i + 1 < pl.num_programs(0))
    def _():
        gather(i + 1, 1 - slot)

    pltpu.make_async_copy(ybuf.at[slot], ybuf.at[slot], sem.at[slot]).wait()
    acc = x_ref[...]
    for k in range(TOPK_IN_GROUP):
        acc = acc + gate_ref[:, k:k + 1] * ybuf[slot, k].reshape(tm, ybuf.shape[-1])
    out_ref[...] = _rms(acc, g_ref[...])


def _final(row, x2, gates, y_rows, g, tm):
    T, D = x2.shape
    grid_spec = pltpu.PrefetchScalarGridSpec(
        num_scalar_prefetch=1,
        grid=(T // tm,),
        in_specs=[pl.BlockSpec((tm, D), lambda i, *_: (i, 0)),
                  pl.BlockSpec((tm, TOPK_IN_GROUP), lambda i, *_: (i, 0)),
                  pl.BlockSpec(memory_space=pl.ANY),
                  pl.BlockSpec(g.shape, lambda i, *_: (0, 0))],
        out_specs=pl.BlockSpec((tm, D), lambda i, *_: (i, 0)),
        scratch_shapes=[pltpu.VMEM((2, TOPK_IN_GROUP, tm // _SUBLANES, _SUBLANES, D), F32),
                        pltpu.SemaphoreType.DMA((2,))],
    )
    return pl.pallas_call(
        functools.partial(_final_body, tm=tm),
        grid_spec=grid_spec,
        out_shape=jax.ShapeDtypeStruct((T, D), F32),
        compiler_params=_params(),
        name="final",
    )(row, x2, gates, y_rows, g)


def _tile_sizes(T):
    pick = lambda want: want if T % want == 0 else CHUNK
    return dict(inproj=pick(256), dsa_q=pick(256), dsa_k=pick(512), mlstm=pick(256), mixout=pick(256),
                final=pick(256), moe=128)


def _layer(x, mem, norm_mix_g, w_in, kv_norm_g, k_idx_norm_g, w_uk, w_uv, conv_w, conv_b, gate_b, ml_norm_g,
           w_out, norm_x_g, mem_norm_g, w_cq, w_ckv, w_co, norm_ffn_g, w_group, b_group, w_router, b_router,
           w_gate, w_up, w_down, out_g):
    T = x.shape[0]
    ts = _tile_sizes(T)
    r2 = lambda v: v.reshape(1, -1)

    w_b = w_in.astype(BF16)
    w_r = jnp.concatenate([
        w_b[:, _O_DQ:_O_MQ], w_b[:, _O_MI:_O_MO],
        jnp.zeros((D_MODEL, _SMALL - IDX_DIM - IDX_HEADS - 2 * ML_HEADS), BF16),
        w_b[:, _O_MQ:_O_MI], w_b[:, _O_MO:_O_END]], axis=1)
    wuk_t = jnp.transpose(w_uk, (1, 2, 0)).astype(BF16)
    wuv_t = jnp.transpose(w_uv, (1, 0, 2)).astype(BF16)

    qabs, ckv, qi, kidx, small, mqk, mv, mo = _inproj(
        x, r2(norm_mix_g), w_r, wuk_t, r2(kv_norm_g), r2(k_idx_norm_g), ts["inproj"])

    gate_rows = jnp.transpose(small[:, _S_WI:_S_MF + ML_HEADS])
    dsa_out = _dsa(qi, gate_rows, qabs, kidx, ckv, wuv_t, ts["dsa_q"], ts["dsa_k"])

    gb_col = jnp.zeros((1, _SMALL), F32).at[0, _S_MI:_S_MI + 2 * ML_HEADS].set(gate_b)
    ml_out = _mlstm(mqk, mv, small, gate_rows, mo, conv_w, r2(conv_b), gb_col, gate_b.reshape(-1, 1),
                    r2(ml_norm_g), ts["mlstm"])

    kv = _memkv(mem, r2(mem_norm_g), w_ckv, 512)
    w_rt = jnp.concatenate([w_group, w_router,
                            jnp.zeros((D_MODEL, 128 - N_GROUPS - N_EXPERTS), w_group.dtype)], axis=1)
    x2, hf, rl = _mixout(x, dsa_out, ml_out, kv, w_out.astype(BF16), w_cq.astype(BF16), w_co.astype(BF16),
                         w_rt.astype(BF16), r2(norm_x_g), r2(norm_ffn_g), ts["mixout"])

    bm = ts["moe"]
    row_tok, row, gates, e_blk, n_blk = _route(rl, b_group, b_router, bm)
    y_rows = _moe(row_tok, e_blk, hf, w_gate, w_up, w_down, n_blk, bm)
    return _final(row, x2, gates, y_rows, r2(out_g), ts["final"])


def kernel(x, mem, norm_mix_g, w_in, kv_norm_g, k_idx_norm_g, w_uk, w_uv, conv_w, conv_b, gate_b, ml_norm_g,
           w_out, norm_x_g, mem_norm_g, w_cq, w_ckv, w_co, norm_ffn_g, w_group, b_group, w_router, b_router,
           w_gate, w_up, w_down, final_norm_g):
    B, T, D = x.shape
    assert B == 1 and D == D_MODEL and norm_mix_g.shape[0] == 1 and T % CHUNK == 0
    out = _layer(x[0], mem[0], norm_mix_g[0], w_in[0], kv_norm_g[0], k_idx_norm_g[0], w_uk[0], w_uv[0],
                 conv_w[0], conv_b[0], gate_b[0], ml_norm_g[0], w_out[0], norm_x_g[0], mem_norm_g[0],
                 w_cq[0], w_ckv[0], w_co[0], norm_ffn_g[0], w_group[0], b_group[0], w_router[0], b_router[0],
                 w_gate[0], w_up[0], w_down[0], final_norm_g)
    return out[None]
```

```python
import functools

import jax
import jax.numpy as jnp
import numpy as np
from jax import lax
from jax.experimental import pallas as pl
from jax.experimental.pallas import tpu as pltpu

F32 = jnp.float32
BF16 = jnp.bfloat16
I32 = jnp.int32
I16 = jnp.int16

EPS = 1e-6
CHUNK = 64
D_MODEL = 2048

DSA_HEADS = 8
DSA_HEAD_DIM = 128
DSA_LATENT = 256
IDX_HEADS = 8
IDX_DIM = 64
TOPK_MAX = 256

ML_HEADS = 4
ML_QK_DIM = 128
ML_V_DIM = 256
CONV_W = 4

X_HEADS = 4
X_HEAD_DIM = D_MODEL // X_HEADS

N_GROUPS = 4
EXP_PER_GROUP = 8
N_EXPERTS = N_GROUPS * EXP_PER_GROUP
TOPK_IN_GROUP = 2
D_EXPERT = 512

_O_DQ = 0
_O_CKV = _O_DQ + DSA_HEADS * DSA_HEAD_DIM
_O_QI = _O_CKV + DSA_LATENT
_O_KI = _O_QI + IDX_HEADS * IDX_DIM
_O_WI = _O_KI + IDX_DIM
_O_MQ = _O_WI + IDX_HEADS
_O_MK = _O_MQ + ML_HEADS * ML_QK_DIM
_O_MV = _O_MK + ML_HEADS * ML_QK_DIM
_O_MI = _O_MV + ML_HEADS * ML_V_DIM
_O_MF = _O_MI + ML_HEADS
_O_MO = _O_MF + ML_HEADS
_O_END = _O_MO + ML_HEADS * ML_V_DIM

_G_DQ = (0, 1024)
_G_CKV = (1024, 1280)
_G_QI = (1280, 1792)
_G_SMALL = (1792, 1920)
_G_MQK = (1920, 2944)
_G_MV = (2944, 3968)
_G_MO = (3968, 4992)
_W_COLS = 4992
_S_WI = IDX_DIM
_S_MI = _S_WI + IDX_HEADS
_S_MF = _S_MI + ML_HEADS
_SMALL = 128

_VMEM_LIMIT = 56 * 1024 * 1024
_INT_MIN = -(2 ** 31)
_I16_MIN = -(2 ** 15)
_CHUNK_SHIFT = CHUNK.bit_length() - 1
_LOG2E = 1.4426950408889634
_SUBLANES = 8
_GATHER_SLOTS = 3
_RANK_SEG = 512
_NEG = -1e30


def _rms(v, g):
    return v * lax.rsqrt(jnp.mean(v * v, axis=-1, keepdims=True) + EPS) * g


def _dot(a, b):
    return jnp.dot(a, b, preferred_element_type=F32)


def _dot_nt(a, b):
    return lax.dot_general(a, b, (((1,), (1,)), ((), ())), preferred_element_type=F32)


def _resident(shape):
    nd = len(shape)
    return pl.BlockSpec(shape, lambda *_: (0,) * nd, pipeline_mode=pl.Buffered(1))


def _params(n_axes=1):
    return pltpu.CompilerParams(dimension_semantics=("arbitrary",) * n_axes,
                                vmem_limit_bytes=_VMEM_LIMIT)


def _inproj_body(x_ref, g_ref, w_ref, wuk_ref, kvg_ref, kig_ref,
                 qabs_ref, ckv_ref, qi_ref, kidx_ref, small_ref, mqk_ref, mv_ref, mo_ref):
    h = _rms(x_ref[...], g_ref[...]).astype(BF16)

    def proj(grp):
        return _dot(h, w_ref[:, grp[0]:grp[1]])

    dq = proj(_G_DQ)
    for hd in range(DSA_HEADS):
        qh = dq[:, hd * DSA_HEAD_DIM:(hd + 1) * DSA_HEAD_DIM].astype(BF16)
        qa = _dot(qh, wuk_ref[hd]) * (DSA_HEAD_DIM ** -0.5 * _LOG2E)
        qabs_ref[:, hd * DSA_LATENT:(hd + 1) * DSA_LATENT] = qa.astype(BF16)
    ckv_ref[...] = _rms(proj(_G_CKV), kvg_ref[...]).astype(BF16)
    qi_ref[...] = (proj(_G_QI) * (IDX_DIM ** -0.5)).astype(BF16)
    small = proj(_G_SMALL)
    small_ref[...] = small
    kidx_ref[...] = _rms(small[:, :IDX_DIM], kig_ref[...]).astype(BF16)
    mqk_ref[...] = proj(_G_MQK)
    mv_ref[...] = proj(_G_MV).astype(BF16)
    mo_ref[...] = proj(_G_MO)


def _inproj(x, g, w, wuk, kvg, kig, tm):
    T = x.shape[0]
    row = lambda n: pl.BlockSpec((tm, n), lambda i: (i, 0))
    outs = [(8 * DSA_LATENT, BF16), (DSA_LATENT, BF16), (IDX_HEADS * IDX_DIM, BF16), (IDX_DIM, BF16),
            (_SMALL, F32), (2 * ML_HEADS * ML_QK_DIM, F32), (ML_HEADS * ML_V_DIM, BF16),
            (ML_HEADS * ML_V_DIM, F32)]
    return pl.pallas_call(
        _inproj_body,
        grid=(T // tm,),
        in_specs=[row(D_MODEL), _resident(g.shape), _resident(w.shape), _resident(wuk.shape),
                  _resident(kvg.shape), _resident(kig.shape)],
        out_specs=[row(n) for n, _ in outs],
        out_shape=[jax.ShapeDtypeStruct((T, n), dt) for n, dt in outs],
        compiler_params=_params(),
        name="inproj",
    )(x, g, w, wuk, kvg, kig)


def _sublane_fold(v, op):
    acc = v[0:8, :]
    for r in range(1, v.shape[0] // 8):
        acc = op(acc, v[r * 8:(r + 1) * 8, :])
    return acc


def _dsa_body(qi_ref, wrow_ref, qabs_ref, kidx_ref, ckv_ref, ckvt_ref, wuv_ref, out_ref,
              key_ref, hi_ref, lga_ref, lgb_ref, m_ref, l_ref, acc_ref, *, tq, tk, topk, nbits_idx):
    i = pl.program_id(0)
    n_kb = ((i + 1) * tq + tk - 1) // tk
    w_rows = wrow_ref[0:IDX_HEADS, :] * (IDX_HEADS ** -0.5)
    q_chunk = (i * tq + lax.broadcasted_iota(I32, (1, tq), 1)) >> _CHUNK_SHIFT

    def key_pos(j):
        return j * tk + lax.broadcasted_iota(I32, (tk, 1), 0)

    def score_block(j, carry):
        kx = kidx_ref[pl.ds(pl.multiple_of(j * tk, tk), tk), :]
        s = jnp.zeros((tk, tq), F32)
        for hd in range(IDX_HEADS):
            d = _dot_nt(kx, qi_ref[:, hd * IDX_DIM:(hd + 1) * IDX_DIM])
            s = s + w_rows[hd:hd + 1, :] * jnp.maximum(d, 0.0)
        bits = lax.bitcast_convert_type(s, I32)
        key = bits ^ ((bits >> 31) & 0x7FFFFFFF)
        key_ref[j] = jnp.where((key_pos(j) >> _CHUNK_SHIFT) <= q_chunk, key, _INT_MIN)
        return carry

    lax.fori_loop(0, n_kb, score_block, 0)

    def count(pred):
        def body(j, acc):
            hit = pred(key_ref[j], key_pos(j)).astype(I32)
            return acc + _sublane_fold(hit, jnp.add)
        acc = lax.fori_loop(0, n_kb, body, jnp.zeros((8, tq), I32))
        return jnp.sum(acc, axis=0, keepdims=True)

    def count16(ref, cand):
        c16 = cand.astype(I16)
        def body(j, acc):
            hit = jnp.where(ref[j] >= c16, jnp.int16(1), jnp.int16(0))
            part = hit[0:16, :]
            for r in range(1, tk // 16):
                part = part + hit[r * 16:(r + 1) * 16, :]
            return acc + part
        acc = lax.fori_loop(0, n_kb, body, jnp.zeros((16, tq), I16))
        return jnp.sum(acc.astype(I32), axis=0, keepdims=True)

    def kth_largest16(ref, kth):
        def bit(b, t):
            cand = t + lax.shift_left(jnp.int32(1), 15 - b)
            return jnp.where(count16(ref, cand) >= kth, cand, t)
        return lax.fori_loop(0, 16, bit, jnp.full((1, tq), _I16_MIN, I32))

    def split_block(j, carry):
        kb = key_ref[j]
        hi_ref[j] = (kb >> 16).astype(I16)
        return carry

    lax.fori_loop(0, n_kb, split_block, 0)
    t_hi = kth_largest16(hi_ref, topk)
    n_above = count16(hi_ref, t_hi + 1)

    def low_block(j, carry):
        kb = key_ref[j]
        lo = ((kb & 0xFFFF) + _I16_MIN).astype(I16)
        hi_ref[j] = jnp.where((kb >> 16) == t_hi, lo, jnp.int16(_I16_MIN))
        return carry

    lax.fori_loop(0, n_kb, low_block, 0)
    t_lo = kth_largest16(hi_ref, topk - n_above)
    t = lax.shift_left(t_hi, 16) + (t_lo - _I16_MIN)
    t = jnp.maximum(t, _INT_MIN + 1)
    n_ge = count(lambda kb, pos: kb >= t)
    n_gt = count(lambda kb, pos: kb > t)
    all_pos = jnp.int32(2 ** nbits_idx - 1)
    n_tie_take = jnp.where(n_ge > topk, topk - n_gt, all_pos)

    def tie_cutoff():
        def pos_bit(b, c):
            cand = c + lax.shift_left(jnp.int32(1), nbits_idx - 1 - b)
            f = count(lambda kb, pos: (kb == t) & (pos < cand))
            return jnp.where(f <= n_tie_take, cand, c)
        return lax.fori_loop(0, nbits_idx, pos_bit, jnp.zeros((1, tq), I32))

    cut = lax.cond(jnp.max(n_ge) > topk, tie_cutoff, lambda: jnp.full((1, tq), all_pos, I32))

    def bias_block(j, carry):
        kb = key_ref[j]
        sel = (kb > t) | ((kb == t) & (key_pos(j) < cut))
        key_ref[j] = lax.bitcast_convert_type(jnp.where(sel, 0.0, _NEG).astype(F32), I32)
        return carry

    lax.fori_loop(0, n_kb, bias_block, 0)

    n_pair = (n_kb + 1) // 2
    neg_bits = lax.bitcast_convert_type(jnp.full((tk, tq), _NEG, F32), I32)

    def masked_block(j, carry):
        key_ref[j] = neg_bits
        return carry

    lax.fori_loop(n_kb, 2 * n_pair, masked_block, 0)

    m_ref[...] = jnp.full(m_ref.shape, _NEG, F32)
    l_ref[...] = jnp.zeros(l_ref.shape, F32)
    acc_ref[...] = jnp.zeros(acc_ref.shape, F32)
    last_blk = ckvt_ref.shape[0] - 1

    def logits(j, lg_buf):
        c_blk = ckv_ref[pl.ds(pl.multiple_of(jnp.minimum(j, last_blk) * tk, tk), tk), :]
        bias = lax.bitcast_convert_type(key_ref[j], F32)
        for hd in range(DSA_HEADS):
            lg_buf[hd] = _dot_nt(c_blk, qabs_ref[:, hd * DSA_LATENT:(hd + 1) * DSA_LATENT]) + bias

    def accumulate(j, lg_buf):
        c_blk_t = ckvt_ref[jnp.minimum(j, last_blk)]
        for hd in range(DSA_HEADS):
            lg = lg_buf[hd]
            m_old = m_ref[hd:hd + 1, :]
            m_new = jnp.maximum(m_old, jnp.max(_sublane_fold(lg, jnp.maximum), axis=0, keepdims=True))
            p = jnp.exp2(lg - m_new)
            alpha = jnp.exp2(m_old - m_new)
            l_ref[hd:hd + 1, :] = alpha * l_ref[hd:hd + 1, :] + jnp.sum(_sublane_fold(p, jnp.add), axis=0,
                                                                         keepdims=True)
            acc_ref[hd] = alpha * acc_ref[hd] + _dot(c_blk_t, p.astype(BF16))
            m_ref[hd:hd + 1, :] = m_new

    logits(0, lga_ref)

    def attn_pair(mi, carry):
        ja = 2 * mi
        accumulate(ja, lga_ref)
        logits(ja + 1, lgb_ref)
        accumulate(ja + 1, lgb_ref)
        logits(jnp.minimum(ja + 2, 2 * n_pair - 1), lga_ref)
        return carry

    lax.fori_loop(0, n_pair, attn_pair, 0)

    for hd in range(DSA_HEADS):
        o_lat = (acc_ref[hd] / l_ref[hd:hd + 1, :]).T.astype(BF16)
        out_ref[:, hd * DSA_HEAD_DIM:(hd + 1) * DSA_HEAD_DIM] = _dot(o_lat, wuv_ref[hd]).astype(BF16)


def _dsa(qi, wrows, qabs, kidx, ckv, wuv, tq, tk):
    T = qi.shape[0]
    topk = min(TOPK_MAX, T // 4)
    n_kb = T // tk
    ckvt = jnp.transpose(ckv.reshape(n_kb, tk, DSA_LATENT), (0, 2, 1))
    row = lambda n: pl.BlockSpec((tq, n), lambda i: (i, 0))
    body = functools.partial(_dsa_body, tq=tq, tk=tk, topk=topk, nbits_idx=int(T).bit_length())
    return pl.pallas_call(
        body,
        grid=(T // tq,),
        in_specs=[row(qi.shape[1]), pl.BlockSpec((wrows.shape[0], tq), lambda i: (0, i)), row(qabs.shape[1]),
                  _resident(kidx.shape), _resident(ckv.shape), _resident(ckvt.shape), _resident(wuv.shape)],
        out_specs=row(DSA_HEADS * DSA_HEAD_DIM),
        out_shape=jax.ShapeDtypeStruct((T, DSA_HEADS * DSA_HEAD_DIM), BF16),
        scratch_shapes=[pltpu.VMEM((n_kb + n_kb % 2, tk, tq), I32), pltpu.VMEM((n_kb, tk, tq), I16),
                        pltpu.VMEM((DSA_HEADS, tk, tq), F32), pltpu.VMEM((DSA_HEADS, tk, tq), F32),
                        pltpu.VMEM((DSA_HEADS, tq), F32),
                        pltpu.VMEM((DSA_HEADS, tq), F32), pltpu.VMEM((DSA_HEADS, DSA_LATENT, tq), F32)],
        compiler_params=_params(),
        name="dsa",
    )(qi, wrows, qabs, kidx, ckv, ckvt, wuv)


def _log_sigmoid(v):
    return jnp.minimum(v, 0.0) - jnp.log1p(jnp.exp(-jnp.abs(v)))


def _chunk_cumsum(v, axis):
    pos = lax.broadcasted_iota(I32, v.shape, axis) & (CHUNK - 1)
    d = 1
    while d < CHUNK:
        v = v + jnp.where(pos >= d, pltpu.roll(v, d, axis=axis), 0.0)
        d *= 2
    return v


def _mlstm_body(mqk_ref, mv_ref, small_ref, gt_ref, mo_ref, cw_ref, cb_ref, gbc_ref, gbr_ref, ng_ref,
                out_ref, xe_ref, c_ref, n_ref, m_ref, hs_ref, *, rows):
    @pl.when(pl.program_id(0) == 0)
    def _():
        xe_ref[0:8, :] = jnp.zeros((8, xe_ref.shape[1]), F32)
        c_ref[...] = jnp.zeros(c_ref.shape, F32)
        n_ref[...] = jnp.zeros(n_ref.shape, F32)
        m_ref[...] = jnp.zeros(m_ref.shape, F32)

    x = mqk_ref[...]
    xe_ref[8:8 + rows, :] = x
    y = cb_ref[...]
    for j in range(CONV_W - 1):
        y = y + xe_ref[5 + j:5 + j + rows, :] * cw_ref[j:j + 1, :]
    y = y + x * cw_ref[CONV_W - 1:CONV_W, :]
    xe_ref[0:8, :] = x[rows - 8:rows, :]
    qk = y * jax.nn.sigmoid(y)
    nqk = ML_HEADS * ML_QK_DIM
    q_all = (qk[:, :nqk] * (ML_QK_DIM ** -0.5)).astype(BF16)
    k_all = qk[:, nqk:]

    g_col = small_ref[...] + gbc_ref[...]
    g_row = gt_ref[...] + gbr_ref[...]
    b_col = _chunk_cumsum(_log_sigmoid(g_col), 0)
    b_row = _chunk_cumsum(_log_sigmoid(g_row), 1)

    tri = lax.broadcasted_iota(I32, (CHUNK, CHUNK), 1) <= lax.broadcasted_iota(I32, (CHUNK, CHUNK), 0)

    for c in range(rows // CHUNK):
        lo, hi = c * CHUNK, (c + 1) * CHUNK
        for hd in range(ML_HEADS):
            bc = b_col[lo:hi, _S_MF + hd:_S_MF + hd + 1]
            lic = g_col[lo:hi, _S_MI + hd:_S_MI + hd + 1]
            br = b_row[ML_HEADS + hd:ML_HEADS + hd + 1, lo:hi]
            lir = g_row[hd:hd + 1, lo:hi]
            g_tot = bc[CHUNK - 1:CHUNK, :]
            m_prev = m_ref[hd][:, 0:1]

            dmat = jnp.where(tri, bc - br + lir, -jnp.inf)
            inter = bc + m_prev
            m_t = jnp.maximum(inter, jnp.max(dmat, axis=-1, keepdims=True))
            w_intra = jnp.exp(dmat - m_t)
            a_inter = jnp.exp(inter - m_t)

            qh = q_all[lo:hi, hd * ML_QK_DIM:(hd + 1) * ML_QK_DIM]
            kh = k_all[lo:hi, hd * ML_QK_DIM:(hd + 1) * ML_QK_DIM]
            vh = mv_ref[lo:hi, hd * ML_V_DIM:(hd + 1) * ML_V_DIM]
            s_qk = _dot_nt(qh, kh.astype(BF16)) * w_intra
            c_prev = c_ref[hd]
            n_prev = n_ref[hd]
            num = a_inter * _dot(qh, c_prev.astype(BF16)) + _dot(s_qk.astype(BF16), vh)
            den = (a_inter * jnp.sum(qh.astype(F32) * n_prev, axis=-1, keepdims=True)
                   + jnp.sum(s_qk, axis=-1, keepdims=True))
            hs_ref[lo:hi, hd * ML_V_DIM:(hd + 1) * ML_V_DIM] = (
                num / jnp.maximum(jnp.abs(den), jnp.exp(-m_t)))

            m_new = jnp.maximum(g_tot + m_prev, jnp.max(g_tot - br + lir, axis=-1, keepdims=True))
            a_state = jnp.exp(g_tot + m_prev - m_new)
            wk = jnp.exp(g_tot - bc + lic - m_new) * kh
            c_ref[hd] = a_state * c_prev + _dot(wk.T.astype(BF16), vh)
            n_ref[hd] = a_state * n_prev + jnp.sum(wk, axis=0, keepdims=True)
            m_ref[hd] = jnp.broadcast_to(m_new, m_ref.shape[1:])

    for hd in range(ML_HEADS):
        sl = slice(hd * ML_V_DIM, (hd + 1) * ML_V_DIM)
        out_ref[:, sl] = (_rms(hs_ref[:, sl], ng_ref[:, sl]) * jax.nn.sigmoid(mo_ref[:, sl])).astype(BF16)


def _mlstm(mqk, mv, small, gt, mo, cw, cb, gbc, gbr, ng, rows):
    T = mqk.shape[0]
    row = lambda n: pl.BlockSpec((rows, n), lambda i: (i, 0))
    nv = ML_HEADS * ML_V_DIM
    return pl.pallas_call(
        functools.partial(_mlstm_body, rows=rows),
        grid=(T // rows,),
        in_specs=[row(mqk.shape[1]), row(nv), row(_SMALL), pl.BlockSpec((8, rows), lambda i: (1, i)), row(nv),
                  _resident(cw.shape), _resident(cb.shape), _resident(gbc.shape), _resident(gbr.shape),
                  _resident(ng.shape)],
        out_specs=row(nv),
        out_shape=jax.ShapeDtypeStruct((T, nv), BF16),
        scratch_shapes=[pltpu.VMEM((rows + 8, mqk.shape[1]), F32),
                        pltpu.VMEM((ML_HEADS, ML_QK_DIM, ML_V_DIM), F32),
                        pltpu.VMEM((ML_HEADS, 1, ML_QK_DIM), F32),
                        pltpu.VMEM((ML_HEADS, 1, 128), F32),
                        pltpu.VMEM((rows, nv), F32)],
        compiler_params=_params(),
        name="mlstm",
    )(mqk, mv, small, gt, mo, cw, cb, gbc, gbr, ng)


def _memkv_body(mem_ref, g_ref, w_ref, out_ref):
    mn = _rms(mem_ref[...], g_ref[...]).astype(BF16)
    out_ref[...] = _dot(mn, w_ref[...].astype(BF16)).astype(BF16)


def _memkv(mem, g, w, tn):
    M, D = mem.shape
    N = w.shape[1]
    return pl.pallas_call(
        _memkv_body,
        grid=(N // tn,),
        in_specs=[_resident(mem.shape), _resident(g.shape), pl.BlockSpec((D, tn), lambda j: (0, j))],
        out_specs=pl.BlockSpec((M, tn), lambda j: (0, j)),
        out_shape=jax.ShapeDtypeStruct((M, N), BF16),
        compiler_params=_params(),
        name="memkv",
    )(mem, g, w)


def _mixout_body(x_ref, dsa_ref, ml_ref, kv_ref, wo_ref, wq_ref, wc_ref, wr_ref, gx_ref, gf_ref,
                 x2_ref, hf_ref, rl_ref, o_ref):
    nd = dsa_ref.shape[1]
    x1 = x_ref[...] + _dot(dsa_ref[...], wo_ref[0:nd, :]) + _dot(ml_ref[...], wo_ref[nd:, :])
    q = _dot(_rms(x1, gx_ref[...]).astype(BF16), wq_ref[...]).astype(BF16)
    for hd in range(X_HEADS):
        sl = slice(hd * X_HEAD_DIM, (hd + 1) * X_HEAD_DIM)
        lg = _dot_nt(q[:, sl], kv_ref[:, sl]) * (X_HEAD_DIM ** -0.5)
        e = jnp.exp(lg - jnp.max(lg, axis=-1, keepdims=True))
        p = e / jnp.sum(e, axis=-1, keepdims=True)
        v = kv_ref[:, D_MODEL + hd * X_HEAD_DIM:D_MODEL + (hd + 1) * X_HEAD_DIM]
        o_ref[:, sl] = _dot(p.astype(BF16), v).astype(BF16)
    x2 = x1 + _dot(o_ref[...], wc_ref[...])
    x2_ref[...] = x2
    hf = _rms(x2, gf_ref[...])
    hf_ref[...] = hf
    rl_ref[...] = _dot(hf.astype(BF16), wr_ref[...])


def _mixout(x, dsa, ml, kv, wo, wq, wc, wr, gx, gf, tm):
    T = x.shape[0]
    row = lambda n: pl.BlockSpec((tm, n), lambda i: (i, 0))
    return pl.pallas_call(
        _mixout_body,
        grid=(T // tm,),
        in_specs=[row(D_MODEL), row(dsa.shape[1]), row(ml.shape[1]), _resident(kv.shape), _resident(wo.shape),
                  _resident(wq.shape), _resident(wc.shape), _resident(wr.shape), _resident(gx.shape),
                  _resident(gf.shape)],
        out_specs=[row(D_MODEL), row(D_MODEL), row(wr.shape[1])],
        out_shape=[jax.ShapeDtypeStruct((T, D_MODEL), F32), jax.ShapeDtypeStruct((T, D_MODEL), F32),
                   jax.ShapeDtypeStruct((T, wr.shape[1]), F32)],
        scratch_shapes=[pltpu.VMEM((tm, D_MODEL), BF16)],
        compiler_params=_params(),
        name="mixout",
    )(x, dsa, ml, kv, wo, wq, wc, wr, gx, gf)


def _moe_body(tok_ref, eb_ref, hf_hbm, wg_ref, wu_ref, wd_ref, y_hbm,
              xbuf, ybuf, wgb, wub, wdb, gsem, ysem, *, bm, n_blk):
    e = pl.program_id(0)
    n_valid = eb_ref[N_EXPERTS]
    b_lo = eb_ref[e]
    b_hi = eb_ref[e + 1]

    def gather(blk, s):
        def issue(g, carry):
            for u in range(_SUBLANES):
                tok = tok_ref[blk * bm + g * _SUBLANES + u]
                pltpu.make_async_copy(hf_hbm.at[pl.ds(tok, 1)], xbuf.at[s, g, pl.ds(u, 1)],
                                      gsem.at[s]).start(priority=1)
            return carry
        lax.fori_loop(0, bm // _SUBLANES, issue, 0)

    def wait_gather(s):
        pltpu.make_async_copy(xbuf.at[s], xbuf.at[s], gsem.at[s]).wait()

    def y_copy(blk, s):
        return pltpu.make_async_copy(ybuf.at[s], y_hbm.at[pl.ds(pl.multiple_of(blk * bm, bm), bm)], ysem.at[s])

    @pl.when(e == 0)
    def _():
        gather(0, 0)

        @pl.when(n_valid > 1)
        def _():
            gather(1, 1)

    @pl.when(b_hi > b_lo)
    def _():
        wgb[...] = wg_ref[0].astype(BF16)
        wub[...] = wu_ref[0].astype(BF16)
        wdb[...] = wd_ref[0].astype(BF16)

        def block(b, carry):
            s = b % _GATHER_SLOTS

            @pl.when(b + 2 < n_valid)
            def _():
                gather(b + 2, (b + 2) % _GATHER_SLOTS)

            wait_gather(s)
            xb = xbuf[s].reshape(bm, xbuf.shape[-1]).astype(BF16)
            gate = _dot(xb, wgb[...])
            a = gate * jax.nn.sigmoid(gate) * _dot(xb, wub[...])
            y = _dot(a.astype(BF16), wdb[...])

            @pl.when(b >= 2)
            def _():
                y_copy(b - 2, b % 2).wait()

            ybuf[b % 2] = y
            y_copy(b, b % 2).start()
            return carry

        lax.fori_loop(b_lo, b_hi, block, 0)

    @pl.when(e == pl.num_programs(0) - 1)
    def _():
        @pl.when(n_valid >= 2)
        def _():
            y_copy(n_valid - 2, n_valid % 2).wait()
        y_copy(n_valid - 1, (n_valid - 1) % 2).wait()
        ybuf[0] = jnp.zeros(ybuf.shape[1:], F32)

        def zero_block(b, carry):
            cp = y_copy(b, 0)
            cp.start()
            cp.wait()
            return carry

        lax.fori_loop(n_valid, n_blk, zero_block, 0)


def _moe(tok, e_blk, hf, wg, wu, wd, n_blk, bm):
    D = hf.shape[1]
    wspec = lambda shape: pl.BlockSpec((1,) + shape, lambda e, *_: (e, 0, 0))
    grid_spec = pltpu.PrefetchScalarGridSpec(
        num_scalar_prefetch=2,
        grid=(N_EXPERTS,),
        in_specs=[pl.BlockSpec(memory_space=pl.ANY),
                  wspec((D, D_EXPERT)), wspec((D, D_EXPERT)), wspec((D_EXPERT, D))],
        out_specs=pl.BlockSpec(memory_space=pl.ANY),
        scratch_shapes=[pltpu.VMEM((_GATHER_SLOTS, bm // _SUBLANES, _SUBLANES, D), F32),
                        pltpu.VMEM((2, bm, D), F32),
                        pltpu.VMEM((D, D_EXPERT), BF16), pltpu.VMEM((D, D_EXPERT), BF16),
                        pltpu.VMEM((D_EXPERT, D), BF16),
                        pltpu.SemaphoreType.DMA((_GATHER_SLOTS,)), pltpu.SemaphoreType.DMA((2,))],
    )
    return pl.pallas_call(
        functools.partial(_moe_body, bm=bm, n_blk=n_blk),
        grid_spec=grid_spec,
        out_shape=jax.ShapeDtypeStruct((n_blk * bm, D), F32),
        compiler_params=_params(),
        name="moe",
    )(tok, e_blk, hf, wg, wu, wd)


def _route(rl, b_group, b_router, bm):
    N = rl.shape[0]
    g_logits = rl[:, :N_GROUPS] + b_group
    g_prob = jax.nn.softmax(g_logits, axis=-1)
    g_sel = jnp.argmax(g_logits, axis=-1)
    p_g = jnp.take_along_axis(g_prob, g_sel[:, None], axis=-1)
    e_logits = (rl[:, N_GROUPS:N_GROUPS + N_EXPERTS] + b_router).reshape(N, N_GROUPS, EXP_PER_GROUP)
    e_logits = jnp.take_along_axis(e_logits, g_sel[:, None, None], axis=1)[:, 0]
    top_p, top_local = lax.top_k(jax.nn.softmax(e_logits, axis=-1), TOPK_IN_GROUP)
    gates = p_g * top_p / top_p.sum(-1, keepdims=True)
    expert_id = (g_sel[:, None] * EXP_PER_GROUP + top_local).astype(I32)

    A = N * TOPK_IN_GROUP
    flat_e = expert_id.reshape(A)
    onehot = (flat_e[:, None] == jnp.arange(N_EXPERTS, dtype=I32)[None, :])
    seg = _RANK_SEG if A % _RANK_SEG == 0 else A
    oh = onehot.astype(BF16).reshape(A // seg, seg, N_EXPERTS)
    before = (jnp.arange(seg)[None, :] < jnp.arange(seg)[:, None]).astype(BF16)
    within = jnp.einsum('ij,bjk->bik', before, oh, preferred_element_type=F32)
    seg_tot = jnp.sum(oh.astype(F32), axis=1)
    seg_base = jnp.cumsum(seg_tot, axis=0) - seg_tot
    rank = jnp.sum((within + seg_base[:, None, :]) * oh.astype(F32), axis=-1).reshape(A).astype(I32)
    counts = jnp.sum(seg_tot, axis=0).astype(I32)
    padded = (counts + bm - 1) // bm * bm
    pad_ends = jnp.cumsum(padded)
    pad_starts = pad_ends - padded
    row = jnp.sum(jnp.where(onehot, pad_starts[None, :], 0), axis=1) + rank
    n_blk = -(-A // bm) + N_EXPERTS
    row_tok = jnp.zeros((n_blk * bm,), I32).at[row].set(jnp.arange(A, dtype=I32) // TOPK_IN_GROUP)
    e_blk = jnp.concatenate([pad_starts, pad_ends[-1:]]) // bm
    return row_tok, row, gates, e_blk.astype(I32), n_blk


def _final_body(row_ref, x_ref, gate_ref, y_hbm, g_ref, out_ref, ybuf, sem, *, tm):
    i = pl.program_id(0)
    slot = i % 2

    def gather(tile, s):
        def issue(g, carry):
            for u in range(_SUBLANES):
                for k in range(TOPK_IN_GROUP):
                    src = row_ref[(tile * tm + g * _SUBLANES + u) * TOPK_IN_GROUP + k]
                    pltpu.make_async_copy(y_hbm.at[pl.ds(src, 1)], ybuf.at[s, k, g, pl.ds(u, 1)],
                                          sem.at[s]).start()
            return carry
        lax.fori_loop(0, tm // _SUBLANES, issue, 0)

    @pl.when(i == 0)
    def _():
        gather(0, 0)

    @pl.when(i + 1 < pl.num_programs(0))
    def _():
        gather(i + 1, 1 - slot)

    pltpu.make_async_copy(ybuf.at[slot], ybuf.at[slot], sem.at[slot]).wait()
    acc = x_ref[...]
    for k in range(TOPK_IN_GROUP):
        acc = acc + gate_ref[:, k:k + 1] * ybuf[slot, k].reshape(tm, ybuf.shape[-1])
    out_ref[...] = _rms(acc, g_ref[...])


def _final(row, x2, gates, y_rows, g, tm):
    T, D = x2.shape
    grid_spec = pltpu.PrefetchScalarGridSpec(
        num_scalar_prefetch=1,
        grid=(T // tm,),
        in_specs=[pl.BlockSpec((tm, D), lambda i, *_: (i, 0)),
                  pl.BlockSpec((tm, TOPK_IN_GROUP), lambda i, *_: (i, 0)),
                  pl.BlockSpec(memory_space=pl.ANY),
                  pl.BlockSpec(g.shape, lambda i, *_: (0, 0))],
        out_specs=pl.BlockSpec((tm, D), lambda i, *_: (i, 0)),
        scratch_shapes=[pltpu.VMEM((2, TOPK_IN_GROUP, tm // _SUBLANES, _SUBLANES, D), F32),
                        pltpu.SemaphoreType.DMA((2,))],
    )
    return pl.pallas_call(
        functools.partial(_final_body, tm=tm),
        grid_spec=grid_spec,
        out_shape=jax.ShapeDtypeStruct((T, D), F32),
        compiler_params=_params(),
        name="final",
    )(row, x2, gates, y_rows, g)


def _tile_sizes(T):
    pick = lambda want: want if T % want == 0 else CHUNK
    return dict(inproj=pick(256), dsa_q=pick(256), dsa_k=pick(512), mlstm=pick(256), mixout=pick(256),
                final=pick(256), moe=128)


def _layer(x, mem, norm_mix_g, w_in, kv_norm_g, k_idx_norm_g, w_uk, w_uv, conv_w, conv_b, gate_b, ml_norm_g,
           w_out, norm_x_g, mem_norm_g, w_cq, w_ckv, w_co, norm_ffn_g, w_group, b_group, w_router, b_router,
           w_gate, w_up, w_down, out_g):
    T = x.shape[0]
    ts = _tile_sizes(T)
    r2 = lambda v: v.reshape(1, -1)

    w_b = w_in.astype(BF16)
    w_r = jnp.concatenate([
        w_b[:, _O_DQ:_O_MQ], w_b[:, _O_MI:_O_MO],
        jnp.zeros((D_MODEL, _SMALL - IDX_DIM - IDX_HEADS - 2 * ML_HEADS), BF16),
        w_b[:, _O_MQ:_O_MI], w_b[:, _O_MO:_O_END]], axis=1)
    wuk_t = jnp.transpose(w_uk, (1, 2, 0)).astype(BF16)
    wuv_t = jnp.transpose(w_uv, (1, 0, 2)).astype(BF16)

    qabs, ckv, qi, kidx, small, mqk, mv, mo = _inproj(
        x, r2(norm_mix_g), w_r, wuk_t, r2(kv_norm_g), r2(k_idx_norm_g), ts["inproj"])

    gate_rows = jnp.transpose(small[:, _S_WI:_S_MF + ML_HEADS])
    dsa_out = _dsa(qi, gate_rows, qabs, kidx, ckv, wuv_t, ts["dsa_q"], ts["dsa_k"])

    gb_col = jnp.zeros((1, _SMALL), F32).at[0, _S_MI:_S_MI + 2 * ML_HEADS].set(gate_b)
    ml_out = _mlstm(mqk, mv, small, gate_rows, mo, conv_w, r2(conv_b), gb_col, gate_b.reshape(-1, 1),
                    r2(ml_norm_g), ts["mlstm"])

    kv = _memkv(mem, r2(mem_norm_g), w_ckv, 512)
    w_rt = jnp.concatenate([w_group, w_router,
                            jnp.zeros((D_MODEL, 128 - N_GROUPS - N_EXPERTS), w_group.dtype)], axis=1)
    x2, hf, rl = _mixout(x, dsa_out, ml_out, kv, w_out.astype(BF16), w_cq.astype(BF16), w_co.astype(BF16),
                         w_rt.astype(BF16), r2(norm_x_g), r2(norm_ffn_g), ts["mixout"])

    bm = ts["moe"]
    row_tok, row, gates, e_blk, n_blk = _route(rl, b_group, b_router, bm)
    y_rows = _moe(row_tok, e_blk, hf, w_gate, w_up, w_down, n_blk, bm)
    return _final(row, x2, gates, y_rows, r2(out_g), ts["final"])


def kernel(x, mem, norm_mix_g, w_in, kv_norm_g, k_idx_norm_g, w_uk, w_uv, conv_w, conv_b, gate_b, ml_norm_g,
           w_out, norm_x_g, mem_norm_g, w_cq, w_ckv, w_co, norm_ffn_g, w_group, b_group, w_router, b_router,
           w_gate, w_up, w_down, final_norm_g):
    B, T, D = x.shape
    assert B == 1 and D == D_MODEL and norm_mix_g.shape[0] == 1 and T % CHUNK == 0
    out = _layer(x[0], mem[0], norm_mix_g[0], w_in[0], kv_norm_g[0], k_idx_norm_g[0], w_uk[0], w_uv[0],
                 conv_w[0], conv_b[0], gate_b[0], ml_norm_g[0], w_out[0], norm_x_g[0], mem_norm_g[0],
                 w_cq[0], w_ckv[0], w_co[0], norm_ffn_g[0], w_group[0], b_group[0], w_router[0], b_router[0],
                 w_gate[0], w_up[0], w_down[0], final_norm_g)
    return out[None]
```

```python
import functools

import jax
import jax.numpy as jnp
import numpy as np
from jax import lax
from jax.experimental import pallas as pl
from jax.experimental.pallas import tpu as pltpu

F32 = jnp.float32
BF16 = jnp.bfloat16
I32 = jnp.int32
I16 = jnp.int16

EPS = 1e-6
CHUNK = 64
D_MODEL = 2048

DSA_HEADS = 8
DSA_HEAD_DIM = 128
DSA_LATENT = 256
IDX_HEADS = 8
IDX_DIM = 64
TOPK_MAX = 256

ML_HEADS = 4
ML_QK_DIM = 128
ML_V_DIM = 256
CONV_W = 4

X_HEADS = 4
X_HEAD_DIM = D_MODEL // X_HEADS

N_GROUPS = 4
EXP_PER_GROUP = 8
N_EXPERTS = N_GROUPS * EXP_PER_GROUP
TOPK_IN_GROUP = 2
D_EXPERT = 512

_O_DQ = 0
_O_CKV = _O_DQ + DSA_HEADS * DSA_HEAD_DIM
_O_QI = _O_CKV + DSA_LATENT
_O_KI = _O_QI + IDX_HEADS * IDX_DIM
_O_WI = _O_KI + IDX_DIM
_O_MQ = _O_WI + IDX_HEADS
_O_MK = _O_MQ + ML_HEADS * ML_QK_DIM
_O_MV = _O_MK + ML_HEADS * ML_QK_DIM
_O_MI = _O_MV + ML_HEADS * ML_V_DIM
_O_MF = _O_MI + ML_HEADS
_O_MO = _O_MF + ML_HEADS
_O_END = _O_MO + ML_HEADS * ML_V_DIM

_G_DQ = (0, 1024)
_G_CKV = (1024, 1280)
_G_QI = (1280, 1792)
_G_SMALL = (1792, 1920)
_G_MQK = (1920, 2944)
_G_MV = (2944, 3968)
_G_MO = (3968, 4992)
_W_COLS = 4992
_S_WI = IDX_DIM
_S_MI = _S_WI + IDX_HEADS
_S_MF = _S_MI + ML_HEADS
_SMALL = 128

_VMEM_LIMIT = 56 * 1024 * 1024
_INT_MIN = -(2 ** 31)
_I16_MIN = -(2 ** 15)
_CHUNK_SHIFT = CHUNK.bit_length() - 1
_LOG2E = 1.4426950408889634
_SUBLANES = 8
_GATHER_SLOTS = 3
_RANK_SEG = 512
_NEG = -1e30


def _rms(v, g):
    return v * lax.rsqrt(jnp.mean(v * v, axis=-1, keepdims=True) + EPS) * g


def _dot(a, b):
    return jnp.dot(a, b, preferred_element_type=F32)


def _dot_nt(a, b):
    return lax.dot_general(a, b, (((1,), (1,)), ((), ())), preferred_element_type=F32)


def _resident(shape):
    nd = len(shape)
    return pl.BlockSpec(shape, lambda *_: (0,) * nd, pipeline_mode=pl.Buffered(1))


def _params(n_axes=1):
    return pltpu.CompilerParams(dimension_semantics=("arbitrary",) * n_axes,
                                vmem_limit_bytes=_VMEM_LIMIT)


def _inproj_body(x_ref, g_ref, w_ref, wuk_ref, kvg_ref, kig_ref,
                 qabs_ref, ckv_ref, qi_ref, kidx_ref, small_ref, mqk_ref, mv_ref, mo_ref):
    h = _rms(x_ref[...], g_ref[...]).astype(BF16)

    def proj(grp):
        return _dot(h, w_ref[:, grp[0]:grp[1]])

    dq = proj(_G_DQ)
    for hd in range(DSA_HEADS):
        qh = dq[:, hd * DSA_HEAD_DIM:(hd + 1) * DSA_HEAD_DIM].astype(BF16)
        qa = _dot(qh, wuk_ref[hd]) * (DSA_HEAD_DIM ** -0.5 * _LOG2E)
        qabs_ref[:, hd * DSA_LATENT:(hd + 1) * DSA_LATENT] = qa.astype(BF16)
    ckv_ref[...] = _rms(proj(_G_CKV), kvg_ref[...]).astype(BF16)
    qi_ref[...] = (proj(_G_QI) * (IDX_DIM ** -0.5)).astype(BF16)
    small = proj(_G_SMALL)
    small_ref[...] = small
    kidx_ref[...] = _rms(small[:, :IDX_DIM], kig_ref[...]).astype(BF16)
    mqk_ref[...] = proj(_G_MQK)
    mv_ref[...] = proj(_G_MV).astype(BF16)
    mo_ref[...] = proj(_G_MO)


def _inproj(x, g, w, wuk, kvg, kig, tm):
    T = x.shape[0]
    row = lambda n: pl.BlockSpec((tm, n), lambda i: (i, 0))
    outs = [(8 * DSA_LATENT, BF16), (DSA_LATENT, BF16), (IDX_HEADS * IDX_DIM, BF16), (IDX_DIM, BF16),
            (_SMALL, F32), (2 * ML_HEADS * ML_QK_DIM, F32), (ML_HEADS * ML_V_DIM, BF16),
            (ML_HEADS * ML_V_DIM, F32)]
    return pl.pallas_call(
        _inproj_body,
        grid=(T // tm,),
        in_specs=[row(D_MODEL), _resident(g.shape), _resident(w.shape), _resident(wuk.shape),
                  _resident(kvg.shape), _resident(kig.shape)],
        out_specs=[row(n) for n, _ in outs],
        out_shape=[jax.ShapeDtypeStruct((T, n), dt) for n, dt in outs],
        compiler_params=_params(),
        name="inproj",
    )(x, g, w, wuk, kvg, kig)


def _sublane_fold(v, op, rows=_SUBLANES, ways=4):
    groups = [v[r * rows:(r + 1) * rows, :] for r in range(v.shape[0] // rows)]
    accs = groups[:ways]
    for r in range(ways, len(groups)):
        accs[r % ways] = op(accs[r % ways], groups[r])
    while len(accs) > 1:
        accs = [op(accs[k], accs[k + 1]) if k + 1 < len(accs) else accs[k] for k in range(0, len(accs), 2)]
    return accs[0]


def _dsa_body(qi_ref, wrow_ref, qabs_ref, kidx_ref, ckv_ref, ckvt_ref, wuv_ref, out_ref,
              key_ref, hi_ref, lga_ref, lgb_ref, m_ref, l_ref, acc_ref, *, tq, tk, topk, nbits_idx):
    i = pl.program_id(0)
    n_kb = ((i + 1) * tq + tk - 1) // tk
    w_rows = wrow_ref[0:IDX_HEADS, :] * (IDX_HEADS ** -0.5)
    q_chunk = (i * tq + lax.broadcasted_iota(I32, (1, tq), 1)) >> _CHUNK_SHIFT

    def key_pos(j):
        return j * tk + lax.broadcasted_iota(I32, (tk, 1), 0)

    def score_block(j, carry):
        kx = kidx_ref[pl.ds(pl.multiple_of(j * tk, tk), tk), :]
        s = jnp.zeros((tk, tq), F32)
        for hd in range(IDX_HEADS):
            d = _dot_nt(kx, qi_ref[:, hd * IDX_DIM:(hd + 1) * IDX_DIM])
            s = s + w_rows[hd:hd + 1, :] * jnp.maximum(d, 0.0)
        bits = lax.bitcast_convert_type(s, I32)
        key = bits ^ ((bits >> 31) & 0x7FFFFFFF)
        key_ref[j] = jnp.where((key_pos(j) >> _CHUNK_SHIFT) <= q_chunk, key, _INT_MIN)
        return carry

    lax.fori_loop(0, n_kb, score_block, 0)

    def count(pred):
        def body(j, acc):
            hit = pred(key_ref[j], key_pos(j)).astype(I32)
            return acc + _sublane_fold(hit, jnp.add)
        acc = lax.fori_loop(0, n_kb, body, jnp.zeros((8, tq), I32))
        return jnp.sum(acc, axis=0, keepdims=True)

    def count16(ref, cand):
        c16 = cand.astype(I16)
        def body(j, acc):
            hit = jnp.where(ref[j] >= c16, jnp.int16(1), jnp.int16(0))
            return acc + _sublane_fold(hit, jnp.add, rows=16)
        acc = lax.fori_loop(0, n_kb, body, jnp.zeros((16, tq), I16))
        return jnp.sum(acc.astype(I32), axis=0, keepdims=True)

    def kth_largest16(ref, kth):
        def bit(b, t):
            cand = t + lax.shift_left(jnp.int32(1), 15 - b)
            return jnp.where(count16(ref, cand) >= kth, cand, t)
        return lax.fori_loop(0, 16, bit, jnp.full((1, tq), _I16_MIN, I32))

    def split_block(j, carry):
        kb = key_ref[j]
        hi_ref[j] = (kb >> 16).astype(I16)
        return carry

    lax.fori_loop(0, n_kb, split_block, 0)
    t_hi = kth_largest16(hi_ref, topk)
    n_above = count16(hi_ref, t_hi + 1)

    def low_block(j, carry):
        kb = key_ref[j]
        lo = ((kb & 0xFFFF) + _I16_MIN).astype(I16)
        hi_ref[j] = jnp.where((kb >> 16) == t_hi, lo, jnp.int16(_I16_MIN))
        return carry

    lax.fori_loop(0, n_kb, low_block, 0)
    t_lo = kth_largest16(hi_ref, topk - n_above)
    t = lax.shift_left(t_hi, 16) + (t_lo - _I16_MIN)
    t = jnp.maximum(t, _INT_MIN + 1)
    n_ge = count(lambda kb, pos: kb >= t)
    n_gt = count(lambda kb, pos: kb > t)
    all_pos = jnp.int32(2 ** nbits_idx - 1)
    n_tie_take = jnp.where(n_ge > topk, topk - n_gt, all_pos)

    def tie_cutoff():
        def pos_bit(b, c):
            cand = c + lax.shift_left(jnp.int32(1), nbits_idx - 1 - b)
            f = count(lambda kb, pos: (kb == t) & (pos < cand))
            return jnp.where(f <= n_tie_take, cand, c)
        return lax.fori_loop(0, nbits_idx, pos_bit, jnp.zeros((1, tq), I32))

    cut = lax.cond(jnp.max(n_ge) > topk, tie_cutoff, lambda: jnp.full((1, tq), all_pos, I32))

    def bias_block(j, carry):
        kb = key_ref[j]
        sel = (kb > t) | ((kb == t) & (key_pos(j) < cut))
        key_ref[j] = lax.bitcast_convert_type(jnp.where(sel, 0.0, _NEG).astype(F32), I32)
        return carry

    lax.fori_loop(0, n_kb, bias_block, 0)

    n_pair = (n_kb + 1) // 2
    neg_bits = lax.bitcast_convert_type(jnp.full((tk, tq), _NEG, F32), I32)

    def masked_block(j, carry):
        key_ref[j] = neg_bits
        return carry

    lax.fori_loop(n_kb, 2 * n_pair, masked_block, 0)

    m_ref[...] = jnp.full(m_ref.shape, _NEG, F32)
    l_ref[...] = jnp.zeros(l_ref.shape, F32)
    acc_ref[...] = jnp.zeros(acc_ref.shape, F32)
    last_blk = ckvt_ref.shape[0] - 1

    def logits(j, lg_buf):
        c_blk = ckv_ref[pl.ds(pl.multiple_of(jnp.minimum(j, last_blk) * tk, tk), tk), :]
        bias = lax.bitcast_convert_type(key_ref[j], F32)
        for hd in range(DSA_HEADS):
            lg_buf[hd] = _dot_nt(c_blk, qabs_ref[:, hd * DSA_LATENT:(hd + 1) * DSA_LATENT]) + bias

    def accumulate(j, lg_buf):
        c_blk_t = ckvt_ref[jnp.minimum(j, last_blk)]
        for hd in range(DSA_HEADS):
            lg = lg_buf[hd]
            m_old = m_ref[hd:hd + 1, :]
            m_new = jnp.maximum(m_old, jnp.max(_sublane_fold(lg, jnp.maximum, ways=1), axis=0, keepdims=True))
            p = jnp.exp2(lg - m_new)
            alpha = jnp.exp2(m_old - m_new)
            l_ref[hd:hd + 1, :] = alpha * l_ref[hd:hd + 1, :] + jnp.sum(_sublane_fold(p, jnp.add, ways=1), axis=0,
                                                                         keepdims=True)
            acc_ref[hd] = alpha * acc_ref[hd] + _dot(c_blk_t, p.astype(BF16))
            m_ref[hd:hd + 1, :] = m_new

    logits(0, lga_ref)

    def attn_pair(mi, carry):
        ja = 2 * mi
        accumulate(ja, lga_ref)
        logits(ja + 1, lgb_ref)
        accumulate(ja + 1, lgb_ref)
        logits(jnp.minimum(ja + 2, 2 * n_pair - 1), lga_ref)
        return carry

    lax.fori_loop(0, n_pair, attn_pair, 0)

    for hd in range(DSA_HEADS):
        o_lat = (acc_ref[hd] / l_ref[hd:hd + 1, :]).T.astype(BF16)
        out_ref[:, hd * DSA_HEAD_DIM:(hd + 1) * DSA_HEAD_DIM] = _dot(o_lat, wuv_ref[hd]).astype(BF16)


def _dsa(qi, wrows, qabs, kidx, ckv, wuv, tq, tk):
    T = qi.shape[0]
    topk = min(TOPK_MAX, T // 4)
    n_kb = T // tk
    ckvt = jnp.transpose(ckv.reshape(n_kb, tk, DSA_LATENT), (0, 2, 1))
    row = lambda n: pl.BlockSpec((tq, n), lambda i: (i, 0))
    body = functools.partial(_dsa_body, tq=tq, tk=tk, topk=topk, nbits_idx=int(T).bit_length())
    return pl.pallas_call(
        body,
        grid=(T // tq,),
        in_specs=[row(qi.shape[1]), pl.BlockSpec((wrows.shape[0], tq), lambda i: (0, i)), row(qabs.shape[1]),
                  _resident(kidx.shape), _resident(ckv.shape), _resident(ckvt.shape), _resident(wuv.shape)],
        out_specs=row(DSA_HEADS * DSA_HEAD_DIM),
        out_shape=jax.ShapeDtypeStruct((T, DSA_HEADS * DSA_HEAD_DIM), BF16),
        scratch_shapes=[pltpu.VMEM((n_kb + n_kb % 2, tk, tq), I32), pltpu.VMEM((n_kb, tk, tq), I16),
                        pltpu.VMEM((DSA_HEADS, tk, tq), F32), pltpu.VMEM((DSA_HEADS, tk, tq), F32),
                        pltpu.VMEM((DSA_HEADS, tq), F32),
                        pltpu.VMEM((DSA_HEADS, tq), F32), pltpu.VMEM((DSA_HEADS, DSA_LATENT, tq), F32)],
        compiler_params=_params(),
        name="dsa",
    )(qi, wrows, qabs, kidx, ckv, ckvt, wuv)


def _log_sigmoid(v):
    return jnp.minimum(v, 0.0) - jnp.log1p(jnp.exp(-jnp.abs(v)))


def _chunk_cumsum(v, axis):
    pos = lax.broadcasted_iota(I32, v.shape, axis) & (CHUNK - 1)
    d = 1
    while d < CHUNK:
        v = v + jnp.where(pos >= d, pltpu.roll(v, d, axis=axis), 0.0)
        d *= 2
    return v


def _mlstm_body(mqk_ref, mv_ref, small_ref, gt_ref, mo_ref, cw_ref, cb_ref, gbc_ref, gbr_ref, ng_ref,
                out_ref, xe_ref, c_ref, n_ref, m_ref, hs_ref, *, rows):
    @pl.when(pl.program_id(0) == 0)
    def _():
        xe_ref[0:8, :] = jnp.zeros((8, xe_ref.shape[1]), F32)
        c_ref[...] = jnp.zeros(c_ref.shape, F32)
        n_ref[...] = jnp.zeros(n_ref.shape, F32)
        m_ref[...] = jnp.zeros(m_ref.shape, F32)

    x = mqk_ref[...]
    xe_ref[8:8 + rows, :] = x
    y = cb_ref[...]
    for j in range(CONV_W - 1):
        y = y + xe_ref[5 + j:5 + j + rows, :] * cw_ref[j:j + 1, :]
    y = y + x * cw_ref[CONV_W - 1:CONV_W, :]
    xe_ref[0:8, :] = x[rows - 8:rows, :]
    qk = y * jax.nn.sigmoid(y)
    nqk = ML_HEADS * ML_QK_DIM
    q_all = (qk[:, :nqk] * (ML_QK_DIM ** -0.5)).astype(BF16)
    k_all = qk[:, nqk:]

    g_col = small_ref[...] + gbc_ref[...]
    g_row = gt_ref[...] + gbr_ref[...]
    b_col = _chunk_cumsum(_log_sigmoid(g_col), 0)
    b_row = _chunk_cumsum(_log_sigmoid(g_row), 1)

    tri = lax.broadcasted_iota(I32, (CHUNK, CHUNK), 1) <= lax.broadcasted_iota(I32, (CHUNK, CHUNK), 0)

    for c in range(rows // CHUNK):
        lo, hi = c * CHUNK, (c + 1) * CHUNK
        for hd in range(ML_HEADS):
            bc = b_col[lo:hi, _S_MF + hd:_S_MF + hd + 1]
            lic = g_col[lo:hi, _S_MI + hd:_S_MI + hd + 1]
            br = b_row[ML_HEADS + hd:ML_HEADS + hd + 1, lo:hi]
            lir = g_row[hd:hd + 1, lo:hi]
            g_tot = bc[CHUNK - 1:CHUNK, :]
            m_prev = m_ref[hd][:, 0:1]

            dmat = jnp.where(tri, bc - br + lir, -jnp.inf)
            inter = bc + m_prev
            m_t = jnp.maximum(inter, jnp.max(dmat, axis=-1, keepdims=True))
            w_intra = jnp.exp(dmat - m_t)
            a_inter = jnp.exp(inter - m_t)

            qh = q_all[lo:hi, hd * ML_QK_DIM:(hd + 1) * ML_QK_DIM]
            kh = k_all[lo:hi, hd * ML_QK_DIM:(hd + 1) * ML_QK_DIM]
            vh = mv_ref[lo:hi, hd * ML_V_DIM:(hd + 1) * ML_V_DIM]
            s_qk = _dot_nt(qh, kh.astype(BF16)) * w_intra
            c_prev = c_ref[hd]
            n_prev = n_ref[hd]
            num = a_inter * _dot(qh, c_prev.astype(BF16)) + _dot(s_qk.astype(BF16), vh)
            den = (a_inter * jnp.sum(qh.astype(F32) * n_prev, axis=-1, keepdims=True)
                   + jnp.sum(s_qk, axis=-1, keepdims=True))
            hs_ref[lo:hi, hd * ML_V_DIM:(hd + 1) * ML_V_DIM] = (
                num / jnp.maximum(jnp.abs(den), jnp.exp(-m_t)))

            m_new = jnp.maximum(g_tot + m_prev, jnp.max(g_tot - br + lir, axis=-1, keepdims=True))
            a_state = jnp.exp(g_tot + m_prev - m_new)
            wk = jnp.exp(g_tot - bc + lic - m_new) * kh
            c_ref[hd] = a_state * c_prev + _dot(wk.T.astype(BF16), vh)
            n_ref[hd] = a_state * n_prev + jnp.sum(wk, axis=0, keepdims=True)
            m_ref[hd] = jnp.broadcast_to(m_new, m_ref.shape[1:])

    for hd in range(ML_HEADS):
        sl = slice(hd * ML_V_DIM, (hd + 1) * ML_V_DIM)
        out_ref[:, sl] = (_rms(hs_ref[:, sl], ng_ref[:, sl]) * jax.nn.sigmoid(mo_ref[:, sl])).astype(BF16)


def _mlstm(mqk, mv, small, gt, mo, cw, cb, gbc, gbr, ng, rows):
    T = mqk.shape[0]
    row = lambda n: pl.BlockSpec((rows, n), lambda i: (i, 0))
    nv = ML_HEADS * ML_V_DIM
    return pl.pallas_call(
        functools.partial(_mlstm_body, rows=rows),
        grid=(T // rows,),
        in_specs=[row(mqk.shape[1]), row(nv), row(_SMALL), pl.BlockSpec((8, rows), lambda i: (1, i)), row(nv),
                  _resident(cw.shape), _resident(cb.shape), _resident(gbc.shape), _resident(gbr.shape),
                  _resident(ng.shape)],
        out_specs=row(nv),
        out_shape=jax.ShapeDtypeStruct((T, nv), BF16),
        scratch_shapes=[pltpu.VMEM((rows + 8, mqk.shape[1]), F32),
                        pltpu.VMEM((ML_HEADS, ML_QK_DIM, ML_V_DIM), F32),
                        pltpu.VMEM((ML_HEADS, 1, ML_QK_DIM), F32),
                        pltpu.VMEM((ML_HEADS, 1, 128), F32),
                        pltpu.VMEM((rows, nv), F32)],
        compiler_params=_params(),
        name="mlstm",
    )(mqk, mv, small, gt, mo, cw, cb, gbc, gbr, ng)


def _memkv_body(mem_ref, g_ref, w_ref, out_ref):
    mn = _rms(mem_ref[...], g_ref[...]).astype(BF16)
    out_ref[...] = _dot(mn, w_ref[...].astype(BF16)).astype(BF16)


def _memkv(mem, g, w, tn):
    M, D = mem.shape
    N = w.shape[1]
    return pl.pallas_call(
        _memkv_body,
        grid=(N // tn,),
        in_specs=[_resident(mem.shape), _resident(g.shape), pl.BlockSpec((D, tn), lambda j: (0, j))],
        out_specs=pl.BlockSpec((M, tn), lambda j: (0, j)),
        out_shape=jax.ShapeDtypeStruct((M, N), BF16),
        compiler_params=_params(),
        name="memkv",
    )(mem, g, w)


def _mixout_body(x_ref, dsa_ref, ml_ref, kv_ref, wo_ref, wq_ref, wc_ref, wr_ref, gx_ref, gf_ref,
                 x2_ref, hf_ref, rl_ref, o_ref):
    nd = dsa_ref.shape[1]
    x1 = x_ref[...] + _dot(dsa_ref[...], wo_ref[0:nd, :]) + _dot(ml_ref[...], wo_ref[nd:, :])
    q = _dot(_rms(x1, gx_ref[...]).astype(BF16), wq_ref[...]).astype(BF16)
    for hd in range(X_HEADS):
        sl = slice(hd * X_HEAD_DIM, (hd + 1) * X_HEAD_DIM)
        lg = _dot_nt(q[:, sl], kv_ref[:, sl]) * (X_HEAD_DIM ** -0.5)
        e = jnp.exp(lg - jnp.max(lg, axis=-1, keepdims=True))
        p = e / jnp.sum(e, axis=-1, keepdims=True)
        v = kv_ref[:, D_MODEL + hd * X_HEAD_DIM:D_MODEL + (hd + 1) * X_HEAD_DIM]
        o_ref[:, sl] = _dot(p.astype(BF16), v).astype(BF16)
    x2 = x1 + _dot(o_ref[...], wc_ref[...])
    x2_ref[...] = x2
    hf = _rms(x2, gf_ref[...])
    hf_ref[...] = hf
    rl_ref[...] = _dot(hf.astype(BF16), wr_ref[...])


def _mixout(x, dsa, ml, kv, wo, wq, wc, wr, gx, gf, tm):
    T = x.shape[0]
    row = lambda n: pl.BlockSpec((tm, n), lambda i: (i, 0))
    return pl.pallas_call(
        _mixout_body,
        grid=(T // tm,),
        in_specs=[row(D_MODEL), row(dsa.shape[1]), row(ml.shape[1]), _resident(kv.shape), _resident(wo.shape),
                  _resident(wq.shape), _resident(wc.shape), _resident(wr.shape), _resident(gx.shape),
                  _resident(gf.shape)],
        out_specs=[row(D_MODEL), row(D_MODEL), row(wr.shape[1])],
        out_shape=[jax.ShapeDtypeStruct((T, D_MODEL), F32), jax.ShapeDtypeStruct((T, D_MODEL), F32),
                   jax.ShapeDtypeStruct((T, wr.shape[1]), F32)],
        scratch_shapes=[pltpu.VMEM((tm, D_MODEL), BF16)],
        compiler_params=_params(),
        name="mixout",
    )(x, dsa, ml, kv, wo, wq, wc, wr, gx, gf)


def _moe_body(tok_ref, eb_ref, hf_hbm, wg_ref, wu_ref, wd_ref, y_hbm,
              xbuf, ybuf, wgb, wub, wdb, gsem, ysem, *, bm, n_blk):
    e = pl.program_id(0)
    n_valid = eb_ref[N_EXPERTS]
    b_lo = eb_ref[e]
    b_hi = eb_ref[e + 1]

    def gather(blk, s):
        def issue(g, carry):
            for u in range(_SUBLANES):
                tok = tok_ref[blk * bm + g * _SUBLANES + u]
                pltpu.make_async_copy(hf_hbm.at[pl.ds(tok, 1)], xbuf.at[s, g, pl.ds(u, 1)],
                                      gsem.at[s]).start(priority=1)
            return carry
        lax.fori_loop(0, bm // _SUBLANES, issue, 0)

    def wait_gather(s):
        pltpu.make_async_copy(xbuf.at[s], xbuf.at[s], gsem.at[s]).wait()

    def y_copy(blk, s):
        return pltpu.make_async_copy(ybuf.at[s], y_hbm.at[pl.ds(pl.multiple_of(blk * bm, bm), bm)], ysem.at[s])

    @pl.when(e == 0)
    def _():
        gather(0, 0)
        gather(1, 1)

    @pl.when(b_hi > b_lo)
    def _():
        wgb[...] = wg_ref[0].astype(BF16)
        wub[...] = wu_ref[0].astype(BF16)
        wdb[...] = wd_ref[0].astype(BF16)

        def block(b, carry):
            s = b % _GATHER_SLOTS
            wait_gather(s)
            xb = xbuf[s].reshape(bm, xbuf.shape[-1]).astype(BF16)
            gate = _dot(xb, wgb[...])
            a = gate * jax.nn.sigmoid(gate) * _dot(xb, wub[...])
            y = _dot(a.astype(BF16), wdb[...])

            nxt = jnp.minimum(b + 2, n_blk - 1)
            for r in range(bm):
                tok = tok_ref[nxt * bm + r]
                pltpu.make_async_copy(hf_hbm.at[pl.ds(tok, 1)],
                                      xbuf.at[(b + 2) % _GATHER_SLOTS, r // _SUBLANES, pl.ds(r % _SUBLANES, 1)],
                                      gsem.at[(b + 2) % _GATHER_SLOTS]).start(priority=1)

            @pl.when(b >= 2)
            def _():
                y_copy(b - 2, b % 2).wait()

            ybuf[b % 2] = y
            y_copy(b, b % 2).start()
            return carry

        lax.fori_loop(b_lo, b_hi, block, 0)

    @pl.when(e == pl.num_programs(0) - 1)
    def _():
        wait_gather(n_valid % _GATHER_SLOTS)
        wait_gather((n_valid + 1) % _GATHER_SLOTS)

        @pl.when(n_valid >= 2)
        def _():
            y_copy(n_valid - 2, n_valid % 2).wait()
        y_copy(n_valid - 1, (n_valid - 1) % 2).wait()
        ybuf[0] = jnp.zeros(ybuf.shape[1:], F32)

        def zero_block(b, carry):
            cp = y_copy(b, 0)
            cp.start()
            cp.wait()
            return carry

        lax.fori_loop(n_valid, n_blk, zero_block, 0)


def _moe(tok, e_blk, hf, wg, wu, wd, n_blk, bm):
    D = hf.shape[1]
    wspec = lambda shape: pl.BlockSpec((1,) + shape, lambda e, *_: (e, 0, 0))
    grid_spec = pltpu.PrefetchScalarGridSpec(
        num_scalar_prefetch=2,
        grid=(N_EXPERTS,),
        in_specs=[pl.BlockSpec(memory_space=pl.ANY),
                  wspec((D, D_EXPERT)), wspec((D, D_EXPERT)), wspec((D_EXPERT, D))],
        out_specs=pl.BlockSpec(memory_space=pl.ANY),
        scratch_shapes=[pltpu.VMEM((_GATHER_SLOTS, bm // _SUBLANES, _SUBLANES, D), F32),
                        pltpu.VMEM((2, bm, D), F32),
                        pltpu.VMEM((D, D_EXPERT), BF16), pltpu.VMEM((D, D_EXPERT), BF16),
                        pltpu.VMEM((D_EXPERT, D), BF16),
                        pltpu.SemaphoreType.DMA((_GATHER_SLOTS,)), pltpu.SemaphoreType.DMA((2,))],
    )
    return pl.pallas_call(
        functools.partial(_moe_body, bm=bm, n_blk=n_blk),
        grid_spec=grid_spec,
        out_shape=jax.ShapeDtypeStruct((n_blk * bm, D), F32),
        compiler_params=_params(),
        name="moe",
    )(tok, e_blk, hf, wg, wu, wd)


def _route(rl, b_group, b_router, bm):
    N = rl.shape[0]
    g_logits = rl[:, :N_GROUPS] + b_group
    g_prob = jax.nn.softmax(g_logits, axis=-1)
    g_sel = jnp.argmax(g_logits, axis=-1)
    p_g = jnp.take_along_axis(g_prob, g_sel[:, None], axis=-1)
    e_logits = (rl[:, N_GROUPS:N_GROUPS + N_EXPERTS] + b_router).reshape(N, N_GROUPS, EXP_PER_GROUP)
    e_logits = jnp.take_along_axis(e_logits, g_sel[:, None, None], axis=1)[:, 0]
    top_p, top_local = lax.top_k(jax.nn.softmax(e_logits, axis=-1), TOPK_IN_GROUP)
    gates = p_g * top_p / top_p.sum(-1, keepdims=True)
    expert_id = (g_sel[:, None] * EXP_PER_GROUP + top_local).astype(I32)

    A = N * TOPK_IN_GROUP
    flat_e = expert_id.reshape(A)
    onehot = (flat_e[:, None] == jnp.arange(N_EXPERTS, dtype=I32)[None, :])
    seg = _RANK_SEG if A % _RANK_SEG == 0 else A
    oh = onehot.astype(BF16).reshape(A // seg, seg, N_EXPERTS)
    before = (jnp.arange(seg)[None, :] < jnp.arange(seg)[:, None]).astype(BF16)
    within = jnp.einsum('ij,bjk->bik', before, oh, preferred_element_type=F32)
    seg_tot = jnp.sum(oh.astype(F32), axis=1)
    seg_base = jnp.cumsum(seg_tot, axis=0) - seg_tot
    rank = jnp.sum((within + seg_base[:, None, :]) * oh.astype(F32), axis=-1).reshape(A).astype(I32)
    counts = jnp.sum(seg_tot, axis=0).astype(I32)
    padded = (counts + bm - 1) // bm * bm
    pad_ends = jnp.cumsum(padded)
    pad_starts = pad_ends - padded
    row = jnp.sum(jnp.where(onehot, pad_starts[None, :], 0), axis=1) + rank
    n_blk = -(-A // bm) + N_EXPERTS
    row_tok = jnp.zeros((n_blk * bm,), I32).at[row].set(jnp.arange(A, dtype=I32) // TOPK_IN_GROUP)
    e_blk = jnp.concatenate([pad_starts, pad_ends[-1:]]) // bm
    return row_tok, row, gates, e_blk.astype(I32), n_blk


def _final_body(row_ref, x_ref, gate_ref, y_hbm, g_ref, out_ref, ybuf, sem, *, tm):
    i = pl.program_id(0)
    slot = i % 2

    def gather(tile, s):
        def issue(g, carry):
            for u in range(_SUBLANES):
                for k in range(TOPK_IN_GROUP):
                    src = row_ref[(tile * tm + g * _SUBLANES + u) * TOPK_IN_GROUP + k]
                    pltpu.make_async_copy(y_hbm.at[pl.ds(src, 1)], ybuf.at[s, k, g, pl.ds(u, 1)],
                                          sem.at[s]).start()
            return carry
        lax.fori_loop(0, tm // _SUBLANES, issue, 0)

    @pl.when(i == 0)
    def _():
        gather(0, 0)

    @pl.when(i + 1 < pl.num_programs(0))
    def _():
        gather(i + 1, 1 - slot)

    pltpu.make_async_copy(ybuf.at[slot], ybuf.at[slot], sem.at[slot]).wait()
    acc = x_ref[...]
    for k in range(TOPK_IN_GROUP):
        acc = acc + gate_ref[:, k:k + 1] * ybuf[slot, k].reshape(tm, ybuf.shape[-1])
    out_ref[...] = _rms(acc, g_ref[...])


def _final(row, x2, gates, y_rows, g, tm):
    T, D = x2.shape
    grid_spec = pltpu.PrefetchScalarGridSpec(
        num_scalar_prefetch=1,
        grid=(T // tm,),
        in_specs=[pl.BlockSpec((tm, D), lambda i, *_: (i, 0)),
                  pl.BlockSpec((tm, TOPK_IN_GROUP), lambda i, *_: (i, 0)),
                  pl.BlockSpec(memory_space=pl.ANY),
                  pl.BlockSpec(g.shape, lambda i, *_: (0, 0))],
        out_specs=pl.BlockSpec((tm, D), lambda i, *_: (i, 0)),
        scratch_shapes=[pltpu.VMEM((2, TOPK_IN_GROUP, tm // _SUBLANES, _SUBLANES, D), F32),
                        pltpu.SemaphoreType.DMA((2,))],
    )
    return pl.pallas_call(
        functools.partial(_final_body, tm=tm),
        grid_spec=grid_spec,
        out_shape=jax.ShapeDtypeStruct((T, D), F32),
        compiler_params=_params(),
        name="final",
    )(row, x2, gates, y_rows, g)


def _tile_sizes(T):
    pick = lambda want: want if T % want == 0 else CHUNK
    return dict(inproj=pick(256), dsa_q=pick(256), dsa_k=pick(512), mlstm=pick(256), mixout=pick(256),
                final=pick(256), moe=128)


def _layer(x, mem, norm_mix_g, w_in, kv_norm_g, k_idx_norm_g, w_uk, w_uv, conv_w, conv_b, gate_b, ml_norm_g,
           w_out, norm_x_g, mem_norm_g, w_cq, w_ckv, w_co, norm_ffn_g, w_group, b_group, w_router, b_router,
           w_gate, w_up, w_down, out_g):
    T = x.shape[0]
    ts = _tile_sizes(T)
    r2 = lambda v: v.reshape(1, -1)

    w_b = w_in.astype(BF16)
    w_r = jnp.concatenate([
        w_b[:, _O_DQ:_O_MQ], w_b[:, _O_MI:_O_MO],
        jnp.zeros((D_MODEL, _SMALL - IDX_DIM - IDX_HEADS - 2 * ML_HEADS), BF16),
        w_b[:, _O_MQ:_O_MI], w_b[:, _O_MO:_O_END]], axis=1)
    wuk_t = jnp.transpose(w_uk, (1, 2, 0)).astype(BF16)
    wuv_t = jnp.transpose(w_uv, (1, 0, 2)).astype(BF16)

    qabs, ckv, qi, kidx, small, mqk, mv, mo = _inproj(
        x, r2(norm_mix_g), w_r, wuk_t, r2(kv_norm_g), r2(k_idx_norm_g), ts["inproj"])

    gate_rows = jnp.transpose(small[:, _S_WI:_S_MF + ML_HEADS])
    dsa_out = _dsa(qi, gate_rows, qabs, kidx, ckv, wuv_t, ts["dsa_q"], ts["dsa_k"])

    gb_col = jnp.zeros((1, _SMALL), F32).at[0, _S_MI:_S_MI + 2 * ML_HEADS].set(gate_b)
    ml_out = _mlstm(mqk, mv, small, gate_rows, mo, conv_w, r2(conv_b), gb_col, gate_b.reshape(-1, 1),
                    r2(ml_norm_g), ts["mlstm"])

    kv = _memkv(mem, r2(mem_norm_g), w_ckv, 512)
    w_rt = jnp.concatenate([w_group, w_router,
                            jnp.zeros((D_MODEL, 128 - N_GROUPS - N_EXPERTS), w_group.dtype)], axis=1)
    x2, hf, rl = _mixout(x, dsa_out, ml_out, kv, w_out.astype(BF16), w_cq.astype(BF16), w_co.astype(BF16),
                         w_rt.astype(BF16), r2(norm_x_g), r2(norm_ffn_g), ts["mixout"])

    bm = ts["moe"]
    row_tok, row, gates, e_blk, n_blk = _route(rl, b_group, b_router, bm)
    y_rows = _moe(row_tok, e_blk, hf, w_gate, w_up, w_down, n_blk, bm)
    return _final(row, x2, gates, y_rows, r2(out_g), ts["final"])


def kernel(x, mem, norm_mix_g, w_in, kv_norm_g, k_idx_norm_g, w_uk, w_uv, conv_w, conv_b, gate_b, ml_norm_g,
           w_out, norm_x_g, mem_norm_g, w_cq, w_ckv, w_co, norm_ffn_g, w_group, b_group, w_router, b_router,
           w_gate, w_up, w_down, final_norm_g):
    B, T, D = x.shape
    assert B == 1 and D == D_MODEL and norm_mix_g.shape[0] == 1 and T % CHUNK == 0
    out = _layer(x[0], mem[0], norm_mix_g[0], w_in[0], kv_norm_g[0], k_idx_norm_g[0], w_uk[0], w_uv[0],
                 conv_w[0], conv_b[0], gate_b[0], ml_norm_g[0], w_out[0], norm_x_g[0], mem_norm_g[0],
                 w_cq[0], w_ckv[0], w_co[0], norm_ffn_g[0], w_group[0], b_group[0], w_router[0], b_router[0],
                 w_gate[0], w_up[0], w_down[0], final_norm_g)
    return out[None]
```

```python
import functools

import jax
import jax.numpy as jnp
import numpy as np
from jax import lax
from jax.experimental import pallas as pl
from jax.experimental.pallas import tpu as pltpu

F32 = jnp.float32
BF16 = jnp.bfloat16
I32 = jnp.int32
I16 = jnp.int16
U32 = jnp.uint32

EPS = 1e-6
CHUNK = 64
D_MODEL = 2048

DSA_HEADS = 8
DSA_HEAD_DIM = 128
DSA_LATENT = 256
IDX_HEADS = 8
IDX_DIM = 64
TOPK_MAX = 256

ML_HEADS = 4
ML_QK_DIM = 128
ML_V_DIM = 256
CONV_W = 4

X_HEADS = 4
X_HEAD_DIM = D_MODEL // X_HEADS

N_GROUPS = 4
EXP_PER_GROUP = 8
N_EXPERTS = N_GROUPS * EXP_PER_GROUP
TOPK_IN_GROUP = 2
D_EXPERT = 512

_O_DQ = 0
_O_CKV = _O_DQ + DSA_HEADS * DSA_HEAD_DIM
_O_QI = _O_CKV + DSA_LATENT
_O_KI = _O_QI + IDX_HEADS * IDX_DIM
_O_WI = _O_KI + IDX_DIM
_O_MQ = _O_WI + IDX_HEADS
_O_MK = _O_MQ + ML_HEADS * ML_QK_DIM
_O_MV = _O_MK + ML_HEADS * ML_QK_DIM
_O_MI = _O_MV + ML_HEADS * ML_V_DIM
_O_MF = _O_MI + ML_HEADS
_O_MO = _O_MF + ML_HEADS
_O_END = _O_MO + ML_HEADS * ML_V_DIM

_G_DQ = (0, 1024)
_G_CKV = (1024, 1280)
_G_QI = (1280, 1792)
_G_SMALL = (1792, 1920)
_G_MQK = (1920, 2944)
_G_MV = (2944, 3968)
_G_MO = (3968, 4992)
_W_COLS = 4992
_S_WI = IDX_DIM
_S_MI = _S_WI + IDX_HEADS
_S_MF = _S_MI + ML_HEADS
_SMALL = 128

_VMEM_LIMIT = 56 * 1024 * 1024
_INT_MIN = -(2 ** 31)
_I16_MIN = -(2 ** 15)
_CHUNK_SHIFT = CHUNK.bit_length() - 1
_LOG2E = 1.4426950408889634
_SUBLANES = 8
_LANES = 128
_GATHER_SLOTS = 3
_RANK_SEG = 512
_NEG = -1e30


def _rms(v, g):
    return v * lax.rsqrt(jnp.mean(v * v, axis=-1, keepdims=True) + EPS) * g


def _dot(a, b):
    return jnp.dot(a, b, preferred_element_type=F32)


def _dot_nt(a, b):
    return lax.dot_general(a, b, (((1,), (1,)), ((), ())), preferred_element_type=F32)


def _resident(shape):
    nd = len(shape)
    return pl.BlockSpec(shape, lambda *_: (0,) * nd, pipeline_mode=pl.Buffered(1))


def _params(n_axes=1):
    return pltpu.CompilerParams(dimension_semantics=("arbitrary",) * n_axes,
                                vmem_limit_bytes=_VMEM_LIMIT)


def _wprep_body(w_ref, out_ref):
    n_small = IDX_DIM + IDX_HEADS
    out_ref[:, _G_DQ[0]:_G_SMALL[0] + n_small] = w_ref[:, _O_DQ:_O_MQ].astype(BF16)
    out_ref[:, _G_SMALL[0] + n_small:_G_SMALL[0] + n_small + 2 * ML_HEADS] = w_ref[:, _O_MI:_O_MO].astype(BF16)
    out_ref[:, _G_SMALL[0] + n_small + 2 * ML_HEADS:_G_SMALL[1]] = jnp.zeros(
        (out_ref.shape[0], _SMALL - n_small - 2 * ML_HEADS), BF16)
    out_ref[:, _G_MQK[0]:_G_MV[1]] = w_ref[:, _O_MQ:_O_MI].astype(BF16)
    out_ref[:, _G_MO[0]:_G_MO[1]] = w_ref[:, _O_MO:_O_END].astype(BF16)


def _wprep(w_in, tk):
    K = w_in.shape[0]
    return pl.pallas_call(
        _wprep_body,
        grid=(K // tk,),
        in_specs=[pl.BlockSpec((tk, w_in.shape[1]), lambda i: (i, 0))],
        out_specs=pl.BlockSpec((tk, _W_COLS), lambda i: (i, 0)),
        out_shape=jax.ShapeDtypeStruct((K, _W_COLS), BF16),
        compiler_params=_params(),
        name="wprep",
    )(w_in)


def _inproj_body(x_ref, g_ref, w_ref, wuk_ref, kvg_ref, kig_ref,
                 qabs_ref, ckv_ref, qi_ref, kidx_ref, small_ref, mqk_ref, mv_ref, mo_ref):
    h = _rms(x_ref[...], g_ref[...]).astype(BF16)

    def proj(grp):
        return _dot(h, w_ref[:, grp[0]:grp[1]])

    dq = proj(_G_DQ)
    for hd in range(DSA_HEADS):
        qh = dq[:, hd * DSA_HEAD_DIM:(hd + 1) * DSA_HEAD_DIM].astype(BF16)
        qa = _dot(qh, wuk_ref[hd]) * (DSA_HEAD_DIM ** -0.5 * _LOG2E)
        qabs_ref[:, hd * DSA_LATENT:(hd + 1) * DSA_LATENT] = qa.astype(BF16)
    ckv_ref[...] = _rms(proj(_G_CKV), kvg_ref[...]).astype(BF16)
    qi_ref[...] = (proj(_G_QI) * (IDX_DIM ** -0.5)).astype(BF16)
    small = proj(_G_SMALL)
    small_ref[...] = small
    kidx_ref[...] = _rms(small[:, :IDX_DIM], kig_ref[...]).astype(BF16)
    mqk_ref[...] = proj(_G_MQK)
    mv_ref[...] = proj(_G_MV).astype(BF16)
    mo_ref[...] = proj(_G_MO)


def _inproj(x, g, w, wuk, kvg, kig, tm):
    T = x.shape[0]
    row = lambda n: pl.BlockSpec((tm, n), lambda i: (i, 0))
    outs = [(8 * DSA_LATENT, BF16), (DSA_LATENT, BF16), (IDX_HEADS * IDX_DIM, BF16), (IDX_DIM, BF16),
            (_SMALL, F32), (2 * ML_HEADS * ML_QK_DIM, F32), (ML_HEADS * ML_V_DIM, BF16),
            (ML_HEADS * ML_V_DIM, F32)]
    return pl.pallas_call(
        _inproj_body,
        grid=(T // tm,),
        in_specs=[row(D_MODEL), _resident(g.shape), _resident(w.shape), _resident(wuk.shape),
                  _resident(kvg.shape), _resident(kig.shape)],
        out_specs=[row(n) for n, _ in outs],
        out_shape=[jax.ShapeDtypeStruct((T, n), dt) for n, dt in outs],
        compiler_params=_params(),
        name="inproj",
    )(x, g, w, wuk, kvg, kig)


def _sublane_fold(v, op, rows=_SUBLANES, ways=4):
    groups = [v[r * rows:(r + 1) * rows, :] for r in range(v.shape[0] // rows)]
    accs = groups[:ways]
    for r in range(ways, len(groups)):
        accs[r % ways] = op(accs[r % ways], groups[r])
    while len(accs) > 1:
        accs = [op(accs[k], accs[k + 1]) if k + 1 < len(accs) else accs[k] for k in range(0, len(accs), 2)]
    return accs[0]


def _dsa_body(qi_ref, wrow_ref, qabs_ref, kidx_ref, ckv_ref, ckvt_ref, wuv_ref, out_ref,
              key_ref, hi_ref, lga_ref, lgb_ref, m_ref, l_ref, acc_ref, *, tq, tk, topk, nbits_idx):
    i = pl.program_id(0)
    n_kb = ((i + 1) * tq + tk - 1) // tk
    w_rows = wrow_ref[0:IDX_HEADS, :] * (IDX_HEADS ** -0.5)
    q_chunk = (i * tq + lax.broadcasted_iota(I32, (1, tq), 1)) >> _CHUNK_SHIFT

    def key_pos(j):
        return j * tk + lax.broadcasted_iota(I32, (tk, 1), 0)

    def score_block(j, carry):
        kx = kidx_ref[pl.ds(pl.multiple_of(j * tk, tk), tk), :]
        s = jnp.zeros((tk, tq), F32)
        for hd in range(IDX_HEADS):
            d = _dot_nt(kx, qi_ref[:, hd * IDX_DIM:(hd + 1) * IDX_DIM])
            s = s + w_rows[hd:hd + 1, :] * jnp.maximum(d, 0.0)
        bits = lax.bitcast_convert_type(s, I32)
        key = bits ^ ((bits >> 31) & 0x7FFFFFFF)
        key_ref[j] = jnp.where((key_pos(j) >> _CHUNK_SHIFT) <= q_chunk, key, _INT_MIN)
        return carry

    lax.fori_loop(0, n_kb, score_block, 0)

    def count(pred):
        def body(j, acc):
            hit = pred(key_ref[j], key_pos(j)).astype(I32)
            return acc + _sublane_fold(hit, jnp.add)
        acc = lax.fori_loop(0, n_kb, body, jnp.zeros((8, tq), I32))
        return jnp.sum(acc, axis=0, keepdims=True)

    def count16(ref, cand):
        c16 = cand.astype(I16)
        def body(j, acc):
            hit = jnp.where(ref[j] >= c16, jnp.int16(1), jnp.int16(0))
            return acc + _sublane_fold(hit, jnp.add, rows=16)
        acc = lax.fori_loop(0, n_kb, body, jnp.zeros((16, tq), I16))
        return jnp.sum(acc.astype(I32), axis=0, keepdims=True)

    def kth_largest16(ref, kth):
        def bit(b, t):
            cand = t + lax.shift_left(jnp.int32(1), 15 - b)
            return jnp.where(count16(ref, cand) >= kth, cand, t)
        return lax.fori_loop(0, 16, bit, jnp.full((1, tq), _I16_MIN, I32))

    def split_block(j, carry):
        kb = key_ref[j]
        hi_ref[j] = (kb >> 16).astype(I16)
        return carry

    lax.fori_loop(0, n_kb, split_block, 0)
    t_hi = kth_largest16(hi_ref, topk)
    n_above = count16(hi_ref, t_hi + 1)

    def low_block(j, carry):
        kb = key_ref[j]
        lo = ((kb & 0xFFFF) + _I16_MIN).astype(I16)
        hi_ref[j] = jnp.where((kb >> 16) == t_hi, lo, jnp.int16(_I16_MIN))
        return carry

    lax.fori_loop(0, n_kb, low_block, 0)
    t_lo = kth_largest16(hi_ref, topk - n_above)
    t = lax.shift_left(t_hi, 16) + (t_lo - _I16_MIN)
    t = jnp.maximum(t, _INT_MIN + 1)
    n_ge = count(lambda kb, pos: kb >= t)
    n_gt = count(lambda kb, pos: kb > t)
    all_pos = jnp.int32(2 ** nbits_idx - 1)
    n_tie_take = jnp.where(n_ge > topk, topk - n_gt, all_pos)

    def tie_cutoff():
        def pos_bit(b, c):
            cand = c + lax.shift_left(jnp.int32(1), nbits_idx - 1 - b)
            f = count(lambda kb, pos: (kb == t) & (pos < cand))
            return jnp.where(f <= n_tie_take, cand, c)
        return lax.fori_loop(0, nbits_idx, pos_bit, jnp.zeros((1, tq), I32))

    cut = lax.cond(jnp.max(n_ge) > topk, tie_cutoff, lambda: jnp.full((1, tq), all_pos, I32))

    def bias_block(j, carry):
        kb = key_ref[j]
        sel = (kb > t) | ((kb == t) & (key_pos(j) < cut))
        key_ref[j] = lax.bitcast_convert_type(jnp.where(sel, 0.0, _NEG).astype(F32), I32)
        return carry

    lax.fori_loop(0, n_kb, bias_block, 0)

    n_pair = (n_kb + 1) // 2
    neg_bits = lax.bitcast_convert_type(jnp.full((tk, tq), _NEG, F32), I32)

    def masked_block(j, carry):
        key_ref[j] = neg_bits
        return carry

    lax.fori_loop(n_kb, 2 * n_pair, masked_block, 0)

    m_ref[...] = jnp.full(m_ref.shape, _NEG, F32)
    l_ref[...] = jnp.zeros(l_ref.shape, F32)
    acc_ref[...] = jnp.zeros(acc_ref.shape, F32)
    last_blk = ckvt_ref.shape[0] - 1

    def logits(j, lg_buf):
        c_blk = ckv_ref[pl.ds(pl.multiple_of(jnp.minimum(j, last_blk) * tk, tk), tk), :]
        bias = lax.bitcast_convert_type(key_ref[j], F32)
        for hd in range(DSA_HEADS):
            lg_buf[hd] = _dot_nt(c_blk, qabs_ref[:, hd * DSA_LATENT:(hd + 1) * DSA_LATENT]) + bias

    def accumulate(j, lg_buf):
        c_blk_t = ckvt_ref[jnp.minimum(j, last_blk)]
        for hd in range(DSA_HEADS):
            lg = lg_buf[hd]
            m_old = m_ref[hd:hd + 1, :]
            m_new = jnp.maximum(m_old, jnp.max(_sublane_fold(lg, jnp.maximum, ways=1), axis=0, keepdims=True))
            p = jnp.exp2(lg - m_new)
            alpha = jnp.exp2(m_old - m_new)
            l_ref[hd:hd + 1, :] = alpha * l_ref[hd:hd + 1, :] + jnp.sum(_sublane_fold(p, jnp.add, ways=1), axis=0,
                                                                         keepdims=True)
            acc_ref[hd] = alpha * acc_ref[hd] + _dot(c_blk_t, p.astype(BF16))
            m_ref[hd:hd + 1, :] = m_new

    logits(0, lga_ref)

    def attn_pair(mi, carry):
        ja = 2 * mi
        accumulate(ja, lga_ref)
        logits(ja + 1, lgb_ref)
        accumulate(ja + 1, lgb_ref)
        logits(jnp.minimum(ja + 2, 2 * n_pair - 1), lga_ref)
        return carry

    lax.fori_loop(0, n_pair, attn_pair, 0)

    for hd in range(DSA_HEADS):
        o_lat = (acc_ref[hd] / l_ref[hd:hd + 1, :]).T.astype(BF16)
        out_ref[:, hd * DSA_HEAD_DIM:(hd + 1) * DSA_HEAD_DIM] = _dot(o_lat, wuv_ref[hd]).astype(BF16)


def _dsa(qi, wrows, qabs, kidx, ckv, wuv, tq, tk):
    T = qi.shape[0]
    topk = min(TOPK_MAX, T // 4)
    n_kb = T // tk
    ckvt = jnp.transpose(ckv.reshape(n_kb, tk, DSA_LATENT), (0, 2, 1))
    row = lambda n: pl.BlockSpec((tq, n), lambda i: (i, 0))
    body = functools.partial(_dsa_body, tq=tq, tk=tk, topk=topk, nbits_idx=int(T).bit_length())
    return pl.pallas_call(
        body,
        grid=(T // tq,),
        in_specs=[row(qi.shape[1]), pl.BlockSpec((wrows.shape[0], tq), lambda i: (0, i)), row(qabs.shape[1]),
                  _resident(kidx.shape), _resident(ckv.shape), _resident(ckvt.shape), _resident(wuv.shape)],
        out_specs=row(DSA_HEADS * DSA_HEAD_DIM),
        out_shape=jax.ShapeDtypeStruct((T, DSA_HEADS * DSA_HEAD_DIM), BF16),
        scratch_shapes=[pltpu.VMEM((n_kb + n_kb % 2, tk, tq), I32), pltpu.VMEM((n_kb, tk, tq), I16),
                        pltpu.VMEM((DSA_HEADS, tk, tq), F32), pltpu.VMEM((DSA_HEADS, tk, tq), F32),
                        pltpu.VMEM((DSA_HEADS, tq), F32),
                        pltpu.VMEM((DSA_HEADS, tq), F32), pltpu.VMEM((DSA_HEADS, DSA_LATENT, tq), F32)],
        compiler_params=_params(),
        name="dsa",
    )(qi, wrows, qabs, kidx, ckv, ckvt, wuv)


def _log_sigmoid(v):
    return jnp.minimum(v, 0.0) - jnp.log1p(jnp.exp(-jnp.abs(v)))


def _chunk_cumsum(v, axis):
    pos = lax.broadcasted_iota(I32, v.shape, axis) & (CHUNK - 1)
    d = 1
    while d < CHUNK:
        v = v + jnp.where(pos >= d, pltpu.roll(v, d, axis=axis), 0.0)
        d *= 2
    return v


def _mlstm_body(mqk_ref, mv_ref, small_ref, gt_ref, mo_ref, cw_ref, cb_ref, gbc_ref, gbr_ref, ng_ref,
                out_ref, xe_ref, c_ref, n_ref, m_ref, hs_ref, *, rows):
    @pl.when(pl.program_id(0) == 0)
    def _():
        xe_ref[0:8, :] = jnp.zeros((8, xe_ref.shape[1]), F32)
        c_ref[...] = jnp.zeros(c_ref.shape, F32)
        n_ref[...] = jnp.zeros(n_ref.shape, F32)
        m_ref[...] = jnp.zeros(m_ref.shape, F32)

    x = mqk_ref[...]
    xe_ref[8:8 + rows, :] = x
    y = cb_ref[...]
    for j in range(CONV_W - 1):
        y = y + xe_ref[5 + j:5 + j + rows, :] * cw_ref[j:j + 1, :]
    y = y + x * cw_ref[CONV_W - 1:CONV_W, :]
    xe_ref[0:8, :] = x[rows - 8:rows, :]
    qk = y * jax.nn.sigmoid(y)
    nqk = ML_HEADS * ML_QK_DIM
    q_all = (qk[:, :nqk] * (ML_QK_DIM ** -0.5)).astype(BF16)
    k_all = qk[:, nqk:]

    g_col = small_ref[...] + gbc_ref[...]
    g_row = gt_ref[...] + gbr_ref[...]
    b_col = _chunk_cumsum(_log_sigmoid(g_col), 0)
    b_row = _chunk_cumsum(_log_sigmoid(g_row), 1)

    tri = lax.broadcasted_iota(I32, (CHUNK, CHUNK), 1) <= lax.broadcasted_iota(I32, (CHUNK, CHUNK), 0)

    for c in range(rows // CHUNK):
        lo, hi = c * CHUNK, (c + 1) * CHUNK
        for hd in range(ML_HEADS):
            bc = b_col[lo:hi, _S_MF + hd:_S_MF + hd + 1]
            lic = g_col[lo:hi, _S_MI + hd:_S_MI + hd + 1]
            br = b_row[ML_HEADS + hd:ML_HEADS + hd + 1, lo:hi]
            lir = g_row[hd:hd + 1, lo:hi]
            g_tot = bc[CHUNK - 1:CHUNK, :]
            m_prev = m_ref[hd][:, 0:1]

            dmat = jnp.where(tri, bc - br + lir, -jnp.inf)
            inter = bc + m_prev
            m_t = jnp.maximum(inter, jnp.max(dmat, axis=-1, keepdims=True))
            w_intra = jnp.exp(dmat - m_t)
            a_inter = jnp.exp(inter - m_t)

            qh = q_all[lo:hi, hd * ML_QK_DIM:(hd + 1) * ML_QK_DIM]
            kh = k_all[lo:hi, hd * ML_QK_DIM:(hd + 1) * ML_QK_DIM]
            vh = mv_ref[lo:hi, hd * ML_V_DIM:(hd + 1) * ML_V_DIM]
            s_qk = _dot_nt(qh, kh.astype(BF16)) * w_intra
            c_prev = c_ref[hd]
            n_prev = n_ref[hd]
            num = a_inter * _dot(qh, c_prev.astype(BF16)) + _dot(s_qk.astype(BF16), vh)
            den = (a_inter * jnp.sum(qh.astype(F32) * n_prev, axis=-1, keepdims=True)
                   + jnp.sum(s_qk, axis=-1, keepdims=True))
            hs_ref[lo:hi, hd * ML_V_DIM:(hd + 1) * ML_V_DIM] = (
                num / jnp.maximum(jnp.abs(den), jnp.exp(-m_t)))

            m_new = jnp.maximum(g_tot + m_prev, jnp.max(g_tot - br + lir, axis=-1, keepdims=True))
            a_state = jnp.exp(g_tot + m_prev - m_new)
            wk = jnp.exp(g_tot - bc + lic - m_new) * kh
            c_ref[hd] = a_state * c_prev + _dot(wk.T.astype(BF16), vh)
            n_ref[hd] = a_state * n_prev + jnp.sum(wk, axis=0, keepdims=True)
            m_ref[hd] = jnp.broadcast_to(m_new, m_ref.shape[1:])

    for hd in range(ML_HEADS):
        sl = slice(hd * ML_V_DIM, (hd + 1) * ML_V_DIM)
        out_ref[:, sl] = (_rms(hs_ref[:, sl], ng_ref[:, sl]) * jax.nn.sigmoid(mo_ref[:, sl])).astype(BF16)


def _mlstm(mqk, mv, small, gt, mo, cw, cb, gbc, gbr, ng, rows):
    T = mqk.shape[0]
    row = lambda n: pl.BlockSpec((rows, n), lambda i: (i, 0))
    nv = ML_HEADS * ML_V_DIM
    return pl.pallas_call(
        functools.partial(_mlstm_body, rows=rows),
        grid=(T // rows,),
        in_specs=[row(mqk.shape[1]), row(nv), row(_SMALL), pl.BlockSpec((8, rows), lambda i: (1, i)), row(nv),
                  _resident(cw.shape), _resident(cb.shape), _resident(gbc.shape), _resident(gbr.shape),
                  _resident(ng.shape)],
        out_specs=row(nv),
        out_shape=jax.ShapeDtypeStruct((T, nv), BF16),
        scratch_shapes=[pltpu.VMEM((rows + 8, mqk.shape[1]), F32),
                        pltpu.VMEM((ML_HEADS, ML_QK_DIM, ML_V_DIM), F32),
                        pltpu.VMEM((ML_HEADS, 1, ML_QK_DIM), F32),
                        pltpu.VMEM((ML_HEADS, 1, 128), F32),
                        pltpu.VMEM((rows, nv), F32)],
        compiler_params=_params(),
        name="mlstm",
    )(mqk, mv, small, gt, mo, cw, cb, gbc, gbr, ng)


def _memkv_body(mem_ref, g_ref, w_ref, out_ref):
    mn = _rms(mem_ref[...], g_ref[...]).astype(BF16)
    out_ref[...] = _dot(mn, w_ref[...].astype(BF16)).astype(BF16)


def _memkv(mem, g, w, tn):
    M, D = mem.shape
    N = w.shape[1]
    return pl.pallas_call(
        _memkv_body,
        grid=(N // tn,),
        in_specs=[_resident(mem.shape), _resident(g.shape), pl.BlockSpec((D, tn), lambda j: (0, j))],
        out_specs=pl.BlockSpec((M, tn), lambda j: (0, j)),
        out_shape=jax.ShapeDtypeStruct((M, N), BF16),
        compiler_params=_params(),
        name="memkv",
    )(mem, g, w)


def _mixout_body(x_ref, dsa_ref, ml_ref, kv_ref, wo_ref, wq_ref, wc_ref, wr_ref, gx_ref, gf_ref,
                 x2_ref, hf_ref, rl_ref, o_ref):
    nd = dsa_ref.shape[1]
    x1 = x_ref[...] + _dot(dsa_ref[...], wo_ref[0:nd, :]) + _dot(ml_ref[...], wo_ref[nd:, :])
    q = _dot(_rms(x1, gx_ref[...]).astype(BF16), wq_ref[...]).astype(BF16)
    for hd in range(X_HEADS):
        sl = slice(hd * X_HEAD_DIM, (hd + 1) * X_HEAD_DIM)
        lg = _dot_nt(q[:, sl], kv_ref[:, sl]) * (X_HEAD_DIM ** -0.5)
        e = jnp.exp(lg - jnp.max(lg, axis=-1, keepdims=True))
        p = e / jnp.sum(e, axis=-1, keepdims=True)
        v = kv_ref[:, D_MODEL + hd * X_HEAD_DIM:D_MODEL + (hd + 1) * X_HEAD_DIM]
        o_ref[:, sl] = _dot(p.astype(BF16), v).astype(BF16)
    x2 = x1 + _dot(o_ref[...], wc_ref[...])
    x2_ref[...] = x2
    hf = _rms(x2, gf_ref[...]).astype(BF16)
    rl_ref[...] = _dot(hf, wr_ref[...])
    bits = lax.bitcast_convert_type(hf.astype(F32), U32)
    for c in range(_SUBLANES):
        lo = bits[:, (2 * c) * _LANES:(2 * c + 1) * _LANES]
        hi = bits[:, (2 * c + 1) * _LANES:(2 * c + 2) * _LANES]
        hf_ref[pl.ds(c, x2.shape[0], stride=_SUBLANES), :] = (hi & jnp.uint32(0xFFFF0000)) | (lo >> 16)


def _mixout(x, dsa, ml, kv, wo, wq, wc, wr, gx, gf, tm):
    T = x.shape[0]
    row = lambda n: pl.BlockSpec((tm, n), lambda i: (i, 0))
    return pl.pallas_call(
        _mixout_body,
        grid=(T // tm,),
        in_specs=[row(D_MODEL), row(dsa.shape[1]), row(ml.shape[1]), _resident(kv.shape), _resident(wo.shape),
                  _resident(wq.shape), _resident(wc.shape), _resident(wr.shape), _resident(gx.shape),
                  _resident(gf.shape)],
        out_specs=[row(D_MODEL), pl.BlockSpec((tm * _SUBLANES, _LANES), lambda i: (i, 0)), row(wr.shape[1])],
        out_shape=[jax.ShapeDtypeStruct((T, D_MODEL), F32),
                   jax.ShapeDtypeStruct((T * _SUBLANES, _LANES), U32),
                   jax.ShapeDtypeStruct((T, wr.shape[1]), F32)],
        scratch_shapes=[pltpu.VMEM((tm, D_MODEL), BF16)],
        compiler_params=_params(),
        name="mixout",
    )(x, dsa, ml, kv, wo, wq, wc, wr, gx, gf)


def _moe_body(tok_ref, eb_ref, hf_hbm, wg_ref, wu_ref, wd_ref, y_hbm,
              xbuf, ybuf, wgb, wub, wdb, gsem, ysem, *, bm, n_blk):
    e = pl.program_id(0)
    n_valid = eb_ref[N_EXPERTS]
    b_lo = eb_ref[e]
    b_hi = eb_ref[e + 1]

    def tok_words(tok):
        return hf_hbm.at[pl.ds(pl.multiple_of(tok * _SUBLANES, _SUBLANES), _SUBLANES)]

    def gather(blk, s):
        def issue(g, carry):
            for u in range(_SUBLANES):
                r = g * _SUBLANES + u
                pltpu.make_async_copy(tok_words(tok_ref[blk * bm + r]),
                                      xbuf.at[s, pl.ds(pl.multiple_of(r * _SUBLANES, _SUBLANES), _SUBLANES)],
                                      gsem.at[s]).start(priority=1)
            return carry
        lax.fori_loop(0, bm // _SUBLANES, issue, 0)

    def wait_gather(s):
        pltpu.make_async_copy(xbuf.at[s], xbuf.at[s], gsem.at[s]).wait()

    def y_copy(blk, s):
        return pltpu.make_async_copy(ybuf.at[s], y_hbm.at[pl.ds(pl.multiple_of(blk * bm, bm), bm)], ysem.at[s])

    @pl.when(e == 0)
    def _():
        gather(0, 0)
        gather(1, 1)

    @pl.when(b_hi > b_lo)
    def _():
        wgb[...] = wg_ref[0].astype(BF16)
        wub[...] = wu_ref[0].astype(BF16)
        wdb[...] = wd_ref[0].astype(BF16)

        def block(b, carry):
            s = b % _GATHER_SLOTS
            wait_gather(s)
            chunks = []
            for c in range(_SUBLANES):
                w = xbuf[s, pl.ds(c, bm, stride=_SUBLANES), :]
                chunks.append(lax.bitcast_convert_type(w << 16, F32).astype(BF16))
                chunks.append(lax.bitcast_convert_type(w & jnp.uint32(0xFFFF0000), F32).astype(BF16))
            xb = jnp.concatenate(chunks, axis=-1)
            gate = _dot(xb, wgb[...])
            a = gate * jax.nn.sigmoid(gate) * _dot(xb, wub[...])
            y = _dot(a.astype(BF16), wdb[...])

            nxt = jnp.minimum(b + 2, n_blk - 1)
            for r in range(bm):
                pltpu.make_async_copy(tok_words(tok_ref[nxt * bm + r]),
                                      xbuf.at[(b + 2) % _GATHER_SLOTS, pl.ds(r * _SUBLANES, _SUBLANES)],
                                      gsem.at[(b + 2) % _GATHER_SLOTS]).start(priority=1)

            @pl.when(b >= 2)
            def _():
                y_copy(b - 2, b % 2).wait()

            ybuf[b % 2] = y
            y_copy(b, b % 2).start()
            return carry

        lax.fori_loop(b_lo, b_hi, block, 0)

    @pl.when(e == pl.num_programs(0) - 1)
    def _():
        wait_gather(n_valid % _GATHER_SLOTS)
        wait_gather((n_valid + 1) % _GATHER_SLOTS)

        @pl.when(n_valid >= 2)
        def _():
            y_copy(n_valid - 2, n_valid % 2).wait()
        y_copy(n_valid - 1, (n_valid - 1) % 2).wait()
        ybuf[0] = jnp.zeros(ybuf.shape[1:], F32)

        def zero_block(b, carry):
            cp = y_copy(b, 0)
            cp.start()
            cp.wait()
            return carry

        lax.fori_loop(n_valid, n_blk, zero_block, 0)


def _moe(tok, e_blk, hf, wg, wu, wd, n_blk, bm):
    D = wg.shape[1]
    wspec = lambda shape: pl.BlockSpec((1,) + shape, lambda e, *_: (e, 0, 0))
    grid_spec = pltpu.PrefetchScalarGridSpec(
        num_scalar_prefetch=2,
        grid=(N_EXPERTS,),
        in_specs=[pl.BlockSpec(memory_space=pl.ANY),
                  wspec((D, D_EXPERT)), wspec((D, D_EXPERT)), wspec((D_EXPERT, D))],
        out_specs=pl.BlockSpec(memory_space=pl.ANY),
        scratch_shapes=[pltpu.VMEM((_GATHER_SLOTS, bm * _SUBLANES, _LANES), U32),
                        pltpu.VMEM((2, bm, D), F32),
                        pltpu.VMEM((D, D_EXPERT), BF16), pltpu.VMEM((D, D_EXPERT), BF16),
                        pltpu.VMEM((D_EXPERT, D), BF16),
                        pltpu.SemaphoreType.DMA((_GATHER_SLOTS,)), pltpu.SemaphoreType.DMA((2,))],
    )
    return pl.pallas_call(
        functools.partial(_moe_body, bm=bm, n_blk=n_blk),
        grid_spec=grid_spec,
        out_shape=jax.ShapeDtypeStruct((n_blk * bm, D), F32),
        compiler_params=_params(),
        name="moe",
    )(tok, e_blk, hf, wg, wu, wd)


def _route(rl, b_group, b_router, bm):
    N = rl.shape[0]
    g_logits = rl[:, :N_GROUPS] + b_group
    g_prob = jax.nn.softmax(g_logits, axis=-1)
    g_sel = jnp.argmax(g_logits, axis=-1)
    p_g = jnp.take_along_axis(g_prob, g_sel[:, None], axis=-1)
    e_logits = (rl[:, N_GROUPS:N_GROUPS + N_EXPERTS] + b_router).reshape(N, N_GROUPS, EXP_PER_GROUP)
    e_logits = jnp.take_along_axis(e_logits, g_sel[:, None, None], axis=1)[:, 0]
    top_p, top_local = lax.top_k(jax.nn.softmax(e_logits, axis=-1), TOPK_IN_GROUP)
    gates = p_g * top_p / top_p.sum(-1, keepdims=True)
    expert_id = (g_sel[:, None] * EXP_PER_GROUP + top_local).astype(I32)

    A = N * TOPK_IN_GROUP
    flat_e = expert_id.reshape(A)
    onehot = (flat_e[:, None] == jnp.arange(N_EXPERTS, dtype=I32)[None, :])
    seg = _RANK_SEG if A % _RANK_SEG == 0 else A
    oh = onehot.astype(BF16).reshape(A // seg, seg, N_EXPERTS)
    before = (jnp.arange(seg)[None, :] < jnp.arange(seg)[:, None]).astype(BF16)
    within = jnp.einsum('ij,bjk->bik', before, oh, preferred_element_type=F32)
    seg_tot = jnp.sum(oh.astype(F32), axis=1)
    seg_base = jnp.cumsum(seg_tot, axis=0) - seg_tot
    rank = jnp.sum((within + seg_base[:, None, :]) * oh.astype(F32), axis=-1).reshape(A).astype(I32)
    counts = jnp.sum(seg_tot, axis=0).astype(I32)
    padded = (counts + bm - 1) // bm * bm
    pad_ends = jnp.cumsum(padded)
    pad_starts = pad_ends - padded
    row = jnp.sum(jnp.where(onehot, pad_starts[None, :], 0), axis=1) + rank
    n_blk = -(-A // bm) + N_EXPERTS
    row_tok = jnp.zeros((n_blk * bm,), I32).at[row].set(jnp.arange(A, dtype=I32) // TOPK_IN_GROUP)
    e_blk = jnp.concatenate([pad_starts, pad_ends[-1:]]) // bm
    return row_tok, row, gates, e_blk.astype(I32), n_blk


def _final_body(row_ref, x_ref, gate_ref, y_hbm, g_ref, out_ref, ybuf, sem, *, tm):
    i = pl.program_id(0)
    slot = i % 2

    def gather(tile, s):
        def issue(g, carry):
            for u in range(_SUBLANES):
                for k in range(TOPK_IN_GROUP):
                    src = row_ref[(tile * tm + g * _SUBLANES + u) * TOPK_IN_GROUP + k]
                    pltpu.make_async_copy(y_hbm.at[pl.ds(src, 1)], ybuf.at[s, k, g, pl.ds(u, 1)],
                                          sem.at[s]).start()
            return carry
        lax.fori_loop(0, tm // _SUBLANES, issue, 0)

    @pl.when(i == 0)
    def _():
        gather(0, 0)

    @pl.when(i + 1 < pl.num_programs(0))
    def _():
        gather(i + 1, 1 - slot)

    pltpu.make_async_copy(ybuf.at[slot], ybuf.at[slot], sem.at[slot]).wait()
    acc = x_ref[...]
    for k in range(TOPK_IN_GROUP):
        acc = acc + gate_ref[:, k:k + 1] * ybuf[slot, k].reshape(tm, ybuf.shape[-1])
    out_ref[...] = _rms(acc, g_ref[...])


def _final(row, x2, gates, y_rows, g, tm):
    T, D = x2.shape
    grid_spec = pltpu.PrefetchScalarGridSpec(
        num_scalar_prefetch=1,
        grid=(T // tm,),
        in_specs=[pl.BlockSpec((tm, D), lambda i, *_: (i, 0)),
                  pl.BlockSpec((tm, TOPK_IN_GROUP), lambda i, *_: (i, 0)),
                  pl.BlockSpec(memory_space=pl.ANY),
                  pl.BlockSpec(g.shape, lambda i, *_: (0, 0))],
        out_specs=pl.BlockSpec((tm, D), lambda i, *_: (i, 0)),
        scratch_shapes=[pltpu.VMEM((2, TOPK_IN_GROUP, tm // _SUBLANES, _SUBLANES, D), F32),
                        pltpu.SemaphoreType.DMA((2,))],
    )
    return pl.pallas_call(
        functools.partial(_final_body, tm=tm),
        grid_spec=grid_spec,
        out_shape=jax.ShapeDtypeStruct((T, D), F32),
        compiler_params=_params(),
        name="final",
    )(row, x2, gates, y_rows, g)


def _tile_sizes(T):
    pick = lambda want: want if T % want == 0 else CHUNK
    return dict(inproj=pick(256), dsa_q=pick(256), dsa_k=pick(512), mlstm=pick(256), mixout=pick(256),
                final=pick(256), moe=128)


def _layer(x, mem, norm_mix_g, w_in, kv_norm_g, k_idx_norm_g, w_uk, w_uv, conv_w, conv_b, gate_b, ml_norm_g,
           w_out, norm_x_g, mem_norm_g, w_cq, w_ckv, w_co, norm_ffn_g, w_group, b_group, w_router, b_router,
           w_gate, w_up, w_down, out_g):
    T = x.shape[0]
    ts = _tile_sizes(T)
    r2 = lambda v: v.reshape(1, -1)

    w_r = _wprep(w_in, 256)
    wuk_t = jnp.transpose(w_uk, (1, 2, 0)).astype(BF16)
    wuv_t = jnp.transpose(w_uv, (1, 0, 2)).astype(BF16)

    qabs, ckv, qi, kidx, small, mqk, mv, mo = _inproj(
        x, r2(norm_mix_g), w_r, wuk_t, r2(kv_norm_g), r2(k_idx_norm_g), ts["inproj"])

    gate_rows = jnp.transpose(small[:, _S_WI:_S_MF + ML_HEADS])
    dsa_out = _dsa(qi, gate_rows, qabs, kidx, ckv, wuv_t, ts["dsa_q"], ts["dsa_k"])

    gb_col = jnp.zeros((1, _SMALL), F32).at[0, _S_MI:_S_MI + 2 * ML_HEADS].set(gate_b)
    ml_out = _mlstm(mqk, mv, small, gate_rows, mo, conv_w, r2(conv_b), gb_col, gate_b.reshape(-1, 1),
                    r2(ml_norm_g), ts["mlstm"])

    kv = _memkv(mem, r2(mem_norm_g), w_ckv, 512)
    w_rt = jnp.concatenate([w_group, w_router,
                            jnp.zeros((D_MODEL, 128 - N_GROUPS - N_EXPERTS), w_group.dtype)], axis=1)
    x2, hf, rl = _mixout(x, dsa_out, ml_out, kv, w_out.astype(BF16), w_cq.astype(BF16), w_co.astype(BF16),
                         w_rt.astype(BF16), r2(norm_x_g), r2(norm_ffn_g), ts["mixout"])

    bm = ts["moe"]
    row_tok, row, gates, e_blk, n_blk = _route(rl, b_group, b_router, bm)
    y_rows = _moe(row_tok, e_blk, hf, w_gate, w_up, w_down, n_blk, bm)
    return _final(row, x2, gates, y_rows, r2(out_g), ts["final"])


def kernel(x, mem, norm_mix_g, w_in, kv_norm_g, k_idx_norm_g, w_uk, w_uv, conv_w, conv_b, gate_b, ml_norm_g,
           w_out, norm_x_g, mem_norm_g, w_cq, w_ckv, w_co, norm_ffn_g, w_group, b_group, w_router, b_router,
           w_gate, w_up, w_down, final_norm_g):
    B, T, D = x.shape
    assert B == 1 and D == D_MODEL and norm_mix_g.shape[0] == 1 and T % CHUNK == 0
    out = _layer(x[0], mem[0], norm_mix_g[0], w_in[0], kv_norm_g[0], k_idx_norm_g[0], w_uk[0], w_uv[0],
                 conv_w[0], conv_b[0], gate_b[0], ml_norm_g[0], w_out[0], norm_x_g[0], mem_norm_g[0],
                 w_cq[0], w_ckv[0], w_co[0], norm_ffn_g[0], w_group[0], b_group[0], w_router[0], b_router[0],
                 w_gate[0], w_up[0], w_down[0], final_norm_g)
    return out[None]
```

```python
import functools

import jax
import jax.numpy as jnp
import numpy as np
from jax import lax
from jax.experimental import pallas as pl
from jax.experimental.pallas import tpu as pltpu

F32 = jnp.float32
BF16 = jnp.bfloat16
I32 = jnp.int32
I16 = jnp.int16
U32 = jnp.uint32

EPS = 1e-6
CHUNK = 64
D_MODEL = 2048

DSA_HEADS = 8
DSA_HEAD_DIM = 128
DSA_LATENT = 256
IDX_HEADS = 8
IDX_DIM = 64
TOPK_MAX = 256

ML_HEADS = 4
ML_QK_DIM = 128
ML_V_DIM = 256
CONV_W = 4

X_HEADS = 4
X_HEAD_DIM = D_MODEL // X_HEADS

N_GROUPS = 4
EXP_PER_GROUP = 8
N_EXPERTS = N_GROUPS * EXP_PER_GROUP
TOPK_IN_GROUP = 2
D_EXPERT = 512

_O_DQ = 0
_O_CKV = _O_DQ + DSA_HEADS * DSA_HEAD_DIM
_O_QI = _O_CKV + DSA_LATENT
_O_KI = _O_QI + IDX_HEADS * IDX_DIM
_O_WI = _O_KI + IDX_DIM
_O_MQ = _O_WI + IDX_HEADS
_O_MK = _O_MQ + ML_HEADS * ML_QK_DIM
_O_MV = _O_MK + ML_HEADS * ML_QK_DIM
_O_MI = _O_MV + ML_HEADS * ML_V_DIM
_O_MF = _O_MI + ML_HEADS
_O_MO = _O_MF + ML_HEADS
_O_END = _O_MO + ML_HEADS * ML_V_DIM

_G_DQ = (0, 1024)
_G_CKV = (1024, 1280)
_G_QI = (1280, 1792)
_G_SMALL = (1792, 1920)
_G_MQK = (1920, 2944)
_G_MV = (2944, 3968)
_G_MO = (3968, 4992)
_W_COLS = 4992
_S_WI = IDX_DIM
_S_MI = _S_WI + IDX_HEADS
_S_MF = _S_MI + ML_HEADS
_SMALL = 128

_VMEM_LIMIT = 56 * 1024 * 1024
_INT_MIN = -(2 ** 31)
_I16_MIN = -(2 ** 15)
_CHUNK_SHIFT = CHUNK.bit_length() - 1
_LOG2E = 1.4426950408889634
_SUBLANES = 8
_LANES = 128
_GATHER_SLOTS = 5
_GATHER_AHEAD = _GATHER_SLOTS - 1
_RANK_SEG = 512
_NEG = -1e30


def _rms(v, g):
    return v * lax.rsqrt(jnp.mean(v * v, axis=-1, keepdims=True) + EPS) * g


def _dot(a, b):
    return jnp.dot(a, b, preferred_element_type=F32)


def _dot_nt(a, b):
    return lax.dot_general(a, b, (((1,), (1,)), ((), ())), preferred_element_type=F32)


def _resident(shape):
    nd = len(shape)
    return pl.BlockSpec(shape, lambda *_: (0,) * nd, pipeline_mode=pl.Buffered(1))


def _params(n_axes=1):
    return pltpu.CompilerParams(dimension_semantics=("arbitrary",) * n_axes,
                                vmem_limit_bytes=_VMEM_LIMIT)


def _wprep_body(w_ref, out_ref):
    n_small = IDX_DIM + IDX_HEADS
    out_ref[:, _G_DQ[0]:_G_SMALL[0] + n_small] = w_ref[:, _O_DQ:_O_MQ].astype(BF16)
    out_ref[:, _G_SMALL[0] + n_small:_G_SMALL[0] + n_small + 2 * ML_HEADS] = w_ref[:, _O_MI:_O_MO].astype(BF16)
    out_ref[:, _G_SMALL[0] + n_small + 2 * ML_HEADS:_G_SMALL[1]] = jnp.zeros(
        (out_ref.shape[0], _SMALL - n_small - 2 * ML_HEADS), BF16)
    out_ref[:, _G_MQK[0]:_G_MV[1]] = w_ref[:, _O_MQ:_O_MI].astype(BF16)
    out_ref[:, _G_MO[0]:_G_MO[1]] = w_ref[:, _O_MO:_O_END].astype(BF16)


def _wprep(w_in, tk):
    K = w_in.shape[0]
    return pl.pallas_call(
        _wprep_body,
        grid=(K // tk,),
        in_specs=[pl.BlockSpec((tk, w_in.shape[1]), lambda i: (i, 0))],
        out_specs=pl.BlockSpec((tk, _W_COLS), lambda i: (i, 0)),
        out_shape=jax.ShapeDtypeStruct((K, _W_COLS), BF16),
        compiler_params=_params(),
        name="wprep",
    )(w_in)


def _inproj_body(x_ref, g_ref, w_ref, wuk_ref, kvg_ref, kig_ref,
                 qabs_ref, ckv_ref, qi_ref, kidx_ref, small_ref, mqk_ref, mv_ref, mo_ref):
    h = _rms(x_ref[...], g_ref[...]).astype(BF16)

    def proj(grp):
        return _dot(h, w_ref[:, grp[0]:grp[1]])

    dq = proj(_G_DQ)
    for hd in range(DSA_HEADS):
        qh = dq[:, hd * DSA_HEAD_DIM:(hd + 1) * DSA_HEAD_DIM].astype(BF16)
        qa = _dot(qh, wuk_ref[hd]) * (DSA_HEAD_DIM ** -0.5 * _LOG2E)
        qabs_ref[:, hd * DSA_LATENT:(hd + 1) * DSA_LATENT] = qa.astype(BF16)
    ckv_ref[...] = _rms(proj(_G_CKV), kvg_ref[...]).astype(BF16)
    qi_ref[...] = (proj(_G_QI) * (IDX_DIM ** -0.5)).astype(BF16)
    small = proj(_G_SMALL)
    small_ref[...] = small
    kidx_ref[...] = _rms(small[:, :IDX_DIM], kig_ref[...]).astype(BF16)
    mqk_ref[...] = proj(_G_MQK)
    mv_ref[...] = proj(_G_MV).astype(BF16)
    mo_ref[...] = proj(_G_MO)


def _inproj(x, g, w, wuk, kvg, kig, tm):
    T = x.shape[0]
    row = lambda n: pl.BlockSpec((tm, n), lambda i: (i, 0))
    outs = [(8 * DSA_LATENT, BF16), (DSA_LATENT, BF16), (IDX_HEADS * IDX_DIM, BF16), (IDX_DIM, BF16),
            (_SMALL, F32), (2 * ML_HEADS * ML_QK_DIM, F32), (ML_HEADS * ML_V_DIM, BF16),
            (ML_HEADS * ML_V_DIM, F32)]
    return pl.pallas_call(
        _inproj_body,
        grid=(T // tm,),
        in_specs=[row(D_MODEL), _resident(g.shape), _resident(w.shape), _resident(wuk.shape),
                  _resident(kvg.shape), _resident(kig.shape)],
        out_specs=[row(n) for n, _ in outs],
        out_shape=[jax.ShapeDtypeStruct((T, n), dt) for n, dt in outs],
        compiler_params=_params(),
        name="inproj",
    )(x, g, w, wuk, kvg, kig)


def _sublane_fold(v, op, rows=_SUBLANES, ways=4):
    groups = [v[r * rows:(r + 1) * rows, :] for r in range(v.shape[0] // rows)]
    accs = groups[:ways]
    for r in range(ways, len(groups)):
        accs[r % ways] = op(accs[r % ways], groups[r])
    while len(accs) > 1:
        accs = [op(accs[k], accs[k + 1]) if k + 1 < len(accs) else accs[k] for k in range(0, len(accs), 2)]
    return accs[0]


def _dsa_body(qi_ref, wrow_ref, qabs_ref, kidx_ref, ckv_ref, ckvt_ref, wuv_ref, out_ref,
              key_ref, hi_ref, lga_ref, lgb_ref, m_ref, l_ref, acc_ref, *, tq, tk, topk, nbits_idx):
    i = pl.program_id(0)
    n_kb = ((i + 1) * tq + tk - 1) // tk
    w_rows = wrow_ref[0:IDX_HEADS, :] * (IDX_HEADS ** -0.5)
    q_chunk = (i * tq + lax.broadcasted_iota(I32, (1, tq), 1)) >> _CHUNK_SHIFT

    def key_pos(j):
        return j * tk + lax.broadcasted_iota(I32, (tk, 1), 0)

    def score_block(j, carry):
        kx = kidx_ref[pl.ds(pl.multiple_of(j * tk, tk), tk), :]
        s = jnp.zeros((tk, tq), F32)
        for hd in range(IDX_HEADS):
            d = _dot_nt(kx, qi_ref[:, hd * IDX_DIM:(hd + 1) * IDX_DIM])
            s = s + w_rows[hd:hd + 1, :] * jnp.maximum(d, 0.0)
        bits = lax.bitcast_convert_type(s, I32)
        key = bits ^ ((bits >> 31) & 0x7FFFFFFF)
        key_ref[j] = jnp.where((key_pos(j) >> _CHUNK_SHIFT) <= q_chunk, key, _INT_MIN)
        return carry

    lax.fori_loop(0, n_kb, score_block, 0)

    def count(pred):
        def body(j, acc):
            hit = pred(key_ref[j], key_pos(j)).astype(I32)
            return acc + _sublane_fold(hit, jnp.add)
        acc = lax.fori_loop(0, n_kb, body, jnp.zeros((8, tq), I32))
        return jnp.sum(acc, axis=0, keepdims=True)

    def count16(ref, cand):
        c16 = cand.astype(I16)
        def body(j, acc):
            hit = jnp.where(ref[j] >= c16, jnp.int16(1), jnp.int16(0))
            return acc + _sublane_fold(hit, jnp.add, rows=16)
        acc = lax.fori_loop(0, n_kb, body, jnp.zeros((16, tq), I16))
        return jnp.sum(acc.astype(I32), axis=0, keepdims=True)

    def kth_largest16(ref, kth):
        def bit(b, t):
            cand = t + lax.shift_left(jnp.int32(1), 15 - b)
            return jnp.where(count16(ref, cand) >= kth, cand, t)
        return lax.fori_loop(0, 16, bit, jnp.full((1, tq), _I16_MIN, I32))

    def split_block(j, carry):
        kb = key_ref[j]
        hi_ref[j] = (kb >> 16).astype(I16)
        return carry

    lax.fori_loop(0, n_kb, split_block, 0)
    t_hi = kth_largest16(hi_ref, topk)
    n_above = count16(hi_ref, t_hi + 1)

    def low_block(j, carry):
        kb = key_ref[j]
        lo = ((kb & 0xFFFF) + _I16_MIN).astype(I16)
        hi_ref[j] = jnp.where((kb >> 16) == t_hi, lo, jnp.int16(_I16_MIN))
        return carry

    lax.fori_loop(0, n_kb, low_block, 0)
    t_lo = kth_largest16(hi_ref, topk - n_above)
    t = lax.shift_left(t_hi, 16) + (t_lo - _I16_MIN)
    t = jnp.maximum(t, _INT_MIN + 1)
    n_ge = count(lambda kb, pos: kb >= t)
    n_gt = count(lambda kb, pos: kb > t)
    all_pos = jnp.int32(2 ** nbits_idx - 1)
    n_tie_take = jnp.where(n_ge > topk, topk - n_gt, all_pos)

    def tie_cutoff():
        def pos_bit(b, c):
            cand = c + lax.shift_left(jnp.int32(1), nbits_idx - 1 - b)
            f = count(lambda kb, pos: (kb == t) & (pos < cand))
            return jnp.where(f <= n_tie_take, cand, c)
        return lax.fori_loop(0, nbits_idx, pos_bit, jnp.zeros((1, tq), I32))

    cut = lax.cond(jnp.max(n_ge) > topk, tie_cutoff, lambda: jnp.full((1, tq), all_pos, I32))

    def bias_block(j, carry):
        kb = key_ref[j]
        sel = (kb > t) | ((kb == t) & (key_pos(j) < cut))
        key_ref[j] = lax.bitcast_convert_type(jnp.where(sel, 0.0, _NEG).astype(F32), I32)
        return carry

    lax.fori_loop(0, n_kb, bias_block, 0)

    n_pair = (n_kb + 1) // 2
    neg_bits = lax.bitcast_convert_type(jnp.full((tk, tq), _NEG, F32), I32)

    def masked_block(j, carry):
        key_ref[j] = neg_bits
        return carry

    lax.fori_loop(n_kb, 2 * n_pair, masked_block, 0)

    m_ref[...] = jnp.full(m_ref.shape, _NEG, F32)
    l_ref[...] = jnp.zeros(l_ref.shape, F32)
    acc_ref[...] = jnp.zeros(acc_ref.shape, F32)
    last_blk = ckvt_ref.shape[0] - 1

    def logits(j, lg_buf):
        c_blk = ckv_ref[pl.ds(pl.multiple_of(jnp.minimum(j, last_blk) * tk, tk), tk), :]
        bias = lax.bitcast_convert_type(key_ref[j], F32)
        for hd in range(DSA_HEADS):
            lg_buf[hd] = _dot_nt(c_blk, qabs_ref[:, hd * DSA_LATENT:(hd + 1) * DSA_LATENT]) + bias

    def accumulate(j, lg_buf):
        c_blk_t = ckvt_ref[jnp.minimum(j, last_blk)]
        for hd in range(DSA_HEADS):
            lg = lg_buf[hd]
            m_old = m_ref[hd:hd + 1, :]
            m_new = jnp.maximum(m_old, jnp.max(_sublane_fold(lg, jnp.maximum, ways=1), axis=0, keepdims=True))
            p = jnp.exp2(lg - m_new)
            alpha = jnp.exp2(m_old - m_new)
            l_ref[hd:hd + 1, :] = alpha * l_ref[hd:hd + 1, :] + jnp.sum(_sublane_fold(p, jnp.add, ways=1), axis=0,
                                                                         keepdims=True)
            acc_ref[hd] = alpha * acc_ref[hd] + _dot(c_blk_t, p.astype(BF16))
            m_ref[hd:hd + 1, :] = m_new

    logits(0, lga_ref)

    def attn_pair(mi, carry):
        ja = 2 * mi
        accumulate(ja, lga_ref)
        logits(ja + 1, lgb_ref)
        accumulate(ja + 1, lgb_ref)
        logits(jnp.minimum(ja + 2, 2 * n_pair - 1), lga_ref)
        return carry

    lax.fori_loop(0, n_pair, attn_pair, 0)

    for hd in range(DSA_HEADS):
        o_lat = (acc_ref[hd] / l_ref[hd:hd + 1, :]).T.astype(BF16)
        out_ref[:, hd * DSA_HEAD_DIM:(hd + 1) * DSA_HEAD_DIM] = _dot(o_lat, wuv_ref[hd]).astype(BF16)


def _dsa(qi, wrows, qabs, kidx, ckv, wuv, tq, tk):
    T = qi.shape[0]
    topk = min(TOPK_MAX, T // 4)
    n_kb = T // tk
    ckvt = jnp.transpose(ckv.reshape(n_kb, tk, DSA_LATENT), (0, 2, 1))
    row = lambda n: pl.BlockSpec((tq, n), lambda i: (i, 0))
    body = functools.partial(_dsa_body, tq=tq, tk=tk, topk=topk, nbits_idx=int(T).bit_length())
    return pl.pallas_call(
        body,
        grid=(T // tq,),
        in_specs=[row(qi.shape[1]), pl.BlockSpec((wrows.shape[0], tq), lambda i: (0, i)), row(qabs.shape[1]),
                  _resident(kidx.shape), _resident(ckv.shape), _resident(ckvt.shape), _resident(wuv.shape)],
        out_specs=row(DSA_HEADS * DSA_HEAD_DIM),
        out_shape=jax.ShapeDtypeStruct((T, DSA_HEADS * DSA_HEAD_DIM), BF16),
        scratch_shapes=[pltpu.VMEM((n_kb + n_kb % 2, tk, tq), I32), pltpu.VMEM((n_kb, tk, tq), I16),
                        pltpu.VMEM((DSA_HEADS, tk, tq), F32), pltpu.VMEM((DSA_HEADS, tk, tq), F32),
                        pltpu.VMEM((DSA_HEADS, tq), F32),
                        pltpu.VMEM((DSA_HEADS, tq), F32), pltpu.VMEM((DSA_HEADS, DSA_LATENT, tq), F32)],
        compiler_params=_params(),
        name="dsa",
    )(qi, wrows, qabs, kidx, ckv, ckvt, wuv)


def _log_sigmoid(v):
    return jnp.minimum(v, 0.0) - jnp.log1p(jnp.exp(-jnp.abs(v)))


def _chunk_cumsum(v, axis):
    pos = lax.broadcasted_iota(I32, v.shape, axis) & (CHUNK - 1)
    d = 1
    while d < CHUNK:
        v = v + jnp.where(pos >= d, pltpu.roll(v, d, axis=axis), 0.0)
        d *= 2
    return v


def _mlstm_body(mqk_ref, mv_ref, small_ref, gt_ref, mo_ref, cw_ref, cb_ref, gbc_ref, gbr_ref, ng_ref,
                out_ref, xe_ref, c_ref, n_ref, m_ref, hs_ref, *, rows):
    @pl.when(pl.program_id(0) == 0)
    def _():
        xe_ref[0:8, :] = jnp.zeros((8, xe_ref.shape[1]), F32)
        c_ref[...] = jnp.zeros(c_ref.shape, F32)
        n_ref[...] = jnp.zeros(n_ref.shape, F32)
        m_ref[...] = jnp.zeros(m_ref.shape, F32)

    x = mqk_ref[...]
    xe_ref[8:8 + rows, :] = x
    y = cb_ref[...]
    for j in range(CONV_W - 1):
        y = y + xe_ref[5 + j:5 + j + rows, :] * cw_ref[j:j + 1, :]
    y = y + x * cw_ref[CONV_W - 1:CONV_W, :]
    xe_ref[0:8, :] = x[rows - 8:rows, :]
    qk = y * jax.nn.sigmoid(y)
    nqk = ML_HEADS * ML_QK_DIM
    q_all = (qk[:, :nqk] * (ML_QK_DIM ** -0.5)).astype(BF16)
    k_all = qk[:, nqk:]

    g_col = small_ref[...] + gbc_ref[...]
    g_row = gt_ref[...] + gbr_ref[...]
    b_col = _chunk_cumsum(_log_sigmoid(g_col), 0)
    b_row = _chunk_cumsum(_log_sigmoid(g_row), 1)

    tri = lax.broadcasted_iota(I32, (CHUNK, CHUNK), 1) <= lax.broadcasted_iota(I32, (CHUNK, CHUNK), 0)

    for c in range(rows // CHUNK):
        lo, hi = c * CHUNK, (c + 1) * CHUNK
        for hd in range(ML_HEADS):
            bc = b_col[lo:hi, _S_MF + hd:_S_MF + hd + 1]
            lic = g_col[lo:hi, _S_MI + hd:_S_MI + hd + 1]
            br = b_row[ML_HEADS + hd:ML_HEADS + hd + 1, lo:hi]
            lir = g_row[hd:hd + 1, lo:hi]
            g_tot = bc[CHUNK - 1:CHUNK, :]
            m_prev = m_ref[hd][:, 0:1]

            dmat = jnp.where(tri, bc - br + lir, -jnp.inf)
            inter = bc + m_prev
            m_t = jnp.maximum(inter, jnp.max(dmat, axis=-1, keepdims=True))
            w_intra = jnp.exp(dmat - m_t)
            a_inter = jnp.exp(inter - m_t)

            qh = q_all[lo:hi, hd * ML_QK_DIM:(hd + 1) * ML_QK_DIM]
            kh = k_all[lo:hi, hd * ML_QK_DIM:(hd + 1) * ML_QK_DIM]
            vh = mv_ref[lo:hi, hd * ML_V_DIM:(hd + 1) * ML_V_DIM]
            s_qk = _dot_nt(qh, kh.astype(BF16)) * w_intra
            c_prev = c_ref[hd]
            n_prev = n_ref[hd]
            num = a_inter * _dot(qh, c_prev.astype(BF16)) + _dot(s_qk.astype(BF16), vh)
            den = (a_inter * jnp.sum(qh.astype(F32) * n_prev, axis=-1, keepdims=True)
                   + jnp.sum(s_qk, axis=-1, keepdims=True))
            hs_ref[lo:hi, hd * ML_V_DIM:(hd + 1) * ML_V_DIM] = (
                num / jnp.maximum(jnp.abs(den), jnp.exp(-m_t)))

            m_new = jnp.maximum(g_tot + m_prev, jnp.max(g_tot - br + lir, axis=-1, keepdims=True))
            a_state = jnp.exp(g_tot + m_prev - m_new)
            wk = jnp.exp(g_tot - bc + lic - m_new) * kh
            c_ref[hd] = a_state * c_prev + _dot(wk.T.astype(BF16), vh)
            n_ref[hd] = a_state * n_prev + jnp.sum(wk, axis=0, keepdims=True)
            m_ref[hd] = jnp.broadcast_to(m_new, m_ref.shape[1:])

    for hd in range(ML_HEADS):
        sl = slice(hd * ML_V_DIM, (hd + 1) * ML_V_DIM)
        out_ref[:, sl] = (_rms(hs_ref[:, sl], ng_ref[:, sl]) * jax.nn.sigmoid(mo_ref[:, sl])).astype(BF16)


def _mlstm(mqk, mv, small, gt, mo, cw, cb, gbc, gbr, ng, rows):
    T = mqk.shape[0]
    row = lambda n: pl.BlockSpec((rows, n), lambda i: (i, 0))
    nv = ML_HEADS * ML_V_DIM
    return pl.pallas_call(
        functools.partial(_mlstm_body, rows=rows),
        grid=(T // rows,),
        in_specs=[row(mqk.shape[1]), row(nv), row(_SMALL), pl.BlockSpec((8, rows), lambda i: (1, i)), row(nv),
                  _resident(cw.shape), _resident(cb.shape), _resident(gbc.shape), _resident(gbr.shape),
                  _resident(ng.shape)],
        out_specs=row(nv),
        out_shape=jax.ShapeDtypeStruct((T, nv), BF16),
        scratch_shapes=[pltpu.VMEM((rows + 8, mqk.shape[1]), F32),
                        pltpu.VMEM((ML_HEADS, ML_QK_DIM, ML_V_DIM), F32),
                        pltpu.VMEM((ML_HEADS, 1, ML_QK_DIM), F32),
                        pltpu.VMEM((ML_HEADS, 1, 128), F32),
                        pltpu.VMEM((rows, nv), F32)],
        compiler_params=_params(),
        name="mlstm",
    )(mqk, mv, small, gt, mo, cw, cb, gbc, gbr, ng)


def _memkv_body(mem_ref, g_ref, w_ref, out_ref):
    mn = _rms(mem_ref[...], g_ref[...]).astype(BF16)
    out_ref[...] = _dot(mn, w_ref[...].astype(BF16)).astype(BF16)


def _memkv(mem, g, w, tn):
    M, D = mem.shape
    N = w.shape[1]
    return pl.pallas_call(
        _memkv_body,
        grid=(N // tn,),
        in_specs=[_resident(mem.shape), _resident(g.shape), pl.BlockSpec((D, tn), lambda j: (0, j))],
        out_specs=pl.BlockSpec((M, tn), lambda j: (0, j)),
        out_shape=jax.ShapeDtypeStruct((M, N), BF16),
        compiler_params=_params(),
        name="memkv",
    )(mem, g, w)


def _mixout_body(x_ref, dsa_ref, ml_ref, kv_ref, wo_ref, wq_ref, wc_ref, wr_ref, gx_ref, gf_ref,
                 x2_ref, hf_ref, rl_ref, o_ref):
    nd = dsa_ref.shape[1]
    x1 = x_ref[...] + _dot(dsa_ref[...], wo_ref[0:nd, :]) + _dot(ml_ref[...], wo_ref[nd:, :])
    q = _dot(_rms(x1, gx_ref[...]).astype(BF16), wq_ref[...]).astype(BF16)
    for hd in range(X_HEADS):
        sl = slice(hd * X_HEAD_DIM, (hd + 1) * X_HEAD_DIM)
        lg = _dot_nt(q[:, sl], kv_ref[:, sl]) * (X_HEAD_DIM ** -0.5)
        e = jnp.exp(lg - jnp.max(lg, axis=-1, keepdims=True))
        p = e / jnp.sum(e, axis=-1, keepdims=True)
        v = kv_ref[:, D_MODEL + hd * X_HEAD_DIM:D_MODEL + (hd + 1) * X_HEAD_DIM]
        o_ref[:, sl] = _dot(p.astype(BF16), v).astype(BF16)
    x2 = x1 + _dot(o_ref[...], wc_ref[...])
    x2_ref[...] = x2
    hf = _rms(x2, gf_ref[...]).astype(BF16)
    rl_ref[...] = _dot(hf, wr_ref[...])
    bits = lax.bitcast_convert_type(hf.astype(F32), U32)
    for c in range(_SUBLANES):
        lo = bits[:, (2 * c) * _LANES:(2 * c + 1) * _LANES]
        hi = bits[:, (2 * c + 1) * _LANES:(2 * c + 2) * _LANES]
        hf_ref[pl.ds(c, x2.shape[0], stride=_SUBLANES), :] = (hi & jnp.uint32(0xFFFF0000)) | (lo >> 16)


def _mixout(x, dsa, ml, kv, wo, wq, wc, wr, gx, gf, tm):
    T = x.shape[0]
    row = lambda n: pl.BlockSpec((tm, n), lambda i: (i, 0))
    return pl.pallas_call(
        _mixout_body,
        grid=(T // tm,),
        in_specs=[row(D_MODEL), row(dsa.shape[1]), row(ml.shape[1]), _resident(kv.shape), _resident(wo.shape),
                  _resident(wq.shape), _resident(wc.shape), _resident(wr.shape), _resident(gx.shape),
                  _resident(gf.shape)],
        out_specs=[row(D_MODEL), pl.BlockSpec((tm * _SUBLANES, _LANES), lambda i: (i, 0)), row(wr.shape[1])],
        out_shape=[jax.ShapeDtypeStruct((T, D_MODEL), F32),
                   jax.ShapeDtypeStruct((T * _SUBLANES, _LANES), U32),
                   jax.ShapeDtypeStruct((T, wr.shape[1]), F32)],
        scratch_shapes=[pltpu.VMEM((tm, D_MODEL), BF16)],
        compiler_params=_params(),
        name="mixout",
    )(x, dsa, ml, kv, wo, wq, wc, wr, gx, gf)


def _moe_body(tok_ref, eb_ref, hf_hbm, wg_ref, wu_ref, wd_ref, y_hbm,
              xbuf, ybuf, wgb, wub, wdb, gsem, ysem, *, bm, n_blk):
    e = pl.program_id(0)
    n_valid = eb_ref[N_EXPERTS]
    b_lo = eb_ref[e]
    b_hi = eb_ref[e + 1]

    def tok_words(tok):
        return hf_hbm.at[pl.ds(pl.multiple_of(tok * _SUBLANES, _SUBLANES), _SUBLANES)]

    def gather(blk, s):
        def issue(g, carry):
            for u in range(_SUBLANES):
                r = g * _SUBLANES + u
                pltpu.make_async_copy(tok_words(tok_ref[blk * bm + r]),
                                      xbuf.at[s, pl.ds(pl.multiple_of(r * _SUBLANES, _SUBLANES), _SUBLANES)],
                                      gsem.at[s]).start(priority=1)
            return carry
        lax.fori_loop(0, bm // _SUBLANES, issue, 0)

    def wait_gather(s):
        pltpu.make_async_copy(xbuf.at[s], xbuf.at[s], gsem.at[s]).wait()

    def y_copy(blk, s):
        return pltpu.make_async_copy(ybuf.at[s], y_hbm.at[pl.ds(pl.multiple_of(blk * bm, bm), bm)], ysem.at[s])

    @pl.when(e == 0)
    def _():
        for k in range(_GATHER_AHEAD):
            gather(k, k)

    @pl.when(b_hi > b_lo)
    def _():
        wgb[...] = wg_ref[0].astype(BF16)
        wub[...] = wu_ref[0].astype(BF16)
        wdb[...] = wd_ref[0].astype(BF16)

        def block(b, carry):
            s = b % _GATHER_SLOTS
            wait_gather(s)
            chunks = []
            for c in range(_SUBLANES):
                w = xbuf[s, pl.ds(c, bm, stride=_SUBLANES), :]
                chunks.append(lax.bitcast_convert_type(w << 16, F32).astype(BF16))
                chunks.append(lax.bitcast_convert_type(w & jnp.uint32(0xFFFF0000), F32).astype(BF16))
            xb = jnp.concatenate(chunks, axis=-1)
            gate = _dot(xb, wgb[...])
            a = gate * jax.nn.sigmoid(gate) * _dot(xb, wub[...])
            y = _dot(a.astype(BF16), wdb[...])

            nxt = jnp.minimum(b + _GATHER_AHEAD, n_blk - 1)
            for r in range(bm):
                pltpu.make_async_copy(tok_words(tok_ref[nxt * bm + r]),
                                      xbuf.at[(b + _GATHER_AHEAD) % _GATHER_SLOTS, pl.ds(r * _SUBLANES, _SUBLANES)],
                                      gsem.at[(b + _GATHER_AHEAD) % _GATHER_SLOTS]).start(priority=1)

            @pl.when(b >= 2)
            def _():
                y_copy(b - 2, b % 2).wait()

            ybuf[b % 2] = y
            y_copy(b, b % 2).start()
            return carry

        lax.fori_loop(b_lo, b_hi, block, 0)

    @pl.when(e == pl.num_programs(0) - 1)
    def _():
        for k in range(_GATHER_AHEAD):
            wait_gather((n_valid + k) % _GATHER_SLOTS)

        @pl.when(n_valid >= 2)
        def _():
            y_copy(n_valid - 2, n_valid % 2).wait()
        y_copy(n_valid - 1, (n_valid - 1) % 2).wait()
        ybuf[0] = jnp.zeros(ybuf.shape[1:], F32)

        def zero_block(b, carry):
            cp = y_copy(b, 0)
            cp.start()
            cp.wait()
            return carry

        lax.fori_loop(n_valid, n_blk, zero_block, 0)


def _moe(tok, e_blk, hf, wg, wu, wd, n_blk, bm):
    D = wg.shape[1]
    wspec = lambda shape: pl.BlockSpec((1,) + shape, lambda e, *_: (e, 0, 0))
    grid_spec = pltpu.PrefetchScalarGridSpec(
        num_scalar_prefetch=2,
        grid=(N_EXPERTS,),
        in_specs=[pl.BlockSpec(memory_space=pl.ANY),
                  wspec((D, D_EXPERT)), wspec((D, D_EXPERT)), wspec((D_EXPERT, D))],
        out_specs=pl.BlockSpec(memory_space=pl.ANY),
        scratch_shapes=[pltpu.VMEM((_GATHER_SLOTS, bm * _SUBLANES, _LANES), U32),
                        pltpu.VMEM((2, bm, D), F32),
                        pltpu.VMEM((D, D_EXPERT), BF16), pltpu.VMEM((D, D_EXPERT), BF16),
                        pltpu.VMEM((D_EXPERT, D), BF16),
                        pltpu.SemaphoreType.DMA((_GATHER_SLOTS,)), pltpu.SemaphoreType.DMA((2,))],
    )
    return pl.pallas_call(
        functools.partial(_moe_body, bm=bm, n_blk=n_blk),
        grid_spec=grid_spec,
        out_shape=jax.ShapeDtypeStruct((n_blk * bm, D), F32),
        compiler_params=_params(),
        name="moe",
    )(tok, e_blk, hf, wg, wu, wd)


def _route(rl, b_group, b_router, bm):
    N = rl.shape[0]
    g_logits = rl[:, :N_GROUPS] + b_group
    g_prob = jax.nn.softmax(g_logits, axis=-1)
    g_sel = jnp.argmax(g_logits, axis=-1)
    p_g = jnp.take_along_axis(g_prob, g_sel[:, None], axis=-1)
    e_logits = (rl[:, N_GROUPS:N_GROUPS + N_EXPERTS] + b_router).reshape(N, N_GROUPS, EXP_PER_GROUP)
    e_logits = jnp.take_along_axis(e_logits, g_sel[:, None, None], axis=1)[:, 0]
    top_p, top_local = lax.top_k(jax.nn.softmax(e_logits, axis=-1), TOPK_IN_GROUP)
    gates = p_g * top_p / top_p.sum(-1, keepdims=True)
    expert_id = (g_sel[:, None] * EXP_PER_GROUP + top_local).astype(I32)

    A = N * TOPK_IN_GROUP
    flat_e = expert_id.reshape(A)
    onehot = (flat_e[:, None] == jnp.arange(N_EXPERTS, dtype=I32)[None, :])
    seg = _RANK_SEG if A % _RANK_SEG == 0 else A
    oh = onehot.astype(BF16).reshape(A // seg, seg, N_EXPERTS)
    before = (jnp.arange(seg)[None, :] < jnp.arange(seg)[:, None]).astype(BF16)
    within = jnp.einsum('ij,bjk->bik', before, oh, preferred_element_type=F32)
    seg_tot = jnp.sum(oh.astype(F32), axis=1)
    seg_base = jnp.cumsum(seg_tot, axis=0) - seg_tot
    rank = jnp.sum((within + seg_base[:, None, :]) * oh.astype(F32), axis=-1).reshape(A).astype(I32)
    counts = jnp.sum(seg_tot, axis=0).astype(I32)
    padded = (counts + bm - 1) // bm * bm
    pad_ends = jnp.cumsum(padded)
    pad_starts = pad_ends - padded
    row = jnp.sum(jnp.where(onehot, pad_starts[None, :], 0), axis=1) + rank
    n_blk = -(-A // bm) + N_EXPERTS
    row_tok = jnp.zeros((n_blk * bm,), I32).at[row].set(jnp.arange(A, dtype=I32) // TOPK_IN_GROUP)
    e_blk = jnp.concatenate([pad_starts, pad_ends[-1:]]) // bm
    return row_tok, row, gates, e_blk.astype(I32), n_blk


def _final_body(row_ref, x_ref, gate_ref, y_hbm, g_ref, out_ref, ybuf, sem, *, tm):
    i = pl.program_id(0)
    slot = i % 2

    def gather(tile, s):
        def issue(g, carry):
            for u in range(_SUBLANES):
                for k in range(TOPK_IN_GROUP):
                    src = row_ref[(tile * tm + g * _SUBLANES + u) * TOPK_IN_GROUP + k]
                    pltpu.make_async_copy(y_hbm.at[pl.ds(src, 1)], ybuf.at[s, k, g, pl.ds(u, 1)],
                                          sem.at[s]).start()
            return carry
        lax.fori_loop(0, tm // _SUBLANES, issue, 0)

    @pl.when(i == 0)
    def _():
        gather(0, 0)

    @pl.when(i + 1 < pl.num_programs(0))
    def _():
        gather(i + 1, 1 - slot)

    pltpu.make_async_copy(ybuf.at[slot], ybuf.at[slot], sem.at[slot]).wait()
    acc = x_ref[...]
    for k in range(TOPK_IN_GROUP):
        acc = acc + gate_ref[:, k:k + 1] * ybuf[slot, k].reshape(tm, ybuf.shape[-1])
    out_ref[...] = _rms(acc, g_ref[...])


def _final(row, x2, gates, y_rows, g, tm):
    T, D = x2.shape
    grid_spec = pltpu.PrefetchScalarGridSpec(
        num_scalar_prefetch=1,
        grid=(T // tm,),
        in_specs=[pl.BlockSpec((tm, D), lambda i, *_: (i, 0)),
                  pl.BlockSpec((tm, TOPK_IN_GROUP), lambda i, *_: (i, 0)),
                  pl.BlockSpec(memory_space=pl.ANY),
                  pl.BlockSpec(g.shape, lambda i, *_: (0, 0))],
        out_specs=pl.BlockSpec((tm, D), lambda i, *_: (i, 0)),
        scratch_shapes=[pltpu.VMEM((2, TOPK_IN_GROUP, tm // _SUBLANES, _SUBLANES, D), F32),
                        pltpu.SemaphoreType.DMA((2,))],
    )
    return pl.pallas_call(
        functools.partial(_final_body, tm=tm),
        grid_spec=grid_spec,
        out_shape=jax.ShapeDtypeStruct((T, D), F32),
        compiler_params=_params(),
        name="final",
    )(row, x2, gates, y_rows, g)


def _tile_sizes(T):
    pick = lambda want: want if T % want == 0 else CHUNK
    return dict(inproj=pick(256), dsa_q=pick(256), dsa_k=pick(512), mlstm=pick(256), mixout=pick(256),
                final=pick(256), moe=128)


def _layer(x, mem, norm_mix_g, w_in, kv_norm_g, k_idx_norm_g, w_uk, w_uv, conv_w, conv_b, gate_b, ml_norm_g,
           w_out, norm_x_g, mem_norm_g, w_cq, w_ckv, w_co, norm_ffn_g, w_group, b_group, w_router, b_router,
           w_gate, w_up, w_down, out_g):
    T = x.shape[0]
    ts = _tile_sizes(T)
    r2 = lambda v: v.reshape(1, -1)

    w_r = _wprep(w_in, 256)
    wuk_t = jnp.transpose(w_uk, (1, 2, 0)).astype(BF16)
    wuv_t = jnp.transpose(w_uv, (1, 0, 2)).astype(BF16)

    qabs, ckv, qi, kidx, small, mqk, mv, mo = _inproj(
        x, r2(norm_mix_g), w_r, wuk_t, r2(kv_norm_g), r2(k_idx_norm_g), ts["inproj"])

    gate_rows = jnp.transpose(small[:, _S_WI:_S_MF + ML_HEADS])
    dsa_out = _dsa(qi, gate_rows, qabs, kidx, ckv, wuv_t, ts["dsa_q"], ts["dsa_k"])

    gb_col = jnp.zeros((1, _SMALL), F32).at[0, _S_MI:_S_MI + 2 * ML_HEADS].set(gate_b)
    ml_out = _mlstm(mqk, mv, small, gate_rows, mo, conv_w, r2(conv_b), gb_col, gate_b.reshape(-1, 1),
                    r2(ml_norm_g), ts["mlstm"])

    kv = _memkv(mem, r2(mem_norm_g), w_ckv, 512)
    w_rt = jnp.concatenate([w_group, w_router,
                            jnp.zeros((D_MODEL, 128 - N_GROUPS - N_EXPERTS), w_group.dtype)], axis=1)
    x2, hf, rl = _mixout(x, dsa_out, ml_out, kv, w_out.astype(BF16), w_cq.astype(BF16), w_co.astype(BF16),
                         w_rt.astype(BF16), r2(norm_x_g), r2(norm_ffn_g), ts["mixout"])

    bm = ts["moe"]
    row_tok, row, gates, e_blk, n_blk = _route(rl, b_group, b_router, bm)
    y_rows = _moe(row_tok, e_blk, hf, w_gate, w_up, w_down, n_blk, bm)
    return _final(row, x2, gates, y_rows, r2(out_g), ts["final"])


def kernel(x, mem, norm_mix_g, w_in, kv_norm_g, k_idx_norm_g, w_uk, w_uv, conv_w, conv_b, gate_b, ml_norm_g,
           w_out, norm_x_g, mem_norm_g, w_cq, w_ckv, w_co, norm_ffn_g, w_group, b_group, w_router, b_router,
           w_gate, w_up, w_down, final_norm_g):
    B, T, D = x.shape
    assert B == 1 and D == D_MODEL and norm_mix_g.shape[0] == 1 and T % CHUNK == 0
    out = _layer(x[0], mem[0], norm_mix_g[0], w_in[0], kv_norm_g[0], k_idx_norm_g[0], w_uk[0], w_uv[0],
                 conv_w[0], conv_b[0], gate_b[0], ml_norm_g[0], w_out[0], norm_x_g[0], mem_norm_g[0],
                 w_cq[0], w_ckv[0], w_co[0], norm_ffn_g[0], w_group[0], b_group[0], w_router[0], b_router[0],
                 w_gate[0], w_up[0], w_down[0], final_norm_g)
    return out[None]
```

```python
import functools

import jax
import jax.numpy as jnp
import numpy as np
from jax import lax
from jax.experimental import pallas as pl
from jax.experimental.pallas import tpu as pltpu

F32 = jnp.float32
BF16 = jnp.bfloat16
I32 = jnp.int32
I16 = jnp.int16
U32 = jnp.uint32

EPS = 1e-6
CHUNK = 64
D_MODEL = 2048

DSA_HEADS = 8
DSA_HEAD_DIM = 128
DSA_LATENT = 256
IDX_HEADS = 8
IDX_DIM = 64
TOPK_MAX = 256

ML_HEADS = 4
ML_QK_DIM = 128
ML_V_DIM = 256
CONV_W = 4

X_HEADS = 4
X_HEAD_DIM = D_MODEL // X_HEADS

N_GROUPS = 4
EXP_PER_GROUP = 8
N_EXPERTS = N_GROUPS * EXP_PER_GROUP
TOPK_IN_GROUP = 2
D_EXPERT = 512

_O_DQ = 0
_O_CKV = _O_DQ + DSA_HEADS * DSA_HEAD_DIM
_O_QI = _O_CKV + DSA_LATENT
_O_KI = _O_QI + IDX_HEADS * IDX_DIM
_O_WI = _O_KI + IDX_DIM
_O_MQ = _O_WI + IDX_HEADS
_O_MK = _O_MQ + ML_HEADS * ML_QK_DIM
_O_MV = _O_MK + ML_HEADS * ML_QK_DIM
_O_MI = _O_MV + ML_HEADS * ML_V_DIM
_O_MF = _O_MI + ML_HEADS
_O_MO = _O_MF + ML_HEADS
_O_END = _O_MO + ML_HEADS * ML_V_DIM

_G_DQ = (0, 1024)
_G_CKV = (1024, 1280)
_G_QI = (1280, 1792)
_G_SMALL = (1792, 1920)
_G_MQK = (1920, 2944)
_G_MV = (2944, 3968)
_G_MO = (3968, 4992)
_W_COLS = 4992
_S_WI = IDX_DIM
_S_MI = _S_WI + IDX_HEADS
_S_MF = _S_MI + ML_HEADS
_SMALL = 128

_VMEM_LIMIT = 56 * 1024 * 1024
_INT_MIN = -(2 ** 31)
_I16_MIN = -(2 ** 15)
_CHUNK_SHIFT = CHUNK.bit_length() - 1
_LOG2E = 1.4426950408889634
_SUBLANES = 8
_LANES = 128
_GATHER_SLOTS = 9
_GATHER_AHEAD = _GATHER_SLOTS - 1
_RANK_SEG = 512
_NEG = -1e30


def _rms(v, g):
    return v * lax.rsqrt(jnp.mean(v * v, axis=-1, keepdims=True) + EPS) * g


def _dot(a, b):
    return jnp.dot(a, b, preferred_element_type=F32)


def _dot_nt(a, b):
    return lax.dot_general(a, b, (((1,), (1,)), ((), ())), preferred_element_type=F32)


def _resident(shape):
    nd = len(shape)
    return pl.BlockSpec(shape, lambda *_: (0,) * nd, pipeline_mode=pl.Buffered(1))


def _params(n_axes=1):
    return pltpu.CompilerParams(dimension_semantics=("arbitrary",) * n_axes,
                                vmem_limit_bytes=_VMEM_LIMIT)


def _wprep_body(w_ref, out_ref):
    n_small = IDX_DIM + IDX_HEADS
    out_ref[:, _G_DQ[0]:_G_SMALL[0] + n_small] = w_ref[:, _O_DQ:_O_MQ].astype(BF16)
    out_ref[:, _G_SMALL[0] + n_small:_G_SMALL[0] + n_small + 2 * ML_HEADS] = w_ref[:, _O_MI:_O_MO].astype(BF16)
    out_ref[:, _G_SMALL[0] + n_small + 2 * ML_HEADS:_G_SMALL[1]] = jnp.zeros(
        (out_ref.shape[0], _SMALL - n_small - 2 * ML_HEADS), BF16)
    out_ref[:, _G_MQK[0]:_G_MV[1]] = w_ref[:, _O_MQ:_O_MI].astype(BF16)
    out_ref[:, _G_MO[0]:_G_MO[1]] = w_ref[:, _O_MO:_O_END].astype(BF16)


def _wprep(w_in, tk):
    K = w_in.shape[0]
    return pl.pallas_call(
        _wprep_body,
        grid=(K // tk,),
        in_specs=[pl.BlockSpec((tk, w_in.shape[1]), lambda i: (i, 0))],
        out_specs=pl.BlockSpec((tk, _W_COLS), lambda i: (i, 0)),
        out_shape=jax.ShapeDtypeStruct((K, _W_COLS), BF16),
        compiler_params=_params(),
        name="wprep",
    )(w_in)


def _inproj_body(x_ref, g_ref, w_ref, wuk_ref, kvg_ref, kig_ref,
                 qabs_ref, ckv_ref, qi_ref, kidx_ref, small_ref, mqk_ref, mv_ref, mo_ref):
    h = _rms(x_ref[...], g_ref[...]).astype(BF16)

    def proj(grp):
        return _dot(h, w_ref[:, grp[0]:grp[1]])

    dq = proj(_G_DQ)
    for hd in range(DSA_HEADS):
        qh = dq[:, hd * DSA_HEAD_DIM:(hd + 1) * DSA_HEAD_DIM].astype(BF16)
        qa = _dot(qh, wuk_ref[hd]) * (DSA_HEAD_DIM ** -0.5 * _LOG2E)
        qabs_ref[:, hd * DSA_LATENT:(hd + 1) * DSA_LATENT] = qa.astype(BF16)
    ckv_ref[...] = _rms(proj(_G_CKV), kvg_ref[...]).astype(BF16)
    qi_ref[...] = (proj(_G_QI) * (IDX_DIM ** -0.5)).astype(BF16)
    small = proj(_G_SMALL)
    small_ref[...] = small
    kidx_ref[...] = _rms(small[:, :IDX_DIM], kig_ref[...]).astype(BF16)
    mqk_ref[...] = proj(_G_MQK)
    mv_ref[...] = proj(_G_MV).astype(BF16)
    mo_ref[...] = proj(_G_MO)


def _inproj(x, g, w, wuk, kvg, kig, tm):
    T = x.shape[0]
    row = lambda n: pl.BlockSpec((tm, n), lambda i: (i, 0))
    outs = [(8 * DSA_LATENT, BF16), (DSA_LATENT, BF16), (IDX_HEADS * IDX_DIM, BF16), (IDX_DIM, BF16),
            (_SMALL, F32), (2 * ML_HEADS * ML_QK_DIM, F32), (ML_HEADS * ML_V_DIM, BF16),
            (ML_HEADS * ML_V_DIM, F32)]
    return pl.pallas_call(
        _inproj_body,
        grid=(T // tm,),
        in_specs=[row(D_MODEL), _resident(g.shape), _resident(w.shape), _resident(wuk.shape),
                  _resident(kvg.shape), _resident(kig.shape)],
        out_specs=[row(n) for n, _ in outs],
        out_shape=[jax.ShapeDtypeStruct((T, n), dt) for n, dt in outs],
        compiler_params=_params(),
        name="inproj",
    )(x, g, w, wuk, kvg, kig)


def _sublane_fold(v, op, rows=_SUBLANES, ways=4):
    groups = [v[r * rows:(r + 1) * rows, :] for r in range(v.shape[0] // rows)]
    accs = groups[:ways]
    for r in range(ways, len(groups)):
        accs[r % ways] = op(accs[r % ways], groups[r])
    while len(accs) > 1:
        accs = [op(accs[k], accs[k + 1]) if k + 1 < len(accs) else accs[k] for k in range(0, len(accs), 2)]
    return accs[0]


def _dsa_body(qi_ref, wrow_ref, qabs_ref, kidx_ref, ckv_ref, ckvt_ref, wuv_ref, out_ref,
              key_ref, hi_ref, lga_ref, lgb_ref, m_ref, l_ref, acc_ref, *, tq, tk, topk, nbits_idx):
    i = pl.program_id(0)
    n_kb = ((i + 1) * tq + tk - 1) // tk
    w_rows = wrow_ref[0:IDX_HEADS, :] * (IDX_HEADS ** -0.5)
    q_chunk = (i * tq + lax.broadcasted_iota(I32, (1, tq), 1)) >> _CHUNK_SHIFT

    def key_pos(j):
        return j * tk + lax.broadcasted_iota(I32, (tk, 1), 0)

    def score_block(j, carry):
        kx = kidx_ref[pl.ds(pl.multiple_of(j * tk, tk), tk), :]
        s = jnp.zeros((tk, tq), F32)
        for hd in range(IDX_HEADS):
            d = _dot_nt(kx, qi_ref[:, hd * IDX_DIM:(hd + 1) * IDX_DIM])
            s = s + w_rows[hd:hd + 1, :] * jnp.maximum(d, 0.0)
        bits = lax.bitcast_convert_type(s, I32)
        key = bits ^ ((bits >> 31) & 0x7FFFFFFF)
        key_ref[j] = jnp.where((key_pos(j) >> _CHUNK_SHIFT) <= q_chunk, key, _INT_MIN)
        return carry

    lax.fori_loop(0, n_kb, score_block, 0)

    def count(pred):
        def body(j, acc):
            hit = pred(key_ref[j], key_pos(j)).astype(I32)
            return acc + _sublane_fold(hit, jnp.add)
        acc = lax.fori_loop(0, n_kb, body, jnp.zeros((8, tq), I32))
        return jnp.sum(acc, axis=0, keepdims=True)

    def count16(ref, cand):
        c16 = cand.astype(I16)
        def body(j, acc):
            hit = jnp.where(ref[j] >= c16, jnp.int16(1), jnp.int16(0))
            return acc + _sublane_fold(hit, jnp.add, rows=16)
        acc = lax.fori_loop(0, n_kb, body, jnp.zeros((16, tq), I16))
        return jnp.sum(acc.astype(I32), axis=0, keepdims=True)

    def kth_largest16(ref, kth):
        def bit(b, t):
            cand = t + lax.shift_left(jnp.int32(1), 15 - b)
            return jnp.where(count16(ref, cand) >= kth, cand, t)
        return lax.fori_loop(0, 16, bit, jnp.full((1, tq), _I16_MIN, I32))

    def split_block(j, carry):
        kb = key_ref[j]
        hi_ref[j] = (kb >> 16).astype(I16)
        return carry

    lax.fori_loop(0, n_kb, split_block, 0)
    t_hi = kth_largest16(hi_ref, topk)
    n_above = count16(hi_ref, t_hi + 1)

    def low_block(j, carry):
        kb = key_ref[j]
        lo = ((kb & 0xFFFF) + _I16_MIN).astype(I16)
        hi_ref[j] = jnp.where((kb >> 16) == t_hi, lo, jnp.int16(_I16_MIN))
        return carry

    lax.fori_loop(0, n_kb, low_block, 0)
    t_lo = kth_largest16(hi_ref, topk - n_above)
    t = lax.shift_left(t_hi, 16) + (t_lo - _I16_MIN)
    t = jnp.maximum(t, _INT_MIN + 1)
    n_ge = count(lambda kb, pos: kb >= t)
    n_gt = count(lambda kb, pos: kb > t)
    all_pos = jnp.int32(2 ** nbits_idx - 1)
    n_tie_take = jnp.where(n_ge > topk, topk - n_gt, all_pos)

    def tie_cutoff():
        def pos_bit(b, c):
            cand = c + lax.shift_left(jnp.int32(1), nbits_idx - 1 - b)
            f = count(lambda kb, pos: (kb == t) & (pos < cand))
            return jnp.where(f <= n_tie_take, cand, c)
        return lax.fori_loop(0, nbits_idx, pos_bit, jnp.zeros((1, tq), I32))

    cut = lax.cond(jnp.max(n_ge) > topk, tie_cutoff, lambda: jnp.full((1, tq), all_pos, I32))

    def bias_block(j, carry):
        kb = key_ref[j]
        sel = (kb > t) | ((kb == t) & (key_pos(j) < cut))
        key_ref[j] = lax.bitcast_convert_type(jnp.where(sel, 0.0, _NEG).astype(F32), I32)
        return carry

    lax.fori_loop(0, n_kb, bias_block, 0)

    n_pair = (n_kb + 1) // 2
    neg_bits = lax.bitcast_convert_type(jnp.full((tk, tq), _NEG, F32), I32)

    def masked_block(j, carry):
        key_ref[j] = neg_bits
        return carry

    lax.fori_loop(n_kb, 2 * n_pair, masked_block, 0)

    m_ref[...] = jnp.full(m_ref.shape, _NEG, F32)
    l_ref[...] = jnp.zeros(l_ref.shape, F32)
    acc_ref[...] = jnp.zeros(acc_ref.shape, F32)
    last_blk = ckvt_ref.shape[0] - 1

    def logits(j, lg_buf):
        c_blk = ckv_ref[pl.ds(pl.multiple_of(jnp.minimum(j, last_blk) * tk, tk), tk), :]
        bias = lax.bitcast_convert_type(key_ref[j], F32)
        for hd in range(DSA_HEADS):
            lg_buf[hd] = _dot_nt(c_blk, qabs_ref[:, hd * DSA_LATENT:(hd + 1) * DSA_LATENT]) + bias

    def accumulate(j, lg_buf):
        c_blk_t = ckvt_ref[jnp.minimum(j, last_blk)]
        for hd in range(DSA_HEADS):
            lg = lg_buf[hd]
            m_old = m_ref[hd:hd + 1, :]
            m_new = jnp.maximum(m_old, jnp.max(_sublane_fold(lg, jnp.maximum, ways=1), axis=0, keepdims=True))
            p = jnp.exp2(lg - m_new)
            alpha = jnp.exp2(m_old - m_new)
            l_ref[hd:hd + 1, :] = alpha * l_ref[hd:hd + 1, :] + jnp.sum(_sublane_fold(p, jnp.add, ways=1), axis=0,
                                                                         keepdims=True)
            acc_ref[hd] = alpha * acc_ref[hd] + _dot(c_blk_t, p.astype(BF16))
            m_ref[hd:hd + 1, :] = m_new

    logits(0, lga_ref)

    def attn_pair(mi, carry):
        ja = 2 * mi
        accumulate(ja, lga_ref)
        logits(ja + 1, lgb_ref)
        accumulate(ja + 1, lgb_ref)
        logits(jnp.minimum(ja + 2, 2 * n_pair - 1), lga_ref)
        return carry

    lax.fori_loop(0, n_pair, attn_pair, 0)

    for hd in range(DSA_HEADS):
        o_lat = (acc_ref[hd] / l_ref[hd:hd + 1, :]).T.astype(BF16)
        out_ref[:, hd * DSA_HEAD_DIM:(hd + 1) * DSA_HEAD_DIM] = _dot(o_lat, wuv_ref[hd]).astype(BF16)


def _dsa(qi, wrows, qabs, kidx, ckv, wuv, tq, tk):
    T = qi.shape[0]
    topk = min(TOPK_MAX, T // 4)
    n_kb = T // tk
    ckvt = jnp.transpose(ckv.reshape(n_kb, tk, DSA_LATENT), (0, 2, 1))
    row = lambda n: pl.BlockSpec((tq, n), lambda i: (i, 0))
    body = functools.partial(_dsa_body, tq=tq, tk=tk, topk=topk, nbits_idx=int(T).bit_length())
    return pl.pallas_call(
        body,
        grid=(T // tq,),
        in_specs=[row(qi.shape[1]), pl.BlockSpec((wrows.shape[0], tq), lambda i: (0, i)), row(qabs.shape[1]),
                  _resident(kidx.shape), _resident(ckv.shape), _resident(ckvt.shape), _resident(wuv.shape)],
        out_specs=row(DSA_HEADS * DSA_HEAD_DIM),
        out_shape=jax.ShapeDtypeStruct((T, DSA_HEADS * DSA_HEAD_DIM), BF16),
        scratch_shapes=[pltpu.VMEM((n_kb + n_kb % 2, tk, tq), I32), pltpu.VMEM((n_kb, tk, tq), I16),
                        pltpu.VMEM((DSA_HEADS, tk, tq), F32), pltpu.VMEM((DSA_HEADS, tk, tq), F32),
                        pltpu.VMEM((DSA_HEADS, tq), F32),
                        pltpu.VMEM((DSA_HEADS, tq), F32), pltpu.VMEM((DSA_HEADS, DSA_LATENT, tq), F32)],
        compiler_params=_params(),
        name="dsa",
    )(qi, wrows, qabs, kidx, ckv, ckvt, wuv)


def _log_sigmoid(v):
    return jnp.minimum(v, 0.0) - jnp.log1p(jnp.exp(-jnp.abs(v)))


def _chunk_cumsum(v, axis):
    pos = lax.broadcasted_iota(I32, v.shape, axis) & (CHUNK - 1)
    d = 1
    while d < CHUNK:
        v = v + jnp.where(pos >= d, pltpu.roll(v, d, axis=axis), 0.0)
        d *= 2
    return v


def _mlstm_body(mqk_ref, mv_ref, small_ref, gt_ref, mo_ref, cw_ref, cb_ref, gbc_ref, gbr_ref, ng_ref,
                out_ref, xe_ref, c_ref, n_ref, m_ref, hs_ref, *, rows):
    @pl.when(pl.program_id(0) == 0)
    def _():
        xe_ref[0:8, :] = jnp.zeros((8, xe_ref.shape[1]), F32)
        c_ref[...] = jnp.zeros(c_ref.shape, F32)
        n_ref[...] = jnp.zeros(n_ref.shape, F32)
        m_ref[...] = jnp.zeros(m_ref.shape, F32)

    x = mqk_ref[...]
    xe_ref[8:8 + rows, :] = x
    y = cb_ref[...]
    for j in range(CONV_W - 1):
        y = y + xe_ref[5 + j:5 + j + rows, :] * cw_ref[j:j + 1, :]
    y = y + x * cw_ref[CONV_W - 1:CONV_W, :]
    xe_ref[0:8, :] = x[rows - 8:rows, :]
    qk = y * jax.nn.sigmoid(y)
    nqk = ML_HEADS * ML_QK_DIM
    q_all = (qk[:, :nqk] * (ML_QK_DIM ** -0.5)).astype(BF16)
    k_all = qk[:, nqk:]

    g_col = small_ref[...] + gbc_ref[...]
    g_row = gt_ref[...] + gbr_ref[...]
    b_col = _chunk_cumsum(_log_sigmoid(g_col), 0)
    b_row = _chunk_cumsum(_log_sigmoid(g_row), 1)

    tri = lax.broadcasted_iota(I32, (CHUNK, CHUNK), 1) <= lax.broadcasted_iota(I32, (CHUNK, CHUNK), 0)

    for c in range(rows // CHUNK):
        lo, hi = c * CHUNK, (c + 1) * CHUNK
        for hd in range(ML_HEADS):
            bc = b_col[lo:hi, _S_MF + hd:_S_MF + hd + 1]
            lic = g_col[lo:hi, _S_MI + hd:_S_MI + hd + 1]
            br = b_row[ML_HEADS + hd:ML_HEADS + hd + 1, lo:hi]
            lir = g_row[hd:hd + 1, lo:hi]
            g_tot = bc[CHUNK - 1:CHUNK, :]
            m_prev = m_ref[hd][:, 0:1]

            dmat = jnp.where(tri, bc - br + lir, -jnp.inf)
            inter = bc + m_prev
            m_t = jnp.maximum(inter, jnp.max(dmat, axis=-1, keepdims=True))
            w_intra = jnp.exp(dmat - m_t)
            a_inter = jnp.exp(inter - m_t)

            qh = q_all[lo:hi, hd * ML_QK_DIM:(hd + 1) * ML_QK_DIM]
            kh = k_all[lo:hi, hd * ML_QK_DIM:(hd + 1) * ML_QK_DIM]
            vh = mv_ref[lo:hi, hd * ML_V_DIM:(hd + 1) * ML_V_DIM]
            s_qk = _dot_nt(qh, kh.astype(BF16)) * w_intra
            c_prev = c_ref[hd]
            n_prev = n_ref[hd]
            num = a_inter * _dot(qh, c_prev.astype(BF16)) + _dot(s_qk.astype(BF16), vh)
            den = (a_inter * jnp.sum(qh.astype(F32) * n_prev, axis=-1, keepdims=True)
                   + jnp.sum(s_qk, axis=-1, keepdims=True))
            hs_ref[lo:hi, hd * ML_V_DIM:(hd + 1) * ML_V_DIM] = (
                num / jnp.maximum(jnp.abs(den), jnp.exp(-m_t)))

            m_new = jnp.maximum(g_tot + m_prev, jnp.max(g_tot - br + lir, axis=-1, keepdims=True))
            a_state = jnp.exp(g_tot + m_prev - m_new)
            wk = jnp.exp(g_tot - bc + lic - m_new) * kh
            c_ref[hd] = a_state * c_prev + _dot(wk.T.astype(BF16), vh)
            n_ref[hd] = a_state * n_prev + jnp.sum(wk, axis=0, keepdims=True)
            m_ref[hd] = jnp.broadcast_to(m_new, m_ref.shape[1:])

    for hd in range(ML_HEADS):
        sl = slice(hd * ML_V_DIM, (hd + 1) * ML_V_DIM)
        out_ref[:, sl] = (_rms(hs_ref[:, sl], ng_ref[:, sl]) * jax.nn.sigmoid(mo_ref[:, sl])).astype(BF16)


def _mlstm(mqk, mv, small, gt, mo, cw, cb, gbc, gbr, ng, rows):
    T = mqk.shape[0]
    row = lambda n: pl.BlockSpec((rows, n), lambda i: (i, 0))
    nv = ML_HEADS * ML_V_DIM
    return pl.pallas_call(
        functools.partial(_mlstm_body, rows=rows),
        grid=(T // rows,),
        in_specs=[row(mqk.shape[1]), row(nv), row(_SMALL), pl.BlockSpec((8, rows), lambda i: (1, i)), row(nv),
                  _resident(cw.shape), _resident(cb.shape), _resident(gbc.shape), _resident(gbr.shape),
                  _resident(ng.shape)],
        out_specs=row(nv),
        out_shape=jax.ShapeDtypeStruct((T, nv), BF16),
        scratch_shapes=[pltpu.VMEM((rows + 8, mqk.shape[1]), F32),
                        pltpu.VMEM((ML_HEADS, ML_QK_DIM, ML_V_DIM), F32),
                        pltpu.VMEM((ML_HEADS, 1, ML_QK_DIM), F32),
                        pltpu.VMEM((ML_HEADS, 1, 128), F32),
                        pltpu.VMEM((rows, nv), F32)],
        compiler_params=_params(),
        name="mlstm",
    )(mqk, mv, small, gt, mo, cw, cb, gbc, gbr, ng)


def _memkv_body(mem_ref, g_ref, w_ref, out_ref):
    mn = _rms(mem_ref[...], g_ref[...]).astype(BF16)
    out_ref[...] = _dot(mn, w_ref[...].astype(BF16)).astype(BF16)


def _memkv(mem, g, w, tn):
    M, D = mem.shape
    N = w.shape[1]
    return pl.pallas_call(
        _memkv_body,
        grid=(N // tn,),
        in_specs=[_resident(mem.shape), _resident(g.shape), pl.BlockSpec((D, tn), lambda j: (0, j))],
        out_specs=pl.BlockSpec((M, tn), lambda j: (0, j)),
        out_shape=jax.ShapeDtypeStruct((M, N), BF16),
        compiler_params=_params(),
        name="memkv",
    )(mem, g, w)


def _mixout_body(x_ref, dsa_ref, ml_ref, kv_ref, wo_ref, wq_ref, wc_ref, wr_ref, gx_ref, gf_ref,
                 x2_ref, hf_ref, rl_ref, o_ref):
    nd = dsa_ref.shape[1]
    x1 = x_ref[...] + _dot(dsa_ref[...], wo_ref[0:nd, :]) + _dot(ml_ref[...], wo_ref[nd:, :])
    q = _dot(_rms(x1, gx_ref[...]).astype(BF16), wq_ref[...]).astype(BF16)
    for hd in range(X_HEADS):
        sl = slice(hd * X_HEAD_DIM, (hd + 1) * X_HEAD_DIM)
        lg = _dot_nt(q[:, sl], kv_ref[:, sl]) * (X_HEAD_DIM ** -0.5)
        e = jnp.exp(lg - jnp.max(lg, axis=-1, keepdims=True))
        p = e / jnp.sum(e, axis=-1, keepdims=True)
        v = kv_ref[:, D_MODEL + hd * X_HEAD_DIM:D_MODEL + (hd + 1) * X_HEAD_DIM]
        o_ref[:, sl] = _dot(p.astype(BF16), v).astype(BF16)
    x2 = x1 + _dot(o_ref[...], wc_ref[...])
    x2_ref[...] = x2
    hf = _rms(x2, gf_ref[...]).astype(BF16)
    rl_ref[...] = _dot(hf, wr_ref[...])
    bits = lax.bitcast_convert_type(hf.astype(F32), U32)
    for c in range(_SUBLANES):
        lo = bits[:, (2 * c) * _LANES:(2 * c + 1) * _LANES]
        hi = bits[:, (2 * c + 1) * _LANES:(2 * c + 2) * _LANES]
        hf_ref[pl.ds(c, x2.shape[0], stride=_SUBLANES), :] = (hi & jnp.uint32(0xFFFF0000)) | (lo >> 16)


def _mixout(x, dsa, ml, kv, wo, wq, wc, wr, gx, gf, tm):
    T = x.shape[0]
    row = lambda n: pl.BlockSpec((tm, n), lambda i: (i, 0))
    return pl.pallas_call(
        _mixout_body,
        grid=(T // tm,),
        in_specs=[row(D_MODEL), row(dsa.shape[1]), row(ml.shape[1]), _resident(kv.shape), _resident(wo.shape),
                  _resident(wq.shape), _resident(wc.shape), _resident(wr.shape), _resident(gx.shape),
                  _resident(gf.shape)],
        out_specs=[row(D_MODEL), pl.BlockSpec((tm * _SUBLANES, _LANES), lambda i: (i, 0)), row(wr.shape[1])],
        out_shape=[jax.ShapeDtypeStruct((T, D_MODEL), F32),
                   jax.ShapeDtypeStruct((T * _SUBLANES, _LANES), U32),
                   jax.ShapeDtypeStruct((T, wr.shape[1]), F32)],
        scratch_shapes=[pltpu.VMEM((tm, D_MODEL), BF16)],
        compiler_params=_params(),
        name="mixout",
    )(x, dsa, ml, kv, wo, wq, wc, wr, gx, gf)


def _moe_body(tok_ref, eb_ref, hf_hbm, wg_ref, wu_ref, wd_ref, y_hbm,
              xbuf, ybuf, wgb, wub, wdb, gsem, ysem, *, bm, n_blk):
    e = pl.program_id(0)
    n_valid = eb_ref[N_EXPERTS]
    b_lo = eb_ref[e]
    b_hi = eb_ref[e + 1]

    def tok_words(tok):
        return hf_hbm.at[pl.ds(pl.multiple_of(tok * _SUBLANES, _SUBLANES), _SUBLANES)]

    def gather(blk, s):
        def issue(g, carry):
            for u in range(_SUBLANES):
                r = g * _SUBLANES + u
                pltpu.make_async_copy(tok_words(tok_ref[blk * bm + r]),
                                      xbuf.at[s, pl.ds(pl.multiple_of(r * _SUBLANES, _SUBLANES), _SUBLANES)],
                                      gsem.at[s]).start(priority=1)
            return carry
        lax.fori_loop(0, bm // _SUBLANES, issue, 0)

    def wait_gather(s):
        pltpu.make_async_copy(xbuf.at[s], xbuf.at[s], gsem.at[s]).wait()

    def y_copy(blk, s):
        return pltpu.make_async_copy(ybuf.at[s], y_hbm.at[pl.ds(pl.multiple_of(blk * bm, bm), bm)], ysem.at[s])

    @pl.when(e == 0)
    def _():
        for k in range(_GATHER_AHEAD):
            gather(k, k)

    @pl.when(b_hi > b_lo)
    def _():
        wgb[...] = wg_ref[0].astype(BF16)
        wub[...] = wu_ref[0].astype(BF16)
        wdb[...] = wd_ref[0].astype(BF16)

        def block(b, carry):
            s = b % _GATHER_SLOTS
            wait_gather(s)
            chunks = []
            for c in range(_SUBLANES):
                w = xbuf[s, pl.ds(c, bm, stride=_SUBLANES), :]
                chunks.append(lax.bitcast_convert_type(w << 16, F32).astype(BF16))
                chunks.append(lax.bitcast_convert_type(w & jnp.uint32(0xFFFF0000), F32).astype(BF16))
            xb = jnp.concatenate(chunks, axis=-1)
            gate = _dot(xb, wgb[...])
            a = gate * jax.nn.sigmoid(gate) * _dot(xb, wub[...])
            y = _dot(a.astype(BF16), wdb[...])

            nxt = jnp.minimum(b + _GATHER_AHEAD, n_blk - 1)
            for r in range(bm):
                pltpu.make_async_copy(tok_words(tok_ref[nxt * bm + r]),
                                      xbuf.at[(b + _GATHER_AHEAD) % _GATHER_SLOTS, pl.ds(r * _SUBLANES, _SUBLANES)],
                                      gsem.at[(b + _GATHER_AHEAD) % _GATHER_SLOTS]).start(priority=1)

            @pl.when(b >= 2)
            def _():
                y_copy(b - 2, b % 2).wait()

            ybuf[b % 2] = y
            y_copy(b, b % 2).start()
            return carry

        lax.fori_loop(b_lo, b_hi, block, 0)

    @pl.when(e == pl.num_programs(0) - 1)
    def _():
        for k in range(_GATHER_AHEAD):
            wait_gather((n_valid + k) % _GATHER_SLOTS)

        @pl.when(n_valid >= 2)
        def _():
            y_copy(n_valid - 2, n_valid % 2).wait()
        y_copy(n_valid - 1, (n_valid - 1) % 2).wait()
        ybuf[0] = jnp.zeros(ybuf.shape[1:], F32)

        def zero_block(b, carry):
            cp = y_copy(b, 0)
            cp.start()
            cp.wait()
            return carry

        lax.fori_loop(n_valid, n_blk, zero_block, 0)


def _moe(tok, e_blk, hf, wg, wu, wd, n_blk, bm):
    D = wg.shape[1]
    wspec = lambda shape: pl.BlockSpec((1,) + shape, lambda e, *_: (e, 0, 0))
    grid_spec = pltpu.PrefetchScalarGridSpec(
        num_scalar_prefetch=2,
        grid=(N_EXPERTS,),
        in_specs=[pl.BlockSpec(memory_space=pl.ANY),
                  wspec((D, D_EXPERT)), wspec((D, D_EXPERT)), wspec((D_EXPERT, D))],
        out_specs=pl.BlockSpec(memory_space=pl.ANY),
        scratch_shapes=[pltpu.VMEM((_GATHER_SLOTS, bm * _SUBLANES, _LANES), U32),
                        pltpu.VMEM((2, bm, D), F32),
                        pltpu.VMEM((D, D_EXPERT), BF16), pltpu.VMEM((D, D_EXPERT), BF16),
                        pltpu.VMEM((D_EXPERT, D), BF16),
                        pltpu.SemaphoreType.DMA((_GATHER_SLOTS,)), pltpu.SemaphoreType.DMA((2,))],
    )
    return pl.pallas_call(
        functools.partial(_moe_body, bm=bm, n_blk=n_blk),
        grid_spec=grid_spec,
        out_shape=jax.ShapeDtypeStruct((n_blk * bm, D), F32),
        compiler_params=_params(),
        name="moe",
    )(tok, e_blk, hf, wg, wu, wd)


def _route(rl, b_group, b_router, bm):
    N = rl.shape[0]
    g_logits = rl[:, :N_GROUPS] + b_group
    g_prob = jax.nn.softmax(g_logits, axis=-1)
    g_sel = jnp.argmax(g_logits, axis=-1)
    p_g = jnp.take_along_axis(g_prob, g_sel[:, None], axis=-1)
    e_logits = (rl[:, N_GROUPS:N_GROUPS + N_EXPERTS] + b_router).reshape(N, N_GROUPS, EXP_PER_GROUP)
    e_logits = jnp.take_along_axis(e_logits, g_sel[:, None, None], axis=1)[:, 0]
    top_p, top_local = lax.top_k(jax.nn.softmax(e_logits, axis=-1), TOPK_IN_GROUP)
    gates = p_g * top_p / top_p.sum(-1, keepdims=True)
    expert_id = (g_sel[:, None] * EXP_PER_GROUP + top_local).astype(I32)

    A = N * TOPK_IN_GROUP
    flat_e = expert_id.reshape(A)
    onehot = (flat_e[:, None] == jnp.arange(N_EXPERTS, dtype=I32)[None, :])
    seg = _RANK_SEG if A % _RANK_SEG == 0 else A
    oh = onehot.astype(BF16).reshape(A // seg, seg, N_EXPERTS)
    before = (jnp.arange(seg)[None, :] < jnp.arange(seg)[:, None]).astype(BF16)
    within = jnp.einsum('ij,bjk->bik', before, oh, preferred_element_type=F32)
    seg_tot = jnp.sum(oh.astype(F32), axis=1)
    seg_base = jnp.cumsum(seg_tot, axis=0) - seg_tot
    rank = jnp.sum((within + seg_base[:, None, :]) * oh.astype(F32), axis=-1).reshape(A).astype(I32)
    counts = jnp.sum(seg_tot, axis=0).astype(I32)
    padded = (counts + bm - 1) // bm * bm
    pad_ends = jnp.cumsum(padded)
    pad_starts = pad_ends - padded
    row = jnp.sum(jnp.where(onehot, pad_starts[None, :], 0), axis=1) + rank
    n_blk = -(-A // bm) + N_EXPERTS
    row_tok = jnp.zeros((n_blk * bm,), I32).at[row].set(jnp.arange(A, dtype=I32) // TOPK_IN_GROUP)
    e_blk = jnp.concatenate([pad_starts, pad_ends[-1:]]) // bm
    return row_tok, row, gates, e_blk.astype(I32), n_blk


def _final_body(row_ref, x_ref, gate_ref, y_hbm, g_ref, out_ref, ybuf, sem, *, tm):
    i = pl.program_id(0)
    slot = i % 2

    def gather(tile, s):
        def issue(g, carry):
            for u in range(_SUBLANES):
                for k in range(TOPK_IN_GROUP):
                    src = row_ref[(tile * tm + g * _SUBLANES + u) * TOPK_IN_GROUP + k]
                    pltpu.make_async_copy(y_hbm.at[pl.ds(src, 1)], ybuf.at[s, k, g, pl.ds(u, 1)],
                                          sem.at[s]).start()
            return carry
        lax.fori_loop(0, tm // _SUBLANES, issue, 0)

    @pl.when(i == 0)
    def _():
        gather(0, 0)

    @pl.when(i + 1 < pl.num_programs(0))
    def _():
        gather(i + 1, 1 - slot)

    pltpu.make_async_copy(ybuf.at[slot], ybuf.at[slot], sem.at[slot]).wait()
    acc = x_ref[...]
    for k in range(TOPK_IN_GROUP):
        acc = acc + gate_ref[:, k:k + 1] * ybuf[slot, k].reshape(tm, ybuf.shape[-1])
    out_ref[...] = _rms(acc, g_ref[...])


def _final(row, x2, gates, y_rows, g, tm):
    T, D = x2.shape
    grid_spec = pltpu.PrefetchScalarGridSpec(
        num_scalar_prefetch=1,
        grid=(T // tm,),
        in_specs=[pl.BlockSpec((tm, D), lambda i, *_: (i, 0)),
                  pl.BlockSpec((tm, TOPK_IN_GROUP), lambda i, *_: (i, 0)),
                  pl.BlockSpec(memory_space=pl.ANY),
                  pl.BlockSpec(g.shape, lambda i, *_: (0, 0))],
        out_specs=pl.BlockSpec((tm, D), lambda i, *_: (i, 0)),
        scratch_shapes=[pltpu.VMEM((2, TOPK_IN_GROUP, tm // _SUBLANES, _SUBLANES, D), F32),
                        pltpu.SemaphoreType.DMA((2,))],
    )
    return pl.pallas_call(
        functools.partial(_final_body, tm=tm),
        grid_spec=grid_spec,
        out_shape=jax.ShapeDtypeStruct((T, D), F32),
        compiler_params=_params(),
        name="final",
    )(row, x2, gates, y_rows, g)


def _tile_sizes(T):
    pick = lambda want: want if T % want == 0 else CHUNK
    return dict(inproj=pick(256), dsa_q=pick(256), dsa_k=pick(512), mlstm=pick(256), mixout=pick(256),
                final=pick(256), moe=128)


def _layer(x, mem, norm_mix_g, w_in, kv_norm_g, k_idx_norm_g, w_uk, w_uv, conv_w, conv_b, gate_b, ml_norm_g,
           w_out, norm_x_g, mem_norm_g, w_cq, w_ckv, w_co, norm_ffn_g, w_group, b_group, w_router, b_router,
           w_gate, w_up, w_down, out_g):
    T = x.shape[0]
    ts = _tile_sizes(T)
    r2 = lambda v: v.reshape(1, -1)

    w_r = _wprep(w_in, 256)
    wuk_t = jnp.transpose(w_uk, (1, 2, 0)).astype(BF16)
    wuv_t = jnp.transpose(w_uv, (1, 0, 2)).astype(BF16)

    qabs, ckv, qi, kidx, small, mqk, mv, mo = _inproj(
        x, r2(norm_mix_g), w_r, wuk_t, r2(kv_norm_g), r2(k_idx_norm_g), ts["inproj"])

    gate_rows = jnp.transpose(small[:, _S_WI:_S_MF + ML_HEADS])
    dsa_out = _dsa(qi, gate_rows, qabs, kidx, ckv, wuv_t, ts["dsa_q"], ts["dsa_k"])

    gb_col = jnp.zeros((1, _SMALL), F32).at[0, _S_MI:_S_MI + 2 * ML_HEADS].set(gate_b)
    ml_out = _mlstm(mqk, mv, small, gate_rows, mo, conv_w, r2(conv_b), gb_col, gate_b.reshape(-1, 1),
                    r2(ml_norm_g), ts["mlstm"])

    kv = _memkv(mem, r2(mem_norm_g), w_ckv, 512)
    w_rt = jnp.concatenate([w_group, w_router,
                            jnp.zeros((D_MODEL, 128 - N_GROUPS - N_EXPERTS), w_group.dtype)], axis=1)
    x2, hf, rl = _mixout(x, dsa_out, ml_out, kv, w_out.astype(BF16), w_cq.astype(BF16), w_co.astype(BF16),
                         w_rt.astype(BF16), r2(norm_x_g), r2(norm_ffn_g), ts["mixout"])

    bm = ts["moe"]
    row_tok, row, gates, e_blk, n_blk = _route(rl, b_group, b_router, bm)
    y_rows = _moe(row_tok, e_blk, hf, w_gate, w_up, w_down, n_blk, bm)
    return _final(row, x2, gates, y_rows, r2(out_g), ts["final"])


def kernel(x, mem, norm_mix_g, w_in, kv_norm_g, k_idx_norm_g, w_uk, w_uv, conv_w, conv_b, gate_b, ml_norm_g,
           w_out, norm_x_g, mem_norm_g, w_cq, w_ckv, w_co, norm_ffn_g, w_group, b_group, w_router, b_router,
           w_gate, w_up, w_down, final_norm_g):
    B, T, D = x.shape
    assert B == 1 and D == D_MODEL and norm_mix_g.shape[0] == 1 and T % CHUNK == 0
    out = _layer(x[0], mem[0], norm_mix_g[0], w_in[0], kv_norm_g[0], k_idx_norm_g[0], w_uk[0], w_uv[0],
                 conv_w[0], conv_b[0], gate_b[0], ml_norm_g[0], w_out[0], norm_x_g[0], mem_norm_g[0],
                 w_cq[0], w_ckv[0], w_co[0], norm_ffn_g[0], w_group[0], b_group[0], w_router[0], b_router[0],
                 w_gate[0], w_up[0], w_down[0], final_norm_g)
    return out[None]
```

```python
import functools

import jax
import jax.numpy as jnp
import numpy as np
from jax import lax
from jax.experimental import pallas as pl
from jax.experimental.pallas import tpu as pltpu

F32 = jnp.float32
BF16 = jnp.bfloat16
I32 = jnp.int32
I16 = jnp.int16
U32 = jnp.uint32

EPS = 1e-6
CHUNK = 64
D_MODEL = 2048

DSA_HEADS = 8
DSA_HEAD_DIM = 128
DSA_LATENT = 256
IDX_HEADS = 8
IDX_DIM = 64
TOPK_MAX = 256

ML_HEADS = 4
ML_QK_DIM = 128
ML_V_DIM = 256
CONV_W = 4

X_HEADS = 4
X_HEAD_DIM = D_MODEL // X_HEADS

N_GROUPS = 4
EXP_PER_GROUP = 8
N_EXPERTS = N_GROUPS * EXP_PER_GROUP
TOPK_IN_GROUP = 2
D_EXPERT = 512

_O_DQ = 0
_O_CKV = _O_DQ + DSA_HEADS * DSA_HEAD_DIM
_O_QI = _O_CKV + DSA_LATENT
_O_KI = _O_QI + IDX_HEADS * IDX_DIM
_O_WI = _O_KI + IDX_DIM
_O_MQ = _O_WI + IDX_HEADS
_O_MK = _O_MQ + ML_HEADS * ML_QK_DIM
_O_MV = _O_MK + ML_HEADS * ML_QK_DIM
_O_MI = _O_MV + ML_HEADS * ML_V_DIM
_O_MF = _O_MI + ML_HEADS
_O_MO = _O_MF + ML_HEADS
_O_END = _O_MO + ML_HEADS * ML_V_DIM

_G_DQ = (0, 1024)
_G_CKV = (1024, 1280)
_G_QI = (1280, 1792)
_G_SMALL = (1792, 1920)
_G_MQK = (1920, 2944)
_G_MV = (2944, 3968)
_G_MO = (3968, 4992)
_W_COLS = 4992
_S_WI = IDX_DIM
_S_MI = _S_WI + IDX_HEADS
_S_MF = _S_MI + ML_HEADS
_SMALL = 128

_VMEM_LIMIT = 56 * 1024 * 1024
_INT_MIN = -(2 ** 31)
_I16_MIN = -(2 ** 15)
_CHUNK_SHIFT = CHUNK.bit_length() - 1
_LOG2E = 1.4426950408889634
_SUBLANES = 8
_LANES = 128
_GATHER_SLOTS = 9
_GATHER_AHEAD = _GATHER_SLOTS - 1
_RANK_SEG = 512
_NEG = -1e30


def _rms(v, g):
    return v * lax.rsqrt(jnp.mean(v * v, axis=-1, keepdims=True) + EPS) * g


def _dot(a, b):
    return jnp.dot(a, b, preferred_element_type=F32)


def _dot_nt(a, b):
    return lax.dot_general(a, b, (((1,), (1,)), ((), ())), preferred_element_type=F32)


def _resident(shape):
    nd = len(shape)
    return pl.BlockSpec(shape, lambda *_: (0,) * nd, pipeline_mode=pl.Buffered(1))


def _params(n_axes=1):
    return pltpu.CompilerParams(dimension_semantics=("arbitrary",) * n_axes,
                                vmem_limit_bytes=_VMEM_LIMIT)


def _wprep_body(w_hbm, out_ref, wbuf, sem, *, tk):
    i = pl.program_id(0)
    slot = i % 2

    def rows(blk, s):
        return pltpu.make_async_copy(w_hbm.at[pl.ds(pl.multiple_of(blk * tk, tk), tk)], wbuf.at[s], sem.at[s])

    @pl.when(i == 0)
    def _():
        rows(0, 0).start()

    @pl.when(i + 1 < pl.num_programs(0))
    def _():
        rows(i + 1, 1 - slot).start()

    rows(i, slot).wait()
    w_ref = wbuf.at[slot]
    n_small = IDX_DIM + IDX_HEADS
    out_ref[:, _G_DQ[0]:_G_SMALL[0] + n_small] = w_ref[:, _O_DQ:_O_MQ].astype(BF16)
    out_ref[:, _G_SMALL[0] + n_small:_G_SMALL[0] + n_small + 2 * ML_HEADS] = w_ref[:, _O_MI:_O_MO].astype(BF16)
    out_ref[:, _G_SMALL[0] + n_small + 2 * ML_HEADS:_G_SMALL[1]] = jnp.zeros(
        (out_ref.shape[0], _SMALL - n_small - 2 * ML_HEADS), BF16)
    out_ref[:, _G_MQK[0]:_G_MV[1]] = w_ref[:, _O_MQ:_O_MI].astype(BF16)
    out_ref[:, _G_MO[0]:_G_MO[1]] = w_ref[:, _O_MO:_O_END].astype(BF16)


def _wprep(w_in, tk):
    K = w_in.shape[0]
    return pl.pallas_call(
        functools.partial(_wprep_body, tk=tk),
        grid=(K // tk,),
        in_specs=[pl.BlockSpec(memory_space=pl.ANY)],
        out_specs=pl.BlockSpec((tk, _W_COLS), lambda i: (i, 0)),
        out_shape=jax.ShapeDtypeStruct((K, _W_COLS), BF16),
        scratch_shapes=[pltpu.VMEM((2, tk, w_in.shape[1]), F32), pltpu.SemaphoreType.DMA((2,))],
        compiler_params=_params(),
        name="wprep",
    )(w_in)


def _inproj_body(x_ref, g_ref, w_ref, wuk_ref, kvg_ref, kig_ref,
                 qabs_ref, ckv_ref, qi_ref, kidx_ref, small_ref, mqk_ref, mv_ref, mo_ref):
    h = _rms(x_ref[...], g_ref[...]).astype(BF16)

    def proj(grp):
        return _dot(h, w_ref[:, grp[0]:grp[1]])

    dq = proj(_G_DQ)
    for hd in range(DSA_HEADS):
        qh = dq[:, hd * DSA_HEAD_DIM:(hd + 1) * DSA_HEAD_DIM].astype(BF16)
        qa = _dot(qh, wuk_ref[hd]) * (DSA_HEAD_DIM ** -0.5 * _LOG2E)
        qabs_ref[:, hd * DSA_LATENT:(hd + 1) * DSA_LATENT] = qa.astype(BF16)
    ckv_ref[...] = _rms(proj(_G_CKV), kvg_ref[...]).astype(BF16)
    qi_ref[...] = (proj(_G_QI) * (IDX_DIM ** -0.5)).astype(BF16)
    small = proj(_G_SMALL)
    small_ref[...] = small
    kidx_ref[...] = _rms(small[:, :IDX_DIM], kig_ref[...]).astype(BF16)
    mqk_ref[...] = proj(_G_MQK)
    mv_ref[...] = proj(_G_MV).astype(BF16)
    mo_ref[...] = proj(_G_MO)


def _inproj(x, g, w, wuk, kvg, kig, tm):
    T = x.shape[0]
    row = lambda n: pl.BlockSpec((tm, n), lambda i: (i, 0))
    outs = [(8 * DSA_LATENT, BF16), (DSA_LATENT, BF16), (IDX_HEADS * IDX_DIM, BF16), (IDX_DIM, BF16),
            (_SMALL, F32), (2 * ML_HEADS * ML_QK_DIM, F32), (ML_HEADS * ML_V_DIM, BF16),
            (ML_HEADS * ML_V_DIM, F32)]
    return pl.pallas_call(
        _inproj_body,
        grid=(T // tm,),
        in_specs=[row(D_MODEL), _resident(g.shape), _resident(w.shape), _resident(wuk.shape),
                  _resident(kvg.shape), _resident(kig.shape)],
        out_specs=[row(n) for n, _ in outs],
        out_shape=[jax.ShapeDtypeStruct((T, n), dt) for n, dt in outs],
        compiler_params=_params(),
        name="inproj",
    )(x, g, w, wuk, kvg, kig)


def _sublane_fold(v, op, rows=_SUBLANES, ways=4):
    groups = [v[r * rows:(r + 1) * rows, :] for r in range(v.shape[0] // rows)]
    accs = groups[:ways]
    for r in range(ways, len(groups)):
        accs[r % ways] = op(accs[r % ways], groups[r])
    while len(accs) > 1:
        accs = [op(accs[k], accs[k + 1]) if k + 1 < len(accs) else accs[k] for k in range(0, len(accs), 2)]
    return accs[0]


def _dsa_body(qi_ref, wrow_ref, qabs_ref, kidx_ref, ckv_ref, ckvt_ref, wuv_ref, out_ref,
              key_ref, hi_ref, lga_ref, lgb_ref, m_ref, l_ref, acc_ref, *, tq, tk, topk, nbits_idx):
    i = pl.program_id(0)
    n_kb = ((i + 1) * tq + tk - 1) // tk
    w_rows = wrow_ref[0:IDX_HEADS, :] * (IDX_HEADS ** -0.5)
    q_chunk = (i * tq + lax.broadcasted_iota(I32, (1, tq), 1)) >> _CHUNK_SHIFT

    def key_pos(j):
        return j * tk + lax.broadcasted_iota(I32, (tk, 1), 0)

    def score_block(j, carry):
        kx = kidx_ref[pl.ds(pl.multiple_of(j * tk, tk), tk), :]
        s = jnp.zeros((tk, tq), F32)
        for hd in range(IDX_HEADS):
            d = _dot_nt(kx, qi_ref[:, hd * IDX_DIM:(hd + 1) * IDX_DIM])
            s = s + w_rows[hd:hd + 1, :] * jnp.maximum(d, 0.0)
        bits = lax.bitcast_convert_type(s, I32)
        key = bits ^ ((bits >> 31) & 0x7FFFFFFF)
        key_ref[j] = jnp.where((key_pos(j) >> _CHUNK_SHIFT) <= q_chunk, key, _INT_MIN)
        return carry

    lax.fori_loop(0, n_kb, score_block, 0)

    def count(pred):
        def body(j, acc):
            hit = pred(key_ref[j], key_pos(j)).astype(I32)
            return acc + _sublane_fold(hit, jnp.add)
        acc = lax.fori_loop(0, n_kb, body, jnp.zeros((8, tq), I32))
        return jnp.sum(acc, axis=0, keepdims=True)

    n_pair = (n_kb + 1) // 2

    def count16(ref, cand):
        c16 = cand.astype(I16)
        def body(mi, acc):
            for j in (2 * mi, 2 * mi + 1):
                hit = jnp.where(ref[j] >= c16, jnp.int16(1), jnp.int16(0))
                acc = acc + _sublane_fold(hit, jnp.add, rows=16)
            return acc
        acc = lax.fori_loop(0, n_pair, body, jnp.zeros((16, tq), I16))
        return jnp.sum(acc.astype(I32), axis=0, keepdims=True)

    def kth_largest16(ref, kth):
        def bit(b, t):
            cand = t + lax.shift_left(jnp.int32(1), 15 - b)
            return jnp.where(count16(ref, cand) >= kth, cand, t)
        return lax.fori_loop(0, 16, bit, jnp.full((1, tq), _I16_MIN, I32))

    def split_block(j, carry):
        kb = key_ref[j]
        hi_ref[j] = (kb >> 16).astype(I16)
        return carry

    def lowest_block(j, carry):
        hi_ref[j] = jnp.full((tk, tq), _I16_MIN, I16)
        return carry

    lax.fori_loop(0, n_kb, split_block, 0)
    lax.fori_loop(n_kb, 2 * n_pair, lowest_block, 0)
    t_hi = kth_largest16(hi_ref, topk)
    n_above = count16(hi_ref, t_hi + 1)

    def low_block(j, carry):
        kb = key_ref[j]
        lo = ((kb & 0xFFFF) + _I16_MIN).astype(I16)
        hi_ref[j] = jnp.where((kb >> 16) == t_hi, lo, jnp.int16(_I16_MIN))
        return carry

    lax.fori_loop(0, n_kb, low_block, 0)
    t_lo = kth_largest16(hi_ref, topk - n_above)
    t = lax.shift_left(t_hi, 16) + (t_lo - _I16_MIN)
    t = jnp.maximum(t, _INT_MIN + 1)
    n_ge = count(lambda kb, pos: kb >= t)
    n_gt = count(lambda kb, pos: kb > t)
    all_pos = jnp.int32(2 ** nbits_idx - 1)
    n_tie_take = jnp.where(n_ge > topk, topk - n_gt, all_pos)

    def tie_cutoff():
        def pos_bit(b, c):
            cand = c + lax.shift_left(jnp.int32(1), nbits_idx - 1 - b)
            f = count(lambda kb, pos: (kb == t) & (pos < cand))
            return jnp.where(f <= n_tie_take, cand, c)
        return lax.fori_loop(0, nbits_idx, pos_bit, jnp.zeros((1, tq), I32))

    cut = lax.cond(jnp.max(n_ge) > topk, tie_cutoff, lambda: jnp.full((1, tq), all_pos, I32))

    def bias_block(j, carry):
        kb = key_ref[j]
        sel = (kb > t) | ((kb == t) & (key_pos(j) < cut))
        key_ref[j] = lax.bitcast_convert_type(jnp.where(sel, 0.0, _NEG).astype(F32), I32)
        return carry

    lax.fori_loop(0, n_kb, bias_block, 0)

    neg_bits = lax.bitcast_convert_type(jnp.full((tk, tq), _NEG, F32), I32)

    def masked_block(j, carry):
        key_ref[j] = neg_bits
        return carry

    lax.fori_loop(n_kb, 2 * n_pair, masked_block, 0)

    m_ref[...] = jnp.full(m_ref.shape, _NEG, F32)
    l_ref[...] = jnp.zeros(l_ref.shape, F32)
    acc_ref[...] = jnp.zeros(acc_ref.shape, F32)
    last_blk = ckvt_ref.shape[0] - 1

    def logits(j, lg_buf):
        c_blk = ckv_ref[pl.ds(pl.multiple_of(jnp.minimum(j, last_blk) * tk, tk), tk), :]
        bias = lax.bitcast_convert_type(key_ref[j], F32)
        for hd in range(DSA_HEADS):
            lg_buf[hd] = _dot_nt(c_blk, qabs_ref[:, hd * DSA_LATENT:(hd + 1) * DSA_LATENT]) + bias

    def accumulate(j, lg_buf):
        c_blk_t = ckvt_ref[jnp.minimum(j, last_blk)]
        for hd in range(DSA_HEADS):
            lg = lg_buf[hd]
            m_old = m_ref[hd:hd + 1, :]
            m_new = jnp.maximum(m_old, jnp.max(_sublane_fold(lg, jnp.maximum, ways=1), axis=0, keepdims=True))
            p = jnp.exp2(lg - m_new)
            alpha = jnp.exp2(m_old - m_new)
            l_ref[hd:hd + 1, :] = alpha * l_ref[hd:hd + 1, :] + jnp.sum(_sublane_fold(p, jnp.add, ways=1), axis=0,
                                                                         keepdims=True)
            acc_ref[hd] = alpha * acc_ref[hd] + _dot(c_blk_t, p.astype(BF16))
            m_ref[hd:hd + 1, :] = m_new

    logits(0, lga_ref)

    def attn_pair(mi, carry):
        ja = 2 * mi
        accumulate(ja, lga_ref)
        logits(ja + 1, lgb_ref)
        accumulate(ja + 1, lgb_ref)
        logits(jnp.minimum(ja + 2, 2 * n_pair - 1), lga_ref)
        return carry

    lax.fori_loop(0, n_pair, attn_pair, 0)

    for hd in range(DSA_HEADS):
        o_lat = (acc_ref[hd] / l_ref[hd:hd + 1, :]).T.astype(BF16)
        out_ref[:, hd * DSA_HEAD_DIM:(hd + 1) * DSA_HEAD_DIM] = _dot(o_lat, wuv_ref[hd]).astype(BF16)


def _dsa(qi, wrows, qabs, kidx, ckv, wuv, tq, tk):
    T = qi.shape[0]
    topk = min(TOPK_MAX, T // 4)
    n_kb = T // tk
    ckvt = jnp.transpose(ckv.reshape(n_kb, tk, DSA_LATENT), (0, 2, 1))
    row = lambda n: pl.BlockSpec((tq, n), lambda i: (i, 0))
    body = functools.partial(_dsa_body, tq=tq, tk=tk, topk=topk, nbits_idx=int(T).bit_length())
    return pl.pallas_call(
        body,
        grid=(T // tq,),
        in_specs=[row(qi.shape[1]), pl.BlockSpec((wrows.shape[0], tq), lambda i: (0, i)), row(qabs.shape[1]),
                  _resident(kidx.shape), _resident(ckv.shape), _resident(ckvt.shape), _resident(wuv.shape)],
        out_specs=row(DSA_HEADS * DSA_HEAD_DIM),
        out_shape=jax.ShapeDtypeStruct((T, DSA_HEADS * DSA_HEAD_DIM), BF16),
        scratch_shapes=[pltpu.VMEM((n_kb + n_kb % 2, tk, tq), I32), pltpu.VMEM((n_kb + n_kb % 2, tk, tq), I16),
                        pltpu.VMEM((DSA_HEADS, tk, tq), F32), pltpu.VMEM((DSA_HEADS, tk, tq), F32),
                        pltpu.VMEM((DSA_HEADS, tq), F32),
                        pltpu.VMEM((DSA_HEADS, tq), F32), pltpu.VMEM((DSA_HEADS, DSA_LATENT, tq), F32)],
        compiler_params=_params(),
        name="dsa",
    )(qi, wrows, qabs, kidx, ckv, ckvt, wuv)


def _log_sigmoid(v):
    return jnp.minimum(v, 0.0) - jnp.log1p(jnp.exp(-jnp.abs(v)))


def _chunk_cumsum(v, axis):
    pos = lax.broadcasted_iota(I32, v.shape, axis) & (CHUNK - 1)
    d = 1
    while d < CHUNK:
        v = v + jnp.where(pos >= d, pltpu.roll(v, d, axis=axis), 0.0)
        d *= 2
    return v


def _mlstm_body(mqk_ref, mv_ref, small_ref, gt_ref, mo_ref, cw_ref, cb_ref, gbc_ref, gbr_ref, ng_ref,
                out_ref, xe_ref, c_ref, n_ref, m_ref, hs_ref, *, rows):
    @pl.when(pl.program_id(0) == 0)
    def _():
        xe_ref[0:8, :] = jnp.zeros((8, xe_ref.shape[1]), F32)
        c_ref[...] = jnp.zeros(c_ref.shape, F32)
        n_ref[...] = jnp.zeros(n_ref.shape, F32)
        m_ref[...] = jnp.zeros(m_ref.shape, F32)

    x = mqk_ref[...]
    xe_ref[8:8 + rows, :] = x
    y = cb_ref[...]
    for j in range(CONV_W - 1):
        y = y + xe_ref[5 + j:5 + j + rows, :] * cw_ref[j:j + 1, :]
    y = y + x * cw_ref[CONV_W - 1:CONV_W, :]
    xe_ref[0:8, :] = x[rows - 8:rows, :]
    qk = y * jax.nn.sigmoid(y)
    nqk = ML_HEADS * ML_QK_DIM
    q_all = (qk[:, :nqk] * (ML_QK_DIM ** -0.5)).astype(BF16)
    k_all = qk[:, nqk:]

    g_col = small_ref[...] + gbc_ref[...]
    g_row = gt_ref[...] + gbr_ref[...]
    b_col = _chunk_cumsum(_log_sigmoid(g_col), 0)
    b_row = _chunk_cumsum(_log_sigmoid(g_row), 1)

    tri = lax.broadcasted_iota(I32, (CHUNK, CHUNK), 1) <= lax.broadcasted_iota(I32, (CHUNK, CHUNK), 0)

    for c in range(rows // CHUNK):
        lo, hi = c * CHUNK, (c + 1) * CHUNK
        for hd in range(ML_HEADS):
            bc = b_col[lo:hi, _S_MF + hd:_S_MF + hd + 1]
            lic = g_col[lo:hi, _S_MI + hd:_S_MI + hd + 1]
            br = b_row[ML_HEADS + hd:ML_HEADS + hd + 1, lo:hi]
            lir = g_row[hd:hd + 1, lo:hi]
            g_tot = bc[CHUNK - 1:CHUNK, :]
            m_prev = m_ref[hd][:, 0:1]

            dmat = jnp.where(tri, bc - br + lir, -jnp.inf)
            inter = bc + m_prev
            m_t = jnp.maximum(inter, jnp.max(dmat, axis=-1, keepdims=True))
            w_intra = jnp.exp(dmat - m_t)
            a_inter = jnp.exp(inter - m_t)

            qh = q_all[lo:hi, hd * ML_QK_DIM:(hd + 1) * ML_QK_DIM]
            kh = k_all[lo:hi, hd * ML_QK_DIM:(hd + 1) * ML_QK_DIM]
            vh = mv_ref[lo:hi, hd * ML_V_DIM:(hd + 1) * ML_V_DIM]
            s_qk = _dot_nt(qh, kh.astype(BF16)) * w_intra
            c_prev = c_ref[hd]
            n_prev = n_ref[hd]
            num = a_inter * _dot(qh, c_prev.astype(BF16)) + _dot(s_qk.astype(BF16), vh)
            den = (a_inter * jnp.sum(qh.astype(F32) * n_prev, axis=-1, keepdims=True)
                   + jnp.sum(s_qk, axis=-1, keepdims=True))
            hs_ref[lo:hi, hd * ML_V_DIM:(hd + 1) * ML_V_DIM] = (
                num / jnp.maximum(jnp.abs(den), jnp.exp(-m_t)))

            m_new = jnp.maximum(g_tot + m_prev, jnp.max(g_tot - br + lir, axis=-1, keepdims=True))
            a_state = jnp.exp(g_tot + m_prev - m_new)
            wk = jnp.exp(g_tot - bc + lic - m_new) * kh
            c_ref[hd] = a_state * c_prev + _dot(wk.T.astype(BF16), vh)
            n_ref[hd] = a_state * n_prev + jnp.sum(wk, axis=0, keepdims=True)
            m_ref[hd] = jnp.broadcast_to(m_new, m_ref.shape[1:])

    for hd in range(ML_HEADS):
        sl = slice(hd * ML_V_DIM, (hd + 1) * ML_V_DIM)
        out_ref[:, sl] = (_rms(hs_ref[:, sl], ng_ref[:, sl]) * jax.nn.sigmoid(mo_ref[:, sl])).astype(BF16)


def _mlstm(mqk, mv, small, gt, mo, cw, cb, gbc, gbr, ng, rows):
    T = mqk.shape[0]
    row = lambda n: pl.BlockSpec((rows, n), lambda i: (i, 0))
    nv = ML_HEADS * ML_V_DIM
    return pl.pallas_call(
        functools.partial(_mlstm_body, rows=rows),
        grid=(T // rows,),
        in_specs=[row(mqk.shape[1]), row(nv), row(_SMALL), pl.BlockSpec((8, rows), lambda i: (1, i)), row(nv),
                  _resident(cw.shape), _resident(cb.shape), _resident(gbc.shape), _resident(gbr.shape),
                  _resident(ng.shape)],
        out_specs=row(nv),
        out_shape=jax.ShapeDtypeStruct((T, nv), BF16),
        scratch_shapes=[pltpu.VMEM((rows + 8, mqk.shape[1]), F32),
                        pltpu.VMEM((ML_HEADS, ML_QK_DIM, ML_V_DIM), F32),
                        pltpu.VMEM((ML_HEADS, 1, ML_QK_DIM), F32),
                        pltpu.VMEM((ML_HEADS, 1, 128), F32),
                        pltpu.VMEM((rows, nv), F32)],
        compiler_params=_params(),
        name="mlstm",
    )(mqk, mv, small, gt, mo, cw, cb, gbc, gbr, ng)


def _memkv_body(mem_ref, g_ref, w_ref, out_ref):
    mn = _rms(mem_ref[...], g_ref[...]).astype(BF16)
    out_ref[...] = _dot(mn, w_ref[...].astype(BF16)).astype(BF16)


def _memkv(mem, g, w, tn):
    M, D = mem.shape
    N = w.shape[1]
    return pl.pallas_call(
        _memkv_body,
        grid=(N // tn,),
        in_specs=[_resident(mem.shape), _resident(g.shape), pl.BlockSpec((D, tn), lambda j: (0, j))],
        out_specs=pl.BlockSpec((M, tn), lambda j: (0, j)),
        out_shape=jax.ShapeDtypeStruct((M, N), BF16),
        compiler_params=_params(),
        name="memkv",
    )(mem, g, w)


def _mixout_body(x_ref, dsa_ref, ml_ref, kv_ref, wo_ref, wq_ref, wc_ref, wr_ref, gx_ref, gf_ref,
                 x2_ref, hf_ref, rl_ref, o_ref):
    nd = dsa_ref.shape[1]
    x1 = x_ref[...] + _dot(dsa_ref[...], wo_ref[0:nd, :]) + _dot(ml_ref[...], wo_ref[nd:, :])
    q = _dot(_rms(x1, gx_ref[...]).astype(BF16), wq_ref[...]).astype(BF16)
    for hd in range(X_HEADS):
        sl = slice(hd * X_HEAD_DIM, (hd + 1) * X_HEAD_DIM)
        lg = _dot_nt(q[:, sl], kv_ref[:, sl]) * (X_HEAD_DIM ** -0.5)
        e = jnp.exp(lg - jnp.max(lg, axis=-1, keepdims=True))
        p = e / jnp.sum(e, axis=-1, keepdims=True)
        v = kv_ref[:, D_MODEL + hd * X_HEAD_DIM:D_MODEL + (hd + 1) * X_HEAD_DIM]
        o_ref[:, sl] = _dot(p.astype(BF16), v).astype(BF16)
    x2 = x1 + _dot(o_ref[...], wc_ref[...])
    x2_ref[...] = x2
    hf = _rms(x2, gf_ref[...]).astype(BF16)
    rl_ref[...] = _dot(hf, wr_ref[...])
    bits = lax.bitcast_convert_type(hf.astype(F32), U32)
    for c in range(_SUBLANES):
        lo = bits[:, (2 * c) * _LANES:(2 * c + 1) * _LANES]
        hi = bits[:, (2 * c + 1) * _LANES:(2 * c + 2) * _LANES]
        hf_ref[pl.ds(c, x2.shape[0], stride=_SUBLANES), :] = (hi & jnp.uint32(0xFFFF0000)) | (lo >> 16)


def _mixout(x, dsa, ml, kv, wo, wq, wc, wr, gx, gf, tm):
    T = x.shape[0]
    row = lambda n: pl.BlockSpec((tm, n), lambda i: (i, 0))
    return pl.pallas_call(
        _mixout_body,
        grid=(T // tm,),
        in_specs=[row(D_MODEL), row(dsa.shape[1]), row(ml.shape[1]), _resident(kv.shape), _resident(wo.shape),
                  _resident(wq.shape), _resident(wc.shape), _resident(wr.shape), _resident(gx.shape),
                  _resident(gf.shape)],
        out_specs=[row(D_MODEL), pl.BlockSpec((tm * _SUBLANES, _LANES), lambda i: (i, 0)), row(wr.shape[1])],
        out_shape=[jax.ShapeDtypeStruct((T, D_MODEL), F32),
                   jax.ShapeDtypeStruct((T * _SUBLANES, _LANES), U32),
                   jax.ShapeDtypeStruct((T, wr.shape[1]), F32)],
        scratch_shapes=[pltpu.VMEM((tm, D_MODEL), BF16)],
        compiler_params=_params(),
        name="mixout",
    )(x, dsa, ml, kv, wo, wq, wc, wr, gx, gf)


def _moe_body(tok_ref, eb_ref, hf_hbm, wg_ref, wu_ref, wd_ref, y_hbm,
              xbuf, ybuf, wgb, wub, wdb, gsem, ysem, *, bm, n_blk):
    e = pl.program_id(0)
    n_valid = eb_ref[N_EXPERTS]
    b_lo = eb_ref[e]
    b_hi = eb_ref[e + 1]

    def tok_words(tok):
        return hf_hbm.at[pl.ds(pl.multiple_of(tok * _SUBLANES, _SUBLANES), _SUBLANES)]

    def gather(blk, s):
        def issue(g, carry):
            for u in range(_SUBLANES):
                r = g * _SUBLANES + u
                pltpu.make_async_copy(tok_words(tok_ref[blk * bm + r]),
                                      xbuf.at[s, pl.ds(pl.multiple_of(r * _SUBLANES, _SUBLANES), _SUBLANES)],
                                      gsem.at[s]).start(priority=1)
            return carry
        lax.fori_loop(0, bm // _SUBLANES, issue, 0)

    def wait_gather(s):
        pltpu.make_async_copy(xbuf.at[s], xbuf.at[s], gsem.at[s]).wait()

    def y_copy(blk, s):
        return pltpu.make_async_copy(ybuf.at[s], y_hbm.at[pl.ds(pl.multiple_of(blk * bm, bm), bm)], ysem.at[s])

    @pl.when(e == 0)
    def _():
        for k in range(_GATHER_AHEAD):
            gather(k, k)

    @pl.when(b_hi > b_lo)
    def _():
        wgb[...] = wg_ref[0].astype(BF16)
        wub[...] = wu_ref[0].astype(BF16)
        wdb[...] = wd_ref[0].astype(BF16)

        def block(b, carry):
            s = b % _GATHER_SLOTS
            wait_gather(s)
            chunks = []
            for c in range(_SUBLANES):
                w = xbuf[s, pl.ds(c, bm, stride=_SUBLANES), :]
                chunks.append(lax.bitcast_convert_type(w << 16, F32).astype(BF16))
                chunks.append(lax.bitcast_convert_type(w & jnp.uint32(0xFFFF0000), F32).astype(BF16))
            xb = jnp.concatenate(chunks, axis=-1)
            gate = _dot(xb, wgb[...])
            a = gate * jax.nn.sigmoid(gate) * _dot(xb, wub[...])
            y = _dot(a.astype(BF16), wdb[...])

            nxt = jnp.minimum(b + _GATHER_AHEAD, n_blk - 1)
            for r in range(bm):
                pltpu.make_async_copy(tok_words(tok_ref[nxt * bm + r]),
                                      xbuf.at[(b + _GATHER_AHEAD) % _GATHER_SLOTS, pl.ds(r * _SUBLANES, _SUBLANES)],
                                      gsem.at[(b + _GATHER_AHEAD) % _GATHER_SLOTS]).start(priority=1)

            @pl.when(b >= 2)
            def _():
                y_copy(b - 2, b % 2).wait()

            ybuf[b % 2] = y
            y_copy(b, b % 2).start()
            return carry

        lax.fori_loop(b_lo, b_hi, block, 0)

    @pl.when(e == pl.num_programs(0) - 1)
    def _():
        for k in range(_GATHER_AHEAD):
            wait_gather((n_valid + k) % _GATHER_SLOTS)

        @pl.when(n_valid >= 2)
        def _():
            y_copy(n_valid - 2, n_valid % 2).wait()
        y_copy(n_valid - 1, (n_valid - 1) % 2).wait()
        ybuf[0] = jnp.zeros(ybuf.shape[1:], F32)

        def zero_block(b, carry):
            cp = y_copy(b, 0)
            cp.start()
            cp.wait()
            return carry

        lax.fori_loop(n_valid, n_blk, zero_block, 0)


def _moe(tok, e_blk, hf, wg, wu, wd, n_blk, bm):
    D = wg.shape[1]
    wspec = lambda shape: pl.BlockSpec((1,) + shape, lambda e, *_: (e, 0, 0))
    grid_spec = pltpu.PrefetchScalarGridSpec(
        num_scalar_prefetch=2,
        grid=(N_EXPERTS,),
        in_specs=[pl.BlockSpec(memory_space=pl.ANY),
                  wspec((D, D_EXPERT)), wspec((D, D_EXPERT)), wspec((D_EXPERT, D))],
        out_specs=pl.BlockSpec(memory_space=pl.ANY),
        scratch_shapes=[pltpu.VMEM((_GATHER_SLOTS, bm * _SUBLANES, _LANES), U32),
                        pltpu.VMEM((2, bm, D), F32),
                        pltpu.VMEM((D, D_EXPERT), BF16), pltpu.VMEM((D, D_EXPERT), BF16),
                        pltpu.VMEM((D_EXPERT, D), BF16),
                        pltpu.SemaphoreType.DMA((_GATHER_SLOTS,)), pltpu.SemaphoreType.DMA((2,))],
    )
    return pl.pallas_call(
        functools.partial(_moe_body, bm=bm, n_blk=n_blk),
        grid_spec=grid_spec,
        out_shape=jax.ShapeDtypeStruct((n_blk * bm, D), F32),
        compiler_params=_params(),
        name="moe",
    )(tok, e_blk, hf, wg, wu, wd)


def _route(rl, b_group, b_router, bm):
    N = rl.shape[0]
    g_logits = rl[:, :N_GROUPS] + b_group
    g_prob = jax.nn.softmax(g_logits, axis=-1)
    g_sel = jnp.argmax(g_logits, axis=-1)
    p_g = jnp.take_along_axis(g_prob, g_sel[:, None], axis=-1)
    e_logits = rl[:, N_GROUPS:N_GROUPS + N_EXPERTS] + b_router
    in_group = (jnp.arange(N_EXPERTS, dtype=I32) // EXP_PER_GROUP)[None, :] == g_sel[:, None]
    e_prob = jax.nn.softmax(jnp.where(in_group, e_logits, -jnp.inf), axis=-1)
    top_p, expert_id = lax.top_k(jnp.where(in_group, e_prob, -1.0), TOPK_IN_GROUP)
    gates = p_g * top_p / top_p.sum(-1, keepdims=True)
    expert_id = expert_id.astype(I32)

    A = N * TOPK_IN_GROUP
    flat_e = expert_id.reshape(A)
    onehot = (flat_e[:, None] == jnp.arange(N_EXPERTS, dtype=I32)[None, :])
    seg = _RANK_SEG if A % _RANK_SEG == 0 else A
    oh = onehot.astype(BF16).reshape(A // seg, seg, N_EXPERTS)
    before = (jnp.arange(seg)[None, :] < jnp.arange(seg)[:, None]).astype(BF16)
    within = jnp.einsum('ij,bjk->bik', before, oh, preferred_element_type=F32)
    seg_tot = jnp.sum(oh.astype(F32), axis=1)
    seg_base = jnp.cumsum(seg_tot, axis=0) - seg_tot
    rank = jnp.sum((within + seg_base[:, None, :]) * oh.astype(F32), axis=-1).reshape(A).astype(I32)
    counts = jnp.sum(seg_tot, axis=0).astype(I32)
    padded = (counts + bm - 1) // bm * bm
    pad_ends = jnp.cumsum(padded)
    pad_starts = pad_ends - padded
    row = jnp.sum(jnp.where(onehot, pad_starts[None, :], 0), axis=1) + rank
    n_blk = -(-A // bm) + N_EXPERTS
    row_tok = jnp.zeros((n_blk * bm,), I32).at[row].set(jnp.arange(A, dtype=I32) // TOPK_IN_GROUP)
    e_blk = jnp.concatenate([pad_starts, pad_ends[-1:]]) // bm
    return row_tok, row, gates, e_blk.astype(I32), n_blk


def _final_body(row_ref, x_ref, gate_ref, y_hbm, g_ref, out_ref, ybuf, sem, *, tm):
    i = pl.program_id(0)
    slot = i % 2

    def gather(tile, s):
        def issue(g, carry):
            for u in range(_SUBLANES):
                for k in range(TOPK_IN_GROUP):
                    src = row_ref[(tile * tm + g * _SUBLANES + u) * TOPK_IN_GROUP + k]
                    pltpu.make_async_copy(y_hbm.at[pl.ds(src, 1)], ybuf.at[s, k, g, pl.ds(u, 1)],
                                          sem.at[s]).start()
            return carry
        lax.fori_loop(0, tm // _SUBLANES, issue, 0)

    @pl.when(i == 0)
    def _():
        gather(0, 0)

    @pl.when(i + 1 < pl.num_programs(0))
    def _():
        gather(i + 1, 1 - slot)

    pltpu.make_async_copy(ybuf.at[slot], ybuf.at[slot], sem.at[slot]).wait()
    acc = x_ref[...]
    for k in range(TOPK_IN_GROUP):
        acc = acc + gate_ref[:, k:k + 1] * ybuf[slot, k].reshape(tm, ybuf.shape[-1])
    out_ref[...] = _rms(acc, g_ref[...])


def _final(row, x2, gates, y_rows, g, tm):
    T, D = x2.shape
    grid_spec = pltpu.PrefetchScalarGridSpec(
        num_scalar_prefetch=1,
        grid=(T // tm,),
        in_specs=[pl.BlockSpec((tm, D), lambda i, *_: (i, 0)),
                  pl.BlockSpec((tm, TOPK_IN_GROUP), lambda i, *_: (i, 0)),
                  pl.BlockSpec(memory_space=pl.ANY),
                  pl.BlockSpec(g.shape, lambda i, *_: (0, 0))],
        out_specs=pl.BlockSpec((tm, D), lambda i, *_: (i, 0)),
        scratch_shapes=[pltpu.VMEM((2, TOPK_IN_GROUP, tm // _SUBLANES, _SUBLANES, D), F32),
                        pltpu.SemaphoreType.DMA((2,))],
    )
    return pl.pallas_call(
        functools.partial(_final_body, tm=tm),
        grid_spec=grid_spec,
        out_shape=jax.ShapeDtypeStruct((T, D), F32),
        compiler_params=_params(),
        name="final",
    )(row, x2, gates, y_rows, g)


def _tile_sizes(T):
    pick = lambda want: want if T % want == 0 else CHUNK
    return dict(inproj=pick(256), dsa_q=pick(256), dsa_k=pick(512), mlstm=pick(256), mixout=pick(256),
                final=pick(256), moe=128)


def _layer(x, mem, norm_mix_g, w_in, kv_norm_g, k_idx_norm_g, w_uk, w_uv, conv_w, conv_b, gate_b, ml_norm_g,
           w_out, norm_x_g, mem_norm_g, w_cq, w_ckv, w_co, norm_ffn_g, w_group, b_group, w_router, b_router,
           w_gate, w_up, w_down, out_g):
    T = x.shape[0]
    ts = _tile_sizes(T)
    r2 = lambda v: v.reshape(1, -1)

    w_r = _wprep(w_in, 256)
    wuk_t = jnp.transpose(w_uk, (1, 2, 0)).astype(BF16)
    wuv_t = jnp.transpose(w_uv, (1, 0, 2)).astype(BF16)

    qabs, ckv, qi, kidx, small, mqk, mv, mo = _inproj(
        x, r2(norm_mix_g), w_r, wuk_t, r2(kv_norm_g), r2(k_idx_norm_g), ts["inproj"])

    gate_rows = jnp.transpose(small[:, _S_WI:_S_MF + ML_HEADS])
    dsa_out = _dsa(qi, gate_rows, qabs, kidx, ckv, wuv_t, ts["dsa_q"], ts["dsa_k"])

    gb_col = jnp.zeros((1, _SMALL), F32).at[0, _S_MI:_S_MI + 2 * ML_HEADS].set(gate_b)
    ml_out = _mlstm(mqk, mv, small, gate_rows, mo, conv_w, r2(conv_b), gb_col, gate_b.reshape(-1, 1),
                    r2(ml_norm_g), ts["mlstm"])

    kv = _memkv(mem, r2(mem_norm_g), w_ckv, 512)
    w_rt = jnp.concatenate([w_group, w_router,
                            jnp.zeros((D_MODEL, 128 - N_GROUPS - N_EXPERTS), w_group.dtype)], axis=1)
    x2, hf, rl = _mixout(x, dsa_out, ml_out, kv, w_out.astype(BF16), w_cq.astype(BF16), w_co.astype(BF16),
                         w_rt.astype(BF16), r2(norm_x_g), r2(norm_ffn_g), ts["mixout"])

    bm = ts["moe"]
    row_tok, row, gates, e_blk, n_blk = _route(rl, b_group, b_router, bm)
    y_rows = _moe(row_tok, e_blk, hf, w_gate, w_up, w_down, n_blk, bm)
    return _final(row, x2, gates, y_rows, r2(out_g), ts["final"])


def kernel(x, mem, norm_mix_g, w_in, kv_norm_g, k_idx_norm_g, w_uk, w_uv, conv_w, conv_b, gate_b, ml_norm_g,
           w_out, norm_x_g, mem_norm_g, w_cq, w_ckv, w_co, norm_ffn_g, w_group, b_group, w_router, b_router,
           w_gate, w_up, w_down, final_norm_g):
    B, T, D = x.shape
    assert B == 1 and D == D_MODEL and norm_mix_g.shape[0] == 1 and T % CHUNK == 0
    out = _layer(x[0], mem[0], norm_mix_g[0], w_in[0], kv_norm_g[0], k_idx_norm_g[0], w_uk[0], w_uv[0],
                 conv_w[0], conv_b[0], gate_b[0], ml_norm_g[0], w_out[0], norm_x_g[0], mem_norm_g[0],
                 w_cq[0], w_ckv[0], w_co[0], norm_ffn_g[0], w_group[0], b_group[0], w_router[0], b_router[0],
                 w_gate[0], w_up[0], w_down[0], final_norm_g)
    return out[None]
```

```python
import functools

import jax
import jax.numpy as jnp
import numpy as np
from jax import lax
from jax.experimental import pallas as pl
from jax.experimental.pallas import tpu as pltpu

F32 = jnp.float32
BF16 = jnp.bfloat16
I32 = jnp.int32
I16 = jnp.int16
U32 = jnp.uint32

EPS = 1e-6
CHUNK = 64
D_MODEL = 2048

DSA_HEADS = 8
DSA_HEAD_DIM = 128
DSA_LATENT = 256
IDX_HEADS = 8
IDX_DIM = 64
TOPK_MAX = 256

ML_HEADS = 4
ML_QK_DIM = 128
ML_V_DIM = 256
CONV_W = 4

X_HEADS = 4
X_HEAD_DIM = D_MODEL // X_HEADS

N_GROUPS = 4
EXP_PER_GROUP = 8
N_EXPERTS = N_GROUPS * EXP_PER_GROUP
TOPK_IN_GROUP = 2
D_EXPERT = 512

_O_DQ = 0
_O_CKV = _O_DQ + DSA_HEADS * DSA_HEAD_DIM
_O_QI = _O_CKV + DSA_LATENT
_O_KI = _O_QI + IDX_HEADS * IDX_DIM
_O_WI = _O_KI + IDX_DIM
_O_MQ = _O_WI + IDX_HEADS
_O_MK = _O_MQ + ML_HEADS * ML_QK_DIM
_O_MV = _O_MK + ML_HEADS * ML_QK_DIM
_O_MI = _O_MV + ML_HEADS * ML_V_DIM
_O_MF = _O_MI + ML_HEADS
_O_MO = _O_MF + ML_HEADS
_O_END = _O_MO + ML_HEADS * ML_V_DIM

_G_DQ = (0, 1024)
_G_CKV = (1024, 1280)
_G_QI = (1280, 1792)
_G_SMALL = (1792, 1920)
_G_MQK = (1920, 2944)
_G_MV = (2944, 3968)
_G_MO = (3968, 4992)
_W_COLS = 4992
_S_WI = IDX_DIM
_S_MI = _S_WI + IDX_HEADS
_S_MF = _S_MI + ML_HEADS
_SMALL = 128

_VMEM_LIMIT = 56 * 1024 * 1024
_INT_MIN = -(2 ** 31)
_I16_MIN = -(2 ** 15)
_CHUNK_SHIFT = CHUNK.bit_length() - 1
_LOG2E = 1.4426950408889634
_SUBLANES = 8
_LANES = 128
_GATHER_SLOTS = 9
_GATHER_AHEAD = _GATHER_SLOTS - 1
_RANK_SEG = 512
_NEG = -1e30


def _rms(v, g):
    return v * lax.rsqrt(jnp.mean(v * v, axis=-1, keepdims=True) + EPS) * g


def _dot(a, b):
    return jnp.dot(a, b, preferred_element_type=F32)


def _dot_nt(a, b):
    return lax.dot_general(a, b, (((1,), (1,)), ((), ())), preferred_element_type=F32)


def _resident(shape):
    nd = len(shape)
    return pl.BlockSpec(shape, lambda *_: (0,) * nd, pipeline_mode=pl.Buffered(1))


def _params(n_axes=1):
    return pltpu.CompilerParams(dimension_semantics=("arbitrary",) * n_axes,
                                vmem_limit_bytes=_VMEM_LIMIT)


def _wprep_body(w_hbm, out_ref, wbuf, sem, *, tk):
    i = pl.program_id(0)
    slot = i % 2

    def rows(blk, s):
        return pltpu.make_async_copy(w_hbm.at[pl.ds(pl.multiple_of(blk * tk, tk), tk)], wbuf.at[s], sem.at[s])

    @pl.when(i == 0)
    def _():
        rows(0, 0).start()

    @pl.when(i + 1 < pl.num_programs(0))
    def _():
        rows(i + 1, 1 - slot).start()

    rows(i, slot).wait()
    w_ref = wbuf.at[slot]
    n_small = IDX_DIM + IDX_HEADS
    out_ref[:, _G_DQ[0]:_G_SMALL[0] + n_small] = w_ref[:, _O_DQ:_O_MQ].astype(BF16)
    out_ref[:, _G_SMALL[0] + n_small:_G_SMALL[0] + n_small + 2 * ML_HEADS] = w_ref[:, _O_MI:_O_MO].astype(BF16)
    out_ref[:, _G_SMALL[0] + n_small + 2 * ML_HEADS:_G_SMALL[1]] = jnp.zeros(
        (out_ref.shape[0], _SMALL - n_small - 2 * ML_HEADS), BF16)
    out_ref[:, _G_MQK[0]:_G_MV[1]] = w_ref[:, _O_MQ:_O_MI].astype(BF16)
    out_ref[:, _G_MO[0]:_G_MO[1]] = w_ref[:, _O_MO:_O_END].astype(BF16)


def _wprep(w_in, tk):
    K = w_in.shape[0]
    return pl.pallas_call(
        functools.partial(_wprep_body, tk=tk),
        grid=(K // tk,),
        in_specs=[pl.BlockSpec(memory_space=pl.ANY)],
        out_specs=pl.BlockSpec((tk, _W_COLS), lambda i: (i, 0)),
        out_shape=jax.ShapeDtypeStruct((K, _W_COLS), BF16),
        scratch_shapes=[pltpu.VMEM((2, tk, w_in.shape[1]), F32), pltpu.SemaphoreType.DMA((2,))],
        compiler_params=_params(),
        name="wprep",
    )(w_in)


def _inproj_body(x_ref, g_ref, w_ref, wuk_ref, kvg_ref, kig_ref,
                 qabs_ref, ckv_ref, qi_ref, kidx_ref, small_ref, mqk_ref, mv_ref, mo_ref):
    h = _rms(x_ref[...], g_ref[...]).astype(BF16)

    def proj(grp):
        return _dot(h, w_ref[:, grp[0]:grp[1]])

    dq = proj(_G_DQ)
    for hd in range(DSA_HEADS):
        qh = dq[:, hd * DSA_HEAD_DIM:(hd + 1) * DSA_HEAD_DIM].astype(BF16)
        qa = _dot(qh, wuk_ref[hd]) * (DSA_HEAD_DIM ** -0.5 * _LOG2E)
        qabs_ref[:, hd * DSA_LATENT:(hd + 1) * DSA_LATENT] = qa.astype(BF16)
    ckv_ref[...] = _rms(proj(_G_CKV), kvg_ref[...]).astype(BF16)
    qi_ref[...] = (proj(_G_QI) * (IDX_DIM ** -0.5)).astype(BF16)
    small = proj(_G_SMALL)
    small_ref[...] = small
    kidx_ref[...] = _rms(small[:, :IDX_DIM], kig_ref[...]).astype(BF16)
    mqk_ref[...] = proj(_G_MQK)
    mv_ref[...] = proj(_G_MV).astype(BF16)
    mo_ref[...] = proj(_G_MO)


def _inproj(x, g, w, wuk, kvg, kig, tm):
    T = x.shape[0]
    row = lambda n: pl.BlockSpec((tm, n), lambda i: (i, 0))
    outs = [(8 * DSA_LATENT, BF16), (DSA_LATENT, BF16), (IDX_HEADS * IDX_DIM, BF16), (IDX_DIM, BF16),
            (_SMALL, F32), (2 * ML_HEADS * ML_QK_DIM, F32), (ML_HEADS * ML_V_DIM, BF16),
            (ML_HEADS * ML_V_DIM, F32)]
    return pl.pallas_call(
        _inproj_body,
        grid=(T // tm,),
        in_specs=[row(D_MODEL), _resident(g.shape), _resident(w.shape), _resident(wuk.shape),
                  _resident(kvg.shape), _resident(kig.shape)],
        out_specs=[row(n) for n, _ in outs],
        out_shape=[jax.ShapeDtypeStruct((T, n), dt) for n, dt in outs],
        compiler_params=_params(),
        name="inproj",
    )(x, g, w, wuk, kvg, kig)


def _sublane_fold(v, op, rows=_SUBLANES, ways=4):
    groups = [v[r * rows:(r + 1) * rows, :] for r in range(v.shape[0] // rows)]
    accs = groups[:ways]
    for r in range(ways, len(groups)):
        accs[r % ways] = op(accs[r % ways], groups[r])
    while len(accs) > 1:
        accs = [op(accs[k], accs[k + 1]) if k + 1 < len(accs) else accs[k] for k in range(0, len(accs), 2)]
    return accs[0]


def _bit_transpose32(words):
    a = list(words)
    j, m = 16, 0x0000FFFF
    while j:
        k = 0
        while k < 32:
            t = (a[k] ^ (a[k + j] >> j)) & jnp.uint32(m)
            a[k] = a[k] ^ t
            a[k + j] = a[k + j] ^ (t << j)
            k = (k + j + 1) & ~j
        j >>= 1
        m = (m ^ (m << j)) & 0xFFFFFFFF
    return a


def _dsa_body(qi_ref, wrow_ref, qabs_ref, kidx_ref, ckv_ref, ckvt_ref, wuv_ref, out_ref,
              key_ref, planes_ref, alive_ref, lga_ref, lgb_ref, m_ref, l_ref, acc_ref, *, tq, tk, topk, nbits_idx):
    i = pl.program_id(0)
    n_kb = ((i + 1) * tq + tk - 1) // tk
    w_rows = wrow_ref[0:IDX_HEADS, :] * (IDX_HEADS ** -0.5)
    q_chunk = (i * tq + lax.broadcasted_iota(I32, (1, tq), 1)) >> _CHUNK_SHIFT

    def key_pos(j):
        return j * tk + lax.broadcasted_iota(I32, (tk, 1), 0)

    def score_block(j, carry):
        kx = kidx_ref[pl.ds(pl.multiple_of(j * tk, tk), tk), :]
        s = jnp.zeros((tk, tq), F32)
        for hd in range(IDX_HEADS):
            d = _dot_nt(kx, qi_ref[:, hd * IDX_DIM:(hd + 1) * IDX_DIM])
            s = s + w_rows[hd:hd + 1, :] * jnp.maximum(d, 0.0)
        bits = lax.bitcast_convert_type(s, I32)
        key = bits ^ ((bits >> 31) & 0x7FFFFFFF)
        key_ref[j] = jnp.where((key_pos(j) >> _CHUNK_SHIFT) <= q_chunk, key, _INT_MIN)
        return carry

    lax.fori_loop(0, n_kb, score_block, 0)

    def count(pred):
        def body(j, acc):
            hit = pred(key_ref[j], key_pos(j)).astype(I32)
            return acc + _sublane_fold(hit, jnp.add)
        acc = lax.fori_loop(0, n_kb, body, jnp.zeros((8, tq), I32))
        return jnp.sum(acc, axis=0, keepdims=True)

    n_pair = (n_kb + 1) // 2
    plane_rows = tk // 32

    @pl.when(i == 0)
    def _():
        planes_ref[...] = jnp.zeros(planes_ref.shape, U32)

    def slice_block(j, carry):
        u = lax.bitcast_convert_type(key_ref[j], U32) ^ jnp.uint32(0x80000000)
        for h in range(plane_rows // _SUBLANES):
            base = h * 32 * _SUBLANES
            planes = _bit_transpose32([u[base + _SUBLANES * v:base + _SUBLANES * (v + 1), :] for v in range(32)])
            row0 = pl.multiple_of(j * plane_rows + h * _SUBLANES, _SUBLANES)
            for b in range(32):
                planes_ref[b, pl.ds(row0, _SUBLANES), :] = planes[31 - b]
        return carry

    lax.fori_loop(0, n_kb, slice_block, 0)
    word_row = lax.broadcasted_iota(I32, (alive_ref.shape[0], 1), 0)
    alive_ref[...] = jnp.where(word_row < n_kb * plane_rows, jnp.full(alive_ref.shape, 0xFFFFFFFF, U32),
                               jnp.uint32(0))

    def select_bit(it, carry):
        need, t_u = carry
        b = 31 - it
        alive = alive_ref[...]
        ones = alive & planes_ref[b]
        n_ones = jnp.sum(_sublane_fold(lax.population_count(ones).astype(I32), jnp.add), axis=0, keepdims=True)
        take = n_ones >= need
        alive_ref[...] = jnp.where(take, ones, alive ^ ones)
        bit = lax.shift_left(jnp.uint32(1), b.astype(U32))
        return jnp.where(take, need, need - n_ones), jnp.where(take, t_u | bit, t_u)

    need, t_u = lax.fori_loop(0, 32, select_bit, (jnp.full((1, tq), topk, I32), jnp.zeros((1, tq), U32)))
    n_equal = jnp.sum(_sublane_fold(lax.population_count(alive_ref[...]).astype(I32), jnp.add), axis=0,
                      keepdims=True)
    short = t_u == 0
    t = jnp.maximum(lax.bitcast_convert_type(t_u ^ jnp.uint32(0x80000000), I32), _INT_MIN + 1)
    all_pos = jnp.int32(2 ** nbits_idx - 1)
    has_ties = (n_equal > need) & jnp.logical_not(short)
    n_tie_take = jnp.where(has_ties, need, all_pos)

    def tie_cutoff():
        def pos_bit(b, c):
            cand = c + lax.shift_left(jnp.int32(1), nbits_idx - 1 - b)
            f = count(lambda kb, pos: (kb == t) & (pos < cand))
            return jnp.where(f <= n_tie_take, cand, c)
        return lax.fori_loop(0, nbits_idx, pos_bit, jnp.zeros((1, tq), I32))

    cut = lax.cond(jnp.max(has_ties.astype(I32)) > 0, tie_cutoff, lambda: jnp.full((1, tq), all_pos, I32))

    def bias_block(j, carry):
        kb = key_ref[j]
        sel = (kb > t) | ((kb == t) & (key_pos(j) < cut))
        key_ref[j] = lax.bitcast_convert_type(jnp.where(sel, 0.0, _NEG).astype(F32), I32)
        return carry

    lax.fori_loop(0, n_kb, bias_block, 0)

    neg_bits = lax.bitcast_convert_type(jnp.full((tk, tq), _NEG, F32), I32)

    def masked_block(j, carry):
        key_ref[j] = neg_bits
        return carry

    lax.fori_loop(n_kb, 2 * n_pair, masked_block, 0)

    m_ref[...] = jnp.full(m_ref.shape, _NEG, F32)
    l_ref[...] = jnp.zeros(l_ref.shape, F32)
    acc_ref[...] = jnp.zeros(acc_ref.shape, F32)
    last_blk = ckvt_ref.shape[0] - 1

    def logits(j, lg_buf):
        c_blk = ckv_ref[pl.ds(pl.multiple_of(jnp.minimum(j, last_blk) * tk, tk), tk), :]
        bias = lax.bitcast_convert_type(key_ref[j], F32)
        for hd in range(DSA_HEADS):
            lg_buf[hd] = _dot_nt(c_blk, qabs_ref[:, hd * DSA_LATENT:(hd + 1) * DSA_LATENT]) + bias

    def accumulate(j, lg_buf):
        c_blk_t = ckvt_ref[jnp.minimum(j, last_blk)]
        for hd in range(DSA_HEADS):
            lg = lg_buf[hd]
            m_old = m_ref[hd:hd + 1, :]
            m_new = jnp.maximum(m_old, jnp.max(_sublane_fold(lg, jnp.maximum, ways=1), axis=0, keepdims=True))
            p = jnp.exp2(lg - m_new)
            alpha = jnp.exp2(m_old - m_new)
            l_ref[hd:hd + 1, :] = alpha * l_ref[hd:hd + 1, :] + jnp.sum(_sublane_fold(p, jnp.add, ways=1), axis=0,
                                                                         keepdims=True)
            acc_ref[hd] = alpha * acc_ref[hd] + _dot(c_blk_t, p.astype(BF16))
            m_ref[hd:hd + 1, :] = m_new

    logits(0, lga_ref)

    def attn_pair(mi, carry):
        ja = 2 * mi
        accumulate(ja, lga_ref)
        logits(ja + 1, lgb_ref)
        accumulate(ja + 1, lgb_ref)
        logits(jnp.minimum(ja + 2, 2 * n_pair - 1), lga_ref)
        return carry

    lax.fori_loop(0, n_pair, attn_pair, 0)

    for hd in range(DSA_HEADS):
        o_lat = (acc_ref[hd] / l_ref[hd:hd + 1, :]).T.astype(BF16)
        out_ref[:, hd * DSA_HEAD_DIM:(hd + 1) * DSA_HEAD_DIM] = _dot(o_lat, wuv_ref[hd]).astype(BF16)


def _dsa(qi, wrows, qabs, kidx, ckv, wuv, tq, tk):
    T = qi.shape[0]
    topk = min(TOPK_MAX, T // 4)
    n_kb = T // tk
    ckvt = jnp.transpose(ckv.reshape(n_kb, tk, DSA_LATENT), (0, 2, 1))
    row = lambda n: pl.BlockSpec((tq, n), lambda i: (i, 0))
    body = functools.partial(_dsa_body, tq=tq, tk=tk, topk=topk, nbits_idx=int(T).bit_length())
    return pl.pallas_call(
        body,
        grid=(T // tq,),
        in_specs=[row(qi.shape[1]), pl.BlockSpec((wrows.shape[0], tq), lambda i: (0, i)), row(qabs.shape[1]),
                  _resident(kidx.shape), _resident(ckv.shape), _resident(ckvt.shape), _resident(wuv.shape)],
        out_specs=row(DSA_HEADS * DSA_HEAD_DIM),
        out_shape=jax.ShapeDtypeStruct((T, DSA_HEADS * DSA_HEAD_DIM), BF16),
        scratch_shapes=[pltpu.VMEM((n_kb + n_kb % 2, tk, tq), I32),
                        pltpu.VMEM((32, T // 32, tq), U32), pltpu.VMEM((T // 32, tq), U32),
                        pltpu.VMEM((DSA_HEADS, tk, tq), F32), pltpu.VMEM((DSA_HEADS, tk, tq), F32),
                        pltpu.VMEM((DSA_HEADS, tq), F32),
                        pltpu.VMEM((DSA_HEADS, tq), F32), pltpu.VMEM((DSA_HEADS, DSA_LATENT, tq), F32)],
        compiler_params=_params(),
        name="dsa",
    )(qi, wrows, qabs, kidx, ckv, ckvt, wuv)


def _log_sigmoid(v):
    return jnp.minimum(v, 0.0) - jnp.log1p(jnp.exp(-jnp.abs(v)))


def _chunk_cumsum(v, axis):
    pos = lax.broadcasted_iota(I32, v.shape, axis) & (CHUNK - 1)
    d = 1
    while d < CHUNK:
        v = v + jnp.where(pos >= d, pltpu.roll(v, d, axis=axis), 0.0)
        d *= 2
    return v


def _mlstm_body(mqk_ref, mv_ref, small_ref, gt_ref, mo_ref, cw_ref, cb_ref, gbc_ref, gbr_ref, ng_ref,
                out_ref, xe_ref, c_ref, n_ref, m_ref, hs_ref, *, rows):
    @pl.when(pl.program_id(0) == 0)
    def _():
        xe_ref[0:8, :] = jnp.zeros((8, xe_ref.shape[1]), F32)
        c_ref[...] = jnp.zeros(c_ref.shape, F32)
        n_ref[...] = jnp.zeros(n_ref.shape, F32)
        m_ref[...] = jnp.zeros(m_ref.shape, F32)

    x = mqk_ref[...]
    xe_ref[8:8 + rows, :] = x
    y = cb_ref[...]
    for j in range(CONV_W - 1):
        y = y + xe_ref[5 + j:5 + j + rows, :] * cw_ref[j:j + 1, :]
    y = y + x * cw_ref[CONV_W - 1:CONV_W, :]
    xe_ref[0:8, :] = x[rows - 8:rows, :]
    qk = y * jax.nn.sigmoid(y)
    nqk = ML_HEADS * ML_QK_DIM
    q_all = (qk[:, :nqk] * (ML_QK_DIM ** -0.5)).astype(BF16)
    k_all = qk[:, nqk:]

    g_col = small_ref[...] + gbc_ref[...]
    g_row = gt_ref[...] + gbr_ref[...]
    b_col = _chunk_cumsum(_log_sigmoid(g_col), 0)
    b_row = _chunk_cumsum(_log_sigmoid(g_row), 1)

    tri = lax.broadcasted_iota(I32, (CHUNK, CHUNK), 1) <= lax.broadcasted_iota(I32, (CHUNK, CHUNK), 0)

    for c in range(rows // CHUNK):
        lo, hi = c * CHUNK, (c + 1) * CHUNK
        for hd in range(ML_HEADS):
            bc = b_col[lo:hi, _S_MF + hd:_S_MF + hd + 1]
            lic = g_col[lo:hi, _S_MI + hd:_S_MI + hd + 1]
            br = b_row[ML_HEADS + hd:ML_HEADS + hd + 1, lo:hi]
            lir = g_row[hd:hd + 1, lo:hi]
            g_tot = bc[CHUNK - 1:CHUNK, :]
            m_prev = m_ref[hd][:, 0:1]

            dmat = jnp.where(tri, bc - br + lir, -jnp.inf)
            inter = bc + m_prev
            m_t = jnp.maximum(inter, jnp.max(dmat, axis=-1, keepdims=True))
            w_intra = jnp.exp(dmat - m_t)
            a_inter = jnp.exp(inter - m_t)

            qh = q_all[lo:hi, hd * ML_QK_DIM:(hd + 1) * ML_QK_DIM]
            kh = k_all[lo:hi, hd * ML_QK_DIM:(hd + 1) * ML_QK_DIM]
            vh = mv_ref[lo:hi, hd * ML_V_DIM:(hd + 1) * ML_V_DIM]
            s_qk = _dot_nt(qh, kh.astype(BF16)) * w_intra
            c_prev = c_ref[hd]
            n_prev = n_ref[hd]
            num = a_inter * _dot(qh, c_prev.astype(BF16)) + _dot(s_qk.astype(BF16), vh)
            den = (a_inter * jnp.sum(qh.astype(F32) * n_prev, axis=-1, keepdims=True)
                   + jnp.sum(s_qk, axis=-1, keepdims=True))
            hs_ref[lo:hi, hd * ML_V_DIM:(hd + 1) * ML_V_DIM] = (
                num / jnp.maximum(jnp.abs(den), jnp.exp(-m_t)))

            m_new = jnp.maximum(g_tot + m_prev, jnp.max(g_tot - br + lir, axis=-1, keepdims=True))
            a_state = jnp.exp(g_tot + m_prev - m_new)
            wk = jnp.exp(g_tot - bc + lic - m_new) * kh
            c_ref[hd] = a_state * c_prev + _dot(wk.T.astype(BF16), vh)
            n_ref[hd] = a_state * n_prev + jnp.sum(wk, axis=0, keepdims=True)
            m_ref[hd] = jnp.broadcast_to(m_new, m_ref.shape[1:])

    for hd in range(ML_HEADS):
        sl = slice(hd * ML_V_DIM, (hd + 1) * ML_V_DIM)
        out_ref[:, sl] = (_rms(hs_ref[:, sl], ng_ref[:, sl]) * jax.nn.sigmoid(mo_ref[:, sl])).astype(BF16)


def _mlstm(mqk, mv, small, gt, mo, cw, cb, gbc, gbr, ng, rows):
    T = mqk.shape[0]
    row = lambda n: pl.BlockSpec((rows, n), lambda i: (i, 0))
    nv = ML_HEADS * ML_V_DIM
    return pl.pallas_call(
        functools.partial(_mlstm_body, rows=rows),
        grid=(T // rows,),
        in_specs=[row(mqk.shape[1]), row(nv), row(_SMALL), pl.BlockSpec((8, rows), lambda i: (1, i)), row(nv),
                  _resident(cw.shape), _resident(cb.shape), _resident(gbc.shape), _resident(gbr.shape),
                  _resident(ng.shape)],
        out_specs=row(nv),
        out_shape=jax.ShapeDtypeStruct((T, nv), BF16),
        scratch_shapes=[pltpu.VMEM((rows + 8, mqk.shape[1]), F32),
                        pltpu.VMEM((ML_HEADS, ML_QK_DIM, ML_V_DIM), F32),
                        pltpu.VMEM((ML_HEADS, 1, ML_QK_DIM), F32),
                        pltpu.VMEM((ML_HEADS, 1, 128), F32),
                        pltpu.VMEM((rows, nv), F32)],
        compiler_params=_params(),
        name="mlstm",
    )(mqk, mv, small, gt, mo, cw, cb, gbc, gbr, ng)


def _memkv_body(mem_ref, g_ref, w_ref, out_ref):
    mn = _rms(mem_ref[...], g_ref[...]).astype(BF16)
    out_ref[...] = _dot(mn, w_ref[...].astype(BF16)).astype(BF16)


def _memkv(mem, g, w, tn):
    M, D = mem.shape
    N = w.shape[1]
    return pl.pallas_call(
        _memkv_body,
        grid=(N // tn,),
        in_specs=[_resident(mem.shape), _resident(g.shape), pl.BlockSpec((D, tn), lambda j: (0, j))],
        out_specs=pl.BlockSpec((M, tn), lambda j: (0, j)),
        out_shape=jax.ShapeDtypeStruct((M, N), BF16),
        compiler_params=_params(),
        name="memkv",
    )(mem, g, w)


def _mixout_body(x_ref, dsa_ref, ml_ref, kv_ref, wo_ref, wq_ref, wc_ref, wr_ref, gx_ref, gf_ref,
                 x2_ref, hf_ref, rl_ref, o_ref):
    nd = dsa_ref.shape[1]
    x1 = x_ref[...] + _dot(dsa_ref[...], wo_ref[0:nd, :]) + _dot(ml_ref[...], wo_ref[nd:, :])
    q = _dot(_rms(x1, gx_ref[...]).astype(BF16), wq_ref[...]).astype(BF16)
    for hd in range(X_HEADS):
        sl = slice(hd * X_HEAD_DIM, (hd + 1) * X_HEAD_DIM)
        lg = _dot_nt(q[:, sl], kv_ref[:, sl]) * (X_HEAD_DIM ** -0.5)
        e = jnp.exp(lg - jnp.max(lg, axis=-1, keepdims=True))
        p = e / jnp.sum(e, axis=-1, keepdims=True)
        v = kv_ref[:, D_MODEL + hd * X_HEAD_DIM:D_MODEL + (hd + 1) * X_HEAD_DIM]
        o_ref[:, sl] = _dot(p.astype(BF16), v).astype(BF16)
    x2 = x1 + _dot(o_ref[...], wc_ref[...])
    x2_ref[...] = x2
    hf = _rms(x2, gf_ref[...]).astype(BF16)
    rl_ref[...] = _dot(hf, wr_ref[...])
    bits = lax.bitcast_convert_type(hf.astype(F32), U32)
    for c in range(_SUBLANES):
        lo = bits[:, (2 * c) * _LANES:(2 * c + 1) * _LANES]
        hi = bits[:, (2 * c + 1) * _LANES:(2 * c + 2) * _LANES]
        hf_ref[pl.ds(c, x2.shape[0], stride=_SUBLANES), :] = (hi & jnp.uint32(0xFFFF0000)) | (lo >> 16)


def _mixout(x, dsa, ml, kv, wo, wq, wc, wr, gx, gf, tm):
    T = x.shape[0]
    row = lambda n: pl.BlockSpec((tm, n), lambda i: (i, 0))
    return pl.pallas_call(
        _mixout_body,
        grid=(T // tm,),
        in_specs=[row(D_MODEL), row(dsa.shape[1]), row(ml.shape[1]), _resident(kv.shape), _resident(wo.shape),
                  _resident(wq.shape), _resident(wc.shape), _resident(wr.shape), _resident(gx.shape),
                  _resident(gf.shape)],
        out_specs=[row(D_MODEL), pl.BlockSpec((tm * _SUBLANES, _LANES), lambda i: (i, 0)), row(wr.shape[1])],
        out_shape=[jax.ShapeDtypeStruct((T, D_MODEL), F32),
                   jax.ShapeDtypeStruct((T * _SUBLANES, _LANES), U32),
                   jax.ShapeDtypeStruct((T, wr.shape[1]), F32)],
        scratch_shapes=[pltpu.VMEM((tm, D_MODEL), BF16)],
        compiler_params=_params(),
        name="mixout",
    )(x, dsa, ml, kv, wo, wq, wc, wr, gx, gf)


def _moe_body(tok_ref, eb_ref, hf_hbm, wg_ref, wu_ref, wd_ref, y_hbm,
              xbuf, ybuf, wgb, wub, wdb, gsem, ysem, *, bm, n_blk):
    e = pl.program_id(0)
    n_valid = eb_ref[N_EXPERTS]
    b_lo = eb_ref[e]
    b_hi = eb_ref[e + 1]

    def tok_words(tok):
        return hf_hbm.at[pl.ds(pl.multiple_of(tok * _SUBLANES, _SUBLANES), _SUBLANES)]

    def gather(blk, s):
        def issue(g, carry):
            for u in range(_SUBLANES):
                r = g * _SUBLANES + u
                pltpu.make_async_copy(tok_words(tok_ref[blk * bm + r]),
                                      xbuf.at[s, pl.ds(pl.multiple_of(r * _SUBLANES, _SUBLANES), _SUBLANES)],
                                      gsem.at[s]).start(priority=1)
            return carry
        lax.fori_loop(0, bm // _SUBLANES, issue, 0)

    def wait_gather(s):
        pltpu.make_async_copy(xbuf.at[s], xbuf.at[s], gsem.at[s]).wait()

    def y_copy(blk, s):
        return pltpu.make_async_copy(ybuf.at[s], y_hbm.at[pl.ds(pl.multiple_of(blk * bm, bm), bm)], ysem.at[s])

    @pl.when(e == 0)
    def _():
        for k in range(_GATHER_AHEAD):
            gather(k, k)

    @pl.when(b_hi > b_lo)
    def _():
        wgb[...] = wg_ref[0].astype(BF16)
        wub[...] = wu_ref[0].astype(BF16)
        wdb[...] = wd_ref[0].astype(BF16)

        def block(b, carry):
            s = b % _GATHER_SLOTS
            wait_gather(s)
            chunks = []
            for c in range(_SUBLANES):
                w = xbuf[s, pl.ds(c, bm, stride=_SUBLANES), :]
                chunks.append(lax.bitcast_convert_type(w << 16, F32).astype(BF16))
                chunks.append(lax.bitcast_convert_type(w & jnp.uint32(0xFFFF0000), F32).astype(BF16))
            xb = jnp.concatenate(chunks, axis=-1)
            gate = _dot(xb, wgb[...])
            a = gate * jax.nn.sigmoid(gate) * _dot(xb, wub[...])
            y = _dot(a.astype(BF16), wdb[...])

            nxt = jnp.minimum(b + _GATHER_AHEAD, n_blk - 1)
            for r in range(bm):
                pltpu.make_async_copy(tok_words(tok_ref[nxt * bm + r]),
                                      xbuf.at[(b + _GATHER_AHEAD) % _GATHER_SLOTS, pl.ds(r * _SUBLANES, _SUBLANES)],
                                      gsem.at[(b + _GATHER_AHEAD) % _GATHER_SLOTS]).start(priority=1)

            @pl.when(b >= 2)
            def _():
                y_copy(b - 2, b % 2).wait()

            ybuf[b % 2] = y
            y_copy(b, b % 2).start()
            return carry

        lax.fori_loop(b_lo, b_hi, block, 0)

    @pl.when(e == pl.num_programs(0) - 1)
    def _():
        for k in range(_GATHER_AHEAD):
            wait_gather((n_valid + k) % _GATHER_SLOTS)

        @pl.when(n_valid >= 2)
        def _():
            y_copy(n_valid - 2, n_valid % 2).wait()
        y_copy(n_valid - 1, (n_valid - 1) % 2).wait()
        ybuf[0] = jnp.zeros(ybuf.shape[1:], F32)

        def zero_block(b, carry):
            cp = y_copy(b, 0)
            cp.start()
            cp.wait()
            return carry

        lax.fori_loop(n_valid, n_blk, zero_block, 0)


def _moe(tok, e_blk, hf, wg, wu, wd, n_blk, bm):
    D = wg.shape[1]
    wspec = lambda shape: pl.BlockSpec((1,) + shape, lambda e, *_: (e, 0, 0))
    grid_spec = pltpu.PrefetchScalarGridSpec(
        num_scalar_prefetch=2,
        grid=(N_EXPERTS,),
        in_specs=[pl.BlockSpec(memory_space=pl.ANY),
                  wspec((D, D_EXPERT)), wspec((D, D_EXPERT)), wspec((D_EXPERT, D))],
        out_specs=pl.BlockSpec(memory_space=pl.ANY),
        scratch_shapes=[pltpu.VMEM((_GATHER_SLOTS, bm * _SUBLANES, _LANES), U32),
                        pltpu.VMEM((2, bm, D), F32),
                        pltpu.VMEM((D, D_EXPERT), BF16), pltpu.VMEM((D, D_EXPERT), BF16),
                        pltpu.VMEM((D_EXPERT, D), BF16),
                        pltpu.SemaphoreType.DMA((_GATHER_SLOTS,)), pltpu.SemaphoreType.DMA((2,))],
    )
    return pl.pallas_call(
        functools.partial(_moe_body, bm=bm, n_blk=n_blk),
        grid_spec=grid_spec,
        out_shape=jax.ShapeDtypeStruct((n_blk * bm, D), F32),
        compiler_params=_params(),
        name="moe",
    )(tok, e_blk, hf, wg, wu, wd)


def _route(rl, b_group, b_router, bm):
    N = rl.shape[0]
    g_logits = rl[:, :N_GROUPS] + b_group
    g_prob = jax.nn.softmax(g_logits, axis=-1)
    g_sel = jnp.argmax(g_logits, axis=-1)
    p_g = jnp.take_along_axis(g_prob, g_sel[:, None], axis=-1)
    e_logits = rl[:, N_GROUPS:N_GROUPS + N_EXPERTS] + b_router
    in_group = (jnp.arange(N_EXPERTS, dtype=I32) // EXP_PER_GROUP)[None, :] == g_sel[:, None]
    e_prob = jax.nn.softmax(jnp.where(in_group, e_logits, -jnp.inf), axis=-1)
    top_p, expert_id = lax.top_k(jnp.where(in_group, e_prob, -1.0), TOPK_IN_GROUP)
    gates = p_g * top_p / top_p.sum(-1, keepdims=True)
    expert_id = expert_id.astype(I32)

    A = N * TOPK_IN_GROUP
    flat_e = expert_id.reshape(A)
    onehot = (flat_e[:, None] == jnp.arange(N_EXPERTS, dtype=I32)[None, :])
    seg = _RANK_SEG if A % _RANK_SEG == 0 else A
    oh = onehot.astype(BF16).reshape(A // seg, seg, N_EXPERTS)
    before = (jnp.arange(seg)[None, :] < jnp.arange(seg)[:, None]).astype(BF16)
    within = jnp.einsum('ij,bjk->bik', before, oh, preferred_element_type=F32)
    seg_tot = jnp.sum(oh.astype(F32), axis=1)
    seg_base = jnp.cumsum(seg_tot, axis=0) - seg_tot
    rank = jnp.sum((within + seg_base[:, None, :]) * oh.astype(F32), axis=-1).reshape(A).astype(I32)
    counts = jnp.sum(seg_tot, axis=0).astype(I32)
    padded = (counts + bm - 1) // bm * bm
    pad_ends = jnp.cumsum(padded)
    pad_starts = pad_ends - padded
    row = jnp.sum(jnp.where(onehot, pad_starts[None, :], 0), axis=1) + rank
    n_blk = -(-A // bm) + N_EXPERTS
    row_tok = jnp.zeros((n_blk * bm,), I32).at[row].set(jnp.arange(A, dtype=I32) // TOPK_IN_GROUP)
    e_blk = jnp.concatenate([pad_starts, pad_ends[-1:]]) // bm
    return row_tok, row, gates, e_blk.astype(I32), n_blk


def _final_body(row_ref, x_ref, gate_ref, y_hbm, g_ref, out_ref, ybuf, sem, *, tm):
    i = pl.program_id(0)
    slot = i % 2

    def gather(tile, s):
        def issue(g, carry):
            for u in range(_SUBLANES):
                for k in range(TOPK_IN_GROUP):
                    src = row_ref[(tile * tm + g * _SUBLANES + u) * TOPK_IN_GROUP + k]
                    pltpu.make_async_copy(y_hbm.at[pl.ds(src, 1)], ybuf.at[s, k, g, pl.ds(u, 1)],
                                          sem.at[s]).start()
            return carry
        lax.fori_loop(0, tm // _SUBLANES, issue, 0)

    @pl.when(i == 0)
    def _():
        gather(0, 0)

    @pl.when(i + 1 < pl.num_programs(0))
    def _():
        gather(i + 1, 1 - slot)

    pltpu.make_async_copy(ybuf.at[slot], ybuf.at[slot], sem.at[slot]).wait()
    acc = x_ref[...]
    for k in range(TOPK_IN_GROUP):
        acc = acc + gate_ref[:, k:k + 1] * ybuf[slot, k].reshape(tm, ybuf.shape[-1])
    out_ref[...] = _rms(acc, g_ref[...])


def _final(row, x2, gates, y_rows, g, tm):
    T, D = x2.shape
    grid_spec = pltpu.PrefetchScalarGridSpec(
        num_scalar_prefetch=1,
        grid=(T // tm,),
        in_specs=[pl.BlockSpec((tm, D), lambda i, *_: (i, 0)),
                  pl.BlockSpec((tm, TOPK_IN_GROUP), lambda i, *_: (i, 0)),
                  pl.BlockSpec(memory_space=pl.ANY),
                  pl.BlockSpec(g.shape, lambda i, *_: (0, 0))],
        out_specs=pl.BlockSpec((tm, D), lambda i, *_: (i, 0)),
        scratch_shapes=[pltpu.VMEM((2, TOPK_IN_GROUP, tm // _SUBLANES, _SUBLANES, D), F32),
                        pltpu.SemaphoreType.DMA((2,))],
    )
    return pl.pallas_call(
        functools.partial(_final_body, tm=tm),
        grid_spec=grid_spec,
        out_shape=jax.ShapeDtypeStruct((T, D), F32),
        compiler_params=_params(),
        name="final",
    )(row, x2, gates, y_rows, g)


def _tile_sizes(T):
    pick = lambda want: want if T % want == 0 else CHUNK
    return dict(inproj=pick(256), dsa_q=pick(256), dsa_k=pick(512), mlstm=pick(256), mixout=pick(256),
                final=pick(256), moe=128)


def _layer(x, mem, norm_mix_g, w_in, kv_norm_g, k_idx_norm_g, w_uk, w_uv, conv_w, conv_b, gate_b, ml_norm_g,
           w_out, norm_x_g, mem_norm_g, w_cq, w_ckv, w_co, norm_ffn_g, w_group, b_group, w_router, b_router,
           w_gate, w_up, w_down, out_g):
    T = x.shape[0]
    ts = _tile_sizes(T)
    r2 = lambda v: v.reshape(1, -1)

    w_r = _wprep(w_in, 256)
    wuk_t = jnp.transpose(w_uk, (1, 2, 0)).astype(BF16)
    wuv_t = jnp.transpose(w_uv, (1, 0, 2)).astype(BF16)

    qabs, ckv, qi, kidx, small, mqk, mv, mo = _inproj(
        x, r2(norm_mix_g), w_r, wuk_t, r2(kv_norm_g), r2(k_idx_norm_g), ts["inproj"])

    gate_rows = jnp.transpose(small[:, _S_WI:_S_MF + ML_HEADS])
    dsa_out = _dsa(qi, gate_rows, qabs, kidx, ckv, wuv_t, ts["dsa_q"], ts["dsa_k"])

    gb_col = jnp.zeros((1, _SMALL), F32).at[0, _S_MI:_S_MI + 2 * ML_HEADS].set(gate_b)
    ml_out = _mlstm(mqk, mv, small, gate_rows, mo, conv_w, r2(conv_b), gb_col, gate_b.reshape(-1, 1),
                    r2(ml_norm_g), ts["mlstm"])

    kv = _memkv(mem, r2(mem_norm_g), w_ckv, 512)
    w_rt = jnp.concatenate([w_group, w_router,
                            jnp.zeros((D_MODEL, 128 - N_GROUPS - N_EXPERTS), w_group.dtype)], axis=1)
    x2, hf, rl = _mixout(x, dsa_out, ml_out, kv, w_out.astype(BF16), w_cq.astype(BF16), w_co.astype(BF16),
                         w_rt.astype(BF16), r2(norm_x_g), r2(norm_ffn_g), ts["mixout"])

    bm = ts["moe"]
    row_tok, row, gates, e_blk, n_blk = _route(rl, b_group, b_router, bm)
    y_rows = _moe(row_tok, e_blk, hf, w_gate, w_up, w_down, n_blk, bm)
    return _final(row, x2, gates, y_rows, r2(out_g), ts["final"])


def kernel(x, mem, norm_mix_g, w_in, kv_norm_g, k_idx_norm_g, w_uk, w_uv, conv_w, conv_b, gate_b, ml_norm_g,
           w_out, norm_x_g, mem_norm_g, w_cq, w_ckv, w_co, norm_ffn_g, w_group, b_group, w_router, b_router,
           w_gate, w_up, w_down, final_norm_g):
    B, T, D = x.shape
    assert B == 1 and D == D_MODEL and norm_mix_g.shape[0] == 1 and T % CHUNK == 0
    out = _layer(x[0], mem[0], norm_mix_g[0], w_in[0], kv_norm_g[0], k_idx_norm_g[0], w_uk[0], w_uv[0],
                 conv_w[0], conv_b[0], gate_b[0], ml_norm_g[0], w_out[0], norm_x_g[0], mem_norm_g[0],
                 w_cq[0], w_ckv[0], w_co[0], norm_ffn_g[0], w_group[0], b_group[0], w_router[0], b_router[0],
                 w_gate[0], w_up[0], w_down[0], final_norm_g)
    return out[None]
```

```python
import functools

import jax
import jax.numpy as jnp
import numpy as np
from jax import lax
from jax.experimental import pallas as pl
from jax.experimental.pallas import tpu as pltpu

F32 = jnp.float32
BF16 = jnp.bfloat16
I32 = jnp.int32
I16 = jnp.int16
U32 = jnp.uint32

EPS = 1e-6
CHUNK = 64
D_MODEL = 2048

DSA_HEADS = 8
DSA_HEAD_DIM = 128
DSA_LATENT = 256
IDX_HEADS = 8
IDX_DIM = 64
TOPK_MAX = 256

ML_HEADS = 4
ML_QK_DIM = 128
ML_V_DIM = 256
CONV_W = 4

X_HEADS = 4
X_HEAD_DIM = D_MODEL // X_HEADS

N_GROUPS = 4
EXP_PER_GROUP = 8
N_EXPERTS = N_GROUPS * EXP_PER_GROUP
TOPK_IN_GROUP = 2
D_EXPERT = 512

_O_DQ = 0
_O_CKV = _O_DQ + DSA_HEADS * DSA_HEAD_DIM
_O_QI = _O_CKV + DSA_LATENT
_O_KI = _O_QI + IDX_HEADS * IDX_DIM
_O_WI = _O_KI + IDX_DIM
_O_MQ = _O_WI + IDX_HEADS
_O_MK = _O_MQ + ML_HEADS * ML_QK_DIM
_O_MV = _O_MK + ML_HEADS * ML_QK_DIM
_O_MI = _O_MV + ML_HEADS * ML_V_DIM
_O_MF = _O_MI + ML_HEADS
_O_MO = _O_MF + ML_HEADS
_O_END = _O_MO + ML_HEADS * ML_V_DIM

_G_DQ = (0, 1024)
_G_CKV = (1024, 1280)
_G_QI = (1280, 1792)
_G_SMALL = (1792, 1920)
_G_MQK = (1920, 2944)
_G_MV = (2944, 3968)
_G_MO = (3968, 4992)
_W_COLS = 4992
_S_WI = IDX_DIM
_S_MI = _S_WI + IDX_HEADS
_S_MF = _S_MI + ML_HEADS
_SMALL = 128

_VMEM_LIMIT = 56 * 1024 * 1024
_INT_MIN = -(2 ** 31)
_I16_MIN = -(2 ** 15)
_CHUNK_SHIFT = CHUNK.bit_length() - 1
_LOG2E = 1.4426950408889634
_SUBLANES = 8
_LANES = 128
_GATHER_SLOTS = 9
_GATHER_AHEAD = _GATHER_SLOTS - 1
_RANK_SEG = 512
_NEG = -1e30


def _rms(v, g):
    return v * lax.rsqrt(jnp.mean(v * v, axis=-1, keepdims=True) + EPS) * g


def _dot(a, b):
    return jnp.dot(a, b, preferred_element_type=F32)


def _dot_nt(a, b):
    return lax.dot_general(a, b, (((1,), (1,)), ((), ())), preferred_element_type=F32)


def _resident(shape):
    nd = len(shape)
    return pl.BlockSpec(shape, lambda *_: (0,) * nd, pipeline_mode=pl.Buffered(1))


def _params(n_axes=1):
    return pltpu.CompilerParams(dimension_semantics=("arbitrary",) * n_axes,
                                vmem_limit_bytes=_VMEM_LIMIT)


def _wprep_body(w_hbm, out_ref, wbuf, sem, *, tk):
    i = pl.program_id(0)
    slot = i % 2

    def rows(blk, s):
        return pltpu.make_async_copy(w_hbm.at[pl.ds(pl.multiple_of(blk * tk, tk), tk)], wbuf.at[s], sem.at[s])

    @pl.when(i == 0)
    def _():
        rows(0, 0).start()

    @pl.when(i + 1 < pl.num_programs(0))
    def _():
        rows(i + 1, 1 - slot).start()

    rows(i, slot).wait()
    w_ref = wbuf.at[slot]
    n_small = IDX_DIM + IDX_HEADS
    out_ref[:, _G_DQ[0]:_G_SMALL[0] + n_small] = w_ref[:, _O_DQ:_O_MQ].astype(BF16)
    out_ref[:, _G_SMALL[0] + n_small:_G_SMALL[0] + n_small + 2 * ML_HEADS] = w_ref[:, _O_MI:_O_MO].astype(BF16)
    out_ref[:, _G_SMALL[0] + n_small + 2 * ML_HEADS:_G_SMALL[1]] = jnp.zeros(
        (out_ref.shape[0], _SMALL - n_small - 2 * ML_HEADS), BF16)
    out_ref[:, _G_MQK[0]:_G_MV[1]] = w_ref[:, _O_MQ:_O_MI].astype(BF16)
    out_ref[:, _G_MO[0]:_G_MO[1]] = w_ref[:, _O_MO:_O_END].astype(BF16)


def _wprep(w_in, tk):
    K = w_in.shape[0]
    return pl.pallas_call(
        functools.partial(_wprep_body, tk=tk),
        grid=(K // tk,),
        in_specs=[pl.BlockSpec(memory_space=pl.ANY)],
        out_specs=pl.BlockSpec((tk, _W_COLS), lambda i: (i, 0)),
        out_shape=jax.ShapeDtypeStruct((K, _W_COLS), BF16),
        scratch_shapes=[pltpu.VMEM((2, tk, w_in.shape[1]), F32), pltpu.SemaphoreType.DMA((2,))],
        compiler_params=_params(),
        name="wprep",
    )(w_in)


def _inproj_body(x_ref, g_ref, w_ref, wuk_ref, kvg_ref, kig_ref,
                 qabs_ref, ckv_ref, qi_ref, kidx_ref, small_ref, mqk_ref, mv_ref, mo_ref):
    h = _rms(x_ref[...], g_ref[...]).astype(BF16)

    def proj(grp):
        return _dot(h, w_ref[:, grp[0]:grp[1]])

    dq = proj(_G_DQ)
    for hd in range(DSA_HEADS):
        qh = dq[:, hd * DSA_HEAD_DIM:(hd + 1) * DSA_HEAD_DIM].astype(BF16)
        qa = _dot(qh, wuk_ref[hd]) * (DSA_HEAD_DIM ** -0.5 * _LOG2E)
        qabs_ref[:, hd * DSA_LATENT:(hd + 1) * DSA_LATENT] = qa.astype(BF16)
    ckv_ref[...] = _rms(proj(_G_CKV), kvg_ref[...]).astype(BF16)
    qi_ref[...] = (proj(_G_QI) * (IDX_DIM ** -0.5)).astype(BF16)
    small = proj(_G_SMALL)
    small_ref[...] = small
    kidx_ref[...] = _rms(small[:, :IDX_DIM], kig_ref[...]).astype(BF16)
    mqk_ref[...] = proj(_G_MQK)
    mv_ref[...] = proj(_G_MV).astype(BF16)
    mo_ref[...] = proj(_G_MO)


def _inproj(x, g, w, wuk, kvg, kig, tm):
    T = x.shape[0]
    row = lambda n: pl.BlockSpec((tm, n), lambda i: (i, 0))
    outs = [(8 * DSA_LATENT, BF16), (DSA_LATENT, BF16), (IDX_HEADS * IDX_DIM, BF16), (IDX_DIM, BF16),
            (_SMALL, F32), (2 * ML_HEADS * ML_QK_DIM, F32), (ML_HEADS * ML_V_DIM, BF16),
            (ML_HEADS * ML_V_DIM, F32)]
    return pl.pallas_call(
        _inproj_body,
        grid=(T // tm,),
        in_specs=[row(D_MODEL), _resident(g.shape), _resident(w.shape), _resident(wuk.shape),
                  _resident(kvg.shape), _resident(kig.shape)],
        out_specs=[row(n) for n, _ in outs],
        out_shape=[jax.ShapeDtypeStruct((T, n), dt) for n, dt in outs],
        compiler_params=_params(),
        name="inproj",
    )(x, g, w, wuk, kvg, kig)


def _sublane_fold(v, op, rows=_SUBLANES, ways=4):
    groups = [v[r * rows:(r + 1) * rows, :] for r in range(v.shape[0] // rows)]
    accs = groups[:ways]
    for r in range(ways, len(groups)):
        accs[r % ways] = op(accs[r % ways], groups[r])
    while len(accs) > 1:
        accs = [op(accs[k], accs[k + 1]) if k + 1 < len(accs) else accs[k] for k in range(0, len(accs), 2)]
    return accs[0]


def _bit_transpose32(words):
    a = list(words)
    j, m = 16, 0x0000FFFF
    while j:
        k = 0
        while k < 32:
            t = (a[k] ^ (a[k + j] >> j)) & jnp.uint32(m)
            a[k] = a[k] ^ t
            a[k + j] = a[k + j] ^ (t << j)
            k = (k + j + 1) & ~j
        j >>= 1
        m = (m ^ (m << j)) & 0xFFFFFFFF
    return a


def _dsa_body(qi_ref, wrow_ref, qabs_ref, kidx_ref, ckv_ref, ckvt_ref, wuv_ref, out_ref,
              key_ref, planes_ref, alive_ref, lga_ref, lgb_ref, m_ref, l_ref, acc_ref, *, tq, tk, topk, nbits_idx):
    i = pl.program_id(0)
    n_kb = ((i + 1) * tq + tk - 1) // tk
    w_rows = wrow_ref[0:IDX_HEADS, :] * (IDX_HEADS ** -0.5)
    q_chunk = (i * tq + lax.broadcasted_iota(I32, (1, tq), 1)) >> _CHUNK_SHIFT

    def key_pos(j):
        return j * tk + lax.broadcasted_iota(I32, (tk, 1), 0)

    def score_block(j, carry):
        kx = kidx_ref[pl.ds(pl.multiple_of(j * tk, tk), tk), :]
        s = jnp.zeros((tk, tq), F32)
        for hd in range(IDX_HEADS):
            d = _dot_nt(kx, qi_ref[:, hd * IDX_DIM:(hd + 1) * IDX_DIM])
            s = s + w_rows[hd:hd + 1, :] * jnp.maximum(d, 0.0)
        bits = lax.bitcast_convert_type(s, I32)
        key = bits ^ ((bits >> 31) & 0x7FFFFFFF)
        key_ref[j] = jnp.where((key_pos(j) >> _CHUNK_SHIFT) <= q_chunk, key, _INT_MIN)
        return carry

    lax.fori_loop(0, n_kb, score_block, 0)

    def count(pred):
        def body(j, acc):
            hit = pred(key_ref[j], key_pos(j)).astype(I32)
            return acc + _sublane_fold(hit, jnp.add)
        acc = lax.fori_loop(0, n_kb, body, jnp.zeros((8, tq), I32))
        return jnp.sum(acc, axis=0, keepdims=True)

    n_pair = (n_kb + 1) // 2
    plane_rows = tk // 32

    @pl.when(i == 0)
    def _():
        planes_ref[...] = jnp.zeros(planes_ref.shape, U32)

    def slice_block(j, carry):
        u = lax.bitcast_convert_type(key_ref[j], U32) ^ jnp.uint32(0x80000000)
        for h in range(plane_rows // _SUBLANES):
            base = h * 32 * _SUBLANES
            planes = _bit_transpose32([u[base + _SUBLANES * v:base + _SUBLANES * (v + 1), :] for v in range(32)])
            row0 = pl.multiple_of(j * plane_rows + h * _SUBLANES, _SUBLANES)
            for b in range(32):
                planes_ref[b, pl.ds(row0, _SUBLANES), :] = planes[31 - b]
        return carry

    lax.fori_loop(0, n_kb, slice_block, 0)
    word_row = lax.broadcasted_iota(I32, (alive_ref.shape[0], 1), 0)
    alive_ref[...] = jnp.where(word_row < n_kb * plane_rows, jnp.full(alive_ref.shape, 0xFFFFFFFF, U32),
                               jnp.uint32(0))

    def select_bit(it, carry):
        need, t_u = carry
        b = 31 - it
        alive = alive_ref[...]
        ones = alive & planes_ref[b]
        n_ones = jnp.sum(_sublane_fold(lax.population_count(ones).astype(I32), jnp.add), axis=0, keepdims=True)
        take = n_ones >= need
        alive_ref[...] = jnp.where(take, ones, alive ^ ones)
        bit = lax.shift_left(jnp.uint32(1), jnp.asarray(b, U32))
        return jnp.where(take, need, need - n_ones), jnp.where(take, t_u | bit, t_u)

    need, t_u = lax.fori_loop(0, 32, select_bit, (jnp.full((1, tq), topk, I32), jnp.zeros((1, tq), U32)))
    n_equal = jnp.sum(_sublane_fold(lax.population_count(alive_ref[...]).astype(I32), jnp.add), axis=0,
                      keepdims=True)
    short = t_u == 0
    t = jnp.maximum(lax.bitcast_convert_type(t_u ^ jnp.uint32(0x80000000), I32), _INT_MIN + 1)
    all_pos = jnp.int32(2 ** nbits_idx - 1)
    has_ties = (n_equal > need) & jnp.logical_not(short)
    n_tie_take = jnp.where(has_ties, need, all_pos)

    def tie_cutoff():
        def pos_bit(b, c):
            cand = c + lax.shift_left(jnp.int32(1), nbits_idx - 1 - b)
            f = count(lambda kb, pos: (kb == t) & (pos < cand))
            return jnp.where(f <= n_tie_take, cand, c)
        return lax.fori_loop(0, nbits_idx, pos_bit, jnp.zeros((1, tq), I32))

    cut = lax.cond(jnp.max(has_ties.astype(I32)) > 0, tie_cutoff, lambda: jnp.full((1, tq), all_pos, I32))

    def masked_block(j, carry):
        key_ref[j] = jnp.full((tk, tq), _INT_MIN, I32)
        return carry

    lax.fori_loop(n_kb, 2 * n_pair, masked_block, 0)

    m_ref[...] = jnp.full(m_ref.shape, _NEG, F32)
    l_ref[...] = jnp.zeros(l_ref.shape, F32)
    acc_ref[...] = jnp.zeros(acc_ref.shape, F32)
    last_blk = ckvt_ref.shape[0] - 1

    def logits(j, lg_buf):
        c_blk = ckv_ref[pl.ds(pl.multiple_of(jnp.minimum(j, last_blk) * tk, tk), tk), :]
        kb = key_ref[j]
        sel = (kb > t) | ((kb == t) & (key_pos(j) < cut))
        bias = jnp.where(sel, 0.0, _NEG).astype(F32)
        for hd in range(DSA_HEADS):
            lg_buf[hd] = _dot_nt(c_blk, qabs_ref[:, hd * DSA_LATENT:(hd + 1) * DSA_LATENT]) + bias

    def accumulate(j, lg_buf):
        c_blk_t = ckvt_ref[jnp.minimum(j, last_blk)]
        for hd in range(DSA_HEADS):
            lg = lg_buf[hd]
            m_old = m_ref[hd:hd + 1, :]
            m_new = jnp.maximum(m_old, jnp.max(_sublane_fold(lg, jnp.maximum, ways=1), axis=0, keepdims=True))
            p = jnp.exp2(lg - m_new)
            alpha = jnp.exp2(m_old - m_new)
            l_ref[hd:hd + 1, :] = alpha * l_ref[hd:hd + 1, :] + jnp.sum(_sublane_fold(p, jnp.add, ways=1), axis=0,
                                                                         keepdims=True)
            acc_ref[hd] = alpha * acc_ref[hd] + _dot(c_blk_t, p.astype(BF16))
            m_ref[hd:hd + 1, :] = m_new

    logits(0, lga_ref)

    def attn_pair(mi, carry):
        ja = 2 * mi
        accumulate(ja, lga_ref)
        logits(ja + 1, lgb_ref)
        accumulate(ja + 1, lgb_ref)
        logits(jnp.minimum(ja + 2, 2 * n_pair - 1), lga_ref)
        return carry

    lax.fori_loop(0, n_pair, attn_pair, 0)

    for hd in range(DSA_HEADS):
        o_lat = (acc_ref[hd] / l_ref[hd:hd + 1, :]).T.astype(BF16)
        out_ref[:, hd * DSA_HEAD_DIM:(hd + 1) * DSA_HEAD_DIM] = _dot(o_lat, wuv_ref[hd]).astype(BF16)


def _dsa(qi, wrows, qabs, kidx, ckv, wuv, tq, tk):
    T = qi.shape[0]
    topk = min(TOPK_MAX, T // 4)
    n_kb = T // tk
    ckvt = jnp.transpose(ckv.reshape(n_kb, tk, DSA_LATENT), (0, 2, 1))
    row = lambda n: pl.BlockSpec((tq, n), lambda i: (i, 0))
    body = functools.partial(_dsa_body, tq=tq, tk=tk, topk=topk, nbits_idx=int(T).bit_length())
    return pl.pallas_call(
        body,
        grid=(T // tq,),
        in_specs=[row(qi.shape[1]), pl.BlockSpec((wrows.shape[0], tq), lambda i: (0, i)), row(qabs.shape[1]),
                  _resident(kidx.shape), _resident(ckv.shape), _resident(ckvt.shape), _resident(wuv.shape)],
        out_specs=row(DSA_HEADS * DSA_HEAD_DIM),
        out_shape=jax.ShapeDtypeStruct((T, DSA_HEADS * DSA_HEAD_DIM), BF16),
        scratch_shapes=[pltpu.VMEM((n_kb + n_kb % 2, tk, tq), I32),
                        pltpu.VMEM((32, T // 32, tq), U32), pltpu.VMEM((T // 32, tq), U32),
                        pltpu.VMEM((DSA_HEADS, tk, tq), F32), pltpu.VMEM((DSA_HEADS, tk, tq), F32),
                        pltpu.VMEM((DSA_HEADS, tq), F32),
                        pltpu.VMEM((DSA_HEADS, tq), F32), pltpu.VMEM((DSA_HEADS, DSA_LATENT, tq), F32)],
        compiler_params=_params(),
        name="dsa",
    )(qi, wrows, qabs, kidx, ckv, ckvt, wuv)


def _log_sigmoid(v):
    return jnp.minimum(v, 0.0) - jnp.log1p(jnp.exp(-jnp.abs(v)))


def _chunk_cumsum(v, axis):
    pos = lax.broadcasted_iota(I32, v.shape, axis) & (CHUNK - 1)
    d = 1
    while d < CHUNK:
        v = v + jnp.where(pos >= d, pltpu.roll(v, d, axis=axis), 0.0)
        d *= 2
    return v


def _mlstm_body(mqk_ref, mv_ref, small_ref, gt_ref, mo_ref, cw_ref, cb_ref, gbc_ref, gbr_ref, ng_ref,
                out_ref, xe_ref, c_ref, n_ref, m_ref, hs_ref, *, rows):
    @pl.when(pl.program_id(0) == 0)
    def _():
        xe_ref[0:8, :] = jnp.zeros((8, xe_ref.shape[1]), F32)
        c_ref[...] = jnp.zeros(c_ref.shape, F32)
        n_ref[...] = jnp.zeros(n_ref.shape, F32)
        m_ref[...] = jnp.zeros(m_ref.shape, F32)

    x = mqk_ref[...]
    xe_ref[8:8 + rows, :] = x
    y = cb_ref[...]
    for j in range(CONV_W - 1):
        y = y + xe_ref[5 + j:5 + j + rows, :] * cw_ref[j:j + 1, :]
    y = y + x * cw_ref[CONV_W - 1:CONV_W, :]
    xe_ref[0:8, :] = x[rows - 8:rows, :]
    qk = y * jax.nn.sigmoid(y)
    nqk = ML_HEADS * ML_QK_DIM
    q_all = (qk[:, :nqk] * (ML_QK_DIM ** -0.5)).astype(BF16)
    k_all = qk[:, nqk:]

    g_col = small_ref[...] + gbc_ref[...]
    g_row = gt_ref[...] + gbr_ref[...]
    b_col = _chunk_cumsum(_log_sigmoid(g_col), 0)
    b_row = _chunk_cumsum(_log_sigmoid(g_row), 1)

    tri = lax.broadcasted_iota(I32, (CHUNK, CHUNK), 1) <= lax.broadcasted_iota(I32, (CHUNK, CHUNK), 0)

    for c in range(rows // CHUNK):
        lo, hi = c * CHUNK, (c + 1) * CHUNK
        for hd in range(ML_HEADS):
            bc = b_col[lo:hi, _S_MF + hd:_S_MF + hd + 1]
            lic = g_col[lo:hi, _S_MI + hd:_S_MI + hd + 1]
            br = b_row[ML_HEADS + hd:ML_HEADS + hd + 1, lo:hi]
            lir = g_row[hd:hd + 1, lo:hi]
            g_tot = bc[CHUNK - 1:CHUNK, :]
            m_prev = m_ref[hd][:, 0:1]

            dmat = jnp.where(tri, bc - br + lir, -jnp.inf)
            inter = bc + m_prev
            m_t = jnp.maximum(inter, jnp.max(dmat, axis=-1, keepdims=True))
            w_intra = jnp.exp(dmat - m_t)
            a_inter = jnp.exp(inter - m_t)

            qh = q_all[lo:hi, hd * ML_QK_DIM:(hd + 1) * ML_QK_DIM]
            kh = k_all[lo:hi, hd * ML_QK_DIM:(hd + 1) * ML_QK_DIM]
            vh = mv_ref[lo:hi, hd * ML_V_DIM:(hd + 1) * ML_V_DIM]
            s_qk = _dot_nt(qh, kh.astype(BF16)) * w_intra
            c_prev = c_ref[hd]
            n_prev = n_ref[hd]
            num = a_inter * _dot(qh, c_prev.astype(BF16)) + _dot(s_qk.astype(BF16), vh)
            den = (a_inter * jnp.sum(qh.astype(F32) * n_prev, axis=-1, keepdims=True)
                   + jnp.sum(s_qk, axis=-1, keepdims=True))
            hs_ref[lo:hi, hd * ML_V_DIM:(hd + 1) * ML_V_DIM] = (
                num / jnp.maximum(jnp.abs(den), jnp.exp(-m_t)))

            m_new = jnp.maximum(g_tot + m_prev, jnp.max(g_tot - br + lir, axis=-1, keepdims=True))
            a_state = jnp.exp(g_tot + m_prev - m_new)
            wk = jnp.exp(g_tot - bc + lic - m_new) * kh
            c_ref[hd] = a_state * c_prev + _dot(wk.T.astype(BF16), vh)
            n_ref[hd] = a_state * n_prev + jnp.sum(wk, axis=0, keepdims=True)
            m_ref[hd] = jnp.broadcast_to(m_new, m_ref.shape[1:])

    for hd in range(ML_HEADS):
        sl = slice(hd * ML_V_DIM, (hd + 1) * ML_V_DIM)
        out_ref[:, sl] = (_rms(hs_ref[:, sl], ng_ref[:, sl]) * jax.nn.sigmoid(mo_ref[:, sl])).astype(BF16)


def _mlstm(mqk, mv, small, gt, mo, cw, cb, gbc, gbr, ng, rows):
    T = mqk.shape[0]
    row = lambda n: pl.BlockSpec((rows, n), lambda i: (i, 0))
    nv = ML_HEADS * ML_V_DIM
    return pl.pallas_call(
        functools.partial(_mlstm_body, rows=rows),
        grid=(T // rows,),
        in_specs=[row(mqk.shape[1]), row(nv), row(_SMALL), pl.BlockSpec((8, rows), lambda i: (1, i)), row(nv),
                  _resident(cw.shape), _resident(cb.shape), _resident(gbc.shape), _resident(gbr.shape),
                  _resident(ng.shape)],
        out_specs=row(nv),
        out_shape=jax.ShapeDtypeStruct((T, nv), BF16),
        scratch_shapes=[pltpu.VMEM((rows + 8, mqk.shape[1]), F32),
                        pltpu.VMEM((ML_HEADS, ML_QK_DIM, ML_V_DIM), F32),
                        pltpu.VMEM((ML_HEADS, 1, ML_QK_DIM), F32),
                        pltpu.VMEM((ML_HEADS, 1, 128), F32),
                        pltpu.VMEM((rows, nv), F32)],
        compiler_params=_params(),
        name="mlstm",
    )(mqk, mv, small, gt, mo, cw, cb, gbc, gbr, ng)


def _memkv_body(mem_ref, g_ref, w_ref, out_ref):
    mn = _rms(mem_ref[...], g_ref[...]).astype(BF16)
    out_ref[...] = _dot(mn, w_ref[...].astype(BF16)).astype(BF16)


def _memkv(mem, g, w, tn):
    M, D = mem.shape
    N = w.shape[1]
    return pl.pallas_call(
        _memkv_body,
        grid=(N // tn,),
        in_specs=[_resident(mem.shape), _resident(g.shape), pl.BlockSpec((D, tn), lambda j: (0, j))],
        out_specs=pl.BlockSpec((M, tn), lambda j: (0, j)),
        out_shape=jax.ShapeDtypeStruct((M, N), BF16),
        compiler_params=_params(),
        name="memkv",
    )(mem, g, w)


def _mixout_body(x_ref, dsa_ref, ml_ref, kv_ref, wo_ref, wq_ref, wc_ref, wr_ref, gx_ref, gf_ref,
                 x2_ref, hf_ref, rl_ref, o_ref):
    nd = dsa_ref.shape[1]
    x1 = x_ref[...] + _dot(dsa_ref[...], wo_ref[0:nd, :]) + _dot(ml_ref[...], wo_ref[nd:, :])
    q = _dot(_rms(x1, gx_ref[...]).astype(BF16), wq_ref[...]).astype(BF16)
    for hd in range(X_HEADS):
        sl = slice(hd * X_HEAD_DIM, (hd + 1) * X_HEAD_DIM)
        lg = _dot_nt(q[:, sl], kv_ref[:, sl]) * (X_HEAD_DIM ** -0.5)
        e = jnp.exp(lg - jnp.max(lg, axis=-1, keepdims=True))
        p = e / jnp.sum(e, axis=-1, keepdims=True)
        v = kv_ref[:, D_MODEL + hd * X_HEAD_DIM:D_MODEL + (hd + 1) * X_HEAD_DIM]
        o_ref[:, sl] = _dot(p.astype(BF16), v).astype(BF16)
    x2 = x1 + _dot(o_ref[...], wc_ref[...])
    x2_ref[...] = x2
    hf = _rms(x2, gf_ref[...]).astype(BF16)
    rl_ref[...] = _dot(hf, wr_ref[...])
    bits = lax.bitcast_convert_type(hf.astype(F32), U32)
    for c in range(_SUBLANES):
        lo = bits[:, (2 * c) * _LANES:(2 * c + 1) * _LANES]
        hi = bits[:, (2 * c + 1) * _LANES:(2 * c + 2) * _LANES]
        hf_ref[pl.ds(c, x2.shape[0], stride=_SUBLANES), :] = (hi & jnp.uint32(0xFFFF0000)) | (lo >> 16)


def _mixout(x, dsa, ml, kv, wo, wq, wc, wr, gx, gf, tm):
    T = x.shape[0]
    row = lambda n: pl.BlockSpec((tm, n), lambda i: (i, 0))
    return pl.pallas_call(
        _mixout_body,
        grid=(T // tm,),
        in_specs=[row(D_MODEL), row(dsa.shape[1]), row(ml.shape[1]), _resident(kv.shape), _resident(wo.shape),
                  _resident(wq.shape), _resident(wc.shape), _resident(wr.shape), _resident(gx.shape),
                  _resident(gf.shape)],
        out_specs=[row(D_MODEL), pl.BlockSpec((tm * _SUBLANES, _LANES), lambda i: (i, 0)), row(wr.shape[1])],
        out_shape=[jax.ShapeDtypeStruct((T, D_MODEL), F32),
                   jax.ShapeDtypeStruct((T * _SUBLANES, _LANES), U32),
                   jax.ShapeDtypeStruct((T, wr.shape[1]), F32)],
        scratch_shapes=[pltpu.VMEM((tm, D_MODEL), BF16)],
        compiler_params=_params(),
        name="mixout",
    )(x, dsa, ml, kv, wo, wq, wc, wr, gx, gf)


def _moe_body(tok_ref, eb_ref, hf_hbm, wg_ref, wu_ref, wd_ref, y_hbm,
              xbuf, ybuf, wgb, wub, wdb, gsem, ysem, *, bm, n_blk):
    e = pl.program_id(0)
    n_valid = eb_ref[N_EXPERTS]
    b_lo = eb_ref[e]
    b_hi = eb_ref[e + 1]

    def tok_words(tok):
        return hf_hbm.at[pl.ds(pl.multiple_of(tok * _SUBLANES, _SUBLANES), _SUBLANES)]

    def gather(blk, s):
        def issue(g, carry):
            for u in range(_SUBLANES):
                r = g * _SUBLANES + u
                pltpu.make_async_copy(tok_words(tok_ref[blk * bm + r]),
                                      xbuf.at[s, pl.ds(pl.multiple_of(r * _SUBLANES, _SUBLANES), _SUBLANES)],
                                      gsem.at[s]).start(priority=1)
            return carry
        lax.fori_loop(0, bm // _SUBLANES, issue, 0)

    def wait_gather(s):
        pltpu.make_async_copy(xbuf.at[s], xbuf.at[s], gsem.at[s]).wait()

    def y_copy(blk, s):
        return pltpu.make_async_copy(ybuf.at[s], y_hbm.at[pl.ds(pl.multiple_of(blk * bm, bm), bm)], ysem.at[s])

    @pl.when(e == 0)
    def _():
        for k in range(_GATHER_AHEAD):
            gather(k, k)

    @pl.when(b_hi > b_lo)
    def _():
        wgb[...] = wg_ref[0].astype(BF16)
        wub[...] = wu_ref[0].astype(BF16)
        wdb[...] = wd_ref[0].astype(BF16)

        def block(b, carry):
            s = b % _GATHER_SLOTS
            wait_gather(s)
            chunks = []
            for c in range(_SUBLANES):
                w = xbuf[s, pl.ds(c, bm, stride=_SUBLANES), :]
                chunks.append(lax.bitcast_convert_type(w << 16, F32).astype(BF16))
                chunks.append(lax.bitcast_convert_type(w & jnp.uint32(0xFFFF0000), F32).astype(BF16))
            xb = jnp.concatenate(chunks, axis=-1)
            gate = _dot(xb, wgb[...])
            a = gate * jax.nn.sigmoid(gate) * _dot(xb, wub[...])
            y = _dot(a.astype(BF16), wdb[...])

            nxt = jnp.minimum(b + _GATHER_AHEAD, n_blk - 1)
            for r in range(bm):
                pltpu.make_async_copy(tok_words(tok_ref[nxt * bm + r]),
                                      xbuf.at[(b + _GATHER_AHEAD) % _GATHER_SLOTS, pl.ds(r * _SUBLANES, _SUBLANES)],
                                      gsem.at[(b + _GATHER_AHEAD) % _GATHER_SLOTS]).start(priority=1)

            @pl.when(b >= 2)
            def _():
                y_copy(b - 2, b % 2).wait()

            ybuf[b % 2] = y
            y_copy(b, b % 2).start()
            return carry

        lax.fori_loop(b_lo, b_hi, block, 0)

    @pl.when(e == pl.num_programs(0) - 1)
    def _():
        for k in range(_GATHER_AHEAD):
            wait_gather((n_valid + k) % _GATHER_SLOTS)

        @pl.when(n_valid >= 2)
        def _():
            y_copy(n_valid - 2, n_valid % 2).wait()
        y_copy(n_valid - 1, (n_valid - 1) % 2).wait()
        ybuf[0] = jnp.zeros(ybuf.shape[1:], F32)

        def zero_block(b, carry):
            cp = y_copy(b, 0)
            cp.start()
            cp.wait()
            return carry

        lax.fori_loop(n_valid, n_blk, zero_block, 0)


def _moe(tok, e_blk, hf, wg, wu, wd, n_blk, bm):
    D = wg.shape[1]
    wspec = lambda shape: pl.BlockSpec((1,) + shape, lambda e, *_: (e, 0, 0))
    grid_spec = pltpu.PrefetchScalarGridSpec(
        num_scalar_prefetch=2,
        grid=(N_EXPERTS,),
        in_specs=[pl.BlockSpec(memory_space=pl.ANY),
                  wspec((D, D_EXPERT)), wspec((D, D_EXPERT)), wspec((D_EXPERT, D))],
        out_specs=pl.BlockSpec(memory_space=pl.ANY),
        scratch_shapes=[pltpu.VMEM((_GATHER_SLOTS, bm * _SUBLANES, _LANES), U32),
                        pltpu.VMEM((2, bm, D), F32),
                        pltpu.VMEM((D, D_EXPERT), BF16), pltpu.VMEM((D, D_EXPERT), BF16),
                        pltpu.VMEM((D_EXPERT, D), BF16),
                        pltpu.SemaphoreType.DMA((_GATHER_SLOTS,)), pltpu.SemaphoreType.DMA((2,))],
    )
    return pl.pallas_call(
        functools.partial(_moe_body, bm=bm, n_blk=n_blk),
        grid_spec=grid_spec,
        out_shape=jax.ShapeDtypeStruct((n_blk * bm, D), F32),
        compiler_params=_params(),
        name="moe",
    )(tok, e_blk, hf, wg, wu, wd)


def _route(rl, b_group, b_router, bm):
    N = rl.shape[0]
    g_logits = rl[:, :N_GROUPS] + b_group
    g_prob = jax.nn.softmax(g_logits, axis=-1)
    g_sel = jnp.argmax(g_logits, axis=-1)
    p_g = jnp.take_along_axis(g_prob, g_sel[:, None], axis=-1)
    e_logits = rl[:, N_GROUPS:N_GROUPS + N_EXPERTS] + b_router
    in_group = (jnp.arange(N_EXPERTS, dtype=I32) // EXP_PER_GROUP)[None, :] == g_sel[:, None]
    e_prob = jax.nn.softmax(jnp.where(in_group, e_logits, -jnp.inf), axis=-1)
    top_p, expert_id = lax.top_k(jnp.where(in_group, e_prob, -1.0), TOPK_IN_GROUP)
    gates = p_g * top_p / top_p.sum(-1, keepdims=True)
    expert_id = expert_id.astype(I32)

    A = N * TOPK_IN_GROUP
    flat_e = expert_id.reshape(A)
    onehot = (flat_e[:, None] == jnp.arange(N_EXPERTS, dtype=I32)[None, :])
    seg = _RANK_SEG if A % _RANK_SEG == 0 else A
    oh = onehot.astype(BF16).reshape(A // seg, seg, N_EXPERTS)
    before = (jnp.arange(seg)[None, :] < jnp.arange(seg)[:, None]).astype(BF16)
    within = jnp.einsum('ij,bjk->bik', before, oh, preferred_element_type=F32)
    seg_tot = jnp.sum(oh.astype(F32), axis=1)
    seg_base = jnp.cumsum(seg_tot, axis=0) - seg_tot
    rank = jnp.sum((within + seg_base[:, None, :]) * oh.astype(F32), axis=-1).reshape(A).astype(I32)
    counts = jnp.sum(seg_tot, axis=0).astype(I32)
    padded = (counts + bm - 1) // bm * bm
    pad_ends = jnp.cumsum(padded)
    pad_starts = pad_ends - padded
    row = jnp.sum(jnp.where(onehot, pad_starts[None, :], 0), axis=1) + rank
    n_blk = -(-A // bm) + N_EXPERTS
    row_tok = jnp.zeros((n_blk * bm,), I32).at[row].set(jnp.arange(A, dtype=I32) // TOPK_IN_GROUP)
    e_blk = jnp.concatenate([pad_starts, pad_ends[-1:]]) // bm
    return row_tok, row, gates, e_blk.astype(I32), n_blk


def _final_body(row_ref, x_ref, gate_ref, y_hbm, g_ref, out_ref, ybuf, sem, *, tm):
    i = pl.program_id(0)
    slot = i % 2

    def gather(tile, s):
        def issue(g, carry):
            for u in range(_SUBLANES):
                for k in range(TOPK_IN_GROUP):
                    src = row_ref[(tile * tm + g * _SUBLANES + u) * TOPK_IN_GROUP + k]
                    pltpu.make_async_copy(y_hbm.at[pl.ds(src, 1)], ybuf.at[s, k, g, pl.ds(u, 1)],
                                          sem.at[s]).start()
            return carry
        lax.fori_loop(0, tm // _SUBLANES, issue, 0)

    @pl.when(i == 0)
    def _():
        gather(0, 0)

    @pl.when(i + 1 < pl.num_programs(0))
    def _():
        gather(i + 1, 1 - slot)

    pltpu.make_async_copy(ybuf.at[slot], ybuf.at[slot], sem.at[slot]).wait()
    acc = x_ref[...]
    for k in range(TOPK_IN_GROUP):
        acc = acc + gate_ref[:, k:k + 1] * ybuf[slot, k].reshape(tm, ybuf.shape[-1])
    out_ref[...] = _rms(acc, g_ref[...])


def _final(row, x2, gates, y_rows, g, tm):
    T, D = x2.shape
    grid_spec = pltpu.PrefetchScalarGridSpec(
        num_scalar_prefetch=1,
        grid=(T // tm,),
        in_specs=[pl.BlockSpec((tm, D), lambda i, *_: (i, 0)),
                  pl.BlockSpec((tm, TOPK_IN_GROUP), lambda i, *_: (i, 0)),
                  pl.BlockSpec(memory_space=pl.ANY),
                  pl.BlockSpec(g.shape, lambda i, *_: (0, 0))],
        out_specs=pl.BlockSpec((tm, D), lambda i, *_: (i, 0)),
        scratch_shapes=[pltpu.VMEM((2, TOPK_IN_GROUP, tm // _SUBLANES, _SUBLANES, D), F32),
                        pltpu.SemaphoreType.DMA((2,))],
    )
    return pl.pallas_call(
        functools.partial(_final_body, tm=tm),
        grid_spec=grid_spec,
        out_shape=jax.ShapeDtypeStruct((T, D), F32),
        compiler_params=_params(),
        name="final",
    )(row, x2, gates, y_rows, g)


def _tile_sizes(T):
    pick = lambda want: want if T % want == 0 else CHUNK
    return dict(inproj=pick(256), dsa_q=pick(256), dsa_k=pick(512), mlstm=pick(256), mixout=pick(256),
                final=pick(256), moe=128)


def _layer(x, mem, norm_mix_g, w_in, kv_norm_g, k_idx_norm_g, w_uk, w_uv, conv_w, conv_b, gate_b, ml_norm_g,
           w_out, norm_x_g, mem_norm_g, w_cq, w_ckv, w_co, norm_ffn_g, w_group, b_group, w_router, b_router,
           w_gate, w_up, w_down, out_g):
    T = x.shape[0]
    ts = _tile_sizes(T)
    r2 = lambda v: v.reshape(1, -1)

    w_r = _wprep(w_in, 256)
    wuk_t = jnp.transpose(w_uk, (1, 2, 0)).astype(BF16)
    wuv_t = jnp.transpose(w_uv, (1, 0, 2)).astype(BF16)

    qabs, ckv, qi, kidx, small, mqk, mv, mo = _inproj(
        x, r2(norm_mix_g), w_r, wuk_t, r2(kv_norm_g), r2(k_idx_norm_g), ts["inproj"])

    gate_rows = jnp.transpose(small[:, _S_WI:_S_MF + ML_HEADS])
    dsa_out = _dsa(qi, gate_rows, qabs, kidx, ckv, wuv_t, ts["dsa_q"], ts["dsa_k"])

    gb_col = jnp.zeros((1, _SMALL), F32).at[0, _S_MI:_S_MI + 2 * ML_HEADS].set(gate_b)
    ml_out = _mlstm(mqk, mv, small, gate_rows, mo, conv_w, r2(conv_b), gb_col, gate_b.reshape(-1, 1),
                    r2(ml_norm_g), ts["mlstm"])

    kv = _memkv(mem, r2(mem_norm_g), w_ckv, 512)
    w_rt = jnp.concatenate([w_group, w_router,
                            jnp.zeros((D_MODEL, 128 - N_GROUPS - N_EXPERTS), w_group.dtype)], axis=1)
    x2, hf, rl = _mixout(x, dsa_out, ml_out, kv, w_out.astype(BF16), w_cq.astype(BF16), w_co.astype(BF16),
                         w_rt.astype(BF16), r2(norm_x_g), r2(norm_ffn_g), ts["mixout"])

    bm = ts["moe"]
    row_tok, row, gates, e_blk, n_blk = _route(rl, b_group, b_router, bm)
    y_rows = _moe(row_tok, e_blk, hf, w_gate, w_up, w_down, n_blk, bm)
    return _final(row, x2, gates, y_rows, r2(out_g), ts["final"])


def kernel(x, mem, norm_mix_g, w_in, kv_norm_g, k_idx_norm_g, w_uk, w_uv, conv_w, conv_b, gate_b, ml_norm_g,
           w_out, norm_x_g, mem_norm_g, w_cq, w_ckv, w_co, norm_ffn_g, w_group, b_group, w_router, b_router,
           w_gate, w_up, w_down, final_norm_g):
    B, T, D = x.shape
    assert B == 1 and D == D_MODEL and norm_mix_g.shape[0] == 1 and T % CHUNK == 0
    out = _layer(x[0], mem[0], norm_mix_g[0], w_in[0], kv_norm_g[0], k_idx_norm_g[0], w_uk[0], w_uv[0],
                 conv_w[0], conv_b[0], gate_b[0], ml_norm_g[0], w_out[0], norm_x_g[0], mem_norm_g[0],
                 w_cq[0], w_ckv[0], w_co[0], norm_ffn_g[0], w_group[0], b_group[0], w_router[0], b_router[0],
                 w_gate[0], w_up[0], w_down[0], final_norm_g)
    return out[None]
```

```python
import functools

import jax
import jax.numpy as jnp
import numpy as np
from jax import lax
from jax.experimental import pallas as pl
from jax.experimental.pallas import tpu as pltpu

F32 = jnp.float32
BF16 = jnp.bfloat16
I32 = jnp.int32
I16 = jnp.int16
U32 = jnp.uint32

EPS = 1e-6
CHUNK = 64
D_MODEL = 2048

DSA_HEADS = 8
DSA_HEAD_DIM = 128
DSA_LATENT = 256
IDX_HEADS = 8
IDX_DIM = 64
TOPK_MAX = 256

ML_HEADS = 4
ML_QK_DIM = 128
ML_V_DIM = 256
CONV_W = 4

X_HEADS = 4
X_HEAD_DIM = D_MODEL // X_HEADS

N_GROUPS = 4
EXP_PER_GROUP = 8
N_EXPERTS = N_GROUPS * EXP_PER_GROUP
TOPK_IN_GROUP = 2
D_EXPERT = 512

_O_DQ = 0
_O_CKV = _O_DQ + DSA_HEADS * DSA_HEAD_DIM
_O_QI = _O_CKV + DSA_LATENT
_O_KI = _O_QI + IDX_HEADS * IDX_DIM
_O_WI = _O_KI + IDX_DIM
_O_MQ = _O_WI + IDX_HEADS
_O_MK = _O_MQ + ML_HEADS * ML_QK_DIM
_O_MV = _O_MK + ML_HEADS * ML_QK_DIM
_O_MI = _O_MV + ML_HEADS * ML_V_DIM
_O_MF = _O_MI + ML_HEADS
_O_MO = _O_MF + ML_HEADS
_O_END = _O_MO + ML_HEADS * ML_V_DIM

_G_DQ = (0, 1024)
_G_CKV = (1024, 1280)
_G_QI = (1280, 1792)
_G_SMALL = (1792, 1920)
_G_MQK = (1920, 2944)
_G_MV = (2944, 3968)
_G_MO = (3968, 4992)
_W_COLS = 4992
_S_WI = IDX_DIM
_S_MI = _S_WI + IDX_HEADS
_S_MF = _S_MI + ML_HEADS
_SMALL = 128

_VMEM_LIMIT = 56 * 1024 * 1024
_INT_MIN = -(2 ** 31)
_I16_MIN = -(2 ** 15)
_CHUNK_SHIFT = CHUNK.bit_length() - 1
_LOG2E = 1.4426950408889634
_SUBLANES = 8
_LANES = 128
_GATHER_SLOTS = 9
_GATHER_AHEAD = _GATHER_SLOTS - 1
_RANK_SEG = 512
_NEG = -1e30


def _rms(v, g):
    return v * lax.rsqrt(jnp.mean(v * v, axis=-1, keepdims=True) + EPS) * g


def _dot(a, b):
    return jnp.dot(a, b, preferred_element_type=F32)


def _dot_nt(a, b):
    return lax.dot_general(a, b, (((1,), (1,)), ((), ())), preferred_element_type=F32)


def _resident(shape):
    nd = len(shape)
    return pl.BlockSpec(shape, lambda *_: (0,) * nd, pipeline_mode=pl.Buffered(1))


def _params(n_axes=1):
    return pltpu.CompilerParams(dimension_semantics=("arbitrary",) * n_axes,
                                vmem_limit_bytes=_VMEM_LIMIT)


def _wprep_body(w_hbm, out_ref, wbuf, sem, *, tk):
    i = pl.program_id(0)
    slot = i % 2

    def rows(blk, s):
        return pltpu.make_async_copy(w_hbm.at[pl.ds(pl.multiple_of(blk * tk, tk), tk)], wbuf.at[s], sem.at[s])

    @pl.when(i == 0)
    def _():
        rows(0, 0).start()

    @pl.when(i + 1 < pl.num_programs(0))
    def _():
        rows(i + 1, 1 - slot).start()

    rows(i, slot).wait()
    w_ref = wbuf.at[slot]
    n_small = IDX_DIM + IDX_HEADS
    out_ref[:, _G_DQ[0]:_G_SMALL[0] + n_small] = w_ref[:, _O_DQ:_O_MQ].astype(BF16)
    out_ref[:, _G_SMALL[0] + n_small:_G_SMALL[0] + n_small + 2 * ML_HEADS] = w_ref[:, _O_MI:_O_MO].astype(BF16)
    out_ref[:, _G_SMALL[0] + n_small + 2 * ML_HEADS:_G_SMALL[1]] = jnp.zeros(
        (out_ref.shape[0], _SMALL - n_small - 2 * ML_HEADS), BF16)
    out_ref[:, _G_MQK[0]:_G_MV[1]] = w_ref[:, _O_MQ:_O_MI].astype(BF16)
    out_ref[:, _G_MO[0]:_G_MO[1]] = w_ref[:, _O_MO:_O_END].astype(BF16)


def _wprep(w_in, tk):
    K = w_in.shape[0]
    return pl.pallas_call(
        functools.partial(_wprep_body, tk=tk),
        grid=(K // tk,),
        in_specs=[pl.BlockSpec(memory_space=pl.ANY)],
        out_specs=pl.BlockSpec((tk, _W_COLS), lambda i: (i, 0)),
        out_shape=jax.ShapeDtypeStruct((K, _W_COLS), BF16),
        scratch_shapes=[pltpu.VMEM((2, tk, w_in.shape[1]), F32), pltpu.SemaphoreType.DMA((2,))],
        compiler_params=_params(),
        name="wprep",
    )(w_in)


def _inproj_body(x_ref, g_ref, w_ref, wuk_ref, kvg_ref, kig_ref,
                 qabs_ref, ckv_ref, qi_ref, kidx_ref, small_ref, mqk_ref, mv_ref, mo_ref):
    h = _rms(x_ref[...], g_ref[...]).astype(BF16)

    def proj(grp):
        return _dot(h, w_ref[:, grp[0]:grp[1]])

    dq = proj(_G_DQ)
    for hd in range(DSA_HEADS):
        qh = dq[:, hd * DSA_HEAD_DIM:(hd + 1) * DSA_HEAD_DIM].astype(BF16)
        qa = _dot(qh, wuk_ref[hd]) * (DSA_HEAD_DIM ** -0.5 * _LOG2E)
        qabs_ref[:, hd * DSA_LATENT:(hd + 1) * DSA_LATENT] = qa.astype(BF16)
    ckv_ref[...] = _rms(proj(_G_CKV), kvg_ref[...]).astype(BF16)
    qi_ref[...] = (proj(_G_QI) * (IDX_DIM ** -0.5)).astype(BF16)
    small = proj(_G_SMALL)
    small_ref[...] = small
    kidx_ref[...] = _rms(small[:, :IDX_DIM], kig_ref[...]).astype(BF16)
    mqk_ref[...] = proj(_G_MQK)
    mv_ref[...] = proj(_G_MV).astype(BF16)
    mo_ref[...] = proj(_G_MO)


def _inproj(x, g, w, wuk, kvg, kig, tm):
    T = x.shape[0]
    row = lambda n: pl.BlockSpec((tm, n), lambda i: (i, 0))
    outs = [(8 * DSA_LATENT, BF16), (DSA_LATENT, BF16), (IDX_HEADS * IDX_DIM, BF16), (IDX_DIM, BF16),
            (_SMALL, F32), (2 * ML_HEADS * ML_QK_DIM, F32), (ML_HEADS * ML_V_DIM, BF16),
            (ML_HEADS * ML_V_DIM, F32)]
    return pl.pallas_call(
        _inproj_body,
        grid=(T // tm,),
        in_specs=[row(D_MODEL), _resident(g.shape), _resident(w.shape), _resident(wuk.shape),
                  _resident(kvg.shape), _resident(kig.shape)],
        out_specs=[row(n) for n, _ in outs],
        out_shape=[jax.ShapeDtypeStruct((T, n), dt) for n, dt in outs],
        compiler_params=_params(),
        name="inproj",
    )(x, g, w, wuk, kvg, kig)


def _sublane_fold(v, op, rows=_SUBLANES, ways=4):
    groups = [v[r * rows:(r + 1) * rows, :] for r in range(v.shape[0] // rows)]
    accs = groups[:ways]
    for r in range(ways, len(groups)):
        accs[r % ways] = op(accs[r % ways], groups[r])
    while len(accs) > 1:
        accs = [op(accs[k], accs[k + 1]) if k + 1 < len(accs) else accs[k] for k in range(0, len(accs), 2)]
    return accs[0]


def _bit_transpose32(words):
    a = list(words)
    j, m = 16, 0x0000FFFF
    while j:
        k = 0
        while k < 32:
            t = (a[k] ^ (a[k + j] >> j)) & jnp.uint32(m)
            a[k] = a[k] ^ t
            a[k + j] = a[k + j] ^ (t << j)
            k = (k + j + 1) & ~j
        j >>= 1
        m = (m ^ (m << j)) & 0xFFFFFFFF
    return a


def _dsa_body(qi_ref, wrow_ref, qabs_ref, kidx_ref, ckv_ref, ckvt_ref, wuv_ref, out_ref,
              key_ref, planes_ref, alive_ref, lga_ref, lgb_ref, m_ref, l_ref, acc_ref, *, tq, tk, topk, nbits_idx):
    i = pl.program_id(0)
    n_kb = ((i + 1) * tq + tk - 1) // tk
    w_rows = wrow_ref[0:IDX_HEADS, :] * (IDX_HEADS ** -0.5)
    q_chunk = (i * tq + lax.broadcasted_iota(I32, (1, tq), 1)) >> _CHUNK_SHIFT

    def key_pos(j):
        return j * tk + lax.broadcasted_iota(I32, (tk, 1), 0)

    def score_block(j, carry):
        kx = kidx_ref[pl.ds(pl.multiple_of(j * tk, tk), tk), :]
        s = jnp.zeros((tk, tq), F32)
        for hd in range(IDX_HEADS):
            d = _dot_nt(kx, qi_ref[:, hd * IDX_DIM:(hd + 1) * IDX_DIM])
            s = s + w_rows[hd:hd + 1, :] * jnp.maximum(d, 0.0)
        bits = lax.bitcast_convert_type(s, I32)
        key = bits ^ ((bits >> 31) & 0x7FFFFFFF)
        key_ref[j] = jnp.where((key_pos(j) >> _CHUNK_SHIFT) <= q_chunk, key, _INT_MIN)
        return carry

    lax.fori_loop(0, n_kb, score_block, 0)

    def count(pred):
        def body(j, acc):
            hit = pred(key_ref[j], key_pos(j)).astype(I32)
            return acc + _sublane_fold(hit, jnp.add)
        acc = lax.fori_loop(0, n_kb, body, jnp.zeros((8, tq), I32))
        return jnp.sum(acc, axis=0, keepdims=True)

    n_pair = (n_kb + 1) // 2
    plane_rows = tk // 32

    @pl.when(i == 0)
    def _():
        planes_ref[...] = jnp.zeros(planes_ref.shape, U32)

    def slice_block(j, carry):
        u = lax.bitcast_convert_type(key_ref[j], U32) ^ jnp.uint32(0x80000000)
        for h in range(plane_rows // _SUBLANES):
            base = h * 32 * _SUBLANES
            planes = _bit_transpose32([u[base + _SUBLANES * v:base + _SUBLANES * (v + 1), :] for v in range(32)])
            row0 = pl.multiple_of(j * plane_rows + h * _SUBLANES, _SUBLANES)
            for b in range(32):
                planes_ref[b, pl.ds(row0, _SUBLANES), :] = planes[31 - b]
        return carry

    lax.fori_loop(0, n_kb, slice_block, 0)
    word_row = lax.broadcasted_iota(I32, (alive_ref.shape[0], 1), 0)
    alive_ref[...] = jnp.where(word_row < n_kb * plane_rows, jnp.full(alive_ref.shape, 0xFFFFFFFF, U32),
                               jnp.uint32(0))

    def select_bit(it, carry):
        need, t_u = carry
        b = 31 - it
        alive = alive_ref[...]
        ones = alive & planes_ref[b]
        n_ones = jnp.sum(_sublane_fold(lax.population_count(ones).astype(I32), jnp.add), axis=0, keepdims=True)
        take = n_ones >= need
        alive_ref[...] = jnp.where(take, ones, alive ^ ones)
        bit = lax.shift_left(jnp.uint32(1), jnp.asarray(b, U32))
        return jnp.where(take, need, need - n_ones), jnp.where(take, t_u | bit, t_u)

    need, t_u = lax.fori_loop(0, 32, select_bit, (jnp.full((1, tq), topk, I32), jnp.zeros((1, tq), U32)))
    n_equal = jnp.sum(_sublane_fold(lax.population_count(alive_ref[...]).astype(I32), jnp.add), axis=0,
                      keepdims=True)
    short = t_u == 0
    t = jnp.maximum(lax.bitcast_convert_type(t_u ^ jnp.uint32(0x80000000), I32), _INT_MIN + 1)
    all_pos = jnp.int32(2 ** nbits_idx - 1)
    has_ties = (n_equal > need) & jnp.logical_not(short)
    n_tie_take = jnp.where(has_ties, need, all_pos)

    def tie_cutoff():
        def pos_bit(b, c):
            cand = c + lax.shift_left(jnp.int32(1), nbits_idx - 1 - b)
            f = count(lambda kb, pos: (kb == t) & (pos < cand))
            return jnp.where(f <= n_tie_take, cand, c)
        return lax.fori_loop(0, nbits_idx, pos_bit, jnp.zeros((1, tq), I32))

    cut = lax.cond(jnp.max(has_ties.astype(I32)) > 0, tie_cutoff, lambda: jnp.full((1, tq), all_pos, I32))

    def masked_block(j, carry):
        key_ref[j] = jnp.full((tk, tq), _INT_MIN, I32)
        return carry

    lax.fori_loop(n_kb, 2 * n_pair, masked_block, 0)

    m_ref[...] = jnp.full(m_ref.shape, _NEG, F32)
    l_ref[...] = jnp.zeros(l_ref.shape, F32)
    acc_ref[...] = jnp.zeros(acc_ref.shape, F32)
    last_blk = ckvt_ref.shape[0] - 1

    def logits(j, lg_buf):
        c_blk = ckv_ref[pl.ds(pl.multiple_of(jnp.minimum(j, last_blk) * tk, tk), tk), :]
        kb = key_ref[j]
        sel = (kb > t) | ((kb == t) & (key_pos(j) < cut))
        bias = jnp.where(sel, 0.0, _NEG).astype(F32)
        for hd in range(DSA_HEADS):
            lg_buf[hd] = _dot_nt(c_blk, qabs_ref[:, hd * DSA_LATENT:(hd + 1) * DSA_LATENT]) + bias

    def accumulate(j, lg_buf):
        c_blk_t = ckvt_ref[jnp.minimum(j, last_blk)]
        for hd in range(DSA_HEADS):
            lg = lg_buf[hd]
            m_old = m_ref[hd:hd + 1, :]
            m_new = jnp.maximum(m_old, jnp.max(_sublane_fold(lg, jnp.maximum, ways=1), axis=0, keepdims=True))
            p = jnp.exp2(lg - m_new)
            alpha = jnp.exp2(m_old - m_new)
            l_ref[hd:hd + 1, :] = alpha * l_ref[hd:hd + 1, :] + jnp.sum(_sublane_fold(p, jnp.add, ways=1), axis=0,
                                                                         keepdims=True)
            acc_ref[hd] = alpha * acc_ref[hd] + _dot(c_blk_t, p.astype(BF16))
            m_ref[hd:hd + 1, :] = m_new

    logits(0, lga_ref)

    def attn_pair(mi, carry):
        ja = 2 * mi
        accumulate(ja, lga_ref)
        logits(ja + 1, lgb_ref)
        accumulate(ja + 1, lgb_ref)
        logits(jnp.minimum(ja + 2, 2 * n_pair - 1), lga_ref)
        return carry

    lax.fori_loop(0, n_pair, attn_pair, 0)

    for hd in range(DSA_HEADS):
        o_lat = (acc_ref[hd] / l_ref[hd:hd + 1, :]).T.astype(BF16)
        out_ref[:, hd * DSA_HEAD_DIM:(hd + 1) * DSA_HEAD_DIM] = _dot(o_lat, wuv_ref[hd]).astype(BF16)


def _dsa(qi, wrows, qabs, kidx, ckv, wuv, tq, tk):
    T = qi.shape[0]
    topk = min(TOPK_MAX, T // 4)
    n_kb = T // tk
    ckvt = jnp.transpose(ckv.reshape(n_kb, tk, DSA_LATENT), (0, 2, 1))
    row = lambda n: pl.BlockSpec((tq, n), lambda i: (i, 0))
    body = functools.partial(_dsa_body, tq=tq, tk=tk, topk=topk, nbits_idx=int(T).bit_length())
    return pl.pallas_call(
        body,
        grid=(T // tq,),
        in_specs=[row(qi.shape[1]), pl.BlockSpec((wrows.shape[0], tq), lambda i: (0, i)), row(qabs.shape[1]),
                  _resident(kidx.shape), _resident(ckv.shape), _resident(ckvt.shape), _resident(wuv.shape)],
        out_specs=row(DSA_HEADS * DSA_HEAD_DIM),
        out_shape=jax.ShapeDtypeStruct((T, DSA_HEADS * DSA_HEAD_DIM), BF16),
        scratch_shapes=[pltpu.VMEM((n_kb + n_kb % 2, tk, tq), I32),
                        pltpu.VMEM((32, T // 32, tq), U32), pltpu.VMEM((T // 32, tq), U32),
                        pltpu.VMEM((DSA_HEADS, tk, tq), F32), pltpu.VMEM((DSA_HEADS, tk, tq), F32),
                        pltpu.VMEM((DSA_HEADS, tq), F32),
                        pltpu.VMEM((DSA_HEADS, tq), F32), pltpu.VMEM((DSA_HEADS, DSA_LATENT, tq), F32)],
        compiler_params=_params(),
        name="dsa",
    )(qi, wrows, qabs, kidx, ckv, ckvt, wuv)


def _log_sigmoid(v):
    return jnp.minimum(v, 0.0) - jnp.log1p(jnp.exp(-jnp.abs(v)))


def _chunk_cumsum(v, axis):
    pos = lax.broadcasted_iota(I32, v.shape, axis) & (CHUNK - 1)
    d = 1
    while d < CHUNK:
        v = v + jnp.where(pos >= d, pltpu.roll(v, d, axis=axis), 0.0)
        d *= 2
    return v


def _mlstm_body(mqk_ref, mv_ref, small_ref, gt_ref, mo_ref, cw_ref, cb_ref, gbc_ref, gbr_ref, ng_ref,
                out_ref, xe_ref, c_ref, n_ref, m_ref, hs_ref, *, rows):
    @pl.when(pl.program_id(0) == 0)
    def _():
        xe_ref[0:8, :] = jnp.zeros((8, xe_ref.shape[1]), F32)
        c_ref[...] = jnp.zeros(c_ref.shape, F32)
        n_ref[...] = jnp.zeros(n_ref.shape, F32)
        m_ref[...] = jnp.zeros(m_ref.shape, F32)

    x = mqk_ref[...]
    xe_ref[8:8 + rows, :] = x
    y = cb_ref[...]
    for j in range(CONV_W - 1):
        y = y + xe_ref[5 + j:5 + j + rows, :] * cw_ref[j:j + 1, :]
    y = y + x * cw_ref[CONV_W - 1:CONV_W, :]
    xe_ref[0:8, :] = x[rows - 8:rows, :]
    qk = y * jax.nn.sigmoid(y)
    nqk = ML_HEADS * ML_QK_DIM
    q_all = (qk[:, :nqk] * (ML_QK_DIM ** -0.5)).astype(BF16)
    k_all = qk[:, nqk:]

    g_col = small_ref[...] + gbc_ref[...]
    g_row = gt_ref[...] + gbr_ref[...]
    b_col = _chunk_cumsum(_log_sigmoid(g_col), 0)
    b_row = _chunk_cumsum(_log_sigmoid(g_row), 1)

    tri = lax.broadcasted_iota(I32, (CHUNK, CHUNK), 1) <= lax.broadcasted_iota(I32, (CHUNK, CHUNK), 0)

    for c in range(rows // CHUNK):
        lo, hi = c * CHUNK, (c + 1) * CHUNK
        for hd in range(ML_HEADS):
            bc = b_col[lo:hi, _S_MF + hd:_S_MF + hd + 1]
            lic = g_col[lo:hi, _S_MI + hd:_S_MI + hd + 1]
            br = b_row[ML_HEADS + hd:ML_HEADS + hd + 1, lo:hi]
            lir = g_row[hd:hd + 1, lo:hi]
            g_tot = bc[CHUNK - 1:CHUNK, :]
            m_prev = m_ref[hd][:, 0:1]

            dmat = jnp.where(tri, bc - br + lir, -jnp.inf)
            inter = bc + m_prev
            m_t = jnp.maximum(inter, jnp.max(dmat, axis=-1, keepdims=True))
            w_intra = jnp.exp(dmat - m_t)
            a_inter = jnp.exp(inter - m_t)

            qh = q_all[lo:hi, hd * ML_QK_DIM:(hd + 1) * ML_QK_DIM]
            kh = k_all[lo:hi, hd * ML_QK_DIM:(hd + 1) * ML_QK_DIM]
            vh = mv_ref[lo:hi, hd * ML_V_DIM:(hd + 1) * ML_V_DIM]
            s_qk = _dot_nt(qh, kh.astype(BF16)) * w_intra
            c_prev = c_ref[hd]
            n_prev = n_ref[hd]
            num = a_inter * _dot(qh, c_prev.astype(BF16)) + _dot(s_qk.astype(BF16), vh)
            den = (a_inter * jnp.sum(qh.astype(F32) * n_prev, axis=-1, keepdims=True)
                   + jnp.sum(s_qk, axis=-1, keepdims=True))
            hs_ref[lo:hi, hd * ML_V_DIM:(hd + 1) * ML_V_DIM] = (
                num / jnp.maximum(jnp.abs(den), jnp.exp(-m_t)))

            m_new = jnp.maximum(g_tot + m_prev, jnp.max(g_tot - br + lir, axis=-1, keepdims=True))
            a_state = jnp.exp(g_tot + m_prev - m_new)
            wk = jnp.exp(g_tot - bc + lic - m_new) * kh
            c_ref[hd] = a_state * c_prev + _dot(wk.T.astype(BF16), vh)
            n_ref[hd] = a_state * n_prev + jnp.sum(wk, axis=0, keepdims=True)
            m_ref[hd] = jnp.broadcast_to(m_new, m_ref.shape[1:])

    for hd in range(ML_HEADS):
        sl = slice(hd * ML_V_DIM, (hd + 1) * ML_V_DIM)
        out_ref[:, sl] = (_rms(hs_ref[:, sl], ng_ref[:, sl]) * jax.nn.sigmoid(mo_ref[:, sl])).astype(BF16)


def _mlstm(mqk, mv, small, gt, mo, cw, cb, gbc, gbr, ng, rows):
    T = mqk.shape[0]
    row = lambda n: pl.BlockSpec((rows, n), lambda i: (i, 0))
    nv = ML_HEADS * ML_V_DIM
    return pl.pallas_call(
        functools.partial(_mlstm_body, rows=rows),
        grid=(T // rows,),
        in_specs=[row(mqk.shape[1]), row(nv), row(_SMALL), pl.BlockSpec((8, rows), lambda i: (1, i)), row(nv),
                  _resident(cw.shape), _resident(cb.shape), _resident(gbc.shape), _resident(gbr.shape),
                  _resident(ng.shape)],
        out_specs=row(nv),
        out_shape=jax.ShapeDtypeStruct((T, nv), BF16),
        scratch_shapes=[pltpu.VMEM((rows + 8, mqk.shape[1]), F32),
                        pltpu.VMEM((ML_HEADS, ML_QK_DIM, ML_V_DIM), F32),
                        pltpu.VMEM((ML_HEADS, 1, ML_QK_DIM), F32),
                        pltpu.VMEM((ML_HEADS, 1, 128), F32),
                        pltpu.VMEM((rows, nv), F32)],
        compiler_params=_params(),
        name="mlstm",
    )(mqk, mv, small, gt, mo, cw, cb, gbc, gbr, ng)


def _memfold_body(mem_ref, g_ref, wk_ref, wv_ref, wq_ref, wc_ref, wqk_ref, wvo_ref):
    mn = _rms(mem_ref[...], g_ref[...]).astype(BF16)
    k = _dot(mn, wk_ref[...].astype(BF16)).astype(BF16)
    v = _dot(mn, wv_ref[...].astype(BF16)).astype(BF16)
    wqk_ref[...] = (_dot_nt(wq_ref[...].astype(BF16), k) * (X_HEAD_DIM ** -0.5)).astype(BF16)
    wvo_ref[...] = _dot(v, wc_ref[...].astype(BF16)).astype(BF16)


def _memfold(mem, g, w_ckv, w_cq, w_co):
    M, D = mem.shape
    dh = X_HEAD_DIM
    return pl.pallas_call(
        _memfold_body,
        grid=(X_HEADS,),
        in_specs=[_resident(mem.shape), _resident(g.shape),
                  pl.BlockSpec((D, dh), lambda h: (0, h)), pl.BlockSpec((D, dh), lambda h: (0, X_HEADS + h)),
                  pl.BlockSpec((D, dh), lambda h: (0, h)), pl.BlockSpec((dh, D), lambda h: (h, 0))],
        out_specs=[pl.BlockSpec((D, M), lambda h: (0, h)), pl.BlockSpec((M, D), lambda h: (h, 0))],
        out_shape=[jax.ShapeDtypeStruct((D, X_HEADS * M), BF16), jax.ShapeDtypeStruct((X_HEADS * M, D), BF16)],
        compiler_params=_params(),
        name="memfold",
    )(mem, g, w_ckv, w_ckv, w_cq, w_co)


def _mixout_body(x_ref, dsa_ref, ml_ref, wo_ref, wqk_ref, wvo_ref, wr_ref, gx_ref, gf_ref,
                 x2_ref, hf_ref, rl_ref, p_ref):
    nd = dsa_ref.shape[1]
    x1 = x_ref[...] + _dot(dsa_ref[...], wo_ref[0:nd, :]) + _dot(ml_ref[...], wo_ref[nd:, :])
    lg_all = _dot(_rms(x1, gx_ref[...]).astype(BF16), wqk_ref[...])
    n_mem = wqk_ref.shape[1] // X_HEADS
    for hd in range(X_HEADS):
        sl = slice(hd * n_mem, (hd + 1) * n_mem)
        lg = lg_all[:, sl]
        e = jnp.exp(lg - jnp.max(lg, axis=-1, keepdims=True))
        p_ref[:, sl] = (e / jnp.sum(e, axis=-1, keepdims=True)).astype(BF16)
    x2 = x1 + _dot(p_ref[...], wvo_ref[...])
    x2_ref[...] = x2
    hf = _rms(x2, gf_ref[...]).astype(BF16)
    rl_ref[...] = _dot(hf, wr_ref[...])
    bits = lax.bitcast_convert_type(hf.astype(F32), U32)
    for c in range(_SUBLANES):
        lo = bits[:, (2 * c) * _LANES:(2 * c + 1) * _LANES]
        hi = bits[:, (2 * c + 1) * _LANES:(2 * c + 2) * _LANES]
        hf_ref[pl.ds(c, x2.shape[0], stride=_SUBLANES), :] = (hi & jnp.uint32(0xFFFF0000)) | (lo >> 16)


def _mixout(x, dsa, ml, wo, wqk, wvo, wr, gx, gf, tm):
    T = x.shape[0]
    row = lambda n: pl.BlockSpec((tm, n), lambda i: (i, 0))
    return pl.pallas_call(
        _mixout_body,
        grid=(T // tm,),
        in_specs=[row(D_MODEL), row(dsa.shape[1]), row(ml.shape[1]), _resident(wo.shape),
                  _resident(wqk.shape), _resident(wvo.shape), _resident(wr.shape), _resident(gx.shape),
                  _resident(gf.shape)],
        out_specs=[row(D_MODEL), pl.BlockSpec((tm * _SUBLANES, _LANES), lambda i: (i, 0)), row(wr.shape[1])],
        out_shape=[jax.ShapeDtypeStruct((T, D_MODEL), F32),
                   jax.ShapeDtypeStruct((T * _SUBLANES, _LANES), U32),
                   jax.ShapeDtypeStruct((T, wr.shape[1]), F32)],
        scratch_shapes=[pltpu.VMEM((tm, wqk.shape[1]), BF16)],
        compiler_params=_params(),
        name="mixout",
    )(x, dsa, ml, wo, wqk, wvo, wr, gx, gf)


def _moe_body(tok_ref, eb_ref, hf_hbm, wg_ref, wu_ref, wd_ref, y_hbm,
              xbuf, ybuf, wgb, wub, wdb, gsem, ysem, *, bm, n_blk):
    e = pl.program_id(0)
    n_valid = eb_ref[N_EXPERTS]
    b_lo = eb_ref[e]
    b_hi = eb_ref[e + 1]

    def tok_words(tok):
        return hf_hbm.at[pl.ds(pl.multiple_of(tok * _SUBLANES, _SUBLANES), _SUBLANES)]

    def gather(blk, s):
        def issue(g, carry):
            for u in range(_SUBLANES):
                r = g * _SUBLANES + u
                pltpu.make_async_copy(tok_words(tok_ref[blk * bm + r]),
                                      xbuf.at[s, pl.ds(pl.multiple_of(r * _SUBLANES, _SUBLANES), _SUBLANES)],
                                      gsem.at[s]).start(priority=1)
            return carry
        lax.fori_loop(0, bm // _SUBLANES, issue, 0)

    def wait_gather(s):
        pltpu.make_async_copy(xbuf.at[s], xbuf.at[s], gsem.at[s]).wait()

    def y_copy(blk, s):
        return pltpu.make_async_copy(ybuf.at[s], y_hbm.at[pl.ds(pl.multiple_of(blk * bm, bm), bm)], ysem.at[s])

    @pl.when(e == 0)
    def _():
        for k in range(_GATHER_AHEAD):
            gather(k, k)

    @pl.when(b_hi > b_lo)
    def _():
        wgb[...] = wg_ref[0].astype(BF16)
        wub[...] = wu_ref[0].astype(BF16)
        wdb[...] = wd_ref[0].astype(BF16)

        def block(b, carry):
            s = b % _GATHER_SLOTS
            wait_gather(s)
            chunks = []
            for c in range(_SUBLANES):
                w = xbuf[s, pl.ds(c, bm, stride=_SUBLANES), :]
                chunks.append(lax.bitcast_convert_type(w << 16, F32).astype(BF16))
                chunks.append(lax.bitcast_convert_type(w & jnp.uint32(0xFFFF0000), F32).astype(BF16))
            xb = jnp.concatenate(chunks, axis=-1)
            gate = _dot(xb, wgb[...])
            a = gate * jax.nn.sigmoid(gate) * _dot(xb, wub[...])
            y = _dot(a.astype(BF16), wdb[...])

            nxt = jnp.minimum(b + _GATHER_AHEAD, n_blk - 1)
            for r in range(bm):
                pltpu.make_async_copy(tok_words(tok_ref[nxt * bm + r]),
                                      xbuf.at[(b + _GATHER_AHEAD) % _GATHER_SLOTS, pl.ds(r * _SUBLANES, _SUBLANES)],
                                      gsem.at[(b + _GATHER_AHEAD) % _GATHER_SLOTS]).start(priority=1)

            @pl.when(b >= 2)
            def _():
                y_copy(b - 2, b % 2).wait()

            ybuf[b % 2] = y
            y_copy(b, b % 2).start()
            return carry

        lax.fori_loop(b_lo, b_hi, block, 0)

    @pl.when(e == pl.num_programs(0) - 1)
    def _():
        for k in range(_GATHER_AHEAD):
            wait_gather((n_valid + k) % _GATHER_SLOTS)

        @pl.when(n_valid >= 2)
        def _():
            y_copy(n_valid - 2, n_valid % 2).wait()
        y_copy(n_valid - 1, (n_valid - 1) % 2).wait()
        ybuf[0] = jnp.zeros(ybuf.shape[1:], F32)

        def zero_block(b, carry):
            cp = y_copy(b, 0)
            cp.start()
            cp.wait()
            return carry

        lax.fori_loop(n_valid, n_blk, zero_block, 0)


def _moe(tok, e_blk, hf, wg, wu, wd, n_blk, bm):
    D = wg.shape[1]
    wspec = lambda shape: pl.BlockSpec((1,) + shape, lambda e, *_: (e, 0, 0))
    grid_spec = pltpu.PrefetchScalarGridSpec(
        num_scalar_prefetch=2,
        grid=(N_EXPERTS,),
        in_specs=[pl.BlockSpec(memory_space=pl.ANY),
                  wspec((D, D_EXPERT)), wspec((D, D_EXPERT)), wspec((D_EXPERT, D))],
        out_specs=pl.BlockSpec(memory_space=pl.ANY),
        scratch_shapes=[pltpu.VMEM((_GATHER_SLOTS, bm * _SUBLANES, _LANES), U32),
                        pltpu.VMEM((2, bm, D), F32),
                        pltpu.VMEM((D, D_EXPERT), BF16), pltpu.VMEM((D, D_EXPERT), BF16),
                        pltpu.VMEM((D_EXPERT, D), BF16),
                        pltpu.SemaphoreType.DMA((_GATHER_SLOTS,)), pltpu.SemaphoreType.DMA((2,))],
    )
    return pl.pallas_call(
        functools.partial(_moe_body, bm=bm, n_blk=n_blk),
        grid_spec=grid_spec,
        out_shape=jax.ShapeDtypeStruct((n_blk * bm, D), F32),
        compiler_params=_params(),
        name="moe",
    )(tok, e_blk, hf, wg, wu, wd)


def _route(rl, b_group, b_router, bm):
    N = rl.shape[0]
    g_logits = rl[:, :N_GROUPS] + b_group
    g_prob = jax.nn.softmax(g_logits, axis=-1)
    g_sel = jnp.argmax(g_logits, axis=-1)
    p_g = jnp.take_along_axis(g_prob, g_sel[:, None], axis=-1)
    e_logits = rl[:, N_GROUPS:N_GROUPS + N_EXPERTS] + b_router
    in_group = (jnp.arange(N_EXPERTS, dtype=I32) // EXP_PER_GROUP)[None, :] == g_sel[:, None]
    e_prob = jax.nn.softmax(jnp.where(in_group, e_logits, -jnp.inf), axis=-1)
    top_p, expert_id = lax.top_k(jnp.where(in_group, e_prob, -1.0), TOPK_IN_GROUP)
    gates = p_g * top_p / top_p.sum(-1, keepdims=True)
    expert_id = expert_id.astype(I32)

    A = N * TOPK_IN_GROUP
    flat_e = expert_id.reshape(A)
    onehot = (flat_e[:, None] == jnp.arange(N_EXPERTS, dtype=I32)[None, :])
    seg = _RANK_SEG if A % _RANK_SEG == 0 else A
    oh = onehot.astype(BF16).reshape(A // seg, seg, N_EXPERTS)
    before = (jnp.arange(seg)[None, :] < jnp.arange(seg)[:, None]).astype(BF16)
    within = jnp.einsum('ij,bjk->bik', before, oh, preferred_element_type=F32)
    seg_tot = jnp.sum(oh.astype(F32), axis=1)
    seg_base = jnp.cumsum(seg_tot, axis=0) - seg_tot
    rank = jnp.sum((within + seg_base[:, None, :]) * oh.astype(F32), axis=-1).reshape(A).astype(I32)
    counts = jnp.sum(seg_tot, axis=0).astype(I32)
    padded = (counts + bm - 1) // bm * bm
    pad_ends = jnp.cumsum(padded)
    pad_starts = pad_ends - padded
    row = jnp.sum(jnp.where(onehot, pad_starts[None, :], 0), axis=1) + rank
    n_blk = -(-A // bm) + N_EXPERTS
    row_tok = jnp.zeros((n_blk * bm,), I32).at[row].set(jnp.arange(A, dtype=I32) // TOPK_IN_GROUP)
    e_blk = jnp.concatenate([pad_starts, pad_ends[-1:]]) // bm
    return row_tok, row, gates, e_blk.astype(I32), n_blk


def _final_body(row_ref, x_ref, gate_ref, y_hbm, g_ref, out_ref, ybuf, sem, *, tm):
    i = pl.program_id(0)
    slot = i % 2

    def gather(tile, s):
        def issue(g, carry):
            for u in range(_SUBLANES):
                for k in range(TOPK_IN_GROUP):
                    src = row_ref[(tile * tm + g * _SUBLANES + u) * TOPK_IN_GROUP + k]
                    pltpu.make_async_copy(y_hbm.at[pl.ds(src, 1)], ybuf.at[s, k, g, pl.ds(u, 1)],
                                          sem.at[s]).start()
            return carry
        lax.fori_loop(0, tm // _SUBLANES, issue, 0)

    @pl.when(i == 0)
    def _():
        gather(0, 0)

    @pl.when(i + 1 < pl.num_programs(0))
    def _():
        gather(i + 1, 1 - slot)

    pltpu.make_async_copy(ybuf.at[slot], ybuf.at[slot], sem.at[slot]).wait()
    acc = x_ref[...]
    for k in range(TOPK_IN_GROUP):
        acc = acc + gate_ref[:, k:k + 1] * ybuf[slot, k].reshape(tm, ybuf.shape[-1])
    out_ref[...] = _rms(acc, g_ref[...])


def _final(row, x2, gates, y_rows, g, tm):
    T, D = x2.shape
    grid_spec = pltpu.PrefetchScalarGridSpec(
        num_scalar_prefetch=1,
        grid=(T // tm,),
        in_specs=[pl.BlockSpec((tm, D), lambda i, *_: (i, 0)),
                  pl.BlockSpec((tm, TOPK_IN_GROUP), lambda i, *_: (i, 0)),
                  pl.BlockSpec(memory_space=pl.ANY),
                  pl.BlockSpec(g.shape, lambda i, *_: (0, 0))],
        out_specs=pl.BlockSpec((tm, D), lambda i, *_: (i, 0)),
        scratch_shapes=[pltpu.VMEM((2, TOPK_IN_GROUP, tm // _SUBLANES, _SUBLANES, D), F32),
                        pltpu.SemaphoreType.DMA((2,))],
    )
    return pl.pallas_call(
        functools.partial(_final_body, tm=tm),
        grid_spec=grid_spec,
        out_shape=jax.ShapeDtypeStruct((T, D), F32),
        compiler_params=_params(),
        name="final",
    )(row, x2, gates, y_rows, g)


def _tile_sizes(T):
    pick = lambda want: want if T % want == 0 else CHUNK
    return dict(inproj=pick(256), dsa_q=pick(256), dsa_k=pick(512), mlstm=pick(256), mixout=pick(256),
                final=pick(256), moe=128)


def _layer(x, mem, norm_mix_g, w_in, kv_norm_g, k_idx_norm_g, w_uk, w_uv, conv_w, conv_b, gate_b, ml_norm_g,
           w_out, norm_x_g, mem_norm_g, w_cq, w_ckv, w_co, norm_ffn_g, w_group, b_group, w_router, b_router,
           w_gate, w_up, w_down, out_g):
    T = x.shape[0]
    ts = _tile_sizes(T)
    r2 = lambda v: v.reshape(1, -1)

    w_r = _wprep(w_in, 256)
    wuk_t = jnp.transpose(w_uk, (1, 2, 0)).astype(BF16)
    wuv_t = jnp.transpose(w_uv, (1, 0, 2)).astype(BF16)

    qabs, ckv, qi, kidx, small, mqk, mv, mo = _inproj(
        x, r2(norm_mix_g), w_r, wuk_t, r2(kv_norm_g), r2(k_idx_norm_g), ts["inproj"])

    gate_rows = jnp.transpose(small[:, _S_WI:_S_MF + ML_HEADS])
    dsa_out = _dsa(qi, gate_rows, qabs, kidx, ckv, wuv_t, ts["dsa_q"], ts["dsa_k"])

    gb_col = jnp.zeros((1, _SMALL), F32).at[0, _S_MI:_S_MI + 2 * ML_HEADS].set(gate_b)
    ml_out = _mlstm(mqk, mv, small, gate_rows, mo, conv_w, r2(conv_b), gb_col, gate_b.reshape(-1, 1),
                    r2(ml_norm_g), ts["mlstm"])

    wqk, wvo = _memfold(mem, r2(mem_norm_g), w_ckv, w_cq, w_co)
    w_rt = jnp.concatenate([w_group, w_router,
                            jnp.zeros((D_MODEL, 128 - N_GROUPS - N_EXPERTS), w_group.dtype)], axis=1)
    x2, hf, rl = _mixout(x, dsa_out, ml_out, w_out.astype(BF16), wqk, wvo, w_rt.astype(BF16),
                         r2(norm_x_g), r2(norm_ffn_g), ts["mixout"])

    bm = ts["moe"]
    row_tok, row, gates, e_blk, n_blk = _route(rl, b_group, b_router, bm)
    y_rows = _moe(row_tok, e_blk, hf, w_gate, w_up, w_down, n_blk, bm)
    return _final(row, x2, gates, y_rows, r2(out_g), ts["final"])


def kernel(x, mem, norm_mix_g, w_in, kv_norm_g, k_idx_norm_g, w_uk, w_uv, conv_w, conv_b, gate_b, ml_norm_g,
           w_out, norm_x_g, mem_norm_g, w_cq, w_ckv, w_co, norm_ffn_g, w_group, b_group, w_router, b_router,
           w_gate, w_up, w_down, final_norm_g):
    B, T, D = x.shape
    assert B == 1 and D == D_MODEL and norm_mix_g.shape[0] == 1 and T % CHUNK == 0
    out = _layer(x[0], mem[0], norm_mix_g[0], w_in[0], kv_norm_g[0], k_idx_norm_g[0], w_uk[0], w_uv[0],
                 conv_w[0], conv_b[0], gate_b[0], ml_norm_g[0], w_out[0], norm_x_g[0], mem_norm_g[0],
                 w_cq[0], w_ckv[0], w_co[0], norm_ffn_g[0], w_group[0], b_group[0], w_router[0], b_router[0],
                 w_gate[0], w_up[0], w_down[0], final_norm_g)
    return out[None]
```

```python
import functools

import jax
import jax.numpy as jnp
import numpy as np
from jax import lax
from jax.experimental import pallas as pl
from jax.experimental.pallas import tpu as pltpu

F32 = jnp.float32
BF16 = jnp.bfloat16
I32 = jnp.int32
I16 = jnp.int16
U32 = jnp.uint32

EPS = 1e-6
CHUNK = 64
D_MODEL = 2048

DSA_HEADS = 8
DSA_HEAD_DIM = 128
DSA_LATENT = 256
IDX_HEADS = 8
IDX_DIM = 64
TOPK_MAX = 256

ML_HEADS = 4
ML_QK_DIM = 128
ML_V_DIM = 256
CONV_W = 4

X_HEADS = 4
X_HEAD_DIM = D_MODEL // X_HEADS

N_GROUPS = 4
EXP_PER_GROUP = 8
N_EXPERTS = N_GROUPS * EXP_PER_GROUP
TOPK_IN_GROUP = 2
D_EXPERT = 512

_O_DQ = 0
_O_CKV = _O_DQ + DSA_HEADS * DSA_HEAD_DIM
_O_QI = _O_CKV + DSA_LATENT
_O_KI = _O_QI + IDX_HEADS * IDX_DIM
_O_WI = _O_KI + IDX_DIM
_O_MQ = _O_WI + IDX_HEADS
_O_MK = _O_MQ + ML_HEADS * ML_QK_DIM
_O_MV = _O_MK + ML_HEADS * ML_QK_DIM
_O_MI = _O_MV + ML_HEADS * ML_V_DIM
_O_MF = _O_MI + ML_HEADS
_O_MO = _O_MF + ML_HEADS
_O_END = _O_MO + ML_HEADS * ML_V_DIM

_G_DQ = (0, 1024)
_G_CKV = (1024, 1280)
_G_QI = (1280, 1792)
_G_SMALL = (1792, 1920)
_G_MQK = (1920, 2944)
_G_MV = (2944, 3968)
_G_MO = (3968, 4992)
_W_COLS = 4992
_S_WI = IDX_DIM
_S_MI = _S_WI + IDX_HEADS
_S_MF = _S_MI + ML_HEADS
_SMALL = 128

_VMEM_LIMIT = 56 * 1024 * 1024
_INT_MIN = -(2 ** 31)
_I16_MIN = -(2 ** 15)
_CHUNK_SHIFT = CHUNK.bit_length() - 1
_LOG2E = 1.4426950408889634
_SUBLANES = 8
_LANES = 128
_GATHER_SLOTS = 9
_GATHER_AHEAD = _GATHER_SLOTS - 1
_RANK_SEG = 512
_NEG = -1e30


def _rms(v, g):
    return v * lax.rsqrt(jnp.mean(v * v, axis=-1, keepdims=True) + EPS) * g


def _dot(a, b):
    return jnp.dot(a, b, preferred_element_type=F32)


def _dot_nt(a, b):
    return lax.dot_general(a, b, (((1,), (1,)), ((), ())), preferred_element_type=F32)


def _resident(shape):
    nd = len(shape)
    return pl.BlockSpec(shape, lambda *_: (0,) * nd, pipeline_mode=pl.Buffered(1))


def _params(n_axes=1):
    return pltpu.CompilerParams(dimension_semantics=("arbitrary",) * n_axes,
                                vmem_limit_bytes=_VMEM_LIMIT)


def _wprep_body(w_hbm, out_ref, wbuf, sem, *, tk):
    i = pl.program_id(0)
    slot = i % 2

    def rows(blk, s):
        return pltpu.make_async_copy(w_hbm.at[0, pl.ds(pl.multiple_of(blk * tk, tk), tk)], wbuf.at[s], sem.at[s])

    @pl.when(i == 0)
    def _():
        rows(0, 0).start()

    @pl.when(i + 1 < pl.num_programs(0))
    def _():
        rows(i + 1, 1 - slot).start()

    rows(i, slot).wait()
    w_ref = wbuf.at[slot]
    n_small = IDX_DIM + IDX_HEADS
    out_ref[:, _G_DQ[0]:_G_SMALL[0] + n_small] = w_ref[:, _O_DQ:_O_MQ].astype(BF16)
    out_ref[:, _G_SMALL[0] + n_small:_G_SMALL[0] + n_small + 2 * ML_HEADS] = w_ref[:, _O_MI:_O_MO].astype(BF16)
    out_ref[:, _G_SMALL[0] + n_small + 2 * ML_HEADS:_G_SMALL[1]] = jnp.zeros(
        (out_ref.shape[0], _SMALL - n_small - 2 * ML_HEADS), BF16)
    out_ref[:, _G_MQK[0]:_G_MV[1]] = w_ref[:, _O_MQ:_O_MI].astype(BF16)
    out_ref[:, _G_MO[0]:_G_MO[1]] = w_ref[:, _O_MO:_O_END].astype(BF16)


def _wprep(w_in, tk):
    K = w_in.shape[1]
    return pl.pallas_call(
        functools.partial(_wprep_body, tk=tk),
        grid=(K // tk,),
        in_specs=[pl.BlockSpec(memory_space=pl.ANY)],
        out_specs=pl.BlockSpec((tk, _W_COLS), lambda i: (i, 0)),
        out_shape=jax.ShapeDtypeStruct((K, _W_COLS), BF16),
        scratch_shapes=[pltpu.VMEM((2, tk, w_in.shape[2]), F32), pltpu.SemaphoreType.DMA((2,))],
        compiler_params=_params(),
        name="wprep",
    )(w_in)


def _inproj_body(x_ref, g_ref, w_ref, wuk_ref, kvg_ref, kig_ref,
                 qabs_ref, ckv_ref, qi_ref, kidx_ref, small_ref, mqk_ref, mv_ref, mo_ref):
    h = _rms(x_ref[...], g_ref[...]).astype(BF16)

    def proj(grp):
        return _dot(h, w_ref[:, grp[0]:grp[1]])

    dq = proj(_G_DQ)
    for hd in range(DSA_HEADS):
        qh = dq[:, hd * DSA_HEAD_DIM:(hd + 1) * DSA_HEAD_DIM].astype(BF16)
        qa = _dot(qh, wuk_ref[hd]) * (DSA_HEAD_DIM ** -0.5 * _LOG2E)
        qabs_ref[:, hd * DSA_LATENT:(hd + 1) * DSA_LATENT] = qa.astype(BF16)
    ckv_ref[...] = _rms(proj(_G_CKV), kvg_ref[...]).astype(BF16)
    qi_ref[...] = (proj(_G_QI) * (IDX_DIM ** -0.5)).astype(BF16)
    small = proj(_G_SMALL)
    small_ref[...] = small
    kidx_ref[...] = _rms(small[:, :IDX_DIM], kig_ref[...]).astype(BF16)
    mqk_ref[...] = proj(_G_MQK)
    mv_ref[...] = proj(_G_MV).astype(BF16)
    mo_ref[...] = proj(_G_MO)


def _inproj(x, g, w, wuk, kvg, kig, tm):
    T = x.shape[0]
    row = lambda n: pl.BlockSpec((tm, n), lambda i: (i, 0))
    outs = [(8 * DSA_LATENT, BF16), (DSA_LATENT, BF16), (IDX_HEADS * IDX_DIM, BF16), (IDX_DIM, BF16),
            (_SMALL, F32), (2 * ML_HEADS * ML_QK_DIM, F32), (ML_HEADS * ML_V_DIM, BF16),
            (ML_HEADS * ML_V_DIM, F32)]
    return pl.pallas_call(
        _inproj_body,
        grid=(T // tm,),
        in_specs=[row(D_MODEL), _resident(g.shape), _resident(w.shape), _resident(wuk.shape),
                  _resident(kvg.shape), _resident(kig.shape)],
        out_specs=[row(n) for n, _ in outs],
        out_shape=[jax.ShapeDtypeStruct((T, n), dt) for n, dt in outs],
        compiler_params=_params(),
        name="inproj",
    )(x, g, w, wuk, kvg, kig)


def _sublane_fold(v, op, rows=_SUBLANES, ways=4):
    groups = [v[r * rows:(r + 1) * rows, :] for r in range(v.shape[0] // rows)]
    accs = groups[:ways]
    for r in range(ways, len(groups)):
        accs[r % ways] = op(accs[r % ways], groups[r])
    while len(accs) > 1:
        accs = [op(accs[k], accs[k + 1]) if k + 1 < len(accs) else accs[k] for k in range(0, len(accs), 2)]
    return accs[0]


def _bit_transpose32(words):
    a = list(words)
    j, m = 16, 0x0000FFFF
    while j:
        k = 0
        while k < 32:
            t = (a[k] ^ (a[k + j] >> j)) & jnp.uint32(m)
            a[k] = a[k] ^ t
            a[k + j] = a[k + j] ^ (t << j)
            k = (k + j + 1) & ~j
        j >>= 1
        m = (m ^ (m << j)) & 0xFFFFFFFF
    return a


def _dsa_body(qi_ref, wrow_ref, qabs_ref, kidx_ref, ckv_ref, ckvt_ref, wuv_ref, out_ref,
              key_ref, planes_ref, alive_ref, lga_ref, lgb_ref, m_ref, l_ref, acc_ref, *, tq, tk, topk, nbits_idx):
    i = pl.program_id(0)
    n_kb = ((i + 1) * tq + tk - 1) // tk
    w_rows = wrow_ref[0:IDX_HEADS, :] * (IDX_HEADS ** -0.5)
    q_chunk = (i * tq + lax.broadcasted_iota(I32, (1, tq), 1)) >> _CHUNK_SHIFT

    def key_pos(j):
        return j * tk + lax.broadcasted_iota(I32, (tk, 1), 0)

    def score_block(j, carry):
        kx = kidx_ref[pl.ds(pl.multiple_of(j * tk, tk), tk), :]
        s = jnp.zeros((tk, tq), F32)
        for hd in range(IDX_HEADS):
            d = _dot_nt(kx, qi_ref[:, hd * IDX_DIM:(hd + 1) * IDX_DIM])
            s = s + w_rows[hd:hd + 1, :] * jnp.maximum(d, 0.0)
        bits = lax.bitcast_convert_type(s, I32)
        key = bits ^ ((bits >> 31) & 0x7FFFFFFF)
        key_ref[j] = jnp.where((key_pos(j) >> _CHUNK_SHIFT) <= q_chunk, key, _INT_MIN)
        return carry

    lax.fori_loop(0, n_kb, score_block, 0)

    def count(pred):
        def body(j, acc):
            hit = pred(key_ref[j], key_pos(j)).astype(I32)
            return acc + _sublane_fold(hit, jnp.add)
        acc = lax.fori_loop(0, n_kb, body, jnp.zeros((8, tq), I32))
        return jnp.sum(acc, axis=0, keepdims=True)

    n_pair = (n_kb + 1) // 2
    plane_rows = tk // 32

    @pl.when(i == 0)
    def _():
        planes_ref[...] = jnp.zeros(planes_ref.shape, U32)

    def slice_block(j, carry):
        u = lax.bitcast_convert_type(key_ref[j], U32) ^ jnp.uint32(0x80000000)
        for h in range(plane_rows // _SUBLANES):
            base = h * 32 * _SUBLANES
            planes = _bit_transpose32([u[base + _SUBLANES * v:base + _SUBLANES * (v + 1), :] for v in range(32)])
            row0 = pl.multiple_of(j * plane_rows + h * _SUBLANES, _SUBLANES)
            for b in range(32):
                planes_ref[b, pl.ds(row0, _SUBLANES), :] = planes[31 - b]
        return carry

    lax.fori_loop(0, n_kb, slice_block, 0)
    word_row = lax.broadcasted_iota(I32, (alive_ref.shape[0], 1), 0)
    alive_ref[...] = jnp.where(word_row < n_kb * plane_rows, jnp.full(alive_ref.shape, 0xFFFFFFFF, U32),
                               jnp.uint32(0))

    def radix_select(n_rows):
        def ones_count(words):
            return jnp.sum(_sublane_fold(lax.population_count(words).astype(I32), jnp.add), axis=0, keepdims=True)

        def select_bit(it, carry):
            need, t_u = carry
            b = 31 - it
            alive = alive_ref[0:n_rows, :]
            ones = alive & planes_ref[b, pl.ds(0, n_rows), :]
            n_ones = ones_count(ones)
            take = n_ones >= need
            alive_ref[0:n_rows, :] = jnp.where(take, ones, alive ^ ones)
            bit = lax.shift_left(jnp.uint32(1), jnp.asarray(b, U32))
            return jnp.where(take, need, need - n_ones), jnp.where(take, t_u | bit, t_u)

        need, t_u = lax.fori_loop(0, 32, select_bit, (jnp.full((1, tq), topk, I32), jnp.zeros((1, tq), U32)))
        return need, t_u, ones_count(alive_ref[0:n_rows, :])

    all_rows = alive_ref.shape[0]
    if all_rows % (2 * _SUBLANES) == 0:
        need, t_u, n_equal = lax.cond(n_kb * plane_rows <= all_rows // 2,
                                      lambda: radix_select(all_rows // 2), lambda: radix_select(all_rows))
    else:
        need, t_u, n_equal = radix_select(all_rows)
    short = t_u == 0
    t = jnp.maximum(lax.bitcast_convert_type(t_u ^ jnp.uint32(0x80000000), I32), _INT_MIN + 1)
    all_pos = jnp.int32(2 ** nbits_idx - 1)
    has_ties = (n_equal > need) & jnp.logical_not(short)
    n_tie_take = jnp.where(has_ties, need, all_pos)

    def tie_cutoff():
        def pos_bit(b, c):
            cand = c + lax.shift_left(jnp.int32(1), nbits_idx - 1 - b)
            f = count(lambda kb, pos: (kb == t) & (pos < cand))
            return jnp.where(f <= n_tie_take, cand, c)
        return lax.fori_loop(0, nbits_idx, pos_bit, jnp.zeros((1, tq), I32))

    cut = lax.cond(jnp.max(has_ties.astype(I32)) > 0, tie_cutoff, lambda: jnp.full((1, tq), all_pos, I32))

    def masked_block(j, carry):
        key_ref[j] = jnp.full((tk, tq), _INT_MIN, I32)
        return carry

    lax.fori_loop(n_kb, 2 * n_pair, masked_block, 0)

    m_ref[...] = jnp.full(m_ref.shape, _NEG, F32)
    l_ref[...] = jnp.zeros(l_ref.shape, F32)
    acc_ref[...] = jnp.zeros(acc_ref.shape, F32)
    last_blk = ckvt_ref.shape[0] - 1

    def logits(j, lg_buf):
        c_blk = ckv_ref[pl.ds(pl.multiple_of(jnp.minimum(j, last_blk) * tk, tk), tk), :]
        kb = key_ref[j]
        sel = (kb > t) | ((kb == t) & (key_pos(j) < cut))
        bias = jnp.where(sel, 0.0, _NEG).astype(F32)
        for hd in range(DSA_HEADS):
            lg_buf[hd] = _dot_nt(c_blk, qabs_ref[:, hd * DSA_LATENT:(hd + 1) * DSA_LATENT]) + bias

    def accumulate(j, lg_buf):
        c_blk_t = ckvt_ref[jnp.minimum(j, last_blk)]
        for hd in range(DSA_HEADS):
            lg = lg_buf[hd]
            m_old = m_ref[hd:hd + 1, :]
            m_new = jnp.maximum(m_old, jnp.max(_sublane_fold(lg, jnp.maximum, ways=1), axis=0, keepdims=True))
            p = jnp.exp2(lg - m_new)
            alpha = jnp.exp2(m_old - m_new)
            l_ref[hd:hd + 1, :] = alpha * l_ref[hd:hd + 1, :] + jnp.sum(_sublane_fold(p, jnp.add, ways=1), axis=0,
                                                                         keepdims=True)
            acc_ref[hd] = alpha * acc_ref[hd] + _dot(c_blk_t, p.astype(BF16))
            m_ref[hd:hd + 1, :] = m_new

    logits(0, lga_ref)

    def attn_pair(mi, carry):
        ja = 2 * mi
        accumulate(ja, lga_ref)
        logits(ja + 1, lgb_ref)
        accumulate(ja + 1, lgb_ref)
        logits(jnp.minimum(ja + 2, 2 * n_pair - 1), lga_ref)
        return carry

    lax.fori_loop(0, n_pair, attn_pair, 0)

    for hd in range(DSA_HEADS):
        o_lat = (acc_ref[hd] / l_ref[hd:hd + 1, :]).T.astype(BF16)
        out_ref[:, hd * DSA_HEAD_DIM:(hd + 1) * DSA_HEAD_DIM] = _dot(o_lat, wuv_ref[hd]).astype(BF16)


def _dsa(qi, wrows, qabs, kidx, ckv, wuv, tq, tk):
    T = qi.shape[0]
    topk = min(TOPK_MAX, T // 4)
    n_kb = T // tk
    ckvt = jnp.transpose(ckv.reshape(n_kb, tk, DSA_LATENT), (0, 2, 1))
    row = lambda n: pl.BlockSpec((tq, n), lambda i: (i, 0))
    body = functools.partial(_dsa_body, tq=tq, tk=tk, topk=topk, nbits_idx=int(T).bit_length())
    return pl.pallas_call(
        body,
        grid=(T // tq,),
        in_specs=[row(qi.shape[1]), pl.BlockSpec((wrows.shape[0], tq), lambda i: (0, i)), row(qabs.shape[1]),
                  _resident(kidx.shape), _resident(ckv.shape), _resident(ckvt.shape), _resident(wuv.shape)],
        out_specs=row(DSA_HEADS * DSA_HEAD_DIM),
        out_shape=jax.ShapeDtypeStruct((T, DSA_HEADS * DSA_HEAD_DIM), BF16),
        scratch_shapes=[pltpu.VMEM((n_kb + n_kb % 2, tk, tq), I32),
                        pltpu.VMEM((32, T // 32, tq), U32), pltpu.VMEM((T // 32, tq), U32),
                        pltpu.VMEM((DSA_HEADS, tk, tq), F32), pltpu.VMEM((DSA_HEADS, tk, tq), F32),
                        pltpu.VMEM((DSA_HEADS, tq), F32),
                        pltpu.VMEM((DSA_HEADS, tq), F32), pltpu.VMEM((DSA_HEADS, DSA_LATENT, tq), F32)],
        compiler_params=_params(),
        name="dsa",
    )(qi, wrows, qabs, kidx, ckv, ckvt, wuv)


def _log_sigmoid(v):
    return jnp.minimum(v, 0.0) - jnp.log1p(jnp.exp(-jnp.abs(v)))


def _chunk_cumsum(v, axis):
    pos = lax.broadcasted_iota(I32, v.shape, axis) & (CHUNK - 1)
    d = 1
    while d < CHUNK:
        v = v + jnp.where(pos >= d, pltpu.roll(v, d, axis=axis), 0.0)
        d *= 2
    return v


def _mlstm_body(mqk_ref, mv_ref, small_ref, gt_ref, mo_ref, cw_ref, cb_ref, gbc_ref, gbr_ref, ng_ref,
                out_ref, xe_ref, c_ref, n_ref, m_ref, hs_ref, *, rows):
    @pl.when(pl.program_id(0) == 0)
    def _():
        xe_ref[0:8, :] = jnp.zeros((8, xe_ref.shape[1]), F32)
        c_ref[...] = jnp.zeros(c_ref.shape, F32)
        n_ref[...] = jnp.zeros(n_ref.shape, F32)
        m_ref[...] = jnp.zeros(m_ref.shape, F32)

    x = mqk_ref[...]
    xe_ref[8:8 + rows, :] = x
    y = cb_ref[...]
    for j in range(CONV_W - 1):
        y = y + xe_ref[5 + j:5 + j + rows, :] * cw_ref[j:j + 1, :]
    y = y + x * cw_ref[CONV_W - 1:CONV_W, :]
    xe_ref[0:8, :] = x[rows - 8:rows, :]
    qk = y * jax.nn.sigmoid(y)
    nqk = ML_HEADS * ML_QK_DIM
    q_all = (qk[:, :nqk] * (ML_QK_DIM ** -0.5)).astype(BF16)
    k_all = qk[:, nqk:]

    g_col = small_ref[...] + gbc_ref[...]
    g_row = gt_ref[...] + gbr_ref[...]
    b_col = _chunk_cumsum(_log_sigmoid(g_col), 0)
    b_row = _chunk_cumsum(_log_sigmoid(g_row), 1)

    tri = lax.broadcasted_iota(I32, (CHUNK, CHUNK), 1) <= lax.broadcasted_iota(I32, (CHUNK, CHUNK), 0)

    for c in range(rows // CHUNK):
        lo, hi = c * CHUNK, (c + 1) * CHUNK
        for hd in range(ML_HEADS):
            bc = b_col[lo:hi, _S_MF + hd:_S_MF + hd + 1]
            lic = g_col[lo:hi, _S_MI + hd:_S_MI + hd + 1]
            br = b_row[ML_HEADS + hd:ML_HEADS + hd + 1, lo:hi]
            lir = g_row[hd:hd + 1, lo:hi]
            g_tot = bc[CHUNK - 1:CHUNK, :]
            m_prev = m_ref[hd][:, 0:1]

            dmat = jnp.where(tri, bc - br + lir, -jnp.inf)
            inter = bc + m_prev
            m_t = jnp.maximum(inter, jnp.max(dmat, axis=-1, keepdims=True))
            w_intra = jnp.exp(dmat - m_t)
            a_inter = jnp.exp(inter - m_t)

            qh = q_all[lo:hi, hd * ML_QK_DIM:(hd + 1) * ML_QK_DIM]
            kh = k_all[lo:hi, hd * ML_QK_DIM:(hd + 1) * ML_QK_DIM]
            vh = mv_ref[lo:hi, hd * ML_V_DIM:(hd + 1) * ML_V_DIM]
            s_qk = _dot_nt(qh, kh.astype(BF16)) * w_intra
            c_prev = c_ref[hd]
            n_prev = n_ref[hd]
            num = a_inter * _dot(qh, c_prev.astype(BF16)) + _dot(s_qk.astype(BF16), vh)
            den = (a_inter * jnp.sum(qh.astype(F32) * n_prev, axis=-1, keepdims=True)
                   + jnp.sum(s_qk, axis=-1, keepdims=True))
            hs_ref[lo:hi, hd * ML_V_DIM:(hd + 1) * ML_V_DIM] = (
                num / jnp.maximum(jnp.abs(den), jnp.exp(-m_t)))

            m_new = jnp.maximum(g_tot + m_prev, jnp.max(g_tot - br + lir, axis=-1, keepdims=True))
            a_state = jnp.exp(g_tot + m_prev - m_new)
            wk = jnp.exp(g_tot - bc + lic - m_new) * kh
            c_ref[hd] = a_state * c_prev + _dot(wk.T.astype(BF16), vh)
            n_ref[hd] = a_state * n_prev + jnp.sum(wk, axis=0, keepdims=True)
            m_ref[hd] = jnp.broadcast_to(m_new, m_ref.shape[1:])

    for hd in range(ML_HEADS):
        sl = slice(hd * ML_V_DIM, (hd + 1) * ML_V_DIM)
        out_ref[:, sl] = (_rms(hs_ref[:, sl], ng_ref[:, sl]) * jax.nn.sigmoid(mo_ref[:, sl])).astype(BF16)


def _mlstm(mqk, mv, small, gt, mo, cw, cb, gbc, gbr, ng, rows):
    T = mqk.shape[0]
    row = lambda n: pl.BlockSpec((rows, n), lambda i: (i, 0))
    nv = ML_HEADS * ML_V_DIM
    return pl.pallas_call(
        functools.partial(_mlstm_body, rows=rows),
        grid=(T // rows,),
        in_specs=[row(mqk.shape[1]), row(nv), row(_SMALL), pl.BlockSpec((8, rows), lambda i: (1, i)), row(nv),
                  _resident(cw.shape), _resident(cb.shape), _resident(gbc.shape), _resident(gbr.shape),
                  _resident(ng.shape)],
        out_specs=row(nv),
        out_shape=jax.ShapeDtypeStruct((T, nv), BF16),
        scratch_shapes=[pltpu.VMEM((rows + 8, mqk.shape[1]), F32),
                        pltpu.VMEM((ML_HEADS, ML_QK_DIM, ML_V_DIM), F32),
                        pltpu.VMEM((ML_HEADS, 1, ML_QK_DIM), F32),
                        pltpu.VMEM((ML_HEADS, 1, 128), F32),
                        pltpu.VMEM((rows, nv), F32)],
        compiler_params=_params(),
        name="mlstm",
    )(mqk, mv, small, gt, mo, cw, cb, gbc, gbr, ng)


def _memfold_body(mem_ref, g_ref, wk_ref, wv_ref, wq_ref, wc_ref, wqk_ref, wvo_ref):
    mn = _rms(mem_ref[...], g_ref[...]).astype(BF16)
    k = _dot(mn, wk_ref[...].astype(BF16)).astype(BF16)
    v = _dot(mn, wv_ref[...].astype(BF16)).astype(BF16)
    wqk_ref[...] = (_dot_nt(wq_ref[...].astype(BF16), k) * (X_HEAD_DIM ** -0.5)).astype(BF16)
    wvo_ref[...] = _dot(v, wc_ref[...].astype(BF16)).astype(BF16)


def _memfold(mem, g, w_ckv, w_cq, w_co):
    M, D = mem.shape
    dh = X_HEAD_DIM
    return pl.pallas_call(
        _memfold_body,
        grid=(X_HEADS,),
        in_specs=[_resident(mem.shape), _resident(g.shape),
                  pl.BlockSpec((D, dh), lambda h: (0, h)), pl.BlockSpec((D, dh), lambda h: (0, X_HEADS + h)),
                  pl.BlockSpec((D, dh), lambda h: (0, h)), pl.BlockSpec((dh, D), lambda h: (h, 0))],
        out_specs=[pl.BlockSpec((D, M), lambda h: (0, h)), pl.BlockSpec((M, D), lambda h: (h, 0))],
        out_shape=[jax.ShapeDtypeStruct((D, X_HEADS * M), BF16), jax.ShapeDtypeStruct((X_HEADS * M, D), BF16)],
        compiler_params=_params(),
        name="memfold",
    )(mem, g, w_ckv, w_ckv, w_cq, w_co)


def _mixout_body(x_ref, dsa_ref, ml_ref, wo_ref, wqk_ref, wvo_ref, wr_ref, gx_ref, gf_ref,
                 x2_ref, hf_ref, rl_ref, p_ref):
    nd = dsa_ref.shape[1]
    x1 = x_ref[...] + _dot(dsa_ref[...], wo_ref[0:nd, :]) + _dot(ml_ref[...], wo_ref[nd:, :])
    lg_all = _dot(_rms(x1, gx_ref[...]).astype(BF16), wqk_ref[...])
    n_mem = wqk_ref.shape[1] // X_HEADS
    for hd in range(X_HEADS):
        sl = slice(hd * n_mem, (hd + 1) * n_mem)
        lg = lg_all[:, sl]
        e = jnp.exp(lg - jnp.max(lg, axis=-1, keepdims=True))
        p_ref[:, sl] = (e / jnp.sum(e, axis=-1, keepdims=True)).astype(BF16)
    x2 = x1 + _dot(p_ref[...], wvo_ref[...])
    x2_ref[...] = x2
    hf = _rms(x2, gf_ref[...]).astype(BF16)
    rl_ref[...] = _dot(hf, wr_ref[...])
    bits = lax.bitcast_convert_type(hf.astype(F32), U32)
    for c in range(_SUBLANES):
        lo = bits[:, (2 * c) * _LANES:(2 * c + 1) * _LANES]
        hi = bits[:, (2 * c + 1) * _LANES:(2 * c + 2) * _LANES]
        hf_ref[pl.ds(c, x2.shape[0], stride=_SUBLANES), :] = (hi & jnp.uint32(0xFFFF0000)) | (lo >> 16)


def _mixout(x, dsa, ml, wo, wqk, wvo, wr, gx, gf, tm):
    T = x.shape[0]
    row = lambda n: pl.BlockSpec((tm, n), lambda i: (i, 0))
    return pl.pallas_call(
        _mixout_body,
        grid=(T // tm,),
        in_specs=[row(D_MODEL), row(dsa.shape[1]), row(ml.shape[1]), _resident(wo.shape),
                  _resident(wqk.shape), _resident(wvo.shape), _resident(wr.shape), _resident(gx.shape),
                  _resident(gf.shape)],
        out_specs=[row(D_MODEL), pl.BlockSpec((tm * _SUBLANES, _LANES), lambda i: (i, 0)), row(wr.shape[1])],
        out_shape=[jax.ShapeDtypeStruct((T, D_MODEL), F32),
                   jax.ShapeDtypeStruct((T * _SUBLANES, _LANES), U32),
                   jax.ShapeDtypeStruct((T, wr.shape[1]), F32)],
        scratch_shapes=[pltpu.VMEM((tm, wqk.shape[1]), BF16)],
        compiler_params=_params(),
        name="mixout",
    )(x, dsa, ml, wo, wqk, wvo, wr, gx, gf)


def _moe_body(tok_ref, eb_ref, hf_hbm, wg_ref, wu_ref, wd_ref, y_hbm,
              xbuf, ybuf, wgb, wub, wdb, gsem, ysem, *, bm, n_blk):
    e = pl.program_id(0)
    n_valid = eb_ref[N_EXPERTS]
    b_lo = eb_ref[e]
    b_hi = eb_ref[e + 1]

    def tok_words(tok):
        return hf_hbm.at[pl.ds(pl.multiple_of(tok * _SUBLANES, _SUBLANES), _SUBLANES)]

    def gather(blk, s):
        def issue(g, carry):
            for u in range(_SUBLANES):
                r = g * _SUBLANES + u
                pltpu.make_async_copy(tok_words(tok_ref[blk * bm + r]),
                                      xbuf.at[s, pl.ds(pl.multiple_of(r * _SUBLANES, _SUBLANES), _SUBLANES)],
                                      gsem.at[s]).start(priority=1)
            return carry
        lax.fori_loop(0, bm // _SUBLANES, issue, 0)

    def wait_gather(s):
        pltpu.make_async_copy(xbuf.at[s], xbuf.at[s], gsem.at[s]).wait()

    def y_copy(blk, s):
        return pltpu.make_async_copy(ybuf.at[s], y_hbm.at[pl.ds(pl.multiple_of(blk * bm, bm), bm)], ysem.at[s])

    @pl.when(e == 0)
    def _():
        for k in range(_GATHER_AHEAD):
            gather(k, k)

    @pl.when(b_hi > b_lo)
    def _():
        wgb[...] = wg_ref[0].astype(BF16)
        wub[...] = wu_ref[0].astype(BF16)
        wdb[...] = wd_ref[0].astype(BF16)

        def block(b, carry):
            s = b % _GATHER_SLOTS
            wait_gather(s)
            chunks = []
            for c in range(_SUBLANES):
                w = xbuf[s, pl.ds(c, bm, stride=_SUBLANES), :]
                chunks.append(lax.bitcast_convert_type(w << 16, F32).astype(BF16))
                chunks.append(lax.bitcast_convert_type(w & jnp.uint32(0xFFFF0000), F32).astype(BF16))
            xb = jnp.concatenate(chunks, axis=-1)
            gate = _dot(xb, wgb[...])
            a = gate * jax.nn.sigmoid(gate) * _dot(xb, wub[...])
            y = _dot(a.astype(BF16), wdb[...])

            nxt = jnp.minimum(b + _GATHER_AHEAD, n_blk - 1)
            for r in range(bm):
                pltpu.make_async_copy(tok_words(tok_ref[nxt * bm + r]),
                                      xbuf.at[(b + _GATHER_AHEAD) % _GATHER_SLOTS, pl.ds(r * _SUBLANES, _SUBLANES)],
                                      gsem.at[(b + _GATHER_AHEAD) % _GATHER_SLOTS]).start(priority=1)

            @pl.when(b >= 2)
            def _():
                y_copy(b - 2, b % 2).wait()

            ybuf[b % 2] = y
            y_copy(b, b % 2).start()
            return carry

        lax.fori_loop(b_lo, b_hi, block, 0)

    @pl.when(e == pl.num_programs(0) - 1)
    def _():
        for k in range(_GATHER_AHEAD):
            wait_gather((n_valid + k) % _GATHER_SLOTS)

        @pl.when(n_valid >= 2)
        def _():
            y_copy(n_valid - 2, n_valid % 2).wait()
        y_copy(n_valid - 1, (n_valid - 1) % 2).wait()
        ybuf[0] = jnp.zeros(ybuf.shape[1:], F32)

        def zero_block(b, carry):
            cp = y_copy(b, 0)
            cp.start()
            cp.wait()
            return carry

        lax.fori_loop(n_valid, n_blk, zero_block, 0)


def _moe(tok, e_blk, hf, wg, wu, wd, n_blk, bm):
    D = wg.shape[1]
    wspec = lambda shape: pl.BlockSpec((1,) + shape, lambda e, *_: (e, 0, 0))
    grid_spec = pltpu.PrefetchScalarGridSpec(
        num_scalar_prefetch=2,
        grid=(N_EXPERTS,),
        in_specs=[pl.BlockSpec(memory_space=pl.ANY),
                  wspec((D, D_EXPERT)), wspec((D, D_EXPERT)), wspec((D_EXPERT, D))],
        out_specs=pl.BlockSpec(memory_space=pl.ANY),
        scratch_shapes=[pltpu.VMEM((_GATHER_SLOTS, bm * _SUBLANES, _LANES), U32),
                        pltpu.VMEM((2, bm, D), F32),
                        pltpu.VMEM((D, D_EXPERT), BF16), pltpu.VMEM((D, D_EXPERT), BF16),
                        pltpu.VMEM((D_EXPERT, D), BF16),
                        pltpu.SemaphoreType.DMA((_GATHER_SLOTS,)), pltpu.SemaphoreType.DMA((2,))],
    )
    return pl.pallas_call(
        functools.partial(_moe_body, bm=bm, n_blk=n_blk),
        grid_spec=grid_spec,
        out_shape=jax.ShapeDtypeStruct((n_blk * bm, D), F32),
        compiler_params=_params(),
        name="moe",
    )(tok, e_blk, hf, wg, wu, wd)


def _route(rl, b_group, b_router, bm):
    N = rl.shape[0]
    g_logits = rl[:, :N_GROUPS] + b_group
    g_prob = jax.nn.softmax(g_logits, axis=-1)
    g_sel = jnp.argmax(g_logits, axis=-1)
    p_g = jnp.take_along_axis(g_prob, g_sel[:, None], axis=-1)
    e_logits = rl[:, N_GROUPS:N_GROUPS + N_EXPERTS] + b_router
    in_group = (jnp.arange(N_EXPERTS, dtype=I32) // EXP_PER_GROUP)[None, :] == g_sel[:, None]
    e_prob = jax.nn.softmax(jnp.where(in_group, e_logits, -jnp.inf), axis=-1)
    e_cand = jnp.where(in_group, e_prob, -1.0)
    tops = []
    for _ in range(TOPK_IN_GROUP):
        best = jnp.argmax(e_cand, axis=-1).astype(I32)
        tops.append((jnp.max(e_cand, axis=-1), best))
        e_cand = jnp.where(jnp.arange(N_EXPERTS, dtype=I32)[None, :] == best[:, None], -2.0, e_cand)
    top_p = jnp.stack([p for p, _ in tops], axis=-1)
    expert_id = jnp.stack([e for _, e in tops], axis=-1)
    gates = p_g * top_p / top_p.sum(-1, keepdims=True)

    A = N * TOPK_IN_GROUP
    flat_e = expert_id.reshape(A)
    onehot = (flat_e[:, None] == jnp.arange(N_EXPERTS, dtype=I32)[None, :])
    seg = _RANK_SEG if A % _RANK_SEG == 0 else A
    oh = onehot.astype(BF16).reshape(A // seg, seg, N_EXPERTS)
    before = (jnp.arange(seg)[None, :] < jnp.arange(seg)[:, None]).astype(BF16)
    within = jnp.einsum('ij,bjk->bik', before, oh, preferred_element_type=F32)
    seg_tot = jnp.sum(oh.astype(F32), axis=1)
    seg_base = jnp.cumsum(seg_tot, axis=0) - seg_tot
    rank = jnp.sum((within + seg_base[:, None, :]) * oh.astype(F32), axis=-1).reshape(A).astype(I32)
    counts = jnp.sum(seg_tot, axis=0).astype(I32)
    padded = (counts + bm - 1) // bm * bm
    pad_ends = jnp.cumsum(padded)
    pad_starts = pad_ends - padded
    row = jnp.sum(jnp.where(onehot, pad_starts[None, :], 0), axis=1) + rank
    n_blk = -(-A // bm) + N_EXPERTS
    row_tok = jnp.zeros((n_blk * bm,), I32).at[row].set(jnp.arange(A, dtype=I32) // TOPK_IN_GROUP)
    e_blk = jnp.concatenate([pad_starts, pad_ends[-1:]]) // bm
    return row_tok, row, gates, e_blk.astype(I32), n_blk


def _final_body(row_ref, x_ref, gate_ref, y_hbm, g_ref, out_ref, ybuf, sem, *, tm):
    i = pl.program_id(0)
    slot = i % 2

    def gather(tile, s):
        def issue(g, carry):
            for u in range(_SUBLANES):
                for k in range(TOPK_IN_GROUP):
                    src = row_ref[(tile * tm + g * _SUBLANES + u) * TOPK_IN_GROUP + k]
                    pltpu.make_async_copy(y_hbm.at[pl.ds(src, 1)], ybuf.at[s, k, g, pl.ds(u, 1)],
                                          sem.at[s]).start()
            return carry
        lax.fori_loop(0, tm // _SUBLANES, issue, 0)

    @pl.when(i == 0)
    def _():
        gather(0, 0)

    @pl.when(i + 1 < pl.num_programs(0))
    def _():
        gather(i + 1, 1 - slot)

    pltpu.make_async_copy(ybuf.at[slot], ybuf.at[slot], sem.at[slot]).wait()
    acc = x_ref[...]
    for k in range(TOPK_IN_GROUP):
        acc = acc + gate_ref[:, k:k + 1] * ybuf[slot, k].reshape(tm, ybuf.shape[-1])
    out_ref[...] = _rms(acc, g_ref[...])


def _final(row, x2, gates, y_rows, g, tm):
    T, D = x2.shape
    grid_spec = pltpu.PrefetchScalarGridSpec(
        num_scalar_prefetch=1,
        grid=(T // tm,),
        in_specs=[pl.BlockSpec((tm, D), lambda i, *_: (i, 0)),
                  pl.BlockSpec((tm, TOPK_IN_GROUP), lambda i, *_: (i, 0)),
                  pl.BlockSpec(memory_space=pl.ANY),
                  pl.BlockSpec(g.shape, lambda i, *_: (0, 0))],
        out_specs=pl.BlockSpec((tm, D), lambda i, *_: (i, 0)),
        scratch_shapes=[pltpu.VMEM((2, TOPK_IN_GROUP, tm // _SUBLANES, _SUBLANES, D), F32),
                        pltpu.SemaphoreType.DMA((2,))],
    )
    return pl.pallas_call(
        functools.partial(_final_body, tm=tm),
        grid_spec=grid_spec,
        out_shape=jax.ShapeDtypeStruct((T, D), F32),
        compiler_params=_params(),
        name="final",
    )(row, x2, gates, y_rows, g)


def _tile_sizes(T):
    pick = lambda want: want if T % want == 0 else CHUNK
    return dict(inproj=pick(256), dsa_q=pick(256), dsa_k=pick(512), mlstm=pick(256), mixout=pick(256),
                final=pick(256), moe=128)


def _layer(x, mem, norm_mix_g, w_in, kv_norm_g, k_idx_norm_g, w_uk, w_uv, conv_w, conv_b, gate_b, ml_norm_g,
           w_out, norm_x_g, mem_norm_g, w_cq, w_ckv, w_co, norm_ffn_g, w_group, b_group, w_router, b_router,
           w_gate, w_up, w_down, out_g):
    T = x.shape[0]
    ts = _tile_sizes(T)
    r2 = lambda v: v.reshape(1, -1)

    w_r = _wprep(w_in, 256)
    wuk_t = jnp.transpose(w_uk, (1, 2, 0)).astype(BF16)
    wuv_t = jnp.transpose(w_uv, (1, 0, 2)).astype(BF16)

    qabs, ckv, qi, kidx, small, mqk, mv, mo = _inproj(
        x, r2(norm_mix_g), w_r, wuk_t, r2(kv_norm_g), r2(k_idx_norm_g), ts["inproj"])

    gate_rows = jnp.transpose(small[:, _S_WI:_S_MF + ML_HEADS])
    dsa_out = _dsa(qi, gate_rows, qabs, kidx, ckv, wuv_t, ts["dsa_q"], ts["dsa_k"])

    gb_col = jnp.zeros((1, _SMALL), F32).at[0, _S_MI:_S_MI + 2 * ML_HEADS].set(gate_b)
    ml_out = _mlstm(mqk, mv, small, gate_rows, mo, conv_w, r2(conv_b), gb_col, gate_b.reshape(-1, 1),
                    r2(ml_norm_g), ts["mlstm"])

    wqk, wvo = _memfold(mem, r2(mem_norm_g), w_ckv, w_cq, w_co)
    w_rt = jnp.concatenate([w_group, w_router,
                            jnp.zeros((D_MODEL, 128 - N_GROUPS - N_EXPERTS), w_group.dtype)], axis=1)
    x2, hf, rl = _mixout(x, dsa_out, ml_out, w_out.astype(BF16), wqk, wvo, w_rt.astype(BF16),
                         r2(norm_x_g), r2(norm_ffn_g), ts["mixout"])

    bm = ts["moe"]
    row_tok, row, gates, e_blk, n_blk = _route(rl, b_group, b_router, bm)
    y_rows = _moe(row_tok, e_blk, hf, w_gate, w_up, w_down, n_blk, bm)
    return _final(row, x2, gates, y_rows, r2(out_g), ts["final"])


def kernel(x, mem, norm_mix_g, w_in, kv_norm_g, k_idx_norm_g, w_uk, w_uv, conv_w, conv_b, gate_b, ml_norm_g,
           w_out, norm_x_g, mem_norm_g, w_cq, w_ckv, w_co, norm_ffn_g, w_group, b_group, w_router, b_router,
           w_gate, w_up, w_down, final_norm_g):
    B, T, D = x.shape
    assert B == 1 and D == D_MODEL and norm_mix_g.shape[0] == 1 and T % CHUNK == 0
    out = _layer(x[0], mem[0], norm_mix_g[0], w_in, kv_norm_g[0], k_idx_norm_g[0], w_uk[0], w_uv[0],
                 conv_w[0], conv_b[0], gate_b[0], ml_norm_g[0], w_out[0], norm_x_g[0], mem_norm_g[0],
                 w_cq[0], w_ckv[0], w_co[0], norm_ffn_g[0], w_group[0], b_group[0], w_router[0], b_router[0],
                 w_gate[0], w_up[0], w_down[0], final_norm_g)
    return out[None]
```

```python
import functools

import jax
import jax.numpy as jnp
import numpy as np
from jax import lax
from jax.experimental import pallas as pl
from jax.experimental.pallas import tpu as pltpu

F32 = jnp.float32
BF16 = jnp.bfloat16
I32 = jnp.int32
I16 = jnp.int16
U32 = jnp.uint32

EPS = 1e-6
CHUNK = 64
D_MODEL = 2048

DSA_HEADS = 8
DSA_HEAD_DIM = 128
DSA_LATENT = 256
IDX_HEADS = 8
IDX_DIM = 64
TOPK_MAX = 256

ML_HEADS = 4
ML_QK_DIM = 128
ML_V_DIM = 256
CONV_W = 4

X_HEADS = 4
X_HEAD_DIM = D_MODEL // X_HEADS

N_GROUPS = 4
EXP_PER_GROUP = 8
N_EXPERTS = N_GROUPS * EXP_PER_GROUP
TOPK_IN_GROUP = 2
D_EXPERT = 512

_O_DQ = 0
_O_CKV = _O_DQ + DSA_HEADS * DSA_HEAD_DIM
_O_QI = _O_CKV + DSA_LATENT
_O_KI = _O_QI + IDX_HEADS * IDX_DIM
_O_WI = _O_KI + IDX_DIM
_O_MQ = _O_WI + IDX_HEADS
_O_MK = _O_MQ + ML_HEADS * ML_QK_DIM
_O_MV = _O_MK + ML_HEADS * ML_QK_DIM
_O_MI = _O_MV + ML_HEADS * ML_V_DIM
_O_MF = _O_MI + ML_HEADS
_O_MO = _O_MF + ML_HEADS
_O_END = _O_MO + ML_HEADS * ML_V_DIM

_G_DQ = (0, 1024)
_G_CKV = (1024, 1280)
_G_QI = (1280, 1792)
_G_SMALL = (1792, 1920)
_G_MQK = (1920, 2944)
_G_MV = (2944, 3968)
_G_MO = (3968, 4992)
_W_COLS = 4992
_S_WI = IDX_DIM
_S_MI = _S_WI + IDX_HEADS
_S_MF = _S_MI + ML_HEADS
_SMALL = 128

_VMEM_LIMIT = 56 * 1024 * 1024
_INT_MIN = -(2 ** 31)
_I16_MIN = -(2 ** 15)
_CHUNK_SHIFT = CHUNK.bit_length() - 1
_LOG2E = 1.4426950408889634
_SUBLANES = 8
_LANES = 128
_GATHER_SLOTS = 9
_GATHER_AHEAD = _GATHER_SLOTS - 1
_RANK_SEG = 512
_NEG = -1e30


def _rms(v, g):
    return v * lax.rsqrt(jnp.mean(v * v, axis=-1, keepdims=True) + EPS) * g


def _dot(a, b):
    return jnp.dot(a, b, preferred_element_type=F32)


def _dot_nt(a, b):
    return lax.dot_general(a, b, (((1,), (1,)), ((), ())), preferred_element_type=F32)


def _resident(shape):
    nd = len(shape)
    return pl.BlockSpec(shape, lambda *_: (0,) * nd, pipeline_mode=pl.Buffered(1))


def _params(n_axes=1):
    return pltpu.CompilerParams(dimension_semantics=("arbitrary",) * n_axes,
                                vmem_limit_bytes=_VMEM_LIMIT)


def _wprep_body(w_hbm, out_ref, wbuf, sem, *, tk):
    i = pl.program_id(0)
    slot = i % 2

    def rows(blk, s):
        return pltpu.make_async_copy(w_hbm.at[0, pl.ds(pl.multiple_of(blk * tk, tk), tk)], wbuf.at[s], sem.at[s])

    @pl.when(i == 0)
    def _():
        rows(0, 0).start()

    @pl.when(i + 1 < pl.num_programs(0))
    def _():
        rows(i + 1, 1 - slot).start()

    rows(i, slot).wait()
    w_ref = wbuf.at[slot]
    n_small = IDX_DIM + IDX_HEADS
    out_ref[:, _G_DQ[0]:_G_SMALL[0] + n_small] = w_ref[:, _O_DQ:_O_MQ].astype(BF16)
    out_ref[:, _G_SMALL[0] + n_small:_G_SMALL[0] + n_small + 2 * ML_HEADS] = w_ref[:, _O_MI:_O_MO].astype(BF16)
    out_ref[:, _G_SMALL[0] + n_small + 2 * ML_HEADS:_G_SMALL[1]] = jnp.zeros(
        (out_ref.shape[0], _SMALL - n_small - 2 * ML_HEADS), BF16)
    out_ref[:, _G_MQK[0]:_G_MV[1]] = w_ref[:, _O_MQ:_O_MI].astype(BF16)
    out_ref[:, _G_MO[0]:_G_MO[1]] = w_ref[:, _O_MO:_O_END].astype(BF16)


def _wprep(w_in, tk):
    K = w_in.shape[1]
    return pl.pallas_call(
        functools.partial(_wprep_body, tk=tk),
        grid=(K // tk,),
        in_specs=[pl.BlockSpec(memory_space=pl.ANY)],
        out_specs=pl.BlockSpec((tk, _W_COLS), lambda i: (i, 0)),
        out_shape=jax.ShapeDtypeStruct((K, _W_COLS), BF16),
        scratch_shapes=[pltpu.VMEM((2, tk, w_in.shape[2]), F32), pltpu.SemaphoreType.DMA((2,))],
        compiler_params=_params(),
        name="wprep",
    )(w_in)


def _inproj_body(x_ref, g_ref, w_ref, wuk_ref, kvg_ref, kig_ref,
                 qabs_ref, ckv_ref, qi_ref, kidx_ref, small_ref, mqk_ref, mv_ref, mo_ref):
    h = _rms(x_ref[...], g_ref[...]).astype(BF16)

    def proj(grp):
        return _dot(h, w_ref[:, grp[0]:grp[1]])

    dq = proj(_G_DQ)
    for hd in range(DSA_HEADS):
        qh = dq[:, hd * DSA_HEAD_DIM:(hd + 1) * DSA_HEAD_DIM].astype(BF16)
        qa = _dot(qh, wuk_ref[hd]) * (DSA_HEAD_DIM ** -0.5 * _LOG2E)
        qabs_ref[:, hd * DSA_LATENT:(hd + 1) * DSA_LATENT] = qa.astype(BF16)
    ckv_ref[...] = _rms(proj(_G_CKV), kvg_ref[...]).astype(BF16)
    qi_ref[...] = (proj(_G_QI) * (IDX_DIM ** -0.5)).astype(BF16)
    small = proj(_G_SMALL)
    small_ref[...] = small
    kidx_ref[...] = _rms(small[:, :IDX_DIM], kig_ref[...]).astype(BF16)
    mqk_ref[...] = proj(_G_MQK)
    mv_ref[...] = proj(_G_MV).astype(BF16)
    mo_ref[...] = proj(_G_MO)


def _inproj(x, g, w, wuk, kvg, kig, tm):
    T = x.shape[0]
    row = lambda n: pl.BlockSpec((tm, n), lambda i: (i, 0))
    outs = [(8 * DSA_LATENT, BF16), (DSA_LATENT, BF16), (IDX_HEADS * IDX_DIM, BF16), (IDX_DIM, BF16),
            (_SMALL, F32), (2 * ML_HEADS * ML_QK_DIM, F32), (ML_HEADS * ML_V_DIM, BF16),
            (ML_HEADS * ML_V_DIM, F32)]
    return pl.pallas_call(
        _inproj_body,
        grid=(T // tm,),
        in_specs=[row(D_MODEL), _resident(g.shape), _resident(w.shape), _resident(wuk.shape),
                  _resident(kvg.shape), _resident(kig.shape)],
        out_specs=[row(n) for n, _ in outs],
        out_shape=[jax.ShapeDtypeStruct((T, n), dt) for n, dt in outs],
        compiler_params=_params(),
        name="inproj",
    )(x, g, w, wuk, kvg, kig)


def _sublane_fold(v, op, rows=_SUBLANES, ways=4):
    groups = [v[r * rows:(r + 1) * rows, :] for r in range(v.shape[0] // rows)]
    accs = groups[:ways]
    for r in range(ways, len(groups)):
        accs[r % ways] = op(accs[r % ways], groups[r])
    while len(accs) > 1:
        accs = [op(accs[k], accs[k + 1]) if k + 1 < len(accs) else accs[k] for k in range(0, len(accs), 2)]
    return accs[0]


def _bit_transpose32(words):
    a = list(words)
    j, m = 16, 0x0000FFFF
    while j:
        k = 0
        while k < 32:
            t = (a[k] ^ (a[k + j] >> j)) & jnp.uint32(m)
            a[k] = a[k] ^ t
            a[k + j] = a[k + j] ^ (t << j)
            k = (k + j + 1) & ~j
        j >>= 1
        m = (m ^ (m << j)) & 0xFFFFFFFF
    return a


def _dsa_body(qi_ref, wrow_ref, qabs_ref, kidx_ref, ckv_ref, ckvt_ref, wuv_ref, out_ref,
              key_ref, planes_ref, alive_ref, lga_ref, lgb_ref, m_ref, l_ref, acc_ref, *, tq, tk, topk, nbits_idx):
    i = pl.program_id(0)
    n_kb = ((i + 1) * tq + tk - 1) // tk
    w_rows = wrow_ref[0:IDX_HEADS, :] * (IDX_HEADS ** -0.5)
    q_chunk = (i * tq + lax.broadcasted_iota(I32, (1, tq), 1)) >> _CHUNK_SHIFT

    def key_pos(j):
        return j * tk + lax.broadcasted_iota(I32, (tk, 1), 0)

    def score_block(j, carry):
        kx = kidx_ref[pl.ds(pl.multiple_of(j * tk, tk), tk), :]
        s = jnp.zeros((tk, tq), F32)
        for hd in range(IDX_HEADS):
            d = _dot_nt(kx, qi_ref[:, hd * IDX_DIM:(hd + 1) * IDX_DIM])
            s = s + w_rows[hd:hd + 1, :] * jnp.maximum(d, 0.0)
        bits = lax.bitcast_convert_type(s, I32)
        key = bits ^ ((bits >> 31) & 0x7FFFFFFF)
        key_ref[j] = jnp.where((key_pos(j) >> _CHUNK_SHIFT) <= q_chunk, key, _INT_MIN)
        return carry

    lax.fori_loop(0, n_kb, score_block, 0)

    def count(pred):
        def body(j, acc):
            hit = pred(key_ref[j], key_pos(j)).astype(I32)
            return acc + _sublane_fold(hit, jnp.add)
        acc = lax.fori_loop(0, n_kb, body, jnp.zeros((8, tq), I32))
        return jnp.sum(acc, axis=0, keepdims=True)

    n_pair = (n_kb + 1) // 2
    plane_rows = tk // 32

    @pl.when(i == 0)
    def _():
        planes_ref[...] = jnp.zeros(planes_ref.shape, U32)

    def slice_block(j, carry):
        u = lax.bitcast_convert_type(key_ref[j], U32) ^ jnp.uint32(0x80000000)
        for h in range(plane_rows // _SUBLANES):
            base = h * 32 * _SUBLANES
            planes = _bit_transpose32([u[base + _SUBLANES * v:base + _SUBLANES * (v + 1), :] for v in range(32)])
            row0 = pl.multiple_of(j * plane_rows + h * _SUBLANES, _SUBLANES)
            for b in range(32):
                planes_ref[b, pl.ds(row0, _SUBLANES), :] = planes[31 - b]
        return carry

    lax.fori_loop(0, n_kb, slice_block, 0)
    word_row = lax.broadcasted_iota(I32, (alive_ref.shape[0], 1), 0)
    alive_ref[...] = jnp.where(word_row < n_kb * plane_rows, jnp.full(alive_ref.shape, 0xFFFFFFFF, U32),
                               jnp.uint32(0))

    def radix_select(n_rows):
        def ones_count(words):
            return jnp.sum(_sublane_fold(lax.population_count(words).astype(I32), jnp.add), axis=0, keepdims=True)

        def select_bit(it, carry):
            need, t_u = carry
            b = 31 - it
            alive = alive_ref[0:n_rows, :]
            ones = alive & planes_ref[b, pl.ds(0, n_rows), :]
            n_ones = ones_count(ones)
            take = n_ones >= need
            alive_ref[0:n_rows, :] = jnp.where(take, ones, alive ^ ones)
            bit = lax.shift_left(jnp.uint32(1), jnp.asarray(b, U32))
            return jnp.where(take, need, need - n_ones), jnp.where(take, t_u | bit, t_u)

        need, t_u = lax.fori_loop(0, 32, select_bit, (jnp.full((1, tq), topk, I32), jnp.zeros((1, tq), U32)))
        return need, t_u, ones_count(alive_ref[0:n_rows, :])

    all_rows = alive_ref.shape[0]
    if all_rows % (2 * _SUBLANES) == 0:
        need, t_u, n_equal = lax.cond(n_kb * plane_rows <= all_rows // 2,
                                      lambda: radix_select(all_rows // 2), lambda: radix_select(all_rows))
    else:
        need, t_u, n_equal = radix_select(all_rows)
    short = t_u == 0
    t = jnp.maximum(lax.bitcast_convert_type(t_u ^ jnp.uint32(0x80000000), I32), _INT_MIN + 1)
    all_pos = jnp.int32(2 ** nbits_idx - 1)
    has_ties = (n_equal > need) & jnp.logical_not(short)
    n_tie_take = jnp.where(has_ties, need, all_pos)

    def tie_cutoff():
        def pos_bit(b, c):
            cand = c + lax.shift_left(jnp.int32(1), nbits_idx - 1 - b)
            f = count(lambda kb, pos: (kb == t) & (pos < cand))
            return jnp.where(f <= n_tie_take, cand, c)
        return lax.fori_loop(0, nbits_idx, pos_bit, jnp.zeros((1, tq), I32))

    cut = lax.cond(jnp.max(has_ties.astype(I32)) > 0, tie_cutoff, lambda: jnp.full((1, tq), all_pos, I32))

    def masked_block(j, carry):
        key_ref[j] = jnp.full((tk, tq), _INT_MIN, I32)
        return carry

    lax.fori_loop(n_kb, 2 * n_pair, masked_block, 0)

    m_ref[...] = jnp.full(m_ref.shape, _NEG, F32)
    l_ref[...] = jnp.zeros(l_ref.shape, F32)
    acc_ref[...] = jnp.zeros(acc_ref.shape, F32)
    last_blk = ckvt_ref.shape[0] - 1

    def logits(j, lg_buf):
        c_blk = ckv_ref[pl.ds(pl.multiple_of(jnp.minimum(j, last_blk) * tk, tk), tk), :]
        kb = key_ref[j]
        sel = (kb > t) | ((kb == t) & (key_pos(j) < cut))
        bias = jnp.where(sel, 0.0, _NEG).astype(F32)
        for hd in range(DSA_HEADS):
            lg_buf[hd] = _dot_nt(c_blk, qabs_ref[:, hd * DSA_LATENT:(hd + 1) * DSA_LATENT]) + bias

    def accumulate(j, lg_buf):
        c_blk_t = ckvt_ref[jnp.minimum(j, last_blk)]
        for hd in range(DSA_HEADS):
            lg = lg_buf[hd]
            m_old = m_ref[hd:hd + 1, :]
            m_new = jnp.maximum(m_old, jnp.max(_sublane_fold(lg, jnp.maximum, ways=1), axis=0, keepdims=True))
            p = jnp.exp2(lg - m_new)
            alpha = jnp.exp2(m_old - m_new)
            l_ref[hd:hd + 1, :] = alpha * l_ref[hd:hd + 1, :] + jnp.sum(_sublane_fold(p, jnp.add, ways=1), axis=0,
                                                                         keepdims=True)
            acc_ref[hd] = alpha * acc_ref[hd] + _dot(c_blk_t, p.astype(BF16))
            m_ref[hd:hd + 1, :] = m_new

    logits(0, lga_ref)

    def attn_pair(mi, carry):
        ja = 2 * mi
        accumulate(ja, lga_ref)
        logits(ja + 1, lgb_ref)
        accumulate(ja + 1, lgb_ref)
        logits(jnp.minimum(ja + 2, 2 * n_pair - 1), lga_ref)
        return carry

    lax.fori_loop(0, n_pair, attn_pair, 0)

    for hd in range(DSA_HEADS):
        o_lat = (acc_ref[hd] / l_ref[hd:hd + 1, :]).T.astype(BF16)
        out_ref[:, hd * DSA_HEAD_DIM:(hd + 1) * DSA_HEAD_DIM] = _dot(o_lat, wuv_ref[hd]).astype(BF16)


def _dsa(qi, wrows, qabs, kidx, ckv, wuv, tq, tk):
    T = qi.shape[0]
    topk = min(TOPK_MAX, T // 4)
    n_kb = T // tk
    ckvt = jnp.transpose(ckv.reshape(n_kb, tk, DSA_LATENT), (0, 2, 1))
    row = lambda n: pl.BlockSpec((tq, n), lambda i: (i, 0))
    body = functools.partial(_dsa_body, tq=tq, tk=tk, topk=topk, nbits_idx=int(T).bit_length())
    return pl.pallas_call(
        body,
        grid=(T // tq,),
        in_specs=[row(qi.shape[1]), pl.BlockSpec((wrows.shape[0], tq), lambda i: (0, i)), row(qabs.shape[1]),
                  _resident(kidx.shape), _resident(ckv.shape), _resident(ckvt.shape), _resident(wuv.shape)],
        out_specs=row(DSA_HEADS * DSA_HEAD_DIM),
        out_shape=jax.ShapeDtypeStruct((T, DSA_HEADS * DSA_HEAD_DIM), BF16),
        scratch_shapes=[pltpu.VMEM((n_kb + n_kb % 2, tk, tq), I32),
                        pltpu.VMEM((32, T // 32, tq), U32), pltpu.VMEM((T // 32, tq), U32),
                        pltpu.VMEM((DSA_HEADS, tk, tq), F32), pltpu.VMEM((DSA_HEADS, tk, tq), F32),
                        pltpu.VMEM((DSA_HEADS, tq), F32),
                        pltpu.VMEM((DSA_HEADS, tq), F32), pltpu.VMEM((DSA_HEADS, DSA_LATENT, tq), F32)],
        compiler_params=_params(),
        name="dsa",
    )(qi, wrows, qabs, kidx, ckv, ckvt, wuv)


def _log_sigmoid(v):
    return jnp.minimum(v, 0.0) - jnp.log1p(jnp.exp(-jnp.abs(v)))


def _chunk_cumsum(v, axis):
    pos = lax.broadcasted_iota(I32, v.shape, axis) & (CHUNK - 1)
    d = 1
    while d < CHUNK:
        v = v + jnp.where(pos >= d, pltpu.roll(v, d, axis=axis), 0.0)
        d *= 2
    return v


def _mlstm_body(mqk_ref, mv_ref, small_ref, gt_ref, mo_ref, cw_ref, cb_ref, gbc_ref, gbr_ref, ng_ref,
                out_ref, xe_ref, c_ref, n_ref, m_ref, hs_ref, *, rows):
    @pl.when(pl.program_id(0) == 0)
    def _():
        xe_ref[0:8, :] = jnp.zeros((8, xe_ref.shape[1]), F32)
        c_ref[...] = jnp.zeros(c_ref.shape, F32)
        n_ref[...] = jnp.zeros(n_ref.shape, F32)
        m_ref[...] = jnp.zeros(m_ref.shape, F32)

    x = mqk_ref[...]
    xe_ref[8:8 + rows, :] = x
    y = cb_ref[...]
    for j in range(CONV_W - 1):
        y = y + xe_ref[5 + j:5 + j + rows, :] * cw_ref[j:j + 1, :]
    y = y + x * cw_ref[CONV_W - 1:CONV_W, :]
    xe_ref[0:8, :] = x[rows - 8:rows, :]
    qk = y * jax.nn.sigmoid(y)
    nqk = ML_HEADS * ML_QK_DIM
    q_all = (qk[:, :nqk] * (ML_QK_DIM ** -0.5)).astype(BF16)
    k_all = qk[:, nqk:]

    g_col = small_ref[...] + gbc_ref[...]
    g_row = gt_ref[...] + gbr_ref[...]
    b_col = _chunk_cumsum(_log_sigmoid(g_col), 0)
    b_row = _chunk_cumsum(_log_sigmoid(g_row), 1)

    tri = lax.broadcasted_iota(I32, (CHUNK, CHUNK), 1) <= lax.broadcasted_iota(I32, (CHUNK, CHUNK), 0)

    heads = range(ML_HEADS)
    for c in range(rows // CHUNK):
        lo, hi = c * CHUNK, (c + 1) * CHUNK
        bc = [b_col[lo:hi, _S_MF + hd:_S_MF + hd + 1] for hd in heads]
        lic = [g_col[lo:hi, _S_MI + hd:_S_MI + hd + 1] for hd in heads]
        br = [b_row[ML_HEADS + hd:ML_HEADS + hd + 1, lo:hi] for hd in heads]
        lir = [g_row[hd:hd + 1, lo:hi] for hd in heads]
        g_tot = [bc[hd][CHUNK - 1:CHUNK, :] for hd in heads]
        m_prev = [m_ref[hd][:, 0:1] for hd in heads]
        qh = [q_all[lo:hi, hd * ML_QK_DIM:(hd + 1) * ML_QK_DIM] for hd in heads]
        kh = [k_all[lo:hi, hd * ML_QK_DIM:(hd + 1) * ML_QK_DIM] for hd in heads]
        vh = [mv_ref[lo:hi, hd * ML_V_DIM:(hd + 1) * ML_V_DIM] for hd in heads]
        c_prev = [c_ref[hd] for hd in heads]
        n_prev = [n_ref[hd] for hd in heads]

        dmat = [jnp.where(tri, bc[hd] - br[hd] + lir[hd], -jnp.inf) for hd in heads]
        inter = [bc[hd] + m_prev[hd] for hd in heads]
        m_t = [jnp.maximum(inter[hd], jnp.max(dmat[hd], axis=-1, keepdims=True)) for hd in heads]
        w_intra = [jnp.exp(dmat[hd] - m_t[hd]) for hd in heads]
        a_inter = [jnp.exp(inter[hd] - m_t[hd]) for hd in heads]
        m_new = [jnp.maximum(g_tot[hd] + m_prev[hd],
                             jnp.max(g_tot[hd] - br[hd] + lir[hd], axis=-1, keepdims=True)) for hd in heads]
        a_state = [jnp.exp(g_tot[hd] + m_prev[hd] - m_new[hd]) for hd in heads]
        wk = [jnp.exp(g_tot[hd] - bc[hd] + lic[hd] - m_new[hd]) * kh[hd] for hd in heads]

        s_qk = [_dot_nt(qh[hd], kh[hd].astype(BF16)) * w_intra[hd] for hd in heads]
        read = [_dot(qh[hd], c_prev[hd].astype(BF16)) for hd in heads]
        d_c = [_dot(wk[hd].T.astype(BF16), vh[hd]) for hd in heads]
        num = [a_inter[hd] * read[hd] + _dot(s_qk[hd].astype(BF16), vh[hd]) for hd in heads]
        den = [a_inter[hd] * jnp.sum(qh[hd].astype(F32) * n_prev[hd], axis=-1, keepdims=True)
               + jnp.sum(s_qk[hd], axis=-1, keepdims=True) for hd in heads]
        for hd in heads:
            hs_ref[lo:hi, hd * ML_V_DIM:(hd + 1) * ML_V_DIM] = (
                num[hd] / jnp.maximum(jnp.abs(den[hd]), jnp.exp(-m_t[hd])))
        for hd in heads:
            c_ref[hd] = a_state[hd] * c_prev[hd] + d_c[hd]
            n_ref[hd] = a_state[hd] * n_prev[hd] + jnp.sum(wk[hd], axis=0, keepdims=True)
            m_ref[hd] = jnp.broadcast_to(m_new[hd], m_ref.shape[1:])

    for hd in range(ML_HEADS):
        sl = slice(hd * ML_V_DIM, (hd + 1) * ML_V_DIM)
        out_ref[:, sl] = (_rms(hs_ref[:, sl], ng_ref[:, sl]) * jax.nn.sigmoid(mo_ref[:, sl])).astype(BF16)


def _mlstm(mqk, mv, small, gt, mo, cw, cb, gbc, gbr, ng, rows):
    T = mqk.shape[0]
    row = lambda n: pl.BlockSpec((rows, n), lambda i: (i, 0))
    nv = ML_HEADS * ML_V_DIM
    return pl.pallas_call(
        functools.partial(_mlstm_body, rows=rows),
        grid=(T // rows,),
        in_specs=[row(mqk.shape[1]), row(nv), row(_SMALL), pl.BlockSpec((8, rows), lambda i: (1, i)), row(nv),
                  _resident(cw.shape), _resident(cb.shape), _resident(gbc.shape), _resident(gbr.shape),
                  _resident(ng.shape)],
        out_specs=row(nv),
        out_shape=jax.ShapeDtypeStruct((T, nv), BF16),
        scratch_shapes=[pltpu.VMEM((rows + 8, mqk.shape[1]), F32),
                        pltpu.VMEM((ML_HEADS, ML_QK_DIM, ML_V_DIM), F32),
                        pltpu.VMEM((ML_HEADS, 1, ML_QK_DIM), F32),
                        pltpu.VMEM((ML_HEADS, 1, 128), F32),
                        pltpu.VMEM((rows, nv), F32)],
        compiler_params=_params(),
        name="mlstm",
    )(mqk, mv, small, gt, mo, cw, cb, gbc, gbr, ng)


def _memfold_body(mem_ref, g_ref, wk_ref, wv_ref, wq_ref, wc_ref, wqk_ref, wvo_ref):
    mn = _rms(mem_ref[...], g_ref[...]).astype(BF16)
    k = _dot(mn, wk_ref[...].astype(BF16)).astype(BF16)
    v = _dot(mn, wv_ref[...].astype(BF16)).astype(BF16)
    wqk_ref[...] = (_dot_nt(wq_ref[...].astype(BF16), k) * (X_HEAD_DIM ** -0.5)).astype(BF16)
    wvo_ref[...] = _dot(v, wc_ref[...].astype(BF16)).astype(BF16)


def _memfold(mem, g, w_ckv, w_cq, w_co):
    M, D = mem.shape
    dh = X_HEAD_DIM
    return pl.pallas_call(
        _memfold_body,
        grid=(X_HEADS,),
        in_specs=[_resident(mem.shape), _resident(g.shape),
                  pl.BlockSpec((D, dh), lambda h: (0, h)), pl.BlockSpec((D, dh), lambda h: (0, X_HEADS + h)),
                  pl.BlockSpec((D, dh), lambda h: (0, h)), pl.BlockSpec((dh, D), lambda h: (h, 0))],
        out_specs=[pl.BlockSpec((D, M), lambda h: (0, h)), pl.BlockSpec((M, D), lambda h: (h, 0))],
        out_shape=[jax.ShapeDtypeStruct((D, X_HEADS * M), BF16), jax.ShapeDtypeStruct((X_HEADS * M, D), BF16)],
        compiler_params=_params(),
        name="memfold",
    )(mem, g, w_ckv, w_ckv, w_cq, w_co)


def _mixout_body(x_ref, dsa_ref, ml_ref, wo_ref, wqk_ref, wvo_ref, wr_ref, gx_ref, gf_ref,
                 x2_ref, hf_ref, rl_ref, p_ref):
    nd = dsa_ref.shape[1]
    x1 = x_ref[...] + _dot(dsa_ref[...], wo_ref[0:nd, :]) + _dot(ml_ref[...], wo_ref[nd:, :])
    lg_all = _dot(_rms(x1, gx_ref[...]).astype(BF16), wqk_ref[...])
    n_mem = wqk_ref.shape[1] // X_HEADS
    for hd in range(X_HEADS):
        sl = slice(hd * n_mem, (hd + 1) * n_mem)
        lg = lg_all[:, sl]
        e = jnp.exp(lg - jnp.max(lg, axis=-1, keepdims=True))
        p_ref[:, sl] = (e / jnp.sum(e, axis=-1, keepdims=True)).astype(BF16)
    x2 = x1 + _dot(p_ref[...], wvo_ref[...])
    x2_ref[...] = x2
    hf = _rms(x2, gf_ref[...]).astype(BF16)
    rl_ref[...] = _dot(hf, wr_ref[...])
    bits = lax.bitcast_convert_type(hf.astype(F32), U32)
    for c in range(_SUBLANES):
        lo = bits[:, (2 * c) * _LANES:(2 * c + 1) * _LANES]
        hi = bits[:, (2 * c + 1) * _LANES:(2 * c + 2) * _LANES]
        hf_ref[pl.ds(c, x2.shape[0], stride=_SUBLANES), :] = (hi & jnp.uint32(0xFFFF0000)) | (lo >> 16)


def _mixout(x, dsa, ml, wo, wqk, wvo, wr, gx, gf, tm):
    T = x.shape[0]
    row = lambda n: pl.BlockSpec((tm, n), lambda i: (i, 0))
    return pl.pallas_call(
        _mixout_body,
        grid=(T // tm,),
        in_specs=[row(D_MODEL), row(dsa.shape[1]), row(ml.shape[1]), _resident(wo.shape),
                  _resident(wqk.shape), _resident(wvo.shape), _resident(wr.shape), _resident(gx.shape),
                  _resident(gf.shape)],
        out_specs=[row(D_MODEL), pl.BlockSpec((tm * _SUBLANES, _LANES), lambda i: (i, 0)), row(wr.shape[1])],
        out_shape=[jax.ShapeDtypeStruct((T, D_MODEL), F32),
                   jax.ShapeDtypeStruct((T * _SUBLANES, _LANES), U32),
                   jax.ShapeDtypeStruct((T, wr.shape[1]), F32)],
        scratch_shapes=[pltpu.VMEM((tm, wqk.shape[1]), BF16)],
        compiler_params=_params(),
        name="mixout",
    )(x, dsa, ml, wo, wqk, wvo, wr, gx, gf)


def _moe_body(tok_ref, eb_ref, hf_hbm, wg_ref, wu_ref, wd_ref, y_hbm,
              xbuf, ybuf, wgb, wub, wdb, gsem, ysem, *, bm, n_blk):
    e = pl.program_id(0)
    n_valid = eb_ref[N_EXPERTS]
    b_lo = eb_ref[e]
    b_hi = eb_ref[e + 1]

    def tok_words(tok):
        return hf_hbm.at[pl.ds(pl.multiple_of(tok * _SUBLANES, _SUBLANES), _SUBLANES)]

    def gather(blk, s):
        def issue(g, carry):
            for u in range(_SUBLANES):
                r = g * _SUBLANES + u
                pltpu.make_async_copy(tok_words(tok_ref[blk * bm + r]),
                                      xbuf.at[s, pl.ds(pl.multiple_of(r * _SUBLANES, _SUBLANES), _SUBLANES)],
                                      gsem.at[s]).start(priority=1)
            return carry
        lax.fori_loop(0, bm // _SUBLANES, issue, 0)

    def wait_gather(s):
        pltpu.make_async_copy(xbuf.at[s], xbuf.at[s], gsem.at[s]).wait()

    def y_copy(blk, s):
        return pltpu.make_async_copy(ybuf.at[s], y_hbm.at[pl.ds(pl.multiple_of(blk * bm, bm), bm)], ysem.at[s])

    @pl.when(e == 0)
    def _():
        for k in range(_GATHER_AHEAD):
            gather(k, k)

    @pl.when(b_hi > b_lo)
    def _():
        wgb[...] = wg_ref[0].astype(BF16)
        wub[...] = wu_ref[0].astype(BF16)
        wdb[...] = wd_ref[0].astype(BF16)

        def block(b, carry):
            s = b % _GATHER_SLOTS
            wait_gather(s)
            chunks = []
            for c in range(_SUBLANES):
                w = xbuf[s, pl.ds(c, bm, stride=_SUBLANES), :]
                chunks.append(lax.bitcast_convert_type(w << 16, F32).astype(BF16))
                chunks.append(lax.bitcast_convert_type(w & jnp.uint32(0xFFFF0000), F32).astype(BF16))
            xb = jnp.concatenate(chunks, axis=-1)
            gate = _dot(xb, wgb[...])
            a = gate * jax.nn.sigmoid(gate) * _dot(xb, wub[...])
            y = _dot(a.astype(BF16), wdb[...])

            nxt = jnp.minimum(b + _GATHER_AHEAD, n_blk - 1)
            for r in range(bm):
                pltpu.make_async_copy(tok_words(tok_ref[nxt * bm + r]),
                                      xbuf.at[(b + _GATHER_AHEAD) % _GATHER_SLOTS, pl.ds(r * _SUBLANES, _SUBLANES)],
                                      gsem.at[(b + _GATHER_AHEAD) % _GATHER_SLOTS]).start(priority=1)

            @pl.when(b >= 2)
            def _():
                y_copy(b - 2, b % 2).wait()

            ybuf[b % 2] = y
            y_copy(b, b % 2).start()
            return carry

        lax.fori_loop(b_lo, b_hi, block, 0)

    @pl.when(e == pl.num_programs(0) - 1)
    def _():
        for k in range(_GATHER_AHEAD):
            wait_gather((n_valid + k) % _GATHER_SLOTS)

        @pl.when(n_valid >= 2)
        def _():
            y_copy(n_valid - 2, n_valid % 2).wait()
        y_copy(n_valid - 1, (n_valid - 1) % 2).wait()
        ybuf[0] = jnp.zeros(ybuf.shape[1:], F32)

        def zero_block(b, carry):
            cp = y_copy(b, 0)
            cp.start()
            cp.wait()
            return carry

        lax.fori_loop(n_valid, n_blk, zero_block, 0)


def _moe(tok, e_blk, hf, wg, wu, wd, n_blk, bm):
    D = wg.shape[1]
    wspec = lambda shape: pl.BlockSpec((1,) + shape, lambda e, *_: (e, 0, 0))
    grid_spec = pltpu.PrefetchScalarGridSpec(
        num_scalar_prefetch=2,
        grid=(N_EXPERTS,),
        in_specs=[pl.BlockSpec(memory_space=pl.ANY),
                  wspec((D, D_EXPERT)), wspec((D, D_EXPERT)), wspec((D_EXPERT, D))],
        out_specs=pl.BlockSpec(memory_space=pl.ANY),
        scratch_shapes=[pltpu.VMEM((_GATHER_SLOTS, bm * _SUBLANES, _LANES), U32),
                        pltpu.VMEM((2, bm, D), F32),
                        pltpu.VMEM((D, D_EXPERT), BF16), pltpu.VMEM((D, D_EXPERT), BF16),
                        pltpu.VMEM((D_EXPERT, D), BF16),
                        pltpu.SemaphoreType.DMA((_GATHER_SLOTS,)), pltpu.SemaphoreType.DMA((2,))],
    )
    return pl.pallas_call(
        functools.partial(_moe_body, bm=bm, n_blk=n_blk),
        grid_spec=grid_spec,
        out_shape=jax.ShapeDtypeStruct((n_blk * bm, D), F32),
        compiler_params=_params(),
        name="moe",
    )(tok, e_blk, hf, wg, wu, wd)


def _route(rl, b_group, b_router, bm):
    N = rl.shape[0]
    g_logits = rl[:, :N_GROUPS] + b_group
    g_prob = jax.nn.softmax(g_logits, axis=-1)
    g_sel = jnp.argmax(g_logits, axis=-1)
    p_g = jnp.take_along_axis(g_prob, g_sel[:, None], axis=-1)
    e_logits = rl[:, N_GROUPS:N_GROUPS + N_EXPERTS] + b_router
    in_group = (jnp.arange(N_EXPERTS, dtype=I32) // EXP_PER_GROUP)[None, :] == g_sel[:, None]
    e_prob = jax.nn.softmax(jnp.where(in_group, e_logits, -jnp.inf), axis=-1)
    e_cand = jnp.where(in_group, e_prob, -1.0)
    tops = []
    for _ in range(TOPK_IN_GROUP):
        best = jnp.argmax(e_cand, axis=-1).astype(I32)
        tops.append((jnp.max(e_cand, axis=-1), best))
        e_cand = jnp.where(jnp.arange(N_EXPERTS, dtype=I32)[None, :] == best[:, None], -2.0, e_cand)
    top_p = jnp.stack([p for p, _ in tops], axis=-1)
    expert_id = jnp.stack([e for _, e in tops], axis=-1)
    gates = p_g * top_p / top_p.sum(-1, keepdims=True)

    A = N * TOPK_IN_GROUP
    flat_e = expert_id.reshape(A)
    onehot = (flat_e[:, None] == jnp.arange(N_EXPERTS, dtype=I32)[None, :])
    seg = _RANK_SEG if A % _RANK_SEG == 0 else A
    oh = onehot.astype(BF16).reshape(A // seg, seg, N_EXPERTS)
    before = (jnp.arange(seg)[None, :] < jnp.arange(seg)[:, None]).astype(BF16)
    within = jnp.einsum('ij,bjk->bik', before, oh, preferred_element_type=F32)
    seg_tot = jnp.sum(oh.astype(F32), axis=1)
    seg_base = jnp.cumsum(seg_tot, axis=0) - seg_tot
    rank = jnp.sum((within + seg_base[:, None, :]) * oh.astype(F32), axis=-1).reshape(A).astype(I32)
    counts = jnp.sum(seg_tot, axis=0).astype(I32)
    padded = (counts + bm - 1) // bm * bm
    pad_ends = jnp.cumsum(padded)
    pad_starts = pad_ends - padded
    row = jnp.sum(jnp.where(onehot, pad_starts[None, :], 0), axis=1) + rank
    n_blk = -(-A // bm) + N_EXPERTS
    row_tok = jnp.zeros((n_blk * bm,), I32).at[row].set(jnp.arange(A, dtype=I32) // TOPK_IN_GROUP)
    e_blk = jnp.concatenate([pad_starts, pad_ends[-1:]]) // bm
    return row_tok, row, gates, e_blk.astype(I32), n_blk


def _final_body(row_ref, x_ref, gate_ref, y_hbm, g_ref, out_ref, ybuf, sem, *, tm):
    i = pl.program_id(0)
    slot = i % 2

    def gather(tile, s):
        def issue(g, carry):
            for u in range(_SUBLANES):
                for k in range(TOPK_IN_GROUP):
                    src = row_ref[(tile * tm + g * _SUBLANES + u) * TOPK_IN_GROUP + k]
                    pltpu.make_async_copy(y_hbm.at[pl.ds(src, 1)], ybuf.at[s, k, g, pl.ds(u, 1)],
                                          sem.at[s]).start()
            return carry
        lax.fori_loop(0, tm // _SUBLANES, issue, 0)

    @pl.when(i == 0)
    def _():
        gather(0, 0)

    @pl.when(i + 1 < pl.num_programs(0))
    def _():
        gather(i + 1, 1 - slot)

    pltpu.make_async_copy(ybuf.at[slot], ybuf.at[slot], sem.at[slot]).wait()
    acc = x_ref[...]
    for k in range(TOPK_IN_GROUP):
        acc = acc + gate_ref[:, k:k + 1] * ybuf[slot, k].reshape(tm, ybuf.shape[-1])
    out_ref[...] = _rms(acc, g_ref[...])


def _final(row, x2, gates, y_rows, g, tm):
    T, D = x2.shape
    grid_spec = pltpu.PrefetchScalarGridSpec(
        num_scalar_prefetch=1,
        grid=(T // tm,),
        in_specs=[pl.BlockSpec((tm, D), lambda i, *_: (i, 0)),
                  pl.BlockSpec((tm, TOPK_IN_GROUP), lambda i, *_: (i, 0)),
                  pl.BlockSpec(memory_space=pl.ANY),
                  pl.BlockSpec(g.shape, lambda i, *_: (0, 0))],
        out_specs=pl.BlockSpec((tm, D), lambda i, *_: (i, 0)),
        scratch_shapes=[pltpu.VMEM((2, TOPK_IN_GROUP, tm // _SUBLANES, _SUBLANES, D), F32),
                        pltpu.SemaphoreType.DMA((2,))],
    )
    return pl.pallas_call(
        functools.partial(_final_body, tm=tm),
        grid_spec=grid_spec,
        out_shape=jax.ShapeDtypeStruct((T, D), F32),
        compiler_params=_params(),
        name="final",
    )(row, x2, gates, y_rows, g)


def _tile_sizes(T):
    pick = lambda want: want if T % want == 0 else CHUNK
    return dict(inproj=pick(256), dsa_q=pick(256), dsa_k=pick(512), mlstm=pick(256), mixout=pick(256),
                final=pick(256), moe=128)


def _layer(x, mem, norm_mix_g, w_in, kv_norm_g, k_idx_norm_g, w_uk, w_uv, conv_w, conv_b, gate_b, ml_norm_g,
           w_out, norm_x_g, mem_norm_g, w_cq, w_ckv, w_co, norm_ffn_g, w_group, b_group, w_router, b_router,
           w_gate, w_up, w_down, out_g):
    T = x.shape[0]
    ts = _tile_sizes(T)
    r2 = lambda v: v.reshape(1, -1)

    w_r = _wprep(w_in, 256)
    wuk_t = jnp.transpose(w_uk, (1, 2, 0)).astype(BF16)
    wuv_t = jnp.transpose(w_uv, (1, 0, 2)).astype(BF16)

    qabs, ckv, qi, kidx, small, mqk, mv, mo = _inproj(
        x, r2(norm_mix_g), w_r, wuk_t, r2(kv_norm_g), r2(k_idx_norm_g), ts["inproj"])

    gate_rows = jnp.transpose(small[:, _S_WI:_S_MF + ML_HEADS])
    dsa_out = _dsa(qi, gate_rows, qabs, kidx, ckv, wuv_t, ts["dsa_q"], ts["dsa_k"])

    gb_col = jnp.zeros((1, _SMALL), F32).at[0, _S_MI:_S_MI + 2 * ML_HEADS].set(gate_b)
    ml_out = _mlstm(mqk, mv, small, gate_rows, mo, conv_w, r2(conv_b), gb_col, gate_b.reshape(-1, 1),
                    r2(ml_norm_g), ts["mlstm"])

    wqk, wvo = _memfold(mem, r2(mem_norm_g), w_ckv, w_cq, w_co)
    w_rt = jnp.concatenate([w_group, w_router,
                            jnp.zeros((D_MODEL, 128 - N_GROUPS - N_EXPERTS), w_group.dtype)], axis=1)
    x2, hf, rl = _mixout(x, dsa_out, ml_out, w_out.astype(BF16), wqk, wvo, w_rt.astype(BF16),
                         r2(norm_x_g), r2(norm_ffn_g), ts["mixout"])

    bm = ts["moe"]
    row_tok, row, gates, e_blk, n_blk = _route(rl, b_group, b_router, bm)
    y_rows = _moe(row_tok, e_blk, hf, w_gate, w_up, w_down, n_blk, bm)
    return _final(row, x2, gates, y_rows, r2(out_g), ts["final"])


def kernel(x, mem, norm_mix_g, w_in, kv_norm_g, k_idx_norm_g, w_uk, w_uv, conv_w, conv_b, gate_b, ml_norm_g,
           w_out, norm_x_g, mem_norm_g, w_cq, w_ckv, w_co, norm_ffn_g, w_group, b_group, w_router, b_router,
           w_gate, w_up, w_down, final_norm_g):
    B, T, D = x.shape
    assert B == 1 and D == D_MODEL and norm_mix_g.shape[0] == 1 and T % CHUNK == 0
    out = _layer(x[0], mem[0], norm_mix_g[0], w_in, kv_norm_g[0], k_idx_norm_g[0], w_uk[0], w_uv[0],
                 conv_w[0], conv_b[0], gate_b[0], ml_norm_g[0], w_out[0], norm_x_g[0], mem_norm_g[0],
                 w_cq[0], w_ckv[0], w_co[0], norm_ffn_g[0], w_group[0], b_group[0], w_router[0], b_router[0],
                 w_gate[0], w_up[0], w_down[0], final_norm_g)
    return out[None]
```

```python
import functools

import jax
import jax.numpy as jnp
import numpy as np
from jax import lax
from jax.experimental import pallas as pl
from jax.experimental.pallas import tpu as pltpu

F32 = jnp.float32
BF16 = jnp.bfloat16
I32 = jnp.int32
I16 = jnp.int16
U32 = jnp.uint32

EPS = 1e-6
CHUNK = 64
D_MODEL = 2048

DSA_HEADS = 8
DSA_HEAD_DIM = 128
DSA_LATENT = 256
IDX_HEADS = 8
IDX_DIM = 64
TOPK_MAX = 256

ML_HEADS = 4
ML_QK_DIM = 128
ML_V_DIM = 256
CONV_W = 4

X_HEADS = 4
X_HEAD_DIM = D_MODEL // X_HEADS

N_GROUPS = 4
EXP_PER_GROUP = 8
N_EXPERTS = N_GROUPS * EXP_PER_GROUP
TOPK_IN_GROUP = 2
D_EXPERT = 512

_O_DQ = 0
_O_CKV = _O_DQ + DSA_HEADS * DSA_HEAD_DIM
_O_QI = _O_CKV + DSA_LATENT
_O_KI = _O_QI + IDX_HEADS * IDX_DIM
_O_WI = _O_KI + IDX_DIM
_O_MQ = _O_WI + IDX_HEADS
_O_MK = _O_MQ + ML_HEADS * ML_QK_DIM
_O_MV = _O_MK + ML_HEADS * ML_QK_DIM
_O_MI = _O_MV + ML_HEADS * ML_V_DIM
_O_MF = _O_MI + ML_HEADS
_O_MO = _O_MF + ML_HEADS
_O_END = _O_MO + ML_HEADS * ML_V_DIM

_G_DQ = (0, 1024)
_G_CKV = (1024, 1280)
_G_QI = (1280, 1792)
_G_SMALL = (1792, 1920)
_G_MQK = (1920, 2944)
_G_MV = (2944, 3968)
_G_MO = (3968, 4992)
_W_COLS = 4992
_S_WI = IDX_DIM
_S_MI = _S_WI + IDX_HEADS
_S_MF = _S_MI + ML_HEADS
_SMALL = 128

_VMEM_LIMIT = 56 * 1024 * 1024
_INT_MIN = -(2 ** 31)
_I16_MIN = -(2 ** 15)
_CHUNK_SHIFT = CHUNK.bit_length() - 1
_LOG2E = 1.4426950408889634
_SUBLANES = 8
_LANES = 128
_GATHER_SLOTS = 9
_GATHER_AHEAD = _GATHER_SLOTS - 1
_RANK_SEG = 512
_NEG = -1e30


def _rms(v, g):
    return v * lax.rsqrt(jnp.mean(v * v, axis=-1, keepdims=True) + EPS) * g


def _dot(a, b):
    return jnp.dot(a, b, preferred_element_type=F32)


def _dot_nt(a, b):
    return lax.dot_general(a, b, (((1,), (1,)), ((), ())), preferred_element_type=F32)


def _resident(shape):
    nd = len(shape)
    return pl.BlockSpec(shape, lambda *_: (0,) * nd, pipeline_mode=pl.Buffered(1))


def _params(n_axes=1):
    return pltpu.CompilerParams(dimension_semantics=("arbitrary",) * n_axes,
                                vmem_limit_bytes=_VMEM_LIMIT)


def _wprep_body(w_hbm, out_ref, wbuf, sem, *, tk):
    i = pl.program_id(0)
    slot = i % 2

    def rows(blk, s):
        return pltpu.make_async_copy(w_hbm.at[0, pl.ds(pl.multiple_of(blk * tk, tk), tk)], wbuf.at[s], sem.at[s])

    @pl.when(i == 0)
    def _():
        rows(0, 0).start()

    @pl.when(i + 1 < pl.num_programs(0))
    def _():
        rows(i + 1, 1 - slot).start()

    rows(i, slot).wait()
    w_ref = wbuf.at[slot]
    n_small = IDX_DIM + IDX_HEADS
    out_ref[:, _G_DQ[0]:_G_SMALL[0] + n_small] = w_ref[:, _O_DQ:_O_MQ].astype(BF16)
    out_ref[:, _G_SMALL[0] + n_small:_G_SMALL[0] + n_small + 2 * ML_HEADS] = w_ref[:, _O_MI:_O_MO].astype(BF16)
    out_ref[:, _G_SMALL[0] + n_small + 2 * ML_HEADS:_G_SMALL[1]] = jnp.zeros(
        (out_ref.shape[0], _SMALL - n_small - 2 * ML_HEADS), BF16)
    out_ref[:, _G_MQK[0]:_G_MV[1]] = w_ref[:, _O_MQ:_O_MI].astype(BF16)
    out_ref[:, _G_MO[0]:_G_MO[1]] = w_ref[:, _O_MO:_O_END].astype(BF16)


def _wprep(w_in, tk):
    K = w_in.shape[1]
    return pl.pallas_call(
        functools.partial(_wprep_body, tk=tk),
        grid=(K // tk,),
        in_specs=[pl.BlockSpec(memory_space=pl.ANY)],
        out_specs=pl.BlockSpec((tk, _W_COLS), lambda i: (i, 0)),
        out_shape=jax.ShapeDtypeStruct((K, _W_COLS), BF16),
        scratch_shapes=[pltpu.VMEM((2, tk, w_in.shape[2]), F32), pltpu.SemaphoreType.DMA((2,))],
        compiler_params=_params(),
        name="wprep",
    )(w_in)


def _inproj_body(x_ref, g_ref, w_ref, wuk_ref, kvg_ref, kig_ref,
                 qabs_ref, ckv_ref, qi_ref, kidx_ref, small_ref, mqk_ref, mv_ref, mo_ref):
    h = _rms(x_ref[...], g_ref[...]).astype(BF16)

    def proj(grp):
        return _dot(h, w_ref[:, grp[0]:grp[1]])

    dq = proj(_G_DQ)
    for hd in range(DSA_HEADS):
        qh = dq[:, hd * DSA_HEAD_DIM:(hd + 1) * DSA_HEAD_DIM].astype(BF16)
        qa = _dot(qh, wuk_ref[hd]) * (DSA_HEAD_DIM ** -0.5 * _LOG2E)
        qabs_ref[:, hd * DSA_LATENT:(hd + 1) * DSA_LATENT] = qa.astype(BF16)
    ckv_ref[...] = _rms(proj(_G_CKV), kvg_ref[...]).astype(BF16)
    qi_ref[...] = (proj(_G_QI) * (IDX_DIM ** -0.5)).astype(BF16)
    small = proj(_G_SMALL)
    small_ref[...] = small
    kidx_ref[...] = _rms(small[:, :IDX_DIM], kig_ref[...]).astype(BF16)
    mqk_ref[...] = proj(_G_MQK)
    mv_ref[...] = proj(_G_MV).astype(BF16)
    mo_ref[...] = proj(_G_MO)


def _inproj(x, g, w, wuk, kvg, kig, tm):
    T = x.shape[0]
    row = lambda n: pl.BlockSpec((tm, n), lambda i: (i, 0))
    outs = [(8 * DSA_LATENT, BF16), (DSA_LATENT, BF16), (IDX_HEADS * IDX_DIM, BF16), (IDX_DIM, BF16),
            (_SMALL, F32), (2 * ML_HEADS * ML_QK_DIM, F32), (ML_HEADS * ML_V_DIM, BF16),
            (ML_HEADS * ML_V_DIM, F32)]
    return pl.pallas_call(
        _inproj_body,
        grid=(T // tm,),
        in_specs=[row(D_MODEL), _resident(g.shape), _resident(w.shape), _resident(wuk.shape),
                  _resident(kvg.shape), _resident(kig.shape)],
        out_specs=[row(n) for n, _ in outs],
        out_shape=[jax.ShapeDtypeStruct((T, n), dt) for n, dt in outs],
        compiler_params=_params(),
        name="inproj",
    )(x, g, w, wuk, kvg, kig)


def _sublane_fold(v, op, rows=_SUBLANES, ways=4):
    groups = [v[r * rows:(r + 1) * rows, :] for r in range(v.shape[0] // rows)]
    accs = groups[:ways]
    for r in range(ways, len(groups)):
        accs[r % ways] = op(accs[r % ways], groups[r])
    while len(accs) > 1:
        accs = [op(accs[k], accs[k + 1]) if k + 1 < len(accs) else accs[k] for k in range(0, len(accs), 2)]
    return accs[0]


def _bit_transpose32(words):
    a = list(words)
    j, m = 16, 0x0000FFFF
    while j:
        k = 0
        while k < 32:
            t = (a[k] ^ (a[k + j] >> j)) & jnp.uint32(m)
            a[k] = a[k] ^ t
            a[k + j] = a[k + j] ^ (t << j)
            k = (k + j + 1) & ~j
        j >>= 1
        m = (m ^ (m << j)) & 0xFFFFFFFF
    return a


def _dsa_body(qi_ref, wrow_ref, qabs_ref, kidx_ref, ckv_ref, ckvt_ref, wuv_ref, out_ref,
              key_ref, planes_ref, alive_ref, lga_ref, lgb_ref, m_ref, l_ref, acc_ref, *, tq, tk, topk, nbits_idx):
    i = pl.program_id(0)
    n_kb = ((i + 1) * tq + tk - 1) // tk
    w_rows = wrow_ref[0:IDX_HEADS, :] * (IDX_HEADS ** -0.5)
    q_chunk = (i * tq + lax.broadcasted_iota(I32, (1, tq), 1)) >> _CHUNK_SHIFT

    def key_pos(j):
        return j * tk + lax.broadcasted_iota(I32, (tk, 1), 0)

    def score_block(j, carry):
        kx = kidx_ref[pl.ds(pl.multiple_of(j * tk, tk), tk), :]
        s = jnp.zeros((tk, tq), F32)
        for hd in range(IDX_HEADS):
            d = _dot_nt(kx, qi_ref[:, hd * IDX_DIM:(hd + 1) * IDX_DIM])
            s = s + w_rows[hd:hd + 1, :] * jnp.maximum(d, 0.0)
        bits = lax.bitcast_convert_type(s, I32)
        key = bits ^ ((bits >> 31) & 0x7FFFFFFF)
        key_ref[j] = jnp.where((key_pos(j) >> _CHUNK_SHIFT) <= q_chunk, key, _INT_MIN)
        return carry

    lax.fori_loop(0, n_kb, score_block, 0)

    def count(pred):
        def body(j, acc):
            hit = pred(key_ref[j], key_pos(j)).astype(I32)
            return acc + _sublane_fold(hit, jnp.add)
        acc = lax.fori_loop(0, n_kb, body, jnp.zeros((8, tq), I32))
        return jnp.sum(acc, axis=0, keepdims=True)

    plane_rows = tk // 32

    @pl.when(i == 0)
    def _():
        planes_ref[...] = jnp.zeros(planes_ref.shape, U32)

    def slice_block(j, carry):
        u = lax.bitcast_convert_type(key_ref[j], U32) ^ jnp.uint32(0x80000000)
        for h in range(plane_rows // _SUBLANES):
            base = h * 32 * _SUBLANES
            planes = _bit_transpose32([u[base + _SUBLANES * v:base + _SUBLANES * (v + 1), :] for v in range(32)])
            row0 = pl.multiple_of(j * plane_rows + h * _SUBLANES, _SUBLANES)
            for b in range(32):
                planes_ref[b, pl.ds(row0, _SUBLANES), :] = planes[31 - b]
        return carry

    lax.fori_loop(0, n_kb, slice_block, 0)
    word_row = lax.broadcasted_iota(I32, (alive_ref.shape[0], 1), 0)
    alive_ref[...] = jnp.where(word_row < n_kb * plane_rows, jnp.full(alive_ref.shape, 0xFFFFFFFF, U32),
                               jnp.uint32(0))

    def radix_select(n_rows):
        def ones_count(words):
            return jnp.sum(_sublane_fold(lax.population_count(words).astype(I32), jnp.add), axis=0, keepdims=True)

        def select_bit(it, carry):
            need, t_u = carry
            b = 31 - it
            alive = alive_ref[0:n_rows, :]
            ones = alive & planes_ref[b, pl.ds(0, n_rows), :]
            n_ones = ones_count(ones)
            take = n_ones >= need
            alive_ref[0:n_rows, :] = jnp.where(take, ones, alive ^ ones)
            bit = lax.shift_left(jnp.uint32(1), jnp.asarray(b, U32))
            return jnp.where(take, need, need - n_ones), jnp.where(take, t_u | bit, t_u)

        need, t_u = lax.fori_loop(0, 32, select_bit, (jnp.full((1, tq), topk, I32), jnp.zeros((1, tq), U32)))
        return need, t_u, ones_count(alive_ref[0:n_rows, :])

    all_rows = alive_ref.shape[0]
    if all_rows % (2 * _SUBLANES) == 0:
        need, t_u, n_equal = lax.cond(n_kb * plane_rows <= all_rows // 2,
                                      lambda: radix_select(all_rows // 2), lambda: radix_select(all_rows))
    else:
        need, t_u, n_equal = radix_select(all_rows)
    short = t_u == 0
    t = jnp.maximum(lax.bitcast_convert_type(t_u ^ jnp.uint32(0x80000000), I32), _INT_MIN + 1)
    all_pos = jnp.int32(2 ** nbits_idx - 1)
    has_ties = (n_equal > need) & jnp.logical_not(short)
    n_tie_take = jnp.where(has_ties, need, all_pos)

    def tie_cutoff():
        def pos_bit(b, c):
            cand = c + lax.shift_left(jnp.int32(1), nbits_idx - 1 - b)
            f = count(lambda kb, pos: (kb == t) & (pos < cand))
            return jnp.where(f <= n_tie_take, cand, c)
        return lax.fori_loop(0, nbits_idx, pos_bit, jnp.zeros((1, tq), I32))

    cut = lax.cond(jnp.max(has_ties.astype(I32)) > 0, tie_cutoff, lambda: jnp.full((1, tq), all_pos, I32))

    m_ref[...] = jnp.full(m_ref.shape, _NEG, F32)
    l_ref[...] = jnp.zeros(l_ref.shape, F32)
    acc_ref[...] = jnp.zeros(acc_ref.shape, F32)
    last_blk = ckvt_ref.shape[0] - 1

    def logits(j, lg_buf):
        c_blk = ckv_ref[pl.ds(pl.multiple_of(jnp.minimum(j, last_blk) * tk, tk), tk), :]
        kb = key_ref[j]
        sel = (kb > t) | ((kb == t) & (key_pos(j) < cut))
        bias = jnp.where(sel, 0.0, _NEG).astype(F32)
        for hd in range(DSA_HEADS):
            lg_buf[hd] = _dot_nt(c_blk, qabs_ref[:, hd * DSA_LATENT:(hd + 1) * DSA_LATENT]) + bias

    def accumulate(j, lg_buf):
        c_blk_t = ckvt_ref[jnp.minimum(j, last_blk)]
        for hd in range(DSA_HEADS):
            lg = lg_buf[hd]
            m_old = m_ref[hd:hd + 1, :]
            m_new = jnp.maximum(m_old, jnp.max(_sublane_fold(lg, jnp.maximum, ways=1), axis=0, keepdims=True))
            p = jnp.exp2(lg - m_new)
            alpha = jnp.exp2(m_old - m_new)
            l_ref[hd:hd + 1, :] = alpha * l_ref[hd:hd + 1, :] + jnp.sum(_sublane_fold(p, jnp.add, ways=1), axis=0,
                                                                         keepdims=True)
            acc_ref[hd] = alpha * acc_ref[hd] + _dot(c_blk_t, p.astype(BF16))
            m_ref[hd:hd + 1, :] = m_new

    logits(0, lga_ref)

    def attn_pair(mi, carry):
        ja = 2 * mi
        accumulate(ja, lga_ref)
        logits(ja + 1, lgb_ref)
        accumulate(ja + 1, lgb_ref)
        logits(jnp.minimum(ja + 2, n_kb - 1), lga_ref)
        return carry

    lax.fori_loop(0, n_kb // 2, attn_pair, 0)

    @pl.when(n_kb % 2 == 1)
    def _():
        accumulate(n_kb - 1, lga_ref)

    for hd in range(DSA_HEADS):
        o_lat = (acc_ref[hd] / l_ref[hd:hd + 1, :]).T.astype(BF16)
        out_ref[:, hd * DSA_HEAD_DIM:(hd + 1) * DSA_HEAD_DIM] = _dot(o_lat, wuv_ref[hd]).astype(BF16)


def _dsa(qi, wrows, qabs, kidx, ckv, wuv, tq, tk):
    T = qi.shape[0]
    topk = min(TOPK_MAX, T // 4)
    n_kb = T // tk
    ckvt = jnp.transpose(ckv.reshape(n_kb, tk, DSA_LATENT), (0, 2, 1))
    row = lambda n: pl.BlockSpec((tq, n), lambda i: (i, 0))
    body = functools.partial(_dsa_body, tq=tq, tk=tk, topk=topk, nbits_idx=int(T).bit_length())
    return pl.pallas_call(
        body,
        grid=(T // tq,),
        in_specs=[row(qi.shape[1]), pl.BlockSpec((wrows.shape[0], tq), lambda i: (0, i)), row(qabs.shape[1]),
                  _resident(kidx.shape), _resident(ckv.shape), _resident(ckvt.shape), _resident(wuv.shape)],
        out_specs=row(DSA_HEADS * DSA_HEAD_DIM),
        out_shape=jax.ShapeDtypeStruct((T, DSA_HEADS * DSA_HEAD_DIM), BF16),
        scratch_shapes=[pltpu.VMEM((n_kb + n_kb % 2, tk, tq), I32),
                        pltpu.VMEM((32, T // 32, tq), U32), pltpu.VMEM((T // 32, tq), U32),
                        pltpu.VMEM((DSA_HEADS, tk, tq), F32), pltpu.VMEM((DSA_HEADS, tk, tq), F32),
                        pltpu.VMEM((DSA_HEADS, tq), F32),
                        pltpu.VMEM((DSA_HEADS, tq), F32), pltpu.VMEM((DSA_HEADS, DSA_LATENT, tq), F32)],
        compiler_params=_params(),
        name="dsa",
    )(qi, wrows, qabs, kidx, ckv, ckvt, wuv)


def _log_sigmoid(v):
    return jnp.minimum(v, 0.0) - jnp.log1p(jnp.exp(-jnp.abs(v)))


def _chunk_cumsum(v, axis):
    pos = lax.broadcasted_iota(I32, v.shape, axis) & (CHUNK - 1)
    d = 1
    while d < CHUNK:
        v = v + jnp.where(pos >= d, pltpu.roll(v, d, axis=axis), 0.0)
        d *= 2
    return v


def _mlstm_body(mqk_ref, mv_ref, small_ref, gt_ref, mo_ref, cw_ref, cb_ref, gbc_ref, gbr_ref, ng_ref,
                out_ref, xe_ref, c_ref, n_ref, m_ref, hs_ref, *, rows):
    @pl.when(pl.program_id(0) == 0)
    def _():
        xe_ref[0:8, :] = jnp.zeros((8, xe_ref.shape[1]), F32)
        c_ref[...] = jnp.zeros(c_ref.shape, F32)
        n_ref[...] = jnp.zeros(n_ref.shape, F32)
        m_ref[...] = jnp.zeros(m_ref.shape, F32)

    x = mqk_ref[...]
    xe_ref[8:8 + rows, :] = x
    y = cb_ref[...]
    for j in range(CONV_W - 1):
        y = y + xe_ref[5 + j:5 + j + rows, :] * cw_ref[j:j + 1, :]
    y = y + x * cw_ref[CONV_W - 1:CONV_W, :]
    xe_ref[0:8, :] = x[rows - 8:rows, :]
    qk = y * jax.nn.sigmoid(y)
    nqk = ML_HEADS * ML_QK_DIM
    q_all = (qk[:, :nqk] * (ML_QK_DIM ** -0.5)).astype(BF16)
    k_all = qk[:, nqk:]

    g_col = small_ref[...] + gbc_ref[...]
    g_row = gt_ref[...] + gbr_ref[...]
    b_col = _chunk_cumsum(_log_sigmoid(g_col), 0)
    b_row = _chunk_cumsum(_log_sigmoid(g_row), 1)

    tri = lax.broadcasted_iota(I32, (CHUNK, CHUNK), 1) <= lax.broadcasted_iota(I32, (CHUNK, CHUNK), 0)

    heads = range(ML_HEADS)
    for c in range(rows // CHUNK):
        lo, hi = c * CHUNK, (c + 1) * CHUNK
        bc = [b_col[lo:hi, _S_MF + hd:_S_MF + hd + 1] for hd in heads]
        lic = [g_col[lo:hi, _S_MI + hd:_S_MI + hd + 1] for hd in heads]
        br = [b_row[ML_HEADS + hd:ML_HEADS + hd + 1, lo:hi] for hd in heads]
        lir = [g_row[hd:hd + 1, lo:hi] for hd in heads]
        g_tot = [bc[hd][CHUNK - 1:CHUNK, :] for hd in heads]
        m_prev = [m_ref[hd][:, 0:1] for hd in heads]
        qh = [q_all[lo:hi, hd * ML_QK_DIM:(hd + 1) * ML_QK_DIM] for hd in heads]
        kh = [k_all[lo:hi, hd * ML_QK_DIM:(hd + 1) * ML_QK_DIM] for hd in heads]
        vh = [mv_ref[lo:hi, hd * ML_V_DIM:(hd + 1) * ML_V_DIM] for hd in heads]
        c_prev = [c_ref[hd] for hd in heads]
        n_prev = [n_ref[hd] for hd in heads]

        dmat = [jnp.where(tri, bc[hd] - br[hd] + lir[hd], -jnp.inf) for hd in heads]
        inter = [bc[hd] + m_prev[hd] for hd in heads]
        m_t = [jnp.maximum(inter[hd], jnp.max(dmat[hd], axis=-1, keepdims=True)) for hd in heads]
        w_intra = [jnp.exp(dmat[hd] - m_t[hd]) for hd in heads]
        a_inter = [jnp.exp(inter[hd] - m_t[hd]) for hd in heads]
        m_new = [jnp.maximum(g_tot[hd] + m_prev[hd],
                             jnp.max(g_tot[hd] - br[hd] + lir[hd], axis=-1, keepdims=True)) for hd in heads]
        a_state = [jnp.exp(g_tot[hd] + m_prev[hd] - m_new[hd]) for hd in heads]
        wk = [jnp.exp(g_tot[hd] - bc[hd] + lic[hd] - m_new[hd]) * kh[hd] for hd in heads]

        s_qk = [_dot_nt(qh[hd], kh[hd].astype(BF16)) * w_intra[hd] for hd in heads]
        read = [_dot(qh[hd], c_prev[hd].astype(BF16)) for hd in heads]
        d_c = [_dot(wk[hd].T.astype(BF16), vh[hd]) for hd in heads]
        num = [a_inter[hd] * read[hd] + _dot(s_qk[hd].astype(BF16), vh[hd]) for hd in heads]
        den = [a_inter[hd] * jnp.sum(qh[hd].astype(F32) * n_prev[hd], axis=-1, keepdims=True)
               + jnp.sum(s_qk[hd], axis=-1, keepdims=True) for hd in heads]
        for hd in heads:
            hs_ref[lo:hi, hd * ML_V_DIM:(hd + 1) * ML_V_DIM] = (
                num[hd] / jnp.maximum(jnp.abs(den[hd]), jnp.exp(-m_t[hd])))
        for hd in heads:
            c_ref[hd] = a_state[hd] * c_prev[hd] + d_c[hd]
            n_ref[hd] = a_state[hd] * n_prev[hd] + jnp.sum(wk[hd], axis=0, keepdims=True)
            m_ref[hd] = jnp.broadcast_to(m_new[hd], m_ref.shape[1:])

    for hd in range(ML_HEADS):
        sl = slice(hd * ML_V_DIM, (hd + 1) * ML_V_DIM)
        out_ref[:, sl] = (_rms(hs_ref[:, sl], ng_ref[:, sl]) * jax.nn.sigmoid(mo_ref[:, sl])).astype(BF16)


def _mlstm(mqk, mv, small, gt, mo, cw, cb, gbc, gbr, ng, rows):
    T = mqk.shape[0]
    row = lambda n: pl.BlockSpec((rows, n), lambda i: (i, 0))
    nv = ML_HEADS * ML_V_DIM
    return pl.pallas_call(
        functools.partial(_mlstm_body, rows=rows),
        grid=(T // rows,),
        in_specs=[row(mqk.shape[1]), row(nv), row(_SMALL), pl.BlockSpec((8, rows), lambda i: (1, i)), row(nv),
                  _resident(cw.shape), _resident(cb.shape), _resident(gbc.shape), _resident(gbr.shape),
                  _resident(ng.shape)],
        out_specs=row(nv),
        out_shape=jax.ShapeDtypeStruct((T, nv), BF16),
        scratch_shapes=[pltpu.VMEM((rows + 8, mqk.shape[1]), F32),
                        pltpu.VMEM((ML_HEADS, ML_QK_DIM, ML_V_DIM), F32),
                        pltpu.VMEM((ML_HEADS, 1, ML_QK_DIM), F32),
                        pltpu.VMEM((ML_HEADS, 1, 128), F32),
                        pltpu.VMEM((rows, nv), F32)],
        compiler_params=_params(),
        name="mlstm",
    )(mqk, mv, small, gt, mo, cw, cb, gbc, gbr, ng)


def _memfold_body(mem_ref, g_ref, wk_ref, wv_ref, wq_ref, wc_ref, wqk_ref, wvo_ref):
    mn = _rms(mem_ref[...], g_ref[...]).astype(BF16)
    k = _dot(mn, wk_ref[...].astype(BF16)).astype(BF16)
    v = _dot(mn, wv_ref[...].astype(BF16)).astype(BF16)
    wqk_ref[...] = (_dot_nt(wq_ref[...].astype(BF16), k) * (X_HEAD_DIM ** -0.5)).astype(BF16)
    wvo_ref[...] = _dot(v, wc_ref[...].astype(BF16)).astype(BF16)


def _memfold(mem, g, w_ckv, w_cq, w_co):
    M, D = mem.shape
    dh = X_HEAD_DIM
    return pl.pallas_call(
        _memfold_body,
        grid=(X_HEADS,),
        in_specs=[_resident(mem.shape), _resident(g.shape),
                  pl.BlockSpec((D, dh), lambda h: (0, h)), pl.BlockSpec((D, dh), lambda h: (0, X_HEADS + h)),
                  pl.BlockSpec((D, dh), lambda h: (0, h)), pl.BlockSpec((dh, D), lambda h: (h, 0))],
        out_specs=[pl.BlockSpec((D, M), lambda h: (0, h)), pl.BlockSpec((M, D), lambda h: (h, 0))],
        out_shape=[jax.ShapeDtypeStruct((D, X_HEADS * M), BF16), jax.ShapeDtypeStruct((X_HEADS * M, D), BF16)],
        compiler_params=_params(),
        name="memfold",
    )(mem, g, w_ckv, w_ckv, w_cq, w_co)


def _mixout_body(x_ref, dsa_ref, ml_ref, wo_ref, wqk_ref, wvo_ref, wr_ref, gx_ref, gf_ref,
                 x2_ref, hf_ref, rl_ref, p_ref):
    nd = dsa_ref.shape[1]
    x1 = x_ref[...] + _dot(dsa_ref[...], wo_ref[0:nd, :]) + _dot(ml_ref[...], wo_ref[nd:, :])
    lg_all = _dot(_rms(x1, gx_ref[...]).astype(BF16), wqk_ref[...])
    n_mem = wqk_ref.shape[1] // X_HEADS
    for hd in range(X_HEADS):
        sl = slice(hd * n_mem, (hd + 1) * n_mem)
        lg = lg_all[:, sl]
        e = jnp.exp(lg - jnp.max(lg, axis=-1, keepdims=True))
        p_ref[:, sl] = (e / jnp.sum(e, axis=-1, keepdims=True)).astype(BF16)
    x2 = x1 + _dot(p_ref[...], wvo_ref[...])
    x2_ref[...] = x2
    hf = _rms(x2, gf_ref[...]).astype(BF16)
    rl_ref[...] = _dot(hf, wr_ref[...])
    bits = lax.bitcast_convert_type(hf.astype(F32), U32)
    for c in range(_SUBLANES):
        lo = bits[:, (2 * c) * _LANES:(2 * c + 1) * _LANES]
        hi = bits[:, (2 * c + 1) * _LANES:(2 * c + 2) * _LANES]
        hf_ref[pl.ds(c, x2.shape[0], stride=_SUBLANES), :] = (hi & jnp.uint32(0xFFFF0000)) | (lo >> 16)


def _mixout(x, dsa, ml, wo, wqk, wvo, wr, gx, gf, tm):
    T = x.shape[0]
    row = lambda n: pl.BlockSpec((tm, n), lambda i: (i, 0))
    return pl.pallas_call(
        _mixout_body,
        grid=(T // tm,),
        in_specs=[row(D_MODEL), row(dsa.shape[1]), row(ml.shape[1]), _resident(wo.shape),
                  _resident(wqk.shape), _resident(wvo.shape), _resident(wr.shape), _resident(gx.shape),
                  _resident(gf.shape)],
        out_specs=[row(D_MODEL), pl.BlockSpec((tm * _SUBLANES, _LANES), lambda i: (i, 0)), row(wr.shape[1])],
        out_shape=[jax.ShapeDtypeStruct((T, D_MODEL), F32),
                   jax.ShapeDtypeStruct((T * _SUBLANES, _LANES), U32),
                   jax.ShapeDtypeStruct((T, wr.shape[1]), F32)],
        scratch_shapes=[pltpu.VMEM((tm, wqk.shape[1]), BF16)],
        compiler_params=_params(),
        name="mixout",
    )(x, dsa, ml, wo, wqk, wvo, wr, gx, gf)


def _moe_body(tok_ref, eb_ref, hf_hbm, wg_ref, wu_ref, wd_ref, y_hbm,
              xbuf, ybuf, wgb, wub, wdb, gsem, ysem, *, bm, n_blk):
    e = pl.program_id(0)
    n_valid = eb_ref[N_EXPERTS]
    b_lo = eb_ref[e]
    b_hi = eb_ref[e + 1]

    def tok_words(tok):
        return hf_hbm.at[pl.ds(pl.multiple_of(tok * _SUBLANES, _SUBLANES), _SUBLANES)]

    def gather(blk, s):
        def issue(g, carry):
            for u in range(_SUBLANES):
                r = g * _SUBLANES + u
                pltpu.make_async_copy(tok_words(tok_ref[blk * bm + r]),
                                      xbuf.at[s, pl.ds(pl.multiple_of(r * _SUBLANES, _SUBLANES), _SUBLANES)],
                                      gsem.at[s]).start(priority=1)
            return carry
        lax.fori_loop(0, bm // _SUBLANES, issue, 0)

    def wait_gather(s):
        pltpu.make_async_copy(xbuf.at[s], xbuf.at[s], gsem.at[s]).wait()

    def y_copy(blk, s):
        return pltpu.make_async_copy(ybuf.at[s], y_hbm.at[pl.ds(pl.multiple_of(blk * bm, bm), bm)], ysem.at[s])

    @pl.when(e == 0)
    def _():
        for k in range(_GATHER_AHEAD):
            gather(k, k)

    @pl.when(b_hi > b_lo)
    def _():
        wgb[...] = wg_ref[0].astype(BF16)
        wub[...] = wu_ref[0].astype(BF16)
        wdb[...] = wd_ref[0].astype(BF16)

        def block(b, carry):
            s = b % _GATHER_SLOTS
            wait_gather(s)
            chunks = []
            for c in range(_SUBLANES):
                w = xbuf[s, pl.ds(c, bm, stride=_SUBLANES), :]
                chunks.append(lax.bitcast_convert_type(w << 16, F32).astype(BF16))
                chunks.append(lax.bitcast_convert_type(w & jnp.uint32(0xFFFF0000), F32).astype(BF16))
            xb = jnp.concatenate(chunks, axis=-1)
            gate = _dot(xb, wgb[...])
            a = gate * jax.nn.sigmoid(gate) * _dot(xb, wub[...])
            y = _dot(a.astype(BF16), wdb[...])

            nxt = jnp.minimum(b + _GATHER_AHEAD, n_blk - 1)
            for r in range(bm):
                pltpu.make_async_copy(tok_words(tok_ref[nxt * bm + r]),
                                      xbuf.at[(b + _GATHER_AHEAD) % _GATHER_SLOTS, pl.ds(r * _SUBLANES, _SUBLANES)],
                                      gsem.at[(b + _GATHER_AHEAD) % _GATHER_SLOTS]).start(priority=1)

            @pl.when(b >= 2)
            def _():
                y_copy(b - 2, b % 2).wait()

            ybuf[b % 2] = y
            y_copy(b, b % 2).start()
            return carry

        lax.fori_loop(b_lo, b_hi, block, 0)

    @pl.when(e == pl.num_programs(0) - 1)
    def _():
        for k in range(_GATHER_AHEAD):
            wait_gather((n_valid + k) % _GATHER_SLOTS)

        @pl.when(n_valid >= 2)
        def _():
            y_copy(n_valid - 2, n_valid % 2).wait()
        y_copy(n_valid - 1, (n_valid - 1) % 2).wait()
        ybuf[0] = jnp.zeros(ybuf.shape[1:], F32)

        def zero_block(b, carry):
            cp = y_copy(b, 0)
            cp.start()
            cp.wait()
            return carry

        lax.fori_loop(n_valid, n_blk, zero_block, 0)


def _moe(tok, e_blk, hf, wg, wu, wd, n_blk, bm):
    D = wg.shape[1]
    wspec = lambda shape: pl.BlockSpec((1,) + shape, lambda e, *_: (e, 0, 0))
    grid_spec = pltpu.PrefetchScalarGridSpec(
        num_scalar_prefetch=2,
        grid=(N_EXPERTS,),
        in_specs=[pl.BlockSpec(memory_space=pl.ANY),
                  wspec((D, D_EXPERT)), wspec((D, D_EXPERT)), wspec((D_EXPERT, D))],
        out_specs=pl.BlockSpec(memory_space=pl.ANY),
        scratch_shapes=[pltpu.VMEM((_GATHER_SLOTS, bm * _SUBLANES, _LANES), U32),
                        pltpu.VMEM((2, bm, D), F32),
                        pltpu.VMEM((D, D_EXPERT), BF16), pltpu.VMEM((D, D_EXPERT), BF16),
                        pltpu.VMEM((D_EXPERT, D), BF16),
                        pltpu.SemaphoreType.DMA((_GATHER_SLOTS,)), pltpu.SemaphoreType.DMA((2,))],
    )
    return pl.pallas_call(
        functools.partial(_moe_body, bm=bm, n_blk=n_blk),
        grid_spec=grid_spec,
        out_shape=jax.ShapeDtypeStruct((n_blk * bm, D), F32),
        compiler_params=_params(),
        name="moe",
    )(tok, e_blk, hf, wg, wu, wd)


def _route(rl, b_group, b_router, bm):
    N = rl.shape[0]
    g_logits = rl[:, :N_GROUPS] + b_group
    g_prob = jax.nn.softmax(g_logits, axis=-1)
    g_sel = jnp.argmax(g_logits, axis=-1)
    p_g = jnp.take_along_axis(g_prob, g_sel[:, None], axis=-1)
    e_logits = rl[:, N_GROUPS:N_GROUPS + N_EXPERTS] + b_router
    in_group = (jnp.arange(N_EXPERTS, dtype=I32) // EXP_PER_GROUP)[None, :] == g_sel[:, None]
    e_prob = jax.nn.softmax(jnp.where(in_group, e_logits, -jnp.inf), axis=-1)
    e_cand = jnp.where(in_group, e_prob, -1.0)
    tops = []
    for _ in range(TOPK_IN_GROUP):
        best = jnp.argmax(e_cand, axis=-1).astype(I32)
        tops.append((jnp.max(e_cand, axis=-1), best))
        e_cand = jnp.where(jnp.arange(N_EXPERTS, dtype=I32)[None, :] == best[:, None], -2.0, e_cand)
    top_p = jnp.stack([p for p, _ in tops], axis=-1)
    expert_id = jnp.stack([e for _, e in tops], axis=-1)
    gates = p_g * top_p / top_p.sum(-1, keepdims=True)

    A = N * TOPK_IN_GROUP
    flat_e = expert_id.reshape(A)
    onehot = (flat_e[:, None] == jnp.arange(N_EXPERTS, dtype=I32)[None, :])
    seg = _RANK_SEG if A % _RANK_SEG == 0 else A
    oh = onehot.astype(BF16).reshape(A // seg, seg, N_EXPERTS)
    before = (jnp.arange(seg)[None, :] < jnp.arange(seg)[:, None]).astype(BF16)
    within = jnp.einsum('ij,bjk->bik', before, oh, preferred_element_type=F32)
    seg_tot = jnp.sum(oh.astype(F32), axis=1)
    seg_base = jnp.cumsum(seg_tot, axis=0) - seg_tot
    rank = jnp.sum((within + seg_base[:, None, :]) * oh.astype(F32), axis=-1).reshape(A).astype(I32)
    counts = jnp.sum(seg_tot, axis=0).astype(I32)
    padded = (counts + bm - 1) // bm * bm
    pad_ends = jnp.cumsum(padded)
    pad_starts = pad_ends - padded
    row = jnp.sum(jnp.where(onehot, pad_starts[None, :], 0), axis=1) + rank
    n_blk = -(-A // bm) + N_EXPERTS
    row_tok = jnp.zeros((n_blk * bm,), I32).at[row].set(jnp.arange(A, dtype=I32) // TOPK_IN_GROUP)
    e_blk = jnp.concatenate([pad_starts, pad_ends[-1:]]) // bm
    return row_tok, row, gates, e_blk.astype(I32), n_blk


def _final_body(row_ref, x_ref, gate_ref, y_hbm, g_ref, out_ref, ybuf, sem, *, tm):
    i = pl.program_id(0)
    slot = i % 2

    def gather(tile, s):
        def issue(g, carry):
            for u in range(_SUBLANES):
                for k in range(TOPK_IN_GROUP):
                    src = row_ref[(tile * tm + g * _SUBLANES + u) * TOPK_IN_GROUP + k]
                    pltpu.make_async_copy(y_hbm.at[pl.ds(src, 1)], ybuf.at[s, k, g, pl.ds(u, 1)],
                                          sem.at[s]).start()
            return carry
        lax.fori_loop(0, tm // _SUBLANES, issue, 0)

    @pl.when(i == 0)
    def _():
        gather(0, 0)

    @pl.when(i + 1 < pl.num_programs(0))
    def _():
        gather(i + 1, 1 - slot)

    pltpu.make_async_copy(ybuf.at[slot], ybuf.at[slot], sem.at[slot]).wait()
    acc = x_ref[...]
    for k in range(TOPK_IN_GROUP):
        acc = acc + gate_ref[:, k:k + 1] * ybuf[slot, k].reshape(tm, ybuf.shape[-1])
    out_ref[...] = _rms(acc, g_ref[...])


def _final(row, x2, gates, y_rows, g, tm):
    T, D = x2.shape
    grid_spec = pltpu.PrefetchScalarGridSpec(
        num_scalar_prefetch=1,
        grid=(T // tm,),
        in_specs=[pl.BlockSpec((tm, D), lambda i, *_: (i, 0)),
                  pl.BlockSpec((tm, TOPK_IN_GROUP), lambda i, *_: (i, 0)),
                  pl.BlockSpec(memory_space=pl.ANY),
                  pl.BlockSpec(g.shape, lambda i, *_: (0, 0))],
        out_specs=pl.BlockSpec((tm, D), lambda i, *_: (i, 0)),
        scratch_shapes=[pltpu.VMEM((2, TOPK_IN_GROUP, tm // _SUBLANES, _SUBLANES, D), F32),
                        pltpu.SemaphoreType.DMA((2,))],
    )
    return pl.pallas_call(
        functools.partial(_final_body, tm=tm),
        grid_spec=grid_spec,
        out_shape=jax.ShapeDtypeStruct((T, D), F32),
        compiler_params=_params(),
        name="final",
    )(row, x2, gates, y_rows, g)


def _tile_sizes(T):
    pick = lambda want: want if T % want == 0 else CHUNK
    return dict(inproj=pick(256), dsa_q=pick(256), dsa_k=pick(512), mlstm=pick(256), mixout=pick(256),
                final=pick(256), moe=128)


def _layer(x, mem, norm_mix_g, w_in, kv_norm_g, k_idx_norm_g, w_uk, w_uv, conv_w, conv_b, gate_b, ml_norm_g,
           w_out, norm_x_g, mem_norm_g, w_cq, w_ckv, w_co, norm_ffn_g, w_group, b_group, w_router, b_router,
           w_gate, w_up, w_down, out_g):
    T = x.shape[0]
    ts = _tile_sizes(T)
    r2 = lambda v: v.reshape(1, -1)

    w_r = _wprep(w_in, 256)
    wuk_t = jnp.transpose(w_uk, (1, 2, 0)).astype(BF16)
    wuv_t = jnp.transpose(w_uv, (1, 0, 2)).astype(BF16)

    qabs, ckv, qi, kidx, small, mqk, mv, mo = _inproj(
        x, r2(norm_mix_g), w_r, wuk_t, r2(kv_norm_g), r2(k_idx_norm_g), ts["inproj"])

    gate_rows = jnp.transpose(small[:, _S_WI:_S_MF + ML_HEADS])
    dsa_out = _dsa(qi, gate_rows, qabs, kidx, ckv, wuv_t, ts["dsa_q"], ts["dsa_k"])

    gb_col = jnp.zeros((1, _SMALL), F32).at[0, _S_MI:_S_MI + 2 * ML_HEADS].set(gate_b)
    ml_out = _mlstm(mqk, mv, small, gate_rows, mo, conv_w, r2(conv_b), gb_col, gate_b.reshape(-1, 1),
                    r2(ml_norm_g), ts["mlstm"])

    wqk, wvo = _memfold(mem, r2(mem_norm_g), w_ckv, w_cq, w_co)
    w_rt = jnp.concatenate([w_group, w_router,
                            jnp.zeros((D_MODEL, 128 - N_GROUPS - N_EXPERTS), w_group.dtype)], axis=1)
    x2, hf, rl = _mixout(x, dsa_out, ml_out, w_out.astype(BF16), wqk, wvo, w_rt.astype(BF16),
                         r2(norm_x_g), r2(norm_ffn_g), ts["mixout"])

    bm = ts["moe"]
    row_tok, row, gates, e_blk, n_blk = _route(rl, b_group, b_router, bm)
    y_rows = _moe(row_tok, e_blk, hf, w_gate, w_up, w_down, n_blk, bm)
    return _final(row, x2, gates, y_rows, r2(out_g), ts["final"])


def kernel(x, mem, norm_mix_g, w_in, kv_norm_g, k_idx_norm_g, w_uk, w_uv, conv_w, conv_b, gate_b, ml_norm_g,
           w_out, norm_x_g, mem_norm_g, w_cq, w_ckv, w_co, norm_ffn_g, w_group, b_group, w_router, b_router,
           w_gate, w_up, w_down, final_norm_g):
    B, T, D = x.shape
    assert B == 1 and D == D_MODEL and norm_mix_g.shape[0] == 1 and T % CHUNK == 0
    out = _layer(x[0], mem[0], norm_mix_g[0], w_in, kv_norm_g[0], k_idx_norm_g[0], w_uk[0], w_uv[0],
                 conv_w[0], conv_b[0], gate_b[0], ml_norm_g[0], w_out[0], norm_x_g[0], mem_norm_g[0],
                 w_cq[0], w_ckv[0], w_co[0], norm_ffn_g[0], w_group[0], b_group[0], w_router[0], b_router[0],
                 w_gate[0], w_up[0], w_down[0], final_norm_g)
    return out[None]
```

```python
import functools

import jax
import jax.numpy as jnp
import numpy as np
from jax import lax
from jax.experimental import pallas as pl
from jax.experimental.pallas import tpu as pltpu

F32 = jnp.float32
BF16 = jnp.bfloat16
I32 = jnp.int32
I16 = jnp.int16
U32 = jnp.uint32

EPS = 1e-6
CHUNK = 64
D_MODEL = 2048

DSA_HEADS = 8
DSA_HEAD_DIM = 128
DSA_LATENT = 256
IDX_HEADS = 8
IDX_DIM = 64
TOPK_MAX = 256

ML_HEADS = 4
ML_QK_DIM = 128
ML_V_DIM = 256
CONV_W = 4

X_HEADS = 4
X_HEAD_DIM = D_MODEL // X_HEADS

N_GROUPS = 4
EXP_PER_GROUP = 8
N_EXPERTS = N_GROUPS * EXP_PER_GROUP
TOPK_IN_GROUP = 2
D_EXPERT = 512

_O_DQ = 0
_O_CKV = _O_DQ + DSA_HEADS * DSA_HEAD_DIM
_O_QI = _O_CKV + DSA_LATENT
_O_KI = _O_QI + IDX_HEADS * IDX_DIM
_O_WI = _O_KI + IDX_DIM
_O_MQ = _O_WI + IDX_HEADS
_O_MK = _O_MQ + ML_HEADS * ML_QK_DIM
_O_MV = _O_MK + ML_HEADS * ML_QK_DIM
_O_MI = _O_MV + ML_HEADS * ML_V_DIM
_O_MF = _O_MI + ML_HEADS
_O_MO = _O_MF + ML_HEADS
_O_END = _O_MO + ML_HEADS * ML_V_DIM

_G_DQ = (0, 1024)
_G_CKV = (1024, 1280)
_G_QI = (1280, 1792)
_G_SMALL = (1792, 1920)
_G_MQK = (1920, 2944)
_G_MV = (2944, 3968)
_G_MO = (3968, 4992)
_W_COLS = 4992
_S_WI = IDX_DIM
_S_MI = _S_WI + IDX_HEADS
_S_MF = _S_MI + ML_HEADS
_SMALL = 128

_VMEM_LIMIT = 56 * 1024 * 1024
_INT_MIN = -(2 ** 31)
_I16_MIN = -(2 ** 15)
_CHUNK_SHIFT = CHUNK.bit_length() - 1
_LOG2E = 1.4426950408889634
_SUBLANES = 8
_LANES = 128
_GATHER_SLOTS = 9
_GATHER_AHEAD = _GATHER_SLOTS - 1
_RANK_SEG = 512
_NEG = -1e30


def _rms(v, g):
    return v * lax.rsqrt(jnp.mean(v * v, axis=-1, keepdims=True) + EPS) * g


def _dot(a, b):
    return jnp.dot(a, b, preferred_element_type=F32)


def _dot_nt(a, b):
    return lax.dot_general(a, b, (((1,), (1,)), ((), ())), preferred_element_type=F32)


def _resident(shape):
    nd = len(shape)
    return pl.BlockSpec(shape, lambda *_: (0,) * nd, pipeline_mode=pl.Buffered(1))


def _params(n_axes=1):
    return pltpu.CompilerParams(dimension_semantics=("arbitrary",) * n_axes,
                                vmem_limit_bytes=_VMEM_LIMIT)


def _wprep_body(w_hbm, out_ref, wbuf, sem, *, tk):
    i = pl.program_id(0)
    slot = i % 2

    def rows(blk, s):
        return pltpu.make_async_copy(w_hbm.at[0, pl.ds(pl.multiple_of(blk * tk, tk), tk)], wbuf.at[s], sem.at[s])

    @pl.when(i == 0)
    def _():
        rows(0, 0).start()

    @pl.when(i + 1 < pl.num_programs(0))
    def _():
        rows(i + 1, 1 - slot).start()

    rows(i, slot).wait()
    w_ref = wbuf.at[slot]
    n_small = IDX_DIM + IDX_HEADS
    out_ref[:, _G_DQ[0]:_G_SMALL[0] + n_small] = w_ref[:, _O_DQ:_O_MQ].astype(BF16)
    out_ref[:, _G_SMALL[0] + n_small:_G_SMALL[0] + n_small + 2 * ML_HEADS] = w_ref[:, _O_MI:_O_MO].astype(BF16)
    out_ref[:, _G_SMALL[0] + n_small + 2 * ML_HEADS:_G_SMALL[1]] = jnp.zeros(
        (out_ref.shape[0], _SMALL - n_small - 2 * ML_HEADS), BF16)
    out_ref[:, _G_MQK[0]:_G_MV[1]] = w_ref[:, _O_MQ:_O_MI].astype(BF16)
    out_ref[:, _G_MO[0]:_G_MO[1]] = w_ref[:, _O_MO:_O_END].astype(BF16)


def _wprep(w_in, tk):
    K = w_in.shape[1]
    return pl.pallas_call(
        functools.partial(_wprep_body, tk=tk),
        grid=(K // tk,),
        in_specs=[pl.BlockSpec(memory_space=pl.ANY)],
        out_specs=pl.BlockSpec((tk, _W_COLS), lambda i: (i, 0)),
        out_shape=jax.ShapeDtypeStruct((K, _W_COLS), BF16),
        scratch_shapes=[pltpu.VMEM((2, tk, w_in.shape[2]), F32), pltpu.SemaphoreType.DMA((2,))],
        compiler_params=_params(),
        name="wprep",
    )(w_in)


def _inproj_body(x_ref, g_ref, w_ref, wuk_ref, kvg_ref, kig_ref,
                 qabs_ref, ckv_ref, qi_ref, kidx_ref, small_ref, mqk_ref, mv_ref, mo_ref):
    h = _rms(x_ref[...], g_ref[...]).astype(BF16)

    def proj(grp):
        return _dot(h, w_ref[:, grp[0]:grp[1]])

    dq = proj(_G_DQ)
    for hd in range(DSA_HEADS):
        qh = dq[:, hd * DSA_HEAD_DIM:(hd + 1) * DSA_HEAD_DIM].astype(BF16)
        qa = _dot(qh, wuk_ref[hd]) * (DSA_HEAD_DIM ** -0.5 * _LOG2E)
        qabs_ref[:, hd * DSA_LATENT:(hd + 1) * DSA_LATENT] = qa.astype(BF16)
    ckv_ref[...] = _rms(proj(_G_CKV), kvg_ref[...]).astype(BF16)
    qi_ref[...] = (proj(_G_QI) * (IDX_DIM ** -0.5)).astype(BF16)
    small = proj(_G_SMALL)
    small_ref[...] = small
    kidx_ref[...] = _rms(small[:, :IDX_DIM], kig_ref[...]).astype(BF16)
    mqk_ref[...] = proj(_G_MQK)
    mv_ref[...] = proj(_G_MV).astype(BF16)
    mo_ref[...] = proj(_G_MO)


def _inproj(x, g, w, wuk, kvg, kig, tm):
    T = x.shape[0]
    row = lambda n: pl.BlockSpec((tm, n), lambda i: (i, 0))
    outs = [(8 * DSA_LATENT, BF16), (DSA_LATENT, BF16), (IDX_HEADS * IDX_DIM, BF16), (IDX_DIM, BF16),
            (_SMALL, F32), (2 * ML_HEADS * ML_QK_DIM, F32), (ML_HEADS * ML_V_DIM, BF16),
            (ML_HEADS * ML_V_DIM, F32)]
    return pl.pallas_call(
        _inproj_body,
        grid=(T // tm,),
        in_specs=[row(D_MODEL), _resident(g.shape), _resident(w.shape), _resident(wuk.shape),
                  _resident(kvg.shape), _resident(kig.shape)],
        out_specs=[row(n) for n, _ in outs],
        out_shape=[jax.ShapeDtypeStruct((T, n), dt) for n, dt in outs],
        compiler_params=_params(),
        name="inproj",
    )(x, g, w, wuk, kvg, kig)


def _sublane_fold(v, op, rows=_SUBLANES, ways=4):
    groups = [v[r * rows:(r + 1) * rows, :] for r in range(v.shape[0] // rows)]
    accs = groups[:ways]
    for r in range(ways, len(groups)):
        accs[r % ways] = op(accs[r % ways], groups[r])
    while len(accs) > 1:
        accs = [op(accs[k], accs[k + 1]) if k + 1 < len(accs) else accs[k] for k in range(0, len(accs), 2)]
    return accs[0]


def _bit_transpose32(words):
    a = list(words)
    j, m = 16, 0x0000FFFF
    while j:
        k = 0
        while k < 32:
            t = (a[k] ^ (a[k + j] >> j)) & jnp.uint32(m)
            a[k] = a[k] ^ t
            a[k + j] = a[k + j] ^ (t << j)
            k = (k + j + 1) & ~j
        j >>= 1
        m = (m ^ (m << j)) & 0xFFFFFFFF
    return a


def _dsa_body(qi_ref, wrow_ref, qabs_ref, kidx_ref, ckv_ref, ckvt_ref, wuv_ref, out_ref,
              key_ref, planes_ref, alive_ref, lga_ref, lgb_ref, m_ref, l_ref, acc_ref, *, tq, tk, topk, nbits_idx):
    i = pl.program_id(0)
    n_kb = ((i + 1) * tq + tk - 1) // tk
    w_rows = wrow_ref[0:IDX_HEADS, :] * (IDX_HEADS ** -0.5)
    q_chunk = (i * tq + lax.broadcasted_iota(I32, (1, tq), 1)) >> _CHUNK_SHIFT

    def key_pos(j):
        return j * tk + lax.broadcasted_iota(I32, (tk, 1), 0)

    def score_block(j, carry):
        kx = kidx_ref[pl.ds(pl.multiple_of(j * tk, tk), tk), :]
        s = jnp.zeros((tk, tq), F32)
        for hd in range(IDX_HEADS):
            d = _dot_nt(kx, qi_ref[:, hd * IDX_DIM:(hd + 1) * IDX_DIM])
            s = s + w_rows[hd:hd + 1, :] * jnp.maximum(d, 0.0)
        bits = lax.bitcast_convert_type(s, I32)
        key = bits ^ ((bits >> 31) & 0x7FFFFFFF)
        key_ref[j] = jnp.where((key_pos(j) >> _CHUNK_SHIFT) <= q_chunk, key, _INT_MIN)
        return carry

    lax.fori_loop(0, n_kb, score_block, 0)

    def count(pred):
        def body(j, acc):
            hit = pred(key_ref[j], key_pos(j)).astype(I32)
            return acc + _sublane_fold(hit, jnp.add)
        acc = lax.fori_loop(0, n_kb, body, jnp.zeros((8, tq), I32))
        return jnp.sum(acc, axis=0, keepdims=True)

    plane_rows = tk // 32

    @pl.when(i == 0)
    def _():
        planes_ref[...] = jnp.zeros(planes_ref.shape, U32)

    def slice_block(j, carry):
        u = lax.bitcast_convert_type(key_ref[j], U32) ^ jnp.uint32(0x80000000)
        for h in range(plane_rows // _SUBLANES):
            base = h * 32 * _SUBLANES
            planes = _bit_transpose32([u[base + _SUBLANES * v:base + _SUBLANES * (v + 1), :] for v in range(32)])
            row0 = pl.multiple_of(j * plane_rows + h * _SUBLANES, _SUBLANES)
            for b in range(32):
                planes_ref[b, pl.ds(row0, _SUBLANES), :] = planes[31 - b]
        return carry

    lax.fori_loop(0, n_kb, slice_block, 0)
    word_row = lax.broadcasted_iota(I32, (alive_ref.shape[0], 1), 0)
    alive_ref[...] = jnp.where(word_row < n_kb * plane_rows, jnp.full(alive_ref.shape, 0xFFFFFFFF, U32),
                               jnp.uint32(0))

    def radix_select(n_rows):
        def ones_count(words):
            return jnp.sum(_sublane_fold(lax.population_count(words).astype(I32), jnp.add), axis=0, keepdims=True)

        def select_bit(it, carry):
            need, t_u = carry
            b = 31 - it
            alive = alive_ref[0:n_rows, :]
            ones = alive & planes_ref[b, pl.ds(0, n_rows), :]
            n_ones = ones_count(ones)
            take = n_ones >= need
            alive_ref[0:n_rows, :] = jnp.where(take, ones, alive ^ ones)
            bit = lax.shift_left(jnp.uint32(1), jnp.asarray(b, U32))
            return jnp.where(take, need, need - n_ones), jnp.where(take, t_u | bit, t_u)

        need, t_u = lax.fori_loop(0, 32, select_bit, (jnp.full((1, tq), topk, I32), jnp.zeros((1, tq), U32)))
        return need, t_u, ones_count(alive_ref[0:n_rows, :])

    all_rows = alive_ref.shape[0]
    if all_rows % (2 * _SUBLANES) == 0:
        need, t_u, n_equal = lax.cond(n_kb * plane_rows <= all_rows // 2,
                                      lambda: radix_select(all_rows // 2), lambda: radix_select(all_rows))
    else:
        need, t_u, n_equal = radix_select(all_rows)
    short = t_u == 0
    t = jnp.maximum(lax.bitcast_convert_type(t_u ^ jnp.uint32(0x80000000), I32), _INT_MIN + 1)
    all_pos = jnp.int32(2 ** nbits_idx - 1)
    has_ties = (n_equal > need) & jnp.logical_not(short)
    n_tie_take = jnp.where(has_ties, need, all_pos)

    def tie_cutoff():
        def pos_bit(b, c):
            cand = c + lax.shift_left(jnp.int32(1), nbits_idx - 1 - b)
            f = count(lambda kb, pos: (kb == t) & (pos < cand))
            return jnp.where(f <= n_tie_take, cand, c)
        return lax.fori_loop(0, nbits_idx, pos_bit, jnp.zeros((1, tq), I32))

    cut = lax.cond(jnp.max(has_ties.astype(I32)) > 0, tie_cutoff, lambda: jnp.full((1, tq), all_pos, I32))

    m_ref[...] = jnp.full(m_ref.shape, _NEG, F32)
    l_ref[...] = jnp.zeros(l_ref.shape, F32)
    acc_ref[...] = jnp.zeros(acc_ref.shape, F32)
    last_blk = ckvt_ref.shape[0] - 1

    def logits(j, lg_buf):
        c_blk = ckv_ref[pl.ds(pl.multiple_of(jnp.minimum(j, last_blk) * tk, tk), tk), :]
        kb = key_ref[j]
        sel = (kb > t) | ((kb == t) & (key_pos(j) < cut))
        bias = jnp.where(sel, 0.0, _NEG).astype(F32)
        for hd in range(DSA_HEADS):
            lg_buf[hd] = _dot_nt(c_blk, qabs_ref[:, hd * DSA_LATENT:(hd + 1) * DSA_LATENT]) + bias

    def accumulate(j, lg_buf):
        c_blk_t = ckvt_ref[jnp.minimum(j, last_blk)]
        for hd in range(DSA_HEADS):
            lg = lg_buf[hd]
            m_old = m_ref[hd:hd + 1, :]
            m_new = jnp.maximum(m_old, jnp.max(_sublane_fold(lg, jnp.maximum, ways=1), axis=0, keepdims=True))
            p = jnp.exp2(lg - m_new)
            alpha = jnp.exp2(m_old - m_new)
            l_ref[hd:hd + 1, :] = alpha * l_ref[hd:hd + 1, :] + jnp.sum(_sublane_fold(p, jnp.add, ways=1), axis=0,
                                                                         keepdims=True)
            acc_ref[hd] = alpha * acc_ref[hd] + _dot(c_blk_t, p.astype(BF16))
            m_ref[hd:hd + 1, :] = m_new

    logits(0, lga_ref)

    def attn_pair(mi, carry):
        ja = 2 * mi
        accumulate(ja, lga_ref)
        logits(ja + 1, lgb_ref)
        accumulate(ja + 1, lgb_ref)
        logits(jnp.minimum(ja + 2, n_kb - 1), lga_ref)
        return carry

    lax.fori_loop(0, n_kb // 2, attn_pair, 0)

    @pl.when(n_kb % 2 == 1)
    def _():
        accumulate(n_kb - 1, lga_ref)

    for hd in range(DSA_HEADS):
        o_lat = (acc_ref[hd] / l_ref[hd:hd + 1, :]).T.astype(BF16)
        out_ref[:, hd * DSA_HEAD_DIM:(hd + 1) * DSA_HEAD_DIM] = _dot(o_lat, wuv_ref[hd]).astype(BF16)


def _dsa(qi, wrows, qabs, kidx, ckv, wuv, tq, tk):
    T = qi.shape[0]
    topk = min(TOPK_MAX, T // 4)
    n_kb = T // tk
    ckvt = jnp.transpose(ckv.reshape(n_kb, tk, DSA_LATENT), (0, 2, 1))
    row = lambda n: pl.BlockSpec((tq, n), lambda i: (i, 0))
    body = functools.partial(_dsa_body, tq=tq, tk=tk, topk=topk, nbits_idx=int(T).bit_length())
    return pl.pallas_call(
        body,
        grid=(T // tq,),
        in_specs=[row(qi.shape[1]), pl.BlockSpec((wrows.shape[0], tq), lambda i: (0, i)), row(qabs.shape[1]),
                  _resident(kidx.shape), _resident(ckv.shape), _resident(ckvt.shape), _resident(wuv.shape)],
        out_specs=row(DSA_HEADS * DSA_HEAD_DIM),
        out_shape=jax.ShapeDtypeStruct((T, DSA_HEADS * DSA_HEAD_DIM), BF16),
        scratch_shapes=[pltpu.VMEM((n_kb + n_kb % 2, tk, tq), I32),
                        pltpu.VMEM((32, T // 32, tq), U32), pltpu.VMEM((T // 32, tq), U32),
                        pltpu.VMEM((DSA_HEADS, tk, tq), F32), pltpu.VMEM((DSA_HEADS, tk, tq), F32),
                        pltpu.VMEM((DSA_HEADS, tq), F32),
                        pltpu.VMEM((DSA_HEADS, tq), F32), pltpu.VMEM((DSA_HEADS, DSA_LATENT, tq), F32)],
        compiler_params=_params(),
        name="dsa",
    )(qi, wrows, qabs, kidx, ckv, ckvt, wuv)


def _log_sigmoid(v):
    return jnp.minimum(v, 0.0) - jnp.log1p(jnp.exp(-jnp.abs(v)))


def _chunk_cumsum(v, axis):
    pos = lax.broadcasted_iota(I32, v.shape, axis) & (CHUNK - 1)
    d = 1
    while d < CHUNK:
        v = v + jnp.where(pos >= d, pltpu.roll(v, d, axis=axis), 0.0)
        d *= 2
    return v


def _mlstm_body(mqk_ref, mv_ref, small_ref, gt_ref, mo_ref, cw_ref, cb_ref, gbc_ref, gbr_ref, ng_ref,
                out_ref, xe_ref, c_ref, n_ref, m_ref, hs_ref, *, rows):
    @pl.when(pl.program_id(0) == 0)
    def _():
        xe_ref[0:8, :] = jnp.zeros((8, xe_ref.shape[1]), F32)
        c_ref[...] = jnp.zeros(c_ref.shape, F32)
        n_ref[...] = jnp.zeros(n_ref.shape, F32)
        m_ref[...] = jnp.zeros(m_ref.shape, F32)

    x = mqk_ref[...]
    xe_ref[8:8 + rows, :] = x
    y = cb_ref[...]
    for j in range(CONV_W - 1):
        y = y + xe_ref[5 + j:5 + j + rows, :] * cw_ref[j:j + 1, :]
    y = y + x * cw_ref[CONV_W - 1:CONV_W, :]
    xe_ref[0:8, :] = x[rows - 8:rows, :]
    qk = y * jax.nn.sigmoid(y)
    nqk = ML_HEADS * ML_QK_DIM
    q_all = (qk[:, :nqk] * (ML_QK_DIM ** -0.5)).astype(BF16)
    k_all = qk[:, nqk:]

    g_col = small_ref[...] + gbc_ref[...]
    g_row = gt_ref[...] + gbr_ref[...]
    b_col = _chunk_cumsum(_log_sigmoid(g_col), 0)
    b_row = _chunk_cumsum(_log_sigmoid(g_row), 1)

    tri = lax.broadcasted_iota(I32, (CHUNK, CHUNK), 1) <= lax.broadcasted_iota(I32, (CHUNK, CHUNK), 0)

    heads = range(ML_HEADS)
    for c in range(rows // CHUNK):
        lo, hi = c * CHUNK, (c + 1) * CHUNK
        bc = [b_col[lo:hi, _S_MF + hd:_S_MF + hd + 1] for hd in heads]
        lic = [g_col[lo:hi, _S_MI + hd:_S_MI + hd + 1] for hd in heads]
        br = [b_row[ML_HEADS + hd:ML_HEADS + hd + 1, lo:hi] for hd in heads]
        lir = [g_row[hd:hd + 1, lo:hi] for hd in heads]
        g_tot = [bc[hd][CHUNK - 1:CHUNK, :] for hd in heads]
        m_prev = [m_ref[hd][:, 0:1] for hd in heads]
        qh = [q_all[lo:hi, hd * ML_QK_DIM:(hd + 1) * ML_QK_DIM] for hd in heads]
        kh = [k_all[lo:hi, hd * ML_QK_DIM:(hd + 1) * ML_QK_DIM] for hd in heads]
        vh = [mv_ref[lo:hi, hd * ML_V_DIM:(hd + 1) * ML_V_DIM] for hd in heads]
        c_prev = [c_ref[hd] for hd in heads]
        n_prev = [n_ref[hd] for hd in heads]

        dmat = [jnp.where(tri, bc[hd] - br[hd] + lir[hd], -jnp.inf) for hd in heads]
        inter = [bc[hd] + m_prev[hd] for hd in heads]
        m_t = [jnp.maximum(inter[hd], jnp.max(dmat[hd], axis=-1, keepdims=True)) for hd in heads]
        w_intra = [jnp.exp(dmat[hd] - m_t[hd]) for hd in heads]
        a_inter = [jnp.exp(inter[hd] - m_t[hd]) for hd in heads]
        m_new = [jnp.maximum(g_tot[hd] + m_prev[hd],
                             jnp.max(g_tot[hd] - br[hd] + lir[hd], axis=-1, keepdims=True)) for hd in heads]
        a_state = [jnp.exp(g_tot[hd] + m_prev[hd] - m_new[hd]) for hd in heads]
        wk = [jnp.exp(g_tot[hd] - bc[hd] + lic[hd] - m_new[hd]) * kh[hd] for hd in heads]

        s_qk = [_dot_nt(qh[hd], kh[hd].astype(BF16)) * w_intra[hd] for hd in heads]
        read = [_dot(qh[hd], c_prev[hd].astype(BF16)) for hd in heads]
        d_c = [_dot(wk[hd].T.astype(BF16), vh[hd]) for hd in heads]
        num = [a_inter[hd] * read[hd] + _dot(s_qk[hd].astype(BF16), vh[hd]) for hd in heads]
        den = [a_inter[hd] * jnp.sum(qh[hd].astype(F32) * n_prev[hd], axis=-1, keepdims=True)
               + jnp.sum(s_qk[hd], axis=-1, keepdims=True) for hd in heads]
        for hd in heads:
            hs_ref[lo:hi, hd * ML_V_DIM:(hd + 1) * ML_V_DIM] = (
                num[hd] / jnp.maximum(jnp.abs(den[hd]), jnp.exp(-m_t[hd])))
        for hd in heads:
            c_ref[hd] = a_state[hd] * c_prev[hd] + d_c[hd]
            n_ref[hd] = a_state[hd] * n_prev[hd] + jnp.sum(wk[hd], axis=0, keepdims=True)
            m_ref[hd] = jnp.broadcast_to(m_new[hd], m_ref.shape[1:])

    for hd in range(ML_HEADS):
        sl = slice(hd * ML_V_DIM, (hd + 1) * ML_V_DIM)
        out_ref[:, sl] = (_rms(hs_ref[:, sl], ng_ref[:, sl]) * jax.nn.sigmoid(mo_ref[:, sl])).astype(BF16)


def _mlstm(mqk, mv, small, gt, mo, cw, cb, gbc, gbr, ng, rows):
    T = mqk.shape[0]
    row = lambda n: pl.BlockSpec((rows, n), lambda i: (i, 0))
    nv = ML_HEADS * ML_V_DIM
    return pl.pallas_call(
        functools.partial(_mlstm_body, rows=rows),
        grid=(T // rows,),
        in_specs=[row(mqk.shape[1]), row(nv), row(_SMALL), pl.BlockSpec((8, rows), lambda i: (1, i)), row(nv),
                  _resident(cw.shape), _resident(cb.shape), _resident(gbc.shape), _resident(gbr.shape),
                  _resident(ng.shape)],
        out_specs=row(nv),
        out_shape=jax.ShapeDtypeStruct((T, nv), BF16),
        scratch_shapes=[pltpu.VMEM((rows + 8, mqk.shape[1]), F32),
                        pltpu.VMEM((ML_HEADS, ML_QK_DIM, ML_V_DIM), F32),
                        pltpu.VMEM((ML_HEADS, 1, ML_QK_DIM), F32),
                        pltpu.VMEM((ML_HEADS, 1, 128), F32),
                        pltpu.VMEM((rows, nv), F32)],
        compiler_params=_params(),
        name="mlstm",
    )(mqk, mv, small, gt, mo, cw, cb, gbc, gbr, ng)


def _memfold_body(mem_ref, g_ref, wk_ref, wv_ref, wq_ref, wc_ref, wqk_ref, wvo_ref):
    mn = _rms(mem_ref[...], g_ref[...]).astype(BF16)
    k = _dot(mn, wk_ref[...].astype(BF16)).astype(BF16)
    v = _dot(mn, wv_ref[...].astype(BF16)).astype(BF16)
    wqk_ref[...] = (_dot_nt(wq_ref[...].astype(BF16), k) * (X_HEAD_DIM ** -0.5)).astype(BF16)
    wvo_ref[...] = _dot(v, wc_ref[...].astype(BF16)).astype(BF16)


def _memfold(mem, g, w_ckv, w_cq, w_co):
    M, D = mem.shape
    dh = X_HEAD_DIM
    return pl.pallas_call(
        _memfold_body,
        grid=(X_HEADS,),
        in_specs=[_resident(mem.shape), _resident(g.shape),
                  pl.BlockSpec((D, dh), lambda h: (0, h)), pl.BlockSpec((D, dh), lambda h: (0, X_HEADS + h)),
                  pl.BlockSpec((D, dh), lambda h: (0, h)), pl.BlockSpec((dh, D), lambda h: (h, 0))],
        out_specs=[pl.BlockSpec((D, M), lambda h: (0, h)), pl.BlockSpec((M, D), lambda h: (h, 0))],
        out_shape=[jax.ShapeDtypeStruct((D, X_HEADS * M), BF16), jax.ShapeDtypeStruct((X_HEADS * M, D), BF16)],
        compiler_params=_params(),
        name="memfold",
    )(mem, g, w_ckv, w_ckv, w_cq, w_co)


def _mixout_body(x_ref, dsa_ref, ml_ref, wo_ref, wqk_ref, wvo_ref, wr_ref, gx_ref, gf_ref,
                 x2_ref, hf_ref, rl_ref, p_ref):
    nd = dsa_ref.shape[1]
    x1 = x_ref[...] + _dot(dsa_ref[...], wo_ref[0:nd, :]) + _dot(ml_ref[...], wo_ref[nd:, :])
    lg_all = _dot(_rms(x1, gx_ref[...]).astype(BF16), wqk_ref[...])
    n_mem = wqk_ref.shape[1] // X_HEADS
    for hd in range(X_HEADS):
        sl = slice(hd * n_mem, (hd + 1) * n_mem)
        lg = lg_all[:, sl]
        e = jnp.exp(lg - jnp.max(lg, axis=-1, keepdims=True))
        p_ref[:, sl] = (e / jnp.sum(e, axis=-1, keepdims=True)).astype(BF16)
    x2 = x1 + _dot(p_ref[...], wvo_ref[...])
    x2_ref[...] = x2
    hf = _rms(x2, gf_ref[...]).astype(BF16)
    rl_ref[...] = _dot(hf, wr_ref[...])
    bits = lax.bitcast_convert_type(hf.astype(F32), U32)
    for c in range(_SUBLANES):
        lo = bits[:, (2 * c) * _LANES:(2 * c + 1) * _LANES]
        hi = bits[:, (2 * c + 1) * _LANES:(2 * c + 2) * _LANES]
        hf_ref[pl.ds(c, x2.shape[0], stride=_SUBLANES), :] = (hi & jnp.uint32(0xFFFF0000)) | (lo >> 16)


def _mixout(x, dsa, ml, wo, wqk, wvo, wr, gx, gf, tm):
    T = x.shape[0]
    row = lambda n: pl.BlockSpec((tm, n), lambda i: (i, 0))
    return pl.pallas_call(
        _mixout_body,
        grid=(T // tm,),
        in_specs=[row(D_MODEL), row(dsa.shape[1]), row(ml.shape[1]), _resident(wo.shape),
                  _resident(wqk.shape), _resident(wvo.shape), _resident(wr.shape), _resident(gx.shape),
                  _resident(gf.shape)],
        out_specs=[row(D_MODEL), pl.BlockSpec((tm * _SUBLANES, _LANES), lambda i: (i, 0)), row(wr.shape[1])],
        out_shape=[jax.ShapeDtypeStruct((T, D_MODEL), F32),
                   jax.ShapeDtypeStruct((T * _SUBLANES, _LANES), U32),
                   jax.ShapeDtypeStruct((T, wr.shape[1]), F32)],
        scratch_shapes=[pltpu.VMEM((tm, wqk.shape[1]), BF16)],
        compiler_params=_params(),
        name="mixout",
    )(x, dsa, ml, wo, wqk, wvo, wr, gx, gf)


def _moe_body(tok_ref, eb_ref, hf_hbm, wg_ref, wu_ref, wd_ref, y_hbm,
              xbuf, ybuf, wgb, wub, wdb, gsem, ysem, *, bm, n_blk):
    e = pl.program_id(0)
    n_valid = eb_ref[N_EXPERTS]
    b_lo = eb_ref[e]
    b_hi = eb_ref[e + 1]

    def tok_words(tok):
        return hf_hbm.at[pl.ds(pl.multiple_of(tok * _SUBLANES, _SUBLANES), _SUBLANES)]

    def gather(blk, s):
        def issue(g, carry):
            for u in range(_SUBLANES):
                r = g * _SUBLANES + u
                pltpu.make_async_copy(tok_words(tok_ref[blk * bm + r]),
                                      xbuf.at[s, pl.ds(pl.multiple_of(r * _SUBLANES, _SUBLANES), _SUBLANES)],
                                      gsem.at[s]).start(priority=1)
            return carry
        lax.fori_loop(0, bm // _SUBLANES, issue, 0)

    def wait_gather(s):
        pltpu.make_async_copy(xbuf.at[s], xbuf.at[s], gsem.at[s]).wait()

    def y_copy(blk, s):
        return pltpu.make_async_copy(ybuf.at[s], y_hbm.at[pl.ds(pl.multiple_of(blk * bm, bm), bm)], ysem.at[s])

    @pl.when(e == 0)
    def _():
        for k in range(_GATHER_AHEAD):
            gather(k, k)

    @pl.when(b_hi > b_lo)
    def _():
        wgb[...] = wg_ref[0].astype(BF16)
        wub[...] = wu_ref[0].astype(BF16)
        wdb[...] = wd_ref[0].astype(BF16)

        def block(b, carry):
            s = b % _GATHER_SLOTS
            wait_gather(s)
            chunks = []
            for c in range(_SUBLANES):
                w = xbuf[s, pl.ds(c, bm, stride=_SUBLANES), :]
                chunks.append(lax.bitcast_convert_type(w << 16, F32).astype(BF16))
                chunks.append(lax.bitcast_convert_type(w & jnp.uint32(0xFFFF0000), F32).astype(BF16))
            xb = jnp.concatenate(chunks, axis=-1)
            gate = _dot(xb, wgb[...])
            a = gate * jax.nn.sigmoid(gate) * _dot(xb, wub[...])
            y = _dot(a.astype(BF16), wdb[...])

            nxt = jnp.minimum(b + _GATHER_AHEAD, n_blk - 1)
            for r in range(bm):
                pltpu.make_async_copy(tok_words(tok_ref[nxt * bm + r]),
                                      xbuf.at[(b + _GATHER_AHEAD) % _GATHER_SLOTS, pl.ds(r * _SUBLANES, _SUBLANES)],
                                      gsem.at[(b + _GATHER_AHEAD) % _GATHER_SLOTS]).start(priority=1)

            @pl.when(b >= 2)
            def _():
                y_copy(b - 2, b % 2).wait()

            ybuf[b % 2] = y
            y_copy(b, b % 2).start()
            return carry

        lax.fori_loop(b_lo, b_hi, block, 0)

    @pl.when(e == pl.num_programs(0) - 1)
    def _():
        for k in range(_GATHER_AHEAD):
            wait_gather((n_valid + k) % _GATHER_SLOTS)

        @pl.when(n_valid >= 2)
        def _():
            y_copy(n_valid - 2, n_valid % 2).wait()
        y_copy(n_valid - 1, (n_valid - 1) % 2).wait()
        ybuf[0] = jnp.zeros(ybuf.shape[1:], F32)

        def zero_block(b, carry):
            cp = y_copy(b, 0)
            cp.start()
            cp.wait()
            return carry

        lax.fori_loop(n_valid, n_blk, zero_block, 0)


def _moe(tok, e_blk, hf, wg, wu, wd, n_blk, bm):
    D = wg.shape[1]
    wspec = lambda shape: pl.BlockSpec((1,) + shape, lambda e, *_: (e, 0, 0))
    grid_spec = pltpu.PrefetchScalarGridSpec(
        num_scalar_prefetch=2,
        grid=(N_EXPERTS,),
        in_specs=[pl.BlockSpec(memory_space=pl.ANY),
                  wspec((D, D_EXPERT)), wspec((D, D_EXPERT)), wspec((D_EXPERT, D))],
        out_specs=pl.BlockSpec(memory_space=pl.ANY),
        scratch_shapes=[pltpu.VMEM((_GATHER_SLOTS, bm * _SUBLANES, _LANES), U32),
                        pltpu.VMEM((2, bm, D), F32),
                        pltpu.VMEM((D, D_EXPERT), BF16), pltpu.VMEM((D, D_EXPERT), BF16),
                        pltpu.VMEM((D_EXPERT, D), BF16),
                        pltpu.SemaphoreType.DMA((_GATHER_SLOTS,)), pltpu.SemaphoreType.DMA((2,))],
    )
    return pl.pallas_call(
        functools.partial(_moe_body, bm=bm, n_blk=n_blk),
        grid_spec=grid_spec,
        out_shape=jax.ShapeDtypeStruct((n_blk * bm, D), F32),
        compiler_params=_params(),
        name="moe",
    )(tok, e_blk, hf, wg, wu, wd)


def _route(rl, b_group, b_router, bm):
    N = rl.shape[0]
    g_logits = rl[:, :N_GROUPS] + b_group
    g_prob = jax.nn.softmax(g_logits, axis=-1)
    g_sel = jnp.argmax(g_logits, axis=-1)
    p_g = jnp.take_along_axis(g_prob, g_sel[:, None], axis=-1)
    e_logits = rl[:, N_GROUPS:N_GROUPS + N_EXPERTS] + b_router
    in_group = (jnp.arange(N_EXPERTS, dtype=I32) // EXP_PER_GROUP)[None, :] == g_sel[:, None]
    e_prob = jax.nn.softmax(jnp.where(in_group, e_logits, -jnp.inf), axis=-1)
    e_cand = jnp.where(in_group, e_prob, -1.0)
    tops = []
    for _ in range(TOPK_IN_GROUP):
        best = jnp.argmax(e_cand, axis=-1).astype(I32)
        tops.append((jnp.max(e_cand, axis=-1), best))
        e_cand = jnp.where(jnp.arange(N_EXPERTS, dtype=I32)[None, :] == best[:, None], -2.0, e_cand)
    top_p = jnp.stack([p for p, _ in tops], axis=-1)
    expert_id = jnp.stack([e for _, e in tops], axis=-1)
    gates = p_g * top_p / top_p.sum(-1, keepdims=True)

    A = N * TOPK_IN_GROUP
    flat_e = expert_id.reshape(A)
    onehot = (flat_e[:, None] == jnp.arange(N_EXPERTS, dtype=I32)[None, :])
    seg = _RANK_SEG if A % _RANK_SEG == 0 else A
    oh = onehot.astype(BF16).reshape(A // seg, seg, N_EXPERTS)
    before = (jnp.arange(seg)[None, :] < jnp.arange(seg)[:, None]).astype(BF16)
    within = jnp.einsum('ij,bjk->bik', before, oh, preferred_element_type=F32)
    seg_tot = jnp.sum(oh.astype(F32), axis=1)
    seg_base = jnp.cumsum(seg_tot, axis=0) - seg_tot
    rank = jnp.sum((within + seg_base[:, None, :]) * oh.astype(F32), axis=-1).reshape(A).astype(I32)
    counts = jnp.sum(seg_tot, axis=0).astype(I32)
    padded = (counts + bm - 1) // bm * bm
    pad_ends = jnp.cumsum(padded)
    pad_starts = pad_ends - padded
    row = jnp.sum(jnp.where(onehot, pad_starts[None, :], 0), axis=1) + rank
    n_blk = -(-A // bm) + N_EXPERTS
    row_tok = jnp.zeros((n_blk * bm,), I32).at[row].set(jnp.arange(A, dtype=I32) // TOPK_IN_GROUP)
    e_blk = jnp.concatenate([pad_starts, pad_ends[-1:]]) // bm
    return row_tok, row, gates, e_blk.astype(I32), n_blk


def _final_body(row_ref, x_ref, gate_ref, y_hbm, g_ref, out_ref, ybuf, sem, *, tm):
    i = pl.program_id(0)
    slot = i % 2

    def gather(tile, s):
        def issue(g, carry):
            for u in range(_SUBLANES):
                for k in range(TOPK_IN_GROUP):
                    src = row_ref[(tile * tm + g * _SUBLANES + u) * TOPK_IN_GROUP + k]
                    pltpu.make_async_copy(y_hbm.at[pl.ds(src, 1)], ybuf.at[s, k, g, pl.ds(u, 1)],
                                          sem.at[s]).start()
            return carry
        lax.fori_loop(0, tm // _SUBLANES, issue, 0)

    def wait_rows(s):
        pltpu.make_async_copy(ybuf.at[s], ybuf.at[s], sem.at[s]).wait()

    @pl.when(i == 0)
    def _():
        gather(0, 0)

    wait_rows(slot)
    acc = x_ref[...]
    for k in range(TOPK_IN_GROUP):
        acc = acc + gate_ref[:, k:k + 1] * ybuf[slot, k].reshape(tm, ybuf.shape[-1])
    y = _rms(acc, g_ref[...])

    nxt = jnp.minimum(i + 1, pl.num_programs(0) - 1)
    for r in range(tm):
        for k in range(TOPK_IN_GROUP):
            src = row_ref[(nxt * tm + r) * TOPK_IN_GROUP + k]
            pltpu.make_async_copy(y_hbm.at[pl.ds(src, 1)],
                                  ybuf.at[1 - slot, k, r // _SUBLANES, pl.ds(r % _SUBLANES, 1)],
                                  sem.at[1 - slot]).start()
    out_ref[...] = y

    @pl.when(i == pl.num_programs(0) - 1)
    def _():
        wait_rows(1 - slot)


def _final(row, x2, gates, y_rows, g, tm):
    T, D = x2.shape
    grid_spec = pltpu.PrefetchScalarGridSpec(
        num_scalar_prefetch=1,
        grid=(T // tm,),
        in_specs=[pl.BlockSpec((tm, D), lambda i, *_: (i, 0)),
                  pl.BlockSpec((tm, TOPK_IN_GROUP), lambda i, *_: (i, 0)),
                  pl.BlockSpec(memory_space=pl.ANY),
                  pl.BlockSpec(g.shape, lambda i, *_: (0, 0))],
        out_specs=pl.BlockSpec((tm, D), lambda i, *_: (i, 0)),
        scratch_shapes=[pltpu.VMEM((2, TOPK_IN_GROUP, tm // _SUBLANES, _SUBLANES, D), F32),
                        pltpu.SemaphoreType.DMA((2,))],
    )
    return pl.pallas_call(
        functools.partial(_final_body, tm=tm),
        grid_spec=grid_spec,
        out_shape=jax.ShapeDtypeStruct((T, D), F32),
        compiler_params=_params(),
        name="final",
    )(row, x2, gates, y_rows, g)


def _tile_sizes(T):
    pick = lambda want: want if T % want == 0 else CHUNK
    return dict(inproj=pick(256), dsa_q=pick(256), dsa_k=pick(512), mlstm=pick(256), mixout=pick(256),
                final=pick(256), moe=128)


def _layer(x, mem, norm_mix_g, w_in, kv_norm_g, k_idx_norm_g, w_uk, w_uv, conv_w, conv_b, gate_b, ml_norm_g,
           w_out, norm_x_g, mem_norm_g, w_cq, w_ckv, w_co, norm_ffn_g, w_group, b_group, w_router, b_router,
           w_gate, w_up, w_down, out_g):
    T = x.shape[0]
    ts = _tile_sizes(T)
    r2 = lambda v: v.reshape(1, -1)

    w_r = _wprep(w_in, 256)
    wuk_t = jnp.transpose(w_uk, (1, 2, 0)).astype(BF16)
    wuv_t = jnp.transpose(w_uv, (1, 0, 2)).astype(BF16)

    qabs, ckv, qi, kidx, small, mqk, mv, mo = _inproj(
        x, r2(norm_mix_g), w_r, wuk_t, r2(kv_norm_g), r2(k_idx_norm_g), ts["inproj"])

    gate_rows = jnp.transpose(small[:, _S_WI:_S_MF + ML_HEADS])
    dsa_out = _dsa(qi, gate_rows, qabs, kidx, ckv, wuv_t, ts["dsa_q"], ts["dsa_k"])

    gb_col = jnp.zeros((1, _SMALL), F32).at[0, _S_MI:_S_MI + 2 * ML_HEADS].set(gate_b)
    ml_out = _mlstm(mqk, mv, small, gate_rows, mo, conv_w, r2(conv_b), gb_col, gate_b.reshape(-1, 1),
                    r2(ml_norm_g), ts["mlstm"])

    wqk, wvo = _memfold(mem, r2(mem_norm_g), w_ckv, w_cq, w_co)
    w_rt = jnp.concatenate([w_group, w_router,
                            jnp.zeros((D_MODEL, 128 - N_GROUPS - N_EXPERTS), w_group.dtype)], axis=1)
    x2, hf, rl = _mixout(x, dsa_out, ml_out, w_out.astype(BF16), wqk, wvo, w_rt.astype(BF16),
                         r2(norm_x_g), r2(norm_ffn_g), ts["mixout"])

    bm = ts["moe"]
    row_tok, row, gates, e_blk, n_blk = _route(rl, b_group, b_router, bm)
    y_rows = _moe(row_tok, e_blk, hf, w_gate, w_up, w_down, n_blk, bm)
    return _final(row, x2, gates, y_rows, r2(out_g), ts["final"])


def kernel(x, mem, norm_mix_g, w_in, kv_norm_g, k_idx_norm_g, w_uk, w_uv, conv_w, conv_b, gate_b, ml_norm_g,
           w_out, norm_x_g, mem_norm_g, w_cq, w_ckv, w_co, norm_ffn_g, w_group, b_group, w_router, b_router,
           w_gate, w_up, w_down, final_norm_g):
    B, T, D = x.shape
    assert B == 1 and D == D_MODEL and norm_mix_g.shape[0] == 1 and T % CHUNK == 0
    out = _layer(x[0], mem[0], norm_mix_g[0], w_in, kv_norm_g[0], k_idx_norm_g[0], w_uk[0], w_uv[0],
                 conv_w[0], conv_b[0], gate_b[0], ml_norm_g[0], w_out[0], norm_x_g[0], mem_norm_g[0],
                 w_cq[0], w_ckv[0], w_co[0], norm_ffn_g[0], w_group[0], b_group[0], w_router[0], b_router[0],
                 w_gate[0], w_up[0], w_down[0], final_norm_g)
    return out[None]
```

```python
import functools

import jax
import jax.numpy as jnp
import numpy as np
from jax import lax
from jax.experimental import pallas as pl
from jax.experimental.pallas import tpu as pltpu

F32 = jnp.float32
BF16 = jnp.bfloat16
I32 = jnp.int32
I16 = jnp.int16
U32 = jnp.uint32

EPS = 1e-6
CHUNK = 64
D_MODEL = 2048

DSA_HEADS = 8
DSA_HEAD_DIM = 128
DSA_LATENT = 256
IDX_HEADS = 8
IDX_DIM = 64
TOPK_MAX = 256

ML_HEADS = 4
ML_QK_DIM = 128
ML_V_DIM = 256
CONV_W = 4

X_HEADS = 4
X_HEAD_DIM = D_MODEL // X_HEADS

N_GROUPS = 4
EXP_PER_GROUP = 8
N_EXPERTS = N_GROUPS * EXP_PER_GROUP
TOPK_IN_GROUP = 2
D_EXPERT = 512

_O_DQ = 0
_O_CKV = _O_DQ + DSA_HEADS * DSA_HEAD_DIM
_O_QI = _O_CKV + DSA_LATENT
_O_KI = _O_QI + IDX_HEADS * IDX_DIM
_O_WI = _O_KI + IDX_DIM
_O_MQ = _O_WI + IDX_HEADS
_O_MK = _O_MQ + ML_HEADS * ML_QK_DIM
_O_MV = _O_MK + ML_HEADS * ML_QK_DIM
_O_MI = _O_MV + ML_HEADS * ML_V_DIM
_O_MF = _O_MI + ML_HEADS
_O_MO = _O_MF + ML_HEADS
_O_END = _O_MO + ML_HEADS * ML_V_DIM

_G_DQ = (0, 1024)
_G_CKV = (1024, 1280)
_G_QI = (1280, 1792)
_G_SMALL = (1792, 1920)
_G_MQK = (1920, 2944)
_G_MV = (2944, 3968)
_G_MO = (3968, 4992)
_W_COLS = 4992
_S_WI = IDX_DIM
_S_MI = _S_WI + IDX_HEADS
_S_MF = _S_MI + ML_HEADS
_SMALL = 128

_VMEM_LIMIT = 56 * 1024 * 1024
_INT_MIN = -(2 ** 31)
_I16_MIN = -(2 ** 15)
_CHUNK_SHIFT = CHUNK.bit_length() - 1
_LOG2E = 1.4426950408889634
_SUBLANES = 8
_LANES = 128
_GATHER_SLOTS = 9
_GATHER_AHEAD = _GATHER_SLOTS - 1
_FINAL_SLOTS = 3
_RANK_SEG = 512
_NEG = -1e30


def _rms(v, g):
    return v * lax.rsqrt(jnp.mean(v * v, axis=-1, keepdims=True) + EPS) * g


def _dot(a, b):
    return jnp.dot(a, b, preferred_element_type=F32)


def _dot_nt(a, b):
    return lax.dot_general(a, b, (((1,), (1,)), ((), ())), preferred_element_type=F32)


def _resident(shape):
    nd = len(shape)
    return pl.BlockSpec(shape, lambda *_: (0,) * nd, pipeline_mode=pl.Buffered(1))


def _params(n_axes=1):
    return pltpu.CompilerParams(dimension_semantics=("arbitrary",) * n_axes,
                                vmem_limit_bytes=_VMEM_LIMIT)


def _wprep_body(w_hbm, out_ref, wbuf, sem, *, tk):
    i = pl.program_id(0)
    slot = i % 2

    def rows(blk, s):
        return pltpu.make_async_copy(w_hbm.at[0, pl.ds(pl.multiple_of(blk * tk, tk), tk)], wbuf.at[s], sem.at[s])

    @pl.when(i == 0)
    def _():
        rows(0, 0).start()

    @pl.when(i + 1 < pl.num_programs(0))
    def _():
        rows(i + 1, 1 - slot).start()

    rows(i, slot).wait()
    w_ref = wbuf.at[slot]
    n_small = IDX_DIM + IDX_HEADS
    out_ref[:, _G_DQ[0]:_G_SMALL[0] + n_small] = w_ref[:, _O_DQ:_O_MQ].astype(BF16)
    out_ref[:, _G_SMALL[0] + n_small:_G_SMALL[0] + n_small + 2 * ML_HEADS] = w_ref[:, _O_MI:_O_MO].astype(BF16)
    out_ref[:, _G_SMALL[0] + n_small + 2 * ML_HEADS:_G_SMALL[1]] = jnp.zeros(
        (out_ref.shape[0], _SMALL - n_small - 2 * ML_HEADS), BF16)
    out_ref[:, _G_MQK[0]:_G_MV[1]] = w_ref[:, _O_MQ:_O_MI].astype(BF16)
    out_ref[:, _G_MO[0]:_G_MO[1]] = w_ref[:, _O_MO:_O_END].astype(BF16)


def _wprep(w_in, tk):
    K = w_in.shape[1]
    return pl.pallas_call(
        functools.partial(_wprep_body, tk=tk),
        grid=(K // tk,),
        in_specs=[pl.BlockSpec(memory_space=pl.ANY)],
        out_specs=pl.BlockSpec((tk, _W_COLS), lambda i: (i, 0)),
        out_shape=jax.ShapeDtypeStruct((K, _W_COLS), BF16),
        scratch_shapes=[pltpu.VMEM((2, tk, w_in.shape[2]), F32), pltpu.SemaphoreType.DMA((2,))],
        compiler_params=_params(),
        name="wprep",
    )(w_in)


def _inproj_body(x_ref, g_ref, w_ref, wuk_ref, kvg_ref, kig_ref,
                 qabs_ref, ckv_ref, qi_ref, kidx_ref, small_ref, mqk_ref, mv_ref, mo_ref):
    h = _rms(x_ref[...], g_ref[...]).astype(BF16)

    def proj(grp):
        return _dot(h, w_ref[:, grp[0]:grp[1]])

    dq = proj(_G_DQ)
    for hd in range(DSA_HEADS):
        qh = dq[:, hd * DSA_HEAD_DIM:(hd + 1) * DSA_HEAD_DIM].astype(BF16)
        qa = _dot(qh, wuk_ref[hd]) * (DSA_HEAD_DIM ** -0.5 * _LOG2E)
        qabs_ref[:, hd * DSA_LATENT:(hd + 1) * DSA_LATENT] = qa.astype(BF16)
    ckv_ref[...] = _rms(proj(_G_CKV), kvg_ref[...]).astype(BF16)
    qi_ref[...] = (proj(_G_QI) * (IDX_DIM ** -0.5)).astype(BF16)
    small = proj(_G_SMALL)
    small_ref[...] = small
    kidx_ref[...] = _rms(small[:, :IDX_DIM], kig_ref[...]).astype(BF16)
    mqk_ref[...] = proj(_G_MQK)
    mv_ref[...] = proj(_G_MV).astype(BF16)
    mo_ref[...] = proj(_G_MO)


def _inproj(x, g, w, wuk, kvg, kig, tm):
    T = x.shape[0]
    row = lambda n: pl.BlockSpec((tm, n), lambda i: (i, 0))
    outs = [(8 * DSA_LATENT, BF16), (DSA_LATENT, BF16), (IDX_HEADS * IDX_DIM, BF16), (IDX_DIM, BF16),
            (_SMALL, F32), (2 * ML_HEADS * ML_QK_DIM, F32), (ML_HEADS * ML_V_DIM, BF16),
            (ML_HEADS * ML_V_DIM, F32)]
    return pl.pallas_call(
        _inproj_body,
        grid=(T // tm,),
        in_specs=[row(D_MODEL), _resident(g.shape), _resident(w.shape), _resident(wuk.shape),
                  _resident(kvg.shape), _resident(kig.shape)],
        out_specs=[row(n) for n, _ in outs],
        out_shape=[jax.ShapeDtypeStruct((T, n), dt) for n, dt in outs],
        compiler_params=_params(),
        name="inproj",
    )(x, g, w, wuk, kvg, kig)


def _sublane_fold(v, op, rows=_SUBLANES, ways=4):
    groups = [v[r * rows:(r + 1) * rows, :] for r in range(v.shape[0] // rows)]
    accs = groups[:ways]
    for r in range(ways, len(groups)):
        accs[r % ways] = op(accs[r % ways], groups[r])
    while len(accs) > 1:
        accs = [op(accs[k], accs[k + 1]) if k + 1 < len(accs) else accs[k] for k in range(0, len(accs), 2)]
    return accs[0]


def _bit_transpose32(words):
    a = list(words)
    j, m = 16, 0x0000FFFF
    while j:
        k = 0
        while k < 32:
            t = (a[k] ^ (a[k + j] >> j)) & jnp.uint32(m)
            a[k] = a[k] ^ t
            a[k + j] = a[k + j] ^ (t << j)
            k = (k + j + 1) & ~j
        j >>= 1
        m = (m ^ (m << j)) & 0xFFFFFFFF
    return a


def _dsa_body(qi_ref, wrow_ref, qabs_ref, kidx_ref, ckv_ref, ckvt_ref, wuv_ref, out_ref,
              key_ref, planes_ref, alive_ref, lga_ref, lgb_ref, m_ref, l_ref, acc_ref, *, tq, tk, topk, nbits_idx):
    i = pl.program_id(0)
    n_kb = ((i + 1) * tq + tk - 1) // tk
    w_rows = wrow_ref[0:IDX_HEADS, :] * (IDX_HEADS ** -0.5)
    q_chunk = (i * tq + lax.broadcasted_iota(I32, (1, tq), 1)) >> _CHUNK_SHIFT

    def key_pos(j):
        return j * tk + lax.broadcasted_iota(I32, (tk, 1), 0)

    def score_block(j, carry):
        kx = kidx_ref[pl.ds(pl.multiple_of(j * tk, tk), tk), :]
        s = jnp.zeros((tk, tq), F32)
        for hd in range(IDX_HEADS):
            d = _dot_nt(kx, qi_ref[:, hd * IDX_DIM:(hd + 1) * IDX_DIM])
            s = s + w_rows[hd:hd + 1, :] * jnp.maximum(d, 0.0)
        bits = lax.bitcast_convert_type(s, I32)
        key = bits ^ ((bits >> 31) & 0x7FFFFFFF)
        key_ref[j] = jnp.where((key_pos(j) >> _CHUNK_SHIFT) <= q_chunk, key, _INT_MIN)
        return carry

    lax.fori_loop(0, n_kb, score_block, 0)

    def count(pred):
        def body(j, acc):
            hit = pred(key_ref[j], key_pos(j)).astype(I32)
            return acc + _sublane_fold(hit, jnp.add)
        acc = lax.fori_loop(0, n_kb, body, jnp.zeros((8, tq), I32))
        return jnp.sum(acc, axis=0, keepdims=True)

    plane_rows = tk // 32

    @pl.when(i == 0)
    def _():
        planes_ref[...] = jnp.zeros(planes_ref.shape, U32)

    def slice_block(j, carry):
        u = lax.bitcast_convert_type(key_ref[j], U32) ^ jnp.uint32(0x80000000)
        for h in range(plane_rows // _SUBLANES):
            base = h * 32 * _SUBLANES
            planes = _bit_transpose32([u[base + _SUBLANES * v:base + _SUBLANES * (v + 1), :] for v in range(32)])
            row0 = pl.multiple_of(j * plane_rows + h * _SUBLANES, _SUBLANES)
            for b in range(32):
                planes_ref[b, pl.ds(row0, _SUBLANES), :] = planes[31 - b]
        return carry

    lax.fori_loop(0, n_kb, slice_block, 0)
    word_row = lax.broadcasted_iota(I32, (alive_ref.shape[0], 1), 0)
    alive_ref[...] = jnp.where(word_row < n_kb * plane_rows, jnp.full(alive_ref.shape, 0xFFFFFFFF, U32),
                               jnp.uint32(0))

    def radix_select(n_rows):
        def ones_count(words):
            return jnp.sum(_sublane_fold(lax.population_count(words).astype(I32), jnp.add), axis=0, keepdims=True)

        def select_bit(it, carry):
            need, t_u = carry
            b = 31 - it
            alive = alive_ref[0:n_rows, :]
            ones = alive & planes_ref[b, pl.ds(0, n_rows), :]
            n_ones = ones_count(ones)
            take = n_ones >= need
            alive_ref[0:n_rows, :] = jnp.where(take, ones, alive ^ ones)
            bit = lax.shift_left(jnp.uint32(1), jnp.asarray(b, U32))
            return jnp.where(take, need, need - n_ones), jnp.where(take, t_u | bit, t_u)

        need, t_u = lax.fori_loop(0, 32, select_bit, (jnp.full((1, tq), topk, I32), jnp.zeros((1, tq), U32)))
        return need, t_u, ones_count(alive_ref[0:n_rows, :])

    all_rows = alive_ref.shape[0]
    if all_rows % (2 * _SUBLANES) == 0:
        need, t_u, n_equal = lax.cond(n_kb * plane_rows <= all_rows // 2,
                                      lambda: radix_select(all_rows // 2), lambda: radix_select(all_rows))
    else:
        need, t_u, n_equal = radix_select(all_rows)
    short = t_u == 0
    t = jnp.maximum(lax.bitcast_convert_type(t_u ^ jnp.uint32(0x80000000), I32), _INT_MIN + 1)
    all_pos = jnp.int32(2 ** nbits_idx - 1)
    has_ties = (n_equal > need) & jnp.logical_not(short)
    n_tie_take = jnp.where(has_ties, need, all_pos)

    def tie_cutoff():
        def pos_bit(b, c):
            cand = c + lax.shift_left(jnp.int32(1), nbits_idx - 1 - b)
            f = count(lambda kb, pos: (kb == t) & (pos < cand))
            return jnp.where(f <= n_tie_take, cand, c)
        return lax.fori_loop(0, nbits_idx, pos_bit, jnp.zeros((1, tq), I32))

    cut = lax.cond(jnp.max(has_ties.astype(I32)) > 0, tie_cutoff, lambda: jnp.full((1, tq), all_pos, I32))

    m_ref[...] = jnp.full(m_ref.shape, _NEG, F32)
    l_ref[...] = jnp.zeros(l_ref.shape, F32)
    acc_ref[...] = jnp.zeros(acc_ref.shape, F32)
    last_blk = ckvt_ref.shape[0] - 1

    def logits(j, lg_buf):
        c_blk = ckv_ref[pl.ds(pl.multiple_of(jnp.minimum(j, last_blk) * tk, tk), tk), :]
        kb = key_ref[j]
        sel = (kb > t) | ((kb == t) & (key_pos(j) < cut))
        bias = jnp.where(sel, 0.0, _NEG).astype(F32)
        for hd in range(DSA_HEADS):
            lg_buf[hd] = _dot_nt(c_blk, qabs_ref[:, hd * DSA_LATENT:(hd + 1) * DSA_LATENT]) + bias

    def accumulate(j, lg_buf):
        c_blk_t = ckvt_ref[jnp.minimum(j, last_blk)]
        for hd in range(DSA_HEADS):
            lg = lg_buf[hd]
            m_old = m_ref[hd:hd + 1, :]
            m_new = jnp.maximum(m_old, jnp.max(_sublane_fold(lg, jnp.maximum, ways=1), axis=0, keepdims=True))
            p = jnp.exp2(lg - m_new)
            alpha = jnp.exp2(m_old - m_new)
            l_ref[hd:hd + 1, :] = alpha * l_ref[hd:hd + 1, :] + jnp.sum(_sublane_fold(p, jnp.add, ways=1), axis=0,
                                                                         keepdims=True)
            acc_ref[hd] = alpha * acc_ref[hd] + _dot(c_blk_t, p.astype(BF16))
            m_ref[hd:hd + 1, :] = m_new

    logits(0, lga_ref)

    def attn_pair(mi, carry):
        ja = 2 * mi
        accumulate(ja, lga_ref)
        logits(ja + 1, lgb_ref)
        accumulate(ja + 1, lgb_ref)
        logits(jnp.minimum(ja + 2, n_kb - 1), lga_ref)
        return carry

    lax.fori_loop(0, n_kb // 2, attn_pair, 0)

    @pl.when(n_kb % 2 == 1)
    def _():
        accumulate(n_kb - 1, lga_ref)

    for hd in range(DSA_HEADS):
        o_lat = (acc_ref[hd] / l_ref[hd:hd + 1, :]).T.astype(BF16)
        out_ref[:, hd * DSA_HEAD_DIM:(hd + 1) * DSA_HEAD_DIM] = _dot(o_lat, wuv_ref[hd]).astype(BF16)


def _dsa(qi, wrows, qabs, kidx, ckv, wuv, tq, tk):
    T = qi.shape[0]
    topk = min(TOPK_MAX, T // 4)
    n_kb = T // tk
    ckvt = jnp.transpose(ckv.reshape(n_kb, tk, DSA_LATENT), (0, 2, 1))
    row = lambda n: pl.BlockSpec((tq, n), lambda i: (i, 0))
    body = functools.partial(_dsa_body, tq=tq, tk=tk, topk=topk, nbits_idx=int(T).bit_length())
    return pl.pallas_call(
        body,
        grid=(T // tq,),
        in_specs=[row(qi.shape[1]), pl.BlockSpec((wrows.shape[0], tq), lambda i: (0, i)), row(qabs.shape[1]),
                  _resident(kidx.shape), _resident(ckv.shape), _resident(ckvt.shape), _resident(wuv.shape)],
        out_specs=row(DSA_HEADS * DSA_HEAD_DIM),
        out_shape=jax.ShapeDtypeStruct((T, DSA_HEADS * DSA_HEAD_DIM), BF16),
        scratch_shapes=[pltpu.VMEM((n_kb + n_kb % 2, tk, tq), I32),
                        pltpu.VMEM((32, T // 32, tq), U32), pltpu.VMEM((T // 32, tq), U32),
                        pltpu.VMEM((DSA_HEADS, tk, tq), F32), pltpu.VMEM((DSA_HEADS, tk, tq), F32),
                        pltpu.VMEM((DSA_HEADS, tq), F32),
                        pltpu.VMEM((DSA_HEADS, tq), F32), pltpu.VMEM((DSA_HEADS, DSA_LATENT, tq), F32)],
        compiler_params=_params(),
        name="dsa",
    )(qi, wrows, qabs, kidx, ckv, ckvt, wuv)


def _log_sigmoid(v):
    return jnp.minimum(v, 0.0) - jnp.log1p(jnp.exp(-jnp.abs(v)))


def _chunk_cumsum(v, axis):
    pos = lax.broadcasted_iota(I32, v.shape, axis) & (CHUNK - 1)
    d = 1
    while d < CHUNK:
        v = v + jnp.where(pos >= d, pltpu.roll(v, d, axis=axis), 0.0)
        d *= 2
    return v


def _mlstm_body(mqk_ref, mv_ref, small_ref, gt_ref, mo_ref, cw_ref, cb_ref, gbc_ref, gbr_ref, ng_ref,
                out_ref, xe_ref, c_ref, n_ref, m_ref, hs_ref, *, rows):
    @pl.when(pl.program_id(0) == 0)
    def _():
        xe_ref[0:8, :] = jnp.zeros((8, xe_ref.shape[1]), F32)
        c_ref[...] = jnp.zeros(c_ref.shape, F32)
        n_ref[...] = jnp.zeros(n_ref.shape, F32)
        m_ref[...] = jnp.zeros(m_ref.shape, F32)

    x = mqk_ref[...]
    xe_ref[8:8 + rows, :] = x
    y = cb_ref[...]
    for j in range(CONV_W - 1):
        y = y + xe_ref[5 + j:5 + j + rows, :] * cw_ref[j:j + 1, :]
    y = y + x * cw_ref[CONV_W - 1:CONV_W, :]
    xe_ref[0:8, :] = x[rows - 8:rows, :]
    qk = y * jax.nn.sigmoid(y)
    nqk = ML_HEADS * ML_QK_DIM
    q_all = (qk[:, :nqk] * (ML_QK_DIM ** -0.5)).astype(BF16)
    k_all = qk[:, nqk:]

    g_col = small_ref[...] + gbc_ref[...]
    g_row = gt_ref[...] + gbr_ref[...]
    b_col = _chunk_cumsum(_log_sigmoid(g_col), 0)
    b_row = _chunk_cumsum(_log_sigmoid(g_row), 1)

    tri = lax.broadcasted_iota(I32, (CHUNK, CHUNK), 1) <= lax.broadcasted_iota(I32, (CHUNK, CHUNK), 0)

    heads = range(ML_HEADS)
    for c in range(rows // CHUNK):
        lo, hi = c * CHUNK, (c + 1) * CHUNK
        bc = [b_col[lo:hi, _S_MF + hd:_S_MF + hd + 1] for hd in heads]
        lic = [g_col[lo:hi, _S_MI + hd:_S_MI + hd + 1] for hd in heads]
        br = [b_row[ML_HEADS + hd:ML_HEADS + hd + 1, lo:hi] for hd in heads]
        lir = [g_row[hd:hd + 1, lo:hi] for hd in heads]
        g_tot = [bc[hd][CHUNK - 1:CHUNK, :] for hd in heads]
        m_prev = [m_ref[hd][:, 0:1] for hd in heads]
        qh = [q_all[lo:hi, hd * ML_QK_DIM:(hd + 1) * ML_QK_DIM] for hd in heads]
        kh = [k_all[lo:hi, hd * ML_QK_DIM:(hd + 1) * ML_QK_DIM] for hd in heads]
        vh = [mv_ref[lo:hi, hd * ML_V_DIM:(hd + 1) * ML_V_DIM] for hd in heads]
        c_prev = [c_ref[hd] for hd in heads]
        n_prev = [n_ref[hd] for hd in heads]

        dmat = [jnp.where(tri, bc[hd] - br[hd] + lir[hd], -jnp.inf) for hd in heads]
        inter = [bc[hd] + m_prev[hd] for hd in heads]
        m_t = [jnp.maximum(inter[hd], jnp.max(dmat[hd], axis=-1, keepdims=True)) for hd in heads]
        w_intra = [jnp.exp(dmat[hd] - m_t[hd]) for hd in heads]
        a_inter = [jnp.exp(inter[hd] - m_t[hd]) for hd in heads]
        m_new = [jnp.maximum(g_tot[hd] + m_prev[hd],
                             jnp.max(g_tot[hd] - br[hd] + lir[hd], axis=-1, keepdims=True)) for hd in heads]
        a_state = [jnp.exp(g_tot[hd] + m_prev[hd] - m_new[hd]) for hd in heads]
        wk = [jnp.exp(g_tot[hd] - bc[hd] + lic[hd] - m_new[hd]) * kh[hd] for hd in heads]

        s_qk = [_dot_nt(qh[hd], kh[hd].astype(BF16)) * w_intra[hd] for hd in heads]
        read = [_dot(qh[hd], c_prev[hd].astype(BF16)) for hd in heads]
        d_c = [_dot(wk[hd].T.astype(BF16), vh[hd]) for hd in heads]
        num = [a_inter[hd] * read[hd] + _dot(s_qk[hd].astype(BF16), vh[hd]) for hd in heads]
        den = [a_inter[hd] * jnp.sum(qh[hd].astype(F32) * n_prev[hd], axis=-1, keepdims=True)
               + jnp.sum(s_qk[hd], axis=-1, keepdims=True) for hd in heads]
        for hd in heads:
            hs_ref[lo:hi, hd * ML_V_DIM:(hd + 1) * ML_V_DIM] = (
                num[hd] / jnp.maximum(jnp.abs(den[hd]), jnp.exp(-m_t[hd])))
        for hd in heads:
            c_ref[hd] = a_state[hd] * c_prev[hd] + d_c[hd]
            n_ref[hd] = a_state[hd] * n_prev[hd] + jnp.sum(wk[hd], axis=0, keepdims=True)
            m_ref[hd] = jnp.broadcast_to(m_new[hd], m_ref.shape[1:])

    for hd in range(ML_HEADS):
        sl = slice(hd * ML_V_DIM, (hd + 1) * ML_V_DIM)
        out_ref[:, sl] = (_rms(hs_ref[:, sl], ng_ref[:, sl]) * jax.nn.sigmoid(mo_ref[:, sl])).astype(BF16)


def _mlstm(mqk, mv, small, gt, mo, cw, cb, gbc, gbr, ng, rows):
    T = mqk.shape[0]
    row = lambda n: pl.BlockSpec((rows, n), lambda i: (i, 0))
    nv = ML_HEADS * ML_V_DIM
    return pl.pallas_call(
        functools.partial(_mlstm_body, rows=rows),
        grid=(T // rows,),
        in_specs=[row(mqk.shape[1]), row(nv), row(_SMALL), pl.BlockSpec((8, rows), lambda i: (1, i)), row(nv),
                  _resident(cw.shape), _resident(cb.shape), _resident(gbc.shape), _resident(gbr.shape),
                  _resident(ng.shape)],
        out_specs=row(nv),
        out_shape=jax.ShapeDtypeStruct((T, nv), BF16),
        scratch_shapes=[pltpu.VMEM((rows + 8, mqk.shape[1]), F32),
                        pltpu.VMEM((ML_HEADS, ML_QK_DIM, ML_V_DIM), F32),
                        pltpu.VMEM((ML_HEADS, 1, ML_QK_DIM), F32),
                        pltpu.VMEM((ML_HEADS, 1, 128), F32),
                        pltpu.VMEM((rows, nv), F32)],
        compiler_params=_params(),
        name="mlstm",
    )(mqk, mv, small, gt, mo, cw, cb, gbc, gbr, ng)


def _memfold_body(mem_ref, g_ref, wk_ref, wv_ref, wq_ref, wc_ref, wqk_ref, wvo_ref):
    mn = _rms(mem_ref[...], g_ref[...]).astype(BF16)
    k = _dot(mn, wk_ref[...].astype(BF16)).astype(BF16)
    v = _dot(mn, wv_ref[...].astype(BF16)).astype(BF16)
    wqk_ref[...] = (_dot_nt(wq_ref[...].astype(BF16), k) * (X_HEAD_DIM ** -0.5)).astype(BF16)
    wvo_ref[...] = _dot(v, wc_ref[...].astype(BF16)).astype(BF16)


def _memfold(mem, g, w_ckv, w_cq, w_co):
    M, D = mem.shape
    dh = X_HEAD_DIM
    return pl.pallas_call(
        _memfold_body,
        grid=(X_HEADS,),
        in_specs=[_resident(mem.shape), _resident(g.shape),
                  pl.BlockSpec((D, dh), lambda h: (0, h)), pl.BlockSpec((D, dh), lambda h: (0, X_HEADS + h)),
                  pl.BlockSpec((D, dh), lambda h: (0, h)), pl.BlockSpec((dh, D), lambda h: (h, 0))],
        out_specs=[pl.BlockSpec((D, M), lambda h: (0, h)), pl.BlockSpec((M, D), lambda h: (h, 0))],
        out_shape=[jax.ShapeDtypeStruct((D, X_HEADS * M), BF16), jax.ShapeDtypeStruct((X_HEADS * M, D), BF16)],
        compiler_params=_params(),
        name="memfold",
    )(mem, g, w_ckv, w_ckv, w_cq, w_co)


def _mixout_body(x_ref, dsa_ref, ml_ref, wo_ref, wqk_ref, wvo_ref, wr_ref, gx_ref, gf_ref,
                 x2_ref, hf_ref, rl_ref, p_ref):
    nd = dsa_ref.shape[1]
    x1 = x_ref[...] + _dot(dsa_ref[...], wo_ref[0:nd, :]) + _dot(ml_ref[...], wo_ref[nd:, :])
    lg_all = _dot(_rms(x1, gx_ref[...]).astype(BF16), wqk_ref[...])
    n_mem = wqk_ref.shape[1] // X_HEADS
    for hd in range(X_HEADS):
        sl = slice(hd * n_mem, (hd + 1) * n_mem)
        lg = lg_all[:, sl]
        e = jnp.exp(lg - jnp.max(lg, axis=-1, keepdims=True))
        p_ref[:, sl] = (e / jnp.sum(e, axis=-1, keepdims=True)).astype(BF16)
    x2 = x1 + _dot(p_ref[...], wvo_ref[...])
    x2_ref[...] = x2
    hf = _rms(x2, gf_ref[...]).astype(BF16)
    rl_ref[...] = _dot(hf, wr_ref[...])
    bits = lax.bitcast_convert_type(hf.astype(F32), U32)
    for c in range(_SUBLANES):
        lo = bits[:, (2 * c) * _LANES:(2 * c + 1) * _LANES]
        hi = bits[:, (2 * c + 1) * _LANES:(2 * c + 2) * _LANES]
        hf_ref[pl.ds(c, x2.shape[0], stride=_SUBLANES), :] = (hi & jnp.uint32(0xFFFF0000)) | (lo >> 16)


def _mixout(x, dsa, ml, wo, wqk, wvo, wr, gx, gf, tm):
    T = x.shape[0]
    row = lambda n: pl.BlockSpec((tm, n), lambda i: (i, 0))
    return pl.pallas_call(
        _mixout_body,
        grid=(T // tm,),
        in_specs=[row(D_MODEL), row(dsa.shape[1]), row(ml.shape[1]), _resident(wo.shape),
                  _resident(wqk.shape), _resident(wvo.shape), _resident(wr.shape), _resident(gx.shape),
                  _resident(gf.shape)],
        out_specs=[row(D_MODEL), pl.BlockSpec((tm * _SUBLANES, _LANES), lambda i: (i, 0)), row(wr.shape[1])],
        out_shape=[jax.ShapeDtypeStruct((T, D_MODEL), F32),
                   jax.ShapeDtypeStruct((T * _SUBLANES, _LANES), U32),
                   jax.ShapeDtypeStruct((T, wr.shape[1]), F32)],
        scratch_shapes=[pltpu.VMEM((tm, wqk.shape[1]), BF16)],
        compiler_params=_params(),
        name="mixout",
    )(x, dsa, ml, wo, wqk, wvo, wr, gx, gf)


def _moe_body(tok_ref, eb_ref, hf_hbm, wg_ref, wu_ref, wd_ref, y_hbm,
              xbuf, ybuf, wgb, wub, wdb, gsem, ysem, *, bm, n_blk):
    e = pl.program_id(0)
    n_valid = eb_ref[N_EXPERTS]
    b_lo = eb_ref[e]
    b_hi = eb_ref[e + 1]

    def tok_words(tok):
        return hf_hbm.at[pl.ds(pl.multiple_of(tok * _SUBLANES, _SUBLANES), _SUBLANES)]

    def gather(blk, s):
        def issue(g, carry):
            for u in range(_SUBLANES):
                r = g * _SUBLANES + u
                pltpu.make_async_copy(tok_words(tok_ref[blk * bm + r]),
                                      xbuf.at[s, pl.ds(pl.multiple_of(r * _SUBLANES, _SUBLANES), _SUBLANES)],
                                      gsem.at[s]).start(priority=1)
            return carry
        lax.fori_loop(0, bm // _SUBLANES, issue, 0)

    def wait_gather(s):
        pltpu.make_async_copy(xbuf.at[s], xbuf.at[s], gsem.at[s]).wait()

    def y_copy(blk, s):
        return pltpu.make_async_copy(ybuf.at[s], y_hbm.at[pl.ds(pl.multiple_of(blk * bm, bm), bm)], ysem.at[s])

    @pl.when(e == 0)
    def _():
        for k in range(_GATHER_AHEAD):
            gather(k, k)

    @pl.when(b_hi > b_lo)
    def _():
        wgb[...] = wg_ref[0].astype(BF16)
        wub[...] = wu_ref[0].astype(BF16)
        wdb[...] = wd_ref[0].astype(BF16)

        def block(b, carry):
            s = b % _GATHER_SLOTS
            wait_gather(s)
            chunks = []
            for c in range(_SUBLANES):
                w = xbuf[s, pl.ds(c, bm, stride=_SUBLANES), :]
                chunks.append(lax.bitcast_convert_type(w << 16, F32).astype(BF16))
                chunks.append(lax.bitcast_convert_type(w & jnp.uint32(0xFFFF0000), F32).astype(BF16))
            xb = jnp.concatenate(chunks, axis=-1)
            gate = _dot(xb, wgb[...])
            a = gate * jax.nn.sigmoid(gate) * _dot(xb, wub[...])
            y = _dot(a.astype(BF16), wdb[...])

            nxt = jnp.minimum(b + _GATHER_AHEAD, n_blk - 1)
            for r in range(bm):
                pltpu.make_async_copy(tok_words(tok_ref[nxt * bm + r]),
                                      xbuf.at[(b + _GATHER_AHEAD) % _GATHER_SLOTS, pl.ds(r * _SUBLANES, _SUBLANES)],
                                      gsem.at[(b + _GATHER_AHEAD) % _GATHER_SLOTS]).start(priority=1)

            @pl.when(b >= 2)
            def _():
                y_copy(b - 2, b % 2).wait()

            ybuf[b % 2] = y
            y_copy(b, b % 2).start()
            return carry

        lax.fori_loop(b_lo, b_hi, block, 0)

    @pl.when(e == pl.num_programs(0) - 1)
    def _():
        for k in range(_GATHER_AHEAD):
            wait_gather((n_valid + k) % _GATHER_SLOTS)

        @pl.when(n_valid >= 2)
        def _():
            y_copy(n_valid - 2, n_valid % 2).wait()
        y_copy(n_valid - 1, (n_valid - 1) % 2).wait()
        ybuf[0] = jnp.zeros(ybuf.shape[1:], F32)

        def zero_block(b, carry):
            cp = y_copy(b, 0)
            cp.start()
            cp.wait()
            return carry

        lax.fori_loop(n_valid, n_blk, zero_block, 0)


def _moe(tok, e_blk, hf, wg, wu, wd, n_blk, bm):
    D = wg.shape[1]
    wspec = lambda shape: pl.BlockSpec((1,) + shape, lambda e, *_: (e, 0, 0))
    grid_spec = pltpu.PrefetchScalarGridSpec(
        num_scalar_prefetch=2,
        grid=(N_EXPERTS,),
        in_specs=[pl.BlockSpec(memory_space=pl.ANY),
                  wspec((D, D_EXPERT)), wspec((D, D_EXPERT)), wspec((D_EXPERT, D))],
        out_specs=pl.BlockSpec(memory_space=pl.ANY),
        scratch_shapes=[pltpu.VMEM((_GATHER_SLOTS, bm * _SUBLANES, _LANES), U32),
                        pltpu.VMEM((2, bm, D), F32),
                        pltpu.VMEM((D, D_EXPERT), BF16), pltpu.VMEM((D, D_EXPERT), BF16),
                        pltpu.VMEM((D_EXPERT, D), BF16),
                        pltpu.SemaphoreType.DMA((_GATHER_SLOTS,)), pltpu.SemaphoreType.DMA((2,))],
    )
    return pl.pallas_call(
        functools.partial(_moe_body, bm=bm, n_blk=n_blk),
        grid_spec=grid_spec,
        out_shape=jax.ShapeDtypeStruct((n_blk * bm, D), F32),
        compiler_params=_params(),
        name="moe",
    )(tok, e_blk, hf, wg, wu, wd)


def _route(rl, b_group, b_router, bm):
    N = rl.shape[0]
    g_logits = rl[:, :N_GROUPS] + b_group
    g_prob = jax.nn.softmax(g_logits, axis=-1)
    g_sel = jnp.argmax(g_logits, axis=-1)
    p_g = jnp.take_along_axis(g_prob, g_sel[:, None], axis=-1)
    e_logits = rl[:, N_GROUPS:N_GROUPS + N_EXPERTS] + b_router
    in_group = (jnp.arange(N_EXPERTS, dtype=I32) // EXP_PER_GROUP)[None, :] == g_sel[:, None]
    e_prob = jax.nn.softmax(jnp.where(in_group, e_logits, -jnp.inf), axis=-1)
    e_cand = jnp.where(in_group, e_prob, -1.0)
    tops = []
    for _ in range(TOPK_IN_GROUP):
        best = jnp.argmax(e_cand, axis=-1).astype(I32)
        tops.append((jnp.max(e_cand, axis=-1), best))
        e_cand = jnp.where(jnp.arange(N_EXPERTS, dtype=I32)[None, :] == best[:, None], -2.0, e_cand)
    top_p = jnp.stack([p for p, _ in tops], axis=-1)
    expert_id = jnp.stack([e for _, e in tops], axis=-1)
    gates = p_g * top_p / top_p.sum(-1, keepdims=True)

    A = N * TOPK_IN_GROUP
    flat_e = expert_id.reshape(A)
    onehot = (flat_e[:, None] == jnp.arange(N_EXPERTS, dtype=I32)[None, :])
    seg = _RANK_SEG if A % _RANK_SEG == 0 else A
    oh = onehot.astype(BF16).reshape(A // seg, seg, N_EXPERTS)
    before = (jnp.arange(seg)[None, :] < jnp.arange(seg)[:, None]).astype(BF16)
    within = jnp.einsum('ij,bjk->bik', before, oh, preferred_element_type=F32)
    seg_tot = jnp.sum(oh.astype(F32), axis=1)
    seg_base = jnp.cumsum(seg_tot, axis=0) - seg_tot
    rank = jnp.sum((within + seg_base[:, None, :]) * oh.astype(F32), axis=-1).reshape(A).astype(I32)
    counts = jnp.sum(seg_tot, axis=0).astype(I32)
    padded = (counts + bm - 1) // bm * bm
    pad_ends = jnp.cumsum(padded)
    pad_starts = pad_ends - padded
    row = jnp.sum(jnp.where(onehot, pad_starts[None, :], 0), axis=1) + rank
    n_blk = -(-A // bm) + N_EXPERTS
    row_tok = jnp.zeros((n_blk * bm,), I32).at[row].set(jnp.arange(A, dtype=I32) // TOPK_IN_GROUP)
    e_blk = jnp.concatenate([pad_starts, pad_ends[-1:]]) // bm
    return row_tok, row, gates, e_blk.astype(I32), n_blk


def _final_body(row_ref, x_ref, gate_ref, y_hbm, g_ref, out_ref, ybuf, sem, *, tm):
    i = pl.program_id(0)
    n_tiles = pl.num_programs(0)
    slot = i % _FINAL_SLOTS

    def gather(tile, s):
        def issue(g, carry):
            for u in range(_SUBLANES):
                for k in range(TOPK_IN_GROUP):
                    src = row_ref[(tile * tm + g * _SUBLANES + u) * TOPK_IN_GROUP + k]
                    pltpu.make_async_copy(y_hbm.at[pl.ds(src, 1)], ybuf.at[s, k, g, pl.ds(u, 1)],
                                          sem.at[s]).start()
            return carry
        lax.fori_loop(0, tm // _SUBLANES, issue, 0)

    def wait_rows(s):
        pltpu.make_async_copy(ybuf.at[s], ybuf.at[s], sem.at[s]).wait()

    @pl.when(i == 0)
    def _():
        for t in range(_FINAL_SLOTS - 1):
            gather(jnp.minimum(t, n_tiles - 1), t)

    wait_rows(slot)
    acc = x_ref[...]
    for k in range(TOPK_IN_GROUP):
        acc = acc + gate_ref[:, k:k + 1] * ybuf[slot, k].reshape(tm, ybuf.shape[-1])
    y = _rms(acc, g_ref[...])

    ahead = _FINAL_SLOTS - 1
    nxt = jnp.minimum(i + ahead, n_tiles - 1)
    for r in range(tm):
        for k in range(TOPK_IN_GROUP):
            src = row_ref[(nxt * tm + r) * TOPK_IN_GROUP + k]
            pltpu.make_async_copy(y_hbm.at[pl.ds(src, 1)],
                                  ybuf.at[(i + ahead) % _FINAL_SLOTS, k, r // _SUBLANES, pl.ds(r % _SUBLANES, 1)],
                                  sem.at[(i + ahead) % _FINAL_SLOTS]).start()
    out_ref[...] = y

    @pl.when(i == n_tiles - 1)
    def _():
        for t in range(1, _FINAL_SLOTS):
            wait_rows((i + t) % _FINAL_SLOTS)


def _final(row, x2, gates, y_rows, g, tm):
    T, D = x2.shape
    grid_spec = pltpu.PrefetchScalarGridSpec(
        num_scalar_prefetch=1,
        grid=(T // tm,),
        in_specs=[pl.BlockSpec((tm, D), lambda i, *_: (i, 0)),
                  pl.BlockSpec((tm, TOPK_IN_GROUP), lambda i, *_: (i, 0)),
                  pl.BlockSpec(memory_space=pl.ANY),
                  pl.BlockSpec(g.shape, lambda i, *_: (0, 0))],
        out_specs=pl.BlockSpec((tm, D), lambda i, *_: (i, 0)),
        scratch_shapes=[pltpu.VMEM((_FINAL_SLOTS, TOPK_IN_GROUP, tm // _SUBLANES, _SUBLANES, D), F32),
                        pltpu.SemaphoreType.DMA((_FINAL_SLOTS,))],
    )
    return pl.pallas_call(
        functools.partial(_final_body, tm=tm),
        grid_spec=grid_spec,
        out_shape=jax.ShapeDtypeStruct((T, D), F32),
        compiler_params=_params(),
        name="final",
    )(row, x2, gates, y_rows, g)


def _tile_sizes(T):
    pick = lambda want: want if T % want == 0 else CHUNK
    return dict(inproj=pick(256), dsa_q=pick(256), dsa_k=pick(512), mlstm=pick(256), mixout=pick(256),
                final=pick(256), moe=128)


def _layer(x, mem, norm_mix_g, w_in, kv_norm_g, k_idx_norm_g, w_uk, w_uv, conv_w, conv_b, gate_b, ml_norm_g,
           w_out, norm_x_g, mem_norm_g, w_cq, w_ckv, w_co, norm_ffn_g, w_group, b_group, w_router, b_router,
           w_gate, w_up, w_down, out_g):
    T = x.shape[0]
    ts = _tile_sizes(T)
    r2 = lambda v: v.reshape(1, -1)

    w_r = _wprep(w_in, 256)
    wuk_t = jnp.transpose(w_uk, (1, 2, 0)).astype(BF16)
    wuv_t = jnp.transpose(w_uv, (1, 0, 2)).astype(BF16)

    qabs, ckv, qi, kidx, small, mqk, mv, mo = _inproj(
        x, r2(norm_mix_g), w_r, wuk_t, r2(kv_norm_g), r2(k_idx_norm_g), ts["inproj"])

    gate_rows = jnp.transpose(small[:, _S_WI:_S_MF + ML_HEADS])
    dsa_out = _dsa(qi, gate_rows, qabs, kidx, ckv, wuv_t, ts["dsa_q"], ts["dsa_k"])

    gb_col = jnp.zeros((1, _SMALL), F32).at[0, _S_MI:_S_MI + 2 * ML_HEADS].set(gate_b)
    ml_out = _mlstm(mqk, mv, small, gate_rows, mo, conv_w, r2(conv_b), gb_col, gate_b.reshape(-1, 1),
                    r2(ml_norm_g), ts["mlstm"])

    wqk, wvo = _memfold(mem, r2(mem_norm_g), w_ckv, w_cq, w_co)
    w_rt = jnp.concatenate([w_group, w_router,
                            jnp.zeros((D_MODEL, 128 - N_GROUPS - N_EXPERTS), w_group.dtype)], axis=1)
    x2, hf, rl = _mixout(x, dsa_out, ml_out, w_out.astype(BF16), wqk, wvo, w_rt.astype(BF16),
                         r2(norm_x_g), r2(norm_ffn_g), ts["mixout"])

    bm = ts["moe"]
    row_tok, row, gates, e_blk, n_blk = _route(rl, b_group, b_router, bm)
    y_rows = _moe(row_tok, e_blk, hf, w_gate, w_up, w_down, n_blk, bm)
    return _final(row, x2, gates, y_rows, r2(out_g), ts["final"])


def kernel(x, mem, norm_mix_g, w_in, kv_norm_g, k_idx_norm_g, w_uk, w_uv, conv_w, conv_b, gate_b, ml_norm_g,
           w_out, norm_x_g, mem_norm_g, w_cq, w_ckv, w_co, norm_ffn_g, w_group, b_group, w_router, b_router,
           w_gate, w_up, w_down, final_norm_g):
    B, T, D = x.shape
    assert B == 1 and D == D_MODEL and norm_mix_g.shape[0] == 1 and T % CHUNK == 0
    out = _layer(x[0], mem[0], norm_mix_g[0], w_in, kv_norm_g[0], k_idx_norm_g[0], w_uk[0], w_uv[0],
                 conv_w[0], conv_b[0], gate_b[0], ml_norm_g[0], w_out[0], norm_x_g[0], mem_norm_g[0],
                 w_cq[0], w_ckv[0], w_co[0], norm_ffn_g[0], w_group[0], b_group[0], w_router[0], b_router[0],
                 w_gate[0], w_up[0], w_down[0], final_norm_g)
    return out[None]
```

```python
import functools

import jax
import jax.numpy as jnp
import numpy as np
from jax import lax
from jax.experimental import pallas as pl
from jax.experimental.pallas import tpu as pltpu

F32 = jnp.float32
BF16 = jnp.bfloat16
I32 = jnp.int32
I16 = jnp.int16
U32 = jnp.uint32

EPS = 1e-6
CHUNK = 64
D_MODEL = 2048

DSA_HEADS = 8
DSA_HEAD_DIM = 128
DSA_LATENT = 256
IDX_HEADS = 8
IDX_DIM = 64
TOPK_MAX = 256

ML_HEADS = 4
ML_QK_DIM = 128
ML_V_DIM = 256
CONV_W = 4

X_HEADS = 4
X_HEAD_DIM = D_MODEL // X_HEADS

N_GROUPS = 4
EXP_PER_GROUP = 8
N_EXPERTS = N_GROUPS * EXP_PER_GROUP
TOPK_IN_GROUP = 2
D_EXPERT = 512

_O_DQ = 0
_O_CKV = _O_DQ + DSA_HEADS * DSA_HEAD_DIM
_O_QI = _O_CKV + DSA_LATENT
_O_KI = _O_QI + IDX_HEADS * IDX_DIM
_O_WI = _O_KI + IDX_DIM
_O_MQ = _O_WI + IDX_HEADS
_O_MK = _O_MQ + ML_HEADS * ML_QK_DIM
_O_MV = _O_MK + ML_HEADS * ML_QK_DIM
_O_MI = _O_MV + ML_HEADS * ML_V_DIM
_O_MF = _O_MI + ML_HEADS
_O_MO = _O_MF + ML_HEADS
_O_END = _O_MO + ML_HEADS * ML_V_DIM

_G_DQ = (0, 1024)
_G_CKV = (1024, 1280)
_G_QI = (1280, 1792)
_G_SMALL = (1792, 1920)
_G_MQK = (1920, 2944)
_G_MV = (2944, 3968)
_G_MO = (3968, 4992)
_W_COLS = 4992
_S_WI = IDX_DIM
_S_MI = _S_WI + IDX_HEADS
_S_MF = _S_MI + ML_HEADS
_SMALL = 128

_VMEM_LIMIT = 56 * 1024 * 1024
_INT_MIN = -(2 ** 31)
_I16_MIN = -(2 ** 15)
_CHUNK_SHIFT = CHUNK.bit_length() - 1
_LOG2E = 1.4426950408889634
_SUBLANES = 8
_LANES = 128
_GATHER_SLOTS = 9
_GATHER_AHEAD = _GATHER_SLOTS - 1
_FINAL_SLOTS = 3
_NEG = -1e30


def _rms(v, g):
    return v * lax.rsqrt(jnp.mean(v * v, axis=-1, keepdims=True) + EPS) * g


def _dot(a, b):
    return jnp.dot(a, b, preferred_element_type=F32)


def _dot_nt(a, b):
    return lax.dot_general(a, b, (((1,), (1,)), ((), ())), preferred_element_type=F32)


def _resident(shape):
    nd = len(shape)
    return pl.BlockSpec(shape, lambda *_: (0,) * nd, pipeline_mode=pl.Buffered(1))


def _params(n_axes=1):
    return pltpu.CompilerParams(dimension_semantics=("arbitrary",) * n_axes,
                                vmem_limit_bytes=_VMEM_LIMIT)


def _wprep_body(w_hbm, out_ref, wbuf, sem, *, tk):
    i = pl.program_id(0)
    slot = i % 2

    def rows(blk, s):
        return pltpu.make_async_copy(w_hbm.at[0, pl.ds(pl.multiple_of(blk * tk, tk), tk)], wbuf.at[s], sem.at[s])

    @pl.when(i == 0)
    def _():
        rows(0, 0).start()

    @pl.when(i + 1 < pl.num_programs(0))
    def _():
        rows(i + 1, 1 - slot).start()

    rows(i, slot).wait()
    w_ref = wbuf.at[slot]
    n_small = IDX_DIM + IDX_HEADS
    out_ref[:, _G_DQ[0]:_G_SMALL[0] + n_small] = w_ref[:, _O_DQ:_O_MQ].astype(BF16)
    out_ref[:, _G_SMALL[0] + n_small:_G_SMALL[0] + n_small + 2 * ML_HEADS] = w_ref[:, _O_MI:_O_MO].astype(BF16)
    out_ref[:, _G_SMALL[0] + n_small + 2 * ML_HEADS:_G_SMALL[1]] = jnp.zeros(
        (out_ref.shape[0], _SMALL - n_small - 2 * ML_HEADS), BF16)
    out_ref[:, _G_MQK[0]:_G_MV[1]] = w_ref[:, _O_MQ:_O_MI].astype(BF16)
    out_ref[:, _G_MO[0]:_G_MO[1]] = w_ref[:, _O_MO:_O_END].astype(BF16)


def _wprep(w_in, tk):
    K = w_in.shape[1]
    return pl.pallas_call(
        functools.partial(_wprep_body, tk=tk),
        grid=(K // tk,),
        in_specs=[pl.BlockSpec(memory_space=pl.ANY)],
        out_specs=pl.BlockSpec((tk, _W_COLS), lambda i: (i, 0)),
        out_shape=jax.ShapeDtypeStruct((K, _W_COLS), BF16),
        scratch_shapes=[pltpu.VMEM((2, tk, w_in.shape[2]), F32), pltpu.SemaphoreType.DMA((2,))],
        compiler_params=_params(),
        name="wprep",
    )(w_in)


def _inproj_body(x_ref, g_ref, w_ref, wuk_ref, kvg_ref, kig_ref,
                 qabs_ref, ckv_ref, qi_ref, kidx_ref, small_ref, mqk_ref, mv_ref, mo_ref):
    h = _rms(x_ref[...], g_ref[...]).astype(BF16)

    def proj(grp):
        return _dot(h, w_ref[:, grp[0]:grp[1]])

    dq = proj(_G_DQ)
    for hd in range(DSA_HEADS):
        qh = dq[:, hd * DSA_HEAD_DIM:(hd + 1) * DSA_HEAD_DIM].astype(BF16)
        qa = _dot(qh, wuk_ref[hd]) * (DSA_HEAD_DIM ** -0.5 * _LOG2E)
        qabs_ref[:, hd * DSA_LATENT:(hd + 1) * DSA_LATENT] = qa.astype(BF16)
    ckv_ref[...] = _rms(proj(_G_CKV), kvg_ref[...]).astype(BF16)
    qi_ref[...] = (proj(_G_QI) * (IDX_DIM ** -0.5)).astype(BF16)
    small = proj(_G_SMALL)
    small_ref[...] = small
    kidx_ref[...] = _rms(small[:, :IDX_DIM], kig_ref[...]).astype(BF16)
    mqk_ref[...] = proj(_G_MQK)
    mv_ref[...] = proj(_G_MV).astype(BF16)
    mo_ref[...] = proj(_G_MO)


def _inproj(x, g, w, wuk, kvg, kig, tm):
    T = x.shape[0]
    row = lambda n: pl.BlockSpec((tm, n), lambda i: (i, 0))
    outs = [(8 * DSA_LATENT, BF16), (DSA_LATENT, BF16), (IDX_HEADS * IDX_DIM, BF16), (IDX_DIM, BF16),
            (_SMALL, F32), (2 * ML_HEADS * ML_QK_DIM, F32), (ML_HEADS * ML_V_DIM, BF16),
            (ML_HEADS * ML_V_DIM, F32)]
    return pl.pallas_call(
        _inproj_body,
        grid=(T // tm,),
        in_specs=[row(D_MODEL), _resident(g.shape), _resident(w.shape), _resident(wuk.shape),
                  _resident(kvg.shape), _resident(kig.shape)],
        out_specs=[row(n) for n, _ in outs],
        out_shape=[jax.ShapeDtypeStruct((T, n), dt) for n, dt in outs],
        compiler_params=_params(),
        name="inproj",
    )(x, g, w, wuk, kvg, kig)


def _sublane_fold(v, op, rows=_SUBLANES, ways=4):
    groups = [v[r * rows:(r + 1) * rows, :] for r in range(v.shape[0] // rows)]
    accs = groups[:ways]
    for r in range(ways, len(groups)):
        accs[r % ways] = op(accs[r % ways], groups[r])
    while len(accs) > 1:
        accs = [op(accs[k], accs[k + 1]) if k + 1 < len(accs) else accs[k] for k in range(0, len(accs), 2)]
    return accs[0]


def _bit_transpose32(words):
    a = list(words)
    j, m = 16, 0x0000FFFF
    while j:
        k = 0
        while k < 32:
            t = (a[k] ^ (a[k + j] >> j)) & jnp.uint32(m)
            a[k] = a[k] ^ t
            a[k + j] = a[k + j] ^ (t << j)
            k = (k + j + 1) & ~j
        j >>= 1
        m = (m ^ (m << j)) & 0xFFFFFFFF
    return a


def _dsa_body(qi_ref, wrow_ref, qabs_ref, kidx_ref, ckv_ref, ckvt_ref, wuv_ref, out_ref,
              key_ref, planes_ref, alive_ref, lga_ref, lgb_ref, m_ref, l_ref, acc_ref, *, tq, tk, topk, nbits_idx):
    i = pl.program_id(0)
    n_kb = ((i + 1) * tq + tk - 1) // tk
    w_rows = wrow_ref[0:IDX_HEADS, :] * (IDX_HEADS ** -0.5)
    q_chunk = (i * tq + lax.broadcasted_iota(I32, (1, tq), 1)) >> _CHUNK_SHIFT

    def key_pos(j):
        return j * tk + lax.broadcasted_iota(I32, (tk, 1), 0)

    def score_block(j, carry):
        kx = kidx_ref[pl.ds(pl.multiple_of(j * tk, tk), tk), :]
        s = jnp.zeros((tk, tq), F32)
        for hd in range(IDX_HEADS):
            d = _dot_nt(kx, qi_ref[:, hd * IDX_DIM:(hd + 1) * IDX_DIM])
            s = s + w_rows[hd:hd + 1, :] * jnp.maximum(d, 0.0)
        bits = lax.bitcast_convert_type(s, I32)
        key = bits ^ ((bits >> 31) & 0x7FFFFFFF)
        key_ref[j] = jnp.where((key_pos(j) >> _CHUNK_SHIFT) <= q_chunk, key, _INT_MIN)
        return carry

    lax.fori_loop(0, n_kb, score_block, 0)

    def count(pred):
        def body(j, acc):
            hit = pred(key_ref[j], key_pos(j)).astype(I32)
            return acc + _sublane_fold(hit, jnp.add)
        acc = lax.fori_loop(0, n_kb, body, jnp.zeros((8, tq), I32))
        return jnp.sum(acc, axis=0, keepdims=True)

    plane_rows = tk // 32

    @pl.when(i == 0)
    def _():
        planes_ref[...] = jnp.zeros(planes_ref.shape, U32)

    def slice_block(j, carry):
        u = lax.bitcast_convert_type(key_ref[j], U32) ^ jnp.uint32(0x80000000)
        for h in range(plane_rows // _SUBLANES):
            base = h * 32 * _SUBLANES
            planes = _bit_transpose32([u[base + _SUBLANES * v:base + _SUBLANES * (v + 1), :] for v in range(32)])
            row0 = pl.multiple_of(j * plane_rows + h * _SUBLANES, _SUBLANES)
            for b in range(32):
                planes_ref[b, pl.ds(row0, _SUBLANES), :] = planes[31 - b]
        return carry

    lax.fori_loop(0, n_kb, slice_block, 0)
    word_row = lax.broadcasted_iota(I32, (alive_ref.shape[0], 1), 0)
    alive_ref[...] = jnp.where(word_row < n_kb * plane_rows, jnp.full(alive_ref.shape, 0xFFFFFFFF, U32),
                               jnp.uint32(0))

    def radix_select(n_rows):
        def ones_count(words):
            return jnp.sum(_sublane_fold(lax.population_count(words).astype(I32), jnp.add), axis=0, keepdims=True)

        def select_bit(it, carry):
            need, t_u = carry
            b = 31 - it
            alive = alive_ref[0:n_rows, :]
            ones = alive & planes_ref[b, pl.ds(0, n_rows), :]
            n_ones = ones_count(ones)
            take = n_ones >= need
            alive_ref[0:n_rows, :] = jnp.where(take, ones, alive ^ ones)
            bit = lax.shift_left(jnp.uint32(1), jnp.asarray(b, U32))
            return jnp.where(take, need, need - n_ones), jnp.where(take, t_u | bit, t_u)

        need, t_u = lax.fori_loop(0, 32, select_bit, (jnp.full((1, tq), topk, I32), jnp.zeros((1, tq), U32)))
        return need, t_u, ones_count(alive_ref[0:n_rows, :])

    all_rows = alive_ref.shape[0]
    if all_rows % (2 * _SUBLANES) == 0:
        need, t_u, n_equal = lax.cond(n_kb * plane_rows <= all_rows // 2,
                                      lambda: radix_select(all_rows // 2), lambda: radix_select(all_rows))
    else:
        need, t_u, n_equal = radix_select(all_rows)
    short = t_u == 0
    t = jnp.maximum(lax.bitcast_convert_type(t_u ^ jnp.uint32(0x80000000), I32), _INT_MIN + 1)
    all_pos = jnp.int32(2 ** nbits_idx - 1)
    has_ties = (n_equal > need) & jnp.logical_not(short)
    n_tie_take = jnp.where(has_ties, need, all_pos)

    def tie_cutoff():
        def pos_bit(b, c):
            cand = c + lax.shift_left(jnp.int32(1), nbits_idx - 1 - b)
            f = count(lambda kb, pos: (kb == t) & (pos < cand))
            return jnp.where(f <= n_tie_take, cand, c)
        return lax.fori_loop(0, nbits_idx, pos_bit, jnp.zeros((1, tq), I32))

    cut = lax.cond(jnp.max(has_ties.astype(I32)) > 0, tie_cutoff, lambda: jnp.full((1, tq), all_pos, I32))

    m_ref[...] = jnp.full(m_ref.shape, _NEG, F32)
    l_ref[...] = jnp.zeros(l_ref.shape, F32)
    acc_ref[...] = jnp.zeros(acc_ref.shape, F32)
    last_blk = ckvt_ref.shape[0] - 1

    def logits(j, lg_buf):
        c_blk = ckv_ref[pl.ds(pl.multiple_of(jnp.minimum(j, last_blk) * tk, tk), tk), :]
        kb = key_ref[j]
        sel = (kb > t) | ((kb == t) & (key_pos(j) < cut))
        bias = jnp.where(sel, 0.0, _NEG).astype(F32)
        for hd in range(DSA_HEADS):
            lg_buf[hd] = _dot_nt(c_blk, qabs_ref[:, hd * DSA_LATENT:(hd + 1) * DSA_LATENT]) + bias

    def accumulate(j, lg_buf):
        c_blk_t = ckvt_ref[jnp.minimum(j, last_blk)]
        for hd in range(DSA_HEADS):
            lg = lg_buf[hd]
            m_old = m_ref[hd:hd + 1, :]
            m_new = jnp.maximum(m_old, jnp.max(_sublane_fold(lg, jnp.maximum, ways=1), axis=0, keepdims=True))
            p = jnp.exp2(lg - m_new)
            alpha = jnp.exp2(m_old - m_new)
            l_ref[hd:hd + 1, :] = alpha * l_ref[hd:hd + 1, :] + jnp.sum(_sublane_fold(p, jnp.add, ways=1), axis=0,
                                                                         keepdims=True)
            acc_ref[hd] = alpha * acc_ref[hd] + _dot(c_blk_t, p.astype(BF16))
            m_ref[hd:hd + 1, :] = m_new

    logits(0, lga_ref)

    def attn_pair(mi, carry):
        ja = 2 * mi
        accumulate(ja, lga_ref)
        logits(ja + 1, lgb_ref)
        accumulate(ja + 1, lgb_ref)
        logits(jnp.minimum(ja + 2, n_kb - 1), lga_ref)
        return carry

    lax.fori_loop(0, n_kb // 2, attn_pair, 0)

    @pl.when(n_kb % 2 == 1)
    def _():
        accumulate(n_kb - 1, lga_ref)

    for hd in range(DSA_HEADS):
        o_lat = (acc_ref[hd] / l_ref[hd:hd + 1, :]).T.astype(BF16)
        out_ref[:, hd * DSA_HEAD_DIM:(hd + 1) * DSA_HEAD_DIM] = _dot(o_lat, wuv_ref[hd]).astype(BF16)


def _dsa(qi, wrows, qabs, kidx, ckv, wuv, tq, tk):
    T = qi.shape[0]
    topk = min(TOPK_MAX, T // 4)
    n_kb = T // tk
    ckvt = jnp.transpose(ckv.reshape(n_kb, tk, DSA_LATENT), (0, 2, 1))
    row = lambda n: pl.BlockSpec((tq, n), lambda i: (i, 0))
    body = functools.partial(_dsa_body, tq=tq, tk=tk, topk=topk, nbits_idx=int(T).bit_length())
    return pl.pallas_call(
        body,
        grid=(T // tq,),
        in_specs=[row(qi.shape[1]), pl.BlockSpec((wrows.shape[0], tq), lambda i: (0, i)), row(qabs.shape[1]),
                  _resident(kidx.shape), _resident(ckv.shape), _resident(ckvt.shape), _resident(wuv.shape)],
        out_specs=row(DSA_HEADS * DSA_HEAD_DIM),
        out_shape=jax.ShapeDtypeStruct((T, DSA_HEADS * DSA_HEAD_DIM), BF16),
        scratch_shapes=[pltpu.VMEM((n_kb + n_kb % 2, tk, tq), I32),
                        pltpu.VMEM((32, T // 32, tq), U32), pltpu.VMEM((T // 32, tq), U32),
                        pltpu.VMEM((DSA_HEADS, tk, tq), F32), pltpu.VMEM((DSA_HEADS, tk, tq), F32),
                        pltpu.VMEM((DSA_HEADS, tq), F32),
                        pltpu.VMEM((DSA_HEADS, tq), F32), pltpu.VMEM((DSA_HEADS, DSA_LATENT, tq), F32)],
        compiler_params=_params(),
        name="dsa",
    )(qi, wrows, qabs, kidx, ckv, ckvt, wuv)


def _log_sigmoid(v):
    return jnp.minimum(v, 0.0) - jnp.log1p(jnp.exp(-jnp.abs(v)))


def _chunk_cumsum(v, axis):
    pos = lax.broadcasted_iota(I32, v.shape, axis) & (CHUNK - 1)
    d = 1
    while d < CHUNK:
        v = v + jnp.where(pos >= d, pltpu.roll(v, d, axis=axis), 0.0)
        d *= 2
    return v


def _mlstm_body(mqk_ref, mv_ref, small_ref, gt_ref, mo_ref, cw_ref, cb_ref, gbc_ref, gbr_ref, ng_ref,
                out_ref, xe_ref, c_ref, n_ref, m_ref, hs_ref, *, rows):
    @pl.when(pl.program_id(0) == 0)
    def _():
        xe_ref[0:8, :] = jnp.zeros((8, xe_ref.shape[1]), F32)
        c_ref[...] = jnp.zeros(c_ref.shape, F32)
        n_ref[...] = jnp.zeros(n_ref.shape, F32)
        m_ref[...] = jnp.zeros(m_ref.shape, F32)

    x = mqk_ref[...]
    xe_ref[8:8 + rows, :] = x
    y = cb_ref[...]
    for j in range(CONV_W - 1):
        y = y + xe_ref[5 + j:5 + j + rows, :] * cw_ref[j:j + 1, :]
    y = y + x * cw_ref[CONV_W - 1:CONV_W, :]
    xe_ref[0:8, :] = x[rows - 8:rows, :]
    qk = y * jax.nn.sigmoid(y)
    nqk = ML_HEADS * ML_QK_DIM
    q_all = (qk[:, :nqk] * (ML_QK_DIM ** -0.5)).astype(BF16)
    k_all = qk[:, nqk:]

    g_col = small_ref[...] + gbc_ref[...]
    g_row = gt_ref[...] + gbr_ref[...]
    b_col = _chunk_cumsum(_log_sigmoid(g_col), 0)
    b_row = _chunk_cumsum(_log_sigmoid(g_row), 1)

    tri = lax.broadcasted_iota(I32, (CHUNK, CHUNK), 1) <= lax.broadcasted_iota(I32, (CHUNK, CHUNK), 0)

    heads = range(ML_HEADS)
    for c in range(rows // CHUNK):
        lo, hi = c * CHUNK, (c + 1) * CHUNK
        bc = [b_col[lo:hi, _S_MF + hd:_S_MF + hd + 1] for hd in heads]
        lic = [g_col[lo:hi, _S_MI + hd:_S_MI + hd + 1] for hd in heads]
        br = [b_row[ML_HEADS + hd:ML_HEADS + hd + 1, lo:hi] for hd in heads]
        lir = [g_row[hd:hd + 1, lo:hi] for hd in heads]
        g_tot = [bc[hd][CHUNK - 1:CHUNK, :] for hd in heads]
        m_prev = [m_ref[hd][:, 0:1] for hd in heads]
        qh = [q_all[lo:hi, hd * ML_QK_DIM:(hd + 1) * ML_QK_DIM] for hd in heads]
        kh = [k_all[lo:hi, hd * ML_QK_DIM:(hd + 1) * ML_QK_DIM] for hd in heads]
        vh = [mv_ref[lo:hi, hd * ML_V_DIM:(hd + 1) * ML_V_DIM] for hd in heads]
        c_prev = [c_ref[hd] for hd in heads]
        n_prev = [n_ref[hd] for hd in heads]

        dmat = [jnp.where(tri, bc[hd] - br[hd] + lir[hd], -jnp.inf) for hd in heads]
        inter = [bc[hd] + m_prev[hd] for hd in heads]
        m_t = [jnp.maximum(inter[hd], jnp.max(dmat[hd], axis=-1, keepdims=True)) for hd in heads]
        w_intra = [jnp.exp(dmat[hd] - m_t[hd]) for hd in heads]
        a_inter = [jnp.exp(inter[hd] - m_t[hd]) for hd in heads]
        m_new = [jnp.maximum(g_tot[hd] + m_prev[hd],
                             jnp.max(g_tot[hd] - br[hd] + lir[hd], axis=-1, keepdims=True)) for hd in heads]
        a_state = [jnp.exp(g_tot[hd] + m_prev[hd] - m_new[hd]) for hd in heads]
        wk = [jnp.exp(g_tot[hd] - bc[hd] + lic[hd] - m_new[hd]) * kh[hd] for hd in heads]

        s_qk = [_dot_nt(qh[hd], kh[hd].astype(BF16)) * w_intra[hd] for hd in heads]
        read = [_dot(qh[hd], c_prev[hd].astype(BF16)) for hd in heads]
        d_c = [_dot(wk[hd].T.astype(BF16), vh[hd]) for hd in heads]
        num = [a_inter[hd] * read[hd] + _dot(s_qk[hd].astype(BF16), vh[hd]) for hd in heads]
        den = [a_inter[hd] * jnp.sum(qh[hd].astype(F32) * n_prev[hd], axis=-1, keepdims=True)
               + jnp.sum(s_qk[hd], axis=-1, keepdims=True) for hd in heads]
        for hd in heads:
            hs_ref[lo:hi, hd * ML_V_DIM:(hd + 1) * ML_V_DIM] = (
                num[hd] / jnp.maximum(jnp.abs(den[hd]), jnp.exp(-m_t[hd])))
        for hd in heads:
            c_ref[hd] = a_state[hd] * c_prev[hd] + d_c[hd]
            n_ref[hd] = a_state[hd] * n_prev[hd] + jnp.sum(wk[hd], axis=0, keepdims=True)
            m_ref[hd] = jnp.broadcast_to(m_new[hd], m_ref.shape[1:])

    for hd in range(ML_HEADS):
        sl = slice(hd * ML_V_DIM, (hd + 1) * ML_V_DIM)
        out_ref[:, sl] = (_rms(hs_ref[:, sl], ng_ref[:, sl]) * jax.nn.sigmoid(mo_ref[:, sl])).astype(BF16)


def _mlstm(mqk, mv, small, gt, mo, cw, cb, gbc, gbr, ng, rows):
    T = mqk.shape[0]
    row = lambda n: pl.BlockSpec((rows, n), lambda i: (i, 0))
    nv = ML_HEADS * ML_V_DIM
    return pl.pallas_call(
        functools.partial(_mlstm_body, rows=rows),
        grid=(T // rows,),
        in_specs=[row(mqk.shape[1]), row(nv), row(_SMALL), pl.BlockSpec((8, rows), lambda i: (1, i)), row(nv),
                  _resident(cw.shape), _resident(cb.shape), _resident(gbc.shape), _resident(gbr.shape),
                  _resident(ng.shape)],
        out_specs=row(nv),
        out_shape=jax.ShapeDtypeStruct((T, nv), BF16),
        scratch_shapes=[pltpu.VMEM((rows + 8, mqk.shape[1]), F32),
                        pltpu.VMEM((ML_HEADS, ML_QK_DIM, ML_V_DIM), F32),
                        pltpu.VMEM((ML_HEADS, 1, ML_QK_DIM), F32),
                        pltpu.VMEM((ML_HEADS, 1, 128), F32),
                        pltpu.VMEM((rows, nv), F32)],
        compiler_params=_params(),
        name="mlstm",
    )(mqk, mv, small, gt, mo, cw, cb, gbc, gbr, ng)


def _memfold_body(mem_ref, g_ref, wk_ref, wv_ref, wq_ref, wc_ref, wqk_ref, wvo_ref):
    mn = _rms(mem_ref[...], g_ref[...]).astype(BF16)
    k = _dot(mn, wk_ref[...].astype(BF16)).astype(BF16)
    v = _dot(mn, wv_ref[...].astype(BF16)).astype(BF16)
    wqk_ref[...] = (_dot_nt(wq_ref[...].astype(BF16), k) * (X_HEAD_DIM ** -0.5)).astype(BF16)
    wvo_ref[...] = _dot(v, wc_ref[...].astype(BF16)).astype(BF16)


def _memfold(mem, g, w_ckv, w_cq, w_co):
    M, D = mem.shape
    dh = X_HEAD_DIM
    return pl.pallas_call(
        _memfold_body,
        grid=(X_HEADS,),
        in_specs=[_resident(mem.shape), _resident(g.shape),
                  pl.BlockSpec((D, dh), lambda h: (0, h)), pl.BlockSpec((D, dh), lambda h: (0, X_HEADS + h)),
                  pl.BlockSpec((D, dh), lambda h: (0, h)), pl.BlockSpec((dh, D), lambda h: (h, 0))],
        out_specs=[pl.BlockSpec((D, M), lambda h: (0, h)), pl.BlockSpec((M, D), lambda h: (h, 0))],
        out_shape=[jax.ShapeDtypeStruct((D, X_HEADS * M), BF16), jax.ShapeDtypeStruct((X_HEADS * M, D), BF16)],
        compiler_params=_params(),
        name="memfold",
    )(mem, g, w_ckv, w_ckv, w_cq, w_co)


def _mixout_body(x_ref, dsa_ref, ml_ref, wo_ref, wqk_ref, wvo_ref, wr_ref, gx_ref, gf_ref,
                 x2_ref, hf_ref, rl_ref, p_ref):
    nd = dsa_ref.shape[1]
    x1 = x_ref[...] + _dot(dsa_ref[...], wo_ref[0:nd, :]) + _dot(ml_ref[...], wo_ref[nd:, :])
    lg_all = _dot(_rms(x1, gx_ref[...]).astype(BF16), wqk_ref[...])
    n_mem = wqk_ref.shape[1] // X_HEADS
    for hd in range(X_HEADS):
        sl = slice(hd * n_mem, (hd + 1) * n_mem)
        lg = lg_all[:, sl]
        e = jnp.exp(lg - jnp.max(lg, axis=-1, keepdims=True))
        p_ref[:, sl] = (e / jnp.sum(e, axis=-1, keepdims=True)).astype(BF16)
    x2 = x1 + _dot(p_ref[...], wvo_ref[...])
    x2_ref[...] = x2
    hf = _rms(x2, gf_ref[...]).astype(BF16)
    rl_ref[...] = _dot(hf, wr_ref[...])
    bits = lax.bitcast_convert_type(hf.astype(F32), U32)
    for c in range(_SUBLANES):
        lo = bits[:, (2 * c) * _LANES:(2 * c + 1) * _LANES]
        hi = bits[:, (2 * c + 1) * _LANES:(2 * c + 2) * _LANES]
        hf_ref[pl.ds(c, x2.shape[0], stride=_SUBLANES), :] = (hi & jnp.uint32(0xFFFF0000)) | (lo >> 16)


def _mixout(x, dsa, ml, wo, wqk, wvo, wr, gx, gf, tm):
    T = x.shape[0]
    row = lambda n: pl.BlockSpec((tm, n), lambda i: (i, 0))
    return pl.pallas_call(
        _mixout_body,
        grid=(T // tm,),
        in_specs=[row(D_MODEL), row(dsa.shape[1]), row(ml.shape[1]), _resident(wo.shape),
                  _resident(wqk.shape), _resident(wvo.shape), _resident(wr.shape), _resident(gx.shape),
                  _resident(gf.shape)],
        out_specs=[row(D_MODEL), pl.BlockSpec((tm * _SUBLANES, _LANES), lambda i: (i, 0)), row(wr.shape[1])],
        out_shape=[jax.ShapeDtypeStruct((T, D_MODEL), F32),
                   jax.ShapeDtypeStruct((T * _SUBLANES, _LANES), U32),
                   jax.ShapeDtypeStruct((T, wr.shape[1]), F32)],
        scratch_shapes=[pltpu.VMEM((tm, wqk.shape[1]), BF16)],
        compiler_params=_params(),
        name="mixout",
    )(x, dsa, ml, wo, wqk, wvo, wr, gx, gf)


def _moe_body(tok_ref, eb_ref, hf_hbm, wg_ref, wu_ref, wd_ref, y_hbm,
              xbuf, ybuf, wgb, wub, wdb, gsem, ysem, *, bm, n_blk):
    e = pl.program_id(0)
    n_valid = eb_ref[N_EXPERTS]
    b_lo = eb_ref[e]
    b_hi = eb_ref[e + 1]

    def tok_words(tok):
        return hf_hbm.at[pl.ds(pl.multiple_of(tok * _SUBLANES, _SUBLANES), _SUBLANES)]

    def gather(blk, s):
        def issue(g, carry):
            for u in range(_SUBLANES):
                r = g * _SUBLANES + u
                pltpu.make_async_copy(tok_words(tok_ref[blk * bm + r]),
                                      xbuf.at[s, pl.ds(pl.multiple_of(r * _SUBLANES, _SUBLANES), _SUBLANES)],
                                      gsem.at[s]).start(priority=1)
            return carry
        lax.fori_loop(0, bm // _SUBLANES, issue, 0)

    def wait_gather(s):
        pltpu.make_async_copy(xbuf.at[s], xbuf.at[s], gsem.at[s]).wait()

    def y_copy(blk, s):
        return pltpu.make_async_copy(ybuf.at[s], y_hbm.at[pl.ds(pl.multiple_of(blk * bm, bm), bm)], ysem.at[s])

    @pl.when(e == 0)
    def _():
        for k in range(_GATHER_AHEAD):
            gather(k, k)

    @pl.when(b_hi > b_lo)
    def _():
        wgb[...] = wg_ref[0].astype(BF16)
        wub[...] = wu_ref[0].astype(BF16)
        wdb[...] = wd_ref[0].astype(BF16)

        def block(b, carry):
            s = b % _GATHER_SLOTS
            wait_gather(s)
            chunks = []
            for c in range(_SUBLANES):
                w = xbuf[s, pl.ds(c, bm, stride=_SUBLANES), :]
                chunks.append(lax.bitcast_convert_type(w << 16, F32).astype(BF16))
                chunks.append(lax.bitcast_convert_type(w & jnp.uint32(0xFFFF0000), F32).astype(BF16))
            xb = jnp.concatenate(chunks, axis=-1)
            gate = _dot(xb, wgb[...])
            a = gate * jax.nn.sigmoid(gate) * _dot(xb, wub[...])
            y = _dot(a.astype(BF16), wdb[...])

            nxt = jnp.minimum(b + _GATHER_AHEAD, n_blk - 1)
            for r in range(bm):
                pltpu.make_async_copy(tok_words(tok_ref[nxt * bm + r]),
                                      xbuf.at[(b + _GATHER_AHEAD) % _GATHER_SLOTS, pl.ds(r * _SUBLANES, _SUBLANES)],
                                      gsem.at[(b + _GATHER_AHEAD) % _GATHER_SLOTS]).start(priority=1)

            @pl.when(b >= 2)
            def _():
                y_copy(b - 2, b % 2).wait()

            ybuf[b % 2] = y
            y_copy(b, b % 2).start()
            return carry

        lax.fori_loop(b_lo, b_hi, block, 0)

    @pl.when(e == pl.num_programs(0) - 1)
    def _():
        for k in range(_GATHER_AHEAD):
            wait_gather((n_valid + k) % _GATHER_SLOTS)

        @pl.when(n_valid >= 2)
        def _():
            y_copy(n_valid - 2, n_valid % 2).wait()
        y_copy(n_valid - 1, (n_valid - 1) % 2).wait()
        ybuf[0] = jnp.zeros(ybuf.shape[1:], F32)

        def zero_block(b, carry):
            cp = y_copy(b, 0)
            cp.start()
            cp.wait()
            return carry

        lax.fori_loop(n_valid, n_blk, zero_block, 0)


def _moe(tok, e_blk, hf, wg, wu, wd, n_blk, bm):
    D = wg.shape[1]
    wspec = lambda shape: pl.BlockSpec((1,) + shape, lambda e, *_: (e, 0, 0))
    grid_spec = pltpu.PrefetchScalarGridSpec(
        num_scalar_prefetch=2,
        grid=(N_EXPERTS,),
        in_specs=[pl.BlockSpec(memory_space=pl.ANY),
                  wspec((D, D_EXPERT)), wspec((D, D_EXPERT)), wspec((D_EXPERT, D))],
        out_specs=pl.BlockSpec(memory_space=pl.ANY),
        scratch_shapes=[pltpu.VMEM((_GATHER_SLOTS, bm * _SUBLANES, _LANES), U32),
                        pltpu.VMEM((2, bm, D), F32),
                        pltpu.VMEM((D, D_EXPERT), BF16), pltpu.VMEM((D, D_EXPERT), BF16),
                        pltpu.VMEM((D_EXPERT, D), BF16),
                        pltpu.SemaphoreType.DMA((_GATHER_SLOTS,)), pltpu.SemaphoreType.DMA((2,))],
    )
    return pl.pallas_call(
        functools.partial(_moe_body, bm=bm, n_blk=n_blk),
        grid_spec=grid_spec,
        out_shape=jax.ShapeDtypeStruct((n_blk * bm, D), F32),
        compiler_params=_params(),
        name="moe",
    )(tok, e_blk, hf, wg, wu, wd)


def _route_body(rl_ref, bias_ref, out_ref, cnt_ref, seen_ref, *, tm):
    @pl.when(pl.program_id(0) == 0)
    def _():
        seen_ref[...] = jnp.zeros(seen_ref.shape, F32)

    lg = rl_ref[...] + bias_ref[...]
    lane = lax.broadcasted_iota(I32, lg.shape, 1).astype(F32)
    first = lambda hit: jnp.min(jnp.where(hit, lane, float(_LANES)), axis=-1, keepdims=True)
    is_group = lane < N_GROUPS
    gl = jnp.where(is_group, lg, -jnp.inf)
    g_max = jnp.max(gl, axis=-1, keepdims=True)
    g_sel = first(gl == g_max)
    p_g = 1.0 / jnp.sum(jnp.where(is_group, jnp.exp(gl - g_max), 0.0), axis=-1, keepdims=True)

    e_id = lane - N_GROUPS
    in_group = (e_id >= 0) & (e_id < N_EXPERTS) & (jnp.floor(e_id / EXP_PER_GROUP) == g_sel)
    el = jnp.where(in_group, lg, -jnp.inf)
    ee = jnp.where(in_group, jnp.exp(el - jnp.max(el, axis=-1, keepdims=True)), 0.0)
    cand = jnp.where(in_group, ee / jnp.sum(ee, axis=-1, keepdims=True), -1.0)
    p1 = jnp.max(cand, axis=-1, keepdims=True)
    l1 = first(cand == p1)
    cand = jnp.where(lane == l1, -2.0, cand)
    p2 = jnp.max(cand, axis=-1, keepdims=True)
    l2 = first(cand == p2)
    g1 = p_g * p1 / (p1 + p2)
    g2 = p_g * p2 / (p1 + p2)

    oh1 = (lane == l1).astype(BF16)
    oh2 = (lane == l2).astype(BF16)
    both = oh1 + oh2
    earlier = (lax.broadcasted_iota(I32, (tm, tm), 1) < lax.broadcasted_iota(I32, (tm, tm), 0)).astype(BF16)
    before = _dot(earlier, both) + seen_ref[...]
    r1 = jnp.sum(before * oh1.astype(F32), axis=-1, keepdims=True)
    r2 = jnp.sum(before * oh2.astype(F32), axis=-1, keepdims=True)
    seen_ref[...] = seen_ref[...] + jnp.sum(both.astype(F32), axis=0, keepdims=True)
    cnt_ref[...] = seen_ref[...]

    cols = (l1 - N_GROUPS, l2 - N_GROUPS, r1, r2, g1, g2)
    out = jnp.zeros(lg.shape, F32)
    for c, v in enumerate(cols):
        out = jnp.where(lane == c, v, out)
    out_ref[...] = out


def _route_tokens(rl, bias, tm):
    T = rl.shape[0]
    return pl.pallas_call(
        functools.partial(_route_body, tm=tm),
        grid=(T // tm,),
        in_specs=[pl.BlockSpec((tm, _LANES), lambda i: (i, 0)), _resident(bias.shape)],
        out_specs=[pl.BlockSpec((tm, _LANES), lambda i: (i, 0)), pl.BlockSpec((1, _LANES), lambda i: (0, 0))],
        out_shape=[jax.ShapeDtypeStruct((T, _LANES), F32), jax.ShapeDtypeStruct((1, _LANES), F32)],
        scratch_shapes=[pltpu.VMEM((1, _LANES), F32)],
        compiler_params=_params(),
        name="route",
    )(rl, bias)


def _route(rl, b_group, b_router, bm, tm):
    N = rl.shape[0]
    bias = jnp.concatenate([b_group, b_router, jnp.zeros((_LANES - N_GROUPS - N_EXPERTS,), F32)]).reshape(1, _LANES)
    per_tok, seen = _route_tokens(rl, bias, tm)
    expert_id = per_tok[:, 0:TOPK_IN_GROUP].astype(I32)
    rank = per_tok[:, TOPK_IN_GROUP:2 * TOPK_IN_GROUP].astype(I32)
    gates = per_tok[:, 2 * TOPK_IN_GROUP:3 * TOPK_IN_GROUP]
    counts = seen[0, N_GROUPS:N_GROUPS + N_EXPERTS].astype(I32)

    A = N * TOPK_IN_GROUP
    padded = (counts + bm - 1) // bm * bm
    pad_ends = jnp.cumsum(padded)
    pad_starts = pad_ends - padded
    onehot = expert_id[..., None] == jnp.arange(N_EXPERTS, dtype=I32)
    row = (jnp.sum(jnp.where(onehot, pad_starts, 0), axis=-1) + rank).reshape(A)
    n_blk = -(-A // bm) + N_EXPERTS
    row_tok = jnp.zeros((n_blk * bm,), I32).at[row].set(jnp.arange(A, dtype=I32) // TOPK_IN_GROUP)
    e_blk = jnp.concatenate([pad_starts, pad_ends[-1:]]) // bm
    return row_tok, row, gates, e_blk.astype(I32), n_blk


def _final_body(row_ref, x_ref, gate_ref, y_hbm, g_ref, out_ref, ybuf, sem, *, tm):
    i = pl.program_id(0)
    n_tiles = pl.num_programs(0)
    slot = i % _FINAL_SLOTS

    def gather(tile, s):
        def issue(g, carry):
            for u in range(_SUBLANES):
                for k in range(TOPK_IN_GROUP):
                    src = row_ref[(tile * tm + g * _SUBLANES + u) * TOPK_IN_GROUP + k]
                    pltpu.make_async_copy(y_hbm.at[pl.ds(src, 1)], ybuf.at[s, k, g, pl.ds(u, 1)],
                                          sem.at[s]).start()
            return carry
        lax.fori_loop(0, tm // _SUBLANES, issue, 0)

    def wait_rows(s):
        pltpu.make_async_copy(ybuf.at[s], ybuf.at[s], sem.at[s]).wait()

    @pl.when(i == 0)
    def _():
        for t in range(_FINAL_SLOTS - 1):
            gather(jnp.minimum(t, n_tiles - 1), t)

    wait_rows(slot)
    acc = x_ref[...]
    for k in range(TOPK_IN_GROUP):
        acc = acc + gate_ref[:, k:k + 1] * ybuf[slot, k].reshape(tm, ybuf.shape[-1])
    y = _rms(acc, g_ref[...])

    ahead = _FINAL_SLOTS - 1
    nxt = jnp.minimum(i + ahead, n_tiles - 1)
    for r in range(tm):
        for k in range(TOPK_IN_GROUP):
            src = row_ref[(nxt * tm + r) * TOPK_IN_GROUP + k]
            pltpu.make_async_copy(y_hbm.at[pl.ds(src, 1)],
                                  ybuf.at[(i + ahead) % _FINAL_SLOTS, k, r // _SUBLANES, pl.ds(r % _SUBLANES, 1)],
                                  sem.at[(i + ahead) % _FINAL_SLOTS]).start()
    out_ref[...] = y

    @pl.when(i == n_tiles - 1)
    def _():
        for t in range(1, _FINAL_SLOTS):
            wait_rows((i + t) % _FINAL_SLOTS)


def _final(row, x2, gates, y_rows, g, tm):
    T, D = x2.shape
    grid_spec = pltpu.PrefetchScalarGridSpec(
        num_scalar_prefetch=1,
        grid=(T // tm,),
        in_specs=[pl.BlockSpec((tm, D), lambda i, *_: (i, 0)),
                  pl.BlockSpec((tm, TOPK_IN_GROUP), lambda i, *_: (i, 0)),
                  pl.BlockSpec(memory_space=pl.ANY),
                  pl.BlockSpec(g.shape, lambda i, *_: (0, 0))],
        out_specs=pl.BlockSpec((tm, D), lambda i, *_: (i, 0)),
        scratch_shapes=[pltpu.VMEM((_FINAL_SLOTS, TOPK_IN_GROUP, tm // _SUBLANES, _SUBLANES, D), F32),
                        pltpu.SemaphoreType.DMA((_FINAL_SLOTS,))],
    )
    return pl.pallas_call(
        functools.partial(_final_body, tm=tm),
        grid_spec=grid_spec,
        out_shape=jax.ShapeDtypeStruct((T, D), F32),
        compiler_params=_params(),
        name="final",
    )(row, x2, gates, y_rows, g)


def _tile_sizes(T):
    pick = lambda want: want if T % want == 0 else CHUNK
    return dict(inproj=pick(256), dsa_q=pick(256), dsa_k=pick(512), mlstm=pick(256), mixout=pick(256),
                final=pick(256), route=pick(512), moe=128)


def _layer(x, mem, norm_mix_g, w_in, kv_norm_g, k_idx_norm_g, w_uk, w_uv, conv_w, conv_b, gate_b, ml_norm_g,
           w_out, norm_x_g, mem_norm_g, w_cq, w_ckv, w_co, norm_ffn_g, w_group, b_group, w_router, b_router,
           w_gate, w_up, w_down, out_g):
    T = x.shape[0]
    ts = _tile_sizes(T)
    r2 = lambda v: v.reshape(1, -1)

    w_r = _wprep(w_in, 256)
    wuk_t = jnp.transpose(w_uk, (1, 2, 0)).astype(BF16)
    wuv_t = jnp.transpose(w_uv, (1, 0, 2)).astype(BF16)

    qabs, ckv, qi, kidx, small, mqk, mv, mo = _inproj(
        x, r2(norm_mix_g), w_r, wuk_t, r2(kv_norm_g), r2(k_idx_norm_g), ts["inproj"])

    gate_rows = jnp.transpose(small[:, _S_WI:_S_MF + ML_HEADS])
    dsa_out = _dsa(qi, gate_rows, qabs, kidx, ckv, wuv_t, ts["dsa_q"], ts["dsa_k"])

    gb_col = jnp.zeros((1, _SMALL), F32).at[0, _S_MI:_S_MI + 2 * ML_HEADS].set(gate_b)
    ml_out = _mlstm(mqk, mv, small, gate_rows, mo, conv_w, r2(conv_b), gb_col, gate_b.reshape(-1, 1),
                    r2(ml_norm_g), ts["mlstm"])

    wqk, wvo = _memfold(mem, r2(mem_norm_g), w_ckv, w_cq, w_co)
    w_rt = jnp.concatenate([w_group, w_router,
                            jnp.zeros((D_MODEL, 128 - N_GROUPS - N_EXPERTS), w_group.dtype)], axis=1)
    x2, hf, rl = _mixout(x, dsa_out, ml_out, w_out.astype(BF16), wqk, wvo, w_rt.astype(BF16),
                         r2(norm_x_g), r2(norm_ffn_g), ts["mixout"])

    bm = ts["moe"]
    row_tok, row, gates, e_blk, n_blk = _route(rl, b_group, b_router, bm, ts["route"])
    y_rows = _moe(row_tok, e_blk, hf, w_gate, w_up, w_down, n_blk, bm)
    return _final(row, x2, gates, y_rows, r2(out_g), ts["final"])


def kernel(x, mem, norm_mix_g, w_in, kv_norm_g, k_idx_norm_g, w_uk, w_uv, conv_w, conv_b, gate_b, ml_norm_g,
           w_out, norm_x_g, mem_norm_g, w_cq, w_ckv, w_co, norm_ffn_g, w_group, b_group, w_router, b_router,
           w_gate, w_up, w_down, final_norm_g):
    B, T, D = x.shape
    assert B == 1 and D == D_MODEL and norm_mix_g.shape[0] == 1 and T % CHUNK == 0
    out = _layer(x[0], mem[0], norm_mix_g[0], w_in, kv_norm_g[0], k_idx_norm_g[0], w_uk[0], w_uv[0],
                 conv_w[0], conv_b[0], gate_b[0], ml_norm_g[0], w_out[0], norm_x_g[0], mem_norm_g[0],
                 w_cq[0], w_ckv[0], w_co[0], norm_ffn_g[0], w_group[0], b_group[0], w_router[0], b_router[0],
                 w_gate[0], w_up[0], w_down[0], final_norm_g)
    return out[None]
```

```python
import functools

import jax
import jax.numpy as jnp
import numpy as np
from jax import lax
from jax.experimental import pallas as pl
from jax.experimental.pallas import tpu as pltpu

F32 = jnp.float32
BF16 = jnp.bfloat16
I32 = jnp.int32
I16 = jnp.int16
U32 = jnp.uint32

EPS = 1e-6
CHUNK = 64
D_MODEL = 2048

DSA_HEADS = 8
DSA_HEAD_DIM = 128
DSA_LATENT = 256
IDX_HEADS = 8
IDX_DIM = 64
TOPK_MAX = 256

ML_HEADS = 4
ML_QK_DIM = 128
ML_V_DIM = 256
CONV_W = 4

X_HEADS = 4
X_HEAD_DIM = D_MODEL // X_HEADS

N_GROUPS = 4
EXP_PER_GROUP = 8
N_EXPERTS = N_GROUPS * EXP_PER_GROUP
TOPK_IN_GROUP = 2
D_EXPERT = 512

_O_DQ = 0
_O_CKV = _O_DQ + DSA_HEADS * DSA_HEAD_DIM
_O_QI = _O_CKV + DSA_LATENT
_O_KI = _O_QI + IDX_HEADS * IDX_DIM
_O_WI = _O_KI + IDX_DIM
_O_MQ = _O_WI + IDX_HEADS
_O_MK = _O_MQ + ML_HEADS * ML_QK_DIM
_O_MV = _O_MK + ML_HEADS * ML_QK_DIM
_O_MI = _O_MV + ML_HEADS * ML_V_DIM
_O_MF = _O_MI + ML_HEADS
_O_MO = _O_MF + ML_HEADS
_O_END = _O_MO + ML_HEADS * ML_V_DIM

_G_DQ = (0, 1024)
_G_CKV = (1024, 1280)
_G_QI = (1280, 1792)
_G_SMALL = (1792, 1920)
_G_MQK = (1920, 2944)
_G_MV = (2944, 3968)
_G_MO = (3968, 4992)
_W_COLS = 4992
_S_WI = IDX_DIM
_S_MI = _S_WI + IDX_HEADS
_S_MF = _S_MI + ML_HEADS
_SMALL = 128

_VMEM_LIMIT = 56 * 1024 * 1024
_INT_MIN = -(2 ** 31)
_I16_MIN = -(2 ** 15)
_CHUNK_SHIFT = CHUNK.bit_length() - 1
_LOG2E = 1.4426950408889634
_SUBLANES = 8
_LANES = 128
_GATHER_SLOTS = 9
_GATHER_AHEAD = _GATHER_SLOTS - 1
_TOPK_SHIFT = TOPK_IN_GROUP.bit_length() - 1
_FINAL_SLOTS = 3
_NEG = -1e30


def _rms(v, g):
    return v * lax.rsqrt(jnp.mean(v * v, axis=-1, keepdims=True) + EPS) * g


def _dot(a, b):
    return jnp.dot(a, b, preferred_element_type=F32)


def _dot_nt(a, b):
    return lax.dot_general(a, b, (((1,), (1,)), ((), ())), preferred_element_type=F32)


def _resident(shape):
    nd = len(shape)
    return pl.BlockSpec(shape, lambda *_: (0,) * nd, pipeline_mode=pl.Buffered(1))


def _params(n_axes=1):
    return pltpu.CompilerParams(dimension_semantics=("arbitrary",) * n_axes,
                                vmem_limit_bytes=_VMEM_LIMIT)


def _wprep_body(w_hbm, out_ref, wbuf, sem, *, tk):
    i = pl.program_id(0)
    slot = i % 2

    def rows(blk, s):
        return pltpu.make_async_copy(w_hbm.at[0, pl.ds(pl.multiple_of(blk * tk, tk), tk)], wbuf.at[s], sem.at[s])

    @pl.when(i == 0)
    def _():
        rows(0, 0).start()

    @pl.when(i + 1 < pl.num_programs(0))
    def _():
        rows(i + 1, 1 - slot).start()

    rows(i, slot).wait()
    w_ref = wbuf.at[slot]
    n_small = IDX_DIM + IDX_HEADS
    out_ref[:, _G_DQ[0]:_G_SMALL[0] + n_small] = w_ref[:, _O_DQ:_O_MQ].astype(BF16)
    out_ref[:, _G_SMALL[0] + n_small:_G_SMALL[0] + n_small + 2 * ML_HEADS] = w_ref[:, _O_MI:_O_MO].astype(BF16)
    out_ref[:, _G_SMALL[0] + n_small + 2 * ML_HEADS:_G_SMALL[1]] = jnp.zeros(
        (out_ref.shape[0], _SMALL - n_small - 2 * ML_HEADS), BF16)
    out_ref[:, _G_MQK[0]:_G_MV[1]] = w_ref[:, _O_MQ:_O_MI].astype(BF16)
    out_ref[:, _G_MO[0]:_G_MO[1]] = w_ref[:, _O_MO:_O_END].astype(BF16)


def _wprep(w_in, tk):
    K = w_in.shape[1]
    return pl.pallas_call(
        functools.partial(_wprep_body, tk=tk),
        grid=(K // tk,),
        in_specs=[pl.BlockSpec(memory_space=pl.ANY)],
        out_specs=pl.BlockSpec((tk, _W_COLS), lambda i: (i, 0)),
        out_shape=jax.ShapeDtypeStruct((K, _W_COLS), BF16),
        scratch_shapes=[pltpu.VMEM((2, tk, w_in.shape[2]), F32), pltpu.SemaphoreType.DMA((2,))],
        compiler_params=_params(),
        name="wprep",
    )(w_in)


def _inproj_body(x_ref, g_ref, w_ref, wuk_ref, kvg_ref, kig_ref,
                 qabs_ref, ckv_ref, qi_ref, kidx_ref, small_ref, mqk_ref, mv_ref, mo_ref):
    h = _rms(x_ref[...], g_ref[...]).astype(BF16)

    def proj(grp):
        return _dot(h, w_ref[:, grp[0]:grp[1]])

    dq = proj(_G_DQ)
    for hd in range(DSA_HEADS):
        qh = dq[:, hd * DSA_HEAD_DIM:(hd + 1) * DSA_HEAD_DIM].astype(BF16)
        qa = _dot(qh, wuk_ref[hd]) * (DSA_HEAD_DIM ** -0.5 * _LOG2E)
        qabs_ref[:, hd * DSA_LATENT:(hd + 1) * DSA_LATENT] = qa.astype(BF16)
    ckv_ref[...] = _rms(proj(_G_CKV), kvg_ref[...]).astype(BF16)
    qi_ref[...] = (proj(_G_QI) * (IDX_DIM ** -0.5)).astype(BF16)
    small = proj(_G_SMALL)
    small_ref[...] = small
    kidx_ref[...] = _rms(small[:, :IDX_DIM], kig_ref[...]).astype(BF16)
    mqk_ref[...] = proj(_G_MQK)
    mv_ref[...] = proj(_G_MV).astype(BF16)
    mo_ref[...] = proj(_G_MO)


def _inproj(x, g, w, wuk, kvg, kig, tm):
    T = x.shape[0]
    row = lambda n: pl.BlockSpec((tm, n), lambda i: (i, 0))
    outs = [(8 * DSA_LATENT, BF16), (DSA_LATENT, BF16), (IDX_HEADS * IDX_DIM, BF16), (IDX_DIM, BF16),
            (_SMALL, F32), (2 * ML_HEADS * ML_QK_DIM, F32), (ML_HEADS * ML_V_DIM, BF16),
            (ML_HEADS * ML_V_DIM, F32)]
    return pl.pallas_call(
        _inproj_body,
        grid=(T // tm,),
        in_specs=[row(D_MODEL), _resident(g.shape), _resident(w.shape), _resident(wuk.shape),
                  _resident(kvg.shape), _resident(kig.shape)],
        out_specs=[row(n) for n, _ in outs],
        out_shape=[jax.ShapeDtypeStruct((T, n), dt) for n, dt in outs],
        compiler_params=_params(),
        name="inproj",
    )(x, g, w, wuk, kvg, kig)


def _sublane_fold(v, op, rows=_SUBLANES, ways=4):
    groups = [v[r * rows:(r + 1) * rows, :] for r in range(v.shape[0] // rows)]
    accs = groups[:ways]
    for r in range(ways, len(groups)):
        accs[r % ways] = op(accs[r % ways], groups[r])
    while len(accs) > 1:
        accs = [op(accs[k], accs[k + 1]) if k + 1 < len(accs) else accs[k] for k in range(0, len(accs), 2)]
    return accs[0]


def _bit_transpose32(words):
    a = list(words)
    j, m = 16, 0x0000FFFF
    while j:
        k = 0
        while k < 32:
            t = (a[k] ^ (a[k + j] >> j)) & jnp.uint32(m)
            a[k] = a[k] ^ t
            a[k + j] = a[k + j] ^ (t << j)
            k = (k + j + 1) & ~j
        j >>= 1
        m = (m ^ (m << j)) & 0xFFFFFFFF
    return a


def _dsa_body(qi_ref, wrow_ref, qabs_ref, kidx_ref, ckv_ref, ckvt_ref, wuv_ref, out_ref,
              key_ref, planes_ref, alive_ref, lga_ref, lgb_ref, m_ref, l_ref, acc_ref, *, tq, tk, topk, nbits_idx):
    i = pl.program_id(0)
    n_kb = ((i + 1) * tq + tk - 1) // tk
    w_rows = wrow_ref[0:IDX_HEADS, :] * (IDX_HEADS ** -0.5)
    q_chunk = (i * tq + lax.broadcasted_iota(I32, (1, tq), 1)) >> _CHUNK_SHIFT

    def key_pos(j):
        return j * tk + lax.broadcasted_iota(I32, (tk, 1), 0)

    def score_block(j, carry):
        kx = kidx_ref[pl.ds(pl.multiple_of(j * tk, tk), tk), :]
        s = jnp.zeros((tk, tq), F32)
        for hd in range(IDX_HEADS):
            d = _dot_nt(kx, qi_ref[:, hd * IDX_DIM:(hd + 1) * IDX_DIM])
            s = s + w_rows[hd:hd + 1, :] * jnp.maximum(d, 0.0)
        bits = lax.bitcast_convert_type(s, I32)
        key = bits ^ ((bits >> 31) & 0x7FFFFFFF)
        key_ref[j] = jnp.where((key_pos(j) >> _CHUNK_SHIFT) <= q_chunk, key, _INT_MIN)
        return carry

    lax.fori_loop(0, n_kb, score_block, 0)

    def count(pred):
        def body(j, acc):
            hit = pred(key_ref[j], key_pos(j)).astype(I32)
            return acc + _sublane_fold(hit, jnp.add)
        acc = lax.fori_loop(0, n_kb, body, jnp.zeros((8, tq), I32))
        return jnp.sum(acc, axis=0, keepdims=True)

    plane_rows = tk // 32

    @pl.when(i == 0)
    def _():
        planes_ref[...] = jnp.zeros(planes_ref.shape, U32)

    def slice_block(j, carry):
        u = lax.bitcast_convert_type(key_ref[j], U32) ^ jnp.uint32(0x80000000)
        for h in range(plane_rows // _SUBLANES):
            base = h * 32 * _SUBLANES
            planes = _bit_transpose32([u[base + _SUBLANES * v:base + _SUBLANES * (v + 1), :] for v in range(32)])
            row0 = pl.multiple_of(j * plane_rows + h * _SUBLANES, _SUBLANES)
            for b in range(32):
                planes_ref[b, pl.ds(row0, _SUBLANES), :] = planes[31 - b]
        return carry

    lax.fori_loop(0, n_kb, slice_block, 0)
    word_row = lax.broadcasted_iota(I32, (alive_ref.shape[0], 1), 0)
    alive_ref[...] = jnp.where(word_row < n_kb * plane_rows, jnp.full(alive_ref.shape, 0xFFFFFFFF, U32),
                               jnp.uint32(0))

    def radix_select(n_rows):
        def ones_count(words):
            return jnp.sum(_sublane_fold(lax.population_count(words).astype(I32), jnp.add), axis=0, keepdims=True)

        def select_bit(it, carry):
            need, t_u = carry
            b = 31 - it
            alive = alive_ref[0:n_rows, :]
            ones = alive & planes_ref[b, pl.ds(0, n_rows), :]
            n_ones = ones_count(ones)
            take = n_ones >= need
            alive_ref[0:n_rows, :] = jnp.where(take, ones, alive ^ ones)
            bit = lax.shift_left(jnp.uint32(1), jnp.asarray(b, U32))
            return jnp.where(take, need, need - n_ones), jnp.where(take, t_u | bit, t_u)

        need, t_u = lax.fori_loop(0, 32, select_bit, (jnp.full((1, tq), topk, I32), jnp.zeros((1, tq), U32)))
        return need, t_u, ones_count(alive_ref[0:n_rows, :])

    all_rows = alive_ref.shape[0]
    if all_rows % (2 * _SUBLANES) == 0:
        need, t_u, n_equal = lax.cond(n_kb * plane_rows <= all_rows // 2,
                                      lambda: radix_select(all_rows // 2), lambda: radix_select(all_rows))
    else:
        need, t_u, n_equal = radix_select(all_rows)
    short = t_u == 0
    t = jnp.maximum(lax.bitcast_convert_type(t_u ^ jnp.uint32(0x80000000), I32), _INT_MIN + 1)
    all_pos = jnp.int32(2 ** nbits_idx - 1)
    has_ties = (n_equal > need) & jnp.logical_not(short)
    n_tie_take = jnp.where(has_ties, need, all_pos)

    def tie_cutoff():
        def pos_bit(b, c):
            cand = c + lax.shift_left(jnp.int32(1), nbits_idx - 1 - b)
            f = count(lambda kb, pos: (kb == t) & (pos < cand))
            return jnp.where(f <= n_tie_take, cand, c)
        return lax.fori_loop(0, nbits_idx, pos_bit, jnp.zeros((1, tq), I32))

    cut = lax.cond(jnp.max(has_ties.astype(I32)) > 0, tie_cutoff, lambda: jnp.full((1, tq), all_pos, I32))

    m_ref[...] = jnp.full(m_ref.shape, _NEG, F32)
    l_ref[...] = jnp.zeros(l_ref.shape, F32)
    acc_ref[...] = jnp.zeros(acc_ref.shape, F32)
    last_blk = ckvt_ref.shape[0] - 1

    def logits(j, lg_buf):
        c_blk = ckv_ref[pl.ds(pl.multiple_of(jnp.minimum(j, last_blk) * tk, tk), tk), :]
        kb = key_ref[j]
        sel = (kb > t) | ((kb == t) & (key_pos(j) < cut))
        bias = jnp.where(sel, 0.0, _NEG).astype(F32)
        for hd in range(DSA_HEADS):
            lg_buf[hd] = _dot_nt(c_blk, qabs_ref[:, hd * DSA_LATENT:(hd + 1) * DSA_LATENT]) + bias

    def accumulate(j, lg_buf):
        c_blk_t = ckvt_ref[jnp.minimum(j, last_blk)]
        for hd in range(DSA_HEADS):
            lg = lg_buf[hd]
            m_old = m_ref[hd:hd + 1, :]
            m_new = jnp.maximum(m_old, jnp.max(_sublane_fold(lg, jnp.maximum, ways=1), axis=0, keepdims=True))
            p = jnp.exp2(lg - m_new)
            alpha = jnp.exp2(m_old - m_new)
            l_ref[hd:hd + 1, :] = alpha * l_ref[hd:hd + 1, :] + jnp.sum(_sublane_fold(p, jnp.add, ways=1), axis=0,
                                                                         keepdims=True)
            acc_ref[hd] = alpha * acc_ref[hd] + _dot(c_blk_t, p.astype(BF16))
            m_ref[hd:hd + 1, :] = m_new

    logits(0, lga_ref)

    def attn_pair(mi, carry):
        ja = 2 * mi
        accumulate(ja, lga_ref)
        logits(ja + 1, lgb_ref)
        accumulate(ja + 1, lgb_ref)
        logits(jnp.minimum(ja + 2, n_kb - 1), lga_ref)
        return carry

    lax.fori_loop(0, n_kb // 2, attn_pair, 0)

    @pl.when(n_kb % 2 == 1)
    def _():
        accumulate(n_kb - 1, lga_ref)

    for hd in range(DSA_HEADS):
        o_lat = (acc_ref[hd] / l_ref[hd:hd + 1, :]).T.astype(BF16)
        out_ref[:, hd * DSA_HEAD_DIM:(hd + 1) * DSA_HEAD_DIM] = _dot(o_lat, wuv_ref[hd]).astype(BF16)


def _dsa(qi, wrows, qabs, kidx, ckv, wuv, tq, tk):
    T = qi.shape[0]
    topk = min(TOPK_MAX, T // 4)
    n_kb = T // tk
    ckvt = jnp.transpose(ckv.reshape(n_kb, tk, DSA_LATENT), (0, 2, 1))
    row = lambda n: pl.BlockSpec((tq, n), lambda i: (i, 0))
    body = functools.partial(_dsa_body, tq=tq, tk=tk, topk=topk, nbits_idx=int(T).bit_length())
    return pl.pallas_call(
        body,
        grid=(T // tq,),
        in_specs=[row(qi.shape[1]), pl.BlockSpec((wrows.shape[0], tq), lambda i: (0, i)), row(qabs.shape[1]),
                  _resident(kidx.shape), _resident(ckv.shape), _resident(ckvt.shape), _resident(wuv.shape)],
        out_specs=row(DSA_HEADS * DSA_HEAD_DIM),
        out_shape=jax.ShapeDtypeStruct((T, DSA_HEADS * DSA_HEAD_DIM), BF16),
        scratch_shapes=[pltpu.VMEM((n_kb + n_kb % 2, tk, tq), I32),
                        pltpu.VMEM((32, T // 32, tq), U32), pltpu.VMEM((T // 32, tq), U32),
                        pltpu.VMEM((DSA_HEADS, tk, tq), F32), pltpu.VMEM((DSA_HEADS, tk, tq), F32),
                        pltpu.VMEM((DSA_HEADS, tq), F32),
                        pltpu.VMEM((DSA_HEADS, tq), F32), pltpu.VMEM((DSA_HEADS, DSA_LATENT, tq), F32)],
        compiler_params=_params(),
        name="dsa",
    )(qi, wrows, qabs, kidx, ckv, ckvt, wuv)


def _log_sigmoid(v):
    return jnp.minimum(v, 0.0) - jnp.log1p(jnp.exp(-jnp.abs(v)))


def _chunk_cumsum(v, axis):
    pos = lax.broadcasted_iota(I32, v.shape, axis) & (CHUNK - 1)
    d = 1
    while d < CHUNK:
        v = v + jnp.where(pos >= d, pltpu.roll(v, d, axis=axis), 0.0)
        d *= 2
    return v


def _mlstm_body(mqk_ref, mv_ref, small_ref, gt_ref, mo_ref, cw_ref, cb_ref, gbc_ref, gbr_ref, ng_ref,
                out_ref, xe_ref, c_ref, n_ref, m_ref, hs_ref, *, rows):
    @pl.when(pl.program_id(0) == 0)
    def _():
        xe_ref[0:8, :] = jnp.zeros((8, xe_ref.shape[1]), F32)
        c_ref[...] = jnp.zeros(c_ref.shape, F32)
        n_ref[...] = jnp.zeros(n_ref.shape, F32)
        m_ref[...] = jnp.zeros(m_ref.shape, F32)

    x = mqk_ref[...]
    xe_ref[8:8 + rows, :] = x
    y = cb_ref[...]
    for j in range(CONV_W - 1):
        y = y + xe_ref[5 + j:5 + j + rows, :] * cw_ref[j:j + 1, :]
    y = y + x * cw_ref[CONV_W - 1:CONV_W, :]
    xe_ref[0:8, :] = x[rows - 8:rows, :]
    qk = y * jax.nn.sigmoid(y)
    nqk = ML_HEADS * ML_QK_DIM
    q_all = (qk[:, :nqk] * (ML_QK_DIM ** -0.5)).astype(BF16)
    k_all = qk[:, nqk:]

    g_col = small_ref[...] + gbc_ref[...]
    g_row = gt_ref[...] + gbr_ref[...]
    b_col = _chunk_cumsum(_log_sigmoid(g_col), 0)
    b_row = _chunk_cumsum(_log_sigmoid(g_row), 1)

    tri = lax.broadcasted_iota(I32, (CHUNK, CHUNK), 1) <= lax.broadcasted_iota(I32, (CHUNK, CHUNK), 0)

    heads = range(ML_HEADS)
    for c in range(rows // CHUNK):
        lo, hi = c * CHUNK, (c + 1) * CHUNK
        bc = [b_col[lo:hi, _S_MF + hd:_S_MF + hd + 1] for hd in heads]
        lic = [g_col[lo:hi, _S_MI + hd:_S_MI + hd + 1] for hd in heads]
        br = [b_row[ML_HEADS + hd:ML_HEADS + hd + 1, lo:hi] for hd in heads]
        lir = [g_row[hd:hd + 1, lo:hi] for hd in heads]
        g_tot = [bc[hd][CHUNK - 1:CHUNK, :] for hd in heads]
        m_prev = [m_ref[hd][:, 0:1] for hd in heads]
        qh = [q_all[lo:hi, hd * ML_QK_DIM:(hd + 1) * ML_QK_DIM] for hd in heads]
        kh = [k_all[lo:hi, hd * ML_QK_DIM:(hd + 1) * ML_QK_DIM] for hd in heads]
        vh = [mv_ref[lo:hi, hd * ML_V_DIM:(hd + 1) * ML_V_DIM] for hd in heads]
        c_prev = [c_ref[hd] for hd in heads]
        n_prev = [n_ref[hd] for hd in heads]

        dmat = [jnp.where(tri, bc[hd] - br[hd] + lir[hd], -jnp.inf) for hd in heads]
        inter = [bc[hd] + m_prev[hd] for hd in heads]
        m_t = [jnp.maximum(inter[hd], jnp.max(dmat[hd], axis=-1, keepdims=True)) for hd in heads]
        w_intra = [jnp.exp(dmat[hd] - m_t[hd]) for hd in heads]
        a_inter = [jnp.exp(inter[hd] - m_t[hd]) for hd in heads]
        m_new = [jnp.maximum(g_tot[hd] + m_prev[hd],
                             jnp.max(g_tot[hd] - br[hd] + lir[hd], axis=-1, keepdims=True)) for hd in heads]
        a_state = [jnp.exp(g_tot[hd] + m_prev[hd] - m_new[hd]) for hd in heads]
        wk = [jnp.exp(g_tot[hd] - bc[hd] + lic[hd] - m_new[hd]) * kh[hd] for hd in heads]

        s_qk = [_dot_nt(qh[hd], kh[hd].astype(BF16)) * w_intra[hd] for hd in heads]
        read = [_dot(qh[hd], c_prev[hd].astype(BF16)) for hd in heads]
        d_c = [_dot(wk[hd].T.astype(BF16), vh[hd]) for hd in heads]
        num = [a_inter[hd] * read[hd] + _dot(s_qk[hd].astype(BF16), vh[hd]) for hd in heads]
        den = [a_inter[hd] * jnp.sum(qh[hd].astype(F32) * n_prev[hd], axis=-1, keepdims=True)
               + jnp.sum(s_qk[hd], axis=-1, keepdims=True) for hd in heads]
        for hd in heads:
            hs_ref[lo:hi, hd * ML_V_DIM:(hd + 1) * ML_V_DIM] = (
                num[hd] / jnp.maximum(jnp.abs(den[hd]), jnp.exp(-m_t[hd])))
        for hd in heads:
            c_ref[hd] = a_state[hd] * c_prev[hd] + d_c[hd]
            n_ref[hd] = a_state[hd] * n_prev[hd] + jnp.sum(wk[hd], axis=0, keepdims=True)
            m_ref[hd] = jnp.broadcast_to(m_new[hd], m_ref.shape[1:])

    for hd in range(ML_HEADS):
        sl = slice(hd * ML_V_DIM, (hd + 1) * ML_V_DIM)
        out_ref[:, sl] = (_rms(hs_ref[:, sl], ng_ref[:, sl]) * jax.nn.sigmoid(mo_ref[:, sl])).astype(BF16)


def _mlstm(mqk, mv, small, gt, mo, cw, cb, gbc, gbr, ng, rows):
    T = mqk.shape[0]
    row = lambda n: pl.BlockSpec((rows, n), lambda i: (i, 0))
    nv = ML_HEADS * ML_V_DIM
    return pl.pallas_call(
        functools.partial(_mlstm_body, rows=rows),
        grid=(T // rows,),
        in_specs=[row(mqk.shape[1]), row(nv), row(_SMALL), pl.BlockSpec((8, rows), lambda i: (1, i)), row(nv),
                  _resident(cw.shape), _resident(cb.shape), _resident(gbc.shape), _resident(gbr.shape),
                  _resident(ng.shape)],
        out_specs=row(nv),
        out_shape=jax.ShapeDtypeStruct((T, nv), BF16),
        scratch_shapes=[pltpu.VMEM((rows + 8, mqk.shape[1]), F32),
                        pltpu.VMEM((ML_HEADS, ML_QK_DIM, ML_V_DIM), F32),
                        pltpu.VMEM((ML_HEADS, 1, ML_QK_DIM), F32),
                        pltpu.VMEM((ML_HEADS, 1, 128), F32),
                        pltpu.VMEM((rows, nv), F32)],
        compiler_params=_params(),
        name="mlstm",
    )(mqk, mv, small, gt, mo, cw, cb, gbc, gbr, ng)


def _memfold_body(mem_ref, g_ref, wk_ref, wv_ref, wq_ref, wc_ref, wqk_ref, wvo_ref):
    mn = _rms(mem_ref[...], g_ref[...]).astype(BF16)
    k = _dot(mn, wk_ref[...].astype(BF16)).astype(BF16)
    v = _dot(mn, wv_ref[...].astype(BF16)).astype(BF16)
    wqk_ref[...] = (_dot_nt(wq_ref[...].astype(BF16), k) * (X_HEAD_DIM ** -0.5)).astype(BF16)
    wvo_ref[...] = _dot(v, wc_ref[...].astype(BF16)).astype(BF16)


def _memfold(mem, g, w_ckv, w_cq, w_co):
    M, D = mem.shape
    dh = X_HEAD_DIM
    return pl.pallas_call(
        _memfold_body,
        grid=(X_HEADS,),
        in_specs=[_resident(mem.shape), _resident(g.shape),
                  pl.BlockSpec((D, dh), lambda h: (0, h)), pl.BlockSpec((D, dh), lambda h: (0, X_HEADS + h)),
                  pl.BlockSpec((D, dh), lambda h: (0, h)), pl.BlockSpec((dh, D), lambda h: (h, 0))],
        out_specs=[pl.BlockSpec((D, M), lambda h: (0, h)), pl.BlockSpec((M, D), lambda h: (h, 0))],
        out_shape=[jax.ShapeDtypeStruct((D, X_HEADS * M), BF16), jax.ShapeDtypeStruct((X_HEADS * M, D), BF16)],
        compiler_params=_params(),
        name="memfold",
    )(mem, g, w_ckv, w_ckv, w_cq, w_co)


def _mixout_body(x_ref, dsa_ref, ml_ref, wo_ref, wqk_ref, wvo_ref, wr_ref, gx_ref, gf_ref,
                 x2_ref, hf_ref, rl_ref, p_ref):
    nd = dsa_ref.shape[1]
    x1 = x_ref[...] + _dot(dsa_ref[...], wo_ref[0:nd, :]) + _dot(ml_ref[...], wo_ref[nd:, :])
    lg_all = _dot(_rms(x1, gx_ref[...]).astype(BF16), wqk_ref[...])
    n_mem = wqk_ref.shape[1] // X_HEADS
    for hd in range(X_HEADS):
        sl = slice(hd * n_mem, (hd + 1) * n_mem)
        lg = lg_all[:, sl]
        e = jnp.exp(lg - jnp.max(lg, axis=-1, keepdims=True))
        p_ref[:, sl] = (e / jnp.sum(e, axis=-1, keepdims=True)).astype(BF16)
    x2 = x1 + _dot(p_ref[...], wvo_ref[...])
    x2_ref[...] = x2
    hf = _rms(x2, gf_ref[...]).astype(BF16)
    rl_ref[...] = _dot(hf, wr_ref[...])
    bits = lax.bitcast_convert_type(hf.astype(F32), U32)
    for c in range(_SUBLANES):
        lo = bits[:, (2 * c) * _LANES:(2 * c + 1) * _LANES]
        hi = bits[:, (2 * c + 1) * _LANES:(2 * c + 2) * _LANES]
        hf_ref[pl.ds(c, x2.shape[0], stride=_SUBLANES), :] = (hi & jnp.uint32(0xFFFF0000)) | (lo >> 16)


def _mixout(x, dsa, ml, wo, wqk, wvo, wr, gx, gf, tm):
    T = x.shape[0]
    row = lambda n: pl.BlockSpec((tm, n), lambda i: (i, 0))
    return pl.pallas_call(
        _mixout_body,
        grid=(T // tm,),
        in_specs=[row(D_MODEL), row(dsa.shape[1]), row(ml.shape[1]), _resident(wo.shape),
                  _resident(wqk.shape), _resident(wvo.shape), _resident(wr.shape), _resident(gx.shape),
                  _resident(gf.shape)],
        out_specs=[row(D_MODEL), pl.BlockSpec((tm * _SUBLANES, _LANES), lambda i: (i, 0)), row(wr.shape[1])],
        out_shape=[jax.ShapeDtypeStruct((T, D_MODEL), F32),
                   jax.ShapeDtypeStruct((T * _SUBLANES, _LANES), U32),
                   jax.ShapeDtypeStruct((T, wr.shape[1]), F32)],
        scratch_shapes=[pltpu.VMEM((tm, wqk.shape[1]), BF16)],
        compiler_params=_params(),
        name="mixout",
    )(x, dsa, ml, wo, wqk, wvo, wr, gx, gf)


def _moe_body(row_ref, gap_lo_ref, gap_hi_ref, eb_ref, hf_hbm, wg_ref, wu_ref, wd_ref, y_hbm,
              tok_ref, xbuf, ybuf, wgb, wub, wdb, gsem, ysem, *, bm, n_blk):
    e = pl.program_id(0)

    @pl.when(e == 0)
    def _():
        def clear_gap(g, carry):
            def clear(r, c):
                tok_ref[r] = 0
                return c
            return lax.fori_loop(gap_lo_ref[g], gap_hi_ref[g], clear, carry)

        def place(a, carry):
            tok_ref[row_ref[a]] = lax.shift_right_logical(a, _TOPK_SHIFT)
            return carry

        lax.fori_loop(0, gap_lo_ref.shape[0], clear_gap, 0)
        lax.fori_loop(0, row_ref.shape[0], place, 0, unroll=8)

    n_valid = eb_ref[N_EXPERTS]
    b_lo = eb_ref[e]
    b_hi = eb_ref[e + 1]

    def tok_words(tok):
        return hf_hbm.at[pl.ds(pl.multiple_of(tok * _SUBLANES, _SUBLANES), _SUBLANES)]

    def gather(blk, s):
        def issue(g, carry):
            for u in range(_SUBLANES):
                r = g * _SUBLANES + u
                pltpu.make_async_copy(tok_words(tok_ref[blk * bm + r]),
                                      xbuf.at[s, pl.ds(pl.multiple_of(r * _SUBLANES, _SUBLANES), _SUBLANES)],
                                      gsem.at[s]).start(priority=1)
            return carry
        lax.fori_loop(0, bm // _SUBLANES, issue, 0)

    def wait_gather(s):
        pltpu.make_async_copy(xbuf.at[s], xbuf.at[s], gsem.at[s]).wait()

    def y_copy(blk, s):
        return pltpu.make_async_copy(ybuf.at[s], y_hbm.at[pl.ds(pl.multiple_of(blk * bm, bm), bm)], ysem.at[s])

    @pl.when(e == 0)
    def _():
        for k in range(_GATHER_AHEAD):
            gather(k, k)

    @pl.when(b_hi > b_lo)
    def _():
        wgb[...] = wg_ref[0].astype(BF16)
        wub[...] = wu_ref[0].astype(BF16)
        wdb[...] = wd_ref[0].astype(BF16)

        def block(b, carry):
            s = b % _GATHER_SLOTS
            wait_gather(s)
            chunks = []
            for c in range(_SUBLANES):
                w = xbuf[s, pl.ds(c, bm, stride=_SUBLANES), :]
                chunks.append(lax.bitcast_convert_type(w << 16, F32).astype(BF16))
                chunks.append(lax.bitcast_convert_type(w & jnp.uint32(0xFFFF0000), F32).astype(BF16))
            xb = jnp.concatenate(chunks, axis=-1)
            gate = _dot(xb, wgb[...])
            a = gate * jax.nn.sigmoid(gate) * _dot(xb, wub[...])
            y = _dot(a.astype(BF16), wdb[...])

            nxt = jnp.minimum(b + _GATHER_AHEAD, n_blk - 1)
            for r in range(bm):
                pltpu.make_async_copy(tok_words(tok_ref[nxt * bm + r]),
                                      xbuf.at[(b + _GATHER_AHEAD) % _GATHER_SLOTS, pl.ds(r * _SUBLANES, _SUBLANES)],
                                      gsem.at[(b + _GATHER_AHEAD) % _GATHER_SLOTS]).start(priority=1)

            @pl.when(b >= 2)
            def _():
                y_copy(b - 2, b % 2).wait()

            ybuf[b % 2] = y
            y_copy(b, b % 2).start()
            return carry

        lax.fori_loop(b_lo, b_hi, block, 0)

    @pl.when(e == pl.num_programs(0) - 1)
    def _():
        for k in range(_GATHER_AHEAD):
            wait_gather((n_valid + k) % _GATHER_SLOTS)

        @pl.when(n_valid >= 2)
        def _():
            y_copy(n_valid - 2, n_valid % 2).wait()
        y_copy(n_valid - 1, (n_valid - 1) % 2).wait()
        ybuf[0] = jnp.zeros(ybuf.shape[1:], F32)

        def zero_block(b, carry):
            cp = y_copy(b, 0)
            cp.start()
            cp.wait()
            return carry

        lax.fori_loop(n_valid, n_blk, zero_block, 0)


def _moe(row, gap_lo, gap_hi, e_blk, hf, wg, wu, wd, n_blk, bm):
    D = wg.shape[1]
    wspec = lambda shape: pl.BlockSpec((1,) + shape, lambda e, *_: (e, 0, 0))
    grid_spec = pltpu.PrefetchScalarGridSpec(
        num_scalar_prefetch=4,
        grid=(N_EXPERTS,),
        in_specs=[pl.BlockSpec(memory_space=pl.ANY),
                  wspec((D, D_EXPERT)), wspec((D, D_EXPERT)), wspec((D_EXPERT, D))],
        out_specs=pl.BlockSpec(memory_space=pl.ANY),
        scratch_shapes=[pltpu.SMEM((n_blk * bm,), I32),
                        pltpu.VMEM((_GATHER_SLOTS, bm * _SUBLANES, _LANES), U32),
                        pltpu.VMEM((2, bm, D), F32),
                        pltpu.VMEM((D, D_EXPERT), BF16), pltpu.VMEM((D, D_EXPERT), BF16),
                        pltpu.VMEM((D_EXPERT, D), BF16),
                        pltpu.SemaphoreType.DMA((_GATHER_SLOTS,)), pltpu.SemaphoreType.DMA((2,))],
    )
    return pl.pallas_call(
        functools.partial(_moe_body, bm=bm, n_blk=n_blk),
        grid_spec=grid_spec,
        out_shape=jax.ShapeDtypeStruct((n_blk * bm, D), F32),
        compiler_params=_params(),
        name="moe",
    )(row, gap_lo, gap_hi, e_blk, hf, wg, wu, wd)


def _route_body(rl_ref, bias_ref, out_ref, cnt_ref, seen_ref, *, tm):
    @pl.when(pl.program_id(0) == 0)
    def _():
        seen_ref[...] = jnp.zeros(seen_ref.shape, F32)

    lg = rl_ref[...] + bias_ref[...]
    lane = lax.broadcasted_iota(I32, lg.shape, 1).astype(F32)
    first = lambda hit: jnp.min(jnp.where(hit, lane, float(_LANES)), axis=-1, keepdims=True)
    is_group = lane < N_GROUPS
    gl = jnp.where(is_group, lg, -jnp.inf)
    g_max = jnp.max(gl, axis=-1, keepdims=True)
    g_sel = first(gl == g_max)
    p_g = 1.0 / jnp.sum(jnp.where(is_group, jnp.exp(gl - g_max), 0.0), axis=-1, keepdims=True)

    e_id = lane - N_GROUPS
    in_group = (e_id >= 0) & (e_id < N_EXPERTS) & (jnp.floor(e_id / EXP_PER_GROUP) == g_sel)
    el = jnp.where(in_group, lg, -jnp.inf)
    ee = jnp.where(in_group, jnp.exp(el - jnp.max(el, axis=-1, keepdims=True)), 0.0)
    cand = jnp.where(in_group, ee / jnp.sum(ee, axis=-1, keepdims=True), -1.0)
    p1 = jnp.max(cand, axis=-1, keepdims=True)
    l1 = first(cand == p1)
    cand = jnp.where(lane == l1, -2.0, cand)
    p2 = jnp.max(cand, axis=-1, keepdims=True)
    l2 = first(cand == p2)
    g1 = p_g * p1 / (p1 + p2)
    g2 = p_g * p2 / (p1 + p2)

    oh1 = (lane == l1).astype(BF16)
    oh2 = (lane == l2).astype(BF16)
    both = oh1 + oh2
    earlier = (lax.broadcasted_iota(I32, (tm, tm), 1) < lax.broadcasted_iota(I32, (tm, tm), 0)).astype(BF16)
    before = _dot(earlier, both) + seen_ref[...]
    r1 = jnp.sum(before * oh1.astype(F32), axis=-1, keepdims=True)
    r2 = jnp.sum(before * oh2.astype(F32), axis=-1, keepdims=True)
    seen_ref[...] = seen_ref[...] + jnp.sum(both.astype(F32), axis=0, keepdims=True)
    cnt_ref[...] = seen_ref[...]

    cols = (l1 - N_GROUPS, l2 - N_GROUPS, r1, r2, g1, g2)
    out = jnp.zeros(lg.shape, F32)
    for c, v in enumerate(cols):
        out = jnp.where(lane == c, v, out)
    out_ref[...] = out


def _route_tokens(rl, bias, tm):
    T = rl.shape[0]
    return pl.pallas_call(
        functools.partial(_route_body, tm=tm),
        grid=(T // tm,),
        in_specs=[pl.BlockSpec((tm, _LANES), lambda i: (i, 0)), _resident(bias.shape)],
        out_specs=[pl.BlockSpec((tm, _LANES), lambda i: (i, 0)), pl.BlockSpec((1, _LANES), lambda i: (0, 0))],
        out_shape=[jax.ShapeDtypeStruct((T, _LANES), F32), jax.ShapeDtypeStruct((1, _LANES), F32)],
        scratch_shapes=[pltpu.VMEM((1, _LANES), F32)],
        compiler_params=_params(),
        name="route",
    )(rl, bias)


def _route(rl, b_group, b_router, bm, tm):
    N = rl.shape[0]
    bias = jnp.concatenate([b_group, b_router, jnp.zeros((_LANES - N_GROUPS - N_EXPERTS,), F32)]).reshape(1, _LANES)
    per_tok, seen = _route_tokens(rl, bias, tm)
    expert_id = per_tok[:, 0:TOPK_IN_GROUP].astype(I32)
    rank = per_tok[:, TOPK_IN_GROUP:2 * TOPK_IN_GROUP].astype(I32)
    gates = per_tok[:, 2 * TOPK_IN_GROUP:3 * TOPK_IN_GROUP]
    counts = seen[0, N_GROUPS:N_GROUPS + N_EXPERTS].astype(I32)

    A = N * TOPK_IN_GROUP
    padded = (counts + bm - 1) // bm * bm
    pad_ends = jnp.cumsum(padded)
    pad_starts = pad_ends - padded
    onehot = expert_id[..., None] == jnp.arange(N_EXPERTS, dtype=I32)
    row = (jnp.sum(jnp.where(onehot, pad_starts, 0), axis=-1) + rank).reshape(A)
    n_blk = -(-A // bm) + N_EXPERTS
    gap_lo = jnp.concatenate([pad_starts + counts, pad_ends[-1:]]).astype(I32)
    gap_hi = jnp.concatenate([pad_ends, jnp.full((1,), n_blk * bm, I32)]).astype(I32)
    e_blk = jnp.concatenate([pad_starts, pad_ends[-1:]]) // bm
    return row, gap_lo, gap_hi, gates, e_blk.astype(I32), n_blk


def _final_body(row_ref, x_ref, gate_ref, y_hbm, g_ref, out_ref, ybuf, sem, *, tm):
    i = pl.program_id(0)
    n_tiles = pl.num_programs(0)
    slot = i % _FINAL_SLOTS

    def gather(tile, s):
        def issue(g, carry):
            for u in range(_SUBLANES):
                for k in range(TOPK_IN_GROUP):
                    src = row_ref[(tile * tm + g * _SUBLANES + u) * TOPK_IN_GROUP + k]
                    pltpu.make_async_copy(y_hbm.at[pl.ds(src, 1)], ybuf.at[s, k, g, pl.ds(u, 1)],
                                          sem.at[s]).start()
            return carry
        lax.fori_loop(0, tm // _SUBLANES, issue, 0)

    def wait_rows(s):
        pltpu.make_async_copy(ybuf.at[s], ybuf.at[s], sem.at[s]).wait()

    @pl.when(i == 0)
    def _():
        for t in range(_FINAL_SLOTS - 1):
            gather(jnp.minimum(t, n_tiles - 1), t)

    wait_rows(slot)
    acc = x_ref[...]
    for k in range(TOPK_IN_GROUP):
        acc = acc + gate_ref[:, k:k + 1] * ybuf[slot, k].reshape(tm, ybuf.shape[-1])
    y = _rms(acc, g_ref[...])

    ahead = _FINAL_SLOTS - 1
    nxt = jnp.minimum(i + ahead, n_tiles - 1)
    for r in range(tm):
        for k in range(TOPK_IN_GROUP):
            src = row_ref[(nxt * tm + r) * TOPK_IN_GROUP + k]
            pltpu.make_async_copy(y_hbm.at[pl.ds(src, 1)],
                                  ybuf.at[(i + ahead) % _FINAL_SLOTS, k, r // _SUBLANES, pl.ds(r % _SUBLANES, 1)],
                                  sem.at[(i + ahead) % _FINAL_SLOTS]).start()
    out_ref[...] = y

    @pl.when(i == n_tiles - 1)
    def _():
        for t in range(1, _FINAL_SLOTS):
            wait_rows((i + t) % _FINAL_SLOTS)


def _final(row, x2, gates, y_rows, g, tm):
    T, D = x2.shape
    grid_spec = pltpu.PrefetchScalarGridSpec(
        num_scalar_prefetch=1,
        grid=(T // tm,),
        in_specs=[pl.BlockSpec((tm, D), lambda i, *_: (i, 0)),
                  pl.BlockSpec((tm, TOPK_IN_GROUP), lambda i, *_: (i, 0)),
                  pl.BlockSpec(memory_space=pl.ANY),
                  pl.BlockSpec(g.shape, lambda i, *_: (0, 0))],
        out_specs=pl.BlockSpec((tm, D), lambda i, *_: (i, 0)),
        scratch_shapes=[pltpu.VMEM((_FINAL_SLOTS, TOPK_IN_GROUP, tm // _SUBLANES, _SUBLANES, D), F32),
                        pltpu.SemaphoreType.DMA((_FINAL_SLOTS,))],
    )
    return pl.pallas_call(
        functools.partial(_final_body, tm=tm),
        grid_spec=grid_spec,
        out_shape=jax.ShapeDtypeStruct((T, D), F32),
        compiler_params=_params(),
        name="final",
    )(row, x2, gates, y_rows, g)


def _tile_sizes(T):
    pick = lambda want: want if T % want == 0 else CHUNK
    return dict(inproj=pick(256), dsa_q=pick(256), dsa_k=pick(512), mlstm=pick(256), mixout=pick(256),
                final=pick(256), route=pick(512), moe=128)


def _layer(x, mem, norm_mix_g, w_in, kv_norm_g, k_idx_norm_g, w_uk, w_uv, conv_w, conv_b, gate_b, ml_norm_g,
           w_out, norm_x_g, mem_norm_g, w_cq, w_ckv, w_co, norm_ffn_g, w_group, b_group, w_router, b_router,
           w_gate, w_up, w_down, out_g):
    T = x.shape[0]
    ts = _tile_sizes(T)
    r2 = lambda v: v.reshape(1, -1)

    w_r = _wprep(w_in, 256)
    wuk_t = jnp.transpose(w_uk, (1, 2, 0)).astype(BF16)
    wuv_t = jnp.transpose(w_uv, (1, 0, 2)).astype(BF16)

    qabs, ckv, qi, kidx, small, mqk, mv, mo = _inproj(
        x, r2(norm_mix_g), w_r, wuk_t, r2(kv_norm_g), r2(k_idx_norm_g), ts["inproj"])

    gate_rows = jnp.transpose(small[:, _S_WI:_S_MF + ML_HEADS])
    dsa_out = _dsa(qi, gate_rows, qabs, kidx, ckv, wuv_t, ts["dsa_q"], ts["dsa_k"])

    gb_col = jnp.zeros((1, _SMALL), F32).at[0, _S_MI:_S_MI + 2 * ML_HEADS].set(gate_b)
    ml_out = _mlstm(mqk, mv, small, gate_rows, mo, conv_w, r2(conv_b), gb_col, gate_b.reshape(-1, 1),
                    r2(ml_norm_g), ts["mlstm"])

    wqk, wvo = _memfold(mem, r2(mem_norm_g), w_ckv, w_cq, w_co)
    w_rt = jnp.concatenate([w_group, w_router,
                            jnp.zeros((D_MODEL, 128 - N_GROUPS - N_EXPERTS), w_group.dtype)], axis=1)
    x2, hf, rl = _mixout(x, dsa_out, ml_out, w_out.astype(BF16), wqk, wvo, w_rt.astype(BF16),
                         r2(norm_x_g), r2(norm_ffn_g), ts["mixout"])

    bm = ts["moe"]
    row, gap_lo, gap_hi, gates, e_blk, n_blk = _route(rl, b_group, b_router, bm, ts["route"])
    y_rows = _moe(row, gap_lo, gap_hi, e_blk, hf, w_gate, w_up, w_down, n_blk, bm)
    return _final(row, x2, gates, y_rows, r2(out_g), ts["final"])


def kernel(x, mem, norm_mix_g, w_in, kv_norm_g, k_idx_norm_g, w_uk, w_uv, conv_w, conv_b, gate_b, ml_norm_g,
           w_out, norm_x_g, mem_norm_g, w_cq, w_ckv, w_co, norm_ffn_g, w_group, b_group, w_router, b_router,
           w_gate, w_up, w_down, final_norm_g):
    B, T, D = x.shape
    assert B == 1 and D == D_MODEL and norm_mix_g.shape[0] == 1 and T % CHUNK == 0
    out = _layer(x[0], mem[0], norm_mix_g[0], w_in, kv_norm_g[0], k_idx_norm_g[0], w_uk[0], w_uv[0],
                 conv_w[0], conv_b[0], gate_b[0], ml_norm_g[0], w_out[0], norm_x_g[0], mem_norm_g[0],
                 w_cq[0], w_ckv[0], w_co[0], norm_ffn_g[0], w_group[0], b_group[0], w_router[0], b_router[0],
                 w_gate[0], w_up[0], w_down[0], final_norm_g)
    return out[None]
```

```python
import functools

import jax
import jax.numpy as jnp
import numpy as np
from jax import lax
from jax.experimental import pallas as pl
from jax.experimental.pallas import tpu as pltpu

F32 = jnp.float32
BF16 = jnp.bfloat16
I32 = jnp.int32
I16 = jnp.int16
U32 = jnp.uint32

EPS = 1e-6
CHUNK = 64
D_MODEL = 2048

DSA_HEADS = 8
DSA_HEAD_DIM = 128
DSA_LATENT = 256
IDX_HEADS = 8
IDX_DIM = 64
TOPK_MAX = 256

ML_HEADS = 4
ML_QK_DIM = 128
ML_V_DIM = 256
CONV_W = 4

X_HEADS = 4
X_HEAD_DIM = D_MODEL // X_HEADS

N_GROUPS = 4
EXP_PER_GROUP = 8
N_EXPERTS = N_GROUPS * EXP_PER_GROUP
TOPK_IN_GROUP = 2
D_EXPERT = 512

_O_DQ = 0
_O_CKV = _O_DQ + DSA_HEADS * DSA_HEAD_DIM
_O_QI = _O_CKV + DSA_LATENT
_O_KI = _O_QI + IDX_HEADS * IDX_DIM
_O_WI = _O_KI + IDX_DIM
_O_MQ = _O_WI + IDX_HEADS
_O_MK = _O_MQ + ML_HEADS * ML_QK_DIM
_O_MV = _O_MK + ML_HEADS * ML_QK_DIM
_O_MI = _O_MV + ML_HEADS * ML_V_DIM
_O_MF = _O_MI + ML_HEADS
_O_MO = _O_MF + ML_HEADS
_O_END = _O_MO + ML_HEADS * ML_V_DIM

_G_DQ = (0, 1024)
_G_CKV = (1024, 1280)
_G_QI = (1280, 1792)
_G_SMALL = (1792, 1920)
_G_MQK = (1920, 2944)
_G_MV = (2944, 3968)
_G_MO = (3968, 4992)
_W_COLS = 4992
_S_WI = IDX_DIM
_S_MI = _S_WI + IDX_HEADS
_S_MF = _S_MI + ML_HEADS
_SMALL = 128

_VMEM_LIMIT = 56 * 1024 * 1024
_INT_MIN = -(2 ** 31)
_I16_MIN = -(2 ** 15)
_CHUNK_SHIFT = CHUNK.bit_length() - 1
_LOG2E = 1.4426950408889634
_SUBLANES = 8
_LANES = 128
_GATHER_SLOTS = 9
_GATHER_AHEAD = _GATHER_SLOTS - 1
_TOPK_SHIFT = TOPK_IN_GROUP.bit_length() - 1
_FINAL_SLOTS = 3
_NEG = -1e30


def _rms(v, g):
    return v * lax.rsqrt(jnp.mean(v * v, axis=-1, keepdims=True) + EPS) * g


def _dot(a, b):
    return jnp.dot(a, b, preferred_element_type=F32)


def _dot_nt(a, b):
    return lax.dot_general(a, b, (((1,), (1,)), ((), ())), preferred_element_type=F32)


def _resident(shape):
    nd = len(shape)
    return pl.BlockSpec(shape, lambda *_: (0,) * nd, pipeline_mode=pl.Buffered(1))


def _params(n_axes=1):
    return pltpu.CompilerParams(dimension_semantics=("arbitrary",) * n_axes,
                                vmem_limit_bytes=_VMEM_LIMIT)


def _wprep_body(w_hbm, out_ref, wbuf, sem, *, tk):
    i = pl.program_id(0)
    slot = i % 2

    def rows(blk, s):
        return pltpu.make_async_copy(w_hbm.at[0, pl.ds(pl.multiple_of(blk * tk, tk), tk)], wbuf.at[s], sem.at[s])

    @pl.when(i == 0)
    def _():
        rows(0, 0).start()

    @pl.when(i + 1 < pl.num_programs(0))
    def _():
        rows(i + 1, 1 - slot).start()

    rows(i, slot).wait()
    w_ref = wbuf.at[slot]
    n_small = IDX_DIM + IDX_HEADS
    out_ref[:, _G_DQ[0]:_G_SMALL[0] + n_small] = w_ref[:, _O_DQ:_O_MQ].astype(BF16)
    out_ref[:, _G_SMALL[0] + n_small:_G_SMALL[0] + n_small + 2 * ML_HEADS] = w_ref[:, _O_MI:_O_MO].astype(BF16)
    out_ref[:, _G_SMALL[0] + n_small + 2 * ML_HEADS:_G_SMALL[1]] = jnp.zeros(
        (out_ref.shape[0], _SMALL - n_small - 2 * ML_HEADS), BF16)
    out_ref[:, _G_MQK[0]:_G_MV[1]] = w_ref[:, _O_MQ:_O_MI].astype(BF16)
    out_ref[:, _G_MO[0]:_G_MO[1]] = w_ref[:, _O_MO:_O_END].astype(BF16)


def _wprep(w_in, tk):
    K = w_in.shape[1]
    return pl.pallas_call(
        functools.partial(_wprep_body, tk=tk),
        grid=(K // tk,),
        in_specs=[pl.BlockSpec(memory_space=pl.ANY)],
        out_specs=pl.BlockSpec((tk, _W_COLS), lambda i: (i, 0)),
        out_shape=jax.ShapeDtypeStruct((K, _W_COLS), BF16),
        scratch_shapes=[pltpu.VMEM((2, tk, w_in.shape[2]), F32), pltpu.SemaphoreType.DMA((2,))],
        compiler_params=_params(),
        name="wprep",
    )(w_in)


def _inproj_body(x_ref, g_ref, w_ref, wuk_ref, kvg_ref, kig_ref,
                 qabs_ref, ckv_ref, qi_ref, kidx_ref, small_ref, mqk_ref, mv_ref, mo_ref):
    h = _rms(x_ref[...], g_ref[...]).astype(BF16)

    def proj(grp):
        return _dot(h, w_ref[:, grp[0]:grp[1]])

    dq = proj(_G_DQ)
    for hd in range(DSA_HEADS):
        qh = dq[:, hd * DSA_HEAD_DIM:(hd + 1) * DSA_HEAD_DIM].astype(BF16)
        qa = _dot(qh, wuk_ref[hd]) * (DSA_HEAD_DIM ** -0.5 * _LOG2E)
        qabs_ref[:, hd * DSA_LATENT:(hd + 1) * DSA_LATENT] = qa.astype(BF16)
    ckv_ref[...] = _rms(proj(_G_CKV), kvg_ref[...]).astype(BF16)
    qi_ref[...] = (proj(_G_QI) * (IDX_DIM ** -0.5)).astype(BF16)
    small = proj(_G_SMALL)
    small_ref[...] = small
    kidx_ref[...] = _rms(small[:, :IDX_DIM], kig_ref[...]).astype(BF16)
    mqk_ref[...] = proj(_G_MQK)
    mv_ref[...] = proj(_G_MV).astype(BF16)
    mo_ref[...] = proj(_G_MO)


def _inproj(x, g, w, wuk, kvg, kig, tm):
    T = x.shape[0]
    row = lambda n: pl.BlockSpec((tm, n), lambda i: (i, 0))
    outs = [(8 * DSA_LATENT, BF16), (DSA_LATENT, BF16), (IDX_HEADS * IDX_DIM, BF16), (IDX_DIM, BF16),
            (_SMALL, F32), (2 * ML_HEADS * ML_QK_DIM, F32), (ML_HEADS * ML_V_DIM, BF16),
            (ML_HEADS * ML_V_DIM, F32)]
    return pl.pallas_call(
        _inproj_body,
        grid=(T // tm,),
        in_specs=[row(D_MODEL), _resident(g.shape), _resident(w.shape), _resident(wuk.shape),
                  _resident(kvg.shape), _resident(kig.shape)],
        out_specs=[row(n) for n, _ in outs],
        out_shape=[jax.ShapeDtypeStruct((T, n), dt) for n, dt in outs],
        compiler_params=_params(),
        name="inproj",
    )(x, g, w, wuk, kvg, kig)


def _sublane_fold(v, op, rows=_SUBLANES, ways=4):
    groups = [v[r * rows:(r + 1) * rows, :] for r in range(v.shape[0] // rows)]
    accs = groups[:ways]
    for r in range(ways, len(groups)):
        accs[r % ways] = op(accs[r % ways], groups[r])
    while len(accs) > 1:
        accs = [op(accs[k], accs[k + 1]) if k + 1 < len(accs) else accs[k] for k in range(0, len(accs), 2)]
    return accs[0]


def _bit_transpose32(words):
    a = list(words)
    j, m = 16, 0x0000FFFF
    while j:
        k = 0
        while k < 32:
            t = (a[k] ^ (a[k + j] >> j)) & jnp.uint32(m)
            a[k] = a[k] ^ t
            a[k + j] = a[k + j] ^ (t << j)
            k = (k + j + 1) & ~j
        j >>= 1
        m = (m ^ (m << j)) & 0xFFFFFFFF
    return a


def _dsa_body(qi_ref, wrow_ref, qabs_ref, kidx_ref, ckv_ref, ckvt_ref, wuv_ref, out_ref,
              key_ref, planes_ref, alive_ref, lga_ref, lgb_ref, m_ref, l_ref, acc_ref, *, tq, tk, topk, nbits_idx):
    i = pl.program_id(0)
    n_kb = ((i + 1) * tq + tk - 1) // tk
    w_rows = wrow_ref[0:IDX_HEADS, :] * (IDX_HEADS ** -0.5)
    q_chunk = (i * tq + lax.broadcasted_iota(I32, (1, tq), 1)) >> _CHUNK_SHIFT

    def key_pos(j):
        return j * tk + lax.broadcasted_iota(I32, (tk, 1), 0)

    plane_rows = tk // 32

    @pl.when(i == 0)
    def _():
        planes_ref[...] = jnp.zeros(planes_ref.shape, U32)

    def score_block(j, carry):
        kx = kidx_ref[pl.ds(pl.multiple_of(j * tk, tk), tk), :]
        s = jnp.zeros((tk, tq), F32)
        for hd in range(IDX_HEADS):
            d = _dot_nt(kx, qi_ref[:, hd * IDX_DIM:(hd + 1) * IDX_DIM])
            s = s + w_rows[hd:hd + 1, :] * jnp.maximum(d, 0.0)
        bits = lax.bitcast_convert_type(s, I32)
        key = bits ^ ((bits >> 31) & 0x7FFFFFFF)
        key = jnp.where((key_pos(j) >> _CHUNK_SHIFT) <= q_chunk, key, _INT_MIN)
        key_ref[j] = key
        u = lax.bitcast_convert_type(key, U32) ^ jnp.uint32(0x80000000)
        for h in range(plane_rows // _SUBLANES):
            base = h * 32 * _SUBLANES
            planes = _bit_transpose32([u[base + _SUBLANES * v:base + _SUBLANES * (v + 1), :] for v in range(32)])
            row0 = pl.multiple_of(j * plane_rows + h * _SUBLANES, _SUBLANES)
            for b in range(32):
                planes_ref[b, pl.ds(row0, _SUBLANES), :] = planes[31 - b]
        return carry

    lax.fori_loop(0, n_kb, score_block, 0)

    def count(pred):
        def body(j, acc):
            hit = pred(key_ref[j], key_pos(j)).astype(I32)
            return acc + _sublane_fold(hit, jnp.add)
        acc = lax.fori_loop(0, n_kb, body, jnp.zeros((8, tq), I32))
        return jnp.sum(acc, axis=0, keepdims=True)

    word_row = lax.broadcasted_iota(I32, (alive_ref.shape[0], 1), 0)
    alive_ref[...] = jnp.where(word_row < n_kb * plane_rows, jnp.full(alive_ref.shape, 0xFFFFFFFF, U32),
                               jnp.uint32(0))

    def radix_select(n_rows):
        def ones_count(words):
            return jnp.sum(_sublane_fold(lax.population_count(words).astype(I32), jnp.add), axis=0, keepdims=True)

        def select_bit(it, carry):
            need, t_u = carry
            b = 31 - it
            alive = alive_ref[0:n_rows, :]
            ones = alive & planes_ref[b, pl.ds(0, n_rows), :]
            n_ones = ones_count(ones)
            take = n_ones >= need
            alive_ref[0:n_rows, :] = jnp.where(take, ones, alive ^ ones)
            bit = lax.shift_left(jnp.uint32(1), jnp.asarray(b, U32))
            return jnp.where(take, need, need - n_ones), jnp.where(take, t_u | bit, t_u)

        need, t_u = lax.fori_loop(0, 32, select_bit, (jnp.full((1, tq), topk, I32), jnp.zeros((1, tq), U32)))
        return need, t_u, ones_count(alive_ref[0:n_rows, :])

    all_rows = alive_ref.shape[0]
    if all_rows % (2 * _SUBLANES) == 0:
        need, t_u, n_equal = lax.cond(n_kb * plane_rows <= all_rows // 2,
                                      lambda: radix_select(all_rows // 2), lambda: radix_select(all_rows))
    else:
        need, t_u, n_equal = radix_select(all_rows)
    short = t_u == 0
    t = jnp.maximum(lax.bitcast_convert_type(t_u ^ jnp.uint32(0x80000000), I32), _INT_MIN + 1)
    all_pos = jnp.int32(2 ** nbits_idx - 1)
    has_ties = (n_equal > need) & jnp.logical_not(short)
    n_tie_take = jnp.where(has_ties, need, all_pos)

    def tie_cutoff():
        def pos_bit(b, c):
            cand = c + lax.shift_left(jnp.int32(1), nbits_idx - 1 - b)
            f = count(lambda kb, pos: (kb == t) & (pos < cand))
            return jnp.where(f <= n_tie_take, cand, c)
        return lax.fori_loop(0, nbits_idx, pos_bit, jnp.zeros((1, tq), I32))

    cut = lax.cond(jnp.max(has_ties.astype(I32)) > 0, tie_cutoff, lambda: jnp.full((1, tq), all_pos, I32))

    m_ref[...] = jnp.full(m_ref.shape, _NEG, F32)
    l_ref[...] = jnp.zeros(l_ref.shape, F32)
    acc_ref[...] = jnp.zeros(acc_ref.shape, F32)
    last_blk = ckvt_ref.shape[0] - 1

    def logits(j, lg_buf):
        c_blk = ckv_ref[pl.ds(pl.multiple_of(jnp.minimum(j, last_blk) * tk, tk), tk), :]
        kb = key_ref[j]
        sel = (kb > t) | ((kb == t) & (key_pos(j) < cut))
        bias = jnp.where(sel, 0.0, _NEG).astype(F32)
        for hd in range(DSA_HEADS):
            lg_buf[hd] = _dot_nt(c_blk, qabs_ref[:, hd * DSA_LATENT:(hd + 1) * DSA_LATENT]) + bias

    def accumulate(j, lg_buf):
        c_blk_t = ckvt_ref[jnp.minimum(j, last_blk)]
        for hd in range(DSA_HEADS):
            lg = lg_buf[hd]
            m_old = m_ref[hd:hd + 1, :]
            m_new = jnp.maximum(m_old, jnp.max(_sublane_fold(lg, jnp.maximum, ways=1), axis=0, keepdims=True))
            p = jnp.exp2(lg - m_new)
            alpha = jnp.exp2(m_old - m_new)
            l_ref[hd:hd + 1, :] = alpha * l_ref[hd:hd + 1, :] + jnp.sum(_sublane_fold(p, jnp.add, ways=1), axis=0,
                                                                         keepdims=True)
            acc_ref[hd] = alpha * acc_ref[hd] + _dot(c_blk_t, p.astype(BF16))
            m_ref[hd:hd + 1, :] = m_new

    logits(0, lga_ref)

    def attn_pair(mi, carry):
        ja = 2 * mi
        accumulate(ja, lga_ref)
        logits(ja + 1, lgb_ref)
        accumulate(ja + 1, lgb_ref)
        logits(jnp.minimum(ja + 2, n_kb - 1), lga_ref)
        return carry

    lax.fori_loop(0, n_kb // 2, attn_pair, 0)

    @pl.when(n_kb % 2 == 1)
    def _():
        accumulate(n_kb - 1, lga_ref)

    for hd in range(DSA_HEADS):
        o_lat = (acc_ref[hd] / l_ref[hd:hd + 1, :]).T.astype(BF16)
        out_ref[:, hd * DSA_HEAD_DIM:(hd + 1) * DSA_HEAD_DIM] = _dot(o_lat, wuv_ref[hd]).astype(BF16)


def _dsa(qi, wrows, qabs, kidx, ckv, wuv, tq, tk):
    T = qi.shape[0]
    topk = min(TOPK_MAX, T // 4)
    n_kb = T // tk
    ckvt = jnp.transpose(ckv.reshape(n_kb, tk, DSA_LATENT), (0, 2, 1))
    row = lambda n: pl.BlockSpec((tq, n), lambda i: (i, 0))
    body = functools.partial(_dsa_body, tq=tq, tk=tk, topk=topk, nbits_idx=int(T).bit_length())
    return pl.pallas_call(
        body,
        grid=(T // tq,),
        in_specs=[row(qi.shape[1]), pl.BlockSpec((wrows.shape[0], tq), lambda i: (0, i)), row(qabs.shape[1]),
                  _resident(kidx.shape), _resident(ckv.shape), _resident(ckvt.shape), _resident(wuv.shape)],
        out_specs=row(DSA_HEADS * DSA_HEAD_DIM),
        out_shape=jax.ShapeDtypeStruct((T, DSA_HEADS * DSA_HEAD_DIM), BF16),
        scratch_shapes=[pltpu.VMEM((n_kb + n_kb % 2, tk, tq), I32),
                        pltpu.VMEM((32, T // 32, tq), U32), pltpu.VMEM((T // 32, tq), U32),
                        pltpu.VMEM((DSA_HEADS, tk, tq), F32), pltpu.VMEM((DSA_HEADS, tk, tq), F32),
                        pltpu.VMEM((DSA_HEADS, tq), F32),
                        pltpu.VMEM((DSA_HEADS, tq), F32), pltpu.VMEM((DSA_HEADS, DSA_LATENT, tq), F32)],
        compiler_params=_params(),
        name="dsa",
    )(qi, wrows, qabs, kidx, ckv, ckvt, wuv)


def _log_sigmoid(v):
    return jnp.minimum(v, 0.0) - jnp.log1p(jnp.exp(-jnp.abs(v)))


def _chunk_cumsum(v, axis):
    pos = lax.broadcasted_iota(I32, v.shape, axis) & (CHUNK - 1)
    d = 1
    while d < CHUNK:
        v = v + jnp.where(pos >= d, pltpu.roll(v, d, axis=axis), 0.0)
        d *= 2
    return v


def _mlstm_body(mqk_ref, mv_ref, small_ref, gt_ref, mo_ref, cw_ref, cb_ref, gbc_ref, gbr_ref, ng_ref,
                out_ref, xe_ref, c_ref, n_ref, m_ref, hs_ref, *, rows):
    @pl.when(pl.program_id(0) == 0)
    def _():
        xe_ref[0:8, :] = jnp.zeros((8, xe_ref.shape[1]), F32)
        c_ref[...] = jnp.zeros(c_ref.shape, F32)
        n_ref[...] = jnp.zeros(n_ref.shape, F32)
        m_ref[...] = jnp.zeros(m_ref.shape, F32)

    x = mqk_ref[...]
    xe_ref[8:8 + rows, :] = x
    y = cb_ref[...]
    for j in range(CONV_W - 1):
        y = y + xe_ref[5 + j:5 + j + rows, :] * cw_ref[j:j + 1, :]
    y = y + x * cw_ref[CONV_W - 1:CONV_W, :]
    xe_ref[0:8, :] = x[rows - 8:rows, :]
    qk = y * jax.nn.sigmoid(y)
    nqk = ML_HEADS * ML_QK_DIM
    q_all = (qk[:, :nqk] * (ML_QK_DIM ** -0.5)).astype(BF16)
    k_all = qk[:, nqk:]

    g_col = small_ref[...] + gbc_ref[...]
    g_row = gt_ref[...] + gbr_ref[...]
    b_col = _chunk_cumsum(_log_sigmoid(g_col), 0)
    b_row = _chunk_cumsum(_log_sigmoid(g_row), 1)

    tri = lax.broadcasted_iota(I32, (CHUNK, CHUNK), 1) <= lax.broadcasted_iota(I32, (CHUNK, CHUNK), 0)

    heads = range(ML_HEADS)
    for c in range(rows // CHUNK):
        lo, hi = c * CHUNK, (c + 1) * CHUNK
        bc = [b_col[lo:hi, _S_MF + hd:_S_MF + hd + 1] for hd in heads]
        lic = [g_col[lo:hi, _S_MI + hd:_S_MI + hd + 1] for hd in heads]
        br = [b_row[ML_HEADS + hd:ML_HEADS + hd + 1, lo:hi] for hd in heads]
        lir = [g_row[hd:hd + 1, lo:hi] for hd in heads]
        g_tot = [bc[hd][CHUNK - 1:CHUNK, :] for hd in heads]
        m_prev = [m_ref[hd][:, 0:1] for hd in heads]
        qh = [q_all[lo:hi, hd * ML_QK_DIM:(hd + 1) * ML_QK_DIM] for hd in heads]
        kh = [k_all[lo:hi, hd * ML_QK_DIM:(hd + 1) * ML_QK_DIM] for hd in heads]
        vh = [mv_ref[lo:hi, hd * ML_V_DIM:(hd + 1) * ML_V_DIM] for hd in heads]
        c_prev = [c_ref[hd] for hd in heads]
        n_prev = [n_ref[hd] for hd in heads]

        dmat = [jnp.where(tri, bc[hd] - br[hd] + lir[hd], -jnp.inf) for hd in heads]
        inter = [bc[hd] + m_prev[hd] for hd in heads]
        m_t = [jnp.maximum(inter[hd], jnp.max(dmat[hd], axis=-1, keepdims=True)) for hd in heads]
        w_intra = [jnp.exp(dmat[hd] - m_t[hd]) for hd in heads]
        a_inter = [jnp.exp(inter[hd] - m_t[hd]) for hd in heads]
        m_new = [jnp.maximum(g_tot[hd] + m_prev[hd],
                             jnp.max(g_tot[hd] - br[hd] + lir[hd], axis=-1, keepdims=True)) for hd in heads]
        a_state = [jnp.exp(g_tot[hd] + m_prev[hd] - m_new[hd]) for hd in heads]
        wk = [jnp.exp(g_tot[hd] - bc[hd] + lic[hd] - m_new[hd]) * kh[hd] for hd in heads]

        s_qk = [_dot_nt(qh[hd], kh[hd].astype(BF16)) * w_intra[hd] for hd in heads]
        read = [_dot(qh[hd], c_prev[hd].astype(BF16)) for hd in heads]
        d_c = [_dot(wk[hd].T.astype(BF16), vh[hd]) for hd in heads]
        num = [a_inter[hd] * read[hd] + _dot(s_qk[hd].astype(BF16), vh[hd]) for hd in heads]
        den = [a_inter[hd] * jnp.sum(qh[hd].astype(F32) * n_prev[hd], axis=-1, keepdims=True)
               + jnp.sum(s_qk[hd], axis=-1, keepdims=True) for hd in heads]
        for hd in heads:
            hs_ref[lo:hi, hd * ML_V_DIM:(hd + 1) * ML_V_DIM] = (
                num[hd] / jnp.maximum(jnp.abs(den[hd]), jnp.exp(-m_t[hd])))
        for hd in heads:
            c_ref[hd] = a_state[hd] * c_prev[hd] + d_c[hd]
            n_ref[hd] = a_state[hd] * n_prev[hd] + jnp.sum(wk[hd], axis=0, keepdims=True)
            m_ref[hd] = jnp.broadcast_to(m_new[hd], m_ref.shape[1:])

    for hd in range(ML_HEADS):
        sl = slice(hd * ML_V_DIM, (hd + 1) * ML_V_DIM)
        out_ref[:, sl] = (_rms(hs_ref[:, sl], ng_ref[:, sl]) * jax.nn.sigmoid(mo_ref[:, sl])).astype(BF16)


def _mlstm(mqk, mv, small, gt, mo, cw, cb, gbc, gbr, ng, rows):
    T = mqk.shape[0]
    row = lambda n: pl.BlockSpec((rows, n), lambda i: (i, 0))
    nv = ML_HEADS * ML_V_DIM
    return pl.pallas_call(
        functools.partial(_mlstm_body, rows=rows),
        grid=(T // rows,),
        in_specs=[row(mqk.shape[1]), row(nv), row(_SMALL), pl.BlockSpec((8, rows), lambda i: (1, i)), row(nv),
                  _resident(cw.shape), _resident(cb.shape), _resident(gbc.shape), _resident(gbr.shape),
                  _resident(ng.shape)],
        out_specs=row(nv),
        out_shape=jax.ShapeDtypeStruct((T, nv), BF16),
        scratch_shapes=[pltpu.VMEM((rows + 8, mqk.shape[1]), F32),
                        pltpu.VMEM((ML_HEADS, ML_QK_DIM, ML_V_DIM), F32),
                        pltpu.VMEM((ML_HEADS, 1, ML_QK_DIM), F32),
                        pltpu.VMEM((ML_HEADS, 1, 128), F32),
                        pltpu.VMEM((rows, nv), F32)],
        compiler_params=_params(),
        name="mlstm",
    )(mqk, mv, small, gt, mo, cw, cb, gbc, gbr, ng)


def _memfold_body(mem_ref, g_ref, wk_ref, wv_ref, wq_ref, wc_ref, wqk_ref, wvo_ref):
    mn = _rms(mem_ref[...], g_ref[...]).astype(BF16)
    k = _dot(mn, wk_ref[...].astype(BF16)).astype(BF16)
    v = _dot(mn, wv_ref[...].astype(BF16)).astype(BF16)
    wqk_ref[...] = (_dot_nt(wq_ref[...].astype(BF16), k) * (X_HEAD_DIM ** -0.5)).astype(BF16)
    wvo_ref[...] = _dot(v, wc_ref[...].astype(BF16)).astype(BF16)


def _memfold(mem, g, w_ckv, w_cq, w_co):
    M, D = mem.shape
    dh = X_HEAD_DIM
    return pl.pallas_call(
        _memfold_body,
        grid=(X_HEADS,),
        in_specs=[_resident(mem.shape), _resident(g.shape),
                  pl.BlockSpec((D, dh), lambda h: (0, h)), pl.BlockSpec((D, dh), lambda h: (0, X_HEADS + h)),
                  pl.BlockSpec((D, dh), lambda h: (0, h)), pl.BlockSpec((dh, D), lambda h: (h, 0))],
        out_specs=[pl.BlockSpec((D, M), lambda h: (0, h)), pl.BlockSpec((M, D), lambda h: (h, 0))],
        out_shape=[jax.ShapeDtypeStruct((D, X_HEADS * M), BF16), jax.ShapeDtypeStruct((X_HEADS * M, D), BF16)],
        compiler_params=_params(),
        name="memfold",
    )(mem, g, w_ckv, w_ckv, w_cq, w_co)


def _mixout_body(x_ref, dsa_ref, ml_ref, wo_ref, wqk_ref, wvo_ref, wr_ref, gx_ref, gf_ref,
                 x2_ref, hf_ref, rl_ref, p_ref):
    nd = dsa_ref.shape[1]
    x1 = x_ref[...] + _dot(dsa_ref[...], wo_ref[0:nd, :]) + _dot(ml_ref[...], wo_ref[nd:, :])
    lg_all = _dot(_rms(x1, gx_ref[...]).astype(BF16), wqk_ref[...])
    n_mem = wqk_ref.shape[1] // X_HEADS
    for hd in range(X_HEADS):
        sl = slice(hd * n_mem, (hd + 1) * n_mem)
        lg = lg_all[:, sl]
        e = jnp.exp(lg - jnp.max(lg, axis=-1, keepdims=True))
        p_ref[:, sl] = (e / jnp.sum(e, axis=-1, keepdims=True)).astype(BF16)
    x2 = x1 + _dot(p_ref[...], wvo_ref[...])
    x2_ref[...] = x2
    hf = _rms(x2, gf_ref[...]).astype(BF16)
    rl_ref[...] = _dot(hf, wr_ref[...])
    bits = lax.bitcast_convert_type(hf.astype(F32), U32)
    for c in range(_SUBLANES):
        lo = bits[:, (2 * c) * _LANES:(2 * c + 1) * _LANES]
        hi = bits[:, (2 * c + 1) * _LANES:(2 * c + 2) * _LANES]
        hf_ref[pl.ds(c, x2.shape[0], stride=_SUBLANES), :] = (hi & jnp.uint32(0xFFFF0000)) | (lo >> 16)


def _mixout(x, dsa, ml, wo, wqk, wvo, wr, gx, gf, tm):
    T = x.shape[0]
    row = lambda n: pl.BlockSpec((tm, n), lambda i: (i, 0))
    return pl.pallas_call(
        _mixout_body,
        grid=(T // tm,),
        in_specs=[row(D_MODEL), row(dsa.shape[1]), row(ml.shape[1]), _resident(wo.shape),
                  _resident(wqk.shape), _resident(wvo.shape), _resident(wr.shape), _resident(gx.shape),
                  _resident(gf.shape)],
        out_specs=[row(D_MODEL), pl.BlockSpec((tm * _SUBLANES, _LANES), lambda i: (i, 0)), row(wr.shape[1])],
        out_shape=[jax.ShapeDtypeStruct((T, D_MODEL), F32),
                   jax.ShapeDtypeStruct((T * _SUBLANES, _LANES), U32),
                   jax.ShapeDtypeStruct((T, wr.shape[1]), F32)],
        scratch_shapes=[pltpu.VMEM((tm, wqk.shape[1]), BF16)],
        compiler_params=_params(),
        name="mixout",
    )(x, dsa, ml, wo, wqk, wvo, wr, gx, gf)


def _moe_body(row_ref, gap_lo_ref, gap_hi_ref, eb_ref, hf_hbm, wg_ref, wu_ref, wd_ref, y_hbm,
              tok_ref, xbuf, ybuf, wgb, wub, wdb, gsem, ysem, *, bm, n_blk):
    e = pl.program_id(0)

    @pl.when(e == 0)
    def _():
        def clear_gap(g, carry):
            def clear(r, c):
                tok_ref[r] = 0
                return c
            return lax.fori_loop(gap_lo_ref[g], gap_hi_ref[g], clear, carry)

        def place(a, carry):
            tok_ref[row_ref[a]] = lax.shift_right_logical(a, _TOPK_SHIFT)
            return carry

        lax.fori_loop(0, gap_lo_ref.shape[0], clear_gap, 0)
        lax.fori_loop(0, row_ref.shape[0], place, 0, unroll=8)

    n_valid = eb_ref[N_EXPERTS]
    b_lo = eb_ref[e]
    b_hi = eb_ref[e + 1]

    def tok_words(tok):
        return hf_hbm.at[pl.ds(pl.multiple_of(tok * _SUBLANES, _SUBLANES), _SUBLANES)]

    def gather(blk, s):
        def issue(g, carry):
            for u in range(_SUBLANES):
                r = g * _SUBLANES + u
                pltpu.make_async_copy(tok_words(tok_ref[blk * bm + r]),
                                      xbuf.at[s, pl.ds(pl.multiple_of(r * _SUBLANES, _SUBLANES), _SUBLANES)],
                                      gsem.at[s]).start(priority=1)
            return carry
        lax.fori_loop(0, bm // _SUBLANES, issue, 0)

    def wait_gather(s):
        pltpu.make_async_copy(xbuf.at[s], xbuf.at[s], gsem.at[s]).wait()

    def y_copy(blk, s):
        return pltpu.make_async_copy(ybuf.at[s], y_hbm.at[pl.ds(pl.multiple_of(blk * bm, bm), bm)], ysem.at[s])

    @pl.when(e == 0)
    def _():
        for k in range(_GATHER_AHEAD):
            gather(k, k)

    @pl.when(b_hi > b_lo)
    def _():
        wgb[...] = wg_ref[0].astype(BF16)
        wub[...] = wu_ref[0].astype(BF16)
        wdb[...] = wd_ref[0].astype(BF16)

        def block(b, carry):
            s = b % _GATHER_SLOTS
            wait_gather(s)
            chunks = []
            for c in range(_SUBLANES):
                w = xbuf[s, pl.ds(c, bm, stride=_SUBLANES), :]
                chunks.append(lax.bitcast_convert_type(w << 16, F32).astype(BF16))
                chunks.append(lax.bitcast_convert_type(w & jnp.uint32(0xFFFF0000), F32).astype(BF16))
            xb = jnp.concatenate(chunks, axis=-1)
            gate = _dot(xb, wgb[...])
            a = gate * jax.nn.sigmoid(gate) * _dot(xb, wub[...])
            y = _dot(a.astype(BF16), wdb[...])

            nxt = jnp.minimum(b + _GATHER_AHEAD, n_blk - 1)
            for r in range(bm):
                pltpu.make_async_copy(tok_words(tok_ref[nxt * bm + r]),
                                      xbuf.at[(b + _GATHER_AHEAD) % _GATHER_SLOTS, pl.ds(r * _SUBLANES, _SUBLANES)],
                                      gsem.at[(b + _GATHER_AHEAD) % _GATHER_SLOTS]).start(priority=1)

            @pl.when(b >= 2)
            def _():
                y_copy(b - 2, b % 2).wait()

            ybuf[b % 2] = y
            y_copy(b, b % 2).start()
            return carry

        lax.fori_loop(b_lo, b_hi, block, 0)

    @pl.when(e == pl.num_programs(0) - 1)
    def _():
        for k in range(_GATHER_AHEAD):
            wait_gather((n_valid + k) % _GATHER_SLOTS)

        @pl.when(n_valid >= 2)
        def _():
            y_copy(n_valid - 2, n_valid % 2).wait()
        y_copy(n_valid - 1, (n_valid - 1) % 2).wait()
        ybuf[0] = jnp.zeros(ybuf.shape[1:], F32)

        def zero_block(b, carry):
            cp = y_copy(b, 0)
            cp.start()
            cp.wait()
            return carry

        lax.fori_loop(n_valid, n_blk, zero_block, 0)


def _moe(row, gap_lo, gap_hi, e_blk, hf, wg, wu, wd, n_blk, bm):
    D = wg.shape[1]
    wspec = lambda shape: pl.BlockSpec((1,) + shape, lambda e, *_: (e, 0, 0))
    grid_spec = pltpu.PrefetchScalarGridSpec(
        num_scalar_prefetch=4,
        grid=(N_EXPERTS,),
        in_specs=[pl.BlockSpec(memory_space=pl.ANY),
                  wspec((D, D_EXPERT)), wspec((D, D_EXPERT)), wspec((D_EXPERT, D))],
        out_specs=pl.BlockSpec(memory_space=pl.ANY),
        scratch_shapes=[pltpu.SMEM((n_blk * bm,), I32),
                        pltpu.VMEM((_GATHER_SLOTS, bm * _SUBLANES, _LANES), U32),
                        pltpu.VMEM((2, bm, D), F32),
                        pltpu.VMEM((D, D_EXPERT), BF16), pltpu.VMEM((D, D_EXPERT), BF16),
                        pltpu.VMEM((D_EXPERT, D), BF16),
                        pltpu.SemaphoreType.DMA((_GATHER_SLOTS,)), pltpu.SemaphoreType.DMA((2,))],
    )
    return pl.pallas_call(
        functools.partial(_moe_body, bm=bm, n_blk=n_blk),
        grid_spec=grid_spec,
        out_shape=jax.ShapeDtypeStruct((n_blk * bm, D), F32),
        compiler_params=_params(),
        name="moe",
    )(row, gap_lo, gap_hi, e_blk, hf, wg, wu, wd)


def _route_body(rl_ref, bias_ref, out_ref, cnt_ref, seen_ref, *, tm):
    @pl.when(pl.program_id(0) == 0)
    def _():
        seen_ref[...] = jnp.zeros(seen_ref.shape, F32)

    lg = rl_ref[...] + bias_ref[...]
    lane = lax.broadcasted_iota(I32, lg.shape, 1).astype(F32)
    first = lambda hit: jnp.min(jnp.where(hit, lane, float(_LANES)), axis=-1, keepdims=True)
    is_group = lane < N_GROUPS
    gl = jnp.where(is_group, lg, -jnp.inf)
    g_max = jnp.max(gl, axis=-1, keepdims=True)
    g_sel = first(gl == g_max)
    p_g = 1.0 / jnp.sum(jnp.where(is_group, jnp.exp(gl - g_max), 0.0), axis=-1, keepdims=True)

    e_id = lane - N_GROUPS
    in_group = (e_id >= 0) & (e_id < N_EXPERTS) & (jnp.floor(e_id / EXP_PER_GROUP) == g_sel)
    el = jnp.where(in_group, lg, -jnp.inf)
    ee = jnp.where(in_group, jnp.exp(el - jnp.max(el, axis=-1, keepdims=True)), 0.0)
    cand = jnp.where(in_group, ee / jnp.sum(ee, axis=-1, keepdims=True), -1.0)
    p1 = jnp.max(cand, axis=-1, keepdims=True)
    l1 = first(cand == p1)
    cand = jnp.where(lane == l1, -2.0, cand)
    p2 = jnp.max(cand, axis=-1, keepdims=True)
    l2 = first(cand == p2)
    g1 = p_g * p1 / (p1 + p2)
    g2 = p_g * p2 / (p1 + p2)

    oh1 = (lane == l1).astype(BF16)
    oh2 = (lane == l2).astype(BF16)
    both = oh1 + oh2
    earlier = (lax.broadcasted_iota(I32, (tm, tm), 1) < lax.broadcasted_iota(I32, (tm, tm), 0)).astype(BF16)
    before = _dot(earlier, both) + seen_ref[...]
    r1 = jnp.sum(before * oh1.astype(F32), axis=-1, keepdims=True)
    r2 = jnp.sum(before * oh2.astype(F32), axis=-1, keepdims=True)
    seen_ref[...] = seen_ref[...] + jnp.sum(both.astype(F32), axis=0, keepdims=True)
    cnt_ref[...] = seen_ref[...]

    cols = (l1 - N_GROUPS, l2 - N_GROUPS, r1, r2, g1, g2)
    out = jnp.zeros(lg.shape, F32)
    for c, v in enumerate(cols):
        out = jnp.where(lane == c, v, out)
    out_ref[...] = out


def _route_tokens(rl, bias, tm):
    T = rl.shape[0]
    return pl.pallas_call(
        functools.partial(_route_body, tm=tm),
        grid=(T // tm,),
        in_specs=[pl.BlockSpec((tm, _LANES), lambda i: (i, 0)), _resident(bias.shape)],
        out_specs=[pl.BlockSpec((tm, _LANES), lambda i: (i, 0)), pl.BlockSpec((1, _LANES), lambda i: (0, 0))],
        out_shape=[jax.ShapeDtypeStruct((T, _LANES), F32), jax.ShapeDtypeStruct((1, _LANES), F32)],
        scratch_shapes=[pltpu.VMEM((1, _LANES), F32)],
        compiler_params=_params(),
        name="route",
    )(rl, bias)


def _route(rl, b_group, b_router, bm, tm):
    N = rl.shape[0]
    bias = jnp.concatenate([b_group, b_router, jnp.zeros((_LANES - N_GROUPS - N_EXPERTS,), F32)]).reshape(1, _LANES)
    per_tok, seen = _route_tokens(rl, bias, tm)
    expert_id = per_tok[:, 0:TOPK_IN_GROUP].astype(I32)
    rank = per_tok[:, TOPK_IN_GROUP:2 * TOPK_IN_GROUP].astype(I32)
    gates = per_tok[:, 2 * TOPK_IN_GROUP:3 * TOPK_IN_GROUP]
    counts = seen[0, N_GROUPS:N_GROUPS + N_EXPERTS].astype(I32)

    A = N * TOPK_IN_GROUP
    padded = (counts + bm - 1) // bm * bm
    pad_ends = jnp.cumsum(padded)
    pad_starts = pad_ends - padded
    onehot = expert_id[..., None] == jnp.arange(N_EXPERTS, dtype=I32)
    row = (jnp.sum(jnp.where(onehot, pad_starts, 0), axis=-1) + rank).reshape(A)
    n_blk = -(-A // bm) + N_EXPERTS
    gap_lo = jnp.concatenate([pad_starts + counts, pad_ends[-1:]]).astype(I32)
    gap_hi = jnp.concatenate([pad_ends, jnp.full((1,), n_blk * bm, I32)]).astype(I32)
    e_blk = jnp.concatenate([pad_starts, pad_ends[-1:]]) // bm
    return row, gap_lo, gap_hi, gates, e_blk.astype(I32), n_blk


def _final_body(row_ref, x_ref, gate_ref, y_hbm, g_ref, out_ref, ybuf, sem, *, tm):
    i = pl.program_id(0)
    n_tiles = pl.num_programs(0)
    slot = i % _FINAL_SLOTS

    def gather(tile, s):
        def issue(g, carry):
            for u in range(_SUBLANES):
                for k in range(TOPK_IN_GROUP):
                    src = row_ref[(tile * tm + g * _SUBLANES + u) * TOPK_IN_GROUP + k]
                    pltpu.make_async_copy(y_hbm.at[pl.ds(src, 1)], ybuf.at[s, k, g, pl.ds(u, 1)],
                                          sem.at[s]).start()
            return carry
        lax.fori_loop(0, tm // _SUBLANES, issue, 0)

    def wait_rows(s):
        pltpu.make_async_copy(ybuf.at[s], ybuf.at[s], sem.at[s]).wait()

    @pl.when(i == 0)
    def _():
        for t in range(_FINAL_SLOTS - 1):
            gather(jnp.minimum(t, n_tiles - 1), t)

    wait_rows(slot)
    acc = x_ref[...]
    for k in range(TOPK_IN_GROUP):
        acc = acc + gate_ref[:, k:k + 1] * ybuf[slot, k].reshape(tm, ybuf.shape[-1])
    y = _rms(acc, g_ref[...])

    ahead = _FINAL_SLOTS - 1
    nxt = jnp.minimum(i + ahead, n_tiles - 1)
    for r in range(tm):
        for k in range(TOPK_IN_GROUP):
            src = row_ref[(nxt * tm + r) * TOPK_IN_GROUP + k]
            pltpu.make_async_copy(y_hbm.at[pl.ds(src, 1)],
                                  ybuf.at[(i + ahead) % _FINAL_SLOTS, k, r // _SUBLANES, pl.ds(r % _SUBLANES, 1)],
                                  sem.at[(i + ahead) % _FINAL_SLOTS]).start()
    out_ref[...] = y

    @pl.when(i == n_tiles - 1)
    def _():
        for t in range(1, _FINAL_SLOTS):
            wait_rows((i + t) % _FINAL_SLOTS)


def _final(row, x2, gates, y_rows, g, tm):
    T, D = x2.shape
    grid_spec = pltpu.PrefetchScalarGridSpec(
        num_scalar_prefetch=1,
        grid=(T // tm,),
        in_specs=[pl.BlockSpec((tm, D), lambda i, *_: (i, 0)),
                  pl.BlockSpec((tm, TOPK_IN_GROUP), lambda i, *_: (i, 0)),
                  pl.BlockSpec(memory_space=pl.ANY),
                  pl.BlockSpec(g.shape, lambda i, *_: (0, 0))],
        out_specs=pl.BlockSpec((tm, D), lambda i, *_: (i, 0)),
        scratch_shapes=[pltpu.VMEM((_FINAL_SLOTS, TOPK_IN_GROUP, tm // _SUBLANES, _SUBLANES, D), F32),
                        pltpu.SemaphoreType.DMA((_FINAL_SLOTS,))],
    )
    return pl.pallas_call(
        functools.partial(_final_body, tm=tm),
        grid_spec=grid_spec,
        out_shape=jax.ShapeDtypeStruct((T, D), F32),
        compiler_params=_params(),
        name="final",
    )(row, x2, gates, y_rows, g)


def _tile_sizes(T):
    pick = lambda want: want if T % want == 0 else CHUNK
    return dict(inproj=pick(256), dsa_q=pick(256), dsa_k=pick(512), mlstm=pick(256), mixout=pick(256),
                final=pick(256), route=pick(512), moe=128)


def _layer(x, mem, norm_mix_g, w_in, kv_norm_g, k_idx_norm_g, w_uk, w_uv, conv_w, conv_b, gate_b, ml_norm_g,
           w_out, norm_x_g, mem_norm_g, w_cq, w_ckv, w_co, norm_ffn_g, w_group, b_group, w_router, b_router,
           w_gate, w_up, w_down, out_g):
    T = x.shape[0]
    ts = _tile_sizes(T)
    r2 = lambda v: v.reshape(1, -1)

    w_r = _wprep(w_in, 256)
    wuk_t = jnp.transpose(w_uk, (1, 2, 0)).astype(BF16)
    wuv_t = jnp.transpose(w_uv, (1, 0, 2)).astype(BF16)

    qabs, ckv, qi, kidx, small, mqk, mv, mo = _inproj(
        x, r2(norm_mix_g), w_r, wuk_t, r2(kv_norm_g), r2(k_idx_norm_g), ts["inproj"])

    gate_rows = jnp.transpose(small[:, _S_WI:_S_MF + ML_HEADS])
    dsa_out = _dsa(qi, gate_rows, qabs, kidx, ckv, wuv_t, ts["dsa_q"], ts["dsa_k"])

    gb_col = jnp.zeros((1, _SMALL), F32).at[0, _S_MI:_S_MI + 2 * ML_HEADS].set(gate_b)
    ml_out = _mlstm(mqk, mv, small, gate_rows, mo, conv_w, r2(conv_b), gb_col, gate_b.reshape(-1, 1),
                    r2(ml_norm_g), ts["mlstm"])

    wqk, wvo = _memfold(mem, r2(mem_norm_g), w_ckv, w_cq, w_co)
    w_rt = jnp.concatenate([w_group, w_router,
                            jnp.zeros((D_MODEL, 128 - N_GROUPS - N_EXPERTS), w_group.dtype)], axis=1)
    x2, hf, rl = _mixout(x, dsa_out, ml_out, w_out.astype(BF16), wqk, wvo, w_rt.astype(BF16),
                         r2(norm_x_g), r2(norm_ffn_g), ts["mixout"])

    bm = ts["moe"]
    row, gap_lo, gap_hi, gates, e_blk, n_blk = _route(rl, b_group, b_router, bm, ts["route"])
    y_rows = _moe(row, gap_lo, gap_hi, e_blk, hf, w_gate, w_up, w_down, n_blk, bm)
    return _final(row, x2, gates, y_rows, r2(out_g), ts["final"])


def kernel(x, mem, norm_mix_g, w_in, kv_norm_g, k_idx_norm_g, w_uk, w_uv, conv_w, conv_b, gate_b, ml_norm_g,
           w_out, norm_x_g, mem_norm_g, w_cq, w_ckv, w_co, norm_ffn_g, w_group, b_group, w_router, b_router,
           w_gate, w_up, w_down, final_norm_g):
    B, T, D = x.shape
    assert B == 1 and D == D_MODEL and norm_mix_g.shape[0] == 1 and T % CHUNK == 0
    out = _layer(x[0], mem[0], norm_mix_g[0], w_in, kv_norm_g[0], k_idx_norm_g[0], w_uk[0], w_uv[0],
                 conv_w[0], conv_b[0], gate_b[0], ml_norm_g[0], w_out[0], norm_x_g[0], mem_norm_g[0],
                 w_cq[0], w_ckv[0], w_co[0], norm_ffn_g[0], w_group[0], b_group[0], w_router[0], b_router[0],
                 w_gate[0], w_up[0], w_down[0], final_norm_g)
    return out[None]
```

```python
import functools

import jax
import jax.numpy as jnp
from jax import lax
from jax.experimental import pallas as pl
from jax.experimental.pallas import tpu as pltpu

F32 = jnp.float32
BF16 = jnp.bfloat16
I32 = jnp.int32
U32 = jnp.uint32

EPS = 1e-6
CHUNK = 64
D_MODEL = 2048

DSA_HEADS = 8
DSA_HEAD_DIM = 128
DSA_LATENT = 256
IDX_HEADS = 8
IDX_DIM = 64
TOPK_MAX = 256

ML_HEADS = 4
ML_QK_DIM = 128
ML_V_DIM = 256
CONV_W = 4

X_HEADS = 4
X_HEAD_DIM = D_MODEL // X_HEADS

N_GROUPS = 4
EXP_PER_GROUP = 8
N_EXPERTS = N_GROUPS * EXP_PER_GROUP
TOPK_IN_GROUP = 2
D_EXPERT = 512

_O_DQ = 0
_O_CKV = _O_DQ + DSA_HEADS * DSA_HEAD_DIM
_O_QI = _O_CKV + DSA_LATENT
_O_KI = _O_QI + IDX_HEADS * IDX_DIM
_O_WI = _O_KI + IDX_DIM
_O_MQ = _O_WI + IDX_HEADS
_O_MK = _O_MQ + ML_HEADS * ML_QK_DIM
_O_MV = _O_MK + ML_HEADS * ML_QK_DIM
_O_MI = _O_MV + ML_HEADS * ML_V_DIM
_O_MF = _O_MI + ML_HEADS
_O_MO = _O_MF + ML_HEADS
_O_END = _O_MO + ML_HEADS * ML_V_DIM

_G_DQ = (0, 1024)
_G_CKV = (1024, 1280)
_G_QI = (1280, 1792)
_G_SMALL = (1792, 1920)
_G_MQK = (1920, 2944)
_G_MV = (2944, 3968)
_G_MO = (3968, 4992)
_W_COLS = 4992
_S_WI = IDX_DIM
_S_MI = _S_WI + IDX_HEADS
_S_MF = _S_MI + ML_HEADS
_SMALL = 128

_VMEM_LIMIT = 56 * 1024 * 1024
_INT_MIN = -(2 ** 31)
_CHUNK_SHIFT = CHUNK.bit_length() - 1
_LOG2E = 1.4426950408889634
_SUBLANES = 8
_LANES = 128
_GATHER_SLOTS = 9
_GATHER_AHEAD = _GATHER_SLOTS - 1
_TOPK_SHIFT = TOPK_IN_GROUP.bit_length() - 1
_FINAL_SLOTS = 4
_NEG = -1e30


def _rms(v, g):
    return v * lax.rsqrt(jnp.mean(v * v, axis=-1, keepdims=True) + EPS) * g


def _dot(a, b):
    return jnp.dot(a, b, preferred_element_type=F32)


def _dot_nt(a, b):
    return lax.dot_general(a, b, (((1,), (1,)), ((), ())), preferred_element_type=F32)


def _resident(shape):
    nd = len(shape)
    return pl.BlockSpec(shape, lambda *_: (0,) * nd, pipeline_mode=pl.Buffered(1))


def _params(n_axes=1):
    return pltpu.CompilerParams(dimension_semantics=("arbitrary",) * n_axes,
                                vmem_limit_bytes=_VMEM_LIMIT)


def _wprep_body(w_hbm, out_ref, wbuf, sem, *, tk):
    i = pl.program_id(0)
    slot = i % 2

    def rows(blk, s):
        return pltpu.make_async_copy(w_hbm.at[0, pl.ds(pl.multiple_of(blk * tk, tk), tk)], wbuf.at[s], sem.at[s])

    @pl.when(i == 0)
    def _():
        rows(0, 0).start()

    @pl.when(i + 1 < pl.num_programs(0))
    def _():
        rows(i + 1, 1 - slot).start()

    rows(i, slot).wait()
    w_ref = wbuf.at[slot]
    n_small = IDX_DIM + IDX_HEADS
    out_ref[:, _G_DQ[0]:_G_SMALL[0] + n_small] = w_ref[:, _O_DQ:_O_MQ].astype(BF16)
    out_ref[:, _G_SMALL[0] + n_small:_G_SMALL[0] + n_small + 2 * ML_HEADS] = w_ref[:, _O_MI:_O_MO].astype(BF16)
    out_ref[:, _G_SMALL[0] + n_small + 2 * ML_HEADS:_G_SMALL[1]] = jnp.zeros(
        (out_ref.shape[0], _SMALL - n_small - 2 * ML_HEADS), BF16)
    out_ref[:, _G_MQK[0]:_G_MV[1]] = w_ref[:, _O_MQ:_O_MI].astype(BF16)
    out_ref[:, _G_MO[0]:_G_MO[1]] = w_ref[:, _O_MO:_O_END].astype(BF16)


def _wprep(w_in, tk):
    K = w_in.shape[1]
    return pl.pallas_call(
        functools.partial(_wprep_body, tk=tk),
        grid=(K // tk,),
        in_specs=[pl.BlockSpec(memory_space=pl.ANY)],
        out_specs=pl.BlockSpec((tk, _W_COLS), lambda i: (i, 0)),
        out_shape=jax.ShapeDtypeStruct((K, _W_COLS), BF16),
        scratch_shapes=[pltpu.VMEM((2, tk, w_in.shape[2]), F32), pltpu.SemaphoreType.DMA((2,))],
        compiler_params=_params(),
        name="wprep",
    )(w_in)


def _inproj_body(x_ref, g_ref, w_ref, wuk_ref, kvg_ref, kig_ref,
                 qabs_ref, ckv_ref, qi_ref, kidx_ref, small_ref, mqk_ref, mv_ref, mo_ref):
    h = _rms(x_ref[...], g_ref[...]).astype(BF16)

    def proj(grp):
        return _dot(h, w_ref[:, grp[0]:grp[1]])

    dq = proj(_G_DQ)
    for hd in range(DSA_HEADS):
        qh = dq[:, hd * DSA_HEAD_DIM:(hd + 1) * DSA_HEAD_DIM].astype(BF16)
        qa = _dot(qh, wuk_ref[hd]) * (DSA_HEAD_DIM ** -0.5 * _LOG2E)
        qabs_ref[:, hd * DSA_LATENT:(hd + 1) * DSA_LATENT] = qa.astype(BF16)
    ckv_ref[...] = _rms(proj(_G_CKV), kvg_ref[...]).astype(BF16)
    qi_ref[...] = (proj(_G_QI) * (IDX_DIM ** -0.5)).astype(BF16)
    small = proj(_G_SMALL)
    small_ref[...] = small
    kidx_ref[...] = _rms(small[:, :IDX_DIM], kig_ref[...]).astype(BF16)
    mqk_ref[...] = proj(_G_MQK)
    mv_ref[...] = proj(_G_MV).astype(BF16)
    mo_ref[...] = proj(_G_MO)


def _inproj(x, g, w, wuk, kvg, kig, tm):
    T = x.shape[0]
    row = lambda n: pl.BlockSpec((tm, n), lambda i: (i, 0))
    outs = [(8 * DSA_LATENT, BF16), (DSA_LATENT, BF16), (IDX_HEADS * IDX_DIM, BF16), (IDX_DIM, BF16),
            (_SMALL, F32), (2 * ML_HEADS * ML_QK_DIM, F32), (ML_HEADS * ML_V_DIM, BF16),
            (ML_HEADS * ML_V_DIM, F32)]
    return pl.pallas_call(
        _inproj_body,
        grid=(T // tm,),
        in_specs=[row(D_MODEL), _resident(g.shape), _resident(w.shape), _resident(wuk.shape),
                  _resident(kvg.shape), _resident(kig.shape)],
        out_specs=[row(n) for n, _ in outs],
        out_shape=[jax.ShapeDtypeStruct((T, n), dt) for n, dt in outs],
        compiler_params=_params(),
        name="inproj",
    )(x, g, w, wuk, kvg, kig)


def _sublane_fold(v, op, rows=_SUBLANES, ways=4):
    groups = [v[r * rows:(r + 1) * rows, :] for r in range(v.shape[0] // rows)]
    accs = groups[:ways]
    for r in range(ways, len(groups)):
        accs[r % ways] = op(accs[r % ways], groups[r])
    while len(accs) > 1:
        accs = [op(accs[k], accs[k + 1]) if k + 1 < len(accs) else accs[k] for k in range(0, len(accs), 2)]
    return accs[0]


def _bit_transpose32(words):
    a = list(words)
    j, m = 16, 0x0000FFFF
    while j:
        k = 0
        while k < 32:
            t = (a[k] ^ (a[k + j] >> j)) & jnp.uint32(m)
            a[k] = a[k] ^ t
            a[k + j] = a[k + j] ^ (t << j)
            k = (k + j + 1) & ~j
        j >>= 1
        m = (m ^ (m << j)) & 0xFFFFFFFF
    return a


def _dsa_body(qi_ref, wrow_ref, qabs_ref, kidx_ref, ckv_ref, ckvt_ref, wuv_ref, out_ref,
              key_ref, planes_ref, alive_ref, lga_ref, lgb_ref, m_ref, l_ref, acc_ref, *, tq, tk, topk, nbits_idx):
    i = pl.program_id(0)
    n_kb = ((i + 1) * tq + tk - 1) // tk
    w_rows = wrow_ref[0:IDX_HEADS, :] * (IDX_HEADS ** -0.5)
    q_chunk = (i * tq + lax.broadcasted_iota(I32, (1, tq), 1)) >> _CHUNK_SHIFT

    def key_pos(j):
        return j * tk + lax.broadcasted_iota(I32, (tk, 1), 0)

    plane_rows = tk // 32

    @pl.when(i == 0)
    def _():
        planes_ref[...] = jnp.zeros(planes_ref.shape, U32)

    def score_block(j, carry):
        kx = kidx_ref[pl.ds(pl.multiple_of(j * tk, tk), tk), :]
        s = jnp.zeros((tk, tq), F32)
        for hd in range(IDX_HEADS):
            d = _dot_nt(kx, qi_ref[:, hd * IDX_DIM:(hd + 1) * IDX_DIM])
            s = s + w_rows[hd:hd + 1, :] * jnp.maximum(d, 0.0)
        bits = lax.bitcast_convert_type(s, I32)
        key = bits ^ ((bits >> 31) & 0x7FFFFFFF)
        key = jnp.where((key_pos(j) >> _CHUNK_SHIFT) <= q_chunk, key, _INT_MIN)
        key_ref[j] = key
        u = lax.bitcast_convert_type(key, U32) ^ jnp.uint32(0x80000000)
        for h in range(plane_rows // _SUBLANES):
            base = h * 32 * _SUBLANES
            planes = _bit_transpose32([u[base + _SUBLANES * v:base + _SUBLANES * (v + 1), :] for v in range(32)])
            row0 = pl.multiple_of(j * plane_rows + h * _SUBLANES, _SUBLANES)
            for b in range(32):
                planes_ref[b, pl.ds(row0, _SUBLANES), :] = planes[31 - b]
        return carry

    lax.fori_loop(0, n_kb, score_block, 0)

    def count(pred):
        def body(j, acc):
            hit = pred(key_ref[j], key_pos(j)).astype(I32)
            return acc + _sublane_fold(hit, jnp.add)
        acc = lax.fori_loop(0, n_kb, body, jnp.zeros((8, tq), I32))
        return jnp.sum(acc, axis=0, keepdims=True)

    word_row = lax.broadcasted_iota(I32, (alive_ref.shape[0], 1), 0)
    alive_ref[...] = jnp.where(word_row < n_kb * plane_rows, jnp.full(alive_ref.shape, 0xFFFFFFFF, U32),
                               jnp.uint32(0))

    def radix_select(n_rows):
        def ones_count(words):
            return jnp.sum(_sublane_fold(lax.population_count(words).astype(I32), jnp.add), axis=0, keepdims=True)

        def select_bit(it, carry):
            need, t_u = carry
            b = 31 - it
            alive = alive_ref[0:n_rows, :]
            ones = alive & planes_ref[b, pl.ds(0, n_rows), :]
            n_ones = ones_count(ones)
            take = n_ones >= need
            alive_ref[0:n_rows, :] = jnp.where(take, ones, alive ^ ones)
            bit = lax.shift_left(jnp.uint32(1), jnp.asarray(b, U32))
            return jnp.where(take, need, need - n_ones), jnp.where(take, t_u | bit, t_u)

        need, t_u = lax.fori_loop(0, 32, select_bit, (jnp.full((1, tq), topk, I32), jnp.zeros((1, tq), U32)))
        return need, t_u, ones_count(alive_ref[0:n_rows, :])

    all_rows = alive_ref.shape[0]
    if all_rows % (2 * _SUBLANES) == 0:
        need, t_u, n_equal = lax.cond(n_kb * plane_rows <= all_rows // 2,
                                      lambda: radix_select(all_rows // 2), lambda: radix_select(all_rows))
    else:
        need, t_u, n_equal = radix_select(all_rows)
    short = t_u == 0
    t = jnp.maximum(lax.bitcast_convert_type(t_u ^ jnp.uint32(0x80000000), I32), _INT_MIN + 1)
    all_pos = jnp.int32(2 ** nbits_idx - 1)
    has_ties = (n_equal > need) & jnp.logical_not(short)
    n_tie_take = jnp.where(has_ties, need, all_pos)

    def tie_cutoff():
        def pos_bit(b, c):
            cand = c + lax.shift_left(jnp.int32(1), nbits_idx - 1 - b)
            f = count(lambda kb, pos: (kb == t) & (pos < cand))
            return jnp.where(f <= n_tie_take, cand, c)
        return lax.fori_loop(0, nbits_idx, pos_bit, jnp.zeros((1, tq), I32))

    cut = lax.cond(jnp.max(has_ties.astype(I32)) > 0, tie_cutoff, lambda: jnp.full((1, tq), all_pos, I32))

    m_ref[...] = jnp.full(m_ref.shape, _NEG, F32)
    l_ref[...] = jnp.zeros(l_ref.shape, F32)
    acc_ref[...] = jnp.zeros(acc_ref.shape, F32)
    last_blk = ckvt_ref.shape[0] - 1

    def logits(j, lg_buf):
        c_blk = ckv_ref[pl.ds(pl.multiple_of(jnp.minimum(j, last_blk) * tk, tk), tk), :]
        kb = key_ref[j]
        sel = (kb > t) | ((kb == t) & (key_pos(j) < cut))
        bias = jnp.where(sel, 0.0, _NEG).astype(F32)
        for hd in range(DSA_HEADS):
            lg_buf[hd] = _dot_nt(c_blk, qabs_ref[:, hd * DSA_LATENT:(hd + 1) * DSA_LATENT]) + bias

    def accumulate(j, lg_buf):
        c_blk_t = ckvt_ref[jnp.minimum(j, last_blk)]
        for hd in range(DSA_HEADS):
            lg = lg_buf[hd]
            m_old = m_ref[hd:hd + 1, :]
            m_new = jnp.maximum(m_old, jnp.max(_sublane_fold(lg, jnp.maximum, ways=1), axis=0, keepdims=True))
            p = jnp.exp2(lg - m_new)
            alpha = jnp.exp2(m_old - m_new)
            l_ref[hd:hd + 1, :] = alpha * l_ref[hd:hd + 1, :] + jnp.sum(_sublane_fold(p, jnp.add, ways=1), axis=0,
                                                                         keepdims=True)
            acc_ref[hd] = alpha * acc_ref[hd] + _dot(c_blk_t, p.astype(BF16))
            m_ref[hd:hd + 1, :] = m_new

    logits(0, lga_ref)

    def attn_pair(mi, carry):
        ja = 2 * mi
        accumulate(ja, lga_ref)
        logits(ja + 1, lgb_ref)
        accumulate(ja + 1, lgb_ref)
        logits(jnp.minimum(ja + 2, n_kb - 1), lga_ref)
        return carry

    lax.fori_loop(0, n_kb // 2, attn_pair, 0)

    @pl.when(n_kb % 2 == 1)
    def _():
        accumulate(n_kb - 1, lga_ref)

    for hd in range(DSA_HEADS):
        o_lat = (acc_ref[hd] / l_ref[hd:hd + 1, :]).T.astype(BF16)
        out_ref[:, hd * DSA_HEAD_DIM:(hd + 1) * DSA_HEAD_DIM] = _dot(o_lat, wuv_ref[hd]).astype(BF16)


def _dsa(qi, wrows, qabs, kidx, ckv, wuv, tq, tk):
    T = qi.shape[0]
    topk = min(TOPK_MAX, T // 4)
    n_kb = T // tk
    ckvt = jnp.transpose(ckv.reshape(n_kb, tk, DSA_LATENT), (0, 2, 1))
    row = lambda n: pl.BlockSpec((tq, n), lambda i: (i, 0))
    body = functools.partial(_dsa_body, tq=tq, tk=tk, topk=topk, nbits_idx=int(T).bit_length())
    return pl.pallas_call(
        body,
        grid=(T // tq,),
        in_specs=[row(qi.shape[1]), pl.BlockSpec((wrows.shape[0], tq), lambda i: (0, i)), row(qabs.shape[1]),
                  _resident(kidx.shape), _resident(ckv.shape), _resident(ckvt.shape), _resident(wuv.shape)],
        out_specs=row(DSA_HEADS * DSA_HEAD_DIM),
        out_shape=jax.ShapeDtypeStruct((T, DSA_HEADS * DSA_HEAD_DIM), BF16),
        scratch_shapes=[pltpu.VMEM((n_kb + n_kb % 2, tk, tq), I32),
                        pltpu.VMEM((32, T // 32, tq), U32), pltpu.VMEM((T // 32, tq), U32),
                        pltpu.VMEM((DSA_HEADS, tk, tq), F32), pltpu.VMEM((DSA_HEADS, tk, tq), F32),
                        pltpu.VMEM((DSA_HEADS, tq), F32),
                        pltpu.VMEM((DSA_HEADS, tq), F32), pltpu.VMEM((DSA_HEADS, DSA_LATENT, tq), F32)],
        compiler_params=_params(),
        name="dsa",
    )(qi, wrows, qabs, kidx, ckv, ckvt, wuv)


def _log_sigmoid(v):
    return jnp.minimum(v, 0.0) - jnp.log1p(jnp.exp(-jnp.abs(v)))


def _chunk_cumsum(v, axis):
    pos = lax.broadcasted_iota(I32, v.shape, axis) & (CHUNK - 1)
    d = 1
    while d < CHUNK:
        v = v + jnp.where(pos >= d, pltpu.roll(v, d, axis=axis), 0.0)
        d *= 2
    return v


def _mlstm_body(mqk_ref, mv_ref, small_ref, gt_ref, mo_ref, cw_ref, cb_ref, gbc_ref, gbr_ref, ng_ref,
                out_ref, xe_ref, c_ref, n_ref, m_ref, hs_ref, *, rows):
    @pl.when(pl.program_id(0) == 0)
    def _():
        xe_ref[0:8, :] = jnp.zeros((8, xe_ref.shape[1]), F32)
        c_ref[...] = jnp.zeros(c_ref.shape, F32)
        n_ref[...] = jnp.zeros(n_ref.shape, F32)
        m_ref[...] = jnp.zeros(m_ref.shape, F32)

    x = mqk_ref[...]
    xe_ref[8:8 + rows, :] = x
    y = cb_ref[...]
    for j in range(CONV_W - 1):
        y = y + xe_ref[5 + j:5 + j + rows, :] * cw_ref[j:j + 1, :]
    y = y + x * cw_ref[CONV_W - 1:CONV_W, :]
    xe_ref[0:8, :] = x[rows - 8:rows, :]
    qk = y * jax.nn.sigmoid(y)
    nqk = ML_HEADS * ML_QK_DIM
    q_all = (qk[:, :nqk] * (ML_QK_DIM ** -0.5)).astype(BF16)
    k_all = qk[:, nqk:]

    g_col = small_ref[...] + gbc_ref[...]
    g_row = gt_ref[...] + gbr_ref[...]
    b_col = _chunk_cumsum(_log_sigmoid(g_col), 0)
    b_row = _chunk_cumsum(_log_sigmoid(g_row), 1)

    tri = lax.broadcasted_iota(I32, (CHUNK, CHUNK), 1) <= lax.broadcasted_iota(I32, (CHUNK, CHUNK), 0)

    heads = range(ML_HEADS)
    for c in range(rows // CHUNK):
        lo, hi = c * CHUNK, (c + 1) * CHUNK
        bc = [b_col[lo:hi, _S_MF + hd:_S_MF + hd + 1] for hd in heads]
        lic = [g_col[lo:hi, _S_MI + hd:_S_MI + hd + 1] for hd in heads]
        br = [b_row[ML_HEADS + hd:ML_HEADS + hd + 1, lo:hi] for hd in heads]
        lir = [g_row[hd:hd + 1, lo:hi] for hd in heads]
        g_tot = [bc[hd][CHUNK - 1:CHUNK, :] for hd in heads]
        m_prev = [m_ref[hd][:, 0:1] for hd in heads]
        qh = [q_all[lo:hi, hd * ML_QK_DIM:(hd + 1) * ML_QK_DIM] for hd in heads]
        kh = [k_all[lo:hi, hd * ML_QK_DIM:(hd + 1) * ML_QK_DIM] for hd in heads]
        vh = [mv_ref[lo:hi, hd * ML_V_DIM:(hd + 1) * ML_V_DIM] for hd in heads]
        c_prev = [c_ref[hd] for hd in heads]
        n_prev = [n_ref[hd] for hd in heads]

        dmat = [jnp.where(tri, bc[hd] - br[hd] + lir[hd], -jnp.inf) for hd in heads]
        inter = [bc[hd] + m_prev[hd] for hd in heads]
        m_t = [jnp.maximum(inter[hd], jnp.max(dmat[hd], axis=-1, keepdims=True)) for hd in heads]
        w_intra = [jnp.exp(dmat[hd] - m_t[hd]) for hd in heads]
        a_inter = [jnp.exp(inter[hd] - m_t[hd]) for hd in heads]
        m_new = [jnp.maximum(g_tot[hd] + m_prev[hd],
                             jnp.max(g_tot[hd] - br[hd] + lir[hd], axis=-1, keepdims=True)) for hd in heads]
        a_state = [jnp.exp(g_tot[hd] + m_prev[hd] - m_new[hd]) for hd in heads]
        wk = [jnp.exp(g_tot[hd] - bc[hd] + lic[hd] - m_new[hd]) * kh[hd] for hd in heads]

        s_qk = [_dot_nt(qh[hd], kh[hd].astype(BF16)) * w_intra[hd] for hd in heads]
        read = [_dot(qh[hd], c_prev[hd].astype(BF16)) for hd in heads]
        d_c = [_dot(wk[hd].T.astype(BF16), vh[hd]) for hd in heads]
        num = [a_inter[hd] * read[hd] + _dot(s_qk[hd].astype(BF16), vh[hd]) for hd in heads]
        den = [a_inter[hd] * jnp.sum(qh[hd].astype(F32) * n_prev[hd], axis=-1, keepdims=True)
               + jnp.sum(s_qk[hd], axis=-1, keepdims=True) for hd in heads]
        for hd in heads:
            hs_ref[lo:hi, hd * ML_V_DIM:(hd + 1) * ML_V_DIM] = (
                num[hd] / jnp.maximum(jnp.abs(den[hd]), jnp.exp(-m_t[hd])))
        for hd in heads:
            c_ref[hd] = a_state[hd] * c_prev[hd] + d_c[hd]
            n_ref[hd] = a_state[hd] * n_prev[hd] + jnp.sum(wk[hd], axis=0, keepdims=True)
            m_ref[hd] = jnp.broadcast_to(m_new[hd], m_ref.shape[1:])

    for hd in range(ML_HEADS):
        sl = slice(hd * ML_V_DIM, (hd + 1) * ML_V_DIM)
        out_ref[:, sl] = (_rms(hs_ref[:, sl], ng_ref[:, sl]) * jax.nn.sigmoid(mo_ref[:, sl])).astype(BF16)


def _mlstm(mqk, mv, small, gt, mo, cw, cb, gbc, gbr, ng, rows):
    T = mqk.shape[0]
    row = lambda n: pl.BlockSpec((rows, n), lambda i: (i, 0))
    nv = ML_HEADS * ML_V_DIM
    return pl.pallas_call(
        functools.partial(_mlstm_body, rows=rows),
        grid=(T // rows,),
        in_specs=[row(mqk.shape[1]), row(nv), row(_SMALL), pl.BlockSpec((8, rows), lambda i: (1, i)), row(nv),
                  _resident(cw.shape), _resident(cb.shape), _resident(gbc.shape), _resident(gbr.shape),
                  _resident(ng.shape)],
        out_specs=row(nv),
        out_shape=jax.ShapeDtypeStruct((T, nv), BF16),
        scratch_shapes=[pltpu.VMEM((rows + 8, mqk.shape[1]), F32),
                        pltpu.VMEM((ML_HEADS, ML_QK_DIM, ML_V_DIM), F32),
                        pltpu.VMEM((ML_HEADS, 1, ML_QK_DIM), F32),
                        pltpu.VMEM((ML_HEADS, 1, 128), F32),
                        pltpu.VMEM((rows, nv), F32)],
        compiler_params=_params(),
        name="mlstm",
    )(mqk, mv, small, gt, mo, cw, cb, gbc, gbr, ng)


def _memfold_body(mem_ref, g_ref, wk_ref, wv_ref, wq_ref, wc_ref, wqk_ref, wvo_ref):
    mn = _rms(mem_ref[...], g_ref[...]).astype(BF16)
    k = _dot(mn, wk_ref[...].astype(BF16)).astype(BF16)
    v = _dot(mn, wv_ref[...].astype(BF16)).astype(BF16)
    wqk_ref[...] = (_dot_nt(wq_ref[...].astype(BF16), k) * (X_HEAD_DIM ** -0.5)).astype(BF16)
    wvo_ref[...] = _dot(v, wc_ref[...].astype(BF16)).astype(BF16)


def _memfold(mem, g, w_ckv, w_cq, w_co):
    M, D = mem.shape
    dh = X_HEAD_DIM
    return pl.pallas_call(
        _memfold_body,
        grid=(X_HEADS,),
        in_specs=[_resident(mem.shape), _resident(g.shape),
                  pl.BlockSpec((D, dh), lambda h: (0, h)), pl.BlockSpec((D, dh), lambda h: (0, X_HEADS + h)),
                  pl.BlockSpec((D, dh), lambda h: (0, h)), pl.BlockSpec((dh, D), lambda h: (h, 0))],
        out_specs=[pl.BlockSpec((D, M), lambda h: (0, h)), pl.BlockSpec((M, D), lambda h: (h, 0))],
        out_shape=[jax.ShapeDtypeStruct((D, X_HEADS * M), BF16), jax.ShapeDtypeStruct((X_HEADS * M, D), BF16)],
        compiler_params=_params(),
        name="memfold",
    )(mem, g, w_ckv, w_ckv, w_cq, w_co)


def _mixout_body(x_ref, dsa_ref, ml_ref, wo_ref, wqk_ref, wvo_ref, wr_ref, gx_ref, gf_ref,
                 x2_ref, hf_ref, rl_ref, p_ref):
    nd = dsa_ref.shape[1]
    x1 = x_ref[...] + _dot(dsa_ref[...], wo_ref[0:nd, :]) + _dot(ml_ref[...], wo_ref[nd:, :])
    lg_all = _dot(_rms(x1, gx_ref[...]).astype(BF16), wqk_ref[...])
    n_mem = wqk_ref.shape[1] // X_HEADS
    for hd in range(X_HEADS):
        sl = slice(hd * n_mem, (hd + 1) * n_mem)
        lg = lg_all[:, sl]
        e = jnp.exp(lg - jnp.max(lg, axis=-1, keepdims=True))
        p_ref[:, sl] = (e / jnp.sum(e, axis=-1, keepdims=True)).astype(BF16)
    x2 = x1 + _dot(p_ref[...], wvo_ref[...])
    x2_ref[...] = x2
    hf = _rms(x2, gf_ref[...]).astype(BF16)
    rl_ref[...] = _dot(hf, wr_ref[...])
    bits = lax.bitcast_convert_type(hf.astype(F32), U32)
    for c in range(_SUBLANES):
        lo = bits[:, (2 * c) * _LANES:(2 * c + 1) * _LANES]
        hi = bits[:, (2 * c + 1) * _LANES:(2 * c + 2) * _LANES]
        hf_ref[pl.ds(c, x2.shape[0], stride=_SUBLANES), :] = (hi & jnp.uint32(0xFFFF0000)) | (lo >> 16)


def _mixout(x, dsa, ml, wo, wqk, wvo, wr, gx, gf, tm):
    T = x.shape[0]
    row = lambda n: pl.BlockSpec((tm, n), lambda i: (i, 0))
    return pl.pallas_call(
        _mixout_body,
        grid=(T // tm,),
        in_specs=[row(D_MODEL), row(dsa.shape[1]), row(ml.shape[1]), _resident(wo.shape),
                  _resident(wqk.shape), _resident(wvo.shape), _resident(wr.shape), _resident(gx.shape),
                  _resident(gf.shape)],
        out_specs=[row(D_MODEL), pl.BlockSpec((tm * _SUBLANES, _LANES), lambda i: (i, 0)), row(wr.shape[1])],
        out_shape=[jax.ShapeDtypeStruct((T, D_MODEL), F32),
                   jax.ShapeDtypeStruct((T * _SUBLANES, _LANES), U32),
                   jax.ShapeDtypeStruct((T, wr.shape[1]), F32)],
        scratch_shapes=[pltpu.VMEM((tm, wqk.shape[1]), BF16)],
        compiler_params=_params(),
        name="mixout",
    )(x, dsa, ml, wo, wqk, wvo, wr, gx, gf)


def _moe_body(row_ref, gap_lo_ref, gap_hi_ref, eb_ref, hf_hbm, wg_ref, wu_ref, wd_ref, y_hbm,
              tok_ref, xbuf, ybuf, wgb, wub, wdb, gsem, ysem, *, bm, n_blk):
    e = pl.program_id(0)

    @pl.when(e == 0)
    def _():
        def clear_gap(g, carry):
            def clear(r, c):
                tok_ref[r] = 0
                return c
            return lax.fori_loop(gap_lo_ref[g], gap_hi_ref[g], clear, carry)

        def place(a, carry):
            tok_ref[row_ref[a]] = lax.shift_right_logical(a, _TOPK_SHIFT)
            return carry

        lax.fori_loop(0, gap_lo_ref.shape[0], clear_gap, 0)
        lax.fori_loop(0, row_ref.shape[0], place, 0, unroll=8)

    n_valid = eb_ref[N_EXPERTS]
    b_lo = eb_ref[e]
    b_hi = eb_ref[e + 1]

    def tok_words(tok):
        return hf_hbm.at[pl.ds(pl.multiple_of(tok * _SUBLANES, _SUBLANES), _SUBLANES)]

    def gather(blk, s):
        def issue(g, carry):
            for u in range(_SUBLANES):
                r = g * _SUBLANES + u
                pltpu.make_async_copy(tok_words(tok_ref[blk * bm + r]),
                                      xbuf.at[s, pl.ds(pl.multiple_of(r * _SUBLANES, _SUBLANES), _SUBLANES)],
                                      gsem.at[s]).start(priority=1)
            return carry
        lax.fori_loop(0, bm // _SUBLANES, issue, 0)

    def wait_gather(s):
        pltpu.make_async_copy(xbuf.at[s], xbuf.at[s], gsem.at[s]).wait()

    def y_copy(blk, s):
        return pltpu.make_async_copy(ybuf.at[s], y_hbm.at[pl.ds(pl.multiple_of(blk * bm, bm), bm)], ysem.at[s])

    @pl.when(e == 0)
    def _():
        for k in range(_GATHER_AHEAD):
            gather(k, k)

    @pl.when(b_hi > b_lo)
    def _():
        wgb[...] = wg_ref[0].astype(BF16)
        wub[...] = wu_ref[0].astype(BF16)
        wdb[...] = wd_ref[0].astype(BF16)

        def block(b, carry):
            s = b % _GATHER_SLOTS
            wait_gather(s)
            chunks = []
            for c in range(_SUBLANES):
                w = xbuf[s, pl.ds(c, bm, stride=_SUBLANES), :]
                chunks.append(lax.bitcast_convert_type(w << 16, F32).astype(BF16))
                chunks.append(lax.bitcast_convert_type(w & jnp.uint32(0xFFFF0000), F32).astype(BF16))
            xb = jnp.concatenate(chunks, axis=-1)
            gate = _dot(xb, wgb[...])
            a = gate * jax.nn.sigmoid(gate) * _dot(xb, wub[...])
            y = _dot(a.astype(BF16), wdb[...])

            nxt = jnp.minimum(b + _GATHER_AHEAD, n_blk - 1)
            for r in range(bm):
                pltpu.make_async_copy(tok_words(tok_ref[nxt * bm + r]),
                                      xbuf.at[(b + _GATHER_AHEAD) % _GATHER_SLOTS, pl.ds(r * _SUBLANES, _SUBLANES)],
                                      gsem.at[(b + _GATHER_AHEAD) % _GATHER_SLOTS]).start(priority=1)

            @pl.when(b >= 2)
            def _():
                y_copy(b - 2, b % 2).wait()

            ybuf[b % 2] = y
            y_copy(b, b % 2).start()
            return carry

        lax.fori_loop(b_lo, b_hi, block, 0)

    @pl.when(e == pl.num_programs(0) - 1)
    def _():
        for k in range(_GATHER_AHEAD):
            wait_gather((n_valid + k) % _GATHER_SLOTS)

        @pl.when(n_valid >= 2)
        def _():
            y_copy(n_valid - 2, n_valid % 2).wait()
        y_copy(n_valid - 1, (n_valid - 1) % 2).wait()
        ybuf[0] = jnp.zeros(ybuf.shape[1:], F32)

        def zero_block(b, carry):
            cp = y_copy(b, 0)
            cp.start()
            cp.wait()
            return carry

        lax.fori_loop(n_valid, n_blk, zero_block, 0)


def _moe(row, gap_lo, gap_hi, e_blk, hf, wg, wu, wd, n_blk, bm):
    D = wg.shape[1]
    wspec = lambda shape: pl.BlockSpec((1,) + shape, lambda e, *_: (e, 0, 0))
    grid_spec = pltpu.PrefetchScalarGridSpec(
        num_scalar_prefetch=4,
        grid=(N_EXPERTS,),
        in_specs=[pl.BlockSpec(memory_space=pl.ANY),
                  wspec((D, D_EXPERT)), wspec((D, D_EXPERT)), wspec((D_EXPERT, D))],
        out_specs=pl.BlockSpec(memory_space=pl.ANY),
        scratch_shapes=[pltpu.SMEM((n_blk * bm,), I32),
                        pltpu.VMEM((_GATHER_SLOTS, bm * _SUBLANES, _LANES), U32),
                        pltpu.VMEM((2, bm, D), F32),
                        pltpu.VMEM((D, D_EXPERT), BF16), pltpu.VMEM((D, D_EXPERT), BF16),
                        pltpu.VMEM((D_EXPERT, D), BF16),
                        pltpu.SemaphoreType.DMA((_GATHER_SLOTS,)), pltpu.SemaphoreType.DMA((2,))],
    )
    return pl.pallas_call(
        functools.partial(_moe_body, bm=bm, n_blk=n_blk),
        grid_spec=grid_spec,
        out_shape=jax.ShapeDtypeStruct((n_blk * bm, D), F32),
        compiler_params=_params(),
        name="moe",
    )(row, gap_lo, gap_hi, e_blk, hf, wg, wu, wd)


def _route_body(rl_ref, bias_ref, out_ref, cnt_ref, seen_ref, *, tm):
    @pl.when(pl.program_id(0) == 0)
    def _():
        seen_ref[...] = jnp.zeros(seen_ref.shape, F32)

    lg = rl_ref[...] + bias_ref[...]
    lane = lax.broadcasted_iota(I32, lg.shape, 1).astype(F32)
    first = lambda hit: jnp.min(jnp.where(hit, lane, float(_LANES)), axis=-1, keepdims=True)
    is_group = lane < N_GROUPS
    gl = jnp.where(is_group, lg, -jnp.inf)
    g_max = jnp.max(gl, axis=-1, keepdims=True)
    g_sel = first(gl == g_max)
    p_g = 1.0 / jnp.sum(jnp.where(is_group, jnp.exp(gl - g_max), 0.0), axis=-1, keepdims=True)

    e_id = lane - N_GROUPS
    in_group = (e_id >= 0) & (e_id < N_EXPERTS) & (jnp.floor(e_id / EXP_PER_GROUP) == g_sel)
    el = jnp.where(in_group, lg, -jnp.inf)
    ee = jnp.where(in_group, jnp.exp(el - jnp.max(el, axis=-1, keepdims=True)), 0.0)
    cand = jnp.where(in_group, ee / jnp.sum(ee, axis=-1, keepdims=True), -1.0)
    p1 = jnp.max(cand, axis=-1, keepdims=True)
    l1 = first(cand == p1)
    cand = jnp.where(lane == l1, -2.0, cand)
    p2 = jnp.max(cand, axis=-1, keepdims=True)
    l2 = first(cand == p2)
    g1 = p_g * p1 / (p1 + p2)
    g2 = p_g * p2 / (p1 + p2)

    oh1 = (lane == l1).astype(BF16)
    oh2 = (lane == l2).astype(BF16)
    both = oh1 + oh2
    earlier = (lax.broadcasted_iota(I32, (tm, tm), 1) < lax.broadcasted_iota(I32, (tm, tm), 0)).astype(BF16)
    before = _dot(earlier, both) + seen_ref[...]
    r1 = jnp.sum(before * oh1.astype(F32), axis=-1, keepdims=True)
    r2 = jnp.sum(before * oh2.astype(F32), axis=-1, keepdims=True)
    seen_ref[...] = seen_ref[...] + jnp.sum(both.astype(F32), axis=0, keepdims=True)
    cnt_ref[...] = seen_ref[...]

    cols = (l1 - N_GROUPS, l2 - N_GROUPS, r1, r2, g1, g2)
    out = jnp.zeros(lg.shape, F32)
    for c, v in enumerate(cols):
        out = jnp.where(lane == c, v, out)
    out_ref[...] = out


def _route_tokens(rl, bias, tm):
    T = rl.shape[0]
    return pl.pallas_call(
        functools.partial(_route_body, tm=tm),
        grid=(T // tm,),
        in_specs=[pl.BlockSpec((tm, _LANES), lambda i: (i, 0)), _resident(bias.shape)],
        out_specs=[pl.BlockSpec((tm, _LANES), lambda i: (i, 0)), pl.BlockSpec((1, _LANES), lambda i: (0, 0))],
        out_shape=[jax.ShapeDtypeStruct((T, _LANES), F32), jax.ShapeDtypeStruct((1, _LANES), F32)],
        scratch_shapes=[pltpu.VMEM((1, _LANES), F32)],
        compiler_params=_params(),
        name="route",
    )(rl, bias)


def _route(rl, b_group, b_router, bm, tm):
    N = rl.shape[0]
    bias = jnp.concatenate([b_group, b_router, jnp.zeros((_LANES - N_GROUPS - N_EXPERTS,), F32)]).reshape(1, _LANES)
    per_tok, seen = _route_tokens(rl, bias, tm)
    expert_id = per_tok[:, 0:TOPK_IN_GROUP].astype(I32)
    rank = per_tok[:, TOPK_IN_GROUP:2 * TOPK_IN_GROUP].astype(I32)
    gates = per_tok[:, 2 * TOPK_IN_GROUP:3 * TOPK_IN_GROUP]
    counts = seen[0, N_GROUPS:N_GROUPS + N_EXPERTS].astype(I32)

    A = N * TOPK_IN_GROUP
    padded = (counts + bm - 1) // bm * bm
    pad_ends = jnp.cumsum(padded)
    pad_starts = pad_ends - padded
    onehot = expert_id[..., None] == jnp.arange(N_EXPERTS, dtype=I32)
    row = (jnp.sum(jnp.where(onehot, pad_starts, 0), axis=-1) + rank).reshape(A)
    n_blk = -(-A // bm) + N_EXPERTS
    gap_lo = jnp.concatenate([pad_starts + counts, pad_ends[-1:]]).astype(I32)
    gap_hi = jnp.concatenate([pad_ends, jnp.full((1,), n_blk * bm, I32)]).astype(I32)
    e_blk = jnp.concatenate([pad_starts, pad_ends[-1:]]) // bm
    return row, gap_lo, gap_hi, gates, e_blk.astype(I32), n_blk


def _final_body(row_ref, x_ref, gate_ref, y_hbm, g_ref, out_ref, ybuf, sem, *, tm):
    i = pl.program_id(0)
    n_tiles = pl.num_programs(0)
    slot = i % _FINAL_SLOTS

    def gather(tile, s):
        def issue(g, carry):
            for u in range(_SUBLANES):
                for k in range(TOPK_IN_GROUP):
                    src = row_ref[(tile * tm + g * _SUBLANES + u) * TOPK_IN_GROUP + k]
                    pltpu.make_async_copy(y_hbm.at[pl.ds(src, 1)], ybuf.at[s, k, g, pl.ds(u, 1)],
                                          sem.at[s]).start()
            return carry
        lax.fori_loop(0, tm // _SUBLANES, issue, 0)

    def wait_rows(s):
        pltpu.make_async_copy(ybuf.at[s], ybuf.at[s], sem.at[s]).wait()

    @pl.when(i == 0)
    def _():
        for t in range(_FINAL_SLOTS - 1):
            gather(jnp.minimum(t, n_tiles - 1), t)

    wait_rows(slot)
    acc = x_ref[...]
    for k in range(TOPK_IN_GROUP):
        acc = acc + gate_ref[:, k:k + 1] * ybuf[slot, k].reshape(tm, ybuf.shape[-1])
    y = _rms(acc, g_ref[...])

    ahead = _FINAL_SLOTS - 1
    nxt = jnp.minimum(i + ahead, n_tiles - 1)
    for r in range(tm):
        for k in range(TOPK_IN_GROUP):
            src = row_ref[(nxt * tm + r) * TOPK_IN_GROUP + k]
            pltpu.make_async_copy(y_hbm.at[pl.ds(src, 1)],
                                  ybuf.at[(i + ahead) % _FINAL_SLOTS, k, r // _SUBLANES, pl.ds(r % _SUBLANES, 1)],
                                  sem.at[(i + ahead) % _FINAL_SLOTS]).start()
    out_ref[...] = y

    @pl.when(i == n_tiles - 1)
    def _():
        for t in range(1, _FINAL_SLOTS):
            wait_rows((i + t) % _FINAL_SLOTS)


def _final(row, x2, gates, y_rows, g, tm):
    T, D = x2.shape
    grid_spec = pltpu.PrefetchScalarGridSpec(
        num_scalar_prefetch=1,
        grid=(T // tm,),
        in_specs=[pl.BlockSpec((tm, D), lambda i, *_: (i, 0)),
                  pl.BlockSpec((tm, TOPK_IN_GROUP), lambda i, *_: (i, 0)),
                  pl.BlockSpec(memory_space=pl.ANY),
                  pl.BlockSpec(g.shape, lambda i, *_: (0, 0))],
        out_specs=pl.BlockSpec((tm, D), lambda i, *_: (i, 0)),
        scratch_shapes=[pltpu.VMEM((_FINAL_SLOTS, TOPK_IN_GROUP, tm // _SUBLANES, _SUBLANES, D), F32),
                        pltpu.SemaphoreType.DMA((_FINAL_SLOTS,))],
    )
    return pl.pallas_call(
        functools.partial(_final_body, tm=tm),
        grid_spec=grid_spec,
        out_shape=jax.ShapeDtypeStruct((T, D), F32),
        compiler_params=_params(),
        name="final",
    )(row, x2, gates, y_rows, g)


def _tile_sizes(T):
    pick = lambda want: want if T % want == 0 else CHUNK
    return dict(inproj=pick(256), dsa_q=pick(256), dsa_k=pick(512), mlstm=pick(256), mixout=pick(256),
                final=pick(256), route=pick(512), moe=128)


def _layer(x, mem, norm_mix_g, w_in, kv_norm_g, k_idx_norm_g, w_uk, w_uv, conv_w, conv_b, gate_b, ml_norm_g,
           w_out, norm_x_g, mem_norm_g, w_cq, w_ckv, w_co, norm_ffn_g, w_group, b_group, w_router, b_router,
           w_gate, w_up, w_down, out_g):
    T = x.shape[0]
    ts = _tile_sizes(T)
    r2 = lambda v: v.reshape(1, -1)

    w_r = _wprep(w_in, 256)
    wuk_t = jnp.transpose(w_uk, (1, 2, 0)).astype(BF16)
    wuv_t = jnp.transpose(w_uv, (1, 0, 2)).astype(BF16)

    qabs, ckv, qi, kidx, small, mqk, mv, mo = _inproj(
        x, r2(norm_mix_g), w_r, wuk_t, r2(kv_norm_g), r2(k_idx_norm_g), ts["inproj"])

    gate_rows = jnp.transpose(small[:, _S_WI:_S_MF + ML_HEADS])
    dsa_out = _dsa(qi, gate_rows, qabs, kidx, ckv, wuv_t, ts["dsa_q"], ts["dsa_k"])

    gb_col = jnp.zeros((1, _SMALL), F32).at[0, _S_MI:_S_MI + 2 * ML_HEADS].set(gate_b)
    ml_out = _mlstm(mqk, mv, small, gate_rows, mo, conv_w, r2(conv_b), gb_col, gate_b.reshape(-1, 1),
                    r2(ml_norm_g), ts["mlstm"])

    wqk, wvo = _memfold(mem, r2(mem_norm_g), w_ckv, w_cq, w_co)
    w_rt = jnp.concatenate([w_group, w_router,
                            jnp.zeros((D_MODEL, 128 - N_GROUPS - N_EXPERTS), w_group.dtype)], axis=1)
    x2, hf, rl = _mixout(x, dsa_out, ml_out, w_out.astype(BF16), wqk, wvo, w_rt.astype(BF16),
                         r2(norm_x_g), r2(norm_ffn_g), ts["mixout"])

    bm = ts["moe"]
    row, gap_lo, gap_hi, gates, e_blk, n_blk = _route(rl, b_group, b_router, bm, ts["route"])
    y_rows = _moe(row, gap_lo, gap_hi, e_blk, hf, w_gate, w_up, w_down, n_blk, bm)
    return _final(row, x2, gates, y_rows, r2(out_g), ts["final"])


def kernel(x, mem, norm_mix_g, w_in, kv_norm_g, k_idx_norm_g, w_uk, w_uv, conv_w, conv_b, gate_b, ml_norm_g,
           w_out, norm_x_g, mem_norm_g, w_cq, w_ckv, w_co, norm_ffn_g, w_group, b_group, w_router, b_router,
           w_gate, w_up, w_down, final_norm_g):
    B, T, D = x.shape
    assert B == 1 and D == D_MODEL and norm_mix_g.shape[0] == 1 and T % CHUNK == 0
    out = _layer(x[0], mem[0], norm_mix_g[0], w_in, kv_norm_g[0], k_idx_norm_g[0], w_uk[0], w_uv[0],
                 conv_w[0], conv_b[0], gate_b[0], ml_norm_g[0], w_out[0], norm_x_g[0], mem_norm_g[0],
                 w_cq[0], w_ckv[0], w_co[0], norm_ffn_g[0], w_group[0], b_group[0], w_router[0], b_router[0],
                 w_gate[0], w_up[0], w_down[0], final_norm_g)
    return out[None]
```

```python
import functools

import jax
import jax.numpy as jnp
from jax import lax
from jax.experimental import pallas as pl
from jax.experimental.pallas import tpu as pltpu

F32 = jnp.float32
BF16 = jnp.bfloat16
I32 = jnp.int32
U32 = jnp.uint32

EPS = 1e-6
CHUNK = 64
D_MODEL = 2048

DSA_HEADS = 8
DSA_HEAD_DIM = 128
DSA_LATENT = 256
IDX_HEADS = 8
IDX_DIM = 64
TOPK_MAX = 256

ML_HEADS = 4
ML_QK_DIM = 128
ML_V_DIM = 256
CONV_W = 4

X_HEADS = 4
X_HEAD_DIM = D_MODEL // X_HEADS

N_GROUPS = 4
EXP_PER_GROUP = 8
N_EXPERTS = N_GROUPS * EXP_PER_GROUP
TOPK_IN_GROUP = 2
D_EXPERT = 512

_O_DQ = 0
_O_CKV = _O_DQ + DSA_HEADS * DSA_HEAD_DIM
_O_QI = _O_CKV + DSA_LATENT
_O_KI = _O_QI + IDX_HEADS * IDX_DIM
_O_WI = _O_KI + IDX_DIM
_O_MQ = _O_WI + IDX_HEADS
_O_MK = _O_MQ + ML_HEADS * ML_QK_DIM
_O_MV = _O_MK + ML_HEADS * ML_QK_DIM
_O_MI = _O_MV + ML_HEADS * ML_V_DIM
_O_MF = _O_MI + ML_HEADS
_O_MO = _O_MF + ML_HEADS
_O_END = _O_MO + ML_HEADS * ML_V_DIM

_G_DQ = (0, 1024)
_G_CKV = (1024, 1280)
_G_QI = (1280, 1792)
_G_SMALL = (1792, 1920)
_G_MQK = (1920, 2944)
_G_MV = (2944, 3968)
_G_MO = (3968, 4992)
_W_COLS = 4992
_S_WI = IDX_DIM
_S_MI = _S_WI + IDX_HEADS
_S_MF = _S_MI + ML_HEADS
_SMALL = 128

_VMEM_LIMIT = 56 * 1024 * 1024
_INT_MIN = -(2 ** 31)
_CHUNK_SHIFT = CHUNK.bit_length() - 1
_LOG2E = 1.4426950408889634
_SUBLANES = 8
_LANES = 128
_GATHER_SLOTS = 9
_GATHER_AHEAD = _GATHER_SLOTS - 1
_TOPK_SHIFT = TOPK_IN_GROUP.bit_length() - 1
_FINAL_SLOTS = 3
_NEG = -1e30


def _rms(v, g):
    return v * lax.rsqrt(jnp.mean(v * v, axis=-1, keepdims=True) + EPS) * g


def _dot(a, b):
    return jnp.dot(a, b, preferred_element_type=F32)


def _dot_nt(a, b):
    return lax.dot_general(a, b, (((1,), (1,)), ((), ())), preferred_element_type=F32)


def _resident(shape):
    nd = len(shape)
    return pl.BlockSpec(shape, lambda *_: (0,) * nd, pipeline_mode=pl.Buffered(1))


def _params(n_axes=1):
    return pltpu.CompilerParams(dimension_semantics=("arbitrary",) * n_axes,
                                vmem_limit_bytes=_VMEM_LIMIT)


def _wprep_body(w_hbm, out_ref, wbuf, sem, *, tk):
    i = pl.program_id(0)
    slot = i % 2

    def rows(blk, s):
        return pltpu.make_async_copy(w_hbm.at[0, pl.ds(pl.multiple_of(blk * tk, tk), tk)], wbuf.at[s], sem.at[s])

    @pl.when(i == 0)
    def _():
        rows(0, 0).start()

    @pl.when(i + 1 < pl.num_programs(0))
    def _():
        rows(i + 1, 1 - slot).start()

    rows(i, slot).wait()
    w_ref = wbuf.at[slot]
    n_small = IDX_DIM + IDX_HEADS
    out_ref[:, _G_DQ[0]:_G_SMALL[0] + n_small] = w_ref[:, _O_DQ:_O_MQ].astype(BF16)
    out_ref[:, _G_SMALL[0] + n_small:_G_SMALL[0] + n_small + 2 * ML_HEADS] = w_ref[:, _O_MI:_O_MO].astype(BF16)
    out_ref[:, _G_SMALL[0] + n_small + 2 * ML_HEADS:_G_SMALL[1]] = jnp.zeros(
        (out_ref.shape[0], _SMALL - n_small - 2 * ML_HEADS), BF16)
    out_ref[:, _G_MQK[0]:_G_MV[1]] = w_ref[:, _O_MQ:_O_MI].astype(BF16)
    out_ref[:, _G_MO[0]:_G_MO[1]] = w_ref[:, _O_MO:_O_END].astype(BF16)


def _wprep(w_in, tk):
    K = w_in.shape[1]
    return pl.pallas_call(
        functools.partial(_wprep_body, tk=tk),
        grid=(K // tk,),
        in_specs=[pl.BlockSpec(memory_space=pl.ANY)],
        out_specs=pl.BlockSpec((tk, _W_COLS), lambda i: (i, 0)),
        out_shape=jax.ShapeDtypeStruct((K, _W_COLS), BF16),
        scratch_shapes=[pltpu.VMEM((2, tk, w_in.shape[2]), F32), pltpu.SemaphoreType.DMA((2,))],
        compiler_params=_params(),
        name="wprep",
    )(w_in)


def _inproj_body(x_ref, g_ref, w_ref, wuk_ref, kvg_ref, kig_ref,
                 qabs_ref, ckv_ref, qi_ref, kidx_ref, small_ref, mqk_ref, mv_ref, mo_ref):
    h = _rms(x_ref[...], g_ref[...]).astype(BF16)

    def proj(grp):
        return _dot(h, w_ref[:, grp[0]:grp[1]])

    dq = proj(_G_DQ)
    for hd in range(DSA_HEADS):
        qh = dq[:, hd * DSA_HEAD_DIM:(hd + 1) * DSA_HEAD_DIM].astype(BF16)
        qa = _dot(qh, wuk_ref[hd]) * (DSA_HEAD_DIM ** -0.5 * _LOG2E)
        qabs_ref[:, hd * DSA_LATENT:(hd + 1) * DSA_LATENT] = qa.astype(BF16)
    ckv_ref[...] = _rms(proj(_G_CKV), kvg_ref[...]).astype(BF16)
    qi_ref[...] = (proj(_G_QI) * (IDX_DIM ** -0.5)).astype(BF16)
    small = proj(_G_SMALL)
    small_ref[...] = small
    kidx_ref[...] = _rms(small[:, :IDX_DIM], kig_ref[...]).astype(BF16)
    mqk_ref[...] = proj(_G_MQK)
    mv_ref[...] = proj(_G_MV).astype(BF16)
    mo_ref[...] = proj(_G_MO)


def _inproj(x, g, w, wuk, kvg, kig, tm):
    T = x.shape[0]
    row = lambda n: pl.BlockSpec((tm, n), lambda i: (i, 0))
    outs = [(8 * DSA_LATENT, BF16), (DSA_LATENT, BF16), (IDX_HEADS * IDX_DIM, BF16), (IDX_DIM, BF16),
            (_SMALL, F32), (2 * ML_HEADS * ML_QK_DIM, F32), (ML_HEADS * ML_V_DIM, BF16),
            (ML_HEADS * ML_V_DIM, F32)]
    return pl.pallas_call(
        _inproj_body,
        grid=(T // tm,),
        in_specs=[row(D_MODEL), _resident(g.shape), _resident(w.shape), _resident(wuk.shape),
                  _resident(kvg.shape), _resident(kig.shape)],
        out_specs=[row(n) for n, _ in outs],
        out_shape=[jax.ShapeDtypeStruct((T, n), dt) for n, dt in outs],
        compiler_params=_params(),
        name="inproj",
    )(x, g, w, wuk, kvg, kig)


def _sublane_fold(v, op, rows=_SUBLANES, ways=4):
    groups = [v[r * rows:(r + 1) * rows, :] for r in range(v.shape[0] // rows)]
    accs = groups[:ways]
    for r in range(ways, len(groups)):
        accs[r % ways] = op(accs[r % ways], groups[r])
    while len(accs) > 1:
        accs = [op(accs[k], accs[k + 1]) if k + 1 < len(accs) else accs[k] for k in range(0, len(accs), 2)]
    return accs[0]


def _bit_transpose32(words):
    a = list(words)
    j, m = 16, 0x0000FFFF
    while j:
        k = 0
        while k < 32:
            t = (a[k] ^ (a[k + j] >> j)) & jnp.uint32(m)
            a[k] = a[k] ^ t
            a[k + j] = a[k + j] ^ (t << j)
            k = (k + j + 1) & ~j
        j >>= 1
        m = (m ^ (m << j)) & 0xFFFFFFFF
    return a


def _dsa_body(qi_ref, wrow_ref, qabs_ref, kidx_ref, ckv_ref, ckvt_ref, wuv_ref, out_ref,
              key_ref, planes_ref, alive_ref, lga_ref, lgb_ref, m_ref, l_ref, acc_ref, *, tq, tk, topk, nbits_idx):
    i = pl.program_id(0)
    n_kb = ((i + 1) * tq + tk - 1) // tk
    w_rows = wrow_ref[0:IDX_HEADS, :] * (IDX_HEADS ** -0.5)
    q_chunk = (i * tq + lax.broadcasted_iota(I32, (1, tq), 1)) >> _CHUNK_SHIFT

    def key_pos(j):
        return j * tk + lax.broadcasted_iota(I32, (tk, 1), 0)

    plane_rows = tk // 32

    @pl.when(i == 0)
    def _():
        planes_ref[...] = jnp.zeros(planes_ref.shape, U32)

    def score_block(j, carry, *, all_admissible):
        kx = kidx_ref[pl.ds(pl.multiple_of(j * tk, tk), tk), :]
        s = jnp.zeros((tk, tq), F32)
        for hd in range(IDX_HEADS):
            d = _dot_nt(kx, qi_ref[:, hd * IDX_DIM:(hd + 1) * IDX_DIM])
            s = s + w_rows[hd:hd + 1, :] * jnp.maximum(d, 0.0)
        bits = lax.bitcast_convert_type(s, I32)
        key = bits ^ ((bits >> 31) & 0x7FFFFFFF)
        if not all_admissible:
            key = jnp.where((key_pos(j) >> _CHUNK_SHIFT) <= q_chunk, key, _INT_MIN)
        key_ref[j] = key
        u = lax.bitcast_convert_type(key, U32) ^ jnp.uint32(0x80000000)
        for h in range(plane_rows // _SUBLANES):
            base = h * 32 * _SUBLANES
            planes = _bit_transpose32([u[base + _SUBLANES * v:base + _SUBLANES * (v + 1), :] for v in range(32)])
            row0 = pl.multiple_of(j * plane_rows + h * _SUBLANES, _SUBLANES)
            for b in range(32):
                planes_ref[b, pl.ds(row0, _SUBLANES), :] = planes[31 - b]
        return carry

    n_open = (i * tq) // tk
    lax.fori_loop(0, n_open, functools.partial(score_block, all_admissible=True), 0)
    lax.fori_loop(n_open, n_kb, functools.partial(score_block, all_admissible=False), 0)

    def count(pred):
        def body(j, acc):
            hit = pred(key_ref[j], key_pos(j)).astype(I32)
            return acc + _sublane_fold(hit, jnp.add)
        acc = lax.fori_loop(0, n_kb, body, jnp.zeros((8, tq), I32))
        return jnp.sum(acc, axis=0, keepdims=True)

    word_row = lax.broadcasted_iota(I32, (alive_ref.shape[0], 1), 0)
    alive_ref[...] = jnp.where(word_row < n_kb * plane_rows, jnp.full(alive_ref.shape, 0xFFFFFFFF, U32),
                               jnp.uint32(0))

    def radix_select(n_rows):
        def ones_count(words):
            return jnp.sum(_sublane_fold(lax.population_count(words).astype(I32), jnp.add), axis=0, keepdims=True)

        def select_bit(it, carry):
            need, t_u = carry
            b = 31 - it
            alive = alive_ref[0:n_rows, :]
            ones = alive & planes_ref[b, pl.ds(0, n_rows), :]
            n_ones = ones_count(ones)
            take = n_ones >= need
            alive_ref[0:n_rows, :] = jnp.where(take, ones, alive ^ ones)
            bit = lax.shift_left(jnp.uint32(1), jnp.asarray(b, U32))
            return jnp.where(take, need, need - n_ones), jnp.where(take, t_u | bit, t_u)

        need, t_u = lax.fori_loop(0, 32, select_bit, (jnp.full((1, tq), topk, I32), jnp.zeros((1, tq), U32)))
        return need, t_u, ones_count(alive_ref[0:n_rows, :])

    all_rows = alive_ref.shape[0]
    if all_rows % (2 * _SUBLANES) == 0:
        need, t_u, n_equal = lax.cond(n_kb * plane_rows <= all_rows // 2,
                                      lambda: radix_select(all_rows // 2), lambda: radix_select(all_rows))
    else:
        need, t_u, n_equal = radix_select(all_rows)
    short = t_u == 0
    t = jnp.maximum(lax.bitcast_convert_type(t_u ^ jnp.uint32(0x80000000), I32), _INT_MIN + 1)
    all_pos = jnp.int32(2 ** nbits_idx - 1)
    has_ties = (n_equal > need) & jnp.logical_not(short)
    n_tie_take = jnp.where(has_ties, need, all_pos)

    def tie_cutoff():
        def pos_bit(b, c):
            cand = c + lax.shift_left(jnp.int32(1), nbits_idx - 1 - b)
            f = count(lambda kb, pos: (kb == t) & (pos < cand))
            return jnp.where(f <= n_tie_take, cand, c)
        return lax.fori_loop(0, nbits_idx, pos_bit, jnp.zeros((1, tq), I32))

    cut = lax.cond(jnp.max(has_ties.astype(I32)) > 0, tie_cutoff, lambda: jnp.full((1, tq), all_pos, I32))

    m_ref[...] = jnp.full(m_ref.shape, _NEG, F32)
    l_ref[...] = jnp.zeros(l_ref.shape, F32)
    acc_ref[...] = jnp.zeros(acc_ref.shape, F32)
    last_blk = ckvt_ref.shape[0] - 1

    def logits(j, lg_buf):
        c_blk = ckv_ref[pl.ds(pl.multiple_of(jnp.minimum(j, last_blk) * tk, tk), tk), :]
        kb = key_ref[j]
        sel = (kb > t) | ((kb == t) & (key_pos(j) < cut))
        bias = jnp.where(sel, 0.0, _NEG).astype(F32)
        for hd in range(DSA_HEADS):
            lg_buf[hd] = _dot_nt(c_blk, qabs_ref[:, hd * DSA_LATENT:(hd + 1) * DSA_LATENT]) + bias

    def accumulate(j, lg_buf):
        c_blk_t = ckvt_ref[jnp.minimum(j, last_blk)]
        for hd in range(DSA_HEADS):
            lg = lg_buf[hd]
            m_old = m_ref[hd:hd + 1, :]
            m_new = jnp.maximum(m_old, jnp.max(_sublane_fold(lg, jnp.maximum, ways=1), axis=0, keepdims=True))
            p = jnp.exp2(lg - m_new)
            alpha = jnp.exp2(m_old - m_new)
            l_ref[hd:hd + 1, :] = alpha * l_ref[hd:hd + 1, :] + jnp.sum(_sublane_fold(p, jnp.add, ways=1), axis=0,
                                                                         keepdims=True)
            acc_ref[hd] = alpha * acc_ref[hd] + _dot(c_blk_t, p.astype(BF16))
            m_ref[hd:hd + 1, :] = m_new

    logits(0, lga_ref)

    def attn_pair(mi, carry):
        ja = 2 * mi
        accumulate(ja, lga_ref)
        logits(ja + 1, lgb_ref)
        accumulate(ja + 1, lgb_ref)
        logits(jnp.minimum(ja + 2, n_kb - 1), lga_ref)
        return carry

    lax.fori_loop(0, n_kb // 2, attn_pair, 0)

    @pl.when(n_kb % 2 == 1)
    def _():
        accumulate(n_kb - 1, lga_ref)

    for hd in range(DSA_HEADS):
        o_lat = (acc_ref[hd] / l_ref[hd:hd + 1, :]).T.astype(BF16)
        out_ref[:, hd * DSA_HEAD_DIM:(hd + 1) * DSA_HEAD_DIM] = _dot(o_lat, wuv_ref[hd]).astype(BF16)


def _dsa(qi, wrows, qabs, kidx, ckv, wuv, tq, tk):
    T = qi.shape[0]
    topk = min(TOPK_MAX, T // 4)
    n_kb = T // tk
    ckvt = jnp.transpose(ckv.reshape(n_kb, tk, DSA_LATENT), (0, 2, 1))
    row = lambda n: pl.BlockSpec((tq, n), lambda i: (i, 0))
    body = functools.partial(_dsa_body, tq=tq, tk=tk, topk=topk, nbits_idx=int(T).bit_length())
    return pl.pallas_call(
        body,
        grid=(T // tq,),
        in_specs=[row(qi.shape[1]), pl.BlockSpec((wrows.shape[0], tq), lambda i: (0, i)), row(qabs.shape[1]),
                  _resident(kidx.shape), _resident(ckv.shape), _resident(ckvt.shape), _resident(wuv.shape)],
        out_specs=row(DSA_HEADS * DSA_HEAD_DIM),
        out_shape=jax.ShapeDtypeStruct((T, DSA_HEADS * DSA_HEAD_DIM), BF16),
        scratch_shapes=[pltpu.VMEM((n_kb + n_kb % 2, tk, tq), I32),
                        pltpu.VMEM((32, T // 32, tq), U32), pltpu.VMEM((T // 32, tq), U32),
                        pltpu.VMEM((DSA_HEADS, tk, tq), F32), pltpu.VMEM((DSA_HEADS, tk, tq), F32),
                        pltpu.VMEM((DSA_HEADS, tq), F32),
                        pltpu.VMEM((DSA_HEADS, tq), F32), pltpu.VMEM((DSA_HEADS, DSA_LATENT, tq), F32)],
        compiler_params=_params(),
        name="dsa",
    )(qi, wrows, qabs, kidx, ckv, ckvt, wuv)


def _log_sigmoid(v):
    return jnp.minimum(v, 0.0) - jnp.log1p(jnp.exp(-jnp.abs(v)))


def _chunk_cumsum(v, axis):
    pos = lax.broadcasted_iota(I32, v.shape, axis) & (CHUNK - 1)
    d = 1
    while d < CHUNK:
        v = v + jnp.where(pos >= d, pltpu.roll(v, d, axis=axis), 0.0)
        d *= 2
    return v


def _mlstm_body(mqk_ref, mv_ref, small_ref, gt_ref, mo_ref, cw_ref, cb_ref, gbc_ref, gbr_ref, ng_ref,
                out_ref, xe_ref, c_ref, n_ref, m_ref, hs_ref, *, rows):
    @pl.when(pl.program_id(0) == 0)
    def _():
        xe_ref[0:8, :] = jnp.zeros((8, xe_ref.shape[1]), F32)
        c_ref[...] = jnp.zeros(c_ref.shape, F32)
        n_ref[...] = jnp.zeros(n_ref.shape, F32)
        m_ref[...] = jnp.zeros(m_ref.shape, F32)

    x = mqk_ref[...]
    xe_ref[8:8 + rows, :] = x
    y = cb_ref[...]
    for j in range(CONV_W - 1):
        y = y + xe_ref[5 + j:5 + j + rows, :] * cw_ref[j:j + 1, :]
    y = y + x * cw_ref[CONV_W - 1:CONV_W, :]
    xe_ref[0:8, :] = x[rows - 8:rows, :]
    qk = y * jax.nn.sigmoid(y)
    nqk = ML_HEADS * ML_QK_DIM
    q_all = (qk[:, :nqk] * (ML_QK_DIM ** -0.5)).astype(BF16)
    k_all = qk[:, nqk:]

    g_col = small_ref[...] + gbc_ref[...]
    g_row = gt_ref[...] + gbr_ref[...]
    b_col = _chunk_cumsum(_log_sigmoid(g_col), 0)
    b_row = _chunk_cumsum(_log_sigmoid(g_row), 1)

    tri = lax.broadcasted_iota(I32, (CHUNK, CHUNK), 1) <= lax.broadcasted_iota(I32, (CHUNK, CHUNK), 0)

    heads = range(ML_HEADS)
    for c in range(rows // CHUNK):
        lo, hi = c * CHUNK, (c + 1) * CHUNK
        bc = [b_col[lo:hi, _S_MF + hd:_S_MF + hd + 1] for hd in heads]
        lic = [g_col[lo:hi, _S_MI + hd:_S_MI + hd + 1] for hd in heads]
        br = [b_row[ML_HEADS + hd:ML_HEADS + hd + 1, lo:hi] for hd in heads]
        lir = [g_row[hd:hd + 1, lo:hi] for hd in heads]
        g_tot = [bc[hd][CHUNK - 1:CHUNK, :] for hd in heads]
        m_prev = [m_ref[hd][:, 0:1] for hd in heads]
        qh = [q_all[lo:hi, hd * ML_QK_DIM:(hd + 1) * ML_QK_DIM] for hd in heads]
        kh = [k_all[lo:hi, hd * ML_QK_DIM:(hd + 1) * ML_QK_DIM] for hd in heads]
        vh = [mv_ref[lo:hi, hd * ML_V_DIM:(hd + 1) * ML_V_DIM] for hd in heads]
        c_prev = [c_ref[hd] for hd in heads]
        n_prev = [n_ref[hd] for hd in heads]

        dmat = [jnp.where(tri, bc[hd] - br[hd] + lir[hd], -jnp.inf) for hd in heads]
        inter = [bc[hd] + m_prev[hd] for hd in heads]
        m_t = [jnp.maximum(inter[hd], jnp.max(dmat[hd], axis=-1, keepdims=True)) for hd in heads]
        w_intra = [jnp.exp(dmat[hd] - m_t[hd]) for hd in heads]
        a_inter = [jnp.exp(inter[hd] - m_t[hd]) for hd in heads]
        m_new = [jnp.maximum(g_tot[hd] + m_prev[hd],
                             jnp.max(g_tot[hd] - br[hd] + lir[hd], axis=-1, keepdims=True)) for hd in heads]
        a_state = [jnp.exp(g_tot[hd] + m_prev[hd] - m_new[hd]) for hd in heads]
        wk = [jnp.exp(g_tot[hd] - bc[hd] + lic[hd] - m_new[hd]) * kh[hd] for hd in heads]

        s_qk = [_dot_nt(qh[hd], kh[hd].astype(BF16)) * w_intra[hd] for hd in heads]
        read = [_dot(qh[hd], c_prev[hd].astype(BF16)) for hd in heads]
        d_c = [_dot(wk[hd].T.astype(BF16), vh[hd]) for hd in heads]
        num = [a_inter[hd] * read[hd] + _dot(s_qk[hd].astype(BF16), vh[hd]) for hd in heads]
        den = [a_inter[hd] * jnp.sum(qh[hd].astype(F32) * n_prev[hd], axis=-1, keepdims=True)
               + jnp.sum(s_qk[hd], axis=-1, keepdims=True) for hd in heads]
        for hd in heads:
            hs_ref[lo:hi, hd * ML_V_DIM:(hd + 1) * ML_V_DIM] = (
                num[hd] / jnp.maximum(jnp.abs(den[hd]), jnp.exp(-m_t[hd])))
        for hd in heads:
            c_ref[hd] = a_state[hd] * c_prev[hd] + d_c[hd]
            n_ref[hd] = a_state[hd] * n_prev[hd] + jnp.sum(wk[hd], axis=0, keepdims=True)
            m_ref[hd] = jnp.broadcast_to(m_new[hd], m_ref.shape[1:])

    for hd in range(ML_HEADS):
        sl = slice(hd * ML_V_DIM, (hd + 1) * ML_V_DIM)
        out_ref[:, sl] = (_rms(hs_ref[:, sl], ng_ref[:, sl]) * jax.nn.sigmoid(mo_ref[:, sl])).astype(BF16)


def _mlstm(mqk, mv, small, gt, mo, cw, cb, gbc, gbr, ng, rows):
    T = mqk.shape[0]
    row = lambda n: pl.BlockSpec((rows, n), lambda i: (i, 0))
    nv = ML_HEADS * ML_V_DIM
    return pl.pallas_call(
        functools.partial(_mlstm_body, rows=rows),
        grid=(T // rows,),
        in_specs=[row(mqk.shape[1]), row(nv), row(_SMALL), pl.BlockSpec((8, rows), lambda i: (1, i)), row(nv),
                  _resident(cw.shape), _resident(cb.shape), _resident(gbc.shape), _resident(gbr.shape),
                  _resident(ng.shape)],
        out_specs=row(nv),
        out_shape=jax.ShapeDtypeStruct((T, nv), BF16),
        scratch_shapes=[pltpu.VMEM((rows + 8, mqk.shape[1]), F32),
                        pltpu.VMEM((ML_HEADS, ML_QK_DIM, ML_V_DIM), F32),
                        pltpu.VMEM((ML_HEADS, 1, ML_QK_DIM), F32),
                        pltpu.VMEM((ML_HEADS, 1, 128), F32),
                        pltpu.VMEM((rows, nv), F32)],
        compiler_params=_params(),
        name="mlstm",
    )(mqk, mv, small, gt, mo, cw, cb, gbc, gbr, ng)


def _memfold_body(mem_ref, g_ref, wk_ref, wv_ref, wq_ref, wc_ref, wqk_ref, wvo_ref):
    mn = _rms(mem_ref[...], g_ref[...]).astype(BF16)
    k = _dot(mn, wk_ref[...].astype(BF16)).astype(BF16)
    v = _dot(mn, wv_ref[...].astype(BF16)).astype(BF16)
    wqk_ref[...] = (_dot_nt(wq_ref[...].astype(BF16), k) * (X_HEAD_DIM ** -0.5)).astype(BF16)
    wvo_ref[...] = _dot(v, wc_ref[...].astype(BF16)).astype(BF16)


def _memfold(mem, g, w_ckv, w_cq, w_co):
    M, D = mem.shape
    dh = X_HEAD_DIM
    return pl.pallas_call(
        _memfold_body,
        grid=(X_HEADS,),
        in_specs=[_resident(mem.shape), _resident(g.shape),
                  pl.BlockSpec((D, dh), lambda h: (0, h)), pl.BlockSpec((D, dh), lambda h: (0, X_HEADS + h)),
                  pl.BlockSpec((D, dh), lambda h: (0, h)), pl.BlockSpec((dh, D), lambda h: (h, 0))],
        out_specs=[pl.BlockSpec((D, M), lambda h: (0, h)), pl.BlockSpec((M, D), lambda h: (h, 0))],
        out_shape=[jax.ShapeDtypeStruct((D, X_HEADS * M), BF16), jax.ShapeDtypeStruct((X_HEADS * M, D), BF16)],
        compiler_params=_params(),
        name="memfold",
    )(mem, g, w_ckv, w_ckv, w_cq, w_co)


def _mixout_body(x_ref, dsa_ref, ml_ref, wo_ref, wqk_ref, wvo_ref, wr_ref, gx_ref, gf_ref,
                 x2_ref, hf_ref, rl_ref, p_ref):
    nd = dsa_ref.shape[1]
    x1 = x_ref[...] + _dot(dsa_ref[...], wo_ref[0:nd, :]) + _dot(ml_ref[...], wo_ref[nd:, :])
    lg_all = _dot(_rms(x1, gx_ref[...]).astype(BF16), wqk_ref[...])
    n_mem = wqk_ref.shape[1] // X_HEADS
    for hd in range(X_HEADS):
        sl = slice(hd * n_mem, (hd + 1) * n_mem)
        lg = lg_all[:, sl]
        e = jnp.exp(lg - jnp.max(lg, axis=-1, keepdims=True))
        p_ref[:, sl] = (e / jnp.sum(e, axis=-1, keepdims=True)).astype(BF16)
    x2 = x1 + _dot(p_ref[...], wvo_ref[...])
    x2_ref[...] = x2
    hf = _rms(x2, gf_ref[...]).astype(BF16)
    rl_ref[...] = _dot(hf, wr_ref[...])
    bits = lax.bitcast_convert_type(hf.astype(F32), U32)
    for c in range(_SUBLANES):
        lo = bits[:, (2 * c) * _LANES:(2 * c + 1) * _LANES]
        hi = bits[:, (2 * c + 1) * _LANES:(2 * c + 2) * _LANES]
        hf_ref[pl.ds(c, x2.shape[0], stride=_SUBLANES), :] = (hi & jnp.uint32(0xFFFF0000)) | (lo >> 16)


def _mixout(x, dsa, ml, wo, wqk, wvo, wr, gx, gf, tm):
    T = x.shape[0]
    row = lambda n: pl.BlockSpec((tm, n), lambda i: (i, 0))
    return pl.pallas_call(
        _mixout_body,
        grid=(T // tm,),
        in_specs=[row(D_MODEL), row(dsa.shape[1]), row(ml.shape[1]), _resident(wo.shape),
                  _resident(wqk.shape), _resident(wvo.shape), _resident(wr.shape), _resident(gx.shape),
                  _resident(gf.shape)],
        out_specs=[row(D_MODEL), pl.BlockSpec((tm * _SUBLANES, _LANES), lambda i: (i, 0)), row(wr.shape[1])],
        out_shape=[jax.ShapeDtypeStruct((T, D_MODEL), F32),
                   jax.ShapeDtypeStruct((T * _SUBLANES, _LANES), U32),
                   jax.ShapeDtypeStruct((T, wr.shape[1]), F32)],
        scratch_shapes=[pltpu.VMEM((tm, wqk.shape[1]), BF16)],
        compiler_params=_params(),
        name="mixout",
    )(x, dsa, ml, wo, wqk, wvo, wr, gx, gf)


def _moe_body(row_ref, gap_lo_ref, gap_hi_ref, eb_ref, hf_hbm, wg_ref, wu_ref, wd_ref, y_hbm,
              tok_ref, xbuf, ybuf, wgb, wub, wdb, gsem, ysem, *, bm, n_blk):
    e = pl.program_id(0)

    @pl.when(e == 0)
    def _():
        def clear_gap(g, carry):
            def clear(r, c):
                tok_ref[r] = 0
                return c
            return lax.fori_loop(gap_lo_ref[g], gap_hi_ref[g], clear, carry)

        def place(a, carry):
            tok_ref[row_ref[a]] = lax.shift_right_logical(a, _TOPK_SHIFT)
            return carry

        lax.fori_loop(0, gap_lo_ref.shape[0], clear_gap, 0)
        lax.fori_loop(0, row_ref.shape[0], place, 0, unroll=8)

    n_valid = eb_ref[N_EXPERTS]
    b_lo = eb_ref[e]
    b_hi = eb_ref[e + 1]

    def tok_words(tok):
        return hf_hbm.at[pl.ds(pl.multiple_of(tok * _SUBLANES, _SUBLANES), _SUBLANES)]

    def gather(blk, s):
        def issue(g, carry):
            for u in range(_SUBLANES):
                r = g * _SUBLANES + u
                pltpu.make_async_copy(tok_words(tok_ref[blk * bm + r]),
                                      xbuf.at[s, pl.ds(pl.multiple_of(r * _SUBLANES, _SUBLANES), _SUBLANES)],
                                      gsem.at[s]).start(priority=1)
            return carry
        lax.fori_loop(0, bm // _SUBLANES, issue, 0)

    def wait_gather(s):
        pltpu.make_async_copy(xbuf.at[s], xbuf.at[s], gsem.at[s]).wait()

    def y_copy(blk, s):
        return pltpu.make_async_copy(ybuf.at[s], y_hbm.at[pl.ds(pl.multiple_of(blk * bm, bm), bm)], ysem.at[s])

    @pl.when(e == 0)
    def _():
        for k in range(_GATHER_AHEAD):
            gather(k, k)

    @pl.when(b_hi > b_lo)
    def _():
        wgb[...] = wg_ref[0].astype(BF16)
        wub[...] = wu_ref[0].astype(BF16)
        wdb[...] = wd_ref[0].astype(BF16)

        def block(b, carry):
            s = b % _GATHER_SLOTS
            wait_gather(s)
            chunks = []
            for c in range(_SUBLANES):
                w = xbuf[s, pl.ds(c, bm, stride=_SUBLANES), :]
                chunks.append(lax.bitcast_convert_type(w << 16, F32).astype(BF16))
                chunks.append(lax.bitcast_convert_type(w & jnp.uint32(0xFFFF0000), F32).astype(BF16))
            xb = jnp.concatenate(chunks, axis=-1)
            gate = _dot(xb, wgb[...])
            a = gate * jax.nn.sigmoid(gate) * _dot(xb, wub[...])
            y = _dot(a.astype(BF16), wdb[...])

            nxt = jnp.minimum(b + _GATHER_AHEAD, n_blk - 1)
            for r in range(bm):
                pltpu.make_async_copy(tok_words(tok_ref[nxt * bm + r]),
                                      xbuf.at[(b + _GATHER_AHEAD) % _GATHER_SLOTS, pl.ds(r * _SUBLANES, _SUBLANES)],
                                      gsem.at[(b + _GATHER_AHEAD) % _GATHER_SLOTS]).start(priority=1)

            @pl.when(b >= 2)
            def _():
                y_copy(b - 2, b % 2).wait()

            ybuf[b % 2] = y
            y_copy(b, b % 2).start()
            return carry

        lax.fori_loop(b_lo, b_hi, block, 0)

    @pl.when(e == pl.num_programs(0) - 1)
    def _():
        for k in range(_GATHER_AHEAD):
            wait_gather((n_valid + k) % _GATHER_SLOTS)

        @pl.when(n_valid >= 2)
        def _():
            y_copy(n_valid - 2, n_valid % 2).wait()
        y_copy(n_valid - 1, (n_valid - 1) % 2).wait()
        ybuf[0] = jnp.zeros(ybuf.shape[1:], F32)

        def zero_block(b, carry):
            cp = y_copy(b, 0)
            cp.start()
            cp.wait()
            return carry

        lax.fori_loop(n_valid, n_blk, zero_block, 0)


def _moe(row, gap_lo, gap_hi, e_blk, hf, wg, wu, wd, n_blk, bm):
    D = wg.shape[1]
    wspec = lambda shape: pl.BlockSpec((1,) + shape, lambda e, *_: (e, 0, 0))
    grid_spec = pltpu.PrefetchScalarGridSpec(
        num_scalar_prefetch=4,
        grid=(N_EXPERTS,),
        in_specs=[pl.BlockSpec(memory_space=pl.ANY),
                  wspec((D, D_EXPERT)), wspec((D, D_EXPERT)), wspec((D_EXPERT, D))],
        out_specs=pl.BlockSpec(memory_space=pl.ANY),
        scratch_shapes=[pltpu.SMEM((n_blk * bm,), I32),
                        pltpu.VMEM((_GATHER_SLOTS, bm * _SUBLANES, _LANES), U32),
                        pltpu.VMEM((2, bm, D), F32),
                        pltpu.VMEM((D, D_EXPERT), BF16), pltpu.VMEM((D, D_EXPERT), BF16),
                        pltpu.VMEM((D_EXPERT, D), BF16),
                        pltpu.SemaphoreType.DMA((_GATHER_SLOTS,)), pltpu.SemaphoreType.DMA((2,))],
    )
    return pl.pallas_call(
        functools.partial(_moe_body, bm=bm, n_blk=n_blk),
        grid_spec=grid_spec,
        out_shape=jax.ShapeDtypeStruct((n_blk * bm, D), F32),
        compiler_params=_params(),
        name="moe",
    )(row, gap_lo, gap_hi, e_blk, hf, wg, wu, wd)


def _route_body(rl_ref, bias_ref, out_ref, cnt_ref, seen_ref, *, tm):
    @pl.when(pl.program_id(0) == 0)
    def _():
        seen_ref[...] = jnp.zeros(seen_ref.shape, F32)

    lg = rl_ref[...] + bias_ref[...]
    lane = lax.broadcasted_iota(I32, lg.shape, 1).astype(F32)
    first = lambda hit: jnp.min(jnp.where(hit, lane, float(_LANES)), axis=-1, keepdims=True)
    is_group = lane < N_GROUPS
    gl = jnp.where(is_group, lg, -jnp.inf)
    g_max = jnp.max(gl, axis=-1, keepdims=True)
    g_sel = first(gl == g_max)
    p_g = 1.0 / jnp.sum(jnp.where(is_group, jnp.exp(gl - g_max), 0.0), axis=-1, keepdims=True)

    e_id = lane - N_GROUPS
    in_group = (e_id >= 0) & (e_id < N_EXPERTS) & (jnp.floor(e_id / EXP_PER_GROUP) == g_sel)
    el = jnp.where(in_group, lg, -jnp.inf)
    ee = jnp.where(in_group, jnp.exp(el - jnp.max(el, axis=-1, keepdims=True)), 0.0)
    cand = jnp.where(in_group, ee / jnp.sum(ee, axis=-1, keepdims=True), -1.0)
    p1 = jnp.max(cand, axis=-1, keepdims=True)
    l1 = first(cand == p1)
    cand = jnp.where(lane == l1, -2.0, cand)
    p2 = jnp.max(cand, axis=-1, keepdims=True)
    l2 = first(cand == p2)
    g1 = p_g * p1 / (p1 + p2)
    g2 = p_g * p2 / (p1 + p2)

    oh1 = (lane == l1).astype(BF16)
    oh2 = (lane == l2).astype(BF16)
    both = oh1 + oh2
    earlier = (lax.broadcasted_iota(I32, (tm, tm), 1) < lax.broadcasted_iota(I32, (tm, tm), 0)).astype(BF16)
    before = _dot(earlier, both) + seen_ref[...]
    r1 = jnp.sum(before * oh1.astype(F32), axis=-1, keepdims=True)
    r2 = jnp.sum(before * oh2.astype(F32), axis=-1, keepdims=True)
    seen_ref[...] = seen_ref[...] + jnp.sum(both.astype(F32), axis=0, keepdims=True)
    cnt_ref[...] = seen_ref[...]

    cols = (l1 - N_GROUPS, l2 - N_GROUPS, r1, r2, g1, g2)
    out = jnp.zeros(lg.shape, F32)
    for c, v in enumerate(cols):
        out = jnp.where(lane == c, v, out)
    out_ref[...] = out


def _route_tokens(rl, bias, tm):
    T = rl.shape[0]
    return pl.pallas_call(
        functools.partial(_route_body, tm=tm),
        grid=(T // tm,),
        in_specs=[pl.BlockSpec((tm, _LANES), lambda i: (i, 0)), _resident(bias.shape)],
        out_specs=[pl.BlockSpec((tm, _LANES), lambda i: (i, 0)), pl.BlockSpec((1, _LANES), lambda i: (0, 0))],
        out_shape=[jax.ShapeDtypeStruct((T, _LANES), F32), jax.ShapeDtypeStruct((1, _LANES), F32)],
        scratch_shapes=[pltpu.VMEM((1, _LANES), F32)],
        compiler_params=_params(),
        name="route",
    )(rl, bias)


def _route(rl, b_group, b_router, bm, tm):
    N = rl.shape[0]
    bias = jnp.concatenate([b_group, b_router, jnp.zeros((_LANES - N_GROUPS - N_EXPERTS,), F32)]).reshape(1, _LANES)
    per_tok, seen = _route_tokens(rl, bias, tm)
    expert_id = per_tok[:, 0:TOPK_IN_GROUP].astype(I32)
    rank = per_tok[:, TOPK_IN_GROUP:2 * TOPK_IN_GROUP].astype(I32)
    gates = per_tok[:, 2 * TOPK_IN_GROUP:3 * TOPK_IN_GROUP]
    counts = seen[0, N_GROUPS:N_GROUPS + N_EXPERTS].astype(I32)

    A = N * TOPK_IN_GROUP
    padded = (counts + bm - 1) // bm * bm
    pad_ends = jnp.cumsum(padded)
    pad_starts = pad_ends - padded
    onehot = expert_id[..., None] == jnp.arange(N_EXPERTS, dtype=I32)
    row = (jnp.sum(jnp.where(onehot, pad_starts, 0), axis=-1) + rank).reshape(A)
    n_blk = -(-A // bm) + N_EXPERTS
    gap_lo = jnp.concatenate([pad_starts + counts, pad_ends[-1:]]).astype(I32)
    gap_hi = jnp.concatenate([pad_ends, jnp.full((1,), n_blk * bm, I32)]).astype(I32)
    e_blk = jnp.concatenate([pad_starts, pad_ends[-1:]]) // bm
    return row, gap_lo, gap_hi, gates, e_blk.astype(I32), n_blk


def _final_body(row_ref, x_ref, gate_ref, y_hbm, g_ref, out_ref, ybuf, sem, *, tm):
    i = pl.program_id(0)
    n_tiles = pl.num_programs(0)
    slot = i % _FINAL_SLOTS

    def gather(tile, s):
        def issue(g, carry):
            for u in range(_SUBLANES):
                for k in range(TOPK_IN_GROUP):
                    src = row_ref[(tile * tm + g * _SUBLANES + u) * TOPK_IN_GROUP + k]
                    pltpu.make_async_copy(y_hbm.at[pl.ds(src, 1)], ybuf.at[s, k, g, pl.ds(u, 1)],
                                          sem.at[s]).start()
            return carry
        lax.fori_loop(0, tm // _SUBLANES, issue, 0)

    def wait_rows(s):
        pltpu.make_async_copy(ybuf.at[s], ybuf.at[s], sem.at[s]).wait()

    @pl.when(i == 0)
    def _():
        for t in range(_FINAL_SLOTS - 1):
            gather(jnp.minimum(t, n_tiles - 1), t)

    wait_rows(slot)
    acc = x_ref[...]
    for k in range(TOPK_IN_GROUP):
        acc = acc + gate_ref[:, k:k + 1] * ybuf[slot, k].reshape(tm, ybuf.shape[-1])
    y = _rms(acc, g_ref[...])

    ahead = _FINAL_SLOTS - 1
    nxt = jnp.minimum(i + ahead, n_tiles - 1)
    for r in range(tm):
        for k in range(TOPK_IN_GROUP):
            src = row_ref[(nxt * tm + r) * TOPK_IN_GROUP + k]
            pltpu.make_async_copy(y_hbm.at[pl.ds(src, 1)],
                                  ybuf.at[(i + ahead) % _FINAL_SLOTS, k, r // _SUBLANES, pl.ds(r % _SUBLANES, 1)],
                                  sem.at[(i + ahead) % _FINAL_SLOTS]).start()
    out_ref[...] = y

    @pl.when(i == n_tiles - 1)
    def _():
        for t in range(1, _FINAL_SLOTS):
            wait_rows((i + t) % _FINAL_SLOTS)


def _final(row, x2, gates, y_rows, g, tm):
    T, D = x2.shape
    grid_spec = pltpu.PrefetchScalarGridSpec(
        num_scalar_prefetch=1,
        grid=(T // tm,),
        in_specs=[pl.BlockSpec((tm, D), lambda i, *_: (i, 0)),
                  pl.BlockSpec((tm, TOPK_IN_GROUP), lambda i, *_: (i, 0)),
                  pl.BlockSpec(memory_space=pl.ANY),
                  pl.BlockSpec(g.shape, lambda i, *_: (0, 0))],
        out_specs=pl.BlockSpec((tm, D), lambda i, *_: (i, 0)),
        scratch_shapes=[pltpu.VMEM((_FINAL_SLOTS, TOPK_IN_GROUP, tm // _SUBLANES, _SUBLANES, D), F32),
                        pltpu.SemaphoreType.DMA((_FINAL_SLOTS,))],
    )
    return pl.pallas_call(
        functools.partial(_final_body, tm=tm),
        grid_spec=grid_spec,
        out_shape=jax.ShapeDtypeStruct((T, D), F32),
        compiler_params=_params(),
        name="final",
    )(row, x2, gates, y_rows, g)


def _tile_sizes(T):
    pick = lambda want: want if T % want == 0 else CHUNK
    return dict(inproj=pick(256), dsa_q=pick(256), dsa_k=pick(512), mlstm=pick(256), mixout=pick(256),
                final=pick(256), route=pick(512), moe=128)


def _layer(x, mem, norm_mix_g, w_in, kv_norm_g, k_idx_norm_g, w_uk, w_uv, conv_w, conv_b, gate_b, ml_norm_g,
           w_out, norm_x_g, mem_norm_g, w_cq, w_ckv, w_co, norm_ffn_g, w_group, b_group, w_router, b_router,
           w_gate, w_up, w_down, out_g):
    T = x.shape[0]
    ts = _tile_sizes(T)
    r2 = lambda v: v.reshape(1, -1)

    w_r = _wprep(w_in, 256)
    wuk_t = jnp.transpose(w_uk, (1, 2, 0)).astype(BF16)
    wuv_t = jnp.transpose(w_uv, (1, 0, 2)).astype(BF16)

    qabs, ckv, qi, kidx, small, mqk, mv, mo = _inproj(
        x, r2(norm_mix_g), w_r, wuk_t, r2(kv_norm_g), r2(k_idx_norm_g), ts["inproj"])

    gate_rows = jnp.transpose(small[:, _S_WI:_S_MF + ML_HEADS])
    dsa_out = _dsa(qi, gate_rows, qabs, kidx, ckv, wuv_t, ts["dsa_q"], ts["dsa_k"])

    gb_col = jnp.zeros((1, _SMALL), F32).at[0, _S_MI:_S_MI + 2 * ML_HEADS].set(gate_b)
    ml_out = _mlstm(mqk, mv, small, gate_rows, mo, conv_w, r2(conv_b), gb_col, gate_b.reshape(-1, 1),
                    r2(ml_norm_g), ts["mlstm"])

    wqk, wvo = _memfold(mem, r2(mem_norm_g), w_ckv, w_cq, w_co)
    w_rt = jnp.concatenate([w_group, w_router,
                            jnp.zeros((D_MODEL, 128 - N_GROUPS - N_EXPERTS), w_group.dtype)], axis=1)
    x2, hf, rl = _mixout(x, dsa_out, ml_out, w_out.astype(BF16), wqk, wvo, w_rt.astype(BF16),
                         r2(norm_x_g), r2(norm_ffn_g), ts["mixout"])

    bm = ts["moe"]
    row, gap_lo, gap_hi, gates, e_blk, n_blk = _route(rl, b_group, b_router, bm, ts["route"])
    y_rows = _moe(row, gap_lo, gap_hi, e_blk, hf, w_gate, w_up, w_down, n_blk, bm)
    return _final(row, x2, gates, y_rows, r2(out_g), ts["final"])


def kernel(x, mem, norm_mix_g, w_in, kv_norm_g, k_idx_norm_g, w_uk, w_uv, conv_w, conv_b, gate_b, ml_norm_g,
           w_out, norm_x_g, mem_norm_g, w_cq, w_ckv, w_co, norm_ffn_g, w_group, b_group, w_router, b_router,
           w_gate, w_up, w_down, final_norm_g):
    B, T, D = x.shape
    assert B == 1 and D == D_MODEL and norm_mix_g.shape[0] == 1 and T % CHUNK == 0
    out = _layer(x[0], mem[0], norm_mix_g[0], w_in, kv_norm_g[0], k_idx_norm_g[0], w_uk[0], w_uv[0],
                 conv_w[0], conv_b[0], gate_b[0], ml_norm_g[0], w_out[0], norm_x_g[0], mem_norm_g[0],
                 w_cq[0], w_ckv[0], w_co[0], norm_ffn_g[0], w_group[0], b_group[0], w_router[0], b_router[0],
                 w_gate[0], w_up[0], w_down[0], final_norm_g)
    return out[None]
```

```python
import functools

import jax
import jax.numpy as jnp
from jax import lax
from jax.experimental import pallas as pl
from jax.experimental.pallas import tpu as pltpu

F32 = jnp.float32
BF16 = jnp.bfloat16
I32 = jnp.int32
U32 = jnp.uint32

EPS = 1e-6
CHUNK = 64
D_MODEL = 2048

DSA_HEADS = 8
DSA_HEAD_DIM = 128
DSA_LATENT = 256
IDX_HEADS = 8
IDX_DIM = 64
TOPK_MAX = 256

ML_HEADS = 4
ML_QK_DIM = 128
ML_V_DIM = 256
CONV_W = 4

X_HEADS = 4
X_HEAD_DIM = D_MODEL // X_HEADS

N_GROUPS = 4
EXP_PER_GROUP = 8
N_EXPERTS = N_GROUPS * EXP_PER_GROUP
TOPK_IN_GROUP = 2
D_EXPERT = 512

_O_DQ = 0
_O_CKV = _O_DQ + DSA_HEADS * DSA_HEAD_DIM
_O_QI = _O_CKV + DSA_LATENT
_O_KI = _O_QI + IDX_HEADS * IDX_DIM
_O_WI = _O_KI + IDX_DIM
_O_MQ = _O_WI + IDX_HEADS
_O_MK = _O_MQ + ML_HEADS * ML_QK_DIM
_O_MV = _O_MK + ML_HEADS * ML_QK_DIM
_O_MI = _O_MV + ML_HEADS * ML_V_DIM
_O_MF = _O_MI + ML_HEADS
_O_MO = _O_MF + ML_HEADS
_O_END = _O_MO + ML_HEADS * ML_V_DIM

_G_DQ = (0, 1024)
_G_CKV = (1024, 1280)
_G_QI = (1280, 1792)
_G_SMALL = (1792, 1920)
_G_MQK = (1920, 2944)
_G_MV = (2944, 3968)
_G_MO = (3968, 4992)
_W_COLS = 4992
_S_WI = IDX_DIM
_S_MI = _S_WI + IDX_HEADS
_S_MF = _S_MI + ML_HEADS
_SMALL = 128

_VMEM_LIMIT = 56 * 1024 * 1024
_INT_MIN = -(2 ** 31)
_CHUNK_SHIFT = CHUNK.bit_length() - 1
_LOG2E = 1.4426950408889634
_SUBLANES = 8
_LANES = 128
_GATHER_SLOTS = 9
_GATHER_AHEAD = _GATHER_SLOTS - 1
_TOPK_SHIFT = TOPK_IN_GROUP.bit_length() - 1
_FINAL_SLOTS = 3
_NEG = -1e30


def _rms(v, g):
    return v * lax.rsqrt(jnp.mean(v * v, axis=-1, keepdims=True) + EPS) * g


def _dot(a, b):
    return jnp.dot(a, b, preferred_element_type=F32)


def _dot_nt(a, b):
    return lax.dot_general(a, b, (((1,), (1,)), ((), ())), preferred_element_type=F32)


def _resident(shape):
    nd = len(shape)
    return pl.BlockSpec(shape, lambda *_: (0,) * nd, pipeline_mode=pl.Buffered(1))


def _params(n_axes=1):
    return pltpu.CompilerParams(dimension_semantics=("arbitrary",) * n_axes,
                                vmem_limit_bytes=_VMEM_LIMIT)


def _wprep_body(w_hbm, out_ref, wbuf, sem, *, tk):
    i = pl.program_id(0)
    slot = i % 2

    def rows(blk, s):
        return pltpu.make_async_copy(w_hbm.at[0, pl.ds(pl.multiple_of(blk * tk, tk), tk)], wbuf.at[s], sem.at[s])

    @pl.when(i == 0)
    def _():
        rows(0, 0).start()

    @pl.when(i + 1 < pl.num_programs(0))
    def _():
        rows(i + 1, 1 - slot).start()

    rows(i, slot).wait()
    w_ref = wbuf.at[slot]
    n_small = IDX_DIM + IDX_HEADS
    out_ref[:, _G_DQ[0]:_G_SMALL[0] + n_small] = w_ref[:, _O_DQ:_O_MQ].astype(BF16)
    out_ref[:, _G_SMALL[0] + n_small:_G_SMALL[0] + n_small + 2 * ML_HEADS] = w_ref[:, _O_MI:_O_MO].astype(BF16)
    out_ref[:, _G_SMALL[0] + n_small + 2 * ML_HEADS:_G_SMALL[1]] = jnp.zeros(
        (out_ref.shape[0], _SMALL - n_small - 2 * ML_HEADS), BF16)
    out_ref[:, _G_MQK[0]:_G_MV[1]] = w_ref[:, _O_MQ:_O_MI].astype(BF16)
    out_ref[:, _G_MO[0]:_G_MO[1]] = w_ref[:, _O_MO:_O_END].astype(BF16)


def _wprep(w_in, tk):
    K = w_in.shape[1]
    return pl.pallas_call(
        functools.partial(_wprep_body, tk=tk),
        grid=(K // tk,),
        in_specs=[pl.BlockSpec(memory_space=pl.ANY)],
        out_specs=pl.BlockSpec((tk, _W_COLS), lambda i: (i, 0)),
        out_shape=jax.ShapeDtypeStruct((K, _W_COLS), BF16),
        scratch_shapes=[pltpu.VMEM((2, tk, w_in.shape[2]), F32), pltpu.SemaphoreType.DMA((2,))],
        compiler_params=_params(),
        name="wprep",
    )(w_in)


def _inproj_body(x_ref, g_ref, w_ref, wuk_ref, kvg_ref, kig_ref,
                 qabs_ref, ckv_ref, qi_ref, kidx_ref, small_ref, mqk_ref, mv_ref, mo_ref):
    h = _rms(x_ref[...], g_ref[...]).astype(BF16)

    def proj(grp):
        return _dot(h, w_ref[:, grp[0]:grp[1]])

    dq = proj(_G_DQ)
    for hd in range(DSA_HEADS):
        qh = dq[:, hd * DSA_HEAD_DIM:(hd + 1) * DSA_HEAD_DIM].astype(BF16)
        qa = _dot(qh, wuk_ref[hd]) * (DSA_HEAD_DIM ** -0.5 * _LOG2E)
        qabs_ref[:, hd * DSA_LATENT:(hd + 1) * DSA_LATENT] = qa.astype(BF16)
    ckv_ref[...] = _rms(proj(_G_CKV), kvg_ref[...]).astype(BF16)
    qi_ref[...] = (proj(_G_QI) * (IDX_DIM ** -0.5)).astype(BF16)
    small = proj(_G_SMALL)
    small_ref[...] = small
    kidx_ref[...] = _rms(small[:, :IDX_DIM], kig_ref[...]).astype(BF16)
    mqk_ref[...] = proj(_G_MQK)
    mv_ref[...] = proj(_G_MV).astype(BF16)
    mo_ref[...] = proj(_G_MO)


def _inproj(x, g, w, wuk, kvg, kig, tm):
    T = x.shape[0]
    row = lambda n: pl.BlockSpec((tm, n), lambda i: (i, 0))
    outs = [(8 * DSA_LATENT, BF16), (DSA_LATENT, BF16), (IDX_HEADS * IDX_DIM, BF16), (IDX_DIM, BF16),
            (_SMALL, F32), (2 * ML_HEADS * ML_QK_DIM, F32), (ML_HEADS * ML_V_DIM, BF16),
            (ML_HEADS * ML_V_DIM, F32)]
    return pl.pallas_call(
        _inproj_body,
        grid=(T // tm,),
        in_specs=[row(D_MODEL), _resident(g.shape), _resident(w.shape), _resident(wuk.shape),
                  _resident(kvg.shape), _resident(kig.shape)],
        out_specs=[row(n) for n, _ in outs],
        out_shape=[jax.ShapeDtypeStruct((T, n), dt) for n, dt in outs],
        compiler_params=_params(),
        name="inproj",
    )(x, g, w, wuk, kvg, kig)


def _sublane_fold(v, op, rows=_SUBLANES, ways=4):
    groups = [v[r * rows:(r + 1) * rows, :] for r in range(v.shape[0] // rows)]
    accs = groups[:ways]
    for r in range(ways, len(groups)):
        accs[r % ways] = op(accs[r % ways], groups[r])
    while len(accs) > 1:
        accs = [op(accs[k], accs[k + 1]) if k + 1 < len(accs) else accs[k] for k in range(0, len(accs), 2)]
    return accs[0]


def _bit_transpose32(words):
    a = list(words)
    j, m = 16, 0x0000FFFF
    while j:
        k = 0
        while k < 32:
            t = (a[k] ^ (a[k + j] >> j)) & jnp.uint32(m)
            a[k] = a[k] ^ t
            a[k + j] = a[k + j] ^ (t << j)
            k = (k + j + 1) & ~j
        j >>= 1
        m = (m ^ (m << j)) & 0xFFFFFFFF
    return a


def _dsa_body(qi_ref, wrow_ref, qabs_ref, kidx_ref, ckv_ref, ckvt_ref, wuv_ref, out_ref,
              key_ref, planes_ref, alive_ref, lga_ref, lgb_ref, m_ref, l_ref, acc_ref, *, tq, tk, topk, nbits_idx):
    i = pl.program_id(0)
    n_kb = ((i + 1) * tq + tk - 1) // tk
    w_rows = wrow_ref[0:IDX_HEADS, :] * (IDX_HEADS ** -0.5)
    q_chunk = (i * tq + lax.broadcasted_iota(I32, (1, tq), 1)) >> _CHUNK_SHIFT

    def key_pos(j):
        return j * tk + lax.broadcasted_iota(I32, (tk, 1), 0)

    plane_rows = tk // 32

    @pl.when(i == 0)
    def _():
        planes_ref[...] = jnp.zeros(planes_ref.shape, U32)

    def score_block(j, carry, *, all_admissible):
        kx = kidx_ref[pl.ds(pl.multiple_of(j * tk, tk), tk), :]
        s = jnp.zeros((tk, tq), F32)
        for hd in range(IDX_HEADS):
            d = _dot_nt(kx, qi_ref[:, hd * IDX_DIM:(hd + 1) * IDX_DIM])
            s = s + w_rows[hd:hd + 1, :] * jnp.maximum(d, 0.0)
        bits = lax.bitcast_convert_type(s, I32)
        key = bits ^ ((bits >> 31) & 0x7FFFFFFF)
        if not all_admissible:
            key = jnp.where((key_pos(j) >> _CHUNK_SHIFT) <= q_chunk, key, _INT_MIN)
        key_ref[j] = key
        u = lax.bitcast_convert_type(key, U32) ^ jnp.uint32(0x80000000)
        for h in range(plane_rows // _SUBLANES):
            base = h * 32 * _SUBLANES
            planes = _bit_transpose32([u[base + _SUBLANES * v:base + _SUBLANES * (v + 1), :] for v in range(32)])
            row0 = pl.multiple_of(j * plane_rows + h * _SUBLANES, _SUBLANES)
            for b in range(32):
                planes_ref[b, pl.ds(row0, _SUBLANES), :] = planes[31 - b]
        return carry

    n_open = (i * tq) // tk
    lax.fori_loop(0, n_open, functools.partial(score_block, all_admissible=True), 0)
    lax.fori_loop(n_open, n_kb, functools.partial(score_block, all_admissible=False), 0)

    def count(pred):
        def body(j, acc):
            hit = pred(key_ref[j], key_pos(j)).astype(I32)
            return acc + _sublane_fold(hit, jnp.add)
        acc = lax.fori_loop(0, n_kb, body, jnp.zeros((8, tq), I32))
        return jnp.sum(acc, axis=0, keepdims=True)

    word_row = lax.broadcasted_iota(I32, (alive_ref.shape[0], 1), 0)
    alive_ref[...] = jnp.where(word_row < n_kb * plane_rows, jnp.full(alive_ref.shape, 0xFFFFFFFF, U32),
                               jnp.uint32(0))

    def radix_select(n_rows):
        def ones_count(words):
            return jnp.sum(_sublane_fold(lax.population_count(words).astype(I32), jnp.add), axis=0, keepdims=True)

        def select_bit(it, carry):
            need, t_u = carry
            b = 31 - it
            alive = alive_ref[0:n_rows, :]
            ones = alive & planes_ref[b, pl.ds(0, n_rows), :]
            n_ones = ones_count(ones)
            take = n_ones >= need
            alive_ref[0:n_rows, :] = jnp.where(take, ones, alive ^ ones)
            bit = lax.shift_left(jnp.uint32(1), jnp.asarray(b, U32))
            return jnp.where(take, need, need - n_ones), jnp.where(take, t_u | bit, t_u)

        need, t_u = lax.fori_loop(0, 32, select_bit, (jnp.full((1, tq), topk, I32), jnp.zeros((1, tq), U32)))
        return need, t_u, ones_count(alive_ref[0:n_rows, :])

    all_rows = alive_ref.shape[0]
    if all_rows % (2 * _SUBLANES) == 0:
        need, t_u, n_equal = lax.cond(n_kb * plane_rows <= all_rows // 2,
                                      lambda: radix_select(all_rows // 2), lambda: radix_select(all_rows))
    else:
        need, t_u, n_equal = radix_select(all_rows)
    short = t_u == 0
    t = jnp.maximum(lax.bitcast_convert_type(t_u ^ jnp.uint32(0x80000000), I32), _INT_MIN + 1)
    all_pos = jnp.int32(2 ** nbits_idx - 1)
    has_ties = (n_equal > need) & jnp.logical_not(short)
    n_tie_take = jnp.where(has_ties, need, all_pos)

    def tie_cutoff():
        def pos_bit(b, c):
            cand = c + lax.shift_left(jnp.int32(1), nbits_idx - 1 - b)
            f = count(lambda kb, pos: (kb == t) & (pos < cand))
            return jnp.where(f <= n_tie_take, cand, c)
        return lax.fori_loop(0, nbits_idx, pos_bit, jnp.zeros((1, tq), I32))

    cut = lax.cond(jnp.max(has_ties.astype(I32)) > 0, tie_cutoff, lambda: jnp.full((1, tq), all_pos, I32))

    m_ref[...] = jnp.full(m_ref.shape, _NEG, F32)
    l_ref[...] = jnp.zeros(l_ref.shape, F32)
    acc_ref[...] = jnp.zeros(acc_ref.shape, F32)
    last_blk = ckvt_ref.shape[0] - 1

    def logits(j, lg_buf):
        c_blk = ckv_ref[pl.ds(pl.multiple_of(jnp.minimum(j, last_blk) * tk, tk), tk), :]
        kb = key_ref[j]
        sel = (kb > t) | ((kb == t) & (key_pos(j) < cut))
        bias = jnp.where(sel, 0.0, _NEG).astype(F32)
        for hd in range(DSA_HEADS):
            lg_buf[hd] = _dot_nt(c_blk, qabs_ref[:, hd * DSA_LATENT:(hd + 1) * DSA_LATENT]) + bias

    def accumulate(j, lg_buf):
        c_blk_t = ckvt_ref[jnp.minimum(j, last_blk)]
        for hd in range(DSA_HEADS):
            lg = lg_buf[hd]
            m_old = m_ref[hd:hd + 1, :]
            m_new = jnp.maximum(m_old, jnp.max(_sublane_fold(lg, jnp.maximum, ways=1), axis=0, keepdims=True))
            p = jnp.exp2(lg - m_new)
            alpha = jnp.exp2(m_old - m_new)
            l_ref[hd:hd + 1, :] = alpha * l_ref[hd:hd + 1, :] + jnp.sum(_sublane_fold(p, jnp.add, ways=1), axis=0,
                                                                         keepdims=True)
            acc_ref[hd] = alpha * acc_ref[hd] + _dot(c_blk_t, p.astype(BF16))
            m_ref[hd:hd + 1, :] = m_new

    logits(0, lga_ref)

    def attn_pair(mi, carry):
        ja = 2 * mi
        accumulate(ja, lga_ref)
        logits(ja + 1, lgb_ref)
        accumulate(ja + 1, lgb_ref)
        logits(jnp.minimum(ja + 2, n_kb - 1), lga_ref)
        return carry

    lax.fori_loop(0, n_kb // 2, attn_pair, 0)

    @pl.when(n_kb % 2 == 1)
    def _():
        accumulate(n_kb - 1, lga_ref)

    for hd in range(DSA_HEADS):
        o_lat = (acc_ref[hd] / l_ref[hd:hd + 1, :]).T.astype(BF16)
        out_ref[:, hd * DSA_HEAD_DIM:(hd + 1) * DSA_HEAD_DIM] = _dot(o_lat, wuv_ref[hd]).astype(BF16)


def _dsa(qi, wrows, qabs, kidx, ckv, wuv, tq, tk):
    T = qi.shape[0]
    topk = min(TOPK_MAX, T // 4)
    n_kb = T // tk
    ckvt = jnp.transpose(ckv.reshape(n_kb, tk, DSA_LATENT), (0, 2, 1))
    row = lambda n: pl.BlockSpec((tq, n), lambda i: (i, 0))
    body = functools.partial(_dsa_body, tq=tq, tk=tk, topk=topk, nbits_idx=int(T).bit_length())
    return pl.pallas_call(
        body,
        grid=(T // tq,),
        in_specs=[row(qi.shape[1]), pl.BlockSpec((wrows.shape[0], tq), lambda i: (0, i)), row(qabs.shape[1]),
                  _resident(kidx.shape), _resident(ckv.shape), _resident(ckvt.shape), _resident(wuv.shape)],
        out_specs=row(DSA_HEADS * DSA_HEAD_DIM),
        out_shape=jax.ShapeDtypeStruct((T, DSA_HEADS * DSA_HEAD_DIM), BF16),
        scratch_shapes=[pltpu.VMEM((n_kb + n_kb % 2, tk, tq), I32),
                        pltpu.VMEM((32, T // 32, tq), U32), pltpu.VMEM((T // 32, tq), U32),
                        pltpu.VMEM((DSA_HEADS, tk, tq), F32), pltpu.VMEM((DSA_HEADS, tk, tq), F32),
                        pltpu.VMEM((DSA_HEADS, tq), F32),
                        pltpu.VMEM((DSA_HEADS, tq), F32), pltpu.VMEM((DSA_HEADS, DSA_LATENT, tq), F32)],
        compiler_params=_params(),
        name="dsa",
    )(qi, wrows, qabs, kidx, ckv, ckvt, wuv)


def _log_sigmoid(v):
    return jnp.minimum(v, 0.0) - jnp.log1p(jnp.exp(-jnp.abs(v)))


def _chunk_cumsum(v, axis):
    pos = lax.broadcasted_iota(I32, v.shape, axis) & (CHUNK - 1)
    d = 1
    while d < CHUNK:
        v = v + jnp.where(pos >= d, pltpu.roll(v, d, axis=axis), 0.0)
        d *= 2
    return v


def _mlstm_body(mqk_ref, mv_ref, small_ref, gt_ref, mo_ref, cw_ref, cb_ref, gbc_ref, gbr_ref, ng_ref,
                out_ref, xe_ref, c_ref, n_ref, m_ref, hs_ref, *, rows):
    @pl.when(pl.program_id(0) == 0)
    def _():
        xe_ref[0:8, :] = jnp.zeros((8, xe_ref.shape[1]), F32)
        c_ref[...] = jnp.zeros(c_ref.shape, F32)
        n_ref[...] = jnp.zeros(n_ref.shape, F32)
        m_ref[...] = jnp.zeros(m_ref.shape, F32)

    x = mqk_ref[...]
    xe_ref[8:8 + rows, :] = x
    y = cb_ref[...]
    for j in range(CONV_W - 1):
        y = y + xe_ref[5 + j:5 + j + rows, :] * cw_ref[j:j + 1, :]
    y = y + x * cw_ref[CONV_W - 1:CONV_W, :]
    xe_ref[0:8, :] = x[rows - 8:rows, :]
    qk = y * jax.nn.sigmoid(y)
    nqk = ML_HEADS * ML_QK_DIM
    q_all = (qk[:, :nqk] * (ML_QK_DIM ** -0.5)).astype(BF16)
    k_all = qk[:, nqk:]

    g_col = small_ref[...] + gbc_ref[...]
    g_row = gt_ref[...] + gbr_ref[...]
    b_col = _chunk_cumsum(_log_sigmoid(g_col), 0)
    b_row = _chunk_cumsum(_log_sigmoid(g_row), 1)

    tri = lax.broadcasted_iota(I32, (CHUNK, CHUNK), 1) <= lax.broadcasted_iota(I32, (CHUNK, CHUNK), 0)

    heads = range(ML_HEADS)
    for c in range(rows // CHUNK):
        lo, hi = c * CHUNK, (c + 1) * CHUNK
        bc = [b_col[lo:hi, _S_MF + hd:_S_MF + hd + 1] for hd in heads]
        lic = [g_col[lo:hi, _S_MI + hd:_S_MI + hd + 1] for hd in heads]
        br = [b_row[ML_HEADS + hd:ML_HEADS + hd + 1, lo:hi] for hd in heads]
        lir = [g_row[hd:hd + 1, lo:hi] for hd in heads]
        g_tot = [bc[hd][CHUNK - 1:CHUNK, :] for hd in heads]
        m_prev = [m_ref[hd][:, 0:1] for hd in heads]
        qh = [q_all[lo:hi, hd * ML_QK_DIM:(hd + 1) * ML_QK_DIM] for hd in heads]
        kh = [k_all[lo:hi, hd * ML_QK_DIM:(hd + 1) * ML_QK_DIM] for hd in heads]
        vh = [mv_ref[lo:hi, hd * ML_V_DIM:(hd + 1) * ML_V_DIM] for hd in heads]
        c_prev = [c_ref[hd] for hd in heads]
        n_prev = [n_ref[hd] for hd in heads]

        dmat = [jnp.where(tri, bc[hd] - br[hd] + lir[hd], -jnp.inf) for hd in heads]
        inter = [bc[hd] + m_prev[hd] for hd in heads]
        m_t = [jnp.maximum(inter[hd], jnp.max(dmat[hd], axis=-1, keepdims=True)) for hd in heads]
        w_intra = [jnp.exp(dmat[hd] - m_t[hd]) for hd in heads]
        a_inter = [jnp.exp(inter[hd] - m_t[hd]) for hd in heads]
        m_new = [jnp.maximum(g_tot[hd] + m_prev[hd],
                             jnp.max(g_tot[hd] - br[hd] + lir[hd], axis=-1, keepdims=True)) for hd in heads]
        a_state = [jnp.exp(g_tot[hd] + m_prev[hd] - m_new[hd]) for hd in heads]
        wk = [jnp.exp(g_tot[hd] - bc[hd] + lic[hd] - m_new[hd]) * kh[hd] for hd in heads]

        s_qk = [_dot_nt(qh[hd], kh[hd].astype(BF16)) * w_intra[hd] for hd in heads]
        read = [_dot(qh[hd], c_prev[hd].astype(BF16)) for hd in heads]
        d_c = [_dot(wk[hd].T.astype(BF16), vh[hd]) for hd in heads]
        num = [a_inter[hd] * read[hd] + _dot(s_qk[hd].astype(BF16), vh[hd]) for hd in heads]
        den = [a_inter[hd] * jnp.sum(qh[hd].astype(F32) * n_prev[hd], axis=-1, keepdims=True)
               + jnp.sum(s_qk[hd], axis=-1, keepdims=True) for hd in heads]
        for hd in heads:
            hs_ref[lo:hi, hd * ML_V_DIM:(hd + 1) * ML_V_DIM] = (
                num[hd] / jnp.maximum(jnp.abs(den[hd]), jnp.exp(-m_t[hd])))
        for hd in heads:
            c_ref[hd] = a_state[hd] * c_prev[hd] + d_c[hd]
            n_ref[hd] = a_state[hd] * n_prev[hd] + jnp.sum(wk[hd], axis=0, keepdims=True)
            m_ref[hd] = jnp.broadcast_to(m_new[hd], m_ref.shape[1:])

    for hd in range(ML_HEADS):
        sl = slice(hd * ML_V_DIM, (hd + 1) * ML_V_DIM)
        out_ref[:, sl] = (_rms(hs_ref[:, sl], ng_ref[:, sl]) * jax.nn.sigmoid(mo_ref[:, sl])).astype(BF16)


def _mlstm(mqk, mv, small, gt, mo, cw, cb, gbc, gbr, ng, rows):
    T = mqk.shape[0]
    row = lambda n: pl.BlockSpec((rows, n), lambda i: (i, 0))
    nv = ML_HEADS * ML_V_DIM
    return pl.pallas_call(
        functools.partial(_mlstm_body, rows=rows),
        grid=(T // rows,),
        in_specs=[row(mqk.shape[1]), row(nv), row(_SMALL), pl.BlockSpec((8, rows), lambda i: (1, i)), row(nv),
                  _resident(cw.shape), _resident(cb.shape), _resident(gbc.shape), _resident(gbr.shape),
                  _resident(ng.shape)],
        out_specs=row(nv),
        out_shape=jax.ShapeDtypeStruct((T, nv), BF16),
        scratch_shapes=[pltpu.VMEM((rows + 8, mqk.shape[1]), F32),
                        pltpu.VMEM((ML_HEADS, ML_QK_DIM, ML_V_DIM), F32),
                        pltpu.VMEM((ML_HEADS, 1, ML_QK_DIM), F32),
                        pltpu.VMEM((ML_HEADS, 1, 128), F32),
                        pltpu.VMEM((rows, nv), F32)],
        compiler_params=_params(),
        name="mlstm",
    )(mqk, mv, small, gt, mo, cw, cb, gbc, gbr, ng)


def _memfold_body(mem_ref, g_ref, wk_ref, wv_ref, wq_ref, wc_ref, wqk_ref, wvo_ref):
    mn = _rms(mem_ref[...], g_ref[...]).astype(BF16)
    k = _dot(mn, wk_ref[...].astype(BF16)).astype(BF16)
    v = _dot(mn, wv_ref[...].astype(BF16)).astype(BF16)
    wqk_ref[...] = (_dot_nt(wq_ref[...].astype(BF16), k) * (X_HEAD_DIM ** -0.5)).astype(BF16)
    wvo_ref[...] = _dot(v, wc_ref[...].astype(BF16)).astype(BF16)


def _memfold(mem, g, w_ckv, w_cq, w_co):
    M, D = mem.shape
    dh = X_HEAD_DIM
    return pl.pallas_call(
        _memfold_body,
        grid=(X_HEADS,),
        in_specs=[_resident(mem.shape), _resident(g.shape),
                  pl.BlockSpec((D, dh), lambda h: (0, h)), pl.BlockSpec((D, dh), lambda h: (0, X_HEADS + h)),
                  pl.BlockSpec((D, dh), lambda h: (0, h)), pl.BlockSpec((dh, D), lambda h: (h, 0))],
        out_specs=[pl.BlockSpec((D, M), lambda h: (0, h)), pl.BlockSpec((M, D), lambda h: (h, 0))],
        out_shape=[jax.ShapeDtypeStruct((D, X_HEADS * M), BF16), jax.ShapeDtypeStruct((X_HEADS * M, D), BF16)],
        compiler_params=_params(),
        name="memfold",
    )(mem, g, w_ckv, w_ckv, w_cq, w_co)


def _mixout_body(x_ref, dsa_ref, ml_ref, wo_ref, wqk_ref, wvo_ref, wr_ref, gx_ref, gf_ref,
                 x2_ref, hf_ref, rl_ref, p_ref):
    nd = dsa_ref.shape[1]
    x1 = x_ref[...] + _dot(dsa_ref[...], wo_ref[0:nd, :]) + _dot(ml_ref[...], wo_ref[nd:, :])
    lg_all = _dot(_rms(x1, gx_ref[...]).astype(BF16), wqk_ref[...])
    n_mem = wqk_ref.shape[1] // X_HEADS
    for hd in range(X_HEADS):
        sl = slice(hd * n_mem, (hd + 1) * n_mem)
        lg = lg_all[:, sl]
        e = jnp.exp(lg - jnp.max(lg, axis=-1, keepdims=True))
        p_ref[:, sl] = (e / jnp.sum(e, axis=-1, keepdims=True)).astype(BF16)
    x2 = x1 + _dot(p_ref[...], wvo_ref[...])
    x2_ref[...] = x2
    hf = _rms(x2, gf_ref[...]).astype(BF16)
    rl_ref[...] = _dot(hf, wr_ref[...])
    bits = lax.bitcast_convert_type(hf.astype(F32), U32)
    for c in range(_SUBLANES):
        lo = bits[:, (2 * c) * _LANES:(2 * c + 1) * _LANES]
        hi = bits[:, (2 * c + 1) * _LANES:(2 * c + 2) * _LANES]
        hf_ref[pl.ds(c, x2.shape[0], stride=_SUBLANES), :] = (hi & jnp.uint32(0xFFFF0000)) | (lo >> 16)


def _mixout(x, dsa, ml, wo, wqk, wvo, wr, gx, gf, tm):
    T = x.shape[0]
    row = lambda n: pl.BlockSpec((tm, n), lambda i: (i, 0))
    return pl.pallas_call(
        _mixout_body,
        grid=(T // tm,),
        in_specs=[row(D_MODEL), row(dsa.shape[1]), row(ml.shape[1]), _resident(wo.shape),
                  _resident(wqk.shape), _resident(wvo.shape), _resident(wr.shape), _resident(gx.shape),
                  _resident(gf.shape)],
        out_specs=[row(D_MODEL), pl.BlockSpec((tm * _SUBLANES, _LANES), lambda i: (i, 0)), row(wr.shape[1])],
        out_shape=[jax.ShapeDtypeStruct((T, D_MODEL), F32),
                   jax.ShapeDtypeStruct((T * _SUBLANES, _LANES), U32),
                   jax.ShapeDtypeStruct((T, wr.shape[1]), F32)],
        scratch_shapes=[pltpu.VMEM((tm, wqk.shape[1]), BF16)],
        compiler_params=_params(),
        name="mixout",
    )(x, dsa, ml, wo, wqk, wvo, wr, gx, gf)


def _moe_body(row_ref, gap_lo_ref, gap_hi_ref, eb_ref, hf_hbm, wg_ref, wu_ref, wd_ref, y_hbm,
              tok_ref, xbuf, ybuf, wgb, wub, wdb, gsem, ysem, *, bm, n_blk):
    e = pl.program_id(0)

    @pl.when(e == 0)
    def _():
        def clear_gap(g, carry):
            def clear(r, c):
                tok_ref[r] = 0
                return c
            return lax.fori_loop(gap_lo_ref[g], gap_hi_ref[g], clear, carry)

        def place(a, carry):
            tok_ref[row_ref[a]] = lax.shift_right_logical(a, _TOPK_SHIFT)
            return carry

        lax.fori_loop(0, gap_lo_ref.shape[0], clear_gap, 0)
        lax.fori_loop(0, row_ref.shape[0], place, 0, unroll=8)

    n_valid = eb_ref[N_EXPERTS]
    b_lo = eb_ref[e]
    b_hi = eb_ref[e + 1]

    def tok_words(tok):
        return hf_hbm.at[pl.ds(pl.multiple_of(tok * _SUBLANES, _SUBLANES), _SUBLANES)]

    def gather(blk, s):
        def issue(g, carry):
            for u in range(_SUBLANES):
                r = g * _SUBLANES + u
                pltpu.make_async_copy(tok_words(tok_ref[blk * bm + r]),
                                      xbuf.at[s, pl.ds(pl.multiple_of(r * _SUBLANES, _SUBLANES), _SUBLANES)],
                                      gsem.at[s]).start(priority=1)
            return carry
        lax.fori_loop(0, bm // _SUBLANES, issue, 0)

    def wait_gather(s):
        pltpu.make_async_copy(xbuf.at[s], xbuf.at[s], gsem.at[s]).wait()

    def y_copy(blk, s):
        return pltpu.make_async_copy(ybuf.at[s], y_hbm.at[pl.ds(pl.multiple_of(blk * bm, bm), bm)], ysem.at[s])

    @pl.when(e == 0)
    def _():
        for k in range(_GATHER_AHEAD):
            gather(k, k)

    @pl.when(b_hi > b_lo)
    def _():
        wgb[...] = wg_ref[0].astype(BF16)
        wub[...] = wu_ref[0].astype(BF16)
        wdb[...] = wd_ref[0].astype(BF16)

        def block(b, carry):
            s = b % _GATHER_SLOTS
            wait_gather(s)
            chunks = []
            for c in range(_SUBLANES):
                w = xbuf[s, pl.ds(c, bm, stride=_SUBLANES), :]
                chunks.append(lax.bitcast_convert_type(w << 16, F32).astype(BF16))
                chunks.append(lax.bitcast_convert_type(w & jnp.uint32(0xFFFF0000), F32).astype(BF16))
            xb = jnp.concatenate(chunks, axis=-1)
            gate = _dot(xb, wgb[...])
            a = gate * jax.nn.sigmoid(gate) * _dot(xb, wub[...])
            y = _dot(a.astype(BF16), wdb[...])

            nxt = jnp.minimum(b + _GATHER_AHEAD, n_blk - 1)
            for r in range(bm):
                pltpu.make_async_copy(tok_words(tok_ref[nxt * bm + r]),
                                      xbuf.at[(b + _GATHER_AHEAD) % _GATHER_SLOTS, pl.ds(r * _SUBLANES, _SUBLANES)],
                                      gsem.at[(b + _GATHER_AHEAD) % _GATHER_SLOTS]).start(priority=1)

            @pl.when(b >= 2)
            def _():
                y_copy(b - 2, b % 2).wait()

            ybuf[b % 2] = y
            y_copy(b, b % 2).start()
            return carry

        lax.fori_loop(b_lo, b_hi, block, 0)

    @pl.when(e == pl.num_programs(0) - 1)
    def _():
        for k in range(_GATHER_AHEAD):
            wait_gather((n_valid + k) % _GATHER_SLOTS)

        @pl.when(n_valid >= 2)
        def _():
            y_copy(n_valid - 2, n_valid % 2).wait()
        y_copy(n_valid - 1, (n_valid - 1) % 2).wait()
        ybuf[0] = jnp.zeros(ybuf.shape[1:], F32)

        def zero_block(b, carry):
            cp = y_copy(b, 0)
            cp.start()
            cp.wait()
            return carry

        lax.fori_loop(n_valid, n_blk, zero_block, 0)


def _moe(row, gap_lo, gap_hi, e_blk, hf, wg, wu, wd, n_blk, bm):
    D = wg.shape[1]
    wspec = lambda shape: pl.BlockSpec((1,) + shape, lambda e, *_: (e, 0, 0))
    grid_spec = pltpu.PrefetchScalarGridSpec(
        num_scalar_prefetch=4,
        grid=(N_EXPERTS,),
        in_specs=[pl.BlockSpec(memory_space=pl.ANY),
                  wspec((D, D_EXPERT)), wspec((D, D_EXPERT)), wspec((D_EXPERT, D))],
        out_specs=pl.BlockSpec(memory_space=pl.ANY),
        scratch_shapes=[pltpu.SMEM((n_blk * bm,), I32),
                        pltpu.VMEM((_GATHER_SLOTS, bm * _SUBLANES, _LANES), U32),
                        pltpu.VMEM((2, bm, D), F32),
                        pltpu.VMEM((D, D_EXPERT), BF16), pltpu.VMEM((D, D_EXPERT), BF16),
                        pltpu.VMEM((D_EXPERT, D), BF16),
                        pltpu.SemaphoreType.DMA((_GATHER_SLOTS,)), pltpu.SemaphoreType.DMA((2,))],
    )
    return pl.pallas_call(
        functools.partial(_moe_body, bm=bm, n_blk=n_blk),
        grid_spec=grid_spec,
        out_shape=jax.ShapeDtypeStruct((n_blk * bm, D), F32),
        compiler_params=_params(),
        name="moe",
    )(row, gap_lo, gap_hi, e_blk, hf, wg, wu, wd)


def _route_body(rl_ref, bias_ref, out_ref, cnt_ref, seen_ref, *, tm):
    @pl.when(pl.program_id(0) == 0)
    def _():
        seen_ref[...] = jnp.zeros(seen_ref.shape, F32)

    lg = rl_ref[...] + bias_ref[...]
    lane = lax.broadcasted_iota(I32, lg.shape, 1).astype(F32)
    first = lambda hit: jnp.min(jnp.where(hit, lane, float(_LANES)), axis=-1, keepdims=True)
    is_group = lane < N_GROUPS
    gl = jnp.where(is_group, lg, -jnp.inf)
    g_max = jnp.max(gl, axis=-1, keepdims=True)
    g_sel = first(gl == g_max)
    p_g = 1.0 / jnp.sum(jnp.where(is_group, jnp.exp(gl - g_max), 0.0), axis=-1, keepdims=True)

    e_id = lane - N_GROUPS
    in_group = (e_id >= 0) & (e_id < N_EXPERTS) & (jnp.floor(e_id / EXP_PER_GROUP) == g_sel)
    el = jnp.where(in_group, lg, -jnp.inf)
    ee = jnp.where(in_group, jnp.exp(el - jnp.max(el, axis=-1, keepdims=True)), 0.0)
    cand = jnp.where(in_group, ee / jnp.sum(ee, axis=-1, keepdims=True), -1.0)
    p1 = jnp.max(cand, axis=-1, keepdims=True)
    l1 = first(cand == p1)
    cand = jnp.where(lane == l1, -2.0, cand)
    p2 = jnp.max(cand, axis=-1, keepdims=True)
    l2 = first(cand == p2)
    g1 = p_g * p1 / (p1 + p2)
    g2 = p_g * p2 / (p1 + p2)

    oh1 = (lane == l1).astype(BF16)
    oh2 = (lane == l2).astype(BF16)
    both = oh1 + oh2
    earlier = (lax.broadcasted_iota(I32, (tm, tm), 1) < lax.broadcasted_iota(I32, (tm, tm), 0)).astype(BF16)
    before = _dot(earlier, both) + seen_ref[...]
    r1 = jnp.sum(before * oh1.astype(F32), axis=-1, keepdims=True)
    r2 = jnp.sum(before * oh2.astype(F32), axis=-1, keepdims=True)
    seen_ref[...] = seen_ref[...] + jnp.sum(both.astype(F32), axis=0, keepdims=True)
    cnt_ref[...] = seen_ref[...]

    cols = (l1 - N_GROUPS, l2 - N_GROUPS, r1, r2, g1, g2)
    out = jnp.zeros(lg.shape, F32)
    for c, v in enumerate(cols):
        out = jnp.where(lane == c, v, out)
    out_ref[...] = out


def _route_tokens(rl, bias, tm):
    T = rl.shape[0]
    return pl.pallas_call(
        functools.partial(_route_body, tm=tm),
        grid=(T // tm,),
        in_specs=[pl.BlockSpec((tm, _LANES), lambda i: (i, 0)), _resident(bias.shape)],
        out_specs=[pl.BlockSpec((tm, _LANES), lambda i: (i, 0)), pl.BlockSpec((1, _LANES), lambda i: (0, 0))],
        out_shape=[jax.ShapeDtypeStruct((T, _LANES), F32), jax.ShapeDtypeStruct((1, _LANES), F32)],
        scratch_shapes=[pltpu.VMEM((1, _LANES), F32)],
        compiler_params=_params(),
        name="route",
    )(rl, bias)


def _route(rl, b_group, b_router, bm, tm):
    N = rl.shape[0]
    bias = jnp.concatenate([b_group, b_router, jnp.zeros((_LANES - N_GROUPS - N_EXPERTS,), F32)]).reshape(1, _LANES)
    per_tok, seen = _route_tokens(rl, bias, tm)
    expert_id = per_tok[:, 0:TOPK_IN_GROUP].astype(I32)
    rank = per_tok[:, TOPK_IN_GROUP:2 * TOPK_IN_GROUP].astype(I32)
    gates = per_tok[:, 2 * TOPK_IN_GROUP:3 * TOPK_IN_GROUP]
    counts = seen[0, N_GROUPS:N_GROUPS + N_EXPERTS].astype(I32)

    A = N * TOPK_IN_GROUP
    padded = (counts + bm - 1) // bm * bm
    pad_ends = jnp.cumsum(padded)
    pad_starts = pad_ends - padded
    onehot = expert_id[..., None] == jnp.arange(N_EXPERTS, dtype=I32)
    row = (jnp.sum(jnp.where(onehot, pad_starts, 0), axis=-1) + rank).reshape(A)
    n_blk = -(-A // bm) + N_EXPERTS
    gap_lo = jnp.concatenate([pad_starts + counts, pad_ends[-1:]]).astype(I32)
    gap_hi = jnp.concatenate([pad_ends, jnp.full((1,), n_blk * bm, I32)]).astype(I32)
    e_blk = jnp.concatenate([pad_starts, pad_ends[-1:]]) // bm
    return row, gap_lo, gap_hi, gates, e_blk.astype(I32), n_blk


def _final_body(row_ref, x_ref, gate_ref, y_hbm, g_ref, out_ref, ybuf, sem, *, tm):
    i = pl.program_id(0)
    n_tiles = pl.num_programs(0)
    slot = i % _FINAL_SLOTS

    def gather(tile, s):
        def issue(g, carry):
            for u in range(_SUBLANES):
                for k in range(TOPK_IN_GROUP):
                    src = row_ref[(tile * tm + g * _SUBLANES + u) * TOPK_IN_GROUP + k]
                    pltpu.make_async_copy(y_hbm.at[pl.ds(src, 1)], ybuf.at[s, k, g, pl.ds(u, 1)],
                                          sem.at[s]).start()
            return carry
        lax.fori_loop(0, tm // _SUBLANES, issue, 0)

    def wait_rows(s):
        pltpu.make_async_copy(ybuf.at[s], ybuf.at[s], sem.at[s]).wait()

    @pl.when(i == 0)
    def _():
        for t in range(_FINAL_SLOTS - 1):
            gather(jnp.minimum(t, n_tiles - 1), t)

    wait_rows(slot)
    acc = x_ref[...]
    for k in range(TOPK_IN_GROUP):
        acc = acc + gate_ref[:, k:k + 1] * ybuf[slot, k].reshape(tm, ybuf.shape[-1])
    y = _rms(acc, g_ref[...])

    ahead = _FINAL_SLOTS - 1
    nxt = jnp.minimum(i + ahead, n_tiles - 1)
    for r in range(tm):
        for k in range(TOPK_IN_GROUP):
            src = row_ref[(nxt * tm + r) * TOPK_IN_GROUP + k]
            pltpu.make_async_copy(y_hbm.at[pl.ds(src, 1)],
                                  ybuf.at[(i + ahead) % _FINAL_SLOTS, k, r // _SUBLANES, pl.ds(r % _SUBLANES, 1)],
                                  sem.at[(i + ahead) % _FINAL_SLOTS]).start()
    out_ref[...] = y

    @pl.when(i == n_tiles - 1)
    def _():
        for t in range(1, _FINAL_SLOTS):
            wait_rows((i + t) % _FINAL_SLOTS)


def _final(row, x2, gates, y_rows, g, tm):
    T, D = x2.shape
    grid_spec = pltpu.PrefetchScalarGridSpec(
        num_scalar_prefetch=1,
        grid=(T // tm,),
        in_specs=[pl.BlockSpec((tm, D), lambda i, *_: (i, 0)),
                  pl.BlockSpec((tm, TOPK_IN_GROUP), lambda i, *_: (i, 0)),
                  pl.BlockSpec(memory_space=pl.ANY),
                  pl.BlockSpec(g.shape, lambda i, *_: (0, 0))],
        out_specs=pl.BlockSpec((tm, D), lambda i, *_: (i, 0)),
        scratch_shapes=[pltpu.VMEM((_FINAL_SLOTS, TOPK_IN_GROUP, tm // _SUBLANES, _SUBLANES, D), F32),
                        pltpu.SemaphoreType.DMA((_FINAL_SLOTS,))],
    )
    return pl.pallas_call(
        functools.partial(_final_body, tm=tm),
        grid_spec=grid_spec,
        out_shape=jax.ShapeDtypeStruct((T, D), F32),
        compiler_params=_params(),
        name="final",
    )(row, x2, gates, y_rows, g)


def _tile_sizes(T):
    pick = lambda want: want if T % want == 0 else CHUNK
    return dict(inproj=pick(256), dsa_q=pick(256), dsa_k=pick(512), mlstm=pick(256), mixout=pick(256),
                final=pick(256), route=pick(512), moe=256)


def _layer(x, mem, norm_mix_g, w_in, kv_norm_g, k_idx_norm_g, w_uk, w_uv, conv_w, conv_b, gate_b, ml_norm_g,
           w_out, norm_x_g, mem_norm_g, w_cq, w_ckv, w_co, norm_ffn_g, w_group, b_group, w_router, b_router,
           w_gate, w_up, w_down, out_g):
    T = x.shape[0]
    ts = _tile_sizes(T)
    r2 = lambda v: v.reshape(1, -1)

    w_r = _wprep(w_in, 256)
    wuk_t = jnp.transpose(w_uk, (1, 2, 0)).astype(BF16)
    wuv_t = jnp.transpose(w_uv, (1, 0, 2)).astype(BF16)

    qabs, ckv, qi, kidx, small, mqk, mv, mo = _inproj(
        x, r2(norm_mix_g), w_r, wuk_t, r2(kv_norm_g), r2(k_idx_norm_g), ts["inproj"])

    gate_rows = jnp.transpose(small[:, _S_WI:_S_MF + ML_HEADS])
    dsa_out = _dsa(qi, gate_rows, qabs, kidx, ckv, wuv_t, ts["dsa_q"], ts["dsa_k"])

    gb_col = jnp.zeros((1, _SMALL), F32).at[0, _S_MI:_S_MI + 2 * ML_HEADS].set(gate_b)
    ml_out = _mlstm(mqk, mv, small, gate_rows, mo, conv_w, r2(conv_b), gb_col, gate_b.reshape(-1, 1),
                    r2(ml_norm_g), ts["mlstm"])

    wqk, wvo = _memfold(mem, r2(mem_norm_g), w_ckv, w_cq, w_co)
    w_rt = jnp.concatenate([w_group, w_router,
                            jnp.zeros((D_MODEL, 128 - N_GROUPS - N_EXPERTS), w_group.dtype)], axis=1)
    x2, hf, rl = _mixout(x, dsa_out, ml_out, w_out.astype(BF16), wqk, wvo, w_rt.astype(BF16),
                         r2(norm_x_g), r2(norm_ffn_g), ts["mixout"])

    bm = ts["moe"]
    row, gap_lo, gap_hi, gates, e_blk, n_blk = _route(rl, b_group, b_router, bm, ts["route"])
    y_rows = _moe(row, gap_lo, gap_hi, e_blk, hf, w_gate, w_up, w_down, n_blk, bm)
    return _final(row, x2, gates, y_rows, r2(out_g), ts["final"])


def kernel(x, mem, norm_mix_g, w_in, kv_norm_g, k_idx_norm_g, w_uk, w_uv, conv_w, conv_b, gate_b, ml_norm_g,
           w_out, norm_x_g, mem_norm_g, w_cq, w_ckv, w_co, norm_ffn_g, w_group, b_group, w_router, b_router,
           w_gate, w_up, w_down, final_norm_g):
    B, T, D = x.shape
    assert B == 1 and D == D_MODEL and norm_mix_g.shape[0] == 1 and T % CHUNK == 0
    out = _layer(x[0], mem[0], norm_mix_g[0], w_in, kv_norm_g[0], k_idx_norm_g[0], w_uk[0], w_uv[0],
                 conv_w[0], conv_b[0], gate_b[0], ml_norm_g[0], w_out[0], norm_x_g[0], mem_norm_g[0],
                 w_cq[0], w_ckv[0], w_co[0], norm_ffn_g[0], w_group[0], b_group[0], w_router[0], b_router[0],
                 w_gate[0], w_up[0], w_down[0], final_norm_g)
    return out[None]
```

```python
import functools

import jax
import jax.numpy as jnp
import numpy as np
from jax import lax
from jax.experimental import pallas as pl
from jax.experimental.pallas import tpu as pltpu

F32 = jnp.float32
BF16 = jnp.bfloat16
I32 = jnp.int32
I16 = jnp.int16
U32 = jnp.uint32

EPS = 1e-6
CHUNK = 64
D_MODEL = 2048

DSA_HEADS = 8
DSA_HEAD_DIM = 128
DSA_LATENT = 256
IDX_HEADS = 8
IDX_DIM = 64
TOPK_MAX = 256

ML_HEADS = 4
ML_QK_DIM = 128
ML_V_DIM = 256
CONV_W = 4

X_HEADS = 4
X_HEAD_DIM = D_MODEL // X_HEADS

N_GROUPS = 4
EXP_PER_GROUP = 8
N_EXPERTS = N_GROUPS * EXP_PER_GROUP
TOPK_IN_GROUP = 2
D_EXPERT = 512

_O_DQ = 0
_O_CKV = _O_DQ + DSA_HEADS * DSA_HEAD_DIM
_O_QI = _O_CKV + DSA_LATENT
_O_KI = _O_QI + IDX_HEADS * IDX_DIM
_O_WI = _O_KI + IDX_DIM
_O_MQ = _O_WI + IDX_HEADS
_O_MK = _O_MQ + ML_HEADS * ML_QK_DIM
_O_MV = _O_MK + ML_HEADS * ML_QK_DIM
_O_MI = _O_MV + ML_HEADS * ML_V_DIM
_O_MF = _O_MI + ML_HEADS
_O_MO = _O_MF + ML_HEADS
_O_END = _O_MO + ML_HEADS * ML_V_DIM

_G_DQ = (0, 1024)
_G_CKV = (1024, 1280)
_G_QI = (1280, 1792)
_G_SMALL = (1792, 1920)
_G_MQK = (1920, 2944)
_G_MV = (2944, 3968)
_G_MO = (3968, 4992)
_W_COLS = 4992
_S_WI = IDX_DIM
_S_MI = _S_WI + IDX_HEADS
_S_MF = _S_MI + ML_HEADS
_SMALL = 128

_VMEM_LIMIT = 56 * 1024 * 1024
_INT_MIN = -(2 ** 31)
_I16_MIN = -(2 ** 15)
_CHUNK_SHIFT = CHUNK.bit_length() - 1
_LOG2E = 1.4426950408889634
_SUBLANES = 8
_LANES = 128
_GATHER_SLOTS = 9
_GATHER_AHEAD = _GATHER_SLOTS - 1
_TOPK_SHIFT = TOPK_IN_GROUP.bit_length() - 1
_FINAL_SLOTS = 3
_NEG = -1e30


def _rms(v, g):
    return v * lax.rsqrt(jnp.mean(v * v, axis=-1, keepdims=True) + EPS) * g


def _dot(a, b):
    return jnp.dot(a, b, preferred_element_type=F32)


def _dot_nt(a, b):
    return lax.dot_general(a, b, (((1,), (1,)), ((), ())), preferred_element_type=F32)


def _resident(shape):
    nd = len(shape)
    return pl.BlockSpec(shape, lambda *_: (0,) * nd, pipeline_mode=pl.Buffered(1))


def _params(n_axes=1):
    return pltpu.CompilerParams(dimension_semantics=("arbitrary",) * n_axes,
                                vmem_limit_bytes=_VMEM_LIMIT)


def _wprep_body(w_hbm, out_ref, wbuf, sem, *, tk):
    i = pl.program_id(0)
    slot = i % 2

    def rows(blk, s):
        return pltpu.make_async_copy(w_hbm.at[0, pl.ds(pl.multiple_of(blk * tk, tk), tk)], wbuf.at[s], sem.at[s])

    @pl.when(i == 0)
    def _():
        rows(0, 0).start()

    @pl.when(i + 1 < pl.num_programs(0))
    def _():
        rows(i + 1, 1 - slot).start()

    rows(i, slot).wait()
    w_ref = wbuf.at[slot]
    n_small = IDX_DIM + IDX_HEADS
    out_ref[:, _G_DQ[0]:_G_SMALL[0] + n_small] = w_ref[:, _O_DQ:_O_MQ].astype(BF16)
    out_ref[:, _G_SMALL[0] + n_small:_G_SMALL[0] + n_small + 2 * ML_HEADS] = w_ref[:, _O_MI:_O_MO].astype(BF16)
    out_ref[:, _G_SMALL[0] + n_small + 2 * ML_HEADS:_G_SMALL[1]] = jnp.zeros(
        (out_ref.shape[0], _SMALL - n_small - 2 * ML_HEADS), BF16)
    out_ref[:, _G_MQK[0]:_G_MV[1]] = w_ref[:, _O_MQ:_O_MI].astype(BF16)
    out_ref[:, _G_MO[0]:_G_MO[1]] = w_ref[:, _O_MO:_O_END].astype(BF16)


def _wprep(w_in, tk):
    K = w_in.shape[1]
    return pl.pallas_call(
        functools.partial(_wprep_body, tk=tk),
        grid=(K // tk,),
        in_specs=[pl.BlockSpec(memory_space=pl.ANY)],
        out_specs=pl.BlockSpec((tk, _W_COLS), lambda i: (i, 0)),
        out_shape=jax.ShapeDtypeStruct((K, _W_COLS), BF16),
        scratch_shapes=[pltpu.VMEM((2, tk, w_in.shape[2]), F32), pltpu.SemaphoreType.DMA((2,))],
        compiler_params=_params(),
        name="wprep",
    )(w_in)


def _inproj_body(x_ref, g_ref, wa_ref, ws_ref, wb_ref, wc_ref, wuk_ref, kvg_ref, kig_ref,
                 qabs_ref, ckv_ref, qi_ref, kidx_ref, small_ref, mqk_ref, mv_ref, mo_ref):
    h = _rms(x_ref[...], g_ref[...]).astype(BF16)

    def proj(grp):
        for ref, base in ((wa_ref, _G_DQ[0]), (ws_ref, _G_SMALL[0]), (wb_ref, _G_MQK[0]), (wc_ref, _G_MO[0])):
            if base <= grp[0] and grp[1] <= base + ref.shape[1]:
                return _dot(h, ref[:, grp[0] - base:grp[1] - base])
        raise ValueError(grp)

    dq = proj(_G_DQ)
    for hd in range(DSA_HEADS):
        qh = dq[:, hd * DSA_HEAD_DIM:(hd + 1) * DSA_HEAD_DIM].astype(BF16)
        qa = _dot(qh, wuk_ref[hd]) * (DSA_HEAD_DIM ** -0.5 * _LOG2E)
        qabs_ref[:, hd * DSA_LATENT:(hd + 1) * DSA_LATENT] = qa.astype(BF16)
    ckv_ref[...] = _rms(proj(_G_CKV), kvg_ref[...]).astype(BF16)
    qi_ref[...] = (proj(_G_QI) * (IDX_DIM ** -0.5)).astype(BF16)
    small = proj(_G_SMALL)
    small_ref[...] = small
    kidx_ref[...] = _rms(small[:, :IDX_DIM], kig_ref[...]).astype(BF16)
    mqk_ref[...] = proj(_G_MQK)
    mv_ref[...] = proj(_G_MV).astype(BF16)
    mo_ref[...] = proj(_G_MO)


def _inproj(x, g, w_parts, wuk, kvg, kig, tm):
    T = x.shape[0]
    row = lambda n: pl.BlockSpec((tm, n), lambda i: (i, 0))
    outs = [(8 * DSA_LATENT, BF16), (DSA_LATENT, BF16), (IDX_HEADS * IDX_DIM, BF16), (IDX_DIM, BF16),
            (_SMALL, F32), (2 * ML_HEADS * ML_QK_DIM, F32), (ML_HEADS * ML_V_DIM, BF16),
            (ML_HEADS * ML_V_DIM, F32)]
    return pl.pallas_call(
        _inproj_body,
        grid=(T // tm,),
        in_specs=[row(D_MODEL), _resident(g.shape)] + [_resident(w.shape) for w in w_parts] + [
            _resident(wuk.shape), _resident(kvg.shape), _resident(kig.shape)],
        out_specs=[row(n) for n, _ in outs],
        out_shape=[jax.ShapeDtypeStruct((T, n), dt) for n, dt in outs],
        compiler_params=_params(),
        name="inproj",
    )(x, g, *w_parts, wuk, kvg, kig)


def _sublane_fold(v, op, rows=_SUBLANES, ways=4):
    groups = [v[r * rows:(r + 1) * rows, :] for r in range(v.shape[0] // rows)]
    accs = groups[:ways]
    for r in range(ways, len(groups)):
        accs[r % ways] = op(accs[r % ways], groups[r])
    while len(accs) > 1:
        accs = [op(accs[k], accs[k + 1]) if k + 1 < len(accs) else accs[k] for k in range(0, len(accs), 2)]
    return accs[0]


def _bit_transpose32(words):
    a = list(words)
    j, m = 16, 0x0000FFFF
    while j:
        k = 0
        while k < 32:
            t = (a[k] ^ (a[k + j] >> j)) & jnp.uint32(m)
            a[k] = a[k] ^ t
            a[k + j] = a[k + j] ^ (t << j)
            k = (k + j + 1) & ~j
        j >>= 1
        m = (m ^ (m << j)) & 0xFFFFFFFF
    return a


def _dsa_body(qi_ref, wrow_ref, qabs_ref, kidx_ref, ckv_ref, ckvt_ref, wuv_ref, out_ref,
              key_ref, planes_ref, alive_ref, lga_ref, lgb_ref, m_ref, l_ref, acc_ref, *, tq, tk, topk, nbits_idx):
    i = pl.program_id(0)
    n_kb = ((i + 1) * tq + tk - 1) // tk
    w_rows = wrow_ref[0:IDX_HEADS, :] * (IDX_HEADS ** -0.5)
    q_chunk = (i * tq + lax.broadcasted_iota(I32, (1, tq), 1)) >> _CHUNK_SHIFT

    def key_pos(j):
        return j * tk + lax.broadcasted_iota(I32, (tk, 1), 0)

    plane_rows = tk // 32

    @pl.when(i == 0)
    def _():
        planes_ref[...] = jnp.zeros(planes_ref.shape, U32)

    def score_block(j, carry):
        kx = kidx_ref[pl.ds(pl.multiple_of(j * tk, tk), tk), :]
        s = jnp.zeros((tk, tq), F32)
        for hd in range(IDX_HEADS):
            d = _dot_nt(kx, qi_ref[:, hd * IDX_DIM:(hd + 1) * IDX_DIM])
            s = s + w_rows[hd:hd + 1, :] * jnp.maximum(d, 0.0)
        bits = lax.bitcast_convert_type(s, I32)
        key = bits ^ ((bits >> 31) & 0x7FFFFFFF)
        key = jnp.where((key_pos(j) >> _CHUNK_SHIFT) <= q_chunk, key, _INT_MIN)
        key_ref[j] = key
        u = lax.bitcast_convert_type(key, U32) ^ jnp.uint32(0x80000000)
        for h in range(plane_rows // _SUBLANES):
            base = h * 32 * _SUBLANES
            planes = _bit_transpose32([u[base + _SUBLANES * v:base + _SUBLANES * (v + 1), :] for v in range(32)])
            row0 = pl.multiple_of(j * plane_rows + h * _SUBLANES, _SUBLANES)
            for b in range(32):
                planes_ref[b, pl.ds(row0, _SUBLANES), :] = planes[31 - b]
        return carry

    lax.fori_loop(0, n_kb, score_block, 0)

    def count(pred):
        def body(j, acc):
            hit = pred(key_ref[j], key_pos(j)).astype(I32)
            return acc + _sublane_fold(hit, jnp.add)
        acc = lax.fori_loop(0, n_kb, body, jnp.zeros((8, tq), I32))
        return jnp.sum(acc, axis=0, keepdims=True)

    word_row = lax.broadcasted_iota(I32, (alive_ref.shape[0], 1), 0)
    alive_ref[...] = jnp.where(word_row < n_kb * plane_rows, jnp.full(alive_ref.shape, 0xFFFFFFFF, U32),
                               jnp.uint32(0))

    def radix_select(n_rows):
        def ones_count(words):
            return jnp.sum(_sublane_fold(lax.population_count(words).astype(I32), jnp.add), axis=0, keepdims=True)

        def select_bit(it, carry):
            need, t_u = carry
            b = 31 - it
            alive = alive_ref[0:n_rows, :]
            ones = alive & planes_ref[b, pl.ds(0, n_rows), :]
            n_ones = ones_count(ones)
            take = n_ones >= need
            alive_ref[0:n_rows, :] = jnp.where(take, ones, alive ^ ones)
            bit = lax.shift_left(jnp.uint32(1), jnp.asarray(b, U32))
            return jnp.where(take, need, need - n_ones), jnp.where(take, t_u | bit, t_u)

        need, t_u = lax.fori_loop(0, 32, select_bit, (jnp.full((1, tq), topk, I32), jnp.zeros((1, tq), U32)))
        return need, t_u, ones_count(alive_ref[0:n_rows, :])

    all_rows = alive_ref.shape[0]
    if all_rows % (2 * _SUBLANES) == 0:
        need, t_u, n_equal = lax.cond(n_kb * plane_rows <= all_rows // 2,
                                      lambda: radix_select(all_rows // 2), lambda: radix_select(all_rows))
    else:
        need, t_u, n_equal = radix_select(all_rows)
    short = t_u == 0
    t = jnp.maximum(lax.bitcast_convert_type(t_u ^ jnp.uint32(0x80000000), I32), _INT_MIN + 1)
    all_pos = jnp.int32(2 ** nbits_idx - 1)
    has_ties = (n_equal > need) & jnp.logical_not(short)
    n_tie_take = jnp.where(has_ties, need, all_pos)

    def tie_cutoff():
        def pos_bit(b, c):
            cand = c + lax.shift_left(jnp.int32(1), nbits_idx - 1 - b)
            f = count(lambda kb, pos: (kb == t) & (pos < cand))
            return jnp.where(f <= n_tie_take, cand, c)
        return lax.fori_loop(0, nbits_idx, pos_bit, jnp.zeros((1, tq), I32))

    cut = lax.cond(jnp.max(has_ties.astype(I32)) > 0, tie_cutoff, lambda: jnp.full((1, tq), all_pos, I32))

    m_ref[...] = jnp.full(m_ref.shape, _NEG, F32)
    l_ref[...] = jnp.zeros(l_ref.shape, F32)
    acc_ref[...] = jnp.zeros(acc_ref.shape, F32)
    last_blk = ckvt_ref.shape[0] - 1

    def logits(j, lg_buf):
        c_blk = ckv_ref[pl.ds(pl.multiple_of(jnp.minimum(j, last_blk) * tk, tk), tk), :]
        kb = key_ref[j]
        sel = (kb > t) | ((kb == t) & (key_pos(j) < cut))
        bias = jnp.where(sel, 0.0, _NEG).astype(F32)
        for hd in range(DSA_HEADS):
            lg_buf[hd] = _dot_nt(c_blk, qabs_ref[:, hd * DSA_LATENT:(hd + 1) * DSA_LATENT]) + bias

    def accumulate(j, lg_buf):
        c_blk_t = ckvt_ref[jnp.minimum(j, last_blk)]
        for hd in range(DSA_HEADS):
            lg = lg_buf[hd]
            m_old = m_ref[hd:hd + 1, :]
            m_new = jnp.maximum(m_old, jnp.max(_sublane_fold(lg, jnp.maximum, ways=1), axis=0, keepdims=True))
            p = jnp.exp2(lg - m_new)
            alpha = jnp.exp2(m_old - m_new)
            l_ref[hd:hd + 1, :] = alpha * l_ref[hd:hd + 1, :] + jnp.sum(_sublane_fold(p, jnp.add, ways=1), axis=0,
                                                                         keepdims=True)
            acc_ref[hd] = alpha * acc_ref[hd] + _dot(c_blk_t, p.astype(BF16))
            m_ref[hd:hd + 1, :] = m_new

    logits(0, lga_ref)

    def attn_pair(mi, carry):
        ja = 2 * mi
        accumulate(ja, lga_ref)
        logits(ja + 1, lgb_ref)
        accumulate(ja + 1, lgb_ref)
        logits(jnp.minimum(ja + 2, n_kb - 1), lga_ref)
        return carry

    lax.fori_loop(0, n_kb // 2, attn_pair, 0)

    @pl.when(n_kb % 2 == 1)
    def _():
        accumulate(n_kb - 1, lga_ref)

    for hd in range(DSA_HEADS):
        o_lat = (acc_ref[hd] / l_ref[hd:hd + 1, :]).T.astype(BF16)
        out_ref[:, hd * DSA_HEAD_DIM:(hd + 1) * DSA_HEAD_DIM] = _dot(o_lat, wuv_ref[hd]).astype(BF16)


def _dsa(qi, wrows, qabs, kidx, ckv, wuv, tq, tk):
    T = qi.shape[0]
    topk = min(TOPK_MAX, T // 4)
    n_kb = T // tk
    ckvt = jnp.transpose(ckv.reshape(n_kb, tk, DSA_LATENT), (0, 2, 1))
    row = lambda n: pl.BlockSpec((tq, n), lambda i: (i, 0))
    body = functools.partial(_dsa_body, tq=tq, tk=tk, topk=topk, nbits_idx=int(T).bit_length())
    return pl.pallas_call(
        body,
        grid=(T // tq,),
        in_specs=[row(qi.shape[1]), pl.BlockSpec((wrows.shape[0], tq), lambda i: (0, i)), row(qabs.shape[1]),
                  _resident(kidx.shape), _resident(ckv.shape), _resident(ckvt.shape), _resident(wuv.shape)],
        out_specs=row(DSA_HEADS * DSA_HEAD_DIM),
        out_shape=jax.ShapeDtypeStruct((T, DSA_HEADS * DSA_HEAD_DIM), BF16),
        scratch_shapes=[pltpu.VMEM((n_kb + n_kb % 2, tk, tq), I32),
                        pltpu.VMEM((32, T // 32, tq), U32), pltpu.VMEM((T // 32, tq), U32),
                        pltpu.VMEM((DSA_HEADS, tk, tq), F32), pltpu.VMEM((DSA_HEADS, tk, tq), F32),
                        pltpu.VMEM((DSA_HEADS, tq), F32),
                        pltpu.VMEM((DSA_HEADS, tq), F32), pltpu.VMEM((DSA_HEADS, DSA_LATENT, tq), F32)],
        compiler_params=_params(),
        name="dsa",
    )(qi, wrows, qabs, kidx, ckv, ckvt, wuv)


def _log_sigmoid(v):
    return jnp.minimum(v, 0.0) - jnp.log1p(jnp.exp(-jnp.abs(v)))


def _chunk_cumsum(v, axis):
    pos = lax.broadcasted_iota(I32, v.shape, axis) & (CHUNK - 1)
    d = 1
    while d < CHUNK:
        v = v + jnp.where(pos >= d, pltpu.roll(v, d, axis=axis), 0.0)
        d *= 2
    return v


def _mlstm_body(mqk_ref, mv_ref, small_ref, gt_ref, mo_ref, cw_ref, cb_ref, gbc_ref, gbr_ref, ng_ref,
                out_ref, xe_ref, c_ref, n_ref, m_ref, hs_ref, *, rows):
    @pl.when(pl.program_id(0) == 0)
    def _():
        xe_ref[0:8, :] = jnp.zeros((8, xe_ref.shape[1]), F32)
        c_ref[...] = jnp.zeros(c_ref.shape, F32)
        n_ref[...] = jnp.zeros(n_ref.shape, F32)
        m_ref[...] = jnp.zeros(m_ref.shape, F32)

    x = mqk_ref[...]
    xe_ref[8:8 + rows, :] = x
    y = cb_ref[...]
    for j in range(CONV_W - 1):
        y = y + xe_ref[5 + j:5 + j + rows, :] * cw_ref[j:j + 1, :]
    y = y + x * cw_ref[CONV_W - 1:CONV_W, :]
    xe_ref[0:8, :] = x[rows - 8:rows, :]
    qk = y * jax.nn.sigmoid(y)
    nqk = ML_HEADS * ML_QK_DIM
    q_all = (qk[:, :nqk] * (ML_QK_DIM ** -0.5)).astype(BF16)
    k_all = qk[:, nqk:]

    g_col = small_ref[...] + gbc_ref[...]
    g_row = gt_ref[...] + gbr_ref[...]
    b_col = _chunk_cumsum(_log_sigmoid(g_col), 0)
    b_row = _chunk_cumsum(_log_sigmoid(g_row), 1)

    tri = lax.broadcasted_iota(I32, (CHUNK, CHUNK), 1) <= lax.broadcasted_iota(I32, (CHUNK, CHUNK), 0)

    heads = range(ML_HEADS)
    for c in range(rows // CHUNK):
        lo, hi = c * CHUNK, (c + 1) * CHUNK
        bc = [b_col[lo:hi, _S_MF + hd:_S_MF + hd + 1] for hd in heads]
        lic = [g_col[lo:hi, _S_MI + hd:_S_MI + hd + 1] for hd in heads]
        br = [b_row[ML_HEADS + hd:ML_HEADS + hd + 1, lo:hi] for hd in heads]
        lir = [g_row[hd:hd + 1, lo:hi] for hd in heads]
        g_tot = [bc[hd][CHUNK - 1:CHUNK, :] for hd in heads]
        m_prev = [m_ref[hd][:, 0:1] for hd in heads]
        qh = [q_all[lo:hi, hd * ML_QK_DIM:(hd + 1) * ML_QK_DIM] for hd in heads]
        kh = [k_all[lo:hi, hd * ML_QK_DIM:(hd + 1) * ML_QK_DIM] for hd in heads]
        vh = [mv_ref[lo:hi, hd * ML_V_DIM:(hd + 1) * ML_V_DIM] for hd in heads]
        c_prev = [c_ref[hd] for hd in heads]
        n_prev = [n_ref[hd] for hd in heads]

        dmat = [jnp.where(tri, bc[hd] - br[hd] + lir[hd], -jnp.inf) for hd in heads]
        inter = [bc[hd] + m_prev[hd] for hd in heads]
        m_t = [jnp.maximum(inter[hd], jnp.max(dmat[hd], axis=-1, keepdims=True)) for hd in heads]
        w_intra = [jnp.exp(dmat[hd] - m_t[hd]) for hd in heads]
        a_inter = [jnp.exp(inter[hd] - m_t[hd]) for hd in heads]
        m_new = [jnp.maximum(g_tot[hd] + m_prev[hd],
                             jnp.max(g_tot[hd] - br[hd] + lir[hd], axis=-1, keepdims=True)) for hd in heads]
        a_state = [jnp.exp(g_tot[hd] + m_prev[hd] - m_new[hd]) for hd in heads]
        wk = [jnp.exp(g_tot[hd] - bc[hd] + lic[hd] - m_new[hd]) * kh[hd] for hd in heads]

        s_qk = [_dot_nt(qh[hd], kh[hd].astype(BF16)) * w_intra[hd] for hd in heads]
        read = [_dot(qh[hd], c_prev[hd].astype(BF16)) for hd in heads]
        d_c = [_dot(wk[hd].T.astype(BF16), vh[hd]) for hd in heads]
        num = [a_inter[hd] * read[hd] + _dot(s_qk[hd].astype(BF16), vh[hd]) for hd in heads]
        den = [a_inter[hd] * jnp.sum(qh[hd].astype(F32) * n_prev[hd], axis=-1, keepdims=True)
               + jnp.sum(s_qk[hd], axis=-1, keepdims=True) for hd in heads]
        for hd in heads:
            hs_ref[lo:hi, hd * ML_V_DIM:(hd + 1) * ML_V_DIM] = (
                num[hd] / jnp.maximum(jnp.abs(den[hd]), jnp.exp(-m_t[hd])))
        for hd in heads:
            c_ref[hd] = a_state[hd] * c_prev[hd] + d_c[hd]
            n_ref[hd] = a_state[hd] * n_prev[hd] + jnp.sum(wk[hd], axis=0, keepdims=True)
            m_ref[hd] = jnp.broadcast_to(m_new[hd], m_ref.shape[1:])

    for hd in range(ML_HEADS):
        sl = slice(hd * ML_V_DIM, (hd + 1) * ML_V_DIM)
        out_ref[:, sl] = (_rms(hs_ref[:, sl], ng_ref[:, sl]) * jax.nn.sigmoid(mo_ref[:, sl])).astype(BF16)


def _mlstm(mqk, mv, small, gt, mo, cw, cb, gbc, gbr, ng, rows):
    T = mqk.shape[0]
    row = lambda n: pl.BlockSpec((rows, n), lambda i: (i, 0))
    nv = ML_HEADS * ML_V_DIM
    return pl.pallas_call(
        functools.partial(_mlstm_body, rows=rows),
        grid=(T // rows,),
        in_specs=[row(mqk.shape[1]), row(nv), row(_SMALL), pl.BlockSpec((8, rows), lambda i: (1, i)), row(nv),
                  _resident(cw.shape), _resident(cb.shape), _resident(gbc.shape), _resident(gbr.shape),
                  _resident(ng.shape)],
        out_specs=row(nv),
        out_shape=jax.ShapeDtypeStruct((T, nv), BF16),
        scratch_shapes=[pltpu.VMEM((rows + 8, mqk.shape[1]), F32),
                        pltpu.VMEM((ML_HEADS, ML_QK_DIM, ML_V_DIM), F32),
                        pltpu.VMEM((ML_HEADS, 1, ML_QK_DIM), F32),
                        pltpu.VMEM((ML_HEADS, 1, 128), F32),
                        pltpu.VMEM((rows, nv), F32)],
        compiler_params=_params(),
        name="mlstm",
    )(mqk, mv, small, gt, mo, cw, cb, gbc, gbr, ng)


def _memfold_body(mem_ref, g_ref, wk_ref, wv_ref, wq_ref, wc_ref, wqk_ref, wvo_ref):
    mn = _rms(mem_ref[...], g_ref[...]).astype(BF16)
    k = _dot(mn, wk_ref[...].astype(BF16)).astype(BF16)
    v = _dot(mn, wv_ref[...].astype(BF16)).astype(BF16)
    wqk_ref[...] = (_dot_nt(wq_ref[...].astype(BF16), k) * (X_HEAD_DIM ** -0.5)).astype(BF16)
    wvo_ref[...] = _dot(v, wc_ref[...].astype(BF16)).astype(BF16)


def _memfold(mem, g, w_ckv, w_cq, w_co):
    M, D = mem.shape
    dh = X_HEAD_DIM
    return pl.pallas_call(
        _memfold_body,
        grid=(X_HEADS,),
        in_specs=[_resident(mem.shape), _resident(g.shape),
                  pl.BlockSpec((D, dh), lambda h: (0, h)), pl.BlockSpec((D, dh), lambda h: (0, X_HEADS + h)),
                  pl.BlockSpec((D, dh), lambda h: (0, h)), pl.BlockSpec((dh, D), lambda h: (h, 0))],
        out_specs=[pl.BlockSpec((D, M), lambda h: (0, h)), pl.BlockSpec((M, D), lambda h: (h, 0))],
        out_shape=[jax.ShapeDtypeStruct((D, X_HEADS * M), BF16), jax.ShapeDtypeStruct((X_HEADS * M, D), BF16)],
        compiler_params=_params(),
        name="memfold",
    )(mem, g, w_ckv, w_ckv, w_cq, w_co)


def _mixout_body(x_ref, dsa_ref, ml_ref, wo_ref, wqk_ref, wvo_ref, wr_ref, gx_ref, gf_ref,
                 x2_ref, hf_ref, rl_ref, p_ref):
    nd = dsa_ref.shape[1]
    x1 = x_ref[...] + _dot(dsa_ref[...], wo_ref[0:nd, :]) + _dot(ml_ref[...], wo_ref[nd:, :])
    lg_all = _dot(_rms(x1, gx_ref[...]).astype(BF16), wqk_ref[...])
    n_mem = wqk_ref.shape[1] // X_HEADS
    for hd in range(X_HEADS):
        sl = slice(hd * n_mem, (hd + 1) * n_mem)
        lg = lg_all[:, sl]
        e = jnp.exp(lg - jnp.max(lg, axis=-1, keepdims=True))
        p_ref[:, sl] = (e / jnp.sum(e, axis=-1, keepdims=True)).astype(BF16)
    x2 = x1 + _dot(p_ref[...], wvo_ref[...])
    x2_ref[...] = x2
    hf = _rms(x2, gf_ref[...]).astype(BF16)
    rl_ref[...] = _dot(hf, wr_ref[...])
    bits = lax.bitcast_convert_type(hf.astype(F32), U32)
    for c in range(_SUBLANES):
        lo = bits[:, (2 * c) * _LANES:(2 * c + 1) * _LANES]
        hi = bits[:, (2 * c + 1) * _LANES:(2 * c + 2) * _LANES]
        hf_ref[pl.ds(c, x2.shape[0], stride=_SUBLANES), :] = (hi & jnp.uint32(0xFFFF0000)) | (lo >> 16)


def _mixout(x, dsa, ml, wo, wqk, wvo, wr, gx, gf, tm):
    T = x.shape[0]
    row = lambda n: pl.BlockSpec((tm, n), lambda i: (i, 0))
    return pl.pallas_call(
        _mixout_body,
        grid=(T // tm,),
        in_specs=[row(D_MODEL), row(dsa.shape[1]), row(ml.shape[1]), _resident(wo.shape),
                  _resident(wqk.shape), _resident(wvo.shape), _resident(wr.shape), _resident(gx.shape),
                  _resident(gf.shape)],
        out_specs=[row(D_MODEL), pl.BlockSpec((tm * _SUBLANES, _LANES), lambda i: (i, 0)), row(wr.shape[1])],
        out_shape=[jax.ShapeDtypeStruct((T, D_MODEL), F32),
                   jax.ShapeDtypeStruct((T * _SUBLANES, _LANES), U32),
                   jax.ShapeDtypeStruct((T, wr.shape[1]), F32)],
        scratch_shapes=[pltpu.VMEM((tm, wqk.shape[1]), BF16)],
        compiler_params=_params(),
        name="mixout",
    )(x, dsa, ml, wo, wqk, wvo, wr, gx, gf)


def _moe_body(row_ref, gap_lo_ref, gap_hi_ref, eb_ref, hf_hbm, wg_ref, wu_ref, wd_ref, y_hbm,
              tok_ref, xbuf, ybuf, wgb, wub, wdb, gsem, ysem, *, bm, n_blk):
    e = pl.program_id(0)

    @pl.when(e == 0)
    def _():
        def clear_gap(g, carry):
            def clear(r, c):
                tok_ref[r] = 0
                return c
            return lax.fori_loop(gap_lo_ref[g], gap_hi_ref[g], clear, carry)

        def place(a, carry):
            tok_ref[row_ref[a]] = lax.shift_right_logical(a, _TOPK_SHIFT)
            return carry

        lax.fori_loop(0, gap_lo_ref.shape[0], clear_gap, 0)
        lax.fori_loop(0, row_ref.shape[0], place, 0, unroll=8)

    n_valid = eb_ref[N_EXPERTS]
    b_lo = eb_ref[e]
    b_hi = eb_ref[e + 1]

    def tok_words(tok):
        return hf_hbm.at[pl.ds(pl.multiple_of(tok * _SUBLANES, _SUBLANES), _SUBLANES)]

    def gather(blk, s):
        def issue(g, carry):
            for u in range(_SUBLANES):
                r = g * _SUBLANES + u
                pltpu.make_async_copy(tok_words(tok_ref[blk * bm + r]),
                                      xbuf.at[s, pl.ds(pl.multiple_of(r * _SUBLANES, _SUBLANES), _SUBLANES)],
                                      gsem.at[s]).start(priority=1)
            return carry
        lax.fori_loop(0, bm // _SUBLANES, issue, 0)

    def wait_gather(s):
        pltpu.make_async_copy(xbuf.at[s], xbuf.at[s], gsem.at[s]).wait()

    def y_copy(blk, s):
        return pltpu.make_async_copy(ybuf.at[s], y_hbm.at[pl.ds(pl.multiple_of(blk * bm, bm), bm)], ysem.at[s])

    @pl.when(e == 0)
    def _():
        for k in range(_GATHER_AHEAD):
            gather(k, k)

    @pl.when(b_hi > b_lo)
    def _():
        wgb[...] = wg_ref[0].astype(BF16)
        wub[...] = wu_ref[0].astype(BF16)
        wdb[...] = wd_ref[0].astype(BF16)

        def block(b, carry):
            s = b % _GATHER_SLOTS
            wait_gather(s)
            chunks = []
            for c in range(_SUBLANES):
                w = xbuf[s, pl.ds(c, bm, stride=_SUBLANES), :]
                chunks.append(lax.bitcast_convert_type(w << 16, F32).astype(BF16))
                chunks.append(lax.bitcast_convert_type(w & jnp.uint32(0xFFFF0000), F32).astype(BF16))
            xb = jnp.concatenate(chunks, axis=-1)
            gate = _dot(xb, wgb[...])
            a = gate * jax.nn.sigmoid(gate) * _dot(xb, wub[...])
            y = _dot(a.astype(BF16), wdb[...])

            nxt = jnp.minimum(b + _GATHER_AHEAD, n_blk - 1)
            for r in range(bm):
                pltpu.make_async_copy(tok_words(tok_ref[nxt * bm + r]),
                                      xbuf.at[(b + _GATHER_AHEAD) % _GATHER_SLOTS, pl.ds(r * _SUBLANES, _SUBLANES)],
                                      gsem.at[(b + _GATHER_AHEAD) % _GATHER_SLOTS]).start(priority=1)

            @pl.when(b >= 2)
            def _():
                y_copy(b - 2, b % 2).wait()

            ybuf[b % 2] = y
            y_copy(b, b % 2).start()
            return carry

        lax.fori_loop(b_lo, b_hi, block, 0)

    @pl.when(e == pl.num_programs(0) - 1)
    def _():
        for k in range(_GATHER_AHEAD):
            wait_gather((n_valid + k) % _GATHER_SLOTS)

        @pl.when(n_valid >= 2)
        def _():
            y_copy(n_valid - 2, n_valid % 2).wait()
        y_copy(n_valid - 1, (n_valid - 1) % 2).wait()
        ybuf[0] = jnp.zeros(ybuf.shape[1:], F32)

        def zero_block(b, carry):
            cp = y_copy(b, 0)
            cp.start()
            cp.wait()
            return carry

        lax.fori_loop(n_valid, n_blk, zero_block, 0)


def _moe(row, gap_lo, gap_hi, e_blk, hf, wg, wu, wd, n_blk, bm):
    D = wg.shape[1]
    wspec = lambda shape: pl.BlockSpec((1,) + shape, lambda e, *_: (e, 0, 0))
    grid_spec = pltpu.PrefetchScalarGridSpec(
        num_scalar_prefetch=4,
        grid=(N_EXPERTS,),
        in_specs=[pl.BlockSpec(memory_space=pl.ANY),
                  wspec((D, D_EXPERT)), wspec((D, D_EXPERT)), wspec((D_EXPERT, D))],
        out_specs=pl.BlockSpec(memory_space=pl.ANY),
        scratch_shapes=[pltpu.SMEM((n_blk * bm,), I32),
                        pltpu.VMEM((_GATHER_SLOTS, bm * _SUBLANES, _LANES), U32),
                        pltpu.VMEM((2, bm, D), F32),
                        pltpu.VMEM((D, D_EXPERT), BF16), pltpu.VMEM((D, D_EXPERT), BF16),
                        pltpu.VMEM((D_EXPERT, D), BF16),
                        pltpu.SemaphoreType.DMA((_GATHER_SLOTS,)), pltpu.SemaphoreType.DMA((2,))],
    )
    return pl.pallas_call(
        functools.partial(_moe_body, bm=bm, n_blk=n_blk),
        grid_spec=grid_spec,
        out_shape=jax.ShapeDtypeStruct((n_blk * bm, D), F32),
        compiler_params=_params(),
        name="moe",
    )(row, gap_lo, gap_hi, e_blk, hf, wg, wu, wd)


def _route_body(rl_ref, bias_ref, out_ref, cnt_ref, seen_ref, *, tm):
    @pl.when(pl.program_id(0) == 0)
    def _():
        seen_ref[...] = jnp.zeros(seen_ref.shape, F32)

    lg = rl_ref[...] + bias_ref[...]
    lane = lax.broadcasted_iota(I32, lg.shape, 1).astype(F32)
    first = lambda hit: jnp.min(jnp.where(hit, lane, float(_LANES)), axis=-1, keepdims=True)
    is_group = lane < N_GROUPS
    gl = jnp.where(is_group, lg, -jnp.inf)
    g_max = jnp.max(gl, axis=-1, keepdims=True)
    g_sel = first(gl == g_max)
    p_g = 1.0 / jnp.sum(jnp.where(is_group, jnp.exp(gl - g_max), 0.0), axis=-1, keepdims=True)

    e_id = lane - N_GROUPS
    in_group = (e_id >= 0) & (e_id < N_EXPERTS) & (jnp.floor(e_id / EXP_PER_GROUP) == g_sel)
    el = jnp.where(in_group, lg, -jnp.inf)
    ee = jnp.where(in_group, jnp.exp(el - jnp.max(el, axis=-1, keepdims=True)), 0.0)
    cand = jnp.where(in_group, ee / jnp.sum(ee, axis=-1, keepdims=True), -1.0)
    p1 = jnp.max(cand, axis=-1, keepdims=True)
    l1 = first(cand == p1)
    cand = jnp.where(lane == l1, -2.0, cand)
    p2 = jnp.max(cand, axis=-1, keepdims=True)
    l2 = first(cand == p2)
    g1 = p_g * p1 / (p1 + p2)
    g2 = p_g * p2 / (p1 + p2)

    oh1 = (lane == l1).astype(BF16)
    oh2 = (lane == l2).astype(BF16)
    both = oh1 + oh2
    earlier = (lax.broadcasted_iota(I32, (tm, tm), 1) < lax.broadcasted_iota(I32, (tm, tm), 0)).astype(BF16)
    before = _dot(earlier, both) + seen_ref[...]
    r1 = jnp.sum(before * oh1.astype(F32), axis=-1, keepdims=True)
    r2 = jnp.sum(before * oh2.astype(F32), axis=-1, keepdims=True)
    seen_ref[...] = seen_ref[...] + jnp.sum(both.astype(F32), axis=0, keepdims=True)
    cnt_ref[...] = seen_ref[...]

    cols = (l1 - N_GROUPS, l2 - N_GROUPS, r1, r2, g1, g2)
    out = jnp.zeros(lg.shape, F32)
    for c, v in enumerate(cols):
        out = jnp.where(lane == c, v, out)
    out_ref[...] = out


def _route_tokens(rl, bias, tm):
    T = rl.shape[0]
    return pl.pallas_call(
        functools.partial(_route_body, tm=tm),
        grid=(T // tm,),
        in_specs=[pl.BlockSpec((tm, _LANES), lambda i: (i, 0)), _resident(bias.shape)],
        out_specs=[pl.BlockSpec((tm, _LANES), lambda i: (i, 0)), pl.BlockSpec((1, _LANES), lambda i: (0, 0))],
        out_shape=[jax.ShapeDtypeStruct((T, _LANES), F32), jax.ShapeDtypeStruct((1, _LANES), F32)],
        scratch_shapes=[pltpu.VMEM((1, _LANES), F32)],
        compiler_params=_params(),
        name="route",
    )(rl, bias)


def _route(rl, b_group, b_router, bm, tm):
    N = rl.shape[0]
    bias = jnp.concatenate([b_group, b_router, jnp.zeros((_LANES - N_GROUPS - N_EXPERTS,), F32)]).reshape(1, _LANES)
    per_tok, seen = _route_tokens(rl, bias, tm)
    expert_id = per_tok[:, 0:TOPK_IN_GROUP].astype(I32)
    rank = per_tok[:, TOPK_IN_GROUP:2 * TOPK_IN_GROUP].astype(I32)
    gates = per_tok[:, 2 * TOPK_IN_GROUP:3 * TOPK_IN_GROUP]
    counts = seen[0, N_GROUPS:N_GROUPS + N_EXPERTS].astype(I32)

    A = N * TOPK_IN_GROUP
    padded = (counts + bm - 1) // bm * bm
    pad_ends = jnp.cumsum(padded)
    pad_starts = pad_ends - padded
    onehot = expert_id[..., None] == jnp.arange(N_EXPERTS, dtype=I32)
    row = (jnp.sum(jnp.where(onehot, pad_starts, 0), axis=-1) + rank).reshape(A)
    n_blk = -(-A // bm) + N_EXPERTS
    gap_lo = jnp.concatenate([pad_starts + counts, pad_ends[-1:]]).astype(I32)
    gap_hi = jnp.concatenate([pad_ends, jnp.full((1,), n_blk * bm, I32)]).astype(I32)
    e_blk = jnp.concatenate([pad_starts, pad_ends[-1:]]) // bm
    return row, gap_lo, gap_hi, gates, e_blk.astype(I32), n_blk


def _final_body(row_ref, x_ref, gate_ref, y_hbm, g_ref, out_ref, ybuf, sem, *, tm):
    i = pl.program_id(0)
    n_tiles = pl.num_programs(0)
    slot = i % _FINAL_SLOTS

    def gather(tile, s):
        def issue(g, carry):
            for u in range(_SUBLANES):
                for k in range(TOPK_IN_GROUP):
                    src = row_ref[(tile * tm + g * _SUBLANES + u) * TOPK_IN_GROUP + k]
                    pltpu.make_async_copy(y_hbm.at[pl.ds(src, 1)], ybuf.at[s, k, g, pl.ds(u, 1)],
                                          sem.at[s]).start()
            return carry
        lax.fori_loop(0, tm // _SUBLANES, issue, 0)

    def wait_rows(s):
        pltpu.make_async_copy(ybuf.at[s], ybuf.at[s], sem.at[s]).wait()

    @pl.when(i == 0)
    def _():
        for t in range(_FINAL_SLOTS - 1):
            gather(jnp.minimum(t, n_tiles - 1), t)

    wait_rows(slot)
    acc = x_ref[...]
    for k in range(TOPK_IN_GROUP):
        acc = acc + gate_ref[:, k:k + 1] * ybuf[slot, k].reshape(tm, ybuf.shape[-1])
    y = _rms(acc, g_ref[...])

    ahead = _FINAL_SLOTS - 1
    nxt = jnp.minimum(i + ahead, n_tiles - 1)
    for r in range(tm):
        for k in range(TOPK_IN_GROUP):
            src = row_ref[(nxt * tm + r) * TOPK_IN_GROUP + k]
            pltpu.make_async_copy(y_hbm.at[pl.ds(src, 1)],
                                  ybuf.at[(i + ahead) % _FINAL_SLOTS, k, r // _SUBLANES, pl.ds(r % _SUBLANES, 1)],
                                  sem.at[(i + ahead) % _FINAL_SLOTS]).start()
    out_ref[...] = y

    @pl.when(i == n_tiles - 1)
    def _():
        for t in range(1, _FINAL_SLOTS):
            wait_rows((i + t) % _FINAL_SLOTS)


def _final(row, x2, gates, y_rows, g, tm):
    T, D = x2.shape
    grid_spec = pltpu.PrefetchScalarGridSpec(
        num_scalar_prefetch=1,
        grid=(T // tm,),
        in_specs=[pl.BlockSpec((tm, D), lambda i, *_: (i, 0)),
                  pl.BlockSpec((tm, TOPK_IN_GROUP), lambda i, *_: (i, 0)),
                  pl.BlockSpec(memory_space=pl.ANY),
                  pl.BlockSpec(g.shape, lambda i, *_: (0, 0))],
        out_specs=pl.BlockSpec((tm, D), lambda i, *_: (i, 0)),
        scratch_shapes=[pltpu.VMEM((_FINAL_SLOTS, TOPK_IN_GROUP, tm // _SUBLANES, _SUBLANES, D), F32),
                        pltpu.SemaphoreType.DMA((_FINAL_SLOTS,))],
    )
    return pl.pallas_call(
        functools.partial(_final_body, tm=tm),
        grid_spec=grid_spec,
        out_shape=jax.ShapeDtypeStruct((T, D), F32),
        compiler_params=_params(),
        name="final",
    )(row, x2, gates, y_rows, g)


def _tile_sizes(T):
    pick = lambda want: want if T % want == 0 else CHUNK
    return dict(inproj=pick(256), dsa_q=pick(256), dsa_k=pick(512), mlstm=pick(256), mixout=pick(256),
                final=pick(256), route=pick(512), moe=128)


def _layer(x, mem, norm_mix_g, w_in, kv_norm_g, k_idx_norm_g, w_uk, w_uv, conv_w, conv_b, gate_b, ml_norm_g,
           w_out, norm_x_g, mem_norm_g, w_cq, w_ckv, w_co, norm_ffn_g, w_group, b_group, w_router, b_router,
           w_gate, w_up, w_down, out_g):
    T = x.shape[0]
    ts = _tile_sizes(T)
    r2 = lambda v: v.reshape(1, -1)

    w2 = w_in[0]
    w_parts = (w2[:, _O_DQ:_O_KI].astype(BF16),
               jnp.concatenate([w2[:, _O_KI:_O_MQ], w2[:, _O_MI:_O_MO],
                                jnp.zeros((D_MODEL, _SMALL - IDX_DIM - IDX_HEADS - 2 * ML_HEADS), w2.dtype)],
                               axis=1).astype(BF16),
               w2[:, _O_MQ:_O_MI].astype(BF16), w2[:, _O_MO:_O_END].astype(BF16))
    wuk_t = jnp.transpose(w_uk, (1, 2, 0)).astype(BF16)
    wuv_t = jnp.transpose(w_uv, (1, 0, 2)).astype(BF16)

    qabs, ckv, qi, kidx, small, mqk, mv, mo = _inproj(
        x, r2(norm_mix_g), w_parts, wuk_t, r2(kv_norm_g), r2(k_idx_norm_g), ts["inproj"])

    gate_rows = jnp.transpose(small[:, _S_WI:_S_MF + ML_HEADS])
    dsa_out = _dsa(qi, gate_rows, qabs, kidx, ckv, wuv_t, ts["dsa_q"], ts["dsa_k"])

    gb_col = jnp.zeros((1, _SMALL), F32).at[0, _S_MI:_S_MI + 2 * ML_HEADS].set(gate_b)
    ml_out = _mlstm(mqk, mv, small, gate_rows, mo, conv_w, r2(conv_b), gb_col, gate_b.reshape(-1, 1),
                    r2(ml_norm_g), ts["mlstm"])

    wqk, wvo = _memfold(mem, r2(mem_norm_g), w_ckv, w_cq, w_co)
    w_rt = jnp.concatenate([w_group, w_router,
                            jnp.zeros((D_MODEL, 128 - N_GROUPS - N_EXPERTS), w_group.dtype)], axis=1)
    x2, hf, rl = _mixout(x, dsa_out, ml_out, w_out.astype(BF16), wqk, wvo, w_rt.astype(BF16),
                         r2(norm_x_g), r2(norm_ffn_g), ts["mixout"])

    bm = ts["moe"]
    row, gap_lo, gap_hi, gates, e_blk, n_blk = _route(rl, b_group, b_router, bm, ts["route"])
    y_rows = _moe(row, gap_lo, gap_hi, e_blk, hf, w_gate, w_up, w_down, n_blk, bm)
    return _final(row, x2, gates, y_rows, r2(out_g), ts["final"])


def kernel(x, mem, norm_mix_g, w_in, kv_norm_g, k_idx_norm_g, w_uk, w_uv, conv_w, conv_b, gate_b, ml_norm_g,
           w_out, norm_x_g, mem_norm_g, w_cq, w_ckv, w_co, norm_ffn_g, w_group, b_group, w_router, b_router,
           w_gate, w_up, w_down, final_norm_g):
    B, T, D = x.shape
    assert B == 1 and D == D_MODEL and norm_mix_g.shape[0] == 1 and T % CHUNK == 0
    out = _layer(x[0], mem[0], norm_mix_g[0], w_in, kv_norm_g[0], k_idx_norm_g[0], w_uk[0], w_uv[0],
                 conv_w[0], conv_b[0], gate_b[0], ml_norm_g[0], w_out[0], norm_x_g[0], mem_norm_g[0],
                 w_cq[0], w_ckv[0], w_co[0], norm_ffn_g[0], w_group[0], b_group[0], w_router[0], b_router[0],
                 w_gate[0], w_up[0], w_down[0], final_norm_g)
    return out[None]
```

```python
import functools

import jax
import jax.numpy as jnp
import numpy as np
from jax import lax
from jax.experimental import pallas as pl
from jax.experimental.pallas import tpu as pltpu

F32 = jnp.float32
BF16 = jnp.bfloat16
I32 = jnp.int32
I16 = jnp.int16
U32 = jnp.uint32

EPS = 1e-6
CHUNK = 64
D_MODEL = 2048

DSA_HEADS = 8
DSA_HEAD_DIM = 128
DSA_LATENT = 256
IDX_HEADS = 8
IDX_DIM = 64
TOPK_MAX = 256

ML_HEADS = 4
ML_QK_DIM = 128
ML_V_DIM = 256
CONV_W = 4

X_HEADS = 4
X_HEAD_DIM = D_MODEL // X_HEADS

N_GROUPS = 4
EXP_PER_GROUP = 8
N_EXPERTS = N_GROUPS * EXP_PER_GROUP
TOPK_IN_GROUP = 2
D_EXPERT = 512

_O_DQ = 0
_O_CKV = _O_DQ + DSA_HEADS * DSA_HEAD_DIM
_O_QI = _O_CKV + DSA_LATENT
_O_KI = _O_QI + IDX_HEADS * IDX_DIM
_O_WI = _O_KI + IDX_DIM
_O_MQ = _O_WI + IDX_HEADS
_O_MK = _O_MQ + ML_HEADS * ML_QK_DIM
_O_MV = _O_MK + ML_HEADS * ML_QK_DIM
_O_MI = _O_MV + ML_HEADS * ML_V_DIM
_O_MF = _O_MI + ML_HEADS
_O_MO = _O_MF + ML_HEADS
_O_END = _O_MO + ML_HEADS * ML_V_DIM

_G_DQ = (0, 1024)
_G_CKV = (1024, 1280)
_G_QI = (1280, 1792)
_G_SMALL = (1792, 1920)
_G_MQK = (1920, 2944)
_G_MV = (2944, 3968)
_G_MO = (3968, 4992)
_W_COLS = 4992
_S_WI = IDX_DIM
_S_MI = _S_WI + IDX_HEADS
_S_MF = _S_MI + ML_HEADS
_SMALL = 128

_VMEM_LIMIT = 56 * 1024 * 1024
_INT_MIN = -(2 ** 31)
_I16_MIN = -(2 ** 15)
_CHUNK_SHIFT = CHUNK.bit_length() - 1
_LOG2E = 1.4426950408889634
_SUBLANES = 8
_LANES = 128
_GATHER_SLOTS = 13
_GATHER_AHEAD = _GATHER_SLOTS - 1
_TOPK_SHIFT = TOPK_IN_GROUP.bit_length() - 1
_FINAL_SLOTS = 3
_NEG = -1e30


def _rms(v, g):
    return v * lax.rsqrt(jnp.mean(v * v, axis=-1, keepdims=True) + EPS) * g


def _dot(a, b):
    return jnp.dot(a, b, preferred_element_type=F32)


def _dot_nt(a, b):
    return lax.dot_general(a, b, (((1,), (1,)), ((), ())), preferred_element_type=F32)


def _resident(shape):
    nd = len(shape)
    return pl.BlockSpec(shape, lambda *_: (0,) * nd, pipeline_mode=pl.Buffered(1))


def _params(n_axes=1):
    return pltpu.CompilerParams(dimension_semantics=("arbitrary",) * n_axes,
                                vmem_limit_bytes=_VMEM_LIMIT)


def _wprep_body(w_hbm, out_ref, wbuf, sem, *, tk):
    i = pl.program_id(0)
    slot = i % 2

    def rows(blk, s):
        return pltpu.make_async_copy(w_hbm.at[0, pl.ds(pl.multiple_of(blk * tk, tk), tk)], wbuf.at[s], sem.at[s])

    @pl.when(i == 0)
    def _():
        rows(0, 0).start()

    @pl.when(i + 1 < pl.num_programs(0))
    def _():
        rows(i + 1, 1 - slot).start()

    rows(i, slot).wait()
    w_ref = wbuf.at[slot]
    n_small = IDX_DIM + IDX_HEADS
    out_ref[:, _G_DQ[0]:_G_SMALL[0] + n_small] = w_ref[:, _O_DQ:_O_MQ].astype(BF16)
    out_ref[:, _G_SMALL[0] + n_small:_G_SMALL[0] + n_small + 2 * ML_HEADS] = w_ref[:, _O_MI:_O_MO].astype(BF16)
    out_ref[:, _G_SMALL[0] + n_small + 2 * ML_HEADS:_G_SMALL[1]] = jnp.zeros(
        (out_ref.shape[0], _SMALL - n_small - 2 * ML_HEADS), BF16)
    out_ref[:, _G_MQK[0]:_G_MV[1]] = w_ref[:, _O_MQ:_O_MI].astype(BF16)
    out_ref[:, _G_MO[0]:_G_MO[1]] = w_ref[:, _O_MO:_O_END].astype(BF16)


def _wprep(w_in, tk):
    K = w_in.shape[1]
    return pl.pallas_call(
        functools.partial(_wprep_body, tk=tk),
        grid=(K // tk,),
        in_specs=[pl.BlockSpec(memory_space=pl.ANY)],
        out_specs=pl.BlockSpec((tk, _W_COLS), lambda i: (i, 0)),
        out_shape=jax.ShapeDtypeStruct((K, _W_COLS), BF16),
        scratch_shapes=[pltpu.VMEM((2, tk, w_in.shape[2]), F32), pltpu.SemaphoreType.DMA((2,))],
        compiler_params=_params(),
        name="wprep",
    )(w_in)


def _inproj_body(x_ref, g_ref, w_ref, wuk_ref, kvg_ref, kig_ref,
                 qabs_ref, ckv_ref, qi_ref, kidx_ref, small_ref, mqk_ref, mv_ref, mo_ref):
    h = _rms(x_ref[...], g_ref[...]).astype(BF16)

    def proj(grp):
        return _dot(h, w_ref[:, grp[0]:grp[1]])

    dq = proj(_G_DQ)
    for hd in range(DSA_HEADS):
        qh = dq[:, hd * DSA_HEAD_DIM:(hd + 1) * DSA_HEAD_DIM].astype(BF16)
        qa = _dot(qh, wuk_ref[hd]) * (DSA_HEAD_DIM ** -0.5 * _LOG2E)
        qabs_ref[:, hd * DSA_LATENT:(hd + 1) * DSA_LATENT] = qa.astype(BF16)
    ckv_ref[...] = _rms(proj(_G_CKV), kvg_ref[...]).astype(BF16)
    qi_ref[...] = (proj(_G_QI) * (IDX_DIM ** -0.5)).astype(BF16)
    small = proj(_G_SMALL)
    small_ref[...] = small
    kidx_ref[...] = _rms(small[:, :IDX_DIM], kig_ref[...]).astype(BF16)
    mqk_ref[...] = proj(_G_MQK)
    mv_ref[...] = proj(_G_MV).astype(BF16)
    mo_ref[...] = proj(_G_MO)


def _inproj(x, g, w, wuk, kvg, kig, tm):
    T = x.shape[0]
    row = lambda n: pl.BlockSpec((tm, n), lambda i: (i, 0))
    outs = [(8 * DSA_LATENT, BF16), (DSA_LATENT, BF16), (IDX_HEADS * IDX_DIM, BF16), (IDX_DIM, BF16),
            (_SMALL, F32), (2 * ML_HEADS * ML_QK_DIM, F32), (ML_HEADS * ML_V_DIM, BF16),
            (ML_HEADS * ML_V_DIM, F32)]
    return pl.pallas_call(
        _inproj_body,
        grid=(T // tm,),
        in_specs=[row(D_MODEL), _resident(g.shape), _resident(w.shape), _resident(wuk.shape),
                  _resident(kvg.shape), _resident(kig.shape)],
        out_specs=[row(n) for n, _ in outs],
        out_shape=[jax.ShapeDtypeStruct((T, n), dt) for n, dt in outs],
        compiler_params=_params(),
        name="inproj",
    )(x, g, w, wuk, kvg, kig)


def _sublane_fold(v, op, rows=_SUBLANES, ways=4):
    groups = [v[r * rows:(r + 1) * rows, :] for r in range(v.shape[0] // rows)]
    accs = groups[:ways]
    for r in range(ways, len(groups)):
        accs[r % ways] = op(accs[r % ways], groups[r])
    while len(accs) > 1:
        accs = [op(accs[k], accs[k + 1]) if k + 1 < len(accs) else accs[k] for k in range(0, len(accs), 2)]
    return accs[0]


def _bit_transpose32(words):
    a = list(words)
    j, m = 16, 0x0000FFFF
    while j:
        k = 0
        while k < 32:
            t = (a[k] ^ (a[k + j] >> j)) & jnp.uint32(m)
            a[k] = a[k] ^ t
            a[k + j] = a[k + j] ^ (t << j)
            k = (k + j + 1) & ~j
        j >>= 1
        m = (m ^ (m << j)) & 0xFFFFFFFF
    return a


def _dsa_body(qi_ref, wrow_ref, qabs_ref, kidx_ref, ckv_ref, ckvt_ref, wuv_ref, out_ref,
              key_ref, planes_ref, alive_ref, lga_ref, lgb_ref, m_ref, l_ref, acc_ref, *, tq, tk, topk, nbits_idx):
    i = pl.program_id(0)
    n_kb = ((i + 1) * tq + tk - 1) // tk
    w_rows = wrow_ref[0:IDX_HEADS, :] * (IDX_HEADS ** -0.5)
    q_chunk = (i * tq + lax.broadcasted_iota(I32, (1, tq), 1)) >> _CHUNK_SHIFT

    def key_pos(j):
        return j * tk + lax.broadcasted_iota(I32, (tk, 1), 0)

    plane_rows = tk // 32

    @pl.when(i == 0)
    def _():
        planes_ref[...] = jnp.zeros(planes_ref.shape, U32)

    def score_block(j, carry):
        kx = kidx_ref[pl.ds(pl.multiple_of(j * tk, tk), tk), :]
        s = jnp.zeros((tk, tq), F32)
        for hd in range(IDX_HEADS):
            d = _dot_nt(kx, qi_ref[:, hd * IDX_DIM:(hd + 1) * IDX_DIM])
            s = s + w_rows[hd:hd + 1, :] * jnp.maximum(d, 0.0)
        bits = lax.bitcast_convert_type(s, I32)
        key = bits ^ ((bits >> 31) & 0x7FFFFFFF)
        key = jnp.where((key_pos(j) >> _CHUNK_SHIFT) <= q_chunk, key, _INT_MIN)
        key_ref[j] = key
        u = lax.bitcast_convert_type(key, U32) ^ jnp.uint32(0x80000000)
        for h in range(plane_rows // _SUBLANES):
            base = h * 32 * _SUBLANES
            planes = _bit_transpose32([u[base + _SUBLANES * v:base + _SUBLANES * (v + 1), :] for v in range(32)])
            row0 = pl.multiple_of(j * plane_rows + h * _SUBLANES, _SUBLANES)
            for b in range(32):
                planes_ref[b, pl.ds(row0, _SUBLANES), :] = planes[31 - b]
        return carry

    lax.fori_loop(0, n_kb, score_block, 0)

    def count(pred):
        def body(j, acc):
            hit = pred(key_ref[j], key_pos(j)).astype(I32)
            return acc + _sublane_fold(hit, jnp.add)
        acc = lax.fori_loop(0, n_kb, body, jnp.zeros((8, tq), I32))
        return jnp.sum(acc, axis=0, keepdims=True)

    word_row = lax.broadcasted_iota(I32, (alive_ref.shape[0], 1), 0)
    alive_ref[...] = jnp.where(word_row < n_kb * plane_rows, jnp.full(alive_ref.shape, 0xFFFFFFFF, U32),
                               jnp.uint32(0))

    def radix_select(n_rows):
        def ones_count(words):
            return jnp.sum(_sublane_fold(lax.population_count(words).astype(I32), jnp.add), axis=0, keepdims=True)

        def select_bit(it, carry):
            need, t_u = carry
            b = 31 - it
            alive = alive_ref[0:n_rows, :]
            ones = alive & planes_ref[b, pl.ds(0, n_rows), :]
            n_ones = ones_count(ones)
            take = n_ones >= need
            alive_ref[0:n_rows, :] = jnp.where(take, ones, alive ^ ones)
            bit = lax.shift_left(jnp.uint32(1), jnp.asarray(b, U32))
            return jnp.where(take, need, need - n_ones), jnp.where(take, t_u | bit, t_u)

        need, t_u = lax.fori_loop(0, 32, select_bit, (jnp.full((1, tq), topk, I32), jnp.zeros((1, tq), U32)))
        return need, t_u, ones_count(alive_ref[0:n_rows, :])

    all_rows = alive_ref.shape[0]
    if all_rows % (2 * _SUBLANES) == 0:
        need, t_u, n_equal = lax.cond(n_kb * plane_rows <= all_rows // 2,
                                      lambda: radix_select(all_rows // 2), lambda: radix_select(all_rows))
    else:
        need, t_u, n_equal = radix_select(all_rows)
    short = t_u == 0
    t = jnp.maximum(lax.bitcast_convert_type(t_u ^ jnp.uint32(0x80000000), I32), _INT_MIN + 1)
    all_pos = jnp.int32(2 ** nbits_idx - 1)
    has_ties = (n_equal > need) & jnp.logical_not(short)
    n_tie_take = jnp.where(has_ties, need, all_pos)

    def tie_cutoff():
        def pos_bit(b, c):
            cand = c + lax.shift_left(jnp.int32(1), nbits_idx - 1 - b)
            f = count(lambda kb, pos: (kb == t) & (pos < cand))
            return jnp.where(f <= n_tie_take, cand, c)
        return lax.fori_loop(0, nbits_idx, pos_bit, jnp.zeros((1, tq), I32))

    cut = lax.cond(jnp.max(has_ties.astype(I32)) > 0, tie_cutoff, lambda: jnp.full((1, tq), all_pos, I32))

    m_ref[...] = jnp.full(m_ref.shape, _NEG, F32)
    l_ref[...] = jnp.zeros(l_ref.shape, F32)
    acc_ref[...] = jnp.zeros(acc_ref.shape, F32)
    last_blk = ckvt_ref.shape[0] - 1

    def logits(j, lg_buf):
        c_blk = ckv_ref[pl.ds(pl.multiple_of(jnp.minimum(j, last_blk) * tk, tk), tk), :]
        kb = key_ref[j]
        sel = (kb > t) | ((kb == t) & (key_pos(j) < cut))
        bias = jnp.where(sel, 0.0, _NEG).astype(F32)
        for hd in range(DSA_HEADS):
            lg_buf[hd] = _dot_nt(c_blk, qabs_ref[:, hd * DSA_LATENT:(hd + 1) * DSA_LATENT]) + bias

    def accumulate(j, lg_buf):
        c_blk_t = ckvt_ref[jnp.minimum(j, last_blk)]
        for hd in range(DSA_HEADS):
            lg = lg_buf[hd]
            m_old = m_ref[hd:hd + 1, :]
            m_new = jnp.maximum(m_old, jnp.max(_sublane_fold(lg, jnp.maximum, ways=1), axis=0, keepdims=True))
            p = jnp.exp2(lg - m_new)
            alpha = jnp.exp2(m_old - m_new)
            l_ref[hd:hd + 1, :] = alpha * l_ref[hd:hd + 1, :] + jnp.sum(_sublane_fold(p, jnp.add, ways=1), axis=0,
                                                                         keepdims=True)
            acc_ref[hd] = alpha * acc_ref[hd] + _dot(c_blk_t, p.astype(BF16))
            m_ref[hd:hd + 1, :] = m_new

    logits(0, lga_ref)

    def attn_pair(mi, carry):
        ja = 2 * mi
        accumulate(ja, lga_ref)
        logits(ja + 1, lgb_ref)
        accumulate(ja + 1, lgb_ref)
        logits(jnp.minimum(ja + 2, n_kb - 1), lga_ref)
        return carry

    lax.fori_loop(0, n_kb // 2, attn_pair, 0)

    @pl.when(n_kb % 2 == 1)
    def _():
        accumulate(n_kb - 1, lga_ref)

    for hd in range(DSA_HEADS):
        o_lat = (acc_ref[hd] / l_ref[hd:hd + 1, :]).T.astype(BF16)
        out_ref[:, hd * DSA_HEAD_DIM:(hd + 1) * DSA_HEAD_DIM] = _dot(o_lat, wuv_ref[hd]).astype(BF16)


def _dsa(qi, wrows, qabs, kidx, ckv, wuv, tq, tk):
    T = qi.shape[0]
    topk = min(TOPK_MAX, T // 4)
    n_kb = T // tk
    ckvt = jnp.transpose(ckv.reshape(n_kb, tk, DSA_LATENT), (0, 2, 1))
    row = lambda n: pl.BlockSpec((tq, n), lambda i: (i, 0))
    body = functools.partial(_dsa_body, tq=tq, tk=tk, topk=topk, nbits_idx=int(T).bit_length())
    return pl.pallas_call(
        body,
        grid=(T // tq,),
        in_specs=[row(qi.shape[1]), pl.BlockSpec((wrows.shape[0], tq), lambda i: (0, i)), row(qabs.shape[1]),
                  _resident(kidx.shape), _resident(ckv.shape), _resident(ckvt.shape), _resident(wuv.shape)],
        out_specs=row(DSA_HEADS * DSA_HEAD_DIM),
        out_shape=jax.ShapeDtypeStruct((T, DSA_HEADS * DSA_HEAD_DIM), BF16),
        scratch_shapes=[pltpu.VMEM((n_kb + n_kb % 2, tk, tq), I32),
                        pltpu.VMEM((32, T // 32, tq), U32), pltpu.VMEM((T // 32, tq), U32),
                        pltpu.VMEM((DSA_HEADS, tk, tq), F32), pltpu.VMEM((DSA_HEADS, tk, tq), F32),
                        pltpu.VMEM((DSA_HEADS, tq), F32),
                        pltpu.VMEM((DSA_HEADS, tq), F32), pltpu.VMEM((DSA_HEADS, DSA_LATENT, tq), F32)],
        compiler_params=_params(),
        name="dsa",
    )(qi, wrows, qabs, kidx, ckv, ckvt, wuv)


def _log_sigmoid(v):
    return jnp.minimum(v, 0.0) - jnp.log1p(jnp.exp(-jnp.abs(v)))


def _chunk_cumsum(v, axis):
    pos = lax.broadcasted_iota(I32, v.shape, axis) & (CHUNK - 1)
    d = 1
    while d < CHUNK:
        v = v + jnp.where(pos >= d, pltpu.roll(v, d, axis=axis), 0.0)
        d *= 2
    return v


def _mlstm_body(mqk_ref, mv_ref, small_ref, gt_ref, mo_ref, cw_ref, cb_ref, gbc_ref, gbr_ref, ng_ref,
                out_ref, xe_ref, c_ref, n_ref, m_ref, hs_ref, *, rows):
    @pl.when(pl.program_id(0) == 0)
    def _():
        xe_ref[0:8, :] = jnp.zeros((8, xe_ref.shape[1]), F32)
        c_ref[...] = jnp.zeros(c_ref.shape, F32)
        n_ref[...] = jnp.zeros(n_ref.shape, F32)
        m_ref[...] = jnp.zeros(m_ref.shape, F32)

    x = mqk_ref[...]
    xe_ref[8:8 + rows, :] = x
    y = cb_ref[...]
    for j in range(CONV_W - 1):
        y = y + xe_ref[5 + j:5 + j + rows, :] * cw_ref[j:j + 1, :]
    y = y + x * cw_ref[CONV_W - 1:CONV_W, :]
    xe_ref[0:8, :] = x[rows - 8:rows, :]
    qk = y * jax.nn.sigmoid(y)
    nqk = ML_HEADS * ML_QK_DIM
    q_all = (qk[:, :nqk] * (ML_QK_DIM ** -0.5)).astype(BF16)
    k_all = qk[:, nqk:]

    g_col = small_ref[...] + gbc_ref[...]
    g_row = gt_ref[...] + gbr_ref[...]
    b_col = _chunk_cumsum(_log_sigmoid(g_col), 0)
    b_row = _chunk_cumsum(_log_sigmoid(g_row), 1)

    tri = lax.broadcasted_iota(I32, (CHUNK, CHUNK), 1) <= lax.broadcasted_iota(I32, (CHUNK, CHUNK), 0)

    heads = range(ML_HEADS)
    for c in range(rows // CHUNK):
        lo, hi = c * CHUNK, (c + 1) * CHUNK
        bc = [b_col[lo:hi, _S_MF + hd:_S_MF + hd + 1] for hd in heads]
        lic = [g_col[lo:hi, _S_MI + hd:_S_MI + hd + 1] for hd in heads]
        br = [b_row[ML_HEADS + hd:ML_HEADS + hd + 1, lo:hi] for hd in heads]
        lir = [g_row[hd:hd + 1, lo:hi] for hd in heads]
        g_tot = [bc[hd][CHUNK - 1:CHUNK, :] for hd in heads]
        m_prev = [m_ref[hd][:, 0:1] for hd in heads]
        qh = [q_all[lo:hi, hd * ML_QK_DIM:(hd + 1) * ML_QK_DIM] for hd in heads]
        kh = [k_all[lo:hi, hd * ML_QK_DIM:(hd + 1) * ML_QK_DIM] for hd in heads]
        vh = [mv_ref[lo:hi, hd * ML_V_DIM:(hd + 1) * ML_V_DIM] for hd in heads]
        c_prev = [c_ref[hd] for hd in heads]
        n_prev = [n_ref[hd] for hd in heads]

        dmat = [jnp.where(tri, bc[hd] - br[hd] + lir[hd], -jnp.inf) for hd in heads]
        inter = [bc[hd] + m_prev[hd] for hd in heads]
        m_t = [jnp.maximum(inter[hd], jnp.max(dmat[hd], axis=-1, keepdims=True)) for hd in heads]
        w_intra = [jnp.exp(dmat[hd] - m_t[hd]) for hd in heads]
        a_inter = [jnp.exp(inter[hd] - m_t[hd]) for hd in heads]
        m_new = [jnp.maximum(g_tot[hd] + m_prev[hd],
                             jnp.max(g_tot[hd] - br[hd] + lir[hd], axis=-1, keepdims=True)) for hd in heads]
        a_state = [jnp.exp(g_tot[hd] + m_prev[hd] - m_new[hd]) for hd in heads]
        wk = [jnp.exp(g_tot[hd] - bc[hd] + lic[hd] - m_new[hd]) * kh[hd] for hd in heads]

        s_qk = [_dot_nt(qh[hd], kh[hd].astype(BF16)) * w_intra[hd] for hd in heads]
        read = [_dot(qh[hd], c_prev[hd].astype(BF16)) for hd in heads]
        d_c = [_dot(wk[hd].T.astype(BF16), vh[hd]) for hd in heads]
        num = [a_inter[hd] * read[hd] + _dot(s_qk[hd].astype(BF16), vh[hd]) for hd in heads]
        den = [a_inter[hd] * jnp.sum(qh[hd].astype(F32) * n_prev[hd], axis=-1, keepdims=True)
               + jnp.sum(s_qk[hd], axis=-1, keepdims=True) for hd in heads]
        for hd in heads:
            hs_ref[lo:hi, hd * ML_V_DIM:(hd + 1) * ML_V_DIM] = (
                num[hd] / jnp.maximum(jnp.abs(den[hd]), jnp.exp(-m_t[hd])))
        for hd in heads:
            c_ref[hd] = a_state[hd] * c_prev[hd] + d_c[hd]
            n_ref[hd] = a_state[hd] * n_prev[hd] + jnp.sum(wk[hd], axis=0, keepdims=True)
            m_ref[hd] = jnp.broadcast_to(m_new[hd], m_ref.shape[1:])

    for hd in range(ML_HEADS):
        sl = slice(hd * ML_V_DIM, (hd + 1) * ML_V_DIM)
        out_ref[:, sl] = (_rms(hs_ref[:, sl], ng_ref[:, sl]) * jax.nn.sigmoid(mo_ref[:, sl])).astype(BF16)


def _mlstm(mqk, mv, small, gt, mo, cw, cb, gbc, gbr, ng, rows):
    T = mqk.shape[0]
    row = lambda n: pl.BlockSpec((rows, n), lambda i: (i, 0))
    nv = ML_HEADS * ML_V_DIM
    return pl.pallas_call(
        functools.partial(_mlstm_body, rows=rows),
        grid=(T // rows,),
        in_specs=[row(mqk.shape[1]), row(nv), row(_SMALL), pl.BlockSpec((8, rows), lambda i: (1, i)), row(nv),
                  _resident(cw.shape), _resident(cb.shape), _resident(gbc.shape), _resident(gbr.shape),
                  _resident(ng.shape)],
        out_specs=row(nv),
        out_shape=jax.ShapeDtypeStruct((T, nv), BF16),
        scratch_shapes=[pltpu.VMEM((rows + 8, mqk.shape[1]), F32),
                        pltpu.VMEM((ML_HEADS, ML_QK_DIM, ML_V_DIM), F32),
                        pltpu.VMEM((ML_HEADS, 1, ML_QK_DIM), F32),
                        pltpu.VMEM((ML_HEADS, 1, 128), F32),
                        pltpu.VMEM((rows, nv), F32)],
        compiler_params=_params(),
        name="mlstm",
    )(mqk, mv, small, gt, mo, cw, cb, gbc, gbr, ng)


def _memfold_body(mem_ref, g_ref, wk_ref, wv_ref, wq_ref, wc_ref, wqk_ref, wvo_ref):
    mn = _rms(mem_ref[...], g_ref[...]).astype(BF16)
    k = _dot(mn, wk_ref[...].astype(BF16)).astype(BF16)
    v = _dot(mn, wv_ref[...].astype(BF16)).astype(BF16)
    wqk_ref[...] = (_dot_nt(wq_ref[...].astype(BF16), k) * (X_HEAD_DIM ** -0.5)).astype(BF16)
    wvo_ref[...] = _dot(v, wc_ref[...].astype(BF16)).astype(BF16)


def _memfold(mem, g, w_ckv, w_cq, w_co):
    M, D = mem.shape
    dh = X_HEAD_DIM
    return pl.pallas_call(
        _memfold_body,
        grid=(X_HEADS,),
        in_specs=[_resident(mem.shape), _resident(g.shape),
                  pl.BlockSpec((D, dh), lambda h: (0, h)), pl.BlockSpec((D, dh), lambda h: (0, X_HEADS + h)),
                  pl.BlockSpec((D, dh), lambda h: (0, h)), pl.BlockSpec((dh, D), lambda h: (h, 0))],
        out_specs=[pl.BlockSpec((D, M), lambda h: (0, h)), pl.BlockSpec((M, D), lambda h: (h, 0))],
        out_shape=[jax.ShapeDtypeStruct((D, X_HEADS * M), BF16), jax.ShapeDtypeStruct((X_HEADS * M, D), BF16)],
        compiler_params=_params(),
        name="memfold",
    )(mem, g, w_ckv, w_ckv, w_cq, w_co)


def _mixout_body(x_ref, dsa_ref, ml_ref, wo_ref, wqk_ref, wvo_ref, wr_ref, gx_ref, gf_ref,
                 x2_ref, hf_ref, rl_ref, p_ref):
    nd = dsa_ref.shape[1]
    x1 = x_ref[...] + _dot(dsa_ref[...], wo_ref[0:nd, :]) + _dot(ml_ref[...], wo_ref[nd:, :])
    lg_all = _dot(_rms(x1, gx_ref[...]).astype(BF16), wqk_ref[...])
    n_mem = wqk_ref.shape[1] // X_HEADS
    for hd in range(X_HEADS):
        sl = slice(hd * n_mem, (hd + 1) * n_mem)
        lg = lg_all[:, sl]
        e = jnp.exp(lg - jnp.max(lg, axis=-1, keepdims=True))
        p_ref[:, sl] = (e / jnp.sum(e, axis=-1, keepdims=True)).astype(BF16)
    x2 = x1 + _dot(p_ref[...], wvo_ref[...])
    x2_ref[...] = x2
    hf = _rms(x2, gf_ref[...]).astype(BF16)
    rl_ref[...] = _dot(hf, wr_ref[...])
    bits = lax.bitcast_convert_type(hf.astype(F32), U32)
    for c in range(_SUBLANES):
        lo = bits[:, (2 * c) * _LANES:(2 * c + 1) * _LANES]
        hi = bits[:, (2 * c + 1) * _LANES:(2 * c + 2) * _LANES]
        hf_ref[pl.ds(c, x2.shape[0], stride=_SUBLANES), :] = (hi & jnp.uint32(0xFFFF0000)) | (lo >> 16)


def _mixout(x, dsa, ml, wo, wqk, wvo, wr, gx, gf, tm):
    T = x.shape[0]
    row = lambda n: pl.BlockSpec((tm, n), lambda i: (i, 0))
    return pl.pallas_call(
        _mixout_body,
        grid=(T // tm,),
        in_specs=[row(D_MODEL), row(dsa.shape[1]), row(ml.shape[1]), _resident(wo.shape),
                  _resident(wqk.shape), _resident(wvo.shape), _resident(wr.shape), _resident(gx.shape),
                  _resident(gf.shape)],
        out_specs=[row(D_MODEL), pl.BlockSpec((tm * _SUBLANES, _LANES), lambda i: (i, 0)), row(wr.shape[1])],
        out_shape=[jax.ShapeDtypeStruct((T, D_MODEL), F32),
                   jax.ShapeDtypeStruct((T * _SUBLANES, _LANES), U32),
                   jax.ShapeDtypeStruct((T, wr.shape[1]), F32)],
        scratch_shapes=[pltpu.VMEM((tm, wqk.shape[1]), BF16)],
        compiler_params=_params(),
        name="mixout",
    )(x, dsa, ml, wo, wqk, wvo, wr, gx, gf)


def _moe_body(row_ref, gap_lo_ref, gap_hi_ref, eb_ref, hf_hbm, wg_ref, wu_ref, wd_ref, y_hbm,
              tok_ref, xbuf, ybuf, wgb, wub, wdb, gsem, ysem, *, bm, n_blk):
    e = pl.program_id(0)

    @pl.when(e == 0)
    def _():
        def clear_gap(g, carry):
            def clear(r, c):
                tok_ref[r] = 0
                return c
            return lax.fori_loop(gap_lo_ref[g], gap_hi_ref[g], clear, carry)

        def place(a, carry):
            tok_ref[row_ref[a]] = lax.shift_right_logical(a, _TOPK_SHIFT)
            return carry

        lax.fori_loop(0, gap_lo_ref.shape[0], clear_gap, 0)
        lax.fori_loop(0, row_ref.shape[0], place, 0, unroll=16)

    n_valid = eb_ref[N_EXPERTS]
    b_lo = eb_ref[e]
    b_hi = eb_ref[e + 1]

    def tok_words(tok):
        return hf_hbm.at[pl.ds(pl.multiple_of(tok * _SUBLANES, _SUBLANES), _SUBLANES)]

    def gather(blk, s):
        def issue(g, carry):
            for u in range(_SUBLANES):
                r = g * _SUBLANES + u
                pltpu.make_async_copy(tok_words(tok_ref[blk * bm + r]),
                                      xbuf.at[s, pl.ds(pl.multiple_of(r * _SUBLANES, _SUBLANES), _SUBLANES)],
                                      gsem.at[s]).start(priority=1)
            return carry
        lax.fori_loop(0, bm // _SUBLANES, issue, 0)

    def wait_gather(s):
        pltpu.make_async_copy(xbuf.at[s], xbuf.at[s], gsem.at[s]).wait()

    def y_copy(blk, s):
        return pltpu.make_async_copy(ybuf.at[s], y_hbm.at[pl.ds(pl.multiple_of(blk * bm, bm), bm)], ysem.at[s])

    @pl.when(e == 0)
    def _():
        for k in range(_GATHER_AHEAD):
            gather(k, k)

    @pl.when(b_hi > b_lo)
    def _():
        wgb[...] = wg_ref[0].astype(BF16)
        wub[...] = wu_ref[0].astype(BF16)
        wdb[...] = wd_ref[0].astype(BF16)

        def block(b, carry):
            s = b % _GATHER_SLOTS
            wait_gather(s)
            chunks = []
            for c in range(_SUBLANES):
                w = xbuf[s, pl.ds(c, bm, stride=_SUBLANES), :]
                chunks.append(lax.bitcast_convert_type(w << 16, F32).astype(BF16))
                chunks.append(lax.bitcast_convert_type(w & jnp.uint32(0xFFFF0000), F32).astype(BF16))
            xb = jnp.concatenate(chunks, axis=-1)
            gate = _dot(xb, wgb[...])
            a = gate * jax.nn.sigmoid(gate) * _dot(xb, wub[...])
            y = _dot(a.astype(BF16), wdb[...])

            nxt = jnp.minimum(b + _GATHER_AHEAD, n_blk - 1)
            for r in range(bm):
                pltpu.make_async_copy(tok_words(tok_ref[nxt * bm + r]),
                                      xbuf.at[(b + _GATHER_AHEAD) % _GATHER_SLOTS, pl.ds(r * _SUBLANES, _SUBLANES)],
                                      gsem.at[(b + _GATHER_AHEAD) % _GATHER_SLOTS]).start(priority=1)

            @pl.when(b >= 2)
            def _():
                y_copy(b - 2, b % 2).wait()

            ybuf[b % 2] = y
            y_copy(b, b % 2).start()
            return carry

        lax.fori_loop(b_lo, b_hi, block, 0)

    @pl.when(e == pl.num_programs(0) - 1)
    def _():
        for k in range(_GATHER_AHEAD):
            wait_gather((n_valid + k) % _GATHER_SLOTS)

        @pl.when(n_valid >= 2)
        def _():
            y_copy(n_valid - 2, n_valid % 2).wait()
        y_copy(n_valid - 1, (n_valid - 1) % 2).wait()
        ybuf[0] = jnp.zeros(ybuf.shape[1:], F32)

        def zero_block(b, carry):
            cp = y_copy(b, 0)
            cp.start()
            cp.wait()
            return carry

        lax.fori_loop(n_valid, n_blk, zero_block, 0)


def _moe(row, gap_lo, gap_hi, e_blk, hf, wg, wu, wd, n_blk, bm):
    D = wg.shape[1]
    wspec = lambda shape: pl.BlockSpec((1,) + shape, lambda e, *_: (e, 0, 0))
    grid_spec = pltpu.PrefetchScalarGridSpec(
        num_scalar_prefetch=4,
        grid=(N_EXPERTS,),
        in_specs=[pl.BlockSpec(memory_space=pl.ANY),
                  wspec((D, D_EXPERT)), wspec((D, D_EXPERT)), wspec((D_EXPERT, D))],
        out_specs=pl.BlockSpec(memory_space=pl.ANY),
        scratch_shapes=[pltpu.SMEM((n_blk * bm,), I32),
                        pltpu.VMEM((_GATHER_SLOTS, bm * _SUBLANES, _LANES), U32),
                        pltpu.VMEM((2, bm, D), F32),
                        pltpu.VMEM((D, D_EXPERT), BF16), pltpu.VMEM((D, D_EXPERT), BF16),
                        pltpu.VMEM((D_EXPERT, D), BF16),
                        pltpu.SemaphoreType.DMA((_GATHER_SLOTS,)), pltpu.SemaphoreType.DMA((2,))],
    )
    return pl.pallas_call(
        functools.partial(_moe_body, bm=bm, n_blk=n_blk),
        grid_spec=grid_spec,
        out_shape=jax.ShapeDtypeStruct((n_blk * bm, D), F32),
        compiler_params=_params(),
        name="moe",
    )(row, gap_lo, gap_hi, e_blk, hf, wg, wu, wd)


def _route_body(rl_ref, bias_ref, out_ref, cnt_ref, seen_ref, *, tm):
    @pl.when(pl.program_id(0) == 0)
    def _():
        seen_ref[...] = jnp.zeros(seen_ref.shape, F32)

    lg = rl_ref[...] + bias_ref[...]
    lane = lax.broadcasted_iota(I32, lg.shape, 1).astype(F32)
    first = lambda hit: jnp.min(jnp.where(hit, lane, float(_LANES)), axis=-1, keepdims=True)
    is_group = lane < N_GROUPS
    gl = jnp.where(is_group, lg, -jnp.inf)
    g_max = jnp.max(gl, axis=-1, keepdims=True)
    g_sel = first(gl == g_max)
    p_g = 1.0 / jnp.sum(jnp.where(is_group, jnp.exp(gl - g_max), 0.0), axis=-1, keepdims=True)

    e_id = lane - N_GROUPS
    in_group = (e_id >= 0) & (e_id < N_EXPERTS) & (jnp.floor(e_id / EXP_PER_GROUP) == g_sel)
    el = jnp.where(in_group, lg, -jnp.inf)
    ee = jnp.where(in_group, jnp.exp(el - jnp.max(el, axis=-1, keepdims=True)), 0.0)
    cand = jnp.where(in_group, ee / jnp.sum(ee, axis=-1, keepdims=True), -1.0)
    p1 = jnp.max(cand, axis=-1, keepdims=True)
    l1 = first(cand == p1)
    cand = jnp.where(lane == l1, -2.0, cand)
    p2 = jnp.max(cand, axis=-1, keepdims=True)
    l2 = first(cand == p2)
    g1 = p_g * p1 / (p1 + p2)
    g2 = p_g * p2 / (p1 + p2)

    oh1 = (lane == l1).astype(BF16)
    oh2 = (lane == l2).astype(BF16)
    both = oh1 + oh2
    earlier = (lax.broadcasted_iota(I32, (tm, tm), 1) < lax.broadcasted_iota(I32, (tm, tm), 0)).astype(BF16)
    before = _dot(earlier, both) + seen_ref[...]
    r1 = jnp.sum(before * oh1.astype(F32), axis=-1, keepdims=True)
    r2 = jnp.sum(before * oh2.astype(F32), axis=-1, keepdims=True)
    seen_ref[...] = seen_ref[...] + jnp.sum(both.astype(F32), axis=0, keepdims=True)
    cnt_ref[...] = seen_ref[...]

    cols = (l1 - N_GROUPS, l2 - N_GROUPS, r1, r2, g1, g2)
    out = jnp.zeros(lg.shape, F32)
    for c, v in enumerate(cols):
        out = jnp.where(lane == c, v, out)
    out_ref[...] = out


def _route_tokens(rl, bias, tm):
    T = rl.shape[0]
    return pl.pallas_call(
        functools.partial(_route_body, tm=tm),
        grid=(T // tm,),
        in_specs=[pl.BlockSpec((tm, _LANES), lambda i: (i, 0)), _resident(bias.shape)],
        out_specs=[pl.BlockSpec((tm, _LANES), lambda i: (i, 0)), pl.BlockSpec((1, _LANES), lambda i: (0, 0))],
        out_shape=[jax.ShapeDtypeStruct((T, _LANES), F32), jax.ShapeDtypeStruct((1, _LANES), F32)],
        scratch_shapes=[pltpu.VMEM((1, _LANES), F32)],
        compiler_params=_params(),
        name="route",
    )(rl, bias)


def _route(rl, b_group, b_router, bm, tm):
    N = rl.shape[0]
    bias = jnp.concatenate([b_group, b_router, jnp.zeros((_LANES - N_GROUPS - N_EXPERTS,), F32)]).reshape(1, _LANES)
    per_tok, seen = _route_tokens(rl, bias, tm)
    expert_id = per_tok[:, 0:TOPK_IN_GROUP].astype(I32)
    rank = per_tok[:, TOPK_IN_GROUP:2 * TOPK_IN_GROUP].astype(I32)
    gates = per_tok[:, 2 * TOPK_IN_GROUP:3 * TOPK_IN_GROUP]
    counts = seen[0, N_GROUPS:N_GROUPS + N_EXPERTS].astype(I32)

    A = N * TOPK_IN_GROUP
    padded = (counts + bm - 1) // bm * bm
    pad_ends = jnp.cumsum(padded)
    pad_starts = pad_ends - padded
    onehot = expert_id[..., None] == jnp.arange(N_EXPERTS, dtype=I32)
    row = (jnp.sum(jnp.where(onehot, pad_starts, 0), axis=-1) + rank).reshape(A)
    n_blk = -(-A // bm) + N_EXPERTS
    gap_lo = jnp.concatenate([pad_starts + counts, pad_ends[-1:]]).astype(I32)
    gap_hi = jnp.concatenate([pad_ends, jnp.full((1,), n_blk * bm, I32)]).astype(I32)
    e_blk = jnp.concatenate([pad_starts, pad_ends[-1:]]) // bm
    return row, gap_lo, gap_hi, gates, e_blk.astype(I32), n_blk


def _final_body(row_ref, x_ref, gate_ref, y_hbm, g_ref, out_ref, ybuf, sem, *, tm):
    i = pl.program_id(0)
    n_tiles = pl.num_programs(0)
    slot = i % _FINAL_SLOTS

    def gather(tile, s):
        def issue(g, carry):
            for u in range(_SUBLANES):
                for k in range(TOPK_IN_GROUP):
                    src = row_ref[(tile * tm + g * _SUBLANES + u) * TOPK_IN_GROUP + k]
                    pltpu.make_async_copy(y_hbm.at[pl.ds(src, 1)], ybuf.at[s, k, g, pl.ds(u, 1)],
                                          sem.at[s]).start()
            return carry
        lax.fori_loop(0, tm // _SUBLANES, issue, 0)

    def wait_rows(s):
        pltpu.make_async_copy(ybuf.at[s], ybuf.at[s], sem.at[s]).wait()

    @pl.when(i == 0)
    def _():
        for t in range(_FINAL_SLOTS - 1):
            gather(jnp.minimum(t, n_tiles - 1), t)

    wait_rows(slot)
    acc = x_ref[...]
    for k in range(TOPK_IN_GROUP):
        acc = acc + gate_ref[:, k:k + 1] * ybuf[slot, k].reshape(tm, ybuf.shape[-1])
    y = _rms(acc, g_ref[...])

    ahead = _FINAL_SLOTS - 1
    nxt = jnp.minimum(i + ahead, n_tiles - 1)
    for r in range(tm):
        for k in range(TOPK_IN_GROUP):
            src = row_ref[(nxt * tm + r) * TOPK_IN_GROUP + k]
            pltpu.make_async_copy(y_hbm.at[pl.ds(src, 1)],
                                  ybuf.at[(i + ahead) % _FINAL_SLOTS, k, r // _SUBLANES, pl.ds(r % _SUBLANES, 1)],
                                  sem.at[(i + ahead) % _FINAL_SLOTS]).start()
    out_ref[...] = y

    @pl.when(i == n_tiles - 1)
    def _():
        for t in range(1, _FINAL_SLOTS):
            wait_rows((i + t) % _FINAL_SLOTS)


def _final(row, x2, gates, y_rows, g, tm):
    T, D = x2.shape
    grid_spec = pltpu.PrefetchScalarGridSpec(
        num_scalar_prefetch=1,
        grid=(T // tm,),
        in_specs=[pl.BlockSpec((tm, D), lambda i, *_: (i, 0)),
                  pl.BlockSpec((tm, TOPK_IN_GROUP), lambda i, *_: (i, 0)),
                  pl.BlockSpec(memory_space=pl.ANY),
                  pl.BlockSpec(g.shape, lambda i, *_: (0, 0))],
        out_specs=pl.BlockSpec((tm, D), lambda i, *_: (i, 0)),
        scratch_shapes=[pltpu.VMEM((_FINAL_SLOTS, TOPK_IN_GROUP, tm // _SUBLANES, _SUBLANES, D), F32),
                        pltpu.SemaphoreType.DMA((_FINAL_SLOTS,))],
    )
    return pl.pallas_call(
        functools.partial(_final_body, tm=tm),
        grid_spec=grid_spec,
        out_shape=jax.ShapeDtypeStruct((T, D), F32),
        compiler_params=_params(),
        name="final",
    )(row, x2, gates, y_rows, g)


def _tile_sizes(T):
    pick = lambda want: want if T % want == 0 else CHUNK
    return dict(inproj=pick(256), dsa_q=pick(256), dsa_k=pick(512), mlstm=pick(256), mixout=pick(256),
                final=pick(256), route=pick(512), moe=128)


def _layer(x, mem, norm_mix_g, w_in, kv_norm_g, k_idx_norm_g, w_uk, w_uv, conv_w, conv_b, gate_b, ml_norm_g,
           w_out, norm_x_g, mem_norm_g, w_cq, w_ckv, w_co, norm_ffn_g, w_group, b_group, w_router, b_router,
           w_gate, w_up, w_down, out_g):
    T = x.shape[0]
    ts = _tile_sizes(T)
    r2 = lambda v: v.reshape(1, -1)

    w_r = _wprep(w_in, 256)
    wuk_t = jnp.transpose(w_uk, (1, 2, 0)).astype(BF16)
    wuv_t = jnp.transpose(w_uv, (1, 0, 2)).astype(BF16)

    qabs, ckv, qi, kidx, small, mqk, mv, mo = _inproj(
        x, r2(norm_mix_g), w_r, wuk_t, r2(kv_norm_g), r2(k_idx_norm_g), ts["inproj"])

    gate_rows = jnp.transpose(small[:, _S_WI:_S_MF + ML_HEADS])
    dsa_out = _dsa(qi, gate_rows, qabs, kidx, ckv, wuv_t, ts["dsa_q"], ts["dsa_k"])

    gb_col = jnp.zeros((1, _SMALL), F32).at[0, _S_MI:_S_MI + 2 * ML_HEADS].set(gate_b)
    ml_out = _mlstm(mqk, mv, small, gate_rows, mo, conv_w, r2(conv_b), gb_col, gate_b.reshape(-1, 1),
                    r2(ml_norm_g), ts["mlstm"])

    wqk, wvo = _memfold(mem, r2(mem_norm_g), w_ckv, w_cq, w_co)
    w_rt = jnp.concatenate([w_group, w_router,
                            jnp.zeros((D_MODEL, 128 - N_GROUPS - N_EXPERTS), w_group.dtype)], axis=1)
    x2, hf, rl = _mixout(x, dsa_out, ml_out, w_out.astype(BF16), wqk, wvo, w_rt.astype(BF16),
                         r2(norm_x_g), r2(norm_ffn_g), ts["mixout"])

    bm = ts["moe"]
    row, gap_lo, gap_hi, gates, e_blk, n_blk = _route(rl, b_group, b_router, bm, ts["route"])
    y_rows = _moe(row, gap_lo, gap_hi, e_blk, hf, w_gate, w_up, w_down, n_blk, bm)
    return _final(row, x2, gates, y_rows, r2(out_g), ts["final"])


def kernel(x, mem, norm_mix_g, w_in, kv_norm_g, k_idx_norm_g, w_uk, w_uv, conv_w, conv_b, gate_b, ml_norm_g,
           w_out, norm_x_g, mem_norm_g, w_cq, w_ckv, w_co, norm_ffn_g, w_group, b_group, w_router, b_router,
           w_gate, w_up, w_down, final_norm_g):
    B, T, D = x.shape
    assert B == 1 and D == D_MODEL and norm_mix_g.shape[0] == 1 and T % CHUNK == 0
    out = _layer(x[0], mem[0], norm_mix_g[0], w_in, kv_norm_g[0], k_idx_norm_g[0], w_uk[0], w_uv[0],
                 conv_w[0], conv_b[0], gate_b[0], ml_norm_g[0], w_out[0], norm_x_g[0], mem_norm_g[0],
                 w_cq[0], w_ckv[0], w_co[0], norm_ffn_g[0], w_group[0], b_group[0], w_router[0], b_router[0],
                 w_gate[0], w_up[0], w_down[0], final_norm_g)
    return out[None]
```

```python
import functools

import jax
import jax.numpy as jnp
import numpy as np
from jax import lax
from jax.experimental import pallas as pl
from jax.experimental.pallas import tpu as pltpu

F32 = jnp.float32
BF16 = jnp.bfloat16
I32 = jnp.int32
I16 = jnp.int16
U32 = jnp.uint32

EPS = 1e-6
CHUNK = 64
D_MODEL = 2048

DSA_HEADS = 8
DSA_HEAD_DIM = 128
DSA_LATENT = 256
IDX_HEADS = 8
IDX_DIM = 64
TOPK_MAX = 256

ML_HEADS = 4
ML_QK_DIM = 128
ML_V_DIM = 256
CONV_W = 4

X_HEADS = 4
X_HEAD_DIM = D_MODEL // X_HEADS

N_GROUPS = 4
EXP_PER_GROUP = 8
N_EXPERTS = N_GROUPS * EXP_PER_GROUP
TOPK_IN_GROUP = 2
D_EXPERT = 512

_O_DQ = 0
_O_CKV = _O_DQ + DSA_HEADS * DSA_HEAD_DIM
_O_QI = _O_CKV + DSA_LATENT
_O_KI = _O_QI + IDX_HEADS * IDX_DIM
_O_WI = _O_KI + IDX_DIM
_O_MQ = _O_WI + IDX_HEADS
_O_MK = _O_MQ + ML_HEADS * ML_QK_DIM
_O_MV = _O_MK + ML_HEADS * ML_QK_DIM
_O_MI = _O_MV + ML_HEADS * ML_V_DIM
_O_MF = _O_MI + ML_HEADS
_O_MO = _O_MF + ML_HEADS
_O_END = _O_MO + ML_HEADS * ML_V_DIM

_G_DQ = (0, 1024)
_G_CKV = (1024, 1280)
_G_QI = (1280, 1792)
_G_SMALL = (1792, 1920)
_G_MQK = (1920, 2944)
_G_MV = (2944, 3968)
_G_MO = (3968, 4992)
_W_COLS = 4992
_S_WI = IDX_DIM
_S_MI = _S_WI + IDX_HEADS
_S_MF = _S_MI + ML_HEADS
_SMALL = 128

_VMEM_LIMIT = 56 * 1024 * 1024
_INT_MIN = -(2 ** 31)
_I16_MIN = -(2 ** 15)
_CHUNK_SHIFT = CHUNK.bit_length() - 1
_LOG2E = 1.4426950408889634
_SUBLANES = 8
_LANES = 128
_GATHER_SLOTS = 7
_GATHER_AHEAD = _GATHER_SLOTS - 1
_TOPK_SHIFT = TOPK_IN_GROUP.bit_length() - 1
_FINAL_SLOTS = 3
_NEG = -1e30


def _rms(v, g):
    return v * lax.rsqrt(jnp.mean(v * v, axis=-1, keepdims=True) + EPS) * g


def _dot(a, b):
    return jnp.dot(a, b, preferred_element_type=F32)


def _dot_nt(a, b):
    return lax.dot_general(a, b, (((1,), (1,)), ((), ())), preferred_element_type=F32)


def _resident(shape):
    nd = len(shape)
    return pl.BlockSpec(shape, lambda *_: (0,) * nd, pipeline_mode=pl.Buffered(1))


def _params(n_axes=1):
    return pltpu.CompilerParams(dimension_semantics=("arbitrary",) * n_axes,
                                vmem_limit_bytes=_VMEM_LIMIT)


def _wprep_body(w_hbm, out_ref, wbuf, sem, *, tk):
    i = pl.program_id(0)
    slot = i % 2

    def rows(blk, s):
        return pltpu.make_async_copy(w_hbm.at[0, pl.ds(pl.multiple_of(blk * tk, tk), tk)], wbuf.at[s], sem.at[s])

    @pl.when(i == 0)
    def _():
        rows(0, 0).start()

    @pl.when(i + 1 < pl.num_programs(0))
    def _():
        rows(i + 1, 1 - slot).start()

    rows(i, slot).wait()
    w_ref = wbuf.at[slot]
    n_small = IDX_DIM + IDX_HEADS
    out_ref[:, _G_DQ[0]:_G_SMALL[0] + n_small] = w_ref[:, _O_DQ:_O_MQ].astype(BF16)
    out_ref[:, _G_SMALL[0] + n_small:_G_SMALL[0] + n_small + 2 * ML_HEADS] = w_ref[:, _O_MI:_O_MO].astype(BF16)
    out_ref[:, _G_SMALL[0] + n_small + 2 * ML_HEADS:_G_SMALL[1]] = jnp.zeros(
        (out_ref.shape[0], _SMALL - n_small - 2 * ML_HEADS), BF16)
    out_ref[:, _G_MQK[0]:_G_MV[1]] = w_ref[:, _O_MQ:_O_MI].astype(BF16)
    out_ref[:, _G_MO[0]:_G_MO[1]] = w_ref[:, _O_MO:_O_END].astype(BF16)


def _wprep(w_in, tk):
    K = w_in.shape[1]
    return pl.pallas_call(
        functools.partial(_wprep_body, tk=tk),
        grid=(K // tk,),
        in_specs=[pl.BlockSpec(memory_space=pl.ANY)],
        out_specs=pl.BlockSpec((tk, _W_COLS), lambda i: (i, 0)),
        out_shape=jax.ShapeDtypeStruct((K, _W_COLS), BF16),
        scratch_shapes=[pltpu.VMEM((2, tk, w_in.shape[2]), F32), pltpu.SemaphoreType.DMA((2,))],
        compiler_params=_params(),
        name="wprep",
    )(w_in)


def _inproj_body(x_ref, g_ref, w_ref, wuk_ref, kvg_ref, kig_ref,
                 qabs_ref, ckv_ref, qi_ref, kidx_ref, small_ref, mqk_ref, mv_ref, mo_ref):
    h = _rms(x_ref[...], g_ref[...]).astype(BF16)

    def proj(grp):
        return _dot(h, w_ref[:, grp[0]:grp[1]])

    dq = proj(_G_DQ)
    for hd in range(DSA_HEADS):
        qh = dq[:, hd * DSA_HEAD_DIM:(hd + 1) * DSA_HEAD_DIM].astype(BF16)
        qa = _dot(qh, wuk_ref[hd]) * (DSA_HEAD_DIM ** -0.5 * _LOG2E)
        qabs_ref[:, hd * DSA_LATENT:(hd + 1) * DSA_LATENT] = qa.astype(BF16)
    ckv_ref[...] = _rms(proj(_G_CKV), kvg_ref[...]).astype(BF16)
    qi_ref[...] = (proj(_G_QI) * (IDX_DIM ** -0.5)).astype(BF16)
    small = proj(_G_SMALL)
    small_ref[...] = small
    kidx_ref[...] = _rms(small[:, :IDX_DIM], kig_ref[...]).astype(BF16)
    mqk_ref[...] = proj(_G_MQK)
    mv_ref[...] = proj(_G_MV).astype(BF16)
    mo_ref[...] = proj(_G_MO)


def _inproj(x, g, w, wuk, kvg, kig, tm):
    T = x.shape[0]
    row = lambda n: pl.BlockSpec((tm, n), lambda i: (i, 0))
    outs = [(8 * DSA_LATENT, BF16), (DSA_LATENT, BF16), (IDX_HEADS * IDX_DIM, BF16), (IDX_DIM, BF16),
            (_SMALL, F32), (2 * ML_HEADS * ML_QK_DIM, F32), (ML_HEADS * ML_V_DIM, BF16),
            (ML_HEADS * ML_V_DIM, F32)]
    return pl.pallas_call(
        _inproj_body,
        grid=(T // tm,),
        in_specs=[row(D_MODEL), _resident(g.shape), _resident(w.shape), _resident(wuk.shape),
                  _resident(kvg.shape), _resident(kig.shape)],
        out_specs=[row(n) for n, _ in outs],
        out_shape=[jax.ShapeDtypeStruct((T, n), dt) for n, dt in outs],
        compiler_params=_params(),
        name="inproj",
    )(x, g, w, wuk, kvg, kig)


def _sublane_fold(v, op, rows=_SUBLANES, ways=4):
    groups = [v[r * rows:(r + 1) * rows, :] for r in range(v.shape[0] // rows)]
    accs = groups[:ways]
    for r in range(ways, len(groups)):
        accs[r % ways] = op(accs[r % ways], groups[r])
    while len(accs) > 1:
        accs = [op(accs[k], accs[k + 1]) if k + 1 < len(accs) else accs[k] for k in range(0, len(accs), 2)]
    return accs[0]


def _bit_transpose32(words):
    a = list(words)
    j, m = 16, 0x0000FFFF
    while j:
        k = 0
        while k < 32:
            t = (a[k] ^ (a[k + j] >> j)) & jnp.uint32(m)
            a[k] = a[k] ^ t
            a[k + j] = a[k + j] ^ (t << j)
            k = (k + j + 1) & ~j
        j >>= 1
        m = (m ^ (m << j)) & 0xFFFFFFFF
    return a


def _dsa_body(qi_ref, wrow_ref, qabs_ref, kidx_ref, ckv_ref, ckvt_ref, wuv_ref, out_ref,
              key_ref, planes_ref, alive_ref, lga_ref, lgb_ref, m_ref, l_ref, acc_ref, *, tq, tk, topk, nbits_idx):
    i = pl.program_id(0)
    n_kb = ((i + 1) * tq + tk - 1) // tk
    w_rows = wrow_ref[0:IDX_HEADS, :] * (IDX_HEADS ** -0.5)
    q_chunk = (i * tq + lax.broadcasted_iota(I32, (1, tq), 1)) >> _CHUNK_SHIFT

    def key_pos(j):
        return j * tk + lax.broadcasted_iota(I32, (tk, 1), 0)

    plane_rows = tk // 32

    @pl.when(i == 0)
    def _():
        planes_ref[...] = jnp.zeros(planes_ref.shape, U32)

    def score_block(j, carry):
        kx = kidx_ref[pl.ds(pl.multiple_of(j * tk, tk), tk), :]
        s = jnp.zeros((tk, tq), F32)
        for hd in range(IDX_HEADS):
            d = _dot_nt(kx, qi_ref[:, hd * IDX_DIM:(hd + 1) * IDX_DIM])
            s = s + w_rows[hd:hd + 1, :] * jnp.maximum(d, 0.0)
        bits = lax.bitcast_convert_type(s, I32)
        key = bits ^ ((bits >> 31) & 0x7FFFFFFF)
        key = jnp.where((key_pos(j) >> _CHUNK_SHIFT) <= q_chunk, key, _INT_MIN)
        key_ref[j] = key
        u = lax.bitcast_convert_type(key, U32) ^ jnp.uint32(0x80000000)
        for h in range(plane_rows // _SUBLANES):
            base = h * 32 * _SUBLANES
            planes = _bit_transpose32([u[base + _SUBLANES * v:base + _SUBLANES * (v + 1), :] for v in range(32)])
            row0 = pl.multiple_of(j * plane_rows + h * _SUBLANES, _SUBLANES)
            for b in range(32):
                planes_ref[b, pl.ds(row0, _SUBLANES), :] = planes[31 - b]
        return carry

    lax.fori_loop(0, n_kb, score_block, 0)

    def count(pred):
        def body(j, acc):
            hit = pred(key_ref[j], key_pos(j)).astype(I32)
            return acc + _sublane_fold(hit, jnp.add)
        acc = lax.fori_loop(0, n_kb, body, jnp.zeros((8, tq), I32))
        return jnp.sum(acc, axis=0, keepdims=True)

    word_row = lax.broadcasted_iota(I32, (alive_ref.shape[0], 1), 0)
    alive_ref[...] = jnp.where(word_row < n_kb * plane_rows, jnp.full(alive_ref.shape, 0xFFFFFFFF, U32),
                               jnp.uint32(0))

    def radix_select(n_rows):
        def ones_count(words):
            return jnp.sum(_sublane_fold(lax.population_count(words).astype(I32), jnp.add), axis=0, keepdims=True)

        def select_bit(it, carry):
            need, t_u = carry
            b = 31 - it
            alive = alive_ref[0:n_rows, :]
            ones = alive & planes_ref[b, pl.ds(0, n_rows), :]
            n_ones = ones_count(ones)
            take = n_ones >= need
            alive_ref[0:n_rows, :] = jnp.where(take, ones, alive ^ ones)
            bit = lax.shift_left(jnp.uint32(1), jnp.asarray(b, U32))
            return jnp.where(take, need, need - n_ones), jnp.where(take, t_u | bit, t_u)

        need, t_u = lax.fori_loop(0, 32, select_bit, (jnp.full((1, tq), topk, I32), jnp.zeros((1, tq), U32)))
        return need, t_u, ones_count(alive_ref[0:n_rows, :])

    all_rows = alive_ref.shape[0]
    if all_rows % (2 * _SUBLANES) == 0:
        need, t_u, n_equal = lax.cond(n_kb * plane_rows <= all_rows // 2,
                                      lambda: radix_select(all_rows // 2), lambda: radix_select(all_rows))
    else:
        need, t_u, n_equal = radix_select(all_rows)
    short = t_u == 0
    t = jnp.maximum(lax.bitcast_convert_type(t_u ^ jnp.uint32(0x80000000), I32), _INT_MIN + 1)
    all_pos = jnp.int32(2 ** nbits_idx - 1)
    has_ties = (n_equal > need) & jnp.logical_not(short)
    n_tie_take = jnp.where(has_ties, need, all_pos)

    def tie_cutoff():
        def pos_bit(b, c):
            cand = c + lax.shift_left(jnp.int32(1), nbits_idx - 1 - b)
            f = count(lambda kb, pos: (kb == t) & (pos < cand))
            return jnp.where(f <= n_tie_take, cand, c)
        return lax.fori_loop(0, nbits_idx, pos_bit, jnp.zeros((1, tq), I32))

    cut = lax.cond(jnp.max(has_ties.astype(I32)) > 0, tie_cutoff, lambda: jnp.full((1, tq), all_pos, I32))

    m_ref[...] = jnp.full(m_ref.shape, _NEG, F32)
    l_ref[...] = jnp.zeros(l_ref.shape, F32)
    acc_ref[...] = jnp.zeros(acc_ref.shape, F32)
    last_blk = ckvt_ref.shape[0] - 1

    def logits(j, lg_buf):
        c_blk = ckv_ref[pl.ds(pl.multiple_of(jnp.minimum(j, last_blk) * tk, tk), tk), :]
        kb = key_ref[j]
        sel = (kb > t) | ((kb == t) & (key_pos(j) < cut))
        bias = jnp.where(sel, 0.0, _NEG).astype(F32)
        for hd in range(DSA_HEADS):
            lg_buf[hd] = _dot_nt(c_blk, qabs_ref[:, hd * DSA_LATENT:(hd + 1) * DSA_LATENT]) + bias

    def accumulate(j, lg_buf):
        c_blk_t = ckvt_ref[jnp.minimum(j, last_blk)]
        for hd in range(DSA_HEADS):
            lg = lg_buf[hd]
            m_old = m_ref[hd:hd + 1, :]
            m_new = jnp.maximum(m_old, jnp.max(_sublane_fold(lg, jnp.maximum, ways=1), axis=0, keepdims=True))
            p = jnp.exp2(lg - m_new)
            alpha = jnp.exp2(m_old - m_new)
            l_ref[hd:hd + 1, :] = alpha * l_ref[hd:hd + 1, :] + jnp.sum(_sublane_fold(p, jnp.add, ways=1), axis=0,
                                                                         keepdims=True)
            acc_ref[hd] = alpha * acc_ref[hd] + _dot(c_blk_t, p.astype(BF16))
            m_ref[hd:hd + 1, :] = m_new

    logits(0, lga_ref)

    def attn_pair(mi, carry):
        ja = 2 * mi
        accumulate(ja, lga_ref)
        logits(ja + 1, lgb_ref)
        accumulate(ja + 1, lgb_ref)
        logits(jnp.minimum(ja + 2, n_kb - 1), lga_ref)
        return carry

    lax.fori_loop(0, n_kb // 2, attn_pair, 0)

    @pl.when(n_kb % 2 == 1)
    def _():
        accumulate(n_kb - 1, lga_ref)

    for hd in range(DSA_HEADS):
        o_lat = (acc_ref[hd] / l_ref[hd:hd + 1, :]).T.astype(BF16)
        out_ref[:, hd * DSA_HEAD_DIM:(hd + 1) * DSA_HEAD_DIM] = _dot(o_lat, wuv_ref[hd]).astype(BF16)


def _dsa(qi, wrows, qabs, kidx, ckv, wuv, tq, tk):
    T = qi.shape[0]
    topk = min(TOPK_MAX, T // 4)
    n_kb = T // tk
    ckvt = jnp.transpose(ckv.reshape(n_kb, tk, DSA_LATENT), (0, 2, 1))
    row = lambda n: pl.BlockSpec((tq, n), lambda i: (i, 0))
    body = functools.partial(_dsa_body, tq=tq, tk=tk, topk=topk, nbits_idx=int(T).bit_length())
    return pl.pallas_call(
        body,
        grid=(T // tq,),
        in_specs=[row(qi.shape[1]), pl.BlockSpec((wrows.shape[0], tq), lambda i: (0, i)), row(qabs.shape[1]),
                  _resident(kidx.shape), _resident(ckv.shape), _resident(ckvt.shape), _resident(wuv.shape)],
        out_specs=row(DSA_HEADS * DSA_HEAD_DIM),
        out_shape=jax.ShapeDtypeStruct((T, DSA_HEADS * DSA_HEAD_DIM), BF16),
        scratch_shapes=[pltpu.VMEM((n_kb + n_kb % 2, tk, tq), I32),
                        pltpu.VMEM((32, T // 32, tq), U32), pltpu.VMEM((T // 32, tq), U32),
                        pltpu.VMEM((DSA_HEADS, tk, tq), F32), pltpu.VMEM((DSA_HEADS, tk, tq), F32),
                        pltpu.VMEM((DSA_HEADS, tq), F32),
                        pltpu.VMEM((DSA_HEADS, tq), F32), pltpu.VMEM((DSA_HEADS, DSA_LATENT, tq), F32)],
        compiler_params=_params(),
        name="dsa",
    )(qi, wrows, qabs, kidx, ckv, ckvt, wuv)


def _log_sigmoid(v):
    return jnp.minimum(v, 0.0) - jnp.log1p(jnp.exp(-jnp.abs(v)))


def _chunk_cumsum(v, axis):
    pos = lax.broadcasted_iota(I32, v.shape, axis) & (CHUNK - 1)
    d = 1
    while d < CHUNK:
        v = v + jnp.where(pos >= d, pltpu.roll(v, d, axis=axis), 0.0)
        d *= 2
    return v


def _mlstm_body(mqk_ref, mv_ref, small_ref, gt_ref, mo_ref, cw_ref, cb_ref, gbc_ref, gbr_ref, ng_ref,
                out_ref, xe_ref, c_ref, n_ref, m_ref, hs_ref, *, rows):
    @pl.when(pl.program_id(0) == 0)
    def _():
        xe_ref[0:8, :] = jnp.zeros((8, xe_ref.shape[1]), F32)
        c_ref[...] = jnp.zeros(c_ref.shape, F32)
        n_ref[...] = jnp.zeros(n_ref.shape, F32)
        m_ref[...] = jnp.zeros(m_ref.shape, F32)

    x = mqk_ref[...]
    xe_ref[8:8 + rows, :] = x
    y = cb_ref[...]
    for j in range(CONV_W - 1):
        y = y + xe_ref[5 + j:5 + j + rows, :] * cw_ref[j:j + 1, :]
    y = y + x * cw_ref[CONV_W - 1:CONV_W, :]
    xe_ref[0:8, :] = x[rows - 8:rows, :]
    qk = y * jax.nn.sigmoid(y)
    nqk = ML_HEADS * ML_QK_DIM
    q_all = (qk[:, :nqk] * (ML_QK_DIM ** -0.5)).astype(BF16)
    k_all = qk[:, nqk:]

    g_col = small_ref[...] + gbc_ref[...]
    g_row = gt_ref[...] + gbr_ref[...]
    b_col = _chunk_cumsum(_log_sigmoid(g_col), 0)
    b_row = _chunk_cumsum(_log_sigmoid(g_row), 1)

    tri = lax.broadcasted_iota(I32, (CHUNK, CHUNK), 1) <= lax.broadcasted_iota(I32, (CHUNK, CHUNK), 0)

    heads = range(ML_HEADS)
    for c in range(rows // CHUNK):
        lo, hi = c * CHUNK, (c + 1) * CHUNK
        bc = [b_col[lo:hi, _S_MF + hd:_S_MF + hd + 1] for hd in heads]
        lic = [g_col[lo:hi, _S_MI + hd:_S_MI + hd + 1] for hd in heads]
        br = [b_row[ML_HEADS + hd:ML_HEADS + hd + 1, lo:hi] for hd in heads]
        lir = [g_row[hd:hd + 1, lo:hi] for hd in heads]
        g_tot = [bc[hd][CHUNK - 1:CHUNK, :] for hd in heads]
        m_prev = [m_ref[hd][:, 0:1] for hd in heads]
        qh = [q_all[lo:hi, hd * ML_QK_DIM:(hd + 1) * ML_QK_DIM] for hd in heads]
        kh = [k_all[lo:hi, hd * ML_QK_DIM:(hd + 1) * ML_QK_DIM] for hd in heads]
        vh = [mv_ref[lo:hi, hd * ML_V_DIM:(hd + 1) * ML_V_DIM] for hd in heads]
        c_prev = [c_ref[hd] for hd in heads]
        n_prev = [n_ref[hd] for hd in heads]

        dmat = [jnp.where(tri, bc[hd] - br[hd] + lir[hd], -jnp.inf) for hd in heads]
        inter = [bc[hd] + m_prev[hd] for hd in heads]
        m_t = [jnp.maximum(inter[hd], jnp.max(dmat[hd], axis=-1, keepdims=True)) for hd in heads]
        w_intra = [jnp.exp(dmat[hd] - m_t[hd]) for hd in heads]
        a_inter = [jnp.exp(inter[hd] - m_t[hd]) for hd in heads]
        m_new = [jnp.maximum(g_tot[hd] + m_prev[hd],
                             jnp.max(g_tot[hd] - br[hd] + lir[hd], axis=-1, keepdims=True)) for hd in heads]
        a_state = [jnp.exp(g_tot[hd] + m_prev[hd] - m_new[hd]) for hd in heads]
        wk = [jnp.exp(g_tot[hd] - bc[hd] + lic[hd] - m_new[hd]) * kh[hd] for hd in heads]

        s_qk = [_dot_nt(qh[hd], kh[hd].astype(BF16)) * w_intra[hd] for hd in heads]
        read = [_dot(qh[hd], c_prev[hd].astype(BF16)) for hd in heads]
        d_c = [_dot(wk[hd].T.astype(BF16), vh[hd]) for hd in heads]
        num = [a_inter[hd] * read[hd] + _dot(s_qk[hd].astype(BF16), vh[hd]) for hd in heads]
        den = [a_inter[hd] * jnp.sum(qh[hd].astype(F32) * n_prev[hd], axis=-1, keepdims=True)
               + jnp.sum(s_qk[hd], axis=-1, keepdims=True) for hd in heads]
        for hd in heads:
            hs_ref[lo:hi, hd * ML_V_DIM:(hd + 1) * ML_V_DIM] = (
                num[hd] / jnp.maximum(jnp.abs(den[hd]), jnp.exp(-m_t[hd])))
        for hd in heads:
            c_ref[hd] = a_state[hd] * c_prev[hd] + d_c[hd]
            n_ref[hd] = a_state[hd] * n_prev[hd] + jnp.sum(wk[hd], axis=0, keepdims=True)
            m_ref[hd] = jnp.broadcast_to(m_new[hd], m_ref.shape[1:])

    for hd in range(ML_HEADS):
        sl = slice(hd * ML_V_DIM, (hd + 1) * ML_V_DIM)
        out_ref[:, sl] = (_rms(hs_ref[:, sl], ng_ref[:, sl]) * jax.nn.sigmoid(mo_ref[:, sl])).astype(BF16)


def _mlstm(mqk, mv, small, gt, mo, cw, cb, gbc, gbr, ng, rows):
    T = mqk.shape[0]
    row = lambda n: pl.BlockSpec((rows, n), lambda i: (i, 0))
    nv = ML_HEADS * ML_V_DIM
    return pl.pallas_call(
        functools.partial(_mlstm_body, rows=rows),
        grid=(T // rows,),
        in_specs=[row(mqk.shape[1]), row(nv), row(_SMALL), pl.BlockSpec((8, rows), lambda i: (1, i)), row(nv),
                  _resident(cw.shape), _resident(cb.shape), _resident(gbc.shape), _resident(gbr.shape),
                  _resident(ng.shape)],
        out_specs=row(nv),
        out_shape=jax.ShapeDtypeStruct((T, nv), BF16),
        scratch_shapes=[pltpu.VMEM((rows + 8, mqk.shape[1]), F32),
                        pltpu.VMEM((ML_HEADS, ML_QK_DIM, ML_V_DIM), F32),
                        pltpu.VMEM((ML_HEADS, 1, ML_QK_DIM), F32),
                        pltpu.VMEM((ML_HEADS, 1, 128), F32),
                        pltpu.VMEM((rows, nv), F32)],
        compiler_params=_params(),
        name="mlstm",
    )(mqk, mv, small, gt, mo, cw, cb, gbc, gbr, ng)


def _memfold_body(mem_ref, g_ref, wk_ref, wv_ref, wq_ref, wc_ref, wqk_ref, wvo_ref):
    mn = _rms(mem_ref[...], g_ref[...]).astype(BF16)
    k = _dot(mn, wk_ref[...].astype(BF16)).astype(BF16)
    v = _dot(mn, wv_ref[...].astype(BF16)).astype(BF16)
    wqk_ref[...] = (_dot_nt(wq_ref[...].astype(BF16), k) * (X_HEAD_DIM ** -0.5)).astype(BF16)
    wvo_ref[...] = _dot(v, wc_ref[...].astype(BF16)).astype(BF16)


def _memfold(mem, g, w_ckv, w_cq, w_co):
    M, D = mem.shape
    dh = X_HEAD_DIM
    return pl.pallas_call(
        _memfold_body,
        grid=(X_HEADS,),
        in_specs=[_resident(mem.shape), _resident(g.shape),
                  pl.BlockSpec((D, dh), lambda h: (0, h)), pl.BlockSpec((D, dh), lambda h: (0, X_HEADS + h)),
                  pl.BlockSpec((D, dh), lambda h: (0, h)), pl.BlockSpec((dh, D), lambda h: (h, 0))],
        out_specs=[pl.BlockSpec((D, M), lambda h: (0, h)), pl.BlockSpec((M, D), lambda h: (h, 0))],
        out_shape=[jax.ShapeDtypeStruct((D, X_HEADS * M), BF16), jax.ShapeDtypeStruct((X_HEADS * M, D), BF16)],
        compiler_params=_params(),
        name="memfold",
    )(mem, g, w_ckv, w_ckv, w_cq, w_co)


def _mixout_body(x_ref, dsa_ref, ml_ref, wo_ref, wqk_ref, wvo_ref, wr_ref, gx_ref, gf_ref,
                 x2_ref, hf_ref, rl_ref, p_ref):
    nd = dsa_ref.shape[1]
    x1 = x_ref[...] + _dot(dsa_ref[...], wo_ref[0:nd, :]) + _dot(ml_ref[...], wo_ref[nd:, :])
    lg_all = _dot(_rms(x1, gx_ref[...]).astype(BF16), wqk_ref[...])
    n_mem = wqk_ref.shape[1] // X_HEADS
    for hd in range(X_HEADS):
        sl = slice(hd * n_mem, (hd + 1) * n_mem)
        lg = lg_all[:, sl]
        e = jnp.exp(lg - jnp.max(lg, axis=-1, keepdims=True))
        p_ref[:, sl] = (e / jnp.sum(e, axis=-1, keepdims=True)).astype(BF16)
    x2 = x1 + _dot(p_ref[...], wvo_ref[...])
    x2_ref[...] = x2
    hf = _rms(x2, gf_ref[...]).astype(BF16)
    rl_ref[...] = _dot(hf, wr_ref[...])
    bits = lax.bitcast_convert_type(hf.astype(F32), U32)
    for c in range(_SUBLANES):
        lo = bits[:, (2 * c) * _LANES:(2 * c + 1) * _LANES]
        hi = bits[:, (2 * c + 1) * _LANES:(2 * c + 2) * _LANES]
        hf_ref[pl.ds(c, x2.shape[0], stride=_SUBLANES), :] = (hi & jnp.uint32(0xFFFF0000)) | (lo >> 16)


def _mixout(x, dsa, ml, wo, wqk, wvo, wr, gx, gf, tm):
    T = x.shape[0]
    row = lambda n: pl.BlockSpec((tm, n), lambda i: (i, 0))
    return pl.pallas_call(
        _mixout_body,
        grid=(T // tm,),
        in_specs=[row(D_MODEL), row(dsa.shape[1]), row(ml.shape[1]), _resident(wo.shape),
                  _resident(wqk.shape), _resident(wvo.shape), _resident(wr.shape), _resident(gx.shape),
                  _resident(gf.shape)],
        out_specs=[row(D_MODEL), pl.BlockSpec((tm * _SUBLANES, _LANES), lambda i: (i, 0)), row(wr.shape[1])],
        out_shape=[jax.ShapeDtypeStruct((T, D_MODEL), F32),
                   jax.ShapeDtypeStruct((T * _SUBLANES, _LANES), U32),
                   jax.ShapeDtypeStruct((T, wr.shape[1]), F32)],
        scratch_shapes=[pltpu.VMEM((tm, wqk.shape[1]), BF16)],
        compiler_params=_params(),
        name="mixout",
    )(x, dsa, ml, wo, wqk, wvo, wr, gx, gf)


def _moe_body(row_ref, gap_lo_ref, gap_hi_ref, eb_ref, hf_hbm, wg_ref, wu_ref, wd_ref, y_hbm,
              tok_ref, xbuf, ybuf, wgb, wub, wdb, gsem, ysem, *, bm, n_blk):
    e = pl.program_id(0)

    @pl.when(e == 0)
    def _():
        def clear_gap(g, carry):
            def clear(r, c):
                tok_ref[r] = 0
                return c
            return lax.fori_loop(gap_lo_ref[g], gap_hi_ref[g], clear, carry)

        def place(a, carry):
            tok_ref[row_ref[a]] = lax.shift_right_logical(a, _TOPK_SHIFT)
            return carry

        lax.fori_loop(0, gap_lo_ref.shape[0], clear_gap, 0)
        lax.fori_loop(0, row_ref.shape[0], place, 0, unroll=8)

    n_valid = eb_ref[N_EXPERTS]
    b_lo = eb_ref[e]
    b_hi = eb_ref[e + 1]

    def tok_words(tok):
        return hf_hbm.at[pl.ds(pl.multiple_of(tok * _SUBLANES, _SUBLANES), _SUBLANES)]

    def gather(blk, s):
        def issue(g, carry):
            for u in range(_SUBLANES):
                r = g * _SUBLANES + u
                pltpu.make_async_copy(tok_words(tok_ref[blk * bm + r]),
                                      xbuf.at[s, pl.ds(pl.multiple_of(r * _SUBLANES, _SUBLANES), _SUBLANES)],
                                      gsem.at[s]).start(priority=1)
            return carry
        lax.fori_loop(0, bm // _SUBLANES, issue, 0)

    def wait_gather(s):
        pltpu.make_async_copy(xbuf.at[s], xbuf.at[s], gsem.at[s]).wait()

    def y_copy(blk, s):
        return pltpu.make_async_copy(ybuf.at[s], y_hbm.at[pl.ds(pl.multiple_of(blk * bm, bm), bm)], ysem.at[s])

    @pl.when(e == 0)
    def _():
        for k in range(_GATHER_AHEAD):
            gather(k, k)

    @pl.when(b_hi > b_lo)
    def _():
        wgb[...] = wg_ref[0].astype(BF16)
        wub[...] = wu_ref[0].astype(BF16)
        wdb[...] = wd_ref[0].astype(BF16)

        def block(b, carry):
            s = b % _GATHER_SLOTS
            wait_gather(s)
            chunks = []
            for c in range(_SUBLANES):
                w = xbuf[s, pl.ds(c, bm, stride=_SUBLANES), :]
                chunks.append(lax.bitcast_convert_type(w << 16, F32).astype(BF16))
                chunks.append(lax.bitcast_convert_type(w & jnp.uint32(0xFFFF0000), F32).astype(BF16))
            xb = jnp.concatenate(chunks, axis=-1)
            gate = _dot(xb, wgb[...])
            a = gate * jax.nn.sigmoid(gate) * _dot(xb, wub[...])
            y = _dot(a.astype(BF16), wdb[...])

            nxt = jnp.minimum(b + _GATHER_AHEAD, n_blk - 1)
            for r in range(bm):
                pltpu.make_async_copy(tok_words(tok_ref[nxt * bm + r]),
                                      xbuf.at[(b + _GATHER_AHEAD) % _GATHER_SLOTS, pl.ds(r * _SUBLANES, _SUBLANES)],
                                      gsem.at[(b + _GATHER_AHEAD) % _GATHER_SLOTS]).start(priority=1)

            @pl.when(b >= 2)
            def _():
                y_copy(b - 2, b % 2).wait()

            ybuf[b % 2] = y
            y_copy(b, b % 2).start()
            return carry

        lax.fori_loop(b_lo, b_hi, block, 0)

    @pl.when(e == pl.num_programs(0) - 1)
    def _():
        for k in range(_GATHER_AHEAD):
            wait_gather((n_valid + k) % _GATHER_SLOTS)

        @pl.when(n_valid >= 2)
        def _():
            y_copy(n_valid - 2, n_valid % 2).wait()
        y_copy(n_valid - 1, (n_valid - 1) % 2).wait()
        ybuf[0] = jnp.zeros(ybuf.shape[1:], F32)

        def zero_block(b, carry):
            cp = y_copy(b, 0)
            cp.start()
            cp.wait()
            return carry

        lax.fori_loop(n_valid, n_blk, zero_block, 0)


def _moe(row, gap_lo, gap_hi, e_blk, hf, wg, wu, wd, n_blk, bm):
    D = wg.shape[1]
    wspec = lambda shape: pl.BlockSpec((1,) + shape, lambda e, *_: (e, 0, 0))
    grid_spec = pltpu.PrefetchScalarGridSpec(
        num_scalar_prefetch=4,
        grid=(N_EXPERTS,),
        in_specs=[pl.BlockSpec(memory_space=pl.ANY),
                  wspec((D, D_EXPERT)), wspec((D, D_EXPERT)), wspec((D_EXPERT, D))],
        out_specs=pl.BlockSpec(memory_space=pl.ANY),
        scratch_shapes=[pltpu.SMEM((n_blk * bm,), I32),
                        pltpu.VMEM((_GATHER_SLOTS, bm * _SUBLANES, _LANES), U32),
                        pltpu.VMEM((2, bm, D), F32),
                        pltpu.VMEM((D, D_EXPERT), BF16), pltpu.VMEM((D, D_EXPERT), BF16),
                        pltpu.VMEM((D_EXPERT, D), BF16),
                        pltpu.SemaphoreType.DMA((_GATHER_SLOTS,)), pltpu.SemaphoreType.DMA((2,))],
    )
    return pl.pallas_call(
        functools.partial(_moe_body, bm=bm, n_blk=n_blk),
        grid_spec=grid_spec,
        out_shape=jax.ShapeDtypeStruct((n_blk * bm, D), F32),
        compiler_params=_params(),
        name="moe",
    )(row, gap_lo, gap_hi, e_blk, hf, wg, wu, wd)


def _route_body(rl_ref, bias_ref, out_ref, cnt_ref, seen_ref, *, tm):
    @pl.when(pl.program_id(0) == 0)
    def _():
        seen_ref[...] = jnp.zeros(seen_ref.shape, F32)

    lg = rl_ref[...] + bias_ref[...]
    lane = lax.broadcasted_iota(I32, lg.shape, 1).astype(F32)
    first = lambda hit: jnp.min(jnp.where(hit, lane, float(_LANES)), axis=-1, keepdims=True)
    is_group = lane < N_GROUPS
    gl = jnp.where(is_group, lg, -jnp.inf)
    g_max = jnp.max(gl, axis=-1, keepdims=True)
    g_sel = first(gl == g_max)
    p_g = 1.0 / jnp.sum(jnp.where(is_group, jnp.exp(gl - g_max), 0.0), axis=-1, keepdims=True)

    e_id = lane - N_GROUPS
    in_group = (e_id >= 0) & (e_id < N_EXPERTS) & (jnp.floor(e_id / EXP_PER_GROUP) == g_sel)
    el = jnp.where(in_group, lg, -jnp.inf)
    ee = jnp.where(in_group, jnp.exp(el - jnp.max(el, axis=-1, keepdims=True)), 0.0)
    cand = jnp.where(in_group, ee / jnp.sum(ee, axis=-1, keepdims=True), -1.0)
    p1 = jnp.max(cand, axis=-1, keepdims=True)
    l1 = first(cand == p1)
    cand = jnp.where(lane == l1, -2.0, cand)
    p2 = jnp.max(cand, axis=-1, keepdims=True)
    l2 = first(cand == p2)
    g1 = p_g * p1 / (p1 + p2)
    g2 = p_g * p2 / (p1 + p2)

    oh1 = (lane == l1).astype(BF16)
    oh2 = (lane == l2).astype(BF16)
    both = oh1 + oh2
    earlier = (lax.broadcasted_iota(I32, (tm, tm), 1) < lax.broadcasted_iota(I32, (tm, tm), 0)).astype(BF16)
    before = _dot(earlier, both) + seen_ref[...]
    r1 = jnp.sum(before * oh1.astype(F32), axis=-1, keepdims=True)
    r2 = jnp.sum(before * oh2.astype(F32), axis=-1, keepdims=True)
    seen_ref[...] = seen_ref[...] + jnp.sum(both.astype(F32), axis=0, keepdims=True)
    cnt_ref[...] = seen_ref[...]

    cols = (l1 - N_GROUPS, l2 - N_GROUPS, r1, r2, g1, g2)
    out = jnp.zeros(lg.shape, F32)
    for c, v in enumerate(cols):
        out = jnp.where(lane == c, v, out)
    out_ref[...] = out


def _route_tokens(rl, bias, tm):
    T = rl.shape[0]
    return pl.pallas_call(
        functools.partial(_route_body, tm=tm),
        grid=(T // tm,),
        in_specs=[pl.BlockSpec((tm, _LANES), lambda i: (i, 0)), _resident(bias.shape)],
        out_specs=[pl.BlockSpec((tm, _LANES), lambda i: (i, 0)), pl.BlockSpec((1, _LANES), lambda i: (0, 0))],
        out_shape=[jax.ShapeDtypeStruct((T, _LANES), F32), jax.ShapeDtypeStruct((1, _LANES), F32)],
        scratch_shapes=[pltpu.VMEM((1, _LANES), F32)],
        compiler_params=_params(),
        name="route",
    )(rl, bias)


def _route(rl, b_group, b_router, bm, tm):
    N = rl.shape[0]
    bias = jnp.concatenate([b_group, b_router, jnp.zeros((_LANES - N_GROUPS - N_EXPERTS,), F32)]).reshape(1, _LANES)
    per_tok, seen = _route_tokens(rl, bias, tm)
    expert_id = per_tok[:, 0:TOPK_IN_GROUP].astype(I32)
    rank = per_tok[:, TOPK_IN_GROUP:2 * TOPK_IN_GROUP].astype(I32)
    gates = per_tok[:, 2 * TOPK_IN_GROUP:3 * TOPK_IN_GROUP]
    counts = seen[0, N_GROUPS:N_GROUPS + N_EXPERTS].astype(I32)

    A = N * TOPK_IN_GROUP
    padded = (counts + bm - 1) // bm * bm
    pad_ends = jnp.cumsum(padded)
    pad_starts = pad_ends - padded
    onehot = expert_id[..., None] == jnp.arange(N_EXPERTS, dtype=I32)
    row = (jnp.sum(jnp.where(onehot, pad_starts, 0), axis=-1) + rank).reshape(A)
    n_blk = -(-A // bm) + N_EXPERTS
    gap_lo = jnp.concatenate([pad_starts + counts, pad_ends[-1:]]).astype(I32)
    gap_hi = jnp.concatenate([pad_ends, jnp.full((1,), n_blk * bm, I32)]).astype(I32)
    e_blk = jnp.concatenate([pad_starts, pad_ends[-1:]]) // bm
    return row, gap_lo, gap_hi, gates, e_blk.astype(I32), n_blk


def _final_body(row_ref, x_ref, gate_ref, y_hbm, g_ref, out_ref, ybuf, sem, *, tm):
    i = pl.program_id(0)
    n_tiles = pl.num_programs(0)
    slot = i % _FINAL_SLOTS

    def gather(tile, s):
        def issue(g, carry):
            for u in range(_SUBLANES):
                for k in range(TOPK_IN_GROUP):
                    src = row_ref[(tile * tm + g * _SUBLANES + u) * TOPK_IN_GROUP + k]
                    pltpu.make_async_copy(y_hbm.at[pl.ds(src, 1)], ybuf.at[s, k, g, pl.ds(u, 1)],
                                          sem.at[s]).start()
            return carry
        lax.fori_loop(0, tm // _SUBLANES, issue, 0)

    def wait_rows(s):
        pltpu.make_async_copy(ybuf.at[s], ybuf.at[s], sem.at[s]).wait()

    @pl.when(i == 0)
    def _():
        for t in range(_FINAL_SLOTS - 1):
            gather(jnp.minimum(t, n_tiles - 1), t)

    wait_rows(slot)
    acc = x_ref[...]
    for k in range(TOPK_IN_GROUP):
        acc = acc + gate_ref[:, k:k + 1] * ybuf[slot, k].reshape(tm, ybuf.shape[-1])
    y = _rms(acc, g_ref[...])

    ahead = _FINAL_SLOTS - 1
    nxt = jnp.minimum(i + ahead, n_tiles - 1)
    for r in range(tm):
        for k in range(TOPK_IN_GROUP):
            src = row_ref[(nxt * tm + r) * TOPK_IN_GROUP + k]
            pltpu.make_async_copy(y_hbm.at[pl.ds(src, 1)],
                                  ybuf.at[(i + ahead) % _FINAL_SLOTS, k, r // _SUBLANES, pl.ds(r % _SUBLANES, 1)],
                                  sem.at[(i + ahead) % _FINAL_SLOTS]).start()
    out_ref[...] = y

    @pl.when(i == n_tiles - 1)
    def _():
        for t in range(1, _FINAL_SLOTS):
            wait_rows((i + t) % _FINAL_SLOTS)


def _final(row, x2, gates, y_rows, g, tm):
    T, D = x2.shape
    grid_spec = pltpu.PrefetchScalarGridSpec(
        num_scalar_prefetch=1,
        grid=(T // tm,),
        in_specs=[pl.BlockSpec((tm, D), lambda i, *_: (i, 0)),
                  pl.BlockSpec((tm, TOPK_IN_GROUP), lambda i, *_: (i, 0)),
                  pl.BlockSpec(memory_space=pl.ANY),
                  pl.BlockSpec(g.shape, lambda i, *_: (0, 0))],
        out_specs=pl.BlockSpec((tm, D), lambda i, *_: (i, 0)),
        scratch_shapes=[pltpu.VMEM((_FINAL_SLOTS, TOPK_IN_GROUP, tm // _SUBLANES, _SUBLANES, D), F32),
                        pltpu.SemaphoreType.DMA((_FINAL_SLOTS,))],
    )
    return pl.pallas_call(
        functools.partial(_final_body, tm=tm),
        grid_spec=grid_spec,
        out_shape=jax.ShapeDtypeStruct((T, D), F32),
        compiler_params=_params(),
        name="final",
    )(row, x2, gates, y_rows, g)


def _tile_sizes(T):
    pick = lambda want: want if T % want == 0 else CHUNK
    return dict(inproj=pick(256), dsa_q=pick(256), dsa_k=pick(512), mlstm=pick(256), mixout=pick(256),
                final=pick(256), route=pick(512), moe=128)


def _layer(x, mem, norm_mix_g, w_in, kv_norm_g, k_idx_norm_g, w_uk, w_uv, conv_w, conv_b, gate_b, ml_norm_g,
           w_out, norm_x_g, mem_norm_g, w_cq, w_ckv, w_co, norm_ffn_g, w_group, b_group, w_router, b_router,
           w_gate, w_up, w_down, out_g):
    T = x.shape[0]
    ts = _tile_sizes(T)
    r2 = lambda v: v.reshape(1, -1)

    w_r = _wprep(w_in, 256)
    wuk_t = jnp.transpose(w_uk, (1, 2, 0)).astype(BF16)
    wuv_t = jnp.transpose(w_uv, (1, 0, 2)).astype(BF16)

    qabs, ckv, qi, kidx, small, mqk, mv, mo = _inproj(
        x, r2(norm_mix_g), w_r, wuk_t, r2(kv_norm_g), r2(k_idx_norm_g), ts["inproj"])

    gate_rows = jnp.transpose(small[:, _S_WI:_S_MF + ML_HEADS])
    dsa_out = _dsa(qi, gate_rows, qabs, kidx, ckv, wuv_t, ts["dsa_q"], ts["dsa_k"])

    gb_col = jnp.zeros((1, _SMALL), F32).at[0, _S_MI:_S_MI + 2 * ML_HEADS].set(gate_b)
    ml_out = _mlstm(mqk, mv, small, gate_rows, mo, conv_w, r2(conv_b), gb_col, gate_b.reshape(-1, 1),
                    r2(ml_norm_g), ts["mlstm"])

    wqk, wvo = _memfold(mem, r2(mem_norm_g), w_ckv, w_cq, w_co)
    w_rt = jnp.concatenate([w_group, w_router,
                            jnp.zeros((D_MODEL, 128 - N_GROUPS - N_EXPERTS), w_group.dtype)], axis=1)
    x2, hf, rl = _mixout(x, dsa_out, ml_out, w_out.astype(BF16), wqk, wvo, w_rt.astype(BF16),
                         r2(norm_x_g), r2(norm_ffn_g), ts["mixout"])

    bm = ts["moe"]
    row, gap_lo, gap_hi, gates, e_blk, n_blk = _route(rl, b_group, b_router, bm, ts["route"])
    y_rows = _moe(row, gap_lo, gap_hi, e_blk, hf, w_gate, w_up, w_down, n_blk, bm)
    return _final(row, x2, gates, y_rows, r2(out_g), ts["final"])


def kernel(x, mem, norm_mix_g, w_in, kv_norm_g, k_idx_norm_g, w_uk, w_uv, conv_w, conv_b, gate_b, ml_norm_g,
           w_out, norm_x_g, mem_norm_g, w_cq, w_ckv, w_co, norm_ffn_g, w_group, b_group, w_router, b_router,
           w_gate, w_up, w_down, final_norm_g):
    B, T, D = x.shape
    assert B == 1 and D == D_MODEL and norm_mix_g.shape[0] == 1 and T % CHUNK == 0
    out = _layer(x[0], mem[0], norm_mix_g[0], w_in, kv_norm_g[0], k_idx_norm_g[0], w_uk[0], w_uv[0],
                 conv_w[0], conv_b[0], gate_b[0], ml_norm_g[0], w_out[0], norm_x_g[0], mem_norm_g[0],
                 w_cq[0], w_ckv[0], w_co[0], norm_ffn_g[0], w_group[0], b_group[0], w_router[0], b_router[0],
                 w_gate[0], w_up[0], w_down[0], final_norm_g)
    return out[None]
```
